```python
import math
import jax
import jax.numpy as jnp
from jax import lax
import numpy as np


D_MODEL = 1024
BATCH = 2
SEQ = 8192
DEPTH = 2

MIX_WIDTH = 1024
MIX_A = MIX_WIDTH // 2
MIX_B = MIX_WIDTH - MIX_A
MIX_C = MIX_WIDTH // 2
MIX_D = MIX_WIDTH - MIX_C

MLSTM_HEADS = 4
MLSTM_DH = MIX_A // MLSTM_HEADS
MLSTM_CHUNK = 64

LRU_HEADS = 4
LRU_DH = MIX_B // LRU_HEADS
LRU_CONV = 4
LRU_C = 8.0

S5_GROUP = 16
S5_GROUPS = MIX_C // S5_GROUP
S5_STATE = 64

GLA_HEADS = 4
GLA_DK = 64
GLA_DV = MIX_D // GLA_HEADS
GLA_GATE_RANK = 16
GLA_GATE_TEMP = 16.0
GLA_CHUNK = 64

N_EXPERTS = 64
TOP_K = 8
N_GROUPS = 8
TOPK_GROUPS = 4
D_EXPERT = 256
D_SHARED = 256
ROUTED_SCALE = 2.5
MOE_BLOCK = 128

N_EVEN = (DEPTH + 1) // 2
N_ODD = DEPTH // 2
ALPHA = (2.0 * DEPTH) ** 0.25
BETA = (8.0 * DEPTH) ** -0.25
EPS = 1e-5
F32 = jnp.float32

EVEN_SPLITS = [MIX_A, 2 * MIX_A, 3 * MIX_A, 4 * MIX_A, 4 * MIX_A + 2 * MLSTM_HEADS, 4 * MIX_A + 2 * MLSTM_HEADS + MIX_B]
EVEN_IN = 4 * MIX_A + 2 * MLSTM_HEADS + 2 * MIX_B
ODD_SPLITS = [MIX_C, MIX_C + GLA_HEADS * GLA_DK, MIX_C + 2 * GLA_HEADS * GLA_DK, MIX_C + 2 * GLA_HEADS * GLA_DK + MIX_D, MIX_C + 2 * GLA_HEADS * GLA_DK + 2 * MIX_D]
ODD_IN = MIX_C + 2 * GLA_HEADS * GLA_DK + 2 * MIX_D + GLA_GATE_RANK

kernel_name = 'hybrid_mlstm_rglru_s5_gla_moe_deepnorm'


def layer_norm(x, w, b):
    xf = x.astype(F32)
    mu = jnp.mean(xf, axis=-1, keepdims=True)
    xc = xf - mu
    y = xc * lax.rsqrt(jnp.mean(xc * xc, axis=-1, keepdims=True) + EPS)
    return (y * w.astype(F32) + b.astype(F32)).astype(x.dtype)


def head_norm(h, w, center):
    if center:
        h = h - jnp.mean(h, axis=-1, keepdims=True)
    y = h * lax.rsqrt(jnp.mean(h * h, axis=-1, keepdims=True) + EPS)
    return y.reshape(h.shape[0], h.shape[1], -1) * w.astype(F32)


def to_chunks(t, L):
    b, s, h = t.shape[:3]
    t = t.reshape((b, s // L, L, h) + t.shape[3:])
    return jnp.moveaxis(t, 3, 1)


def from_chunks(t):
    t = jnp.moveaxis(t, 1, 3)
    b, n, l, h = t.shape[:4]
    return t.reshape((b, n * l, h) + t.shape[4:])


def linear_scan(a, u, axis):
    def combine(lhs, rhs):
        a1, u1 = lhs
        a2, u2 = rhs
        return a1 * a2, a2 * u1 + u2
    return lax.associative_scan(combine, (a, u), axis=axis)[1]


def mlstm(q, k, v, i_pre, f_pre):
    L = MLSTM_CHUNK
    bsz, _, nh, dh = q.shape
    qc = to_chunks(q, L)
    kc = to_chunks(k, L) * (dh ** -0.5)
    vc = to_chunks(v, L)
    ig = to_chunks(i_pre, L)
    b = jnp.cumsum(to_chunks(jax.nn.log_sigmoid(f_pre), L), axis=-1)
    b_last = b[..., -1]
    w_loc = b_last[..., None] - b + ig
    m_loc = jnp.max(w_loc, axis=-1)
    p_loc = jnp.exp(w_loc - m_loc[..., None])
    c_loc = jnp.einsum('bhnl,bhnld,bhnle->bhnde', p_loc, kc, vc)
    n_loc = jnp.einsum('bhnl,bhnld->bhnd', p_loc, kc)

    def step(carry, xs):
        c, n, m = carry
        bl, ml, cl, nl = xs
        m_new = jnp.maximum(bl + m, ml)
        keep = jnp.exp(bl + m - m_new)
        add = jnp.exp(ml - m_new)
        c_new = keep[..., None, None] * c + add[..., None, None] * cl
        n_new = keep[..., None] * n + add[..., None] * nl
        return (c_new, n_new, m_new), (c, n, m)

    init = (jnp.zeros((bsz, nh, dh, dh), F32), jnp.zeros((bsz, nh, dh), F32), jnp.zeros((bsz, nh), F32))
    xs = (jnp.moveaxis(b_last, 2, 0), jnp.moveaxis(m_loc, 2, 0), jnp.moveaxis(c_loc, 2, 0), jnp.moveaxis(n_loc, 2, 0))
    _, (c_prev, n_prev, m_prev) = lax.scan(step, init, xs)
    c_prev = jnp.moveaxis(c_prev, 0, 2)
    n_prev = jnp.moveaxis(n_prev, 0, 2)
    m_prev = jnp.moveaxis(m_prev, 0, 2)
    causal = jnp.tril(jnp.ones((L, L), dtype=bool))
    d_mat = jnp.where(causal, b[..., :, None] - b[..., None, :] + ig[..., None, :], -jnp.inf)
    m_inter = b + m_prev[..., None]
    m_i = jnp.maximum(m_inter, jnp.max(d_mat, axis=-1))
    s = jnp.einsum('bhnid,bhnjd->bhnij', qc, kc) * jnp.exp(d_mat - m_i[..., None])
    w_inter = jnp.exp(m_inter - m_i)
    num = jnp.einsum('bhnij,bhnje->bhnie', s, vc) + w_inter[..., None] * jnp.einsum('bhnid,bhnde->bhnie', qc, c_prev)
    den = jnp.sum(s, axis=-1) + w_inter * jnp.einsum('bhnid,bhnd->bhni', qc, n_prev)
    h = num / jnp.maximum(jnp.abs(den), jnp.exp(-m_i))[..., None]
    return from_chunks(h)


def rglru(xb, gb, conv_w, conv_b, wa, ba, wx, bx, lam):
    bsz, s, c = xb.shape
    xc = lax.conv_general_dilated(xb, conv_w[:, None, :], (1,), [(LRU_CONV - 1, 0)],
                                  dimension_numbers=('NWC', 'WIO', 'NWC'), feature_group_count=c) + conv_b
    xh = xc.reshape(bsz, s, LRU_HEADS, LRU_DH)
    r = jax.nn.sigmoid((jnp.einsum('bshi,hij->bshj', xh, wa).reshape(bsz, s, c) + ba).astype(F32))
    i = jax.nn.sigmoid((jnp.einsum('bshi,hij->bshj', xh, wx).reshape(bsz, s, c) + bx).astype(F32))
    log_a = -LRU_C * r * jax.nn.softplus(-lam.astype(F32))
    a = jnp.exp(log_a)
    u = jnp.sqrt(-jnp.expm1(2.0 * log_a)) * i * xc.astype(F32)
    h = linear_scan(a, u, axis=1)
    return h * jax.nn.gelu(gb.astype(F32))


def even_mixer(x, w_in, gate_b, norm_w, conv_w, conv_b, wa, ba, wx, bx, lam):
    bsz, s, _ = x.shape
    q, k, v, o, g, xb, gb = jnp.split(x @ w_in, EVEN_SPLITS, axis=-1)
    heads = lambda t: t.astype(F32).reshape(bsz, s, MLSTM_HEADS, MLSTM_DH)
    g = g.astype(F32) + gate_b.astype(F32)
    h = mlstm(heads(q), heads(k), heads(v), g[..., :MLSTM_HEADS], g[..., MLSTM_HEADS:])
    y_a = head_norm(h, norm_w, True) * jax.nn.sigmoid(o.astype(F32))
    y_b = rglru(xb, gb, conv_w, conv_b, wa, ba, wx, bx, lam)
    return jnp.concatenate([y_a.astype(x.dtype), y_b.astype(x.dtype)], axis=-1)


def s5(u, lam_re, lam_im, b_re, b_im, c_re, c_im, d_skip, log_dt, glu_w, glu_b):
    bsz, s, _ = u.shape
    uf = u.astype(F32).reshape(bsz, s, S5_GROUPS, S5_GROUP)
    lr = lam_re.astype(F32)
    li = lam_im.astype(F32)
    dt = jnp.exp(log_dt.astype(F32))[:, None]
    mag = jnp.exp(lr * dt)
    abar_re = mag * jnp.cos(li * dt)
    abar_im = mag * jnp.sin(li * dt)
    den = lr * lr + li * li
    nr = abar_re - 1.0
    coef_re = (nr * lr + abar_im * li) / den
    coef_im = (abar_im * lr - nr * li) / den
    br = b_re.astype(F32)
    bi = b_im.astype(F32)
    bbar_re = coef_re[..., None] * br - coef_im[..., None] * bi
    bbar_im = coef_re[..., None] * bi + coef_im[..., None] * br
    bu_re = jnp.einsum('bsgh,gph->bsgp', uf, bbar_re)
    bu_im = jnp.einsum('bsgh,gph->bsgp', uf, bbar_im)
    a_re = jnp.broadcast_to(abar_re, (1, s, S5_GROUPS, S5_STATE))
    a_im = jnp.broadcast_to(abar_im, (1, s, S5_GROUPS, S5_STATE))

    def combine(lhs, rhs):
        a1r, a1i, u1r, u1i = lhs
        a2r, a2i, u2r, u2i = rhs
        return (a1r * a2r - a1i * a2i, a1r * a2i + a1i * a2r,
                a2r * u1r - a2i * u1i + u2r, a2r * u1i + a2i * u1r + u2i)

    _, _, xr, xi = lax.associative_scan(combine, (a_re, a_im, bu_re, bu_im), axis=1)
    y = (jnp.einsum('bsgp,ghp->bsgh', xr, c_re.astype(F32)) - jnp.einsum('bsgp,ghp->bsgh', xi, c_im.astype(F32))
         + d_skip.astype(F32).reshape(S5_GROUPS, S5_GROUP) * uf)
    g = jax.nn.gelu(y.reshape(bsz, s, MIX_C)).astype(u.dtype)
    return g * jax.nn.sigmoid(g @ glu_w + glu_b)


def gla(q, k, v, log_alpha, r, norm_w):
    L = GLA_CHUNK
    bsz, _, nh, dk = q.shape
    dv = v.shape[-1]
    qc = to_chunks(q, L) * (dk ** -0.5)
    kc = to_chunks(k, L)
    vc = to_chunks(v, L)
    bcum = jnp.cumsum(to_chunks(log_alpha, L), axis=3)
    b_last = bcum[..., -1, :]
    q_dec = qc * jnp.exp(bcum)
    k_inv = kc * jnp.exp(-bcum)
    k_end = kc * jnp.exp(b_last[..., None, :] - bcum)
    u = jnp.einsum('bhnld,bhnle->bhnde', k_end, vc)

    def step(state, xs):
        g, uu = xs
        return jnp.exp(g)[..., None] * state + uu, state

    _, s_prev = lax.scan(step, jnp.zeros((bsz, nh, dk, dv), F32), (jnp.moveaxis(b_last, 2, 0), jnp.moveaxis(u, 2, 0)))
    s_prev = jnp.moveaxis(s_prev, 0, 2)
    causal = jnp.tril(jnp.ones((L, L), dtype=bool))
    att = jnp.where(causal, jnp.einsum('bhnid,bhnjd->bhnij', q_dec, k_inv), 0.0)
    o = jnp.einsum('bhnij,bhnje->bhnie', att, vc) + jnp.einsum('bhnid,bhnde->bhnie', q_dec, s_prev)
    o = head_norm(from_chunks(o), norm_w, False)
    return o * jax.nn.silu(r.astype(F32))


def odd_mixer(x, w_in, lam_re, lam_im, b_re, b_im, c_re, c_im, d_skip, log_dt, glu_w, glu_b, gate_w, gate_b, norm_w):
    bsz, s, _ = x.shape
    u, q, k, v, r, glow = jnp.split(x @ w_in, ODD_SPLITS, axis=-1)
    y_c = s5(u, lam_re, lam_im, b_re, b_im, c_re, c_im, d_skip, log_dt, glu_w, glu_b)
    log_alpha = jax.nn.log_sigmoid((glow @ gate_w + gate_b).astype(F32)) / GLA_GATE_TEMP
    hk = lambda t, d: t.astype(F32).reshape(bsz, s, GLA_HEADS, d)
    y_d = gla(hk(q, GLA_DK), hk(k, GLA_DK), hk(v, GLA_DV), hk(log_alpha, GLA_DK), r, norm_w)
    return jnp.concatenate([y_c.astype(x.dtype), y_d.astype(x.dtype)], axis=-1)


def moe(x, router_w, router_bias, w_gate, w_up, w_down, sw_gate, sw_up, sw_down):
    bsz, s, d = x.shape
    xt = x.reshape(-1, d)
    t = xt.shape[0]
    scores = jax.nn.sigmoid((xt @ router_w).astype(F32))
    biased = scores + router_bias.astype(F32)
    grp_score = jnp.sum(lax.top_k(biased.reshape(t, N_GROUPS, -1), 2)[0], axis=-1)
    _, grp_idx = lax.top_k(grp_score, TOPK_GROUPS)
    grp_mask = jnp.sum(jax.nn.one_hot(grp_idx, N_GROUPS, dtype=F32), axis=-2) > 0
    exp_mask = jnp.repeat(grp_mask, N_EXPERTS // N_GROUPS, axis=-1)
    _, idx = lax.top_k(jnp.where(exp_mask, biased, -jnp.inf), TOP_K)
    wts = jnp.take_along_axis(scores, idx, axis=-1)
    wts = wts / jnp.sum(wts, axis=-1, keepdims=True) * ROUTED_SCALE
    n_assign = t * TOP_K
    e_flat = idx.reshape(-1)
    tok = jnp.broadcast_to(jnp.arange(t, dtype=jnp.int32)[:, None], (t, TOP_K)).reshape(-1)
    order = jnp.argsort(e_flat)
    e_s = e_flat[order]
    tok_s = tok[order]
    w_s = wts.reshape(-1)[order]
    counts = jnp.bincount(e_flat, length=N_EXPERTS)
    start = jnp.cumsum(counts) - counts
    padded = (counts + MOE_BLOCK - 1) // MOE_BLOCK * MOE_BLOCK
    pend = jnp.cumsum(padded)
    pstart = pend - padded
    pos = pstart[e_s] + jnp.arange(n_assign) - start[e_s]
    n_blocks = n_assign // MOE_BLOCK + N_EXPERTS
    rows = n_blocks * MOE_BLOCK
    row_tok = jnp.zeros((rows,), jnp.int32).at[pos].set(tok_s)
    row_w = jnp.zeros((rows,), F32).at[pos].set(w_s)
    block_e = jnp.minimum(jnp.searchsorted(pend, jnp.arange(n_blocks) * MOE_BLOCK, side='right'), N_EXPERTS - 1)

    def expert_block(args):
        b_tok, b_w, e = args
        xb = xt[b_tok]
        h = jax.nn.silu(xb @ w_gate[e]) * (xb @ w_up[e])
        return (h @ w_down[e]) * b_w[:, None].astype(xt.dtype)

    y_rows = lax.map(expert_block, (row_tok.reshape(n_blocks, MOE_BLOCK), row_w.reshape(n_blocks, MOE_BLOCK), block_e))
    routed = jax.ops.segment_sum(y_rows.reshape(rows, d), row_tok, num_segments=t)
    shared = (jax.nn.silu(xt @ sw_gate) * (xt @ sw_up)) @ sw_down
    return (routed + shared).reshape(bsz, s, d)


def setup_inputs(seed: int = 0) -> dict:
    key = jax.random.key(seed)
    keys = iter(jax.random.split(key, 64))

    def nrm(shape, scale):
        return jax.random.normal(next(keys), shape, F32) * scale

    def gain(shape):
        return 1.0 + nrm(shape, 0.02)

    D = D_MODEL
    x = nrm((BATCH, SEQ, D), 1.0)
    ln1_w = gain((DEPTH, D))
    ln1_b = nrm((DEPTH, D), 0.02)
    ln2_w = gain((DEPTH, D))
    ln2_b = nrm((DEPTH, D), 0.02)
    w_out = nrm((DEPTH, MIX_WIDTH, D), MIX_WIDTH ** -0.5 * BETA)
    w_in_even = nrm((N_EVEN, D, EVEN_IN), D ** -0.5)
    mlstm_gate_b = jnp.concatenate([nrm((N_EVEN, MLSTM_HEADS), 0.1),
                                    jnp.linspace(3.0, 6.0, MLSTM_HEADS, dtype=F32) + nrm((N_EVEN, MLSTM_HEADS), 0.1)], axis=-1)
    mlstm_norm_w = gain((N_EVEN, MIX_A))
    lru_conv_w = nrm((N_EVEN, LRU_CONV, MIX_B), LRU_CONV ** -0.5)
    lru_conv_b = nrm((N_EVEN, MIX_B), 0.02)
    lru_wa = nrm((N_EVEN, LRU_HEADS, LRU_DH, LRU_DH), LRU_DH ** -0.5)
    lru_ba = nrm((N_EVEN, MIX_B), 0.02)
    lru_wx = nrm((N_EVEN, LRU_HEADS, LRU_DH, LRU_DH), LRU_DH ** -0.5)
    lru_bx = nrm((N_EVEN, MIX_B), 0.02)
    a0 = jax.random.uniform(next(keys), (N_EVEN, MIX_B), F32, minval=0.9, maxval=0.999)
    p0 = a0 ** (1.0 / LRU_C)
    lru_lambda = jnp.log(p0) - jnp.log1p(-p0)
    w_in_odd = nrm((N_ODD, D, ODD_IN), D ** -0.5)
    s5_lam_re = -0.5 + nrm((N_ODD, S5_GROUPS, S5_STATE), 0.01)
    s5_lam_im = jnp.pi * jnp.arange(S5_STATE, dtype=F32) + nrm((N_ODD, S5_GROUPS, S5_STATE), 0.01)
    s5_b_re = nrm((N_ODD, S5_GROUPS, S5_STATE, S5_GROUP), S5_GROUP ** -0.5)
    s5_b_im = nrm((N_ODD, S5_GROUPS, S5_STATE, S5_GROUP), S5_GROUP ** -0.5)
    s5_c_re = nrm((N_ODD, S5_GROUPS, S5_GROUP, S5_STATE), S5_STATE ** -0.5)
    s5_c_im = nrm((N_ODD, S5_GROUPS, S5_GROUP, S5_STATE), S5_STATE ** -0.5)
    s5_d = nrm((N_ODD, MIX_C), 1.0)
    s5_log_dt = jax.random.uniform(next(keys), (N_ODD, S5_GROUPS), F32, minval=math.log(1e-3), maxval=math.log(1e-1))
    s5_glu_w = nrm((N_ODD, MIX_C, MIX_C), MIX_C ** -0.5)
    s5_glu_b = nrm((N_ODD, MIX_C), 0.02)
    gla_gate_w = nrm((N_ODD, GLA_GATE_RANK, GLA_HEADS * GLA_DK), GLA_GATE_RANK ** -0.5)
    gla_gate_b = nrm((N_ODD, GLA_HEADS * GLA_DK), 0.02)
    gla_norm_w = gain((N_ODD, MIX_D))
    router_w = nrm((DEPTH, D, N_EXPERTS), D ** -0.5)
    router_bias = nrm((DEPTH, N_EXPERTS), 0.01)
    exp_w_gate = nrm((DEPTH, N_EXPERTS, D, D_EXPERT), D ** -0.5)
    exp_w_up = nrm((DEPTH, N_EXPERTS, D, D_EXPERT), D ** -0.5)
    exp_w_down = nrm((DEPTH, N_EXPERTS, D_EXPERT, D), D_EXPERT ** -0.5 * BETA)
    sh_w_gate = nrm((DEPTH, D, D_SHARED), D ** -0.5)
    sh_w_up = nrm((DEPTH, D, D_SHARED), D ** -0.5)
    sh_w_down = nrm((DEPTH, D_SHARED, D), D_SHARED ** -0.5 * BETA)
    return {'x': x, 'ln1_w': ln1_w, 'ln1_b': ln1_b, 'ln2_w': ln2_w, 'ln2_b': ln2_b, 'w_out': w_out,
            'w_in_even': w_in_even, 'mlstm_gate_b': mlstm_gate_b, 'mlstm_norm_w': mlstm_norm_w,
            'lru_conv_w': lru_conv_w, 'lru_conv_b': lru_conv_b, 'lru_wa': lru_wa, 'lru_ba': lru_ba,
            'lru_wx': lru_wx, 'lru_bx': lru_bx, 'lru_lambda': lru_lambda,
            'w_in_odd': w_in_odd, 's5_lam_re': s5_lam_re, 's5_lam_im': s5_lam_im, 's5_b_re': s5_b_re,
            's5_b_im': s5_b_im, 's5_c_re': s5_c_re, 's5_c_im': s5_c_im, 's5_d': s5_d, 's5_log_dt': s5_log_dt,
            's5_glu_w': s5_glu_w, 's5_glu_b': s5_glu_b, 'gla_gate_w': gla_gate_w, 'gla_gate_b': gla_gate_b,
            'gla_norm_w': gla_norm_w, 'router_w': router_w, 'router_bias': router_bias,
            'exp_w_gate': exp_w_gate, 'exp_w_up': exp_w_up, 'exp_w_down': exp_w_down,
            'sh_w_gate': sh_w_gate, 'sh_w_up': sh_w_up, 'sh_w_down': sh_w_down}


def reference(x, ln1_w, ln1_b, ln2_w, ln2_b, w_out,
              w_in_even, mlstm_gate_b, mlstm_norm_w,
              lru_conv_w, lru_conv_b, lru_wa, lru_ba, lru_wx, lru_bx, lru_lambda,
              w_in_odd, s5_lam_re, s5_lam_im, s5_b_re, s5_b_im, s5_c_re, s5_c_im, s5_d, s5_log_dt,
              s5_glu_w, s5_glu_b, gla_gate_w, gla_gate_b, gla_norm_w,
              router_w, router_bias, exp_w_gate, exp_w_up, exp_w_down,
              sh_w_gate, sh_w_up, sh_w_down):
    h = x
    for layer in range(DEPTH):
        j = layer // 2
        if layer % 2 == 0:
            mixed = even_mixer(h, w_in_even[j], mlstm_gate_b[j], mlstm_norm_w[j], lru_conv_w[j], lru_conv_b[j],
                               lru_wa[j], lru_ba[j], lru_wx[j], lru_bx[j], lru_lambda[j])
        else:
            mixed = odd_mixer(h, w_in_odd[j], s5_lam_re[j], s5_lam_im[j], s5_b_re[j], s5_b_im[j], s5_c_re[j],
                              s5_c_im[j], s5_d[j], s5_log_dt[j], s5_glu_w[j], s5_glu_b[j], gla_gate_w[j],
                              gla_gate_b[j], gla_norm_w[j])
        h = layer_norm(ALPHA * h + mixed @ w_out[layer], ln1_w[layer], ln1_b[layer])
        ffn = moe(h, router_w[layer], router_bias[layer], exp_w_gate[layer], exp_w_up[layer], exp_w_down[layer],
                  sh_w_gate[layer], sh_w_up[layer], sh_w_down[layer])
        h = layer_norm(ALPHA * h + ffn, ln2_w[layer], ln2_b[layer])
    return h
```

```python
import functools
import math

import jax
import jax.numpy as jnp
from jax import lax
from jax.experimental import pallas as pl
from jax.experimental.pallas import tpu as pltpu

F32 = jnp.float32
BF16 = jnp.bfloat16

D_MODEL = 1024
DEPTH = 2
MIX_HALF = 512
HEADS = 4
HEAD_DIM = 128
GLA_DK = 64
GLA_CHUNK = 64
GLA_GATE_RANK = 16
GLA_GATE_TEMP = 16.0
LRU_C = 8.0
LRU_CONV = 4
S5_GROUP = 16
S5_GROUPS = 32
S5_STATE = 64
S5_LANES = S5_GROUPS * S5_STATE
S5_BLOCKS = 4
N_EXPERTS = 64
N_GROUPS = 8
GROUP_SIZE = N_EXPERTS // N_GROUPS
TOP_K = 8
TOPK_GROUPS = 4
D_EXPERT = 256
ROUTED_SCALE = 2.5
ALPHA = (2.0 * DEPTH) ** 0.25
EPS = 1e-5
LANES = 128
NEG_INF = float("-inf")

VMEM_LIMIT = 56 * 1024 * 1024

MLSTM_CHUNK = 128
MLSTM_TILE = 1024
LRU_TILE = 256
S5_TILE = 512
S5_SUB = 64
GLA_TILE = 1024
PROJ_TILE = 512
OUT_TILE = 512
ROUTER_TILE = 512
MOE_TILE = 1024
MOE_EXPERTS_PER_STEP = 2


def _params(*sem):
    return pltpu.CompilerParams(dimension_semantics=sem, vmem_limit_bytes=VMEM_LIMIT)


def _split3(x):
    hi = x.astype(BF16)
    r1 = x - hi.astype(F32)
    mid = r1.astype(BF16)
    lo = (r1 - mid.astype(F32)).astype(BF16)
    return hi, mid, lo


def _dot(a, b):
    return jnp.dot(a, b, preferred_element_type=F32)


def _dot_nt(a, b):
    return lax.dot_general(a, b, (((1,), (1,)), ((), ())), preferred_element_type=F32)


def _dot_tn(a, b):
    return lax.dot_general(a, b, (((0,), (0,)), ((), ())), preferred_element_type=F32)


def _exact_left01(mask01_bf16, x):
    hi, mid, lo = _split3(x)
    return _dot(mask01_bf16, hi) + _dot(mask01_bf16, mid) + _dot(mask01_bf16, lo)


def _exact_right01(x, mask01_bf16):
    hi, mid, lo = _split3(x)
    return _dot(hi, mask01_bf16) + _dot(mid, mask01_bf16) + _dot(lo, mask01_bf16)


def _log_sigmoid(x):
    return jnp.minimum(x, 0.0) - jnp.log(1.0 + jnp.exp(-jnp.abs(x)))


def _sigmoid(x):
    return 1.0 / (1.0 + jnp.exp(-x))


def _gelu_tanh(x):
    c = math.sqrt(2.0 / math.pi)
    return 0.5 * x * (1.0 + jnp.tanh(c * (x + 0.044715 * (x * x * x))))


def _layer_norm(z, w, b):
    mu = jnp.mean(z, axis=-1, keepdims=True)
    zc = z - mu
    return zc * lax.rsqrt(jnp.mean(zc * zc, axis=-1, keepdims=True) + EPS) * w + b


def _shift_rows(x, s, fill):
    rows = lax.broadcasted_iota(jnp.int32, x.shape, 0)
    return jnp.where(rows < s, fill, pltpu.roll(x, s, 0))


def _proj_kernel(x_ref, w_ref, wg_ref, o_ref, og_ref):
    x = x_ref[...].astype(BF16)
    o_ref[...] = _dot(x, w_ref[...])
    og_ref[...] = _dot(x, wg_ref[...])


def _proj(x, w_main, w_small):
    t, d = x.shape
    n = w_main.shape[1]
    tm = PROJ_TILE
    return pl.pallas_call(
        _proj_kernel,
        grid=(t // tm,),
        in_specs=[pl.BlockSpec((tm, d), lambda i: (i, 0)),
                  pl.BlockSpec((d, n), lambda i: (0, 0)),
                  pl.BlockSpec((d, LANES), lambda i: (0, 0))],
        out_specs=[pl.BlockSpec((tm, n), lambda i: (i, 0)),
                   pl.BlockSpec((tm, LANES), lambda i: (i, 0))],
        out_shape=[jax.ShapeDtypeStruct((t, n), F32), jax.ShapeDtypeStruct((t, LANES), F32)],
        compiler_params=_params("parallel"),
        name="in_proj",
    )(x, w_main, w_small)


def _mlstm_kernel(q_ref, k_ref, v_ref, o_ref, gc_ref, gr_ref, bc_ref, br_ref, nw_ref,
                  y_ref, c_ref, m_ref, *, chunk, n_chunks):
    L = chunk

    @pl.when(pl.program_id(1) == 0)
    def _():
        c_ref[...] = jnp.zeros_like(c_ref)
        m_ref[...] = jnp.zeros_like(m_ref)

    ri = lax.broadcasted_iota(jnp.int32, (L, L), 0)
    ci = lax.broadcasted_iota(jnp.int32, (L, L), 1)
    causal = ci <= ri
    tril = causal.astype(BF16)
    triu = (ri <= ci).astype(BF16)
    ones_v = jnp.ones((L, HEAD_DIM), BF16)
    scale = HEAD_DIM ** -0.5

    def body(c, carry):
        r0 = pl.multiple_of(c * L, L)
        g_col = gc_ref[pl.ds(r0, L), :] + bc_ref[...]
        g_row = gr_ref[c] + br_ref[...]
        b_col_all = _exact_left01(tril, _log_sigmoid(g_col))
        b_row_all = _exact_right01(_log_sigmoid(g_row), triu)
        for h in range(HEADS):
            lo = h * HEAD_DIM
            q = q_ref[pl.ds(r0, L), lo:lo + HEAD_DIM].astype(BF16)
            k = k_ref[pl.ds(r0, L), lo:lo + HEAD_DIM] * scale
            v = v_ref[pl.ds(r0, L), lo:lo + HEAD_DIM].astype(BF16)
            v_aug = jnp.concatenate([v, ones_v], axis=1)
            i_col = g_col[:, h:h + 1]
            i_row = g_row[h:h + 1, :]
            b_col = b_col_all[:, HEADS + h:HEADS + h + 1]
            b_row = b_row_all[HEADS + h:HEADS + h + 1, :]
            b_last = b_col[L - 1:L, :]
            m_prev = m_ref[h:h + 1, 0:1]
            c_prev = c_ref[h]

            d_mat = jnp.where(causal, b_col - b_row + i_row, NEG_INF)
            m_inter = b_col + m_prev
            m_i = jnp.maximum(m_inter, jnp.max(d_mat, axis=1, keepdims=True))
            s = _dot_nt(q, k.astype(BF16)) * jnp.exp(d_mat - m_i)
            w_inter = jnp.exp(m_inter - m_i)
            both = _dot(s.astype(BF16), v_aug) + w_inter * _dot(q, c_prev.astype(BF16))
            num = both[:, :HEAD_DIM]
            den = both[:, HEAD_DIM:]
            hh = num / jnp.maximum(jnp.abs(den), jnp.exp(-m_i))

            w_loc = b_last - b_col + i_col
            m_loc = jnp.max(w_loc, axis=0, keepdims=True)
            kp = (k * jnp.exp(w_loc - m_loc)).astype(BF16)
            c_loc = _dot_tn(kp, v_aug)
            m_new = jnp.maximum(b_last + m_prev, m_loc)
            c_ref[h] = jnp.exp(b_last + m_prev - m_new) * c_prev + jnp.exp(m_loc - m_new) * c_loc
            m_ref[h:h + 1, :] = jnp.broadcast_to(m_new, (1, LANES))

            hc = hh - jnp.mean(hh, axis=-1, keepdims=True)
            yn = hc * lax.rsqrt(jnp.mean(hc * hc, axis=-1, keepdims=True) + EPS)
            og = o_ref[pl.ds(r0, L), lo:lo + HEAD_DIM]
            y_ref[pl.ds(r0, L), lo:lo + HEAD_DIM] = yn * nw_ref[:, lo:lo + HEAD_DIM] * _sigmoid(og)
        return carry

    lax.fori_loop(0, n_chunks, body, 0)


def _mlstm(proj, gates, gate_b, norm_w, batch, seq):
    t = batch * seq
    L = MLSTM_CHUNK
    ts = MLSTM_TILE
    nj = seq // ts
    nc = ts // L
    g_row = gates[:, :2 * HEADS].reshape(t // L, L, 2 * HEADS).transpose(0, 2, 1)
    b_col = jnp.zeros((1, LANES), F32).at[0, :2 * HEADS].set(gate_b)
    b_row = gate_b.reshape(2 * HEADS, 1)
    blk = lambda col: pl.BlockSpec((ts, MIX_HALF), lambda b, j, col=col: (b * nj + j, col))
    kern = functools.partial(_mlstm_kernel, chunk=L, n_chunks=nc)
    return pl.pallas_call(
        kern,
        grid=(batch, nj),
        in_specs=[blk(0), blk(1), blk(2), blk(3),
                  pl.BlockSpec((ts, LANES), lambda b, j: (b * nj + j, 0)),
                  pl.BlockSpec((nc, 2 * HEADS, L), lambda b, j: (b * nj + j, 0, 0)),
                  pl.BlockSpec((1, LANES), lambda b, j: (0, 0)),
                  pl.BlockSpec((2 * HEADS, 1), lambda b, j: (0, 0)),
                  pl.BlockSpec((1, MIX_HALF), lambda b, j: (0, 0))],
        out_specs=pl.BlockSpec((ts, MIX_HALF), lambda b, j: (b * nj + j, 0)),
        out_shape=jax.ShapeDtypeStruct((t, MIX_HALF), F32),
        scratch_shapes=[pltpu.VMEM((HEADS, HEAD_DIM, 2 * HEAD_DIM), F32),
                        pltpu.VMEM((8, LANES), F32)],
        compiler_params=_params("arbitrary", "arbitrary"),
        name="mlstm",
    )(proj, proj, proj, proj, gates, g_row, b_col, b_row, norm_w.reshape(1, MIX_HALF))


def _rglru_kernel(xb_ref, gb_ref, cw_ref, cb_ref, wa_ref, ba_ref, wx_ref, bx_ref, lam_ref,
                  y_ref, xext_ref, h_ref, *, tile):
    @pl.when(pl.program_id(1) == 0)
    def _():
        xext_ref[0:8, :] = jnp.zeros((8, MIX_HALF), F32)
        h_ref[...] = jnp.zeros_like(h_ref)

    x = xb_ref[...]
    xext_ref[8:8 + tile, :] = x
    xc = cb_ref[...] + cw_ref[LRU_CONV - 1:LRU_CONV, :] * x
    for tap in range(LRU_CONV - 1):
        back = LRU_CONV - 1 - tap
        xc = xc + cw_ref[tap:tap + 1, :] * xext_ref[8 - back:8 - back + tile, :]
    xext_ref[0:8, :] = x[tile - 8:tile, :]

    xc16 = xc.astype(BF16)
    r_parts, i_parts = [], []
    for h in range(HEADS):
        lo = h * HEAD_DIM
        xh = xc16[:, lo:lo + HEAD_DIM]
        r_parts.append(_dot(xh, wa_ref[h]))
        i_parts.append(_dot(xh, wx_ref[h]))
    r = _sigmoid(jnp.concatenate(r_parts, axis=1) + ba_ref[...])
    ig = _sigmoid(jnp.concatenate(i_parts, axis=1) + bx_ref[...])
    lam = lam_ref[...]
    softplus_neg = jnp.maximum(-lam, 0.0) + jnp.log(1.0 + jnp.exp(-jnp.abs(lam)))
    log_a = -LRU_C * r * softplus_neg
    a = jnp.exp(log_a)
    th = jnp.tanh(log_a)
    u = jnp.sqrt(-2.0 * th / (1.0 - th)) * ig * xc

    s = 1
    while s < tile:
        u = a * _shift_rows(u, s, 0.0) + u
        a = a * _shift_rows(a, s, 1.0)
        s *= 2
    hcur = u + a * h_ref[0:1, :]
    h_ref[...] = jnp.broadcast_to(hcur[tile - 1:tile, :], h_ref.shape)
    y_ref[...] = hcur * _gelu_tanh(gb_ref[...])


def _rglru(proj, conv_w, conv_b, wa, ba, wx, bx, lam, batch, seq):
    t = batch * seq
    ts = LRU_TILE
    nj = seq // ts
    row = lambda a: a.reshape(1, MIX_HALF)
    const2 = lambda shape: pl.BlockSpec(shape, lambda b, j: (0, 0))
    const3 = lambda shape: pl.BlockSpec(shape, lambda b, j: (0, 0, 0))
    blk = lambda col: pl.BlockSpec((ts, MIX_HALF), lambda b, j, col=col: (b * nj + j, col))
    return pl.pallas_call(
        functools.partial(_rglru_kernel, tile=ts),
        grid=(batch, nj),
        in_specs=[blk(4), blk(5), const2((LRU_CONV, MIX_HALF)), const2((1, MIX_HALF)),
                  const3((HEADS, HEAD_DIM, HEAD_DIM)), const2((1, MIX_HALF)),
                  const3((HEADS, HEAD_DIM, HEAD_DIM)), const2((1, MIX_HALF)), const2((1, MIX_HALF))],
        out_specs=pl.BlockSpec((ts, MIX_HALF), lambda b, j: (b * nj + j, 0)),
        out_shape=jax.ShapeDtypeStruct((t, MIX_HALF), F32),
        scratch_shapes=[pltpu.VMEM((ts + 8, MIX_HALF), F32), pltpu.VMEM((8, MIX_HALF), F32)],
        compiler_params=_params("arbitrary", "arbitrary"),
        name="rglru",
    )(proj, proj, conv_w, row(conv_b), wa.astype(BF16), row(ba), wx.astype(BF16), row(bx), row(lam))


def _s5_kernel(u_ref, bre_ref, bim_ref, cre_ref, cim_ref, are_ref, aim_ref, d_ref, gw_ref, gb_ref,
               y_ref, xr_ref, xi_ref, cr_ref, ci_ref, *, tile, sub):
    @pl.when(pl.program_id(1) == 0)
    def _():
        cr_ref[...] = jnp.zeros_like(cr_ref)
        ci_ref[...] = jnp.zeros_like(ci_ref)

    u = u_ref[...]
    u16 = u.astype(BF16)
    blk_c = MIX_HALF // S5_BLOCKS
    blk_s = S5_LANES // S5_BLOCKS
    for j in range(S5_BLOCKS):
        uj = u16[:, j * blk_c:(j + 1) * blk_c]
        xr_ref[:, j * blk_s:(j + 1) * blk_s] = _dot(uj, bre_ref[j])
        xi_ref[:, j * blk_s:(j + 1) * blk_s] = _dot(uj, bim_ref[j])

    n_steps = int(math.log2(sub))
    rows = lax.broadcasted_iota(jnp.int32, (sub, S5_LANES), 0)

    def seg(c, carry):
        r0 = pl.multiple_of(c * sub, sub)
        xr = xr_ref[pl.ds(r0, sub), :]
        xi = xi_ref[pl.ds(r0, sub), :]
        a1r = are_ref[0:1, :]
        a1i = aim_ref[0:1, :]
        pr = cr_ref[0:1, :]
        pi = ci_ref[0:1, :]
        first = rows == 0
        xr = xr + jnp.where(first, a1r * pr - a1i * pi, 0.0)
        xi = xi + jnp.where(first, a1r * pi + a1i * pr, 0.0)
        for kk in range(n_steps):
            sft = 1 << kk
            ar = are_ref[kk:kk + 1, :]
            ai = aim_ref[kk:kk + 1, :]
            sr = jnp.where(rows < sft, 0.0, pltpu.roll(xr, sft, 0))
            si = jnp.where(rows < sft, 0.0, pltpu.roll(xi, sft, 0))
            xr, xi = xr + ar * sr - ai * si, xi + ar * si + ai * sr
        xr_ref[pl.ds(r0, sub), :] = xr
        xi_ref[pl.ds(r0, sub), :] = xi
        cr_ref[...] = jnp.broadcast_to(xr[sub - 1:sub, :], cr_ref.shape)
        ci_ref[...] = jnp.broadcast_to(xi[sub - 1:sub, :], ci_ref.shape)
        return carry

    lax.fori_loop(0, tile // sub, seg, 0)

    parts = []
    for j in range(S5_BLOCKS):
        xrj = xr_ref[:, j * blk_s:(j + 1) * blk_s].astype(BF16)
        xij = xi_ref[:, j * blk_s:(j + 1) * blk_s].astype(BF16)
        parts.append(_dot(xrj, cre_ref[j]) - _dot(xij, cim_ref[j]))
    y = jnp.concatenate(parts, axis=1) + d_ref[...] * u
    g = _gelu_tanh(y)
    y_ref[...] = g * _sigmoid(_dot(g.astype(BF16), gw_ref[...]) + gb_ref[...])


def _s5_tables(lam_re, lam_im, b_re, b_im, c_re, c_im, log_dt, sub):
    lr, li = lam_re.astype(F32), lam_im.astype(F32)
    dt = jnp.exp(log_dt.astype(F32))[:, None]
    mag = jnp.exp(lr * dt)
    abar_re = mag * jnp.cos(li * dt)
    abar_im = mag * jnp.sin(li * dt)
    den = lr * lr + li * li
    nr = abar_re - 1.0
    coef_re = (nr * lr + abar_im * li) / den
    coef_im = (abar_im * lr - nr * li) / den
    bbar_re = coef_re[..., None] * b_re - coef_im[..., None] * b_im
    bbar_im = coef_re[..., None] * b_im + coef_im[..., None] * b_re
    gpb = S5_GROUPS // S5_BLOCKS
    eye = jnp.eye(gpb, dtype=F32)

    def in_map(bb):
        bb = bb.reshape(S5_BLOCKS, gpb, S5_STATE, S5_GROUP)
        return jnp.einsum("jgph,gk->jghkp", bb, eye).reshape(S5_BLOCKS, gpb * S5_GROUP, gpb * S5_STATE)

    def out_map(cc):
        cc = cc.reshape(S5_BLOCKS, gpb, S5_GROUP, S5_STATE)
        return jnp.einsum("jghp,gk->jgpkh", cc, eye).reshape(S5_BLOCKS, gpb * S5_STATE, gpb * S5_GROUP)

    n_steps = int(math.log2(sub))
    pw = (2.0 ** jnp.arange(n_steps, dtype=F32))[:, None, None]
    pmag = jnp.exp(pw * (lr * dt)[None])
    a_re = (pmag * jnp.cos(pw * (li * dt)[None])).reshape(n_steps, S5_LANES)
    a_im = (pmag * jnp.sin(pw * (li * dt)[None])).reshape(n_steps, S5_LANES)
    pad = (-n_steps) % 8
    a_re = jnp.pad(a_re, ((0, pad), (0, 0)))
    a_im = jnp.pad(a_im, ((0, pad), (0, 0)))
    return (in_map(bbar_re).astype(BF16), in_map(bbar_im).astype(BF16),
            out_map(c_re.astype(F32)).astype(BF16), out_map(c_im.astype(F32)).astype(BF16), a_re, a_im)


def _s5(proj, tables, d_skip, glu_w, glu_b, batch, seq):
    t = batch * seq
    ts = S5_TILE
    nj = seq // ts
    bre, bim, cre, cim, a_re, a_im = tables
    blk_c = MIX_HALF // S5_BLOCKS
    blk_s = S5_LANES // S5_BLOCKS
    const2 = lambda shape: pl.BlockSpec(shape, lambda b, j: (0, 0))
    const3 = lambda shape: pl.BlockSpec(shape, lambda b, j: (0, 0, 0))
    return pl.pallas_call(
        functools.partial(_s5_kernel, tile=ts, sub=S5_SUB),
        grid=(batch, nj),
        in_specs=[pl.BlockSpec((ts, MIX_HALF), lambda b, j: (b * nj + j, 0)),
                  const3((S5_BLOCKS, blk_c, blk_s)), const3((S5_BLOCKS, blk_c, blk_s)),
                  const3((S5_BLOCKS, blk_s, blk_c)), const3((S5_BLOCKS, blk_s, blk_c)),
                  const2(a_re.shape), const2(a_im.shape),
                  const2((1, MIX_HALF)), const2((MIX_HALF, MIX_HALF)), const2((1, MIX_HALF))],
        out_specs=pl.BlockSpec((ts, MIX_HALF), lambda b, j: (b * nj + j, 0)),
        out_shape=jax.ShapeDtypeStruct((t, MIX_HALF), F32),
        scratch_shapes=[pltpu.VMEM((ts, S5_LANES), F32), pltpu.VMEM((ts, S5_LANES), F32),
                        pltpu.VMEM((8, S5_LANES), F32), pltpu.VMEM((8, S5_LANES), F32)],
        compiler_params=_params("arbitrary", "arbitrary"),
        name="s5",
    )(proj, bre, bim, cre, cim, a_re, a_im, d_skip.reshape(1, MIX_HALF), glu_w.astype(BF16),
      glu_b.reshape(1, MIX_HALF))


def _gla_kernel(q_ref, k_ref, v_ref, r_ref, gl_ref, gw_ref, gb_ref, nw_ref, y_ref, st_ref,
                *, chunk, n_chunks):
    L = chunk

    @pl.when(pl.program_id(1) == 0)
    def _():
        st_ref[...] = jnp.zeros_like(st_ref)

    ri = lax.broadcasted_iota(jnp.int32, (L, L), 0)
    ci = lax.broadcasted_iota(jnp.int32, (L, L), 1)
    causal = ci <= ri
    tril = causal.astype(BF16)
    scale = GLA_DK ** -0.5

    def body(c, carry):
        r0 = pl.multiple_of(c * L, L)
        z = _dot(gl_ref[pl.ds(r0, L), :].astype(BF16), gw_ref[...]) + gb_ref[...]
        log_alpha = _log_sigmoid(z) * (1.0 / GLA_GATE_TEMP)
        bcum_all = _exact_left01(tril, log_alpha)
        for h in range(HEADS):
            lo = h * HEAD_DIM
            bcum = bcum_all[:, lo:lo + HEAD_DIM]
            b_last = bcum[L - 1:L, :]
            q = q_ref[pl.ds(r0, L), lo:lo + HEAD_DIM] * scale
            k = k_ref[pl.ds(r0, L), lo:lo + HEAD_DIM]
            v = v_ref[pl.ds(r0, L), lo:lo + HEAD_DIM].astype(BF16)
            q_dec = (q * jnp.exp(bcum)).astype(BF16)
            k_inv = (k * jnp.exp(-bcum)).astype(BF16)
            k_end = (k * jnp.exp(b_last - bcum)).astype(BF16)
            st = st_ref[h]
            att = jnp.where(causal, _dot_nt(q_dec, k_inv), 0.0)
            o = _dot(att.astype(BF16), v) + _dot_nt(q_dec, st.astype(BF16))
            st_ref[h] = jnp.exp(b_last) * st + _dot_tn(v, k_end)
            yn = o * lax.rsqrt(jnp.mean(o * o, axis=-1, keepdims=True) + EPS)
            rg = r_ref[pl.ds(r0, L), lo:lo + HEAD_DIM]
            y_ref[pl.ds(r0, L), lo:lo + HEAD_DIM] = yn * nw_ref[:, lo:lo + HEAD_DIM] * (rg * _sigmoid(rg))
        return carry

    lax.fori_loop(0, n_chunks, body, 0)


def _gla(proj, glow, gate_w, gate_b, norm_w, batch, seq):
    t = batch * seq
    ts = GLA_TILE
    nj = seq // ts
    blk = lambda col: pl.BlockSpec((ts, MIX_HALF), lambda b, j, col=col: (b * nj + j, col))
    const2 = lambda shape: pl.BlockSpec(shape, lambda b, j: (0, 0))
    return pl.pallas_call(
        functools.partial(_gla_kernel, chunk=GLA_CHUNK, n_chunks=ts // GLA_CHUNK),
        grid=(batch, nj),
        in_specs=[blk(1), blk(2), blk(3), blk(4),
                  pl.BlockSpec((ts, LANES), lambda b, j: (b * nj + j, 0)),
                  const2((LANES, MIX_HALF)), const2((1, MIX_HALF)), const2((1, MIX_HALF))],
        out_specs=pl.BlockSpec((ts, MIX_HALF), lambda b, j: (b * nj + j, 0)),
        out_shape=jax.ShapeDtypeStruct((t, MIX_HALF), F32),
        scratch_shapes=[pltpu.VMEM((HEADS, HEAD_DIM, HEAD_DIM), F32)],
        compiler_params=_params("arbitrary", "arbitrary"),
        name="gla",
    )(proj, proj, proj, proj, glow, gate_w, gate_b, norm_w.reshape(1, MIX_HALF))


def _pad_heads(w, axis):
    shape = list(w.shape)
    shape[axis:axis + 1] = [HEADS, GLA_DK]
    w = w.reshape(shape)
    pad = [(0, 0)] * w.ndim
    pad[axis + 1] = (0, HEAD_DIM - GLA_DK)
    w = jnp.pad(w, pad)
    shape[axis:axis + 2] = [HEADS * HEAD_DIM]
    return w.reshape(shape)


def _out_kernel(ya_ref, yb_ref, h_ref, w_ref, lw_ref, lb_ref, o_ref, o16_ref):
    mixed = jnp.concatenate([ya_ref[...], yb_ref[...]], axis=1).astype(BF16)
    z = ALPHA * h_ref[...] + _dot(mixed, w_ref[...])
    out = _layer_norm(z, lw_ref[...], lb_ref[...])
    o_ref[...] = out
    o16_ref[...] = out.astype(BF16)


def _out_proj_ln(ya, yb, h, w_out, ln_w, ln_b):
    t = h.shape[0]
    tm = OUT_TILE
    const = lambda shape: pl.BlockSpec(shape, lambda i: (0, 0))
    return pl.pallas_call(
        _out_kernel,
        grid=(t // tm,),
        in_specs=[pl.BlockSpec((tm, MIX_HALF), lambda i: (i, 0)),
                  pl.BlockSpec((tm, MIX_HALF), lambda i: (i, 0)),
                  pl.BlockSpec((tm, D_MODEL), lambda i: (i, 0)),
                  const((D_MODEL, D_MODEL)), const((1, D_MODEL)), const((1, D_MODEL))],
        out_specs=[pl.BlockSpec((tm, D_MODEL), lambda i: (i, 0)),
                   pl.BlockSpec((tm, D_MODEL), lambda i: (i, 0))],
        out_shape=[jax.ShapeDtypeStruct((t, D_MODEL), F32), jax.ShapeDtypeStruct((t, D_MODEL), BF16)],
        compiler_params=_params("parallel"),
        name="out_proj_ln",
    )(ya, yb, h, w_out.astype(BF16), ln_w.reshape(1, D_MODEL), ln_b.reshape(1, D_MODEL))


def _first_index(hit, idx, big):
    return jnp.min(jnp.where(hit, idx, big), axis=0, keepdims=True)


def _router_kernel(h_ref, w_ref, b_ref, o_ref, *, tile):
    h_hi, h_mid, _ = _split3(h_ref[...])
    w_hi, w_mid, _ = _split3(w_ref[...])
    logits = _dot_nt(w_hi, h_hi) + _dot_nt(w_hi, h_mid) + _dot_nt(w_mid, h_hi)
    scores = _sigmoid(logits)
    biased = scores + b_ref[...]

    sub = lax.broadcasted_iota(jnp.int32, (GROUP_SIZE, tile), 0)
    grp_rows = []
    for g in range(N_GROUPS):
        xg = biased[g * GROUP_SIZE:(g + 1) * GROUP_SIZE, :]
        m1 = jnp.max(xg, axis=0, keepdims=True)
        i1 = _first_index(xg == m1, sub, GROUP_SIZE)
        m2 = jnp.max(jnp.where(sub == i1, NEG_INF, xg), axis=0, keepdims=True)
        grp_rows.append(m1 + m2)
    gs = jnp.concatenate(grp_rows, axis=0)
    gsel = jnp.zeros((N_GROUPS, tile), F32)
    for _ in range(TOPK_GROUPS):
        mx = jnp.max(gs, axis=0, keepdims=True)
        hit = sub == _first_index(gs == mx, sub, N_GROUPS)
        gsel = jnp.where(hit, 1.0, gsel)
        gs = jnp.where(hit, NEG_INF, gs)
    emask = jnp.concatenate(
        [jnp.broadcast_to(gsel[g:g + 1, :], (GROUP_SIZE, tile)) for g in range(N_GROUPS)], axis=0)

    eidx = lax.broadcasted_iota(jnp.int32, (N_EXPERTS, tile), 0)
    cand = jnp.where(emask > 0.5, biased, NEG_INF)
    sel = jnp.zeros((N_EXPERTS, tile), F32)
    for _ in range(TOP_K):
        mx = jnp.max(cand, axis=0, keepdims=True)
        hit = eidx == _first_index(cand == mx, eidx, N_EXPERTS)
        sel = jnp.where(hit, 1.0, sel)
        cand = jnp.where(hit, NEG_INF, cand)
    picked = jnp.where(sel > 0.5, scores, 0.0)
    wts = picked / jnp.sum(picked, axis=0, keepdims=True) * ROUTED_SCALE
    padded = jnp.concatenate([wts, jnp.zeros((LANES - N_EXPERTS, tile), F32)], axis=0)
    o_ref[...] = padded.T


def _router(h, router_w, router_bias):
    t = h.shape[0]
    tm = ROUTER_TILE
    return pl.pallas_call(
        functools.partial(_router_kernel, tile=tm),
        grid=(t // tm,),
        in_specs=[pl.BlockSpec((tm, D_MODEL), lambda i: (i, 0)),
                  pl.BlockSpec((N_EXPERTS, D_MODEL), lambda i: (0, 0)),
                  pl.BlockSpec((N_EXPERTS, 1), lambda i: (0, 0))],
        out_specs=pl.BlockSpec((tm, LANES), lambda i: (i, 0)),
        out_shape=jax.ShapeDtypeStruct((t, LANES), F32),
        compiler_params=_params("parallel"),
        name="router",
    )(h, router_w.T, router_bias.reshape(N_EXPERTS, 1))


def _silu(x):
    return x * _sigmoid(x)


def _moe_kernel(x_ref, wt_ref, h_ref, wg_ref, wu_ref, wd_ref, sg_ref, su_ref, sd_ref, lw_ref, lb_ref,
                o_ref, acc_ref, *, per_step):
    e_blk = pl.program_id(1)
    x = x_ref[...]

    @pl.when(e_blk == 0)
    def _():
        hs = _silu(_dot(x, sg_ref[...])) * _dot(x, su_ref[...])
        acc_ref[...] = _dot(hs.astype(BF16), sd_ref[...])

    wt = wt_ref[...]
    lane = lax.broadcasted_iota(jnp.int32, wt.shape, 1)
    parts = []
    for i in range(per_step):
        e = e_blk * per_step + i
        w_col = jnp.sum(jnp.where(lane == e, wt, 0.0), axis=1, keepdims=True)
        hh = _silu(_dot(x, wg_ref[i])) * _dot(x, wu_ref[i])
        parts.append(jnp.where(w_col > 0.0, hh * w_col, 0.0).astype(BF16))
    hcat = jnp.concatenate(parts, axis=1)
    acc_ref[...] += _dot(hcat, wd_ref[...].reshape(per_step * D_EXPERT, D_MODEL))

    @pl.when(e_blk == pl.num_programs(1) - 1)
    def _():
        o_ref[...] = _layer_norm(ALPHA * h_ref[...] + acc_ref[...], lw_ref[...], lb_ref[...])


def _moe_ln(x16, wts, h, wg, wu, wd, sg, su, sd, ln_w, ln_b):
    t = h.shape[0]
    tm = MOE_TILE
    ps = MOE_EXPERTS_PER_STEP
    const = lambda shape: pl.BlockSpec(shape, lambda i, e: (0, 0))
    return pl.pallas_call(
        functools.partial(_moe_kernel, per_step=ps),
        grid=(t // tm, N_EXPERTS // ps),
        in_specs=[pl.BlockSpec((tm, D_MODEL), lambda i, e: (i, 0)),
                  pl.BlockSpec((tm, LANES), lambda i, e: (i, 0)),
                  pl.BlockSpec((tm, D_MODEL), lambda i, e: (i, 0)),
                  pl.BlockSpec((ps, D_MODEL, D_EXPERT), lambda i, e: (e, 0, 0)),
                  pl.BlockSpec((ps, D_MODEL, D_EXPERT), lambda i, e: (e, 0, 0)),
                  pl.BlockSpec((ps, D_EXPERT, D_MODEL), lambda i, e: (e, 0, 0)),
                  const((D_MODEL, D_EXPERT)), const((D_MODEL, D_EXPERT)), const((D_EXPERT, D_MODEL)),
                  const((1, D_MODEL)), const((1, D_MODEL))],
        out_specs=pl.BlockSpec((tm, D_MODEL), lambda i, e: (i, 0)),
        out_shape=jax.ShapeDtypeStruct((t, D_MODEL), F32),
        scratch_shapes=[pltpu.VMEM((tm, D_MODEL), F32)],
        compiler_params=_params("parallel", "arbitrary"),
        name="moe_ln",
    )(x16, wts, h, wg.astype(BF16), wu.astype(BF16), wd.astype(BF16),
      sg.astype(BF16), su.astype(BF16), sd.astype(BF16),
      ln_w.reshape(1, D_MODEL), ln_b.reshape(1, D_MODEL))


def _pad_cols(w, width=LANES):
    return jnp.pad(w, ((0, 0), (0, width - w.shape[1])))


def _even_mixer(h, batch, seq, w_in, gate_b, norm_w, conv_w, conv_b, wa, ba, wx, bx, lam):
    a4 = 4 * MIX_HALF
    ng = 2 * HEADS
    w_main = jnp.concatenate([w_in[:, :a4], w_in[:, a4 + ng:]], axis=1).astype(BF16)
    w_gate = _pad_cols(w_in[:, a4:a4 + ng]).astype(BF16)
    proj, gates = _proj(h, w_main, w_gate)
    ya = _mlstm(proj, gates, gate_b, norm_w, batch, seq)
    yb = _rglru(proj, conv_w, conv_b, wa, ba, wx, bx, lam, batch, seq)
    return ya, yb


def _odd_mixer(h, batch, seq, w_in, lam_re, lam_im, b_re, b_im, c_re, c_im, d_skip, log_dt,
               glu_w, glu_b, gate_w, gate_b, norm_w):
    c0 = MIX_HALF
    c1 = c0 + HEADS * GLA_DK
    c2 = c1 + HEADS * GLA_DK
    c3 = c2 + MIX_HALF
    c4 = c3 + MIX_HALF
    w_main = jnp.concatenate([w_in[:, :c0], _pad_heads(w_in[:, c0:c1], 1), _pad_heads(w_in[:, c1:c2], 1),
                              w_in[:, c2:c4]], axis=1).astype(BF16)
    w_low = _pad_cols(w_in[:, c4:]).astype(BF16)
    proj, glow = _proj(h, w_main, w_low)
    tables = _s5_tables(lam_re, lam_im, b_re, b_im, c_re, c_im, log_dt, S5_SUB)
    yc = _s5(proj, tables, d_skip, glu_w, glu_b, batch, seq)
    gw = jnp.pad(_pad_heads(gate_w, 1), ((0, LANES - GLA_GATE_RANK), (0, 0))).astype(BF16)
    gb = _pad_heads(gate_b.reshape(1, -1), 1)
    yd = _gla(proj, glow, gw, gb, norm_w, batch, seq)
    return yc, yd


def kernel(x, ln1_w, ln1_b, ln2_w, ln2_b, w_out, w_in_even, mlstm_gate_b, mlstm_norm_w, lru_conv_w, lru_conv_b, lru_wa, lru_ba, lru_wx, lru_bx, lru_lambda, w_in_odd, s5_lam_re, s5_lam_im, s5_b_re, s5_b_im, s5_c_re, s5_c_im, s5_d, s5_log_dt, s5_glu_w, s5_glu_b, gla_gate_w, gla_gate_b, gla_norm_w, router_w, router_bias, exp_w_gate, exp_w_up, exp_w_down, sh_w_gate, sh_w_up, sh_w_down):
    batch, seq, d = x.shape
    h = x.reshape(batch * seq, d)
    for layer in range(DEPTH):
        j = layer // 2
        if layer % 2 == 0:
            y1, y2 = _even_mixer(h, batch, seq, w_in_even[j], mlstm_gate_b[j], mlstm_norm_w[j],
                                 lru_conv_w[j], lru_conv_b[j], lru_wa[j], lru_ba[j], lru_wx[j],
                                 lru_bx[j], lru_lambda[j])
        else:
            y1, y2 = _odd_mixer(h, batch, seq, w_in_odd[j], s5_lam_re[j], s5_lam_im[j], s5_b_re[j],
                                s5_b_im[j], s5_c_re[j], s5_c_im[j], s5_d[j], s5_log_dt[j],
                                s5_glu_w[j], s5_glu_b[j], gla_gate_w[j], gla_gate_b[j], gla_norm_w[j])
        h, h16 = _out_proj_ln(y1, y2, h, w_out[layer], ln1_w[layer], ln1_b[layer])
        wts = _router(h, router_w[layer], router_bias[layer])
        h = _moe_ln(h16, wts, h, exp_w_gate[layer], exp_w_up[layer], exp_w_down[layer],
                    sh_w_gate[layer], sh_w_up[layer], sh_w_down[layer], ln2_w[layer], ln2_b[layer])
    return h.reshape(batch, seq, d)
```

```python
import functools
import math

import jax
import jax.numpy as jnp
from jax import lax
from jax.experimental import pallas as pl
from jax.experimental.pallas import tpu as pltpu
from jax.experimental.pallas import tpu_sc as plsc

F32 = jnp.float32
BF16 = jnp.bfloat16

D_MODEL = 1024
DEPTH = 2
MIX_HALF = 512
HEADS = 4
HEAD_DIM = 128
GLA_DK = 64
GLA_CHUNK = 64
GLA_GATE_RANK = 16
GLA_GATE_TEMP = 16.0
LRU_C = 8.0
LRU_CONV = 4
S5_GROUP = 16
S5_GROUPS = 32
S5_STATE = 64
S5_LANES = S5_GROUPS * S5_STATE
S5_BLOCKS = 4
N_EXPERTS = 64
N_GROUPS = 8
GROUP_SIZE = N_EXPERTS // N_GROUPS
TOP_K = 8
TOPK_GROUPS = 4
D_EXPERT = 256
ROUTED_SCALE = 2.5
ALPHA = (2.0 * DEPTH) ** 0.25
EPS = 1e-5
LANES = 128
NEG_INF = float("-inf")

VMEM_LIMIT = 56 * 1024 * 1024

MLSTM_CHUNK = 128
MLSTM_TILE = 1024
LRU_TILE = 256
S5_TILE = 512
S5_SUB = 64
GLA_TILE = 1024
PROJ_TILE = 512
OUT_TILE = 512
ROUTER_TILE = 512
MOE_BLOCK = 512
COMBINE_TILE = 256
PACKED = D_MODEL // 2
SC_CHUNK = 64
SC_CORES = 2
SC_SUBCORES = 16
SC_WORKERS = SC_CORES * SC_SUBCORES


def _params(*sem):
    return pltpu.CompilerParams(dimension_semantics=sem, vmem_limit_bytes=VMEM_LIMIT)


def _split3(x):
    hi = x.astype(BF16)
    r1 = x - hi.astype(F32)
    mid = r1.astype(BF16)
    lo = (r1 - mid.astype(F32)).astype(BF16)
    return hi, mid, lo


def _dot(a, b):
    return jnp.dot(a, b, preferred_element_type=F32)


def _dot_nt(a, b):
    return lax.dot_general(a, b, (((1,), (1,)), ((), ())), preferred_element_type=F32)


def _dot_tn(a, b):
    return lax.dot_general(a, b, (((0,), (0,)), ((), ())), preferred_element_type=F32)


def _exact_left01(mask01_bf16, x):
    hi, mid, lo = _split3(x)
    return _dot(mask01_bf16, hi) + _dot(mask01_bf16, mid) + _dot(mask01_bf16, lo)


def _exact_right01(x, mask01_bf16):
    hi, mid, lo = _split3(x)
    return _dot(hi, mask01_bf16) + _dot(mid, mask01_bf16) + _dot(lo, mask01_bf16)


def _log_sigmoid(x):
    return jnp.minimum(x, 0.0) - jnp.log(1.0 + jnp.exp(-jnp.abs(x)))


def _sigmoid(x):
    return 1.0 / (1.0 + jnp.exp(-x))


def _gelu_tanh(x):
    c = math.sqrt(2.0 / math.pi)
    return 0.5 * x * (1.0 + jnp.tanh(c * (x + 0.044715 * (x * x * x))))


def _layer_norm(z, w, b):
    mu = jnp.mean(z, axis=-1, keepdims=True)
    zc = z - mu
    return zc * lax.rsqrt(jnp.mean(zc * zc, axis=-1, keepdims=True) + EPS) * w + b


def _shift_rows(x, s, fill):
    rows = lax.broadcasted_iota(jnp.int32, x.shape, 0)
    return jnp.where(rows < s, fill, pltpu.roll(x, s, 0))


def _proj_kernel(x_ref, w_ref, wg_ref, o_ref, og_ref):
    x = x_ref[...].astype(BF16)
    o_ref[...] = _dot(x, w_ref[...])
    og_ref[...] = _dot(x, wg_ref[...])


def _proj(x, w_main, w_small):
    t, d = x.shape
    n = w_main.shape[1]
    tm = PROJ_TILE
    return pl.pallas_call(
        _proj_kernel,
        grid=(t // tm,),
        in_specs=[pl.BlockSpec((tm, d), lambda i: (i, 0)),
                  pl.BlockSpec((d, n), lambda i: (0, 0)),
                  pl.BlockSpec((d, LANES), lambda i: (0, 0))],
        out_specs=[pl.BlockSpec((tm, n), lambda i: (i, 0)),
                   pl.BlockSpec((tm, LANES), lambda i: (i, 0))],
        out_shape=[jax.ShapeDtypeStruct((t, n), F32), jax.ShapeDtypeStruct((t, LANES), F32)],
        compiler_params=_params("parallel"),
        name="in_proj",
    )(x, w_main, w_small)


def _mlstm_kernel(q_ref, k_ref, v_ref, o_ref, gc_ref, gr_ref, bc_ref, br_ref, nw_ref,
                  y_ref, c_ref, m_ref, *, chunk, n_chunks):
    L = chunk

    @pl.when(pl.program_id(1) == 0)
    def _():
        c_ref[...] = jnp.zeros_like(c_ref)
        m_ref[...] = jnp.zeros_like(m_ref)

    ri = lax.broadcasted_iota(jnp.int32, (L, L), 0)
    ci = lax.broadcasted_iota(jnp.int32, (L, L), 1)
    causal = ci <= ri
    tril = causal.astype(BF16)
    triu = (ri <= ci).astype(BF16)
    ones_v = jnp.ones((L, HEAD_DIM), BF16)
    scale = HEAD_DIM ** -0.5

    def body(c, carry):
        r0 = pl.multiple_of(c * L, L)
        g_col = gc_ref[pl.ds(r0, L), :] + bc_ref[...]
        g_row = gr_ref[c] + br_ref[...]
        b_col_all = _exact_left01(tril, _log_sigmoid(g_col))
        b_row_all = _exact_right01(_log_sigmoid(g_row), triu)
        for h in range(HEADS):
            lo = h * HEAD_DIM
            q = q_ref[pl.ds(r0, L), lo:lo + HEAD_DIM].astype(BF16)
            k = k_ref[pl.ds(r0, L), lo:lo + HEAD_DIM] * scale
            v = v_ref[pl.ds(r0, L), lo:lo + HEAD_DIM].astype(BF16)
            v_aug = jnp.concatenate([v, ones_v], axis=1)
            i_col = g_col[:, h:h + 1]
            i_row = g_row[h:h + 1, :]
            b_col = b_col_all[:, HEADS + h:HEADS + h + 1]
            b_row = b_row_all[HEADS + h:HEADS + h + 1, :]
            b_last = b_col[L - 1:L, :]
            m_prev = m_ref[h:h + 1, 0:1]
            c_prev = c_ref[h]

            d_mat = jnp.where(causal, b_col - b_row + i_row, NEG_INF)
            m_inter = b_col + m_prev
            m_i = jnp.maximum(m_inter, jnp.max(d_mat, axis=1, keepdims=True))
            s = _dot_nt(q, k.astype(BF16)) * jnp.exp(d_mat - m_i)
            w_inter = jnp.exp(m_inter - m_i)
            both = _dot(s.astype(BF16), v_aug) + w_inter * _dot(q, c_prev.astype(BF16))
            num = both[:, :HEAD_DIM]
            den = both[:, HEAD_DIM:]
            hh = num / jnp.maximum(jnp.abs(den), jnp.exp(-m_i))

            w_loc = b_last - b_col + i_col
            m_loc = jnp.max(w_loc, axis=0, keepdims=True)
            kp = (k * jnp.exp(w_loc - m_loc)).astype(BF16)
            c_loc = _dot_tn(kp, v_aug)
            m_new = jnp.maximum(b_last + m_prev, m_loc)
            c_ref[h] = jnp.exp(b_last + m_prev - m_new) * c_prev + jnp.exp(m_loc - m_new) * c_loc
            m_ref[h:h + 1, :] = jnp.broadcast_to(m_new, (1, LANES))

            hc = hh - jnp.mean(hh, axis=-1, keepdims=True)
            yn = hc * lax.rsqrt(jnp.mean(hc * hc, axis=-1, keepdims=True) + EPS)
            og = o_ref[pl.ds(r0, L), lo:lo + HEAD_DIM]
            y_ref[pl.ds(r0, L), lo:lo + HEAD_DIM] = yn * nw_ref[:, lo:lo + HEAD_DIM] * _sigmoid(og)
        return carry

    lax.fori_loop(0, n_chunks, body, 0)


def _mlstm(proj, gates, gate_b, norm_w, batch, seq):
    t = batch * seq
    L = MLSTM_CHUNK
    ts = MLSTM_TILE
    nj = seq // ts
    nc = ts // L
    g_row = gates[:, :2 * HEADS].reshape(t // L, L, 2 * HEADS).transpose(0, 2, 1)
    b_col = jnp.zeros((1, LANES), F32).at[0, :2 * HEADS].set(gate_b)
    b_row = gate_b.reshape(2 * HEADS, 1)
    blk = lambda col: pl.BlockSpec((ts, MIX_HALF), lambda b, j, col=col: (b * nj + j, col))
    kern = functools.partial(_mlstm_kernel, chunk=L, n_chunks=nc)
    return pl.pallas_call(
        kern,
        grid=(batch, nj),
        in_specs=[blk(0), blk(1), blk(2), blk(3),
                  pl.BlockSpec((ts, LANES), lambda b, j: (b * nj + j, 0)),
                  pl.BlockSpec((nc, 2 * HEADS, L), lambda b, j: (b * nj + j, 0, 0)),
                  pl.BlockSpec((1, LANES), lambda b, j: (0, 0)),
                  pl.BlockSpec((2 * HEADS, 1), lambda b, j: (0, 0)),
                  pl.BlockSpec((1, MIX_HALF), lambda b, j: (0, 0))],
        out_specs=pl.BlockSpec((ts, MIX_HALF), lambda b, j: (b * nj + j, 0)),
        out_shape=jax.ShapeDtypeStruct((t, MIX_HALF), F32),
        scratch_shapes=[pltpu.VMEM((HEADS, HEAD_DIM, 2 * HEAD_DIM), F32),
                        pltpu.VMEM((8, LANES), F32)],
        compiler_params=_params("arbitrary", "arbitrary"),
        name="mlstm",
    )(proj, proj, proj, proj, gates, g_row, b_col, b_row, norm_w.reshape(1, MIX_HALF))


def _rglru_kernel(xb_ref, gb_ref, cw_ref, cb_ref, wa_ref, ba_ref, wx_ref, bx_ref, lam_ref,
                  y_ref, xext_ref, h_ref, *, tile):
    @pl.when(pl.program_id(1) == 0)
    def _():
        xext_ref[0:8, :] = jnp.zeros((8, MIX_HALF), F32)
        h_ref[...] = jnp.zeros_like(h_ref)

    x = xb_ref[...]
    xext_ref[8:8 + tile, :] = x
    xc = cb_ref[...] + cw_ref[LRU_CONV - 1:LRU_CONV, :] * x
    for tap in range(LRU_CONV - 1):
        back = LRU_CONV - 1 - tap
        xc = xc + cw_ref[tap:tap + 1, :] * xext_ref[8 - back:8 - back + tile, :]
    xext_ref[0:8, :] = x[tile - 8:tile, :]

    xc16 = xc.astype(BF16)
    r_parts, i_parts = [], []
    for h in range(HEADS):
        lo = h * HEAD_DIM
        xh = xc16[:, lo:lo + HEAD_DIM]
        r_parts.append(_dot(xh, wa_ref[h]))
        i_parts.append(_dot(xh, wx_ref[h]))
    r = _sigmoid(jnp.concatenate(r_parts, axis=1) + ba_ref[...])
    ig = _sigmoid(jnp.concatenate(i_parts, axis=1) + bx_ref[...])
    lam = lam_ref[...]
    softplus_neg = jnp.maximum(-lam, 0.0) + jnp.log(1.0 + jnp.exp(-jnp.abs(lam)))
    log_a = -LRU_C * r * softplus_neg
    a = jnp.exp(log_a)
    th = jnp.tanh(log_a)
    u = jnp.sqrt(-2.0 * th / (1.0 - th)) * ig * xc

    s = 1
    while s < tile:
        u = a * _shift_rows(u, s, 0.0) + u
        a = a * _shift_rows(a, s, 1.0)
        s *= 2
    hcur = u + a * h_ref[0:1, :]
    h_ref[...] = jnp.broadcast_to(hcur[tile - 1:tile, :], h_ref.shape)
    y_ref[...] = hcur * _gelu_tanh(gb_ref[...])


def _rglru(proj, conv_w, conv_b, wa, ba, wx, bx, lam, batch, seq):
    t = batch * seq
    ts = LRU_TILE
    nj = seq // ts
    row = lambda a: a.reshape(1, MIX_HALF)
    const2 = lambda shape: pl.BlockSpec(shape, lambda b, j: (0, 0))
    const3 = lambda shape: pl.BlockSpec(shape, lambda b, j: (0, 0, 0))
    blk = lambda col: pl.BlockSpec((ts, MIX_HALF), lambda b, j, col=col: (b * nj + j, col))
    return pl.pallas_call(
        functools.partial(_rglru_kernel, tile=ts),
        grid=(batch, nj),
        in_specs=[blk(4), blk(5), const2((LRU_CONV, MIX_HALF)), const2((1, MIX_HALF)),
                  const3((HEADS, HEAD_DIM, HEAD_DIM)), const2((1, MIX_HALF)),
                  const3((HEADS, HEAD_DIM, HEAD_DIM)), const2((1, MIX_HALF)), const2((1, MIX_HALF))],
        out_specs=pl.BlockSpec((ts, MIX_HALF), lambda b, j: (b * nj + j, 0)),
        out_shape=jax.ShapeDtypeStruct((t, MIX_HALF), F32),
        scratch_shapes=[pltpu.VMEM((ts + 8, MIX_HALF), F32), pltpu.VMEM((8, MIX_HALF), F32)],
        compiler_params=_params("arbitrary", "arbitrary"),
        name="rglru",
    )(proj, proj, conv_w, row(conv_b), wa.astype(BF16), row(ba), wx.astype(BF16), row(bx), row(lam))


def _s5_kernel(u_ref, bre_ref, bim_ref, cre_ref, cim_ref, are_ref, aim_ref, d_ref, gw_ref, gb_ref,
               y_ref, xr_ref, xi_ref, cr_ref, ci_ref, *, tile, sub):
    @pl.when(pl.program_id(1) == 0)
    def _():
        cr_ref[...] = jnp.zeros_like(cr_ref)
        ci_ref[...] = jnp.zeros_like(ci_ref)

    u = u_ref[...]
    u16 = u.astype(BF16)
    blk_c = MIX_HALF // S5_BLOCKS
    blk_s = S5_LANES // S5_BLOCKS
    for j in range(S5_BLOCKS):
        uj = u16[:, j * blk_c:(j + 1) * blk_c]
        xr_ref[:, j * blk_s:(j + 1) * blk_s] = _dot(uj, bre_ref[j])
        xi_ref[:, j * blk_s:(j + 1) * blk_s] = _dot(uj, bim_ref[j])

    n_steps = int(math.log2(sub))
    rows = lax.broadcasted_iota(jnp.int32, (sub, S5_LANES), 0)

    def seg(c, carry):
        r0 = pl.multiple_of(c * sub, sub)
        xr = xr_ref[pl.ds(r0, sub), :]
        xi = xi_ref[pl.ds(r0, sub), :]
        a1r = are_ref[0:1, :]
        a1i = aim_ref[0:1, :]
        pr = cr_ref[0:1, :]
        pi = ci_ref[0:1, :]
        first = rows == 0
        xr = xr + jnp.where(first, a1r * pr - a1i * pi, 0.0)
        xi = xi + jnp.where(first, a1r * pi + a1i * pr, 0.0)
        for kk in range(n_steps):
            sft = 1 << kk
            ar = are_ref[kk:kk + 1, :]
            ai = aim_ref[kk:kk + 1, :]
            sr = jnp.where(rows < sft, 0.0, pltpu.roll(xr, sft, 0))
            si = jnp.where(rows < sft, 0.0, pltpu.roll(xi, sft, 0))
            xr, xi = xr + ar * sr - ai * si, xi + ar * si + ai * sr
        xr_ref[pl.ds(r0, sub), :] = xr
        xi_ref[pl.ds(r0, sub), :] = xi
        cr_ref[...] = jnp.broadcast_to(xr[sub - 1:sub, :], cr_ref.shape)
        ci_ref[...] = jnp.broadcast_to(xi[sub - 1:sub, :], ci_ref.shape)
        return carry

    lax.fori_loop(0, tile // sub, seg, 0)

    parts = []
    for j in range(S5_BLOCKS):
        xrj = xr_ref[:, j * blk_s:(j + 1) * blk_s].astype(BF16)
        xij = xi_ref[:, j * blk_s:(j + 1) * blk_s].astype(BF16)
        parts.append(_dot(xrj, cre_ref[j]) - _dot(xij, cim_ref[j]))
    y = jnp.concatenate(parts, axis=1) + d_ref[...] * u
    g = _gelu_tanh(y)
    y_ref[...] = g * _sigmoid(_dot(g.astype(BF16), gw_ref[...]) + gb_ref[...])


def _s5_tables(lam_re, lam_im, b_re, b_im, c_re, c_im, log_dt, sub):
    lr, li = lam_re.astype(F32), lam_im.astype(F32)
    dt = jnp.exp(log_dt.astype(F32))[:, None]
    mag = jnp.exp(lr * dt)
    abar_re = mag * jnp.cos(li * dt)
    abar_im = mag * jnp.sin(li * dt)
    den = lr * lr + li * li
    nr = abar_re - 1.0
    coef_re = (nr * lr + abar_im * li) / den
    coef_im = (abar_im * lr - nr * li) / den
    bbar_re = coef_re[..., None] * b_re - coef_im[..., None] * b_im
    bbar_im = coef_re[..., None] * b_im + coef_im[..., None] * b_re
    gpb = S5_GROUPS // S5_BLOCKS
    eye = jnp.eye(gpb, dtype=F32)

    def in_map(bb):
        bb = bb.reshape(S5_BLOCKS, gpb, S5_STATE, S5_GROUP)
        return jnp.einsum("jgph,gk->jghkp", bb, eye).reshape(S5_BLOCKS, gpb * S5_GROUP, gpb * S5_STATE)

    def out_map(cc):
        cc = cc.reshape(S5_BLOCKS, gpb, S5_GROUP, S5_STATE)
        return jnp.einsum("jghp,gk->jgpkh", cc, eye).reshape(S5_BLOCKS, gpb * S5_STATE, gpb * S5_GROUP)

    n_steps = int(math.log2(sub))
    pw = (2.0 ** jnp.arange(n_steps, dtype=F32))[:, None, None]
    pmag = jnp.exp(pw * (lr * dt)[None])
    a_re = (pmag * jnp.cos(pw * (li * dt)[None])).reshape(n_steps, S5_LANES)
    a_im = (pmag * jnp.sin(pw * (li * dt)[None])).reshape(n_steps, S5_LANES)
    pad = (-n_steps) % 8
    a_re = jnp.pad(a_re, ((0, pad), (0, 0)))
    a_im = jnp.pad(a_im, ((0, pad), (0, 0)))
    return (in_map(bbar_re).astype(BF16), in_map(bbar_im).astype(BF16),
            out_map(c_re.astype(F32)).astype(BF16), out_map(c_im.astype(F32)).astype(BF16), a_re, a_im)


def _s5(proj, tables, d_skip, glu_w, glu_b, batch, seq):
    t = batch * seq
    ts = S5_TILE
    nj = seq // ts
    bre, bim, cre, cim, a_re, a_im = tables
    blk_c = MIX_HALF // S5_BLOCKS
    blk_s = S5_LANES // S5_BLOCKS
    const2 = lambda shape: pl.BlockSpec(shape, lambda b, j: (0, 0))
    const3 = lambda shape: pl.BlockSpec(shape, lambda b, j: (0, 0, 0))
    return pl.pallas_call(
        functools.partial(_s5_kernel, tile=ts, sub=S5_SUB),
        grid=(batch, nj),
        in_specs=[pl.BlockSpec((ts, MIX_HALF), lambda b, j: (b * nj + j, 0)),
                  const3((S5_BLOCKS, blk_c, blk_s)), const3((S5_BLOCKS, blk_c, blk_s)),
                  const3((S5_BLOCKS, blk_s, blk_c)), const3((S5_BLOCKS, blk_s, blk_c)),
                  const2(a_re.shape), const2(a_im.shape),
                  const2((1, MIX_HALF)), const2((MIX_HALF, MIX_HALF)), const2((1, MIX_HALF))],
        out_specs=pl.BlockSpec((ts, MIX_HALF), lambda b, j: (b * nj + j, 0)),
        out_shape=jax.ShapeDtypeStruct((t, MIX_HALF), F32),
        scratch_shapes=[pltpu.VMEM((ts, S5_LANES), F32), pltpu.VMEM((ts, S5_LANES), F32),
                        pltpu.VMEM((8, S5_LANES), F32), pltpu.VMEM((8, S5_LANES), F32)],
        compiler_params=_params("arbitrary", "arbitrary"),
        name="s5",
    )(proj, bre, bim, cre, cim, a_re, a_im, d_skip.reshape(1, MIX_HALF), glu_w.astype(BF16),
      glu_b.reshape(1, MIX_HALF))


def _gla_kernel(q_ref, k_ref, v_ref, r_ref, gl_ref, gw_ref, gb_ref, nw_ref, y_ref, st_ref,
                *, chunk, n_chunks):
    L = chunk

    @pl.when(pl.program_id(1) == 0)
    def _():
        st_ref[...] = jnp.zeros_like(st_ref)

    ri = lax.broadcasted_iota(jnp.int32, (L, L), 0)
    ci = lax.broadcasted_iota(jnp.int32, (L, L), 1)
    causal = ci <= ri
    tril = causal.astype(BF16)
    scale = GLA_DK ** -0.5

    def body(c, carry):
        r0 = pl.multiple_of(c * L, L)
        z = _dot(gl_ref[pl.ds(r0, L), :].astype(BF16), gw_ref[...]) + gb_ref[...]
        log_alpha = _log_sigmoid(z) * (1.0 / GLA_GATE_TEMP)
        bcum_all = _exact_left01(tril, log_alpha)
        for h in range(HEADS):
            lo = h * HEAD_DIM
            bcum = bcum_all[:, lo:lo + HEAD_DIM]
            b_last = bcum[L - 1:L, :]
            q = q_ref[pl.ds(r0, L), lo:lo + HEAD_DIM] * scale
            k = k_ref[pl.ds(r0, L), lo:lo + HEAD_DIM]
            v = v_ref[pl.ds(r0, L), lo:lo + HEAD_DIM].astype(BF16)
            q_dec = (q * jnp.exp(bcum)).astype(BF16)
            k_inv = (k * jnp.exp(-bcum)).astype(BF16)
            k_end = (k * jnp.exp(b_last - bcum)).astype(BF16)
            st = st_ref[h]
            att = jnp.where(causal, _dot_nt(q_dec, k_inv), 0.0)
            o = _dot(att.astype(BF16), v) + _dot_nt(q_dec, st.astype(BF16))
            st_ref[h] = jnp.exp(b_last) * st + _dot_tn(v, k_end)
            yn = o * lax.rsqrt(jnp.mean(o * o, axis=-1, keepdims=True) + EPS)
            rg = r_ref[pl.ds(r0, L), lo:lo + HEAD_DIM]
            y_ref[pl.ds(r0, L), lo:lo + HEAD_DIM] = yn * nw_ref[:, lo:lo + HEAD_DIM] * (rg * _sigmoid(rg))
        return carry

    lax.fori_loop(0, n_chunks, body, 0)


def _gla(proj, glow, gate_w, gate_b, norm_w, batch, seq):
    t = batch * seq
    ts = GLA_TILE
    nj = seq // ts
    blk = lambda col: pl.BlockSpec((ts, MIX_HALF), lambda b, j, col=col: (b * nj + j, col))
    const2 = lambda shape: pl.BlockSpec(shape, lambda b, j: (0, 0))
    return pl.pallas_call(
        functools.partial(_gla_kernel, chunk=GLA_CHUNK, n_chunks=ts // GLA_CHUNK),
        grid=(batch, nj),
        in_specs=[blk(1), blk(2), blk(3), blk(4),
                  pl.BlockSpec((ts, LANES), lambda b, j: (b * nj + j, 0)),
                  const2((LANES, MIX_HALF)), const2((1, MIX_HALF)), const2((1, MIX_HALF))],
        out_specs=pl.BlockSpec((ts, MIX_HALF), lambda b, j: (b * nj + j, 0)),
        out_shape=jax.ShapeDtypeStruct((t, MIX_HALF), F32),
        scratch_shapes=[pltpu.VMEM((HEADS, HEAD_DIM, HEAD_DIM), F32)],
        compiler_params=_params("arbitrary", "arbitrary"),
        name="gla",
    )(proj, proj, proj, proj, glow, gate_w, gate_b, norm_w.reshape(1, MIX_HALF))


def _pad_heads(w, axis):
    shape = list(w.shape)
    shape[axis:axis + 1] = [HEADS, GLA_DK]
    w = w.reshape(shape)
    pad = [(0, 0)] * w.ndim
    pad[axis + 1] = (0, HEAD_DIM - GLA_DK)
    w = jnp.pad(w, pad)
    shape[axis:axis + 2] = [HEADS * HEAD_DIM]
    return w.reshape(shape)


def _pack_bf16_pairs(z):
    hi = lax.bitcast_convert_type(z[:, :PACKED].astype(BF16).astype(F32), jnp.uint32)
    lo = lax.bitcast_convert_type(z[:, PACKED:].astype(BF16).astype(F32), jnp.uint32)
    word = (hi & jnp.uint32(0xFFFF0000)) | lax.shift_right_logical(lo, jnp.uint32(16))
    return lax.bitcast_convert_type(word, jnp.int32)


def _unpack_bf16_pairs(p):
    word = lax.bitcast_convert_type(p, jnp.uint32)
    hi = lax.bitcast_convert_type(word & jnp.uint32(0xFFFF0000), F32)
    lo = lax.bitcast_convert_type(lax.shift_left(word, jnp.uint32(16)), F32)
    return hi, lo


def _out_kernel(ya_ref, yb_ref, h_ref, w_ref, lw_ref, lb_ref, o_ref, opk_ref):
    mixed = jnp.concatenate([ya_ref[...], yb_ref[...]], axis=1).astype(BF16)
    z = ALPHA * h_ref[...] + _dot(mixed, w_ref[...])
    out = _layer_norm(z, lw_ref[...], lb_ref[...])
    o_ref[...] = out
    opk_ref[...] = _pack_bf16_pairs(out)


def _out_proj_ln(ya, yb, h, w_out, ln_w, ln_b):
    t = h.shape[0]
    tm = OUT_TILE
    const = lambda shape: pl.BlockSpec(shape, lambda i: (0, 0))
    return pl.pallas_call(
        _out_kernel,
        grid=(t // tm,),
        in_specs=[pl.BlockSpec((tm, MIX_HALF), lambda i: (i, 0)),
                  pl.BlockSpec((tm, MIX_HALF), lambda i: (i, 0)),
                  pl.BlockSpec((tm, D_MODEL), lambda i: (i, 0)),
                  const((D_MODEL, D_MODEL)), const((1, D_MODEL)), const((1, D_MODEL))],
        out_specs=[pl.BlockSpec((tm, D_MODEL), lambda i: (i, 0)),
                   pl.BlockSpec((tm, PACKED), lambda i: (i, 0))],
        out_shape=[jax.ShapeDtypeStruct((t, D_MODEL), F32), jax.ShapeDtypeStruct((t, PACKED), jnp.int32)],
        compiler_params=_params("parallel"),
        name="out_proj_ln",
    )(ya, yb, h, w_out.astype(BF16), ln_w.reshape(1, D_MODEL), ln_b.reshape(1, D_MODEL))


def _first_index(hit, idx, big):
    return jnp.min(jnp.where(hit, idx, big), axis=0, keepdims=True)


def _router_kernel(h_ref, w_ref, b_ref, idx_ref, rank_ref, wk_ref, cnt_ref, base_ref, *, tile):
    @pl.when(pl.program_id(0) == 0)
    def _():
        base_ref[...] = jnp.zeros_like(base_ref)

    h_hi, h_mid, _ = _split3(h_ref[...])
    w_hi, w_mid, _ = _split3(w_ref[...])
    logits = _dot_nt(w_hi, h_hi) + _dot_nt(w_hi, h_mid) + _dot_nt(w_mid, h_hi)
    scores = _sigmoid(logits)
    biased = scores + b_ref[...]

    sub = lax.broadcasted_iota(jnp.int32, (GROUP_SIZE, tile), 0)
    grp_rows = []
    for g in range(N_GROUPS):
        xg = biased[g * GROUP_SIZE:(g + 1) * GROUP_SIZE, :]
        m1 = jnp.max(xg, axis=0, keepdims=True)
        i1 = _first_index(xg == m1, sub, GROUP_SIZE)
        m2 = jnp.max(jnp.where(sub == i1, NEG_INF, xg), axis=0, keepdims=True)
        grp_rows.append(m1 + m2)
    gs = jnp.concatenate(grp_rows, axis=0)
    gsel = jnp.zeros((N_GROUPS, tile), F32)
    for _ in range(TOPK_GROUPS):
        mx = jnp.max(gs, axis=0, keepdims=True)
        hit = sub == _first_index(gs == mx, sub, N_GROUPS)
        gsel = jnp.where(hit, 1.0, gsel)
        gs = jnp.where(hit, NEG_INF, gs)
    emask = jnp.concatenate(
        [jnp.broadcast_to(gsel[g:g + 1, :], (GROUP_SIZE, tile)) for g in range(N_GROUPS)], axis=0)

    eidx = lax.broadcasted_iota(jnp.int32, (N_EXPERTS, tile), 0)
    cand = jnp.where(emask > 0.5, biased, NEG_INF)
    sel = jnp.zeros((N_EXPERTS, tile), F32)
    hits, picks = [], []
    for _ in range(TOP_K):
        mx = jnp.max(cand, axis=0, keepdims=True)
        first = _first_index(cand == mx, eidx, N_EXPERTS)
        hit = eidx == first
        hits.append(hit)
        picks.append(first)
        sel = jnp.where(hit, 1.0, sel)
        cand = jnp.where(hit, NEG_INF, cand)
    picked = jnp.where(sel > 0.5, scores, 0.0)
    wts = picked / jnp.sum(picked, axis=0, keepdims=True) * ROUTED_SCALE

    ri = lax.broadcasted_iota(jnp.int32, (tile, tile), 0)
    ci = lax.broadcasted_iota(jnp.int32, (tile, tile), 1)
    before = (ri < ci).astype(BF16)
    prior = _dot(sel.astype(BF16), before) + base_ref[:, 0:1]
    ranks = [jnp.sum(jnp.where(hit, prior, 0.0), axis=0, keepdims=True) for hit in hits]
    wsel = [jnp.sum(jnp.where(hit, wts, 0.0), axis=0, keepdims=True) for hit in hits]
    idx_ref[...] = jnp.concatenate(picks, axis=0)
    rank_ref[...] = jnp.concatenate(ranks, axis=0).astype(jnp.int32)
    wk_ref[...] = jnp.concatenate(wsel, axis=0)
    total = base_ref[...] + jnp.sum(sel, axis=1, keepdims=True)
    base_ref[...] = total
    cnt_ref[...] = total


def _router(h, router_w, router_bias):
    t = h.shape[0]
    tm = ROUTER_TILE
    per_tok = lambda dt: jax.ShapeDtypeStruct((TOP_K, t), dt)
    tok_blk = pl.BlockSpec((TOP_K, tm), lambda i: (0, i))
    return pl.pallas_call(
        functools.partial(_router_kernel, tile=tm),
        grid=(t // tm,),
        in_specs=[pl.BlockSpec((tm, D_MODEL), lambda i: (i, 0)),
                  pl.BlockSpec((N_EXPERTS, D_MODEL), lambda i: (0, 0)),
                  pl.BlockSpec((N_EXPERTS, 1), lambda i: (0, 0))],
        out_specs=[tok_blk, tok_blk, tok_blk, pl.BlockSpec((N_EXPERTS, LANES), lambda i: (0, 0))],
        out_shape=[per_tok(jnp.int32), per_tok(jnp.int32), per_tok(F32),
                   jax.ShapeDtypeStruct((N_EXPERTS, LANES), F32)],
        scratch_shapes=[pltpu.VMEM((N_EXPERTS, LANES), F32)],
        compiler_params=_params("arbitrary"),
        name="router",
    )(h, router_w.T, router_bias.reshape(N_EXPERTS, 1))


def _silu(x):
    return x * _sigmoid(x)


def _sc_mesh():
    return plsc.VectorSubcoreMesh(core_axis_name="c", subcore_axis_name="s")


def _sc_worker_id():
    return lax.axis_index("s") * SC_CORES + lax.axis_index("c")


def _dispatch_rows(xpk, pos_chunks, n_rows):
    t = xpk.shape[0]
    n_ch = t // SC_WORKERS // SC_CHUNK

    @functools.partial(
        pl.kernel, mesh=_sc_mesh(),
        out_type=jax.ShapeDtypeStruct((n_rows, PACKED), jnp.int32),
        scratch_types=[pltpu.VMEM((TOP_K, SC_CHUNK), jnp.int32),
                       pltpu.VMEM((SC_CHUNK, PACKED), jnp.int32),
                       pltpu.SemaphoreType.DMA],
        name="moe_dispatch",
    )
    def scatter(x_hbm, pos_hbm, out_hbm, idx_v, rows_v, sem):
        wid = _sc_worker_id()

        @pl.loop(0, n_ch)
        def _(c):
            chunk = wid * n_ch + c
            off = pl.multiple_of(chunk * SC_CHUNK, SC_CHUNK)
            pltpu.sync_copy(pos_hbm.at[chunk], idx_v)
            pltpu.sync_copy(x_hbm.at[pl.ds(off, SC_CHUNK)], rows_v)
            copies = [pltpu.async_copy(rows_v, out_hbm.at[idx_v.at[k]], sem) for k in range(TOP_K)]
            for cp in copies:
                cp.wait()

    return scatter(xpk, pos_chunks)


def _gather_rows(table, idx):
    n = idx.shape[0]
    per_w = n // SC_WORKERS
    n_ch = per_w // SC_CHUNK

    @functools.partial(
        pl.kernel, mesh=_sc_mesh(),
        out_type=jax.ShapeDtypeStruct((n, PACKED), jnp.int32),
        scratch_types=[pltpu.VMEM((SC_CHUNK,), jnp.int32),
                       pltpu.VMEM((SC_CHUNK, PACKED), jnp.int32),
                       pltpu.SemaphoreType.DMA],
        name="moe_gather",
    )
    def gather(table_hbm, idx_hbm, out_hbm, idx_v, rows_v, sem):
        base = _sc_worker_id() * per_w

        @pl.loop(0, n_ch)
        def _(c):
            off = pl.multiple_of(base + c * SC_CHUNK, SC_CHUNK)
            pltpu.sync_copy(idx_hbm.at[pl.ds(off, SC_CHUNK)], idx_v)
            pltpu.async_copy(table_hbm.at[idx_v], rows_v, sem).wait()
            pltpu.sync_copy(rows_v, out_hbm.at[pl.ds(off, SC_CHUNK)])

    return gather(table, idx)


def _unpacked_bf16(p):
    hi, lo = _unpack_bf16_pairs(p)
    return jnp.concatenate([hi.astype(BF16), lo.astype(BF16)], axis=1)


def _expert_kernel(be_ref, nu_ref, x_ref, wg_ref, wu_ref, wd_ref, y_ref, g16_ref, u16_ref, d16_ref):
    i = pl.program_id(0)
    changed = jnp.logical_or(i == 0, be_ref[i] != be_ref[jnp.maximum(i - 1, 0)])

    @pl.when(changed)
    def _():
        g16_ref[...] = wg_ref[0].astype(BF16)
        u16_ref[...] = wu_ref[0].astype(BF16)
        d16_ref[...] = wd_ref[0].astype(BF16)

    @pl.when(i < nu_ref[0])
    def _():
        x = _unpacked_bf16(x_ref[...])
        hh = _silu(_dot(x, g16_ref[...])) * _dot(x, u16_ref[...])
        y_ref[...] = _pack_bf16_pairs(_dot(hh.astype(BF16), d16_ref[...]))


def _experts(block_e, n_used, xs, wg, wu, wd):
    nb = block_e.shape[0]
    bm = MOE_BLOCK
    grid_spec = pltpu.PrefetchScalarGridSpec(
        num_scalar_prefetch=2,
        grid=(nb,),
        in_specs=[pl.BlockSpec((bm, PACKED), lambda i, be, nu: (i, 0)),
                  pl.BlockSpec((1, D_MODEL, D_EXPERT), lambda i, be, nu: (be[i], 0, 0)),
                  pl.BlockSpec((1, D_MODEL, D_EXPERT), lambda i, be, nu: (be[i], 0, 0)),
                  pl.BlockSpec((1, D_EXPERT, D_MODEL), lambda i, be, nu: (be[i], 0, 0))],
        out_specs=pl.BlockSpec((bm, PACKED), lambda i, be, nu: (i, 0)),
        scratch_shapes=[pltpu.VMEM((D_MODEL, D_EXPERT), BF16), pltpu.VMEM((D_MODEL, D_EXPERT), BF16),
                        pltpu.VMEM((D_EXPERT, D_MODEL), BF16)],
    )
    return pl.pallas_call(
        _expert_kernel,
        grid_spec=grid_spec,
        out_shape=jax.ShapeDtypeStruct((nb * bm, PACKED), jnp.int32),
        compiler_params=_params("arbitrary"),
        name="moe_experts",
    )(block_e, n_used, xs, wg, wu, wd)


def _combine_kernel(g_ref, wk_ref, h_ref, xpk_ref, sg_ref, su_ref, sd_ref, lw_ref, lb_ref, o_ref):
    x = _unpacked_bf16(xpk_ref[...])
    hs = _silu(_dot(x, sg_ref[...])) * _dot(x, su_ref[...])
    shared = _dot(hs.astype(BF16), sd_ref[...])
    acc_hi = shared[:, :PACKED]
    acc_lo = shared[:, PACKED:]
    wk = wk_ref[...]
    for k in range(TOP_K):
        y_hi, y_lo = _unpack_bf16_pairs(g_ref[k])
        w = wk[:, k:k + 1]
        acc_hi = acc_hi + w * y_hi
        acc_lo = acc_lo + w * y_lo
    ffn = jnp.concatenate([acc_hi, acc_lo], axis=1)
    o_ref[...] = _layer_norm(ALPHA * h_ref[...] + ffn, lw_ref[...], lb_ref[...])


def _combine_ln(g, wk, h, xpk, sg, su, sd, ln_w, ln_b):
    t = h.shape[0]
    tm = COMBINE_TILE
    const = lambda shape: pl.BlockSpec(shape, lambda i: (0, 0))
    return pl.pallas_call(
        _combine_kernel,
        grid=(t // tm,),
        in_specs=[pl.BlockSpec((TOP_K, tm, PACKED), lambda i: (0, i, 0)),
                  pl.BlockSpec((tm, TOP_K), lambda i: (i, 0)),
                  pl.BlockSpec((tm, D_MODEL), lambda i: (i, 0)),
                  pl.BlockSpec((tm, PACKED), lambda i: (i, 0)),
                  const((D_MODEL, D_EXPERT)), const((D_MODEL, D_EXPERT)), const((D_EXPERT, D_MODEL)),
                  const((1, D_MODEL)), const((1, D_MODEL))],
        out_specs=pl.BlockSpec((tm, D_MODEL), lambda i: (i, 0)),
        out_shape=jax.ShapeDtypeStruct((t, D_MODEL), F32),
        compiler_params=_params("parallel"),
        name="moe_combine_ln",
    )(g, wk, h, xpk, sg.astype(BF16), su.astype(BF16), sd.astype(BF16),
      ln_w.reshape(1, D_MODEL), ln_b.reshape(1, D_MODEL))


def _moe_ln(h, hpk, router_w, router_bias, wg, wu, wd, sg, su, sd, ln_w, ln_b):
    t = h.shape[0]
    idx, rank, wk, counts = _router(h, router_w, router_bias)
    cnt = counts[:, 0].astype(jnp.int32)
    padded = (cnt + MOE_BLOCK - 1) // MOE_BLOCK * MOE_BLOCK
    pend = jnp.cumsum(padded)
    pos = (pend - padded)[idx] + rank
    nb = -(-(t * TOP_K + N_EXPERTS * (MOE_BLOCK - 1)) // MOE_BLOCK)
    block_e = jnp.minimum(jnp.searchsorted(pend, jnp.arange(nb, dtype=jnp.int32) * MOE_BLOCK, side="right"),
                          N_EXPERTS - 1).astype(jnp.int32)
    n_used = (pend[-1] // MOE_BLOCK).astype(jnp.int32).reshape(1)
    pos_chunks = pos.reshape(TOP_K, t // SC_CHUNK, SC_CHUNK).transpose(1, 0, 2)
    xs = _dispatch_rows(hpk, pos_chunks, nb * MOE_BLOCK)
    ys = _experts(block_e, n_used, xs, wg, wu, wd)
    g = _gather_rows(ys, pos.reshape(-1)).reshape(TOP_K, t, PACKED)
    return _combine_ln(g, wk.T, h, hpk, sg, su, sd, ln_w, ln_b)


def _pad_cols(w, width=LANES):
    return jnp.pad(w, ((0, 0), (0, width - w.shape[1])))


def _even_mixer(h, batch, seq, w_in, gate_b, norm_w, conv_w, conv_b, wa, ba, wx, bx, lam):
    a4 = 4 * MIX_HALF
    ng = 2 * HEADS
    w_main = jnp.concatenate([w_in[:, :a4], w_in[:, a4 + ng:]], axis=1).astype(BF16)
    w_gate = _pad_cols(w_in[:, a4:a4 + ng]).astype(BF16)
    proj, gates = _proj(h, w_main, w_gate)
    ya = _mlstm(proj, gates, gate_b, norm_w, batch, seq)
    yb = _rglru(proj, conv_w, conv_b, wa, ba, wx, bx, lam, batch, seq)
    return ya, yb


def _odd_mixer(h, batch, seq, w_in, lam_re, lam_im, b_re, b_im, c_re, c_im, d_skip, log_dt,
               glu_w, glu_b, gate_w, gate_b, norm_w):
    c0 = MIX_HALF
    c1 = c0 + HEADS * GLA_DK
    c2 = c1 + HEADS * GLA_DK
    c3 = c2 + MIX_HALF
    c4 = c3 + MIX_HALF
    w_main = jnp.concatenate([w_in[:, :c0], _pad_heads(w_in[:, c0:c1], 1), _pad_heads(w_in[:, c1:c2], 1),
                              w_in[:, c2:c4]], axis=1).astype(BF16)
    w_low = _pad_cols(w_in[:, c4:]).astype(BF16)
    proj, glow = _proj(h, w_main, w_low)
    tables = _s5_tables(lam_re, lam_im, b_re, b_im, c_re, c_im, log_dt, S5_SUB)
    yc = _s5(proj, tables, d_skip, glu_w, glu_b, batch, seq)
    gw = jnp.pad(_pad_heads(gate_w, 1), ((0, LANES - GLA_GATE_RANK), (0, 0))).astype(BF16)
    gb = _pad_heads(gate_b.reshape(1, -1), 1)
    yd = _gla(proj, glow, gw, gb, norm_w, batch, seq)
    return yc, yd


def kernel(x, ln1_w, ln1_b, ln2_w, ln2_b, w_out, w_in_even, mlstm_gate_b, mlstm_norm_w, lru_conv_w, lru_conv_b, lru_wa, lru_ba, lru_wx, lru_bx, lru_lambda, w_in_odd, s5_lam_re, s5_lam_im, s5_b_re, s5_b_im, s5_c_re, s5_c_im, s5_d, s5_log_dt, s5_glu_w, s5_glu_b, gla_gate_w, gla_gate_b, gla_norm_w, router_w, router_bias, exp_w_gate, exp_w_up, exp_w_down, sh_w_gate, sh_w_up, sh_w_down):
    batch, seq, d = x.shape
    h = x.reshape(batch * seq, d)
    for layer in range(DEPTH):
        j = layer // 2
        if layer % 2 == 0:
            y1, y2 = _even_mixer(h, batch, seq, w_in_even[j], mlstm_gate_b[j], mlstm_norm_w[j],
                                 lru_conv_w[j], lru_conv_b[j], lru_wa[j], lru_ba[j], lru_wx[j],
                                 lru_bx[j], lru_lambda[j])
        else:
            y1, y2 = _odd_mixer(h, batch, seq, w_in_odd[j], s5_lam_re[j], s5_lam_im[j], s5_b_re[j],
                                s5_b_im[j], s5_c_re[j], s5_c_im[j], s5_d[j], s5_log_dt[j],
                                s5_glu_w[j], s5_glu_b[j], gla_gate_w[j], gla_gate_b[j], gla_norm_w[j])
        h, hpk = _out_proj_ln(y1, y2, h, w_out[layer], ln1_w[layer], ln1_b[layer])
        h = _moe_ln(h, hpk, router_w[layer], router_bias[layer], exp_w_gate[layer], exp_w_up[layer],
                    exp_w_down[layer], sh_w_gate[layer], sh_w_up[layer], sh_w_down[layer],
                    ln2_w[layer], ln2_b[layer])
    return h.reshape(batch, seq, d)
```

```python
import functools
import math

import jax
import jax.numpy as jnp
from jax import lax
from jax.experimental import pallas as pl
from jax.experimental.pallas import tpu as pltpu
from jax.experimental.pallas import tpu_sc as plsc

F32 = jnp.float32
BF16 = jnp.bfloat16

D_MODEL = 1024
DEPTH = 2
MIX_HALF = 512
HEADS = 4
HEAD_DIM = 128
GLA_DK = 64
GLA_CHUNK = 64
GLA_GATE_RANK = 16
GLA_GATE_TEMP = 16.0
LRU_C = 8.0
LRU_CONV = 4
S5_GROUP = 16
S5_GROUPS = 32
S5_STATE = 64
S5_LANES = S5_GROUPS * S5_STATE
S5_BLOCKS = 4
N_EXPERTS = 64
N_GROUPS = 8
GROUP_SIZE = N_EXPERTS // N_GROUPS
TOP_K = 8
TOPK_GROUPS = 4
D_EXPERT = 256
ROUTED_SCALE = 2.5
ALPHA = (2.0 * DEPTH) ** 0.25
EPS = 1e-5
LANES = 128
NEG_INF = float("-inf")

VMEM_LIMIT = 56 * 1024 * 1024

MLSTM_CHUNK = 128
MLSTM_TILE = 1024
LRU_TILE = 256
S5_TILE = 512
S5_SUB = 64
GLA_TILE = 1024
PROJ_TILE = 512
OUT_TILE = 512
ROUTER_TILE = 512
MOE_BLOCK = 512
COMBINE_TILE = 256
PACKED = D_MODEL // 2
SC_CHUNK = 64
SC_CORES = 2
SC_SUBCORES = 16
SC_WORKERS = SC_CORES * SC_SUBCORES


def _params(*sem):
    return pltpu.CompilerParams(dimension_semantics=sem, vmem_limit_bytes=VMEM_LIMIT)


def _split3(x):
    hi = x.astype(BF16)
    r1 = x - hi.astype(F32)
    mid = r1.astype(BF16)
    lo = (r1 - mid.astype(F32)).astype(BF16)
    return hi, mid, lo


def _dot(a, b):
    return jnp.dot(a, b, preferred_element_type=F32)


def _dot_nt(a, b):
    return lax.dot_general(a, b, (((1,), (1,)), ((), ())), preferred_element_type=F32)


def _dot_tn(a, b):
    return lax.dot_general(a, b, (((0,), (0,)), ((), ())), preferred_element_type=F32)


def _exact_left01(mask01_bf16, x):
    hi, mid, lo = _split3(x)
    return _dot(mask01_bf16, hi) + _dot(mask01_bf16, mid) + _dot(mask01_bf16, lo)


def _exact_right01(x, mask01_bf16):
    hi, mid, lo = _split3(x)
    return _dot(hi, mask01_bf16) + _dot(mid, mask01_bf16) + _dot(lo, mask01_bf16)


def _log_sigmoid(x):
    return jnp.minimum(x, 0.0) - jnp.log(1.0 + jnp.exp(-jnp.abs(x)))


def _sigmoid(x):
    return 1.0 / (1.0 + jnp.exp(-x))


def _gelu_tanh(x):
    c = math.sqrt(2.0 / math.pi)
    return 0.5 * x * (1.0 + jnp.tanh(c * (x + 0.044715 * (x * x * x))))


def _layer_norm(z, w, b):
    mu = jnp.mean(z, axis=-1, keepdims=True)
    zc = z - mu
    return zc * lax.rsqrt(jnp.mean(zc * zc, axis=-1, keepdims=True) + EPS) * w + b


def _shift_rows(x, s, fill):
    rows = lax.broadcasted_iota(jnp.int32, x.shape, 0)
    return jnp.where(rows < s, fill, pltpu.roll(x, s, 0))


def _proj_kernel(x_ref, w_ref, wg_ref, o_ref, og_ref):
    x = x_ref[...].astype(BF16)
    o_ref[...] = _dot(x, w_ref[...])
    og_ref[...] = _dot(x, wg_ref[...])


def _proj(x, w_main, w_small):
    t, d = x.shape
    n = w_main.shape[1]
    tm = PROJ_TILE
    return pl.pallas_call(
        _proj_kernel,
        grid=(t // tm,),
        in_specs=[pl.BlockSpec((tm, d), lambda i: (i, 0)),
                  pl.BlockSpec((d, n), lambda i: (0, 0)),
                  pl.BlockSpec((d, LANES), lambda i: (0, 0))],
        out_specs=[pl.BlockSpec((tm, n), lambda i: (i, 0)),
                   pl.BlockSpec((tm, LANES), lambda i: (i, 0))],
        out_shape=[jax.ShapeDtypeStruct((t, n), F32), jax.ShapeDtypeStruct((t, LANES), F32)],
        compiler_params=_params("parallel"),
        name="in_proj",
    )(x, w_main, w_small)


def _mlstm_kernel(q_ref, k_ref, v_ref, o_ref, gc_ref, gr_ref, bc_ref, br_ref, nw_ref,
                  y_ref, c_ref, m_ref, *, chunk, n_chunks):
    L = chunk

    @pl.when(pl.program_id(1) == 0)
    def _():
        c_ref[...] = jnp.zeros_like(c_ref)
        m_ref[...] = jnp.zeros_like(m_ref)

    ri = lax.broadcasted_iota(jnp.int32, (L, L), 0)
    ci = lax.broadcasted_iota(jnp.int32, (L, L), 1)
    causal = ci <= ri
    tril = causal.astype(BF16)
    triu = (ri <= ci).astype(BF16)
    ones_v = jnp.ones((L, HEAD_DIM), BF16)
    scale = HEAD_DIM ** -0.5

    def body(c, carry):
        r0 = pl.multiple_of(c * L, L)
        g_col = gc_ref[pl.ds(r0, L), :] + bc_ref[...]
        g_row = gr_ref[c] + br_ref[...]
        b_col_all = _exact_left01(tril, _log_sigmoid(g_col))
        b_row_all = _exact_right01(_log_sigmoid(g_row), triu)
        for h in range(HEADS):
            lo = h * HEAD_DIM
            q = q_ref[pl.ds(r0, L), lo:lo + HEAD_DIM].astype(BF16)
            k = k_ref[pl.ds(r0, L), lo:lo + HEAD_DIM] * scale
            v = v_ref[pl.ds(r0, L), lo:lo + HEAD_DIM].astype(BF16)
            v_aug = jnp.concatenate([v, ones_v], axis=1)
            i_col = g_col[:, h:h + 1]
            i_row = g_row[h:h + 1, :]
            b_col = b_col_all[:, HEADS + h:HEADS + h + 1]
            b_row = b_row_all[HEADS + h:HEADS + h + 1, :]
            b_last = b_col[L - 1:L, :]
            m_prev = m_ref[h:h + 1, 0:1]
            c_prev = c_ref[h]

            d_mat = jnp.where(causal, b_col - b_row + i_row, NEG_INF)
            m_inter = b_col + m_prev
            m_i = jnp.maximum(m_inter, jnp.max(d_mat, axis=1, keepdims=True))
            s = _dot_nt(q, k.astype(BF16)) * jnp.exp(d_mat - m_i)
            w_inter = jnp.exp(m_inter - m_i)
            both = _dot(s.astype(BF16), v_aug) + w_inter * _dot(q, c_prev.astype(BF16))
            num = both[:, :HEAD_DIM]
            den = both[:, HEAD_DIM:]
            hh = num / jnp.maximum(jnp.abs(den), jnp.exp(-m_i))

            w_loc = b_last - b_col + i_col
            m_loc = jnp.max(w_loc, axis=0, keepdims=True)
            kp = (k * jnp.exp(w_loc - m_loc)).astype(BF16)
            c_loc = _dot_tn(kp, v_aug)
            m_new = jnp.maximum(b_last + m_prev, m_loc)
            c_ref[h] = jnp.exp(b_last + m_prev - m_new) * c_prev + jnp.exp(m_loc - m_new) * c_loc
            m_ref[h:h + 1, :] = jnp.broadcast_to(m_new, (1, LANES))

            hc = hh - jnp.mean(hh, axis=-1, keepdims=True)
            yn = hc * lax.rsqrt(jnp.mean(hc * hc, axis=-1, keepdims=True) + EPS)
            og = o_ref[pl.ds(r0, L), lo:lo + HEAD_DIM]
            y_ref[pl.ds(r0, L), lo:lo + HEAD_DIM] = yn * nw_ref[:, lo:lo + HEAD_DIM] * _sigmoid(og)
        return carry

    lax.fori_loop(0, n_chunks, body, 0)


def _mlstm(proj, gates, gate_b, norm_w, batch, seq):
    t = batch * seq
    L = MLSTM_CHUNK
    ts = MLSTM_TILE
    nj = seq // ts
    nc = ts // L
    g_row = gates[:, :2 * HEADS].reshape(t // L, L, 2 * HEADS).transpose(0, 2, 1)
    b_col = jnp.zeros((1, LANES), F32).at[0, :2 * HEADS].set(gate_b)
    b_row = gate_b.reshape(2 * HEADS, 1)
    blk = lambda col: pl.BlockSpec((ts, MIX_HALF), lambda b, j, col=col: (b * nj + j, col))
    kern = functools.partial(_mlstm_kernel, chunk=L, n_chunks=nc)
    return pl.pallas_call(
        kern,
        grid=(batch, nj),
        in_specs=[blk(0), blk(1), blk(2), blk(3),
                  pl.BlockSpec((ts, LANES), lambda b, j: (b * nj + j, 0)),
                  pl.BlockSpec((nc, 2 * HEADS, L), lambda b, j: (b * nj + j, 0, 0)),
                  pl.BlockSpec((1, LANES), lambda b, j: (0, 0)),
                  pl.BlockSpec((2 * HEADS, 1), lambda b, j: (0, 0)),
                  pl.BlockSpec((1, MIX_HALF), lambda b, j: (0, 0))],
        out_specs=pl.BlockSpec((ts, MIX_HALF), lambda b, j: (b * nj + j, 0)),
        out_shape=jax.ShapeDtypeStruct((t, MIX_HALF), F32),
        scratch_shapes=[pltpu.VMEM((HEADS, HEAD_DIM, 2 * HEAD_DIM), F32),
                        pltpu.VMEM((8, LANES), F32)],
        compiler_params=_params("arbitrary", "arbitrary"),
        name="mlstm",
    )(proj, proj, proj, proj, gates, g_row, b_col, b_row, norm_w.reshape(1, MIX_HALF))


def _rglru_kernel(xb_ref, gb_ref, cw_ref, cb_ref, wa_ref, ba_ref, wx_ref, bx_ref, lam_ref,
                  y_ref, xext_ref, h_ref, *, tile):
    @pl.when(pl.program_id(1) == 0)
    def _():
        xext_ref[0:8, :] = jnp.zeros((8, MIX_HALF), F32)
        h_ref[...] = jnp.zeros_like(h_ref)

    x = xb_ref[...]
    xext_ref[8:8 + tile, :] = x
    xc = cb_ref[...] + cw_ref[LRU_CONV - 1:LRU_CONV, :] * x
    for tap in range(LRU_CONV - 1):
        back = LRU_CONV - 1 - tap
        xc = xc + cw_ref[tap:tap + 1, :] * xext_ref[8 - back:8 - back + tile, :]
    xext_ref[0:8, :] = x[tile - 8:tile, :]

    xc16 = xc.astype(BF16)
    r_parts, i_parts = [], []
    for h in range(HEADS):
        lo = h * HEAD_DIM
        xh = xc16[:, lo:lo + HEAD_DIM]
        r_parts.append(_dot(xh, wa_ref[h]))
        i_parts.append(_dot(xh, wx_ref[h]))
    r = _sigmoid(jnp.concatenate(r_parts, axis=1) + ba_ref[...])
    ig = _sigmoid(jnp.concatenate(i_parts, axis=1) + bx_ref[...])
    lam = lam_ref[...]
    softplus_neg = jnp.maximum(-lam, 0.0) + jnp.log(1.0 + jnp.exp(-jnp.abs(lam)))
    log_a = -LRU_C * r * softplus_neg
    a = jnp.exp(log_a)
    th = jnp.tanh(log_a)
    u = jnp.sqrt(-2.0 * th / (1.0 - th)) * ig * xc

    s = 1
    while s < tile:
        u = a * _shift_rows(u, s, 0.0) + u
        a = a * _shift_rows(a, s, 1.0)
        s *= 2
    hcur = u + a * h_ref[0:1, :]
    h_ref[...] = jnp.broadcast_to(hcur[tile - 1:tile, :], h_ref.shape)
    y_ref[...] = hcur * _gelu_tanh(gb_ref[...])


def _rglru(proj, conv_w, conv_b, wa, ba, wx, bx, lam, batch, seq):
    t = batch * seq
    ts = LRU_TILE
    nj = seq // ts
    row = lambda a: a.reshape(1, MIX_HALF)
    const2 = lambda shape: pl.BlockSpec(shape, lambda b, j: (0, 0))
    const3 = lambda shape: pl.BlockSpec(shape, lambda b, j: (0, 0, 0))
    blk = lambda col: pl.BlockSpec((ts, MIX_HALF), lambda b, j, col=col: (b * nj + j, col))
    return pl.pallas_call(
        functools.partial(_rglru_kernel, tile=ts),
        grid=(batch, nj),
        in_specs=[blk(4), blk(5), const2((LRU_CONV, MIX_HALF)), const2((1, MIX_HALF)),
                  const3((HEADS, HEAD_DIM, HEAD_DIM)), const2((1, MIX_HALF)),
                  const3((HEADS, HEAD_DIM, HEAD_DIM)), const2((1, MIX_HALF)), const2((1, MIX_HALF))],
        out_specs=pl.BlockSpec((ts, MIX_HALF), lambda b, j: (b * nj + j, 0)),
        out_shape=jax.ShapeDtypeStruct((t, MIX_HALF), F32),
        scratch_shapes=[pltpu.VMEM((ts + 8, MIX_HALF), F32), pltpu.VMEM((8, MIX_HALF), F32)],
        compiler_params=_params("arbitrary", "arbitrary"),
        name="rglru",
    )(proj, proj, conv_w, row(conv_b), wa.astype(BF16), row(ba), wx.astype(BF16), row(bx), row(lam))


def _s5_kernel(u_ref, bre_ref, bim_ref, cre_ref, cim_ref, are_ref, aim_ref, d_ref, gw_ref, gb_ref,
               y_ref, xr_ref, xi_ref, cr_ref, ci_ref, *, tile, sub):
    @pl.when(pl.program_id(1) == 0)
    def _():
        cr_ref[...] = jnp.zeros_like(cr_ref)
        ci_ref[...] = jnp.zeros_like(ci_ref)

    u = u_ref[...]
    u16 = u.astype(BF16)
    blk_c = MIX_HALF // S5_BLOCKS
    blk_s = S5_LANES // S5_BLOCKS
    for j in range(S5_BLOCKS):
        uj = u16[:, j * blk_c:(j + 1) * blk_c]
        xr_ref[:, j * blk_s:(j + 1) * blk_s] = _dot(uj, bre_ref[j])
        xi_ref[:, j * blk_s:(j + 1) * blk_s] = _dot(uj, bim_ref[j])

    n_steps = int(math.log2(sub))
    rows = lax.broadcasted_iota(jnp.int32, (sub, S5_LANES), 0)

    def seg(c, carry):
        r0 = pl.multiple_of(c * sub, sub)
        xr = xr_ref[pl.ds(r0, sub), :]
        xi = xi_ref[pl.ds(r0, sub), :]
        a1r = are_ref[0:1, :]
        a1i = aim_ref[0:1, :]
        pr = cr_ref[0:1, :]
        pi = ci_ref[0:1, :]
        first = rows == 0
        xr = xr + jnp.where(first, a1r * pr - a1i * pi, 0.0)
        xi = xi + jnp.where(first, a1r * pi + a1i * pr, 0.0)
        for kk in range(n_steps):
            sft = 1 << kk
            ar = are_ref[kk:kk + 1, :]
            ai = aim_ref[kk:kk + 1, :]
            sr = jnp.where(rows < sft, 0.0, pltpu.roll(xr, sft, 0))
            si = jnp.where(rows < sft, 0.0, pltpu.roll(xi, sft, 0))
            xr, xi = xr + ar * sr - ai * si, xi + ar * si + ai * sr
        xr_ref[pl.ds(r0, sub), :] = xr
        xi_ref[pl.ds(r0, sub), :] = xi
        cr_ref[...] = jnp.broadcast_to(xr[sub - 1:sub, :], cr_ref.shape)
        ci_ref[...] = jnp.broadcast_to(xi[sub - 1:sub, :], ci_ref.shape)
        return carry

    lax.fori_loop(0, tile // sub, seg, 0)

    parts = []
    for j in range(S5_BLOCKS):
        xrj = xr_ref[:, j * blk_s:(j + 1) * blk_s].astype(BF16)
        xij = xi_ref[:, j * blk_s:(j + 1) * blk_s].astype(BF16)
        parts.append(_dot(xrj, cre_ref[j]) - _dot(xij, cim_ref[j]))
    y = jnp.concatenate(parts, axis=1) + d_ref[...] * u
    g = _gelu_tanh(y)
    y_ref[...] = g * _sigmoid(_dot(g.astype(BF16), gw_ref[...]) + gb_ref[...])


def _s5_tables(lam_re, lam_im, b_re, b_im, c_re, c_im, log_dt, sub):
    lr, li = lam_re.astype(F32), lam_im.astype(F32)
    dt = jnp.exp(log_dt.astype(F32))[:, None]
    mag = jnp.exp(lr * dt)
    abar_re = mag * jnp.cos(li * dt)
    abar_im = mag * jnp.sin(li * dt)
    den = lr * lr + li * li
    nr = abar_re - 1.0
    coef_re = (nr * lr + abar_im * li) / den
    coef_im = (abar_im * lr - nr * li) / den
    bbar_re = coef_re[..., None] * b_re - coef_im[..., None] * b_im
    bbar_im = coef_re[..., None] * b_im + coef_im[..., None] * b_re
    gpb = S5_GROUPS // S5_BLOCKS
    eye = jnp.eye(gpb, dtype=F32)

    def in_map(bb):
        bb = bb.reshape(S5_BLOCKS, gpb, S5_STATE, S5_GROUP)
        return jnp.einsum("jgph,gk->jghkp", bb, eye).reshape(S5_BLOCKS, gpb * S5_GROUP, gpb * S5_STATE)

    def out_map(cc):
        cc = cc.reshape(S5_BLOCKS, gpb, S5_GROUP, S5_STATE)
        return jnp.einsum("jghp,gk->jgpkh", cc, eye).reshape(S5_BLOCKS, gpb * S5_STATE, gpb * S5_GROUP)

    n_steps = int(math.log2(sub))
    pw = (2.0 ** jnp.arange(n_steps, dtype=F32))[:, None, None]
    pmag = jnp.exp(pw * (lr * dt)[None])
    a_re = (pmag * jnp.cos(pw * (li * dt)[None])).reshape(n_steps, S5_LANES)
    a_im = (pmag * jnp.sin(pw * (li * dt)[None])).reshape(n_steps, S5_LANES)
    pad = (-n_steps) % 8
    a_re = jnp.pad(a_re, ((0, pad), (0, 0)))
    a_im = jnp.pad(a_im, ((0, pad), (0, 0)))
    return (in_map(bbar_re).astype(BF16), in_map(bbar_im).astype(BF16),
            out_map(c_re.astype(F32)).astype(BF16), out_map(c_im.astype(F32)).astype(BF16), a_re, a_im)


def _s5(proj, tables, d_skip, glu_w, glu_b, batch, seq):
    t = batch * seq
    ts = S5_TILE
    nj = seq // ts
    bre, bim, cre, cim, a_re, a_im = tables
    blk_c = MIX_HALF // S5_BLOCKS
    blk_s = S5_LANES // S5_BLOCKS
    const2 = lambda shape: pl.BlockSpec(shape, lambda b, j: (0, 0))
    const3 = lambda shape: pl.BlockSpec(shape, lambda b, j: (0, 0, 0))
    return pl.pallas_call(
        functools.partial(_s5_kernel, tile=ts, sub=S5_SUB),
        grid=(batch, nj),
        in_specs=[pl.BlockSpec((ts, MIX_HALF), lambda b, j: (b * nj + j, 0)),
                  const3((S5_BLOCKS, blk_c, blk_s)), const3((S5_BLOCKS, blk_c, blk_s)),
                  const3((S5_BLOCKS, blk_s, blk_c)), const3((S5_BLOCKS, blk_s, blk_c)),
                  const2(a_re.shape), const2(a_im.shape),
                  const2((1, MIX_HALF)), const2((MIX_HALF, MIX_HALF)), const2((1, MIX_HALF))],
        out_specs=pl.BlockSpec((ts, MIX_HALF), lambda b, j: (b * nj + j, 0)),
        out_shape=jax.ShapeDtypeStruct((t, MIX_HALF), F32),
        scratch_shapes=[pltpu.VMEM((ts, S5_LANES), F32), pltpu.VMEM((ts, S5_LANES), F32),
                        pltpu.VMEM((8, S5_LANES), F32), pltpu.VMEM((8, S5_LANES), F32)],
        compiler_params=_params("arbitrary", "arbitrary"),
        name="s5",
    )(proj, bre, bim, cre, cim, a_re, a_im, d_skip.reshape(1, MIX_HALF), glu_w.astype(BF16),
      glu_b.reshape(1, MIX_HALF))


def _gla_kernel(q_ref, k_ref, v_ref, r_ref, gl_ref, gw_ref, gb_ref, nw_ref, y_ref, st_ref,
                *, chunk, n_chunks):
    L = chunk

    @pl.when(pl.program_id(1) == 0)
    def _():
        st_ref[...] = jnp.zeros_like(st_ref)

    ri = lax.broadcasted_iota(jnp.int32, (L, L), 0)
    ci = lax.broadcasted_iota(jnp.int32, (L, L), 1)
    causal = ci <= ri
    tril = causal.astype(BF16)
    scale = GLA_DK ** -0.5

    def body(c, carry):
        r0 = pl.multiple_of(c * L, L)
        z = _dot(gl_ref[pl.ds(r0, L), :].astype(BF16), gw_ref[...]) + gb_ref[...]
        log_alpha = _log_sigmoid(z) * (1.0 / GLA_GATE_TEMP)
        bcum_all = _exact_left01(tril, log_alpha)
        for h in range(HEADS):
            lo = h * HEAD_DIM
            bcum = bcum_all[:, lo:lo + HEAD_DIM]
            b_last = bcum[L - 1:L, :]
            q = q_ref[pl.ds(r0, L), lo:lo + HEAD_DIM] * scale
            k = k_ref[pl.ds(r0, L), lo:lo + HEAD_DIM]
            v = v_ref[pl.ds(r0, L), lo:lo + HEAD_DIM].astype(BF16)
            q_dec = (q * jnp.exp(bcum)).astype(BF16)
            k_inv = (k * jnp.exp(-bcum)).astype(BF16)
            k_end = (k * jnp.exp(b_last - bcum)).astype(BF16)
            st = st_ref[h]
            att = jnp.where(causal, _dot_nt(q_dec, k_inv), 0.0)
            o = _dot(att.astype(BF16), v) + _dot_nt(q_dec, st.astype(BF16))
            st_ref[h] = jnp.exp(b_last) * st + _dot_tn(v, k_end)
            yn = o * lax.rsqrt(jnp.mean(o * o, axis=-1, keepdims=True) + EPS)
            rg = r_ref[pl.ds(r0, L), lo:lo + HEAD_DIM]
            y_ref[pl.ds(r0, L), lo:lo + HEAD_DIM] = yn * nw_ref[:, lo:lo + HEAD_DIM] * (rg * _sigmoid(rg))
        return carry

    lax.fori_loop(0, n_chunks, body, 0)


def _gla(proj, glow, gate_w, gate_b, norm_w, batch, seq):
    t = batch * seq
    ts = GLA_TILE
    nj = seq // ts
    blk = lambda col: pl.BlockSpec((ts, MIX_HALF), lambda b, j, col=col: (b * nj + j, col))
    const2 = lambda shape: pl.BlockSpec(shape, lambda b, j: (0, 0))
    return pl.pallas_call(
        functools.partial(_gla_kernel, chunk=GLA_CHUNK, n_chunks=ts // GLA_CHUNK),
        grid=(batch, nj),
        in_specs=[blk(1), blk(2), blk(3), blk(4),
                  pl.BlockSpec((ts, LANES), lambda b, j: (b * nj + j, 0)),
                  const2((LANES, MIX_HALF)), const2((1, MIX_HALF)), const2((1, MIX_HALF))],
        out_specs=pl.BlockSpec((ts, MIX_HALF), lambda b, j: (b * nj + j, 0)),
        out_shape=jax.ShapeDtypeStruct((t, MIX_HALF), F32),
        scratch_shapes=[pltpu.VMEM((HEADS, HEAD_DIM, HEAD_DIM), F32)],
        compiler_params=_params("arbitrary", "arbitrary"),
        name="gla",
    )(proj, proj, proj, proj, glow, gate_w, gate_b, norm_w.reshape(1, MIX_HALF))


def _pad_heads(w, axis):
    shape = list(w.shape)
    shape[axis:axis + 1] = [HEADS, GLA_DK]
    w = w.reshape(shape)
    pad = [(0, 0)] * w.ndim
    pad[axis + 1] = (0, HEAD_DIM - GLA_DK)
    w = jnp.pad(w, pad)
    shape[axis:axis + 2] = [HEADS * HEAD_DIM]
    return w.reshape(shape)


def _pack_bf16_pairs(z):
    hi = lax.bitcast_convert_type(z[:, :PACKED].astype(BF16).astype(F32), jnp.uint32)
    lo = lax.bitcast_convert_type(z[:, PACKED:].astype(BF16).astype(F32), jnp.uint32)
    word = (hi & jnp.uint32(0xFFFF0000)) | lax.shift_right_logical(lo, jnp.uint32(16))
    return lax.bitcast_convert_type(word, jnp.int32)


def _unpack_bf16_pairs(p):
    word = lax.bitcast_convert_type(p, jnp.uint32)
    hi = lax.bitcast_convert_type(word & jnp.uint32(0xFFFF0000), F32)
    lo = lax.bitcast_convert_type(lax.shift_left(word, jnp.uint32(16)), F32)
    return hi, lo


def _out_kernel(ya_ref, yb_ref, h_ref, w_ref, lw_ref, lb_ref, o_ref, opk_ref):
    mixed = jnp.concatenate([ya_ref[...], yb_ref[...]], axis=1).astype(BF16)
    z = ALPHA * h_ref[...] + _dot(mixed, w_ref[...])
    out = _layer_norm(z, lw_ref[...], lb_ref[...])
    o_ref[...] = out
    opk_ref[...] = _pack_bf16_pairs(out)


def _out_proj_ln(ya, yb, h, w_out, ln_w, ln_b):
    t = h.shape[0]
    tm = OUT_TILE
    const = lambda shape: pl.BlockSpec(shape, lambda i: (0, 0))
    return pl.pallas_call(
        _out_kernel,
        grid=(t // tm,),
        in_specs=[pl.BlockSpec((tm, MIX_HALF), lambda i: (i, 0)),
                  pl.BlockSpec((tm, MIX_HALF), lambda i: (i, 0)),
                  pl.BlockSpec((tm, D_MODEL), lambda i: (i, 0)),
                  const((D_MODEL, D_MODEL)), const((1, D_MODEL)), const((1, D_MODEL))],
        out_specs=[pl.BlockSpec((tm, D_MODEL), lambda i: (i, 0)),
                   pl.BlockSpec((tm, PACKED), lambda i: (i, 0))],
        out_shape=[jax.ShapeDtypeStruct((t, D_MODEL), F32), jax.ShapeDtypeStruct((t, PACKED), jnp.int32)],
        compiler_params=_params("parallel"),
        name="out_proj_ln",
    )(ya, yb, h, w_out.astype(BF16), ln_w.reshape(1, D_MODEL), ln_b.reshape(1, D_MODEL))


def _first_index(hit, idx, big):
    return jnp.min(jnp.where(hit, idx, big), axis=0, keepdims=True)


def _router_kernel(h_ref, w_ref, b_ref, idx_ref, rank_ref, wk_ref, cnt_ref, base_ref, *, tile):
    @pl.when(pl.program_id(0) == 0)
    def _():
        base_ref[...] = jnp.zeros_like(base_ref)

    h_hi, h_mid, _ = _split3(h_ref[...])
    w_hi, w_mid, _ = _split3(w_ref[...])
    logits = _dot_nt(w_hi, h_hi) + _dot_nt(w_hi, h_mid) + _dot_nt(w_mid, h_hi)
    scores = _sigmoid(logits)
    biased = scores + b_ref[...]

    sub = lax.broadcasted_iota(jnp.int32, (GROUP_SIZE, tile), 0)
    grp_rows = []
    for g in range(N_GROUPS):
        xg = biased[g * GROUP_SIZE:(g + 1) * GROUP_SIZE, :]
        m1 = jnp.max(xg, axis=0, keepdims=True)
        i1 = _first_index(xg == m1, sub, GROUP_SIZE)
        m2 = jnp.max(jnp.where(sub == i1, NEG_INF, xg), axis=0, keepdims=True)
        grp_rows.append(m1 + m2)
    gs = jnp.concatenate(grp_rows, axis=0)
    gsel = jnp.zeros((N_GROUPS, tile), F32)
    for _ in range(TOPK_GROUPS):
        mx = jnp.max(gs, axis=0, keepdims=True)
        hit = sub == _first_index(gs == mx, sub, N_GROUPS)
        gsel = jnp.where(hit, 1.0, gsel)
        gs = jnp.where(hit, NEG_INF, gs)
    emask = jnp.concatenate(
        [jnp.broadcast_to(gsel[g:g + 1, :], (GROUP_SIZE, tile)) for g in range(N_GROUPS)], axis=0)

    eidx = lax.broadcasted_iota(jnp.int32, (N_EXPERTS, tile), 0)
    cand = jnp.where(emask > 0.5, biased, NEG_INF)
    sel = jnp.zeros((N_EXPERTS, tile), F32)
    hits, picks = [], []
    for _ in range(TOP_K):
        mx = jnp.max(cand, axis=0, keepdims=True)
        first = _first_index(cand == mx, eidx, N_EXPERTS)
        hit = eidx == first
        hits.append(hit)
        picks.append(first)
        sel = jnp.where(hit, 1.0, sel)
        cand = jnp.where(hit, NEG_INF, cand)
    picked = jnp.where(sel > 0.5, scores, 0.0)
    wts = picked / jnp.sum(picked, axis=0, keepdims=True) * ROUTED_SCALE

    ri = lax.broadcasted_iota(jnp.int32, (tile, tile), 0)
    ci = lax.broadcasted_iota(jnp.int32, (tile, tile), 1)
    before = (ri < ci).astype(BF16)
    prior = _dot(sel.astype(BF16), before) + base_ref[:, 0:1]
    ranks = [jnp.sum(jnp.where(hit, prior, 0.0), axis=0, keepdims=True) for hit in hits]
    wsel = [jnp.sum(jnp.where(hit, wts, 0.0), axis=0, keepdims=True) for hit in hits]
    idx_ref[...] = jnp.concatenate(picks, axis=0)
    rank_ref[...] = jnp.concatenate(ranks, axis=0).astype(jnp.int32)
    wk_ref[...] = jnp.concatenate(wsel, axis=0)
    total = base_ref[...] + jnp.sum(sel, axis=1, keepdims=True)
    base_ref[...] = total
    cnt_ref[...] = total


def _router(h, router_w, router_bias):
    t = h.shape[0]
    tm = ROUTER_TILE
    per_tok = lambda dt: jax.ShapeDtypeStruct((TOP_K, t), dt)
    tok_blk = pl.BlockSpec((TOP_K, tm), lambda i: (0, i))
    return pl.pallas_call(
        functools.partial(_router_kernel, tile=tm),
        grid=(t // tm,),
        in_specs=[pl.BlockSpec((tm, D_MODEL), lambda i: (i, 0)),
                  pl.BlockSpec((N_EXPERTS, D_MODEL), lambda i: (0, 0)),
                  pl.BlockSpec((N_EXPERTS, 1), lambda i: (0, 0))],
        out_specs=[tok_blk, tok_blk, tok_blk, pl.BlockSpec((N_EXPERTS, LANES), lambda i: (0, 0))],
        out_shape=[per_tok(jnp.int32), per_tok(jnp.int32), per_tok(F32),
                   jax.ShapeDtypeStruct((N_EXPERTS, LANES), F32)],
        scratch_shapes=[pltpu.VMEM((N_EXPERTS, LANES), F32)],
        compiler_params=_params("arbitrary"),
        name="router",
    )(h, router_w.T, router_bias.reshape(N_EXPERTS, 1))


def _silu(x):
    return x * _sigmoid(x)


def _sc_mesh():
    return plsc.VectorSubcoreMesh(core_axis_name="c", subcore_axis_name="s")


def _sc_worker_id():
    return lax.axis_index("s") * SC_CORES + lax.axis_index("c")


def _dispatch_rows(xpk, pos_chunks, n_rows):
    t = xpk.shape[0]
    n_ch = t // SC_WORKERS // SC_CHUNK

    @functools.partial(
        pl.kernel, mesh=_sc_mesh(),
        out_type=jax.ShapeDtypeStruct((n_rows, PACKED), jnp.int32),
        scratch_types=[pltpu.VMEM((TOP_K, SC_CHUNK), jnp.int32),
                       pltpu.VMEM((SC_CHUNK, PACKED), jnp.int32),
                       pltpu.SemaphoreType.DMA],
        name="moe_dispatch",
    )
    def scatter(x_hbm, pos_hbm, out_hbm, idx_v, rows_v, sem):
        wid = _sc_worker_id()

        @pl.loop(0, n_ch)
        def _(c):
            chunk = wid * n_ch + c
            off = pl.multiple_of(chunk * SC_CHUNK, SC_CHUNK)
            pltpu.sync_copy(pos_hbm.at[chunk], idx_v)
            pltpu.sync_copy(x_hbm.at[pl.ds(off, SC_CHUNK)], rows_v)
            copies = [pltpu.async_copy(rows_v, out_hbm.at[idx_v.at[k]], sem) for k in range(TOP_K)]
            for cp in copies:
                cp.wait()

    return scatter(xpk, pos_chunks)


def _gather_rows(table, idx):
    n = idx.shape[0]
    per_w = n // SC_WORKERS
    n_ch = per_w // SC_CHUNK

    @functools.partial(
        pl.kernel, mesh=_sc_mesh(),
        out_type=jax.ShapeDtypeStruct((n, PACKED), jnp.int32),
        scratch_types=[pltpu.VMEM((SC_CHUNK,), jnp.int32),
                       pltpu.VMEM((SC_CHUNK, PACKED), jnp.int32),
                       pltpu.SemaphoreType.DMA],
        name="moe_gather",
    )
    def gather(table_hbm, idx_hbm, out_hbm, idx_v, rows_v, sem):
        base = _sc_worker_id() * per_w

        @pl.loop(0, n_ch)
        def _(c):
            off = pl.multiple_of(base + c * SC_CHUNK, SC_CHUNK)
            pltpu.sync_copy(idx_hbm.at[pl.ds(off, SC_CHUNK)], idx_v)
            pltpu.async_copy(table_hbm.at[idx_v], rows_v, sem).wait()
            pltpu.sync_copy(rows_v, out_hbm.at[pl.ds(off, SC_CHUNK)])

    return gather(table, idx)


def _unpacked_bf16(p):
    hi, lo = _unpack_bf16_pairs(p)
    return jnp.concatenate([hi.astype(BF16), lo.astype(BF16)], axis=1)


def _expert_kernel(be_ref, nu_ref, x_ref, wg_ref, wu_ref, wd_ref, y_ref, g16_ref, u16_ref, d16_ref):
    i = pl.program_id(0)
    changed = jnp.logical_or(i == 0, be_ref[i] != be_ref[jnp.maximum(i - 1, 0)])

    @pl.when(changed)
    def _():
        g16_ref[...] = wg_ref[0, 0].astype(BF16)
        u16_ref[...] = wu_ref[0, 0].astype(BF16)
        d16_ref[...] = wd_ref[0, 0].astype(BF16)

    @pl.when(i < nu_ref[0])
    def _():
        x = _unpacked_bf16(x_ref[...])
        hh = _silu(_dot(x, g16_ref[...])) * _dot(x, u16_ref[...])
        y_ref[...] = _pack_bf16_pairs(_dot(hh.astype(BF16), d16_ref[...]))


def _experts(block_e, n_used, xs, wg, wu, wd, layer):
    nb = block_e.shape[0]
    bm = MOE_BLOCK
    grid_spec = pltpu.PrefetchScalarGridSpec(
        num_scalar_prefetch=2,
        grid=(nb,),
        in_specs=[pl.BlockSpec((bm, PACKED), lambda i, be, nu: (i, 0)),
                  pl.BlockSpec((1, 1, D_MODEL, D_EXPERT), lambda i, be, nu: (layer, be[i], 0, 0)),
                  pl.BlockSpec((1, 1, D_MODEL, D_EXPERT), lambda i, be, nu: (layer, be[i], 0, 0)),
                  pl.BlockSpec((1, 1, D_EXPERT, D_MODEL), lambda i, be, nu: (layer, be[i], 0, 0))],
        out_specs=pl.BlockSpec((bm, PACKED), lambda i, be, nu: (i, 0)),
        scratch_shapes=[pltpu.VMEM((D_MODEL, D_EXPERT), BF16), pltpu.VMEM((D_MODEL, D_EXPERT), BF16),
                        pltpu.VMEM((D_EXPERT, D_MODEL), BF16)],
    )
    return pl.pallas_call(
        _expert_kernel,
        grid_spec=grid_spec,
        out_shape=jax.ShapeDtypeStruct((nb * bm, PACKED), jnp.int32),
        compiler_params=_params("arbitrary"),
        name="moe_experts",
    )(block_e, n_used, xs, wg, wu, wd)


def _combine_kernel(g_ref, wk_ref, h_ref, xpk_ref, sg_ref, su_ref, sd_ref, lw_ref, lb_ref, o_ref):
    x = _unpacked_bf16(xpk_ref[...])
    hs = _silu(_dot(x, sg_ref[...])) * _dot(x, su_ref[...])
    shared = _dot(hs.astype(BF16), sd_ref[...])
    acc_hi = shared[:, :PACKED]
    acc_lo = shared[:, PACKED:]
    wk = wk_ref[...]
    for k in range(TOP_K):
        y_hi, y_lo = _unpack_bf16_pairs(g_ref[k])
        w = wk[:, k:k + 1]
        acc_hi = acc_hi + w * y_hi
        acc_lo = acc_lo + w * y_lo
    ffn = jnp.concatenate([acc_hi, acc_lo], axis=1)
    o_ref[...] = _layer_norm(ALPHA * h_ref[...] + ffn, lw_ref[...], lb_ref[...])


def _combine_ln(g, wk, h, xpk, sg, su, sd, ln_w, ln_b):
    t = h.shape[0]
    tm = COMBINE_TILE
    const = lambda shape: pl.BlockSpec(shape, lambda i: (0, 0))
    return pl.pallas_call(
        _combine_kernel,
        grid=(t // tm,),
        in_specs=[pl.BlockSpec((TOP_K, tm, PACKED), lambda i: (0, i, 0)),
                  pl.BlockSpec((tm, TOP_K), lambda i: (i, 0)),
                  pl.BlockSpec((tm, D_MODEL), lambda i: (i, 0)),
                  pl.BlockSpec((tm, PACKED), lambda i: (i, 0)),
                  const((D_MODEL, D_EXPERT)), const((D_MODEL, D_EXPERT)), const((D_EXPERT, D_MODEL)),
                  const((1, D_MODEL)), const((1, D_MODEL))],
        out_specs=pl.BlockSpec((tm, D_MODEL), lambda i: (i, 0)),
        out_shape=jax.ShapeDtypeStruct((t, D_MODEL), F32),
        compiler_params=_params("parallel"),
        name="moe_combine_ln",
    )(g, wk, h, xpk, sg.astype(BF16), su.astype(BF16), sd.astype(BF16),
      ln_w.reshape(1, D_MODEL), ln_b.reshape(1, D_MODEL))


def _moe_ln(h, hpk, router_w, router_bias, wg, wu, wd, layer, sg, su, sd, ln_w, ln_b):
    t = h.shape[0]
    idx, rank, wk, counts = _router(h, router_w, router_bias)
    cnt = counts[:, 0].astype(jnp.int32)
    padded = (cnt + MOE_BLOCK - 1) // MOE_BLOCK * MOE_BLOCK
    pend = jnp.cumsum(padded)
    experts = jnp.arange(N_EXPERTS, dtype=jnp.int32)
    pstart_of_pick = jnp.sum(jnp.where(idx[:, :, None] == experts, pend - padded, 0), axis=-1)
    pos = pstart_of_pick + rank
    nb = -(-(t * TOP_K + N_EXPERTS * (MOE_BLOCK - 1)) // MOE_BLOCK)
    starts = jnp.arange(nb, dtype=jnp.int32) * MOE_BLOCK
    block_e = jnp.minimum(jnp.sum((pend[None, :] <= starts[:, None]).astype(jnp.int32), axis=1), N_EXPERTS - 1)
    n_used = (pend[-1] // MOE_BLOCK).astype(jnp.int32).reshape(1)
    pos_chunks = pos.reshape(TOP_K, t // SC_CHUNK, SC_CHUNK).transpose(1, 0, 2)
    xs = _dispatch_rows(hpk, pos_chunks, nb * MOE_BLOCK)
    ys = _experts(block_e, n_used, xs, wg, wu, wd, layer)
    g = _gather_rows(ys, pos.reshape(-1)).reshape(TOP_K, t, PACKED)
    return _combine_ln(g, wk.T, h, hpk, sg, su, sd, ln_w, ln_b)


def _pad_cols(w, width=LANES):
    return jnp.pad(w, ((0, 0), (0, width - w.shape[1])))


def _even_mixer(h, batch, seq, w_in, gate_b, norm_w, conv_w, conv_b, wa, ba, wx, bx, lam):
    a4 = 4 * MIX_HALF
    ng = 2 * HEADS
    w_main = jnp.concatenate([w_in[:, :a4], w_in[:, a4 + ng:]], axis=1).astype(BF16)
    w_gate = _pad_cols(w_in[:, a4:a4 + ng]).astype(BF16)
    proj, gates = _proj(h, w_main, w_gate)
    ya = _mlstm(proj, gates, gate_b, norm_w, batch, seq)
    yb = _rglru(proj, conv_w, conv_b, wa, ba, wx, bx, lam, batch, seq)
    return ya, yb


def _odd_mixer(h, batch, seq, w_in, lam_re, lam_im, b_re, b_im, c_re, c_im, d_skip, log_dt,
               glu_w, glu_b, gate_w, gate_b, norm_w):
    c0 = MIX_HALF
    c1 = c0 + HEADS * GLA_DK
    c2 = c1 + HEADS * GLA_DK
    c3 = c2 + MIX_HALF
    c4 = c3 + MIX_HALF
    w_main = jnp.concatenate([w_in[:, :c0], _pad_heads(w_in[:, c0:c1], 1), _pad_heads(w_in[:, c1:c2], 1),
                              w_in[:, c2:c4]], axis=1).astype(BF16)
    w_low = _pad_cols(w_in[:, c4:]).astype(BF16)
    proj, glow = _proj(h, w_main, w_low)
    tables = _s5_tables(lam_re, lam_im, b_re, b_im, c_re, c_im, log_dt, S5_SUB)
    yc = _s5(proj, tables, d_skip, glu_w, glu_b, batch, seq)
    gw = jnp.pad(_pad_heads(gate_w, 1), ((0, LANES - GLA_GATE_RANK), (0, 0))).astype(BF16)
    gb = _pad_heads(gate_b.reshape(1, -1), 1)
    yd = _gla(proj, glow, gw, gb, norm_w, batch, seq)
    return yc, yd


def kernel(x, ln1_w, ln1_b, ln2_w, ln2_b, w_out, w_in_even, mlstm_gate_b, mlstm_norm_w, lru_conv_w, lru_conv_b, lru_wa, lru_ba, lru_wx, lru_bx, lru_lambda, w_in_odd, s5_lam_re, s5_lam_im, s5_b_re, s5_b_im, s5_c_re, s5_c_im, s5_d, s5_log_dt, s5_glu_w, s5_glu_b, gla_gate_w, gla_gate_b, gla_norm_w, router_w, router_bias, exp_w_gate, exp_w_up, exp_w_down, sh_w_gate, sh_w_up, sh_w_down):
    batch, seq, d = x.shape
    h = x.reshape(batch * seq, d)
    for layer in range(DEPTH):
        j = layer // 2
        if layer % 2 == 0:
            y1, y2 = _even_mixer(h, batch, seq, w_in_even[j], mlstm_gate_b[j], mlstm_norm_w[j],
                                 lru_conv_w[j], lru_conv_b[j], lru_wa[j], lru_ba[j], lru_wx[j],
                                 lru_bx[j], lru_lambda[j])
        else:
            y1, y2 = _odd_mixer(h, batch, seq, w_in_odd[j], s5_lam_re[j], s5_lam_im[j], s5_b_re[j],
                                s5_b_im[j], s5_c_re[j], s5_c_im[j], s5_d[j], s5_log_dt[j],
                                s5_glu_w[j], s5_glu_b[j], gla_gate_w[j], gla_gate_b[j], gla_norm_w[j])
        h, hpk = _out_proj_ln(y1, y2, h, w_out[layer], ln1_w[layer], ln1_b[layer])
        h = _moe_ln(h, hpk, router_w[layer], router_bias[layer], exp_w_gate, exp_w_up, exp_w_down, layer,
                    sh_w_gate[layer], sh_w_up[layer], sh_w_down[layer], ln2_w[layer], ln2_b[layer])
    return h.reshape(batch, seq, d)
```

```python
import functools
import math

import jax
import jax.numpy as jnp
from jax import lax
from jax.experimental import pallas as pl
from jax.experimental.pallas import tpu as pltpu
from jax.experimental.pallas import tpu_sc as plsc

F32 = jnp.float32
BF16 = jnp.bfloat16

D_MODEL = 1024
DEPTH = 2
MIX_HALF = 512
HEADS = 4
HEAD_DIM = 128
GLA_DK = 64
GLA_CHUNK = 64
GLA_GATE_RANK = 16
GLA_GATE_TEMP = 16.0
LRU_C = 8.0
LRU_CONV = 4
S5_GROUP = 16
S5_GROUPS = 32
S5_STATE = 64
S5_LANES = S5_GROUPS * S5_STATE
S5_BLOCKS = 4
N_EXPERTS = 64
N_GROUPS = 8
GROUP_SIZE = N_EXPERTS // N_GROUPS
TOP_K = 8
TOPK_GROUPS = 4
D_EXPERT = 256
ROUTED_SCALE = 2.5
ALPHA = (2.0 * DEPTH) ** 0.25
EPS = 1e-5
LANES = 128
NEG_INF = float("-inf")

VMEM_LIMIT = 56 * 1024 * 1024

MLSTM_CHUNK = 128
MLSTM_TILE = 1024
LRU_TILE = 256
S5_TILE = 512
S5_SUB = 64
GLA_TILE = 1024
PROJ_TILE = 512
OUT_TILE = 512
ROUTER_TILE = 512
MOE_BLOCK = 512
COMBINE_TILE = 256
PACKED = D_MODEL // 2
SC_CHUNK = 64
SC_CORES = 2
SC_SUBCORES = 16
SC_WORKERS = SC_CORES * SC_SUBCORES


def _params(*sem):
    return pltpu.CompilerParams(dimension_semantics=sem, vmem_limit_bytes=VMEM_LIMIT)


def _split3(x):
    hi = x.astype(BF16)
    r1 = x - hi.astype(F32)
    mid = r1.astype(BF16)
    lo = (r1 - mid.astype(F32)).astype(BF16)
    return hi, mid, lo


def _dot(a, b):
    return jnp.dot(a, b, preferred_element_type=F32)


def _dot_nt(a, b):
    return lax.dot_general(a, b, (((1,), (1,)), ((), ())), preferred_element_type=F32)


def _dot_tn(a, b):
    return lax.dot_general(a, b, (((0,), (0,)), ((), ())), preferred_element_type=F32)


def _exact_left01(mask01_bf16, x):
    hi, mid, lo = _split3(x)
    return _dot(mask01_bf16, hi) + _dot(mask01_bf16, mid) + _dot(mask01_bf16, lo)


def _exact_right01(x, mask01_bf16):
    hi, mid, lo = _split3(x)
    return _dot(hi, mask01_bf16) + _dot(mid, mask01_bf16) + _dot(lo, mask01_bf16)


def _log_sigmoid(x):
    return jnp.minimum(x, 0.0) - jnp.log(1.0 + jnp.exp(-jnp.abs(x)))


def _sigmoid(x):
    return 1.0 / (1.0 + jnp.exp(-x))


def _gelu_tanh(x):
    c = math.sqrt(2.0 / math.pi)
    return 0.5 * x * (1.0 + jnp.tanh(c * (x + 0.044715 * (x * x * x))))


def _layer_norm(z, w, b):
    mu = jnp.mean(z, axis=-1, keepdims=True)
    zc = z - mu
    return zc * lax.rsqrt(jnp.mean(zc * zc, axis=-1, keepdims=True) + EPS) * w + b


def _shift_rows(x, s, fill):
    rows = lax.broadcasted_iota(jnp.int32, x.shape, 0)
    return jnp.where(rows < s, fill, pltpu.roll(x, s, 0))


def _proj_kernel(x_ref, w_ref, wg_ref, o_ref, og_ref):
    x = x_ref[...].astype(BF16)
    o_ref[...] = _dot(x, w_ref[...])
    og_ref[...] = _dot(x, wg_ref[...])


def _proj(x, w_main, w_small):
    t, d = x.shape
    n = w_main.shape[1]
    tm = PROJ_TILE
    return pl.pallas_call(
        _proj_kernel,
        grid=(t // tm,),
        in_specs=[pl.BlockSpec((tm, d), lambda i: (i, 0)),
                  pl.BlockSpec((d, n), lambda i: (0, 0)),
                  pl.BlockSpec((d, LANES), lambda i: (0, 0))],
        out_specs=[pl.BlockSpec((tm, n), lambda i: (i, 0)),
                   pl.BlockSpec((tm, LANES), lambda i: (i, 0))],
        out_shape=[jax.ShapeDtypeStruct((t, n), F32), jax.ShapeDtypeStruct((t, LANES), F32)],
        compiler_params=_params("parallel"),
        name="in_proj",
    )(x, w_main, w_small)


def _mlstm_kernel(q_ref, k_ref, v_ref, o_ref, gc_ref, gr_ref, bc_ref, br_ref, nw_ref,
                  y_ref, c_ref, m_ref, *, chunk, n_chunks):
    L = chunk

    @pl.when(pl.program_id(1) == 0)
    def _():
        c_ref[...] = jnp.zeros_like(c_ref)
        m_ref[...] = jnp.zeros_like(m_ref)

    ri = lax.broadcasted_iota(jnp.int32, (L, L), 0)
    ci = lax.broadcasted_iota(jnp.int32, (L, L), 1)
    causal = ci <= ri
    tril = causal.astype(BF16)
    triu = (ri <= ci).astype(BF16)
    ones_v = jnp.ones((L, HEAD_DIM), BF16)
    scale = HEAD_DIM ** -0.5

    def body(c, carry):
        r0 = pl.multiple_of(c * L, L)
        g_col = gc_ref[pl.ds(r0, L), :] + bc_ref[...]
        g_row = gr_ref[c] + br_ref[...]
        b_col_all = _exact_left01(tril, _log_sigmoid(g_col))
        b_row_all = _exact_right01(_log_sigmoid(g_row), triu)
        for h in range(HEADS):
            lo = h * HEAD_DIM
            q = q_ref[pl.ds(r0, L), lo:lo + HEAD_DIM].astype(BF16)
            k = k_ref[pl.ds(r0, L), lo:lo + HEAD_DIM] * scale
            v = v_ref[pl.ds(r0, L), lo:lo + HEAD_DIM].astype(BF16)
            v_aug = jnp.concatenate([v, ones_v], axis=1)
            i_col = g_col[:, h:h + 1]
            i_row = g_row[h:h + 1, :]
            b_col = b_col_all[:, HEADS + h:HEADS + h + 1]
            b_row = b_row_all[HEADS + h:HEADS + h + 1, :]
            b_last = b_col[L - 1:L, :]
            m_prev = m_ref[h:h + 1, 0:1]
            c_prev = c_ref[h]

            d_mat = jnp.where(causal, b_col - b_row + i_row, NEG_INF)
            m_inter = b_col + m_prev
            m_i = jnp.maximum(m_inter, jnp.max(d_mat, axis=1, keepdims=True))
            s = _dot_nt(q, k.astype(BF16)) * jnp.exp(d_mat - m_i)
            w_inter = jnp.exp(m_inter - m_i)
            both = _dot(s.astype(BF16), v_aug) + w_inter * _dot(q, c_prev.astype(BF16))
            num = both[:, :HEAD_DIM]
            den = both[:, HEAD_DIM:]
            hh = num / jnp.maximum(jnp.abs(den), jnp.exp(-m_i))

            w_loc = b_last - b_col + i_col
            m_loc = jnp.max(w_loc, axis=0, keepdims=True)
            kp = (k * jnp.exp(w_loc - m_loc)).astype(BF16)
            c_loc = _dot_tn(kp, v_aug)
            m_new = jnp.maximum(b_last + m_prev, m_loc)
            c_ref[h] = jnp.exp(b_last + m_prev - m_new) * c_prev + jnp.exp(m_loc - m_new) * c_loc
            m_ref[h:h + 1, :] = jnp.broadcast_to(m_new, (1, LANES))

            hc = hh - jnp.mean(hh, axis=-1, keepdims=True)
            yn = hc * lax.rsqrt(jnp.mean(hc * hc, axis=-1, keepdims=True) + EPS)
            og = o_ref[pl.ds(r0, L), lo:lo + HEAD_DIM]
            y_ref[pl.ds(r0, L), lo:lo + HEAD_DIM] = yn * nw_ref[:, lo:lo + HEAD_DIM] * _sigmoid(og)
        return carry

    lax.fori_loop(0, n_chunks, body, 0)


def _mlstm(proj, gates, gate_b, norm_w, batch, seq):
    t = batch * seq
    L = MLSTM_CHUNK
    ts = MLSTM_TILE
    nj = seq // ts
    nc = ts // L
    g_row = gates[:, :2 * HEADS].reshape(t // L, L, 2 * HEADS).transpose(0, 2, 1)
    b_col = jnp.zeros((1, LANES), F32).at[0, :2 * HEADS].set(gate_b)
    b_row = gate_b.reshape(2 * HEADS, 1)
    blk = lambda col: pl.BlockSpec((ts, MIX_HALF), lambda b, j, col=col: (b * nj + j, col))
    kern = functools.partial(_mlstm_kernel, chunk=L, n_chunks=nc)
    return pl.pallas_call(
        kern,
        grid=(batch, nj),
        in_specs=[blk(0), blk(1), blk(2), blk(3),
                  pl.BlockSpec((ts, LANES), lambda b, j: (b * nj + j, 0)),
                  pl.BlockSpec((nc, 2 * HEADS, L), lambda b, j: (b * nj + j, 0, 0)),
                  pl.BlockSpec((1, LANES), lambda b, j: (0, 0)),
                  pl.BlockSpec((2 * HEADS, 1), lambda b, j: (0, 0)),
                  pl.BlockSpec((1, MIX_HALF), lambda b, j: (0, 0))],
        out_specs=pl.BlockSpec((ts, MIX_HALF), lambda b, j: (b * nj + j, 0)),
        out_shape=jax.ShapeDtypeStruct((t, MIX_HALF), F32),
        scratch_shapes=[pltpu.VMEM((HEADS, HEAD_DIM, 2 * HEAD_DIM), F32),
                        pltpu.VMEM((8, LANES), F32)],
        compiler_params=_params("arbitrary", "arbitrary"),
        name="mlstm",
    )(proj, proj, proj, proj, gates, g_row, b_col, b_row, norm_w.reshape(1, MIX_HALF))


def _rglru_kernel(xb_ref, gb_ref, cw_ref, cb_ref, wa_ref, ba_ref, wx_ref, bx_ref, lam_ref,
                  y_ref, xext_ref, h_ref, *, tile):
    @pl.when(pl.program_id(1) == 0)
    def _():
        xext_ref[0:8, :] = jnp.zeros((8, MIX_HALF), F32)
        h_ref[...] = jnp.zeros_like(h_ref)

    x = xb_ref[...]
    xext_ref[8:8 + tile, :] = x
    xc = cb_ref[...] + cw_ref[LRU_CONV - 1:LRU_CONV, :] * x
    for tap in range(LRU_CONV - 1):
        back = LRU_CONV - 1 - tap
        xc = xc + cw_ref[tap:tap + 1, :] * xext_ref[8 - back:8 - back + tile, :]
    xext_ref[0:8, :] = x[tile - 8:tile, :]

    xc16 = xc.astype(BF16)
    r_parts, i_parts = [], []
    for h in range(HEADS):
        lo = h * HEAD_DIM
        xh = xc16[:, lo:lo + HEAD_DIM]
        r_parts.append(_dot(xh, wa_ref[h]))
        i_parts.append(_dot(xh, wx_ref[h]))
    r = _sigmoid(jnp.concatenate(r_parts, axis=1) + ba_ref[...])
    ig = _sigmoid(jnp.concatenate(i_parts, axis=1) + bx_ref[...])
    lam = lam_ref[...]
    softplus_neg = jnp.maximum(-lam, 0.0) + jnp.log(1.0 + jnp.exp(-jnp.abs(lam)))
    log_a = -LRU_C * r * softplus_neg
    a = jnp.exp(log_a)
    th = jnp.tanh(log_a)
    u = jnp.sqrt(-2.0 * th / (1.0 - th)) * ig * xc

    s = 1
    while s < tile:
        u = a * _shift_rows(u, s, 0.0) + u
        a = a * _shift_rows(a, s, 1.0)
        s *= 2
    hcur = u + a * h_ref[0:1, :]
    h_ref[...] = jnp.broadcast_to(hcur[tile - 1:tile, :], h_ref.shape)
    y_ref[...] = hcur * _gelu_tanh(gb_ref[...])


def _rglru(proj, conv_w, conv_b, wa, ba, wx, bx, lam, batch, seq):
    t = batch * seq
    ts = LRU_TILE
    nj = seq // ts
    row = lambda a: a.reshape(1, MIX_HALF)
    const2 = lambda shape: pl.BlockSpec(shape, lambda b, j: (0, 0))
    const3 = lambda shape: pl.BlockSpec(shape, lambda b, j: (0, 0, 0))
    blk = lambda col: pl.BlockSpec((ts, MIX_HALF), lambda b, j, col=col: (b * nj + j, col))
    return pl.pallas_call(
        functools.partial(_rglru_kernel, tile=ts),
        grid=(batch, nj),
        in_specs=[blk(4), blk(5), const2((LRU_CONV, MIX_HALF)), const2((1, MIX_HALF)),
                  const3((HEADS, HEAD_DIM, HEAD_DIM)), const2((1, MIX_HALF)),
                  const3((HEADS, HEAD_DIM, HEAD_DIM)), const2((1, MIX_HALF)), const2((1, MIX_HALF))],
        out_specs=pl.BlockSpec((ts, MIX_HALF), lambda b, j: (b * nj + j, 0)),
        out_shape=jax.ShapeDtypeStruct((t, MIX_HALF), F32),
        scratch_shapes=[pltpu.VMEM((ts + 8, MIX_HALF), F32), pltpu.VMEM((8, MIX_HALF), F32)],
        compiler_params=_params("arbitrary", "arbitrary"),
        name="rglru",
    )(proj, proj, conv_w, row(conv_b), wa.astype(BF16), row(ba), wx.astype(BF16), row(bx), row(lam))


def _s5_kernel(u_ref, bre_ref, bim_ref, cre_ref, cim_ref, are_ref, aim_ref, d_ref, gw_ref, gb_ref,
               y_ref, xr_ref, xi_ref, cr_ref, ci_ref, *, tile, sub):
    @pl.when(pl.program_id(1) == 0)
    def _():
        cr_ref[...] = jnp.zeros_like(cr_ref)
        ci_ref[...] = jnp.zeros_like(ci_ref)

    u = u_ref[...]
    u16 = u.astype(BF16)
    blk_c = MIX_HALF // S5_BLOCKS
    blk_s = S5_LANES // S5_BLOCKS
    for j in range(S5_BLOCKS):
        uj = u16[:, j * blk_c:(j + 1) * blk_c]
        xr_ref[:, j * blk_s:(j + 1) * blk_s] = _dot(uj, bre_ref[j])
        xi_ref[:, j * blk_s:(j + 1) * blk_s] = _dot(uj, bim_ref[j])

    n_steps = int(math.log2(sub))
    rows = lax.broadcasted_iota(jnp.int32, (sub, S5_LANES), 0)

    def seg(c, carry):
        r0 = pl.multiple_of(c * sub, sub)
        xr = xr_ref[pl.ds(r0, sub), :]
        xi = xi_ref[pl.ds(r0, sub), :]
        a1r = are_ref[0:1, :]
        a1i = aim_ref[0:1, :]
        pr = cr_ref[0:1, :]
        pi = ci_ref[0:1, :]
        first = rows == 0
        xr = xr + jnp.where(first, a1r * pr - a1i * pi, 0.0)
        xi = xi + jnp.where(first, a1r * pi + a1i * pr, 0.0)
        for kk in range(n_steps):
            sft = 1 << kk
            ar = are_ref[kk:kk + 1, :]
            ai = aim_ref[kk:kk + 1, :]
            sr = jnp.where(rows < sft, 0.0, pltpu.roll(xr, sft, 0))
            si = jnp.where(rows < sft, 0.0, pltpu.roll(xi, sft, 0))
            xr, xi = xr + ar * sr - ai * si, xi + ar * si + ai * sr
        xr_ref[pl.ds(r0, sub), :] = xr
        xi_ref[pl.ds(r0, sub), :] = xi
        cr_ref[...] = jnp.broadcast_to(xr[sub - 1:sub, :], cr_ref.shape)
        ci_ref[...] = jnp.broadcast_to(xi[sub - 1:sub, :], ci_ref.shape)
        return carry

    lax.fori_loop(0, tile // sub, seg, 0)

    parts = []
    for j in range(S5_BLOCKS):
        xrj = xr_ref[:, j * blk_s:(j + 1) * blk_s].astype(BF16)
        xij = xi_ref[:, j * blk_s:(j + 1) * blk_s].astype(BF16)
        parts.append(_dot(xrj, cre_ref[j]) - _dot(xij, cim_ref[j]))
    y = jnp.concatenate(parts, axis=1) + d_ref[...] * u
    g = _gelu_tanh(y)
    y_ref[...] = g * _sigmoid(_dot(g.astype(BF16), gw_ref[...]) + gb_ref[...])


def _s5_tables(lam_re, lam_im, b_re, b_im, c_re, c_im, log_dt, sub):
    lr, li = lam_re.astype(F32), lam_im.astype(F32)
    dt = jnp.exp(log_dt.astype(F32))[:, None]
    mag = jnp.exp(lr * dt)
    abar_re = mag * jnp.cos(li * dt)
    abar_im = mag * jnp.sin(li * dt)
    den = lr * lr + li * li
    nr = abar_re - 1.0
    coef_re = (nr * lr + abar_im * li) / den
    coef_im = (abar_im * lr - nr * li) / den
    bbar_re = coef_re[..., None] * b_re - coef_im[..., None] * b_im
    bbar_im = coef_re[..., None] * b_im + coef_im[..., None] * b_re
    gpb = S5_GROUPS // S5_BLOCKS
    eye = jnp.eye(gpb, dtype=F32)

    def in_map(bb):
        bb = bb.reshape(S5_BLOCKS, gpb, S5_STATE, S5_GROUP)
        return jnp.einsum("jgph,gk->jghkp", bb, eye).reshape(S5_BLOCKS, gpb * S5_GROUP, gpb * S5_STATE)

    def out_map(cc):
        cc = cc.reshape(S5_BLOCKS, gpb, S5_GROUP, S5_STATE)
        return jnp.einsum("jghp,gk->jgpkh", cc, eye).reshape(S5_BLOCKS, gpb * S5_STATE, gpb * S5_GROUP)

    n_steps = int(math.log2(sub))
    pw = (2.0 ** jnp.arange(n_steps, dtype=F32))[:, None, None]
    pmag = jnp.exp(pw * (lr * dt)[None])
    a_re = (pmag * jnp.cos(pw * (li * dt)[None])).reshape(n_steps, S5_LANES)
    a_im = (pmag * jnp.sin(pw * (li * dt)[None])).reshape(n_steps, S5_LANES)
    pad = (-n_steps) % 8
    a_re = jnp.pad(a_re, ((0, pad), (0, 0)))
    a_im = jnp.pad(a_im, ((0, pad), (0, 0)))
    return (in_map(bbar_re).astype(BF16), in_map(bbar_im).astype(BF16),
            out_map(c_re.astype(F32)).astype(BF16), out_map(c_im.astype(F32)).astype(BF16), a_re, a_im)


def _s5(proj, tables, d_skip, glu_w, glu_b, batch, seq):
    t = batch * seq
    ts = S5_TILE
    nj = seq // ts
    bre, bim, cre, cim, a_re, a_im = tables
    blk_c = MIX_HALF // S5_BLOCKS
    blk_s = S5_LANES // S5_BLOCKS
    const2 = lambda shape: pl.BlockSpec(shape, lambda b, j: (0, 0))
    const3 = lambda shape: pl.BlockSpec(shape, lambda b, j: (0, 0, 0))
    return pl.pallas_call(
        functools.partial(_s5_kernel, tile=ts, sub=S5_SUB),
        grid=(batch, nj),
        in_specs=[pl.BlockSpec((ts, MIX_HALF), lambda b, j: (b * nj + j, 0)),
                  const3((S5_BLOCKS, blk_c, blk_s)), const3((S5_BLOCKS, blk_c, blk_s)),
                  const3((S5_BLOCKS, blk_s, blk_c)), const3((S5_BLOCKS, blk_s, blk_c)),
                  const2(a_re.shape), const2(a_im.shape),
                  const2((1, MIX_HALF)), const2((MIX_HALF, MIX_HALF)), const2((1, MIX_HALF))],
        out_specs=pl.BlockSpec((ts, MIX_HALF), lambda b, j: (b * nj + j, 0)),
        out_shape=jax.ShapeDtypeStruct((t, MIX_HALF), F32),
        scratch_shapes=[pltpu.VMEM((ts, S5_LANES), F32), pltpu.VMEM((ts, S5_LANES), F32),
                        pltpu.VMEM((8, S5_LANES), F32), pltpu.VMEM((8, S5_LANES), F32)],
        compiler_params=_params("arbitrary", "arbitrary"),
        name="s5",
    )(proj, bre, bim, cre, cim, a_re, a_im, d_skip.reshape(1, MIX_HALF), glu_w.astype(BF16),
      glu_b.reshape(1, MIX_HALF))


def _gla_kernel(q_ref, k_ref, v_ref, r_ref, gl_ref, gw_ref, gb_ref, nw_ref, y_ref, st_ref,
                *, chunk, n_chunks):
    L = chunk

    @pl.when(pl.program_id(1) == 0)
    def _():
        st_ref[...] = jnp.zeros_like(st_ref)

    ri = lax.broadcasted_iota(jnp.int32, (L, L), 0)
    ci = lax.broadcasted_iota(jnp.int32, (L, L), 1)
    causal = ci <= ri
    tril = causal.astype(BF16)
    scale = GLA_DK ** -0.5

    def body(c, carry):
        r0 = pl.multiple_of(c * L, L)
        z = _dot(gl_ref[pl.ds(r0, L), :].astype(BF16), gw_ref[...]) + gb_ref[...]
        log_alpha = _log_sigmoid(z) * (1.0 / GLA_GATE_TEMP)
        bcum_all = _exact_left01(tril, log_alpha)
        for h in range(HEADS):
            lo = h * HEAD_DIM
            bcum = bcum_all[:, lo:lo + HEAD_DIM]
            b_last = bcum[L - 1:L, :]
            q = q_ref[pl.ds(r0, L), lo:lo + HEAD_DIM] * scale
            k = k_ref[pl.ds(r0, L), lo:lo + HEAD_DIM]
            v = v_ref[pl.ds(r0, L), lo:lo + HEAD_DIM].astype(BF16)
            q_dec = (q * jnp.exp(bcum)).astype(BF16)
            k_inv = (k * jnp.exp(-bcum)).astype(BF16)
            k_end = (k * jnp.exp(b_last - bcum)).astype(BF16)
            st = st_ref[h]
            att = jnp.where(causal, _dot_nt(q_dec, k_inv), 0.0)
            o = _dot(att.astype(BF16), v) + _dot_nt(q_dec, st.astype(BF16))
            st_ref[h] = jnp.exp(b_last) * st + _dot_tn(v, k_end)
            yn = o * lax.rsqrt(jnp.mean(o * o, axis=-1, keepdims=True) + EPS)
            rg = r_ref[pl.ds(r0, L), lo:lo + HEAD_DIM]
            y_ref[pl.ds(r0, L), lo:lo + HEAD_DIM] = yn * nw_ref[:, lo:lo + HEAD_DIM] * (rg * _sigmoid(rg))
        return carry

    lax.fori_loop(0, n_chunks, body, 0)


def _gla(proj, glow, gate_w, gate_b, norm_w, batch, seq):
    t = batch * seq
    ts = GLA_TILE
    nj = seq // ts
    blk = lambda col: pl.BlockSpec((ts, MIX_HALF), lambda b, j, col=col: (b * nj + j, col))
    const2 = lambda shape: pl.BlockSpec(shape, lambda b, j: (0, 0))
    return pl.pallas_call(
        functools.partial(_gla_kernel, chunk=GLA_CHUNK, n_chunks=ts // GLA_CHUNK),
        grid=(batch, nj),
        in_specs=[blk(1), blk(2), blk(3), blk(4),
                  pl.BlockSpec((ts, LANES), lambda b, j: (b * nj + j, 0)),
                  const2((LANES, MIX_HALF)), const2((1, MIX_HALF)), const2((1, MIX_HALF))],
        out_specs=pl.BlockSpec((ts, MIX_HALF), lambda b, j: (b * nj + j, 0)),
        out_shape=jax.ShapeDtypeStruct((t, MIX_HALF), F32),
        scratch_shapes=[pltpu.VMEM((HEADS, HEAD_DIM, HEAD_DIM), F32)],
        compiler_params=_params("arbitrary", "arbitrary"),
        name="gla",
    )(proj, proj, proj, proj, glow, gate_w, gate_b, norm_w.reshape(1, MIX_HALF))


def _pad_heads(w, axis):
    shape = list(w.shape)
    shape[axis:axis + 1] = [HEADS, GLA_DK]
    w = w.reshape(shape)
    pad = [(0, 0)] * w.ndim
    pad[axis + 1] = (0, HEAD_DIM - GLA_DK)
    w = jnp.pad(w, pad)
    shape[axis:axis + 2] = [HEADS * HEAD_DIM]
    return w.reshape(shape)


def _pack_bf16_pairs(z):
    hi = lax.bitcast_convert_type(z[:, :PACKED].astype(BF16).astype(F32), jnp.uint32)
    lo = lax.bitcast_convert_type(z[:, PACKED:].astype(BF16).astype(F32), jnp.uint32)
    word = (hi & jnp.uint32(0xFFFF0000)) | lax.shift_right_logical(lo, jnp.uint32(16))
    return lax.bitcast_convert_type(word, jnp.int32)


def _unpack_bf16_pairs(p):
    word = lax.bitcast_convert_type(p, jnp.uint32)
    hi = lax.bitcast_convert_type(word & jnp.uint32(0xFFFF0000), F32)
    lo = lax.bitcast_convert_type(lax.shift_left(word, jnp.uint32(16)), F32)
    return hi, lo


def _out_kernel(ya_ref, yb_ref, h_ref, w_ref, lw_ref, lb_ref, o_ref, opk_ref):
    mixed = jnp.concatenate([ya_ref[...], yb_ref[...]], axis=1).astype(BF16)
    z = ALPHA * h_ref[...] + _dot(mixed, w_ref[...])
    out = _layer_norm(z, lw_ref[...], lb_ref[...])
    o_ref[...] = out
    opk_ref[...] = _pack_bf16_pairs(out)


def _out_proj_ln(ya, yb, h, w_out, ln_w, ln_b):
    t = h.shape[0]
    tm = OUT_TILE
    const = lambda shape: pl.BlockSpec(shape, lambda i: (0, 0))
    return pl.pallas_call(
        _out_kernel,
        grid=(t // tm,),
        in_specs=[pl.BlockSpec((tm, MIX_HALF), lambda i: (i, 0)),
                  pl.BlockSpec((tm, MIX_HALF), lambda i: (i, 0)),
                  pl.BlockSpec((tm, D_MODEL), lambda i: (i, 0)),
                  const((D_MODEL, D_MODEL)), const((1, D_MODEL)), const((1, D_MODEL))],
        out_specs=[pl.BlockSpec((tm, D_MODEL), lambda i: (i, 0)),
                   pl.BlockSpec((tm, PACKED), lambda i: (i, 0))],
        out_shape=[jax.ShapeDtypeStruct((t, D_MODEL), F32), jax.ShapeDtypeStruct((t, PACKED), jnp.int32)],
        compiler_params=_params("parallel"),
        name="out_proj_ln",
    )(ya, yb, h, w_out.astype(BF16), ln_w.reshape(1, D_MODEL), ln_b.reshape(1, D_MODEL))


def _first_index(hit, idx, big):
    return jnp.min(jnp.where(hit, idx, big), axis=0, keepdims=True)


def _router_kernel(h_ref, w_ref, b_ref, idx_ref, rank_ref, wk_ref, cnt_ref, base_ref, *, tile):
    @pl.when(pl.program_id(0) == 0)
    def _():
        base_ref[...] = jnp.zeros_like(base_ref)

    h_hi, h_mid, _ = _split3(h_ref[...])
    w_hi, w_mid, _ = _split3(w_ref[...])
    logits = _dot_nt(w_hi, h_hi) + _dot_nt(w_hi, h_mid) + _dot_nt(w_mid, h_hi)
    scores = _sigmoid(logits)
    biased = scores + b_ref[...]

    sub = lax.broadcasted_iota(jnp.int32, (GROUP_SIZE, tile), 0)
    grp_rows = []
    for g in range(N_GROUPS):
        xg = biased[g * GROUP_SIZE:(g + 1) * GROUP_SIZE, :]
        m1 = jnp.max(xg, axis=0, keepdims=True)
        i1 = _first_index(xg == m1, sub, GROUP_SIZE)
        m2 = jnp.max(jnp.where(sub == i1, NEG_INF, xg), axis=0, keepdims=True)
        grp_rows.append(m1 + m2)
    gs = jnp.concatenate(grp_rows, axis=0)
    gsel = jnp.zeros((N_GROUPS, tile), F32)
    for _ in range(TOPK_GROUPS):
        mx = jnp.max(gs, axis=0, keepdims=True)
        hit = sub == _first_index(gs == mx, sub, N_GROUPS)
        gsel = jnp.where(hit, 1.0, gsel)
        gs = jnp.where(hit, NEG_INF, gs)
    emask = jnp.concatenate(
        [jnp.broadcast_to(gsel[g:g + 1, :], (GROUP_SIZE, tile)) for g in range(N_GROUPS)], axis=0)

    eidx = lax.broadcasted_iota(jnp.int32, (N_EXPERTS, tile), 0)
    cand = jnp.where(emask > 0.5, biased, NEG_INF)
    sel = jnp.zeros((N_EXPERTS, tile), F32)
    hits, picks = [], []
    for _ in range(TOP_K):
        mx = jnp.max(cand, axis=0, keepdims=True)
        first = _first_index(cand == mx, eidx, N_EXPERTS)
        hit = eidx == first
        hits.append(hit)
        picks.append(first)
        sel = jnp.where(hit, 1.0, sel)
        cand = jnp.where(hit, NEG_INF, cand)
    picked = jnp.where(sel > 0.5, scores, 0.0)
    wts = picked / jnp.sum(picked, axis=0, keepdims=True) * ROUTED_SCALE

    ri = lax.broadcasted_iota(jnp.int32, (tile, tile), 0)
    ci = lax.broadcasted_iota(jnp.int32, (tile, tile), 1)
    before = (ri < ci).astype(BF16)
    prior = _dot(sel.astype(BF16), before) + base_ref[:, 0:1]
    ranks = [jnp.sum(jnp.where(hit, prior, 0.0), axis=0, keepdims=True) for hit in hits]
    wsel = [jnp.sum(jnp.where(hit, wts, 0.0), axis=0, keepdims=True) for hit in hits]
    idx_ref[...] = jnp.concatenate(picks, axis=0)
    rank_ref[...] = jnp.concatenate(ranks, axis=0).astype(jnp.int32)
    wk_ref[...] = jnp.concatenate(wsel, axis=0)
    total = base_ref[...] + jnp.sum(sel, axis=1, keepdims=True)
    base_ref[...] = total
    cnt_ref[...] = total


def _router(h, router_w, router_bias):
    t = h.shape[0]
    tm = ROUTER_TILE
    per_tok = lambda dt: jax.ShapeDtypeStruct((TOP_K, t), dt)
    tok_blk = pl.BlockSpec((TOP_K, tm), lambda i: (0, i))
    return pl.pallas_call(
        functools.partial(_router_kernel, tile=tm),
        grid=(t // tm,),
        in_specs=[pl.BlockSpec((tm, D_MODEL), lambda i: (i, 0)),
                  pl.BlockSpec((N_EXPERTS, D_MODEL), lambda i: (0, 0)),
                  pl.BlockSpec((N_EXPERTS, 1), lambda i: (0, 0))],
        out_specs=[tok_blk, tok_blk, tok_blk, pl.BlockSpec((N_EXPERTS, LANES), lambda i: (0, 0))],
        out_shape=[per_tok(jnp.int32), per_tok(jnp.int32), per_tok(F32),
                   jax.ShapeDtypeStruct((N_EXPERTS, LANES), F32)],
        scratch_shapes=[pltpu.VMEM((N_EXPERTS, LANES), F32)],
        compiler_params=_params("arbitrary"),
        name="router",
    )(h, router_w.T, router_bias.reshape(N_EXPERTS, 1))


def _silu(x):
    return x * _sigmoid(x)


def _sc_mesh():
    return plsc.VectorSubcoreMesh(core_axis_name="c", subcore_axis_name="s")


def _sc_worker_id():
    return lax.axis_index("s") * SC_CORES + lax.axis_index("c")


def _dispatch_rows(xpk, pos_chunks, n_rows):
    t = xpk.shape[0]
    n_ch = t // SC_WORKERS // SC_CHUNK

    @functools.partial(
        pl.kernel, mesh=_sc_mesh(),
        out_type=jax.ShapeDtypeStruct((n_rows, PACKED), jnp.int32),
        scratch_types=[pltpu.VMEM((TOP_K, SC_CHUNK), jnp.int32),
                       pltpu.VMEM((SC_CHUNK, PACKED), jnp.int32),
                       pltpu.SemaphoreType.DMA],
        name="moe_dispatch",
    )
    def scatter(x_hbm, pos_hbm, out_hbm, idx_v, rows_v, sem):
        wid = _sc_worker_id()

        @pl.loop(0, n_ch)
        def _(c):
            chunk = wid * n_ch + c
            off = pl.multiple_of(chunk * SC_CHUNK, SC_CHUNK)
            pltpu.sync_copy(pos_hbm.at[chunk], idx_v)
            pltpu.sync_copy(x_hbm.at[pl.ds(off, SC_CHUNK)], rows_v)
            copies = [pltpu.async_copy(rows_v, out_hbm.at[idx_v.at[k]], sem) for k in range(TOP_K)]
            for cp in copies:
                cp.wait()

    return scatter(xpk, pos_chunks)


def _gather_rows(table, idx):
    n = idx.shape[0]
    per_w = n // SC_WORKERS
    n_ch = per_w // SC_CHUNK

    @functools.partial(
        pl.kernel, mesh=_sc_mesh(),
        out_type=jax.ShapeDtypeStruct((n, PACKED), jnp.int32),
        scratch_types=[pltpu.VMEM((SC_CHUNK,), jnp.int32),
                       pltpu.VMEM((SC_CHUNK, PACKED), jnp.int32),
                       pltpu.SemaphoreType.DMA],
        name="moe_gather",
    )
    def gather(table_hbm, idx_hbm, out_hbm, idx_v, rows_v, sem):
        base = _sc_worker_id() * per_w

        @pl.loop(0, n_ch)
        def _(c):
            off = pl.multiple_of(base + c * SC_CHUNK, SC_CHUNK)
            pltpu.sync_copy(idx_hbm.at[pl.ds(off, SC_CHUNK)], idx_v)
            pltpu.async_copy(table_hbm.at[idx_v], rows_v, sem).wait()
            pltpu.sync_copy(rows_v, out_hbm.at[pl.ds(off, SC_CHUNK)])

    return gather(table, idx)


def _unpacked_bf16(p):
    hi, lo = _unpack_bf16_pairs(p)
    return jnp.concatenate([hi.astype(BF16), lo.astype(BF16)], axis=1)


def _expert_kernel(be_ref, nu_ref, x_ref, wg_ref, wu_ref, wd_ref, y_ref, g16_ref, u16_ref, d16_ref):
    i = pl.program_id(0)
    changed = jnp.logical_or(i == 0, be_ref[i] != be_ref[jnp.maximum(i - 1, 0)])

    @pl.when(changed)
    def _():
        g16_ref[...] = wg_ref[0, 0].astype(BF16)
        u16_ref[...] = wu_ref[0, 0].astype(BF16)
        d16_ref[...] = wd_ref[0, 0].astype(BF16)

    @pl.when(i < nu_ref[0])
    def _():
        x = _unpacked_bf16(x_ref[...])
        hh = _silu(_dot(x, g16_ref[...])) * _dot(x, u16_ref[...])
        y_ref[...] = _pack_bf16_pairs(_dot(hh.astype(BF16), d16_ref[...]))


def _experts(block_e, n_used, xs, wg, wu, wd, layer):
    nb = block_e.shape[0]
    bm = MOE_BLOCK
    grid_spec = pltpu.PrefetchScalarGridSpec(
        num_scalar_prefetch=2,
        grid=(nb,),
        in_specs=[pl.BlockSpec((bm, PACKED), lambda i, be, nu: (i, 0)),
                  pl.BlockSpec((1, 1, D_MODEL, D_EXPERT), lambda i, be, nu: (layer, be[i], 0, 0)),
                  pl.BlockSpec((1, 1, D_MODEL, D_EXPERT), lambda i, be, nu: (layer, be[i], 0, 0)),
                  pl.BlockSpec((1, 1, D_EXPERT, D_MODEL), lambda i, be, nu: (layer, be[i], 0, 0))],
        out_specs=pl.BlockSpec((bm, PACKED), lambda i, be, nu: (i, 0)),
        scratch_shapes=[pltpu.VMEM((D_MODEL, D_EXPERT), BF16), pltpu.VMEM((D_MODEL, D_EXPERT), BF16),
                        pltpu.VMEM((D_EXPERT, D_MODEL), BF16)],
    )
    return pl.pallas_call(
        _expert_kernel,
        grid_spec=grid_spec,
        out_shape=jax.ShapeDtypeStruct((nb * bm, PACKED), jnp.int32),
        compiler_params=_params("arbitrary"),
        name="moe_experts",
    )(block_e, n_used, xs, wg, wu, wd)


def _combine_kernel(g_ref, wk_ref, h_ref, xpk_ref, sg_ref, su_ref, sd_ref, lw_ref, lb_ref, o_ref):
    x = _unpacked_bf16(xpk_ref[...])
    hs = _silu(_dot(x, sg_ref[...])) * _dot(x, su_ref[...])
    shared = _dot(hs.astype(BF16), sd_ref[...])
    acc_hi = shared[:, :PACKED]
    acc_lo = shared[:, PACKED:]
    wk = wk_ref[...]
    for k in range(TOP_K):
        y_hi, y_lo = _unpack_bf16_pairs(g_ref[k])
        w = wk[:, k:k + 1]
        acc_hi = acc_hi + w * y_hi
        acc_lo = acc_lo + w * y_lo
    ffn = jnp.concatenate([acc_hi, acc_lo], axis=1)
    o_ref[...] = _layer_norm(ALPHA * h_ref[...] + ffn, lw_ref[...], lb_ref[...])


def _combine_ln(g, wk, h, xpk, sg, su, sd, ln_w, ln_b):
    t = h.shape[0]
    tm = COMBINE_TILE
    const = lambda shape: pl.BlockSpec(shape, lambda i: (0, 0))
    return pl.pallas_call(
        _combine_kernel,
        grid=(t // tm,),
        in_specs=[pl.BlockSpec((TOP_K, tm, PACKED), lambda i: (0, i, 0)),
                  pl.BlockSpec((tm, TOP_K), lambda i: (i, 0)),
                  pl.BlockSpec((tm, D_MODEL), lambda i: (i, 0)),
                  pl.BlockSpec((tm, PACKED), lambda i: (i, 0)),
                  const((D_MODEL, D_EXPERT)), const((D_MODEL, D_EXPERT)), const((D_EXPERT, D_MODEL)),
                  const((1, D_MODEL)), const((1, D_MODEL))],
        out_specs=pl.BlockSpec((tm, D_MODEL), lambda i: (i, 0)),
        out_shape=jax.ShapeDtypeStruct((t, D_MODEL), F32),
        compiler_params=_params("parallel"),
        name="moe_combine_ln",
    )(g, wk, h, xpk, sg.astype(BF16), su.astype(BF16), sd.astype(BF16),
      ln_w.reshape(1, D_MODEL), ln_b.reshape(1, D_MODEL))


def _moe_ln(h, hpk, router_w, router_bias, wg, wu, wd, layer, sg, su, sd, ln_w, ln_b):
    t = h.shape[0]
    idx, rank, wk, counts = _router(h, router_w, router_bias)
    cnt = counts[:, 0].astype(jnp.int32)
    padded = (cnt + MOE_BLOCK - 1) // MOE_BLOCK * MOE_BLOCK
    pend = jnp.cumsum(padded)
    experts = jnp.arange(N_EXPERTS, dtype=jnp.int32)
    pstart_of_pick = jnp.sum(jnp.where(idx[:, :, None] == experts, pend - padded, 0), axis=-1)
    pos = pstart_of_pick + rank
    nb = -(-(t * TOP_K + N_EXPERTS * (MOE_BLOCK - 1)) // MOE_BLOCK)
    starts = jnp.arange(nb, dtype=jnp.int32) * MOE_BLOCK
    block_e = jnp.minimum(jnp.sum((pend[None, :] <= starts[:, None]).astype(jnp.int32), axis=1), N_EXPERTS - 1)
    n_used = (pend[-1] // MOE_BLOCK).astype(jnp.int32).reshape(1)
    pos_chunks = pos.reshape(TOP_K, t // SC_CHUNK, SC_CHUNK).transpose(1, 0, 2)
    xs = _dispatch_rows(hpk, pos_chunks, nb * MOE_BLOCK)
    ys = _experts(block_e, n_used, xs, wg, wu, wd, layer)
    g = _gather_rows(ys, pos.reshape(-1)).reshape(TOP_K, t, PACKED)
    return _combine_ln(g, wk.T, h, hpk, sg, su, sd, ln_w, ln_b)


def _pad_cols(w, width=LANES):
    return jnp.pad(w, ((0, 0), (0, width - w.shape[1])))


def _even_mixer(h, batch, seq, w_in, gate_b, norm_w, conv_w, conv_b, wa, ba, wx, bx, lam):
    a4 = 4 * MIX_HALF
    ng = 2 * HEADS
    w_main = jnp.concatenate([w_in[:, :a4], w_in[:, a4 + ng:]], axis=1).astype(BF16)
    w_gate = _pad_cols(w_in[:, a4:a4 + ng]).astype(BF16)
    proj, gates = _proj(h, w_main, w_gate)
    ya = _mlstm(proj, gates, gate_b, norm_w, batch, seq)
    yb = _rglru(proj, conv_w, conv_b, wa, ba, wx, bx, lam, batch, seq)
    return ya, yb


def _odd_mixer(h, batch, seq, w_in, lam_re, lam_im, b_re, b_im, c_re, c_im, d_skip, log_dt,
               glu_w, glu_b, gate_w, gate_b, norm_w):
    c0 = MIX_HALF
    c1 = c0 + HEADS * GLA_DK
    c2 = c1 + HEADS * GLA_DK
    c3 = c2 + MIX_HALF
    c4 = c3 + MIX_HALF
    w_main = jnp.concatenate([w_in[:, :c0], _pad_heads(w_in[:, c0:c1], 1), _pad_heads(w_in[:, c1:c2], 1),
                              w_in[:, c2:c4]], axis=1).astype(BF16)
    w_low = _pad_cols(w_in[:, c4:]).astype(BF16)
    proj, glow = _proj(h, w_main, w_low)
    tables = _s5_tables(lam_re, lam_im, b_re, b_im, c_re, c_im, log_dt, S5_SUB)
    yc = _s5(proj, tables, d_skip, glu_w, glu_b, batch, seq)
    gw = jnp.pad(_pad_heads(gate_w, 1), ((0, LANES - GLA_GATE_RANK), (0, 0))).astype(BF16)
    gb = _pad_heads(gate_b.reshape(1, -1), 1)
    yd = _gla(proj, glow, gw, gb, norm_w, batch, seq)
    return yc, yd


def kernel(x, ln1_w, ln1_b, ln2_w, ln2_b, w_out, w_in_even, mlstm_gate_b, mlstm_norm_w, lru_conv_w, lru_conv_b, lru_wa, lru_ba, lru_wx, lru_bx, lru_lambda, w_in_odd, s5_lam_re, s5_lam_im, s5_b_re, s5_b_im, s5_c_re, s5_c_im, s5_d, s5_log_dt, s5_glu_w, s5_glu_b, gla_gate_w, gla_gate_b, gla_norm_w, router_w, router_bias, exp_w_gate, exp_w_up, exp_w_down, sh_w_gate, sh_w_up, sh_w_down):
    batch, seq, d = x.shape
    streams = [x[b] for b in range(batch)]
    for layer in range(DEPTH):
        j = layer // 2
        nxt = []
        for h in streams:
            if layer % 2 == 0:
                y1, y2 = _even_mixer(h, 1, seq, w_in_even[j], mlstm_gate_b[j], mlstm_norm_w[j],
                                     lru_conv_w[j], lru_conv_b[j], lru_wa[j], lru_ba[j], lru_wx[j],
                                     lru_bx[j], lru_lambda[j])
            else:
                y1, y2 = _odd_mixer(h, 1, seq, w_in_odd[j], s5_lam_re[j], s5_lam_im[j], s5_b_re[j],
                                    s5_b_im[j], s5_c_re[j], s5_c_im[j], s5_d[j], s5_log_dt[j],
                                    s5_glu_w[j], s5_glu_b[j], gla_gate_w[j], gla_gate_b[j], gla_norm_w[j])
            h, hpk = _out_proj_ln(y1, y2, h, w_out[layer], ln1_w[layer], ln1_b[layer])
            nxt.append(_moe_ln(h, hpk, router_w[layer], router_bias[layer], exp_w_gate, exp_w_up,
                               exp_w_down, layer, sh_w_gate[layer], sh_w_up[layer], sh_w_down[layer],
                               ln2_w[layer], ln2_b[layer]))
        streams = nxt
    return jnp.stack(streams)
```

```python
import functools
import math

import jax
import jax.numpy as jnp
from jax import lax
from jax.experimental import pallas as pl
from jax.experimental.pallas import tpu as pltpu
from jax.experimental.pallas import tpu_sc as plsc

F32 = jnp.float32
BF16 = jnp.bfloat16

D_MODEL = 1024
DEPTH = 2
MIX_HALF = 512
HEADS = 4
HEAD_DIM = 128
GLA_DK = 64
GLA_CHUNK = 64
GLA_GATE_RANK = 16
GLA_GATE_TEMP = 16.0
LRU_C = 8.0
LRU_CONV = 4
S5_GROUP = 16
S5_GROUPS = 32
S5_STATE = 64
S5_LANES = S5_GROUPS * S5_STATE
S5_BLOCKS = 4
N_EXPERTS = 64
N_GROUPS = 8
GROUP_SIZE = N_EXPERTS // N_GROUPS
TOP_K = 8
TOPK_GROUPS = 4
D_EXPERT = 256
ROUTED_SCALE = 2.5
ALPHA = (2.0 * DEPTH) ** 0.25
EPS = 1e-5
LANES = 128
SUBLANES = 8
NEG_INF = float("-inf")

VMEM_LIMIT = 56 * 1024 * 1024

MLSTM_CHUNK = 128
MLSTM_TILE = 1024
LRU_TILE = 256
S5_TILE = 512
S5_STRIP = 512
S5_LOG_STEPS = 3
S5_UNROLL = 4
GLA_UNROLL = 4
GLA_TILE = 1024
PROJ_TILE = 512
OUT_TILE = 512
ROUTER_TILE = 512
MOE_BLOCK = 1024
COMBINE_TILE = 256
PACKED = D_MODEL // 2
SC_CHUNK = 64
SC_CORES = 2
SC_SUBCORES = 16
SC_WORKERS = SC_CORES * SC_SUBCORES


def _params(*sem):
    return pltpu.CompilerParams(dimension_semantics=sem, vmem_limit_bytes=VMEM_LIMIT)


def _split3(x):
    hi = x.astype(BF16)
    r1 = x - hi.astype(F32)
    mid = r1.astype(BF16)
    lo = (r1 - mid.astype(F32)).astype(BF16)
    return hi, mid, lo


def _dot(a, b):
    return jnp.dot(a, b, preferred_element_type=F32)


def _dot_nt(a, b):
    return lax.dot_general(a, b, (((1,), (1,)), ((), ())), preferred_element_type=F32)


def _dot_tn(a, b):
    return lax.dot_general(a, b, (((0,), (0,)), ((), ())), preferred_element_type=F32)


def _exact_left01(mask01_bf16, x):
    hi, mid, lo = _split3(x)
    return _dot(mask01_bf16, hi) + _dot(mask01_bf16, mid) + _dot(mask01_bf16, lo)


def _exact_right01(x, mask01_bf16):
    hi, mid, lo = _split3(x)
    return _dot(hi, mask01_bf16) + _dot(mid, mask01_bf16) + _dot(lo, mask01_bf16)


def _log_sigmoid(x):
    return jnp.minimum(x, 0.0) - jnp.log(1.0 + jnp.exp(-jnp.abs(x)))


def _sigmoid(x):
    return 1.0 / (1.0 + jnp.exp(-x))


def _gelu_tanh(x):
    c = math.sqrt(2.0 / math.pi)
    return 0.5 * x * (1.0 + jnp.tanh(c * (x + 0.044715 * (x * x * x))))


def _layer_norm(z, w, b):
    mu = jnp.mean(z, axis=-1, keepdims=True)
    zc = z - mu
    return zc * lax.rsqrt(jnp.mean(zc * zc, axis=-1, keepdims=True) + EPS) * w + b


def _shift_rows(x, s, fill):
    rows = lax.broadcasted_iota(jnp.int32, x.shape, 0)
    return jnp.where(rows < s, fill, pltpu.roll(x, s, 0))


def _proj_kernel(x_ref, w_ref, wg_ref, o_ref, og_ref):
    x = x_ref[...].astype(BF16)
    o_ref[...] = _dot(x, w_ref[...])
    og_ref[...] = _dot(x, wg_ref[...])


def _proj(x, w_main, w_small):
    t, d = x.shape
    n = w_main.shape[1]
    tm = PROJ_TILE
    return pl.pallas_call(
        _proj_kernel,
        grid=(t // tm,),
        in_specs=[pl.BlockSpec((tm, d), lambda i: (i, 0)),
                  pl.BlockSpec((d, n), lambda i: (0, 0)),
                  pl.BlockSpec((d, LANES), lambda i: (0, 0))],
        out_specs=[pl.BlockSpec((tm, n), lambda i: (i, 0)),
                   pl.BlockSpec((tm, LANES), lambda i: (i, 0))],
        out_shape=[jax.ShapeDtypeStruct((t, n), F32), jax.ShapeDtypeStruct((t, LANES), F32)],
        compiler_params=_params("parallel"),
        name="in_proj",
    )(x, w_main, w_small)


def _mlstm_kernel(q_ref, k_ref, v_ref, o_ref, gc_ref, gr_ref, bc_ref, br_ref, nw_ref,
                  y_ref, c_ref, m_ref, *, chunk, n_chunks):
    L = chunk

    @pl.when(pl.program_id(1) == 0)
    def _():
        c_ref[...] = jnp.zeros_like(c_ref)
        m_ref[...] = jnp.zeros_like(m_ref)

    ri = lax.broadcasted_iota(jnp.int32, (L, L), 0)
    ci = lax.broadcasted_iota(jnp.int32, (L, L), 1)
    causal = ci <= ri
    tril = causal.astype(BF16)
    triu = (ri <= ci).astype(BF16)
    ones_v = jnp.ones((L, HEAD_DIM), BF16)
    scale = HEAD_DIM ** -0.5

    def body(c, carry):
        r0 = pl.multiple_of(c * L, L)
        g_col = gc_ref[pl.ds(r0, L), :] + bc_ref[...]
        g_row = gr_ref[c] + br_ref[...]
        b_col_all = _exact_left01(tril, _log_sigmoid(g_col))
        b_row_all = _exact_right01(_log_sigmoid(g_row), triu)
        for h in range(HEADS):
            lo = h * HEAD_DIM
            q = q_ref[pl.ds(r0, L), lo:lo + HEAD_DIM].astype(BF16)
            k = k_ref[pl.ds(r0, L), lo:lo + HEAD_DIM] * scale
            v = v_ref[pl.ds(r0, L), lo:lo + HEAD_DIM].astype(BF16)
            v_aug = jnp.concatenate([v, ones_v], axis=1)
            i_col = g_col[:, h:h + 1]
            i_row = g_row[h:h + 1, :]
            b_col = b_col_all[:, HEADS + h:HEADS + h + 1]
            b_row = b_row_all[HEADS + h:HEADS + h + 1, :]
            b_last = b_col[L - 1:L, :]
            m_prev = m_ref[h:h + 1, 0:1]
            c_prev = c_ref[h]

            d_mat = jnp.where(causal, b_col - b_row + i_row, NEG_INF)
            m_inter = b_col + m_prev
            m_i = jnp.maximum(m_inter, jnp.max(d_mat, axis=1, keepdims=True))
            s = _dot_nt(q, k.astype(BF16)) * jnp.exp(d_mat - m_i)
            w_inter = jnp.exp(m_inter - m_i)
            both = _dot(s.astype(BF16), v_aug) + w_inter * _dot(q, c_prev.astype(BF16))
            num = both[:, :HEAD_DIM]
            den = both[:, HEAD_DIM:]
            hh = num / jnp.maximum(jnp.abs(den), jnp.exp(-m_i))

            w_loc = b_last - b_col + i_col
            m_loc = jnp.max(w_loc, axis=0, keepdims=True)
            kp = (k * jnp.exp(w_loc - m_loc)).astype(BF16)
            c_loc = _dot_tn(kp, v_aug)
            m_new = jnp.maximum(b_last + m_prev, m_loc)
            c_ref[h] = jnp.exp(b_last + m_prev - m_new) * c_prev + jnp.exp(m_loc - m_new) * c_loc
            m_ref[h:h + 1, :] = jnp.broadcast_to(m_new, (1, LANES))

            hc = hh - jnp.mean(hh, axis=-1, keepdims=True)
            yn = hc * lax.rsqrt(jnp.mean(hc * hc, axis=-1, keepdims=True) + EPS)
            og = o_ref[pl.ds(r0, L), lo:lo + HEAD_DIM]
            y_ref[pl.ds(r0, L), lo:lo + HEAD_DIM] = yn * nw_ref[:, lo:lo + HEAD_DIM] * _sigmoid(og)
        return carry

    lax.fori_loop(0, n_chunks, body, 0)


def _mlstm(proj, gates, gate_b, norm_w, batch, seq):
    t = batch * seq
    L = MLSTM_CHUNK
    ts = MLSTM_TILE
    nj = seq // ts
    nc = ts // L
    g_row = gates[:, :2 * HEADS].reshape(t // L, L, 2 * HEADS).transpose(0, 2, 1)
    b_col = jnp.zeros((1, LANES), F32).at[0, :2 * HEADS].set(gate_b)
    b_row = gate_b.reshape(2 * HEADS, 1)
    blk = lambda col: pl.BlockSpec((ts, MIX_HALF), lambda b, j, col=col: (b * nj + j, col))
    kern = functools.partial(_mlstm_kernel, chunk=L, n_chunks=nc)
    return pl.pallas_call(
        kern,
        grid=(batch, nj),
        in_specs=[blk(0), blk(1), blk(2), blk(3),
                  pl.BlockSpec((ts, LANES), lambda b, j: (b * nj + j, 0)),
                  pl.BlockSpec((nc, 2 * HEADS, L), lambda b, j: (b * nj + j, 0, 0)),
                  pl.BlockSpec((1, LANES), lambda b, j: (0, 0)),
                  pl.BlockSpec((2 * HEADS, 1), lambda b, j: (0, 0)),
                  pl.BlockSpec((1, MIX_HALF), lambda b, j: (0, 0))],
        out_specs=pl.BlockSpec((ts, MIX_HALF), lambda b, j: (b * nj + j, 0)),
        out_shape=jax.ShapeDtypeStruct((t, MIX_HALF), F32),
        scratch_shapes=[pltpu.VMEM((HEADS, HEAD_DIM, 2 * HEAD_DIM), F32),
                        pltpu.VMEM((8, LANES), F32)],
        compiler_params=_params("arbitrary", "arbitrary"),
        name="mlstm",
    )(proj, proj, proj, proj, gates, g_row, b_col, b_row, norm_w.reshape(1, MIX_HALF))


def _rglru_kernel(xb_ref, gb_ref, cw_ref, cb_ref, wa_ref, ba_ref, wx_ref, bx_ref, lam_ref,
                  y_ref, xext_ref, h_ref, *, tile):
    @pl.when(pl.program_id(1) == 0)
    def _():
        xext_ref[0:8, :] = jnp.zeros((8, MIX_HALF), F32)
        h_ref[...] = jnp.zeros_like(h_ref)

    x = xb_ref[...]
    xext_ref[8:8 + tile, :] = x
    xc = cb_ref[...] + cw_ref[LRU_CONV - 1:LRU_CONV, :] * x
    for tap in range(LRU_CONV - 1):
        back = LRU_CONV - 1 - tap
        xc = xc + cw_ref[tap:tap + 1, :] * xext_ref[8 - back:8 - back + tile, :]
    xext_ref[0:8, :] = x[tile - 8:tile, :]

    xc16 = xc.astype(BF16)
    r_parts, i_parts = [], []
    for h in range(HEADS):
        lo = h * HEAD_DIM
        xh = xc16[:, lo:lo + HEAD_DIM]
        r_parts.append(_dot(xh, wa_ref[h]))
        i_parts.append(_dot(xh, wx_ref[h]))
    r = _sigmoid(jnp.concatenate(r_parts, axis=1) + ba_ref[...])
    ig = _sigmoid(jnp.concatenate(i_parts, axis=1) + bx_ref[...])
    lam = lam_ref[...]
    softplus_neg = jnp.maximum(-lam, 0.0) + jnp.log(1.0 + jnp.exp(-jnp.abs(lam)))
    log_a = -LRU_C * r * softplus_neg
    a = jnp.exp(log_a)
    th = jnp.tanh(log_a)
    u = jnp.sqrt(-2.0 * th / (1.0 - th)) * ig * xc

    s = 1
    while s < tile:
        u = a * _shift_rows(u, s, 0.0) + u
        a = a * _shift_rows(a, s, 1.0)
        s *= 2
    hcur = u + a * h_ref[0:1, :]
    h_ref[...] = jnp.broadcast_to(hcur[tile - 1:tile, :], h_ref.shape)
    y_ref[...] = hcur * _gelu_tanh(gb_ref[...])


def _rglru(proj, conv_w, conv_b, wa, ba, wx, bx, lam, batch, seq):
    t = batch * seq
    ts = LRU_TILE
    nj = seq // ts
    row = lambda a: a.reshape(1, MIX_HALF)
    const2 = lambda shape: pl.BlockSpec(shape, lambda b, j: (0, 0))
    const3 = lambda shape: pl.BlockSpec(shape, lambda b, j: (0, 0, 0))
    blk = lambda col: pl.BlockSpec((ts, MIX_HALF), lambda b, j, col=col: (b * nj + j, col))
    return pl.pallas_call(
        functools.partial(_rglru_kernel, tile=ts),
        grid=(batch, nj),
        in_specs=[blk(4), blk(5), const2((LRU_CONV, MIX_HALF)), const2((1, MIX_HALF)),
                  const3((HEADS, HEAD_DIM, HEAD_DIM)), const2((1, MIX_HALF)),
                  const3((HEADS, HEAD_DIM, HEAD_DIM)), const2((1, MIX_HALF)), const2((1, MIX_HALF))],
        out_specs=pl.BlockSpec((ts, MIX_HALF), lambda b, j: (b * nj + j, 0)),
        out_shape=jax.ShapeDtypeStruct((t, MIX_HALF), F32),
        scratch_shapes=[pltpu.VMEM((ts + 8, MIX_HALF), F32), pltpu.VMEM((8, MIX_HALF), F32)],
        compiler_params=_params("arbitrary", "arbitrary"),
        name="rglru",
    )(proj, proj, conv_w, row(conv_b), wa.astype(BF16), row(ba), wx.astype(BF16), row(bx), row(lam))


def _s5_kernel(u_ref, bre_ref, bim_ref, cre_ref, cim_ref, mre_ref, mim_ref, pre_ref, pim_ref, d_ref, gw_ref,
               gb_ref, y_ref, xr_ref, xi_ref, cr_ref, ci_ref, *, tile):
    @pl.when(pl.program_id(1) == 0)
    def _():
        cr_ref[...] = jnp.zeros_like(cr_ref)
        ci_ref[...] = jnp.zeros_like(ci_ref)

    u = u_ref[...]
    u16 = u.astype(BF16)
    blk_c = MIX_HALF // S5_BLOCKS
    blk_s = S5_LANES // S5_BLOCKS
    for j in range(S5_BLOCKS):
        uj = u16[:, j * blk_c:(j + 1) * blk_c]
        xr_ref[:, j * blk_s:(j + 1) * blk_s] = _dot(uj, bre_ref[j])
        xi_ref[:, j * blk_s:(j + 1) * blk_s] = _dot(uj, bim_ref[j])

    for s0 in range(0, S5_LANES, S5_STRIP):
        lanes = slice(s0, s0 + S5_STRIP)

        def group(i, carry, lanes=lanes):
            cr, ci = carry
            r0 = pl.multiple_of(i * SUBLANES, SUBLANES)
            xr = xr_ref[pl.ds(r0, SUBLANES), lanes]
            xi = xi_ref[pl.ds(r0, SUBLANES), lanes]
            for k in range(S5_LOG_STEPS):
                sr = pltpu.roll(xr, 1 << k, 0)
                si = pltpu.roll(xi, 1 << k, 0)
                mr = mre_ref[k, :, lanes]
                mi = mim_ref[k, :, lanes]
                xr, xi = xr + mr * sr - mi * si, xi + mr * si + mi * sr
            pr = pre_ref[:, lanes]
            pi = pim_ref[:, lanes]
            xr, xi = xr + pr * cr - pi * ci, xi + pr * ci + pi * cr
            xr_ref[pl.ds(r0, SUBLANES), lanes] = xr
            xi_ref[pl.ds(r0, SUBLANES), lanes] = xi
            return xr[SUBLANES - 1:SUBLANES, :], xi[SUBLANES - 1:SUBLANES, :]

        cr, ci = lax.fori_loop(0, tile // SUBLANES, group, (cr_ref[0:1, lanes], ci_ref[0:1, lanes]),
                               unroll=S5_UNROLL)
        cr_ref[0:1, lanes] = cr
        ci_ref[0:1, lanes] = ci

    parts = []
    for j in range(S5_BLOCKS):
        xrj = xr_ref[:, j * blk_s:(j + 1) * blk_s].astype(BF16)
        xij = xi_ref[:, j * blk_s:(j + 1) * blk_s].astype(BF16)
        parts.append(_dot(xrj, cre_ref[j]) - _dot(xij, cim_ref[j]))
    y = jnp.concatenate(parts, axis=1) + d_ref[...] * u
    g = _gelu_tanh(y)
    y_ref[...] = g * _sigmoid(_dot(g.astype(BF16), gw_ref[...]) + gb_ref[...])


def _s5_tables(lam_re, lam_im, b_re, b_im, c_re, c_im, log_dt):
    lr, li = lam_re.astype(F32), lam_im.astype(F32)
    dt = jnp.exp(log_dt.astype(F32))[:, None]
    mag = jnp.exp(lr * dt)
    abar_re = mag * jnp.cos(li * dt)
    abar_im = mag * jnp.sin(li * dt)
    den = lr * lr + li * li
    nr = abar_re - 1.0
    coef_re = (nr * lr + abar_im * li) / den
    coef_im = (abar_im * lr - nr * li) / den
    bbar_re = coef_re[..., None] * b_re - coef_im[..., None] * b_im
    bbar_im = coef_re[..., None] * b_im + coef_im[..., None] * b_re
    gpb = S5_GROUPS // S5_BLOCKS
    eye = jnp.eye(gpb, dtype=F32)

    def in_map(bb):
        bb = bb.reshape(S5_BLOCKS, gpb, S5_STATE, S5_GROUP)
        return jnp.einsum("jgph,gk->jghkp", bb, eye).reshape(S5_BLOCKS, gpb * S5_GROUP, gpb * S5_STATE)

    def out_map(cc):
        cc = cc.reshape(S5_BLOCKS, gpb, S5_GROUP, S5_STATE)
        return jnp.einsum("jghp,gk->jgpkh", cc, eye).reshape(S5_BLOCKS, gpb * S5_STATE, gpb * S5_GROUP)

    def power(n):
        n = jnp.asarray(n, F32)[..., None, None]
        pmag = jnp.exp(n * (lr * dt))
        shape = n.shape[:-2] + (S5_LANES,)
        return (pmag * jnp.cos(n * (li * dt))).reshape(shape), (pmag * jnp.sin(n * (li * dt))).reshape(shape)

    row = jnp.arange(SUBLANES)
    step = 2 ** jnp.arange(S5_LOG_STEPS)
    s_re, s_im = power(step)
    keep = (row[None, :] >= step[:, None])[..., None]
    m_re = jnp.where(keep, s_re[:, None, :], 0.0)
    m_im = jnp.where(keep, s_im[:, None, :], 0.0)
    p_re, p_im = power(row + 1)
    return (in_map(bbar_re).astype(BF16), in_map(bbar_im).astype(BF16),
            out_map(c_re.astype(F32)).astype(BF16), out_map(c_im.astype(F32)).astype(BF16),
            m_re, m_im, p_re, p_im)


def _s5(proj, tables, d_skip, glu_w, glu_b, batch, seq):
    t = batch * seq
    ts = S5_TILE
    nj = seq // ts
    bre, bim, cre, cim, m_re, m_im, p_re, p_im = tables
    blk_c = MIX_HALF // S5_BLOCKS
    blk_s = S5_LANES // S5_BLOCKS
    const2 = lambda shape: pl.BlockSpec(shape, lambda b, j: (0, 0))
    const3 = lambda shape: pl.BlockSpec(shape, lambda b, j: (0, 0, 0))
    return pl.pallas_call(
        functools.partial(_s5_kernel, tile=ts),
        grid=(batch, nj),
        in_specs=[pl.BlockSpec((ts, MIX_HALF), lambda b, j: (b * nj + j, 0)),
                  const3((S5_BLOCKS, blk_c, blk_s)), const3((S5_BLOCKS, blk_c, blk_s)),
                  const3((S5_BLOCKS, blk_s, blk_c)), const3((S5_BLOCKS, blk_s, blk_c)),
                  const3(m_re.shape), const3(m_im.shape), const2(p_re.shape), const2(p_im.shape),
                  const2((1, MIX_HALF)), const2((MIX_HALF, MIX_HALF)), const2((1, MIX_HALF))],
        out_specs=pl.BlockSpec((ts, MIX_HALF), lambda b, j: (b * nj + j, 0)),
        out_shape=jax.ShapeDtypeStruct((t, MIX_HALF), F32),
        scratch_shapes=[pltpu.VMEM((ts, S5_LANES), F32), pltpu.VMEM((ts, S5_LANES), F32),
                        pltpu.VMEM((8, S5_LANES), F32), pltpu.VMEM((8, S5_LANES), F32)],
        compiler_params=_params("arbitrary", "arbitrary"),
        name="s5",
    )(proj, bre, bim, cre, cim, m_re, m_im, p_re, p_im, d_skip.reshape(1, MIX_HALF), glu_w.astype(BF16),
      glu_b.reshape(1, MIX_HALF))


def _gla_kernel(q_ref, k_ref, v_ref, r_ref, gl_ref, gw_ref, gb_ref, nw_ref, y_ref, st_ref,
                *, chunk, n_chunks):
    L = chunk

    @pl.when(pl.program_id(1) == 0)
    def _():
        st_ref[...] = jnp.zeros_like(st_ref)

    ri = lax.broadcasted_iota(jnp.int32, (L, L), 0)
    ci = lax.broadcasted_iota(jnp.int32, (L, L), 1)
    causal = ci <= ri
    tril = causal.astype(BF16)
    scale = GLA_DK ** -0.5

    def body(c, carry):
        r0 = pl.multiple_of(c * L, L)
        z = _dot(gl_ref[pl.ds(r0, L), :].astype(BF16), gw_ref[...]) + gb_ref[...]
        log_alpha = _log_sigmoid(z) * (1.0 / GLA_GATE_TEMP)
        bcum_all = _exact_left01(tril, log_alpha)
        for h in range(HEADS):
            lo = h * HEAD_DIM
            bcum = bcum_all[:, lo:lo + HEAD_DIM]
            b_last = bcum[L - 1:L, :]
            q = q_ref[pl.ds(r0, L), lo:lo + HEAD_DIM] * scale
            k = k_ref[pl.ds(r0, L), lo:lo + HEAD_DIM]
            v = v_ref[pl.ds(r0, L), lo:lo + HEAD_DIM].astype(BF16)
            q_dec = (q * jnp.exp(bcum)).astype(BF16)
            k_inv = (k * jnp.exp(-bcum)).astype(BF16)
            k_end = (k * jnp.exp(b_last - bcum)).astype(BF16)
            st = st_ref[h]
            att = jnp.where(causal, _dot_nt(q_dec, k_inv), 0.0)
            o = _dot(att.astype(BF16), v) + _dot_nt(q_dec, st.astype(BF16))
            st_ref[h] = jnp.exp(b_last) * st + _dot_tn(v, k_end)
            yn = o * lax.rsqrt(jnp.mean(o * o, axis=-1, keepdims=True) + EPS)
            rg = r_ref[pl.ds(r0, L), lo:lo + HEAD_DIM]
            y_ref[pl.ds(r0, L), lo:lo + HEAD_DIM] = yn * nw_ref[:, lo:lo + HEAD_DIM] * (rg * _sigmoid(rg))
        return carry

    lax.fori_loop(0, n_chunks, body, 0, unroll=GLA_UNROLL)


def _gla(proj, glow, gate_w, gate_b, norm_w, batch, seq):
    t = batch * seq
    ts = GLA_TILE
    nj = seq // ts
    blk = lambda col: pl.BlockSpec((ts, MIX_HALF), lambda b, j, col=col: (b * nj + j, col))
    const2 = lambda shape: pl.BlockSpec(shape, lambda b, j: (0, 0))
    return pl.pallas_call(
        functools.partial(_gla_kernel, chunk=GLA_CHUNK, n_chunks=ts // GLA_CHUNK),
        grid=(batch, nj),
        in_specs=[blk(1), blk(2), blk(3), blk(4),
                  pl.BlockSpec((ts, LANES), lambda b, j: (b * nj + j, 0)),
                  const2((LANES, MIX_HALF)), const2((1, MIX_HALF)), const2((1, MIX_HALF))],
        out_specs=pl.BlockSpec((ts, MIX_HALF), lambda b, j: (b * nj + j, 0)),
        out_shape=jax.ShapeDtypeStruct((t, MIX_HALF), F32),
        scratch_shapes=[pltpu.VMEM((HEADS, HEAD_DIM, HEAD_DIM), F32)],
        compiler_params=_params("arbitrary", "arbitrary"),
        name="gla",
    )(proj, proj, proj, proj, glow, gate_w, gate_b, norm_w.reshape(1, MIX_HALF))


def _pad_heads(w, axis):
    shape = list(w.shape)
    shape[axis:axis + 1] = [HEADS, GLA_DK]
    w = w.reshape(shape)
    pad = [(0, 0)] * w.ndim
    pad[axis + 1] = (0, HEAD_DIM - GLA_DK)
    w = jnp.pad(w, pad)
    shape[axis:axis + 2] = [HEADS * HEAD_DIM]
    return w.reshape(shape)


def _pack_bf16_pairs(z):
    hi = lax.bitcast_convert_type(z[:, :PACKED].astype(BF16).astype(F32), jnp.uint32)
    lo = lax.bitcast_convert_type(z[:, PACKED:].astype(BF16).astype(F32), jnp.uint32)
    word = (hi & jnp.uint32(0xFFFF0000)) | lax.shift_right_logical(lo, jnp.uint32(16))
    return lax.bitcast_convert_type(word, jnp.int32)


def _unpack_bf16_pairs(p):
    word = lax.bitcast_convert_type(p, jnp.uint32)
    hi = lax.bitcast_convert_type(word & jnp.uint32(0xFFFF0000), F32)
    lo = lax.bitcast_convert_type(lax.shift_left(word, jnp.uint32(16)), F32)
    return hi, lo


def _out_kernel(ya_ref, yb_ref, h_ref, w_ref, lw_ref, lb_ref, o_ref, opk_ref):
    mixed = jnp.concatenate([ya_ref[...], yb_ref[...]], axis=1).astype(BF16)
    z = ALPHA * h_ref[...] + _dot(mixed, w_ref[...])
    out = _layer_norm(z, lw_ref[...], lb_ref[...])
    o_ref[...] = out
    opk_ref[...] = _pack_bf16_pairs(out)


def _out_proj_ln(ya, yb, h, w_out, ln_w, ln_b):
    t = h.shape[0]
    tm = OUT_TILE
    const = lambda shape: pl.BlockSpec(shape, lambda i: (0, 0))
    return pl.pallas_call(
        _out_kernel,
        grid=(t // tm,),
        in_specs=[pl.BlockSpec((tm, MIX_HALF), lambda i: (i, 0)),
                  pl.BlockSpec((tm, MIX_HALF), lambda i: (i, 0)),
                  pl.BlockSpec((tm, D_MODEL), lambda i: (i, 0)),
                  const((D_MODEL, D_MODEL)), const((1, D_MODEL)), const((1, D_MODEL))],
        out_specs=[pl.BlockSpec((tm, D_MODEL), lambda i: (i, 0)),
                   pl.BlockSpec((tm, PACKED), lambda i: (i, 0))],
        out_shape=[jax.ShapeDtypeStruct((t, D_MODEL), F32), jax.ShapeDtypeStruct((t, PACKED), jnp.int32)],
        compiler_params=_params("parallel"),
        name="out_proj_ln",
    )(ya, yb, h, w_out.astype(BF16), ln_w.reshape(1, D_MODEL), ln_b.reshape(1, D_MODEL))


def _first_index(hit, idx, big):
    return jnp.min(jnp.where(hit, idx, big), axis=0, keepdims=True)


def _router_kernel(h_ref, w_ref, b_ref, idx_ref, rank_ref, wk_ref, cnt_ref, base_ref, *, tile):
    @pl.when(pl.program_id(0) == 0)
    def _():
        base_ref[...] = jnp.zeros_like(base_ref)

    h_hi, h_mid, _ = _split3(h_ref[...])
    w_hi, w_mid, _ = _split3(w_ref[...])
    logits = _dot_nt(w_hi, h_hi) + _dot_nt(w_hi, h_mid) + _dot_nt(w_mid, h_hi)
    scores = _sigmoid(logits)
    biased = scores + b_ref[...]

    sub = lax.broadcasted_iota(jnp.int32, (GROUP_SIZE, tile), 0)
    grp_rows = []
    for g in range(N_GROUPS):
        xg = biased[g * GROUP_SIZE:(g + 1) * GROUP_SIZE, :]
        m1 = jnp.max(xg, axis=0, keepdims=True)
        i1 = _first_index(xg == m1, sub, GROUP_SIZE)
        m2 = jnp.max(jnp.where(sub == i1, NEG_INF, xg), axis=0, keepdims=True)
        grp_rows.append(m1 + m2)
    gs = jnp.concatenate(grp_rows, axis=0)
    gsel = jnp.zeros((N_GROUPS, tile), F32)
    for _ in range(TOPK_GROUPS):
        mx = jnp.max(gs, axis=0, keepdims=True)
        hit = sub == _first_index(gs == mx, sub, N_GROUPS)
        gsel = jnp.where(hit, 1.0, gsel)
        gs = jnp.where(hit, NEG_INF, gs)
    emask = jnp.concatenate(
        [jnp.broadcast_to(gsel[g:g + 1, :], (GROUP_SIZE, tile)) for g in range(N_GROUPS)], axis=0)

    eidx = lax.broadcasted_iota(jnp.int32, (N_EXPERTS, tile), 0)
    cand = jnp.where(emask > 0.5, biased, NEG_INF)
    sel = jnp.zeros((N_EXPERTS, tile), F32)
    hits, picks = [], []
    for _ in range(TOP_K):
        mx = jnp.max(cand, axis=0, keepdims=True)
        first = _first_index(cand == mx, eidx, N_EXPERTS)
        hit = eidx == first
        hits.append(hit)
        picks.append(first)
        sel = jnp.where(hit, 1.0, sel)
        cand = jnp.where(hit, NEG_INF, cand)
    picked = jnp.where(sel > 0.5, scores, 0.0)
    wts = picked / jnp.sum(picked, axis=0, keepdims=True) * ROUTED_SCALE

    ri = lax.broadcasted_iota(jnp.int32, (tile, tile), 0)
    ci = lax.broadcasted_iota(jnp.int32, (tile, tile), 1)
    before = (ri < ci).astype(BF16)
    prior = _dot(sel.astype(BF16), before) + base_ref[:, 0:1]
    ranks = [jnp.sum(jnp.where(hit, prior, 0.0), axis=0, keepdims=True) for hit in hits]
    wsel = [jnp.sum(jnp.where(hit, wts, 0.0), axis=0, keepdims=True) for hit in hits]
    idx_ref[...] = jnp.concatenate(picks, axis=0)
    rank_ref[...] = jnp.concatenate(ranks, axis=0).astype(jnp.int32)
    wk_ref[...] = jnp.concatenate(wsel, axis=0)
    total = base_ref[...] + jnp.sum(sel, axis=1, keepdims=True)
    base_ref[...] = total
    cnt_ref[...] = total


def _router(h, router_w, router_bias):
    t = h.shape[0]
    tm = ROUTER_TILE
    per_tok = lambda dt: jax.ShapeDtypeStruct((TOP_K, t), dt)
    tok_blk = pl.BlockSpec((TOP_K, tm), lambda i: (0, i))
    return pl.pallas_call(
        functools.partial(_router_kernel, tile=tm),
        grid=(t // tm,),
        in_specs=[pl.BlockSpec((tm, D_MODEL), lambda i: (i, 0)),
                  pl.BlockSpec((N_EXPERTS, D_MODEL), lambda i: (0, 0)),
                  pl.BlockSpec((N_EXPERTS, 1), lambda i: (0, 0))],
        out_specs=[tok_blk, tok_blk, tok_blk, pl.BlockSpec((N_EXPERTS, LANES), lambda i: (0, 0))],
        out_shape=[per_tok(jnp.int32), per_tok(jnp.int32), per_tok(F32),
                   jax.ShapeDtypeStruct((N_EXPERTS, LANES), F32)],
        scratch_shapes=[pltpu.VMEM((N_EXPERTS, LANES), F32)],
        compiler_params=_params("arbitrary"),
        name="router",
    )(h, router_w.T, router_bias.reshape(N_EXPERTS, 1))


def _silu(x):
    return x * _sigmoid(x)


def _sc_mesh():
    return plsc.VectorSubcoreMesh(core_axis_name="c", subcore_axis_name="s")


def _sc_worker_id():
    return lax.axis_index("s") * SC_CORES + lax.axis_index("c")


def _dispatch_rows(xpk, pos_chunks, n_rows):
    t = xpk.shape[0]
    n_ch = t // SC_WORKERS // SC_CHUNK

    @functools.partial(
        pl.kernel, mesh=_sc_mesh(),
        out_type=jax.ShapeDtypeStruct((n_rows, PACKED), jnp.int32),
        scratch_types=[pltpu.VMEM((TOP_K, SC_CHUNK), jnp.int32),
                       pltpu.VMEM((SC_CHUNK, PACKED), jnp.int32),
                       pltpu.SemaphoreType.DMA],
        name="moe_dispatch",
    )
    def scatter(x_hbm, pos_hbm, out_hbm, idx_v, rows_v, sem):
        wid = _sc_worker_id()

        @pl.loop(0, n_ch)
        def _(c):
            chunk = wid * n_ch + c
            off = pl.multiple_of(chunk * SC_CHUNK, SC_CHUNK)
            pltpu.sync_copy(pos_hbm.at[chunk], idx_v)
            pltpu.sync_copy(x_hbm.at[pl.ds(off, SC_CHUNK)], rows_v)
            copies = [pltpu.async_copy(rows_v, out_hbm.at[idx_v.at[k]], sem) for k in range(TOP_K)]
            for cp in copies:
                cp.wait()

    return scatter(xpk, pos_chunks)


def _gather_rows(table, idx):
    n = idx.shape[0]
    per_w = n // SC_WORKERS
    n_ch = per_w // SC_CHUNK

    @functools.partial(
        pl.kernel, mesh=_sc_mesh(),
        out_type=jax.ShapeDtypeStruct((n, PACKED), jnp.int32),
        scratch_types=[pltpu.VMEM((SC_CHUNK,), jnp.int32),
                       pltpu.VMEM((SC_CHUNK, PACKED), jnp.int32),
                       pltpu.SemaphoreType.DMA],
        name="moe_gather",
    )
    def gather(table_hbm, idx_hbm, out_hbm, idx_v, rows_v, sem):
        base = _sc_worker_id() * per_w

        @pl.loop(0, n_ch)
        def _(c):
            off = pl.multiple_of(base + c * SC_CHUNK, SC_CHUNK)
            pltpu.sync_copy(idx_hbm.at[pl.ds(off, SC_CHUNK)], idx_v)
            pltpu.async_copy(table_hbm.at[idx_v], rows_v, sem).wait()
            pltpu.sync_copy(rows_v, out_hbm.at[pl.ds(off, SC_CHUNK)])

    return gather(table, idx)


def _unpacked_bf16(p):
    hi, lo = _unpack_bf16_pairs(p)
    return jnp.concatenate([hi.astype(BF16), lo.astype(BF16)], axis=1)


def _expert_kernel(be_ref, nu_ref, x_ref, wg_ref, wu_ref, wd_ref, y_ref, g16_ref, u16_ref, d16_ref):
    i = pl.program_id(0)
    changed = jnp.logical_or(i == 0, be_ref[i] != be_ref[jnp.maximum(i - 1, 0)])

    @pl.when(changed)
    def _():
        g16_ref[...] = wg_ref[0, 0].astype(BF16)
        u16_ref[...] = wu_ref[0, 0].astype(BF16)
        d16_ref[...] = wd_ref[0, 0].astype(BF16)

    @pl.when(i < nu_ref[0])
    def _():
        x = _unpacked_bf16(x_ref[...])
        hh = _silu(_dot(x, g16_ref[...])) * _dot(x, u16_ref[...])
        y_ref[...] = _pack_bf16_pairs(_dot(hh.astype(BF16), d16_ref[...]))


def _experts(block_e, n_used, xs, wg, wu, wd, layer):
    nb = block_e.shape[0]
    bm = MOE_BLOCK
    grid_spec = pltpu.PrefetchScalarGridSpec(
        num_scalar_prefetch=2,
        grid=(nb,),
        in_specs=[pl.BlockSpec((bm, PACKED), lambda i, be, nu: (i, 0)),
                  pl.BlockSpec((1, 1, D_MODEL, D_EXPERT), lambda i, be, nu: (layer, be[i], 0, 0)),
                  pl.BlockSpec((1, 1, D_MODEL, D_EXPERT), lambda i, be, nu: (layer, be[i], 0, 0)),
                  pl.BlockSpec((1, 1, D_EXPERT, D_MODEL), lambda i, be, nu: (layer, be[i], 0, 0))],
        out_specs=pl.BlockSpec((bm, PACKED), lambda i, be, nu: (i, 0)),
        scratch_shapes=[pltpu.VMEM((D_MODEL, D_EXPERT), BF16), pltpu.VMEM((D_MODEL, D_EXPERT), BF16),
                        pltpu.VMEM((D_EXPERT, D_MODEL), BF16)],
    )
    return pl.pallas_call(
        _expert_kernel,
        grid_spec=grid_spec,
        out_shape=jax.ShapeDtypeStruct((nb * bm, PACKED), jnp.int32),
        compiler_params=_params("arbitrary"),
        name="moe_experts",
    )(block_e, n_used, xs, wg, wu, wd)


def _combine_kernel(g_ref, wk_ref, h_ref, xpk_ref, sg_ref, su_ref, sd_ref, lw_ref, lb_ref, o_ref):
    x = _unpacked_bf16(xpk_ref[...])
    hs = _silu(_dot(x, sg_ref[...])) * _dot(x, su_ref[...])
    shared = _dot(hs.astype(BF16), sd_ref[...])
    acc_hi = shared[:, :PACKED]
    acc_lo = shared[:, PACKED:]
    wk = wk_ref[...]
    for k in range(TOP_K):
        y_hi, y_lo = _unpack_bf16_pairs(g_ref[k])
        w = wk[:, k:k + 1]
        acc_hi = acc_hi + w * y_hi
        acc_lo = acc_lo + w * y_lo
    ffn = jnp.concatenate([acc_hi, acc_lo], axis=1)
    o_ref[...] = _layer_norm(ALPHA * h_ref[...] + ffn, lw_ref[...], lb_ref[...])


def _combine_ln(g, wk, h, xpk, sg, su, sd, ln_w, ln_b):
    t = h.shape[0]
    tm = COMBINE_TILE
    const = lambda shape: pl.BlockSpec(shape, lambda i: (0, 0))
    return pl.pallas_call(
        _combine_kernel,
        grid=(t // tm,),
        in_specs=[pl.BlockSpec((TOP_K, tm, PACKED), lambda i: (0, i, 0)),
                  pl.BlockSpec((tm, TOP_K), lambda i: (i, 0)),
                  pl.BlockSpec((tm, D_MODEL), lambda i: (i, 0)),
                  pl.BlockSpec((tm, PACKED), lambda i: (i, 0)),
                  const((D_MODEL, D_EXPERT)), const((D_MODEL, D_EXPERT)), const((D_EXPERT, D_MODEL)),
                  const((1, D_MODEL)), const((1, D_MODEL))],
        out_specs=pl.BlockSpec((tm, D_MODEL), lambda i: (i, 0)),
        out_shape=jax.ShapeDtypeStruct((t, D_MODEL), F32),
        compiler_params=_params("parallel"),
        name="moe_combine_ln",
    )(g, wk, h, xpk, sg.astype(BF16), su.astype(BF16), sd.astype(BF16),
      ln_w.reshape(1, D_MODEL), ln_b.reshape(1, D_MODEL))


def _moe_ln(h, hpk, router_w, router_bias, wg, wu, wd, layer, sg, su, sd, ln_w, ln_b):
    t = h.shape[0]
    idx, rank, wk, counts = _router(h, router_w, router_bias)
    cnt = counts[:, 0].astype(jnp.int32)
    padded = (cnt + MOE_BLOCK - 1) // MOE_BLOCK * MOE_BLOCK
    pend = jnp.cumsum(padded)
    experts = jnp.arange(N_EXPERTS, dtype=jnp.int32)
    pstart_of_pick = jnp.sum(jnp.where(idx[:, :, None] == experts, pend - padded, 0), axis=-1)
    pos = pstart_of_pick + rank
    nb = -(-(t * TOP_K + N_EXPERTS * (MOE_BLOCK - 1)) // MOE_BLOCK)
    starts = jnp.arange(nb, dtype=jnp.int32) * MOE_BLOCK
    block_e = jnp.minimum(jnp.sum((pend[None, :] <= starts[:, None]).astype(jnp.int32), axis=1), N_EXPERTS - 1)
    n_used = (pend[-1] // MOE_BLOCK).astype(jnp.int32).reshape(1)
    pos_chunks = pos.reshape(TOP_K, t // SC_CHUNK, SC_CHUNK).transpose(1, 0, 2)
    xs = _dispatch_rows(hpk, pos_chunks, nb * MOE_BLOCK)
    ys = _experts(block_e, n_used, xs, wg, wu, wd, layer)
    g = _gather_rows(ys, pos.reshape(-1)).reshape(TOP_K, t, PACKED)
    return _combine_ln(g, wk.T, h, hpk, sg, su, sd, ln_w, ln_b)


def _pad_cols(w, width=LANES):
    return jnp.pad(w, ((0, 0), (0, width - w.shape[1])))


def _even_mixer(h, batch, seq, w_in, gate_b, norm_w, conv_w, conv_b, wa, ba, wx, bx, lam):
    a4 = 4 * MIX_HALF
    ng = 2 * HEADS
    w_main = jnp.concatenate([w_in[:, :a4], w_in[:, a4 + ng:]], axis=1).astype(BF16)
    w_gate = _pad_cols(w_in[:, a4:a4 + ng]).astype(BF16)
    proj, gates = _proj(h, w_main, w_gate)
    ya = _mlstm(proj, gates, gate_b, norm_w, batch, seq)
    yb = _rglru(proj, conv_w, conv_b, wa, ba, wx, bx, lam, batch, seq)
    return ya, yb


def _odd_mixer(h, batch, seq, w_in, lam_re, lam_im, b_re, b_im, c_re, c_im, d_skip, log_dt,
               glu_w, glu_b, gate_w, gate_b, norm_w):
    c0 = MIX_HALF
    c1 = c0 + HEADS * GLA_DK
    c2 = c1 + HEADS * GLA_DK
    c3 = c2 + MIX_HALF
    c4 = c3 + MIX_HALF
    w_main = jnp.concatenate([w_in[:, :c0], _pad_heads(w_in[:, c0:c1], 1), _pad_heads(w_in[:, c1:c2], 1),
                              w_in[:, c2:c4]], axis=1).astype(BF16)
    w_low = _pad_cols(w_in[:, c4:]).astype(BF16)
    proj, glow = _proj(h, w_main, w_low)
    tables = _s5_tables(lam_re, lam_im, b_re, b_im, c_re, c_im, log_dt)
    yc = _s5(proj, tables, d_skip, glu_w, glu_b, batch, seq)
    gw = jnp.pad(_pad_heads(gate_w, 1), ((0, LANES - GLA_GATE_RANK), (0, 0))).astype(BF16)
    gb = _pad_heads(gate_b.reshape(1, -1), 1)
    yd = _gla(proj, glow, gw, gb, norm_w, batch, seq)
    return yc, yd


def kernel(x, ln1_w, ln1_b, ln2_w, ln2_b, w_out, w_in_even, mlstm_gate_b, mlstm_norm_w, lru_conv_w, lru_conv_b, lru_wa, lru_ba, lru_wx, lru_bx, lru_lambda, w_in_odd, s5_lam_re, s5_lam_im, s5_b_re, s5_b_im, s5_c_re, s5_c_im, s5_d, s5_log_dt, s5_glu_w, s5_glu_b, gla_gate_w, gla_gate_b, gla_norm_w, router_w, router_bias, exp_w_gate, exp_w_up, exp_w_down, sh_w_gate, sh_w_up, sh_w_down):
    batch, seq, d = x.shape
    h = x.reshape(batch * seq, d)
    for layer in range(DEPTH):
        j = layer // 2
        if layer % 2 == 0:
            y1, y2 = _even_mixer(h, batch, seq, w_in_even[j], mlstm_gate_b[j], mlstm_norm_w[j],
                                 lru_conv_w[j], lru_conv_b[j], lru_wa[j], lru_ba[j], lru_wx[j],
                                 lru_bx[j], lru_lambda[j])
        else:
            y1, y2 = _odd_mixer(h, batch, seq, w_in_odd[j], s5_lam_re[j], s5_lam_im[j], s5_b_re[j],
                                s5_b_im[j], s5_c_re[j], s5_c_im[j], s5_d[j], s5_log_dt[j],
                                s5_glu_w[j], s5_glu_b[j], gla_gate_w[j], gla_gate_b[j], gla_norm_w[j])
        h, hpk = _out_proj_ln(y1, y2, h, w_out[layer], ln1_w[layer], ln1_b[layer])
        h = _moe_ln(h, hpk, router_w[layer], router_bias[layer], exp_w_gate, exp_w_up, exp_w_down, layer,
                    sh_w_gate[layer], sh_w_up[layer], sh_w_down[layer], ln2_w[layer], ln2_b[layer])
    return h.reshape(batch, seq, d)
```

```python
import functools
import math

import jax
import jax.numpy as jnp
from jax import lax
from jax.experimental import pallas as pl
from jax.experimental.pallas import tpu as pltpu
from jax.experimental.pallas import tpu_sc as plsc

F32 = jnp.float32
BF16 = jnp.bfloat16

D_MODEL = 1024
DEPTH = 2
MIX_HALF = 512
HEADS = 4
HEAD_DIM = 128
GLA_DK = 64
GLA_CHUNK = 64
GLA_GATE_RANK = 16
GLA_GATE_TEMP = 16.0
LRU_C = 8.0
LRU_CONV = 4
S5_GROUP = 16
S5_GROUPS = 32
S5_STATE = 64
S5_LANES = S5_GROUPS * S5_STATE
S5_BLOCKS = 4
N_EXPERTS = 64
N_GROUPS = 8
GROUP_SIZE = N_EXPERTS // N_GROUPS
TOP_K = 8
TOPK_GROUPS = 4
D_EXPERT = 256
ROUTED_SCALE = 2.5
ALPHA = (2.0 * DEPTH) ** 0.25
EPS = 1e-5
LANES = 128
SUBLANES = 8
NEG_INF = float("-inf")

VMEM_LIMIT = 56 * 1024 * 1024

MLSTM_CHUNK = 128
MLSTM_TILE = 1024
LRU_TILE = 1024
LRU_LOG_STEPS = 3
LRU_UNROLL = 4
S5_TILE = 512
S5_STRIP = 512
S5_LOG_STEPS = 3
S5_UNROLL = 4
GLA_UNROLL = 4
GLA_TILE = 1024
PROJ_TILE = 512
OUT_TILE = 512
MOE_BLOCK = 1024
COMBINE_TILE = 256
PACKED = D_MODEL // 2
SC_CHUNK = 64
SC_CORES = 2
SC_SUBCORES = 16
SC_WORKERS = SC_CORES * SC_SUBCORES


def _params(*sem):
    return pltpu.CompilerParams(dimension_semantics=sem, vmem_limit_bytes=VMEM_LIMIT)


def _split3(x):
    hi = x.astype(BF16)
    r1 = x - hi.astype(F32)
    mid = r1.astype(BF16)
    lo = (r1 - mid.astype(F32)).astype(BF16)
    return hi, mid, lo


def _dot(a, b):
    return jnp.dot(a, b, preferred_element_type=F32)


def _dot_nt(a, b):
    return lax.dot_general(a, b, (((1,), (1,)), ((), ())), preferred_element_type=F32)


def _dot_tn(a, b):
    return lax.dot_general(a, b, (((0,), (0,)), ((), ())), preferred_element_type=F32)


def _exact_left01(mask01_bf16, x):
    hi, mid, lo = _split3(x)
    return _dot(mask01_bf16, hi) + _dot(mask01_bf16, mid) + _dot(mask01_bf16, lo)


def _exact_right01(x, mask01_bf16):
    hi, mid, lo = _split3(x)
    return _dot(hi, mask01_bf16) + _dot(mid, mask01_bf16) + _dot(lo, mask01_bf16)


def _log_sigmoid(x):
    return jnp.minimum(x, 0.0) - jnp.log(1.0 + jnp.exp(-jnp.abs(x)))


def _sigmoid(x):
    return 1.0 / (1.0 + jnp.exp(-x))


def _gelu_tanh(x):
    c = math.sqrt(2.0 / math.pi)
    return 0.5 * x * (1.0 + jnp.tanh(c * (x + 0.044715 * (x * x * x))))


def _layer_norm(z, w, b):
    mu = jnp.mean(z, axis=-1, keepdims=True)
    zc = z - mu
    return zc * lax.rsqrt(jnp.mean(zc * zc, axis=-1, keepdims=True) + EPS) * w + b


def _proj_kernel(x_ref, w_ref, wg_ref, o_ref, og_ref):
    x = x_ref[...].astype(BF16)
    o_ref[...] = _dot(x, w_ref[...])
    og_ref[...] = _dot(x, wg_ref[...])


def _proj(x, w_main, w_small):
    t, d = x.shape
    n = w_main.shape[1]
    tm = PROJ_TILE
    return pl.pallas_call(
        _proj_kernel,
        grid=(t // tm,),
        in_specs=[pl.BlockSpec((tm, d), lambda i: (i, 0)),
                  pl.BlockSpec((d, n), lambda i: (0, 0)),
                  pl.BlockSpec((d, LANES), lambda i: (0, 0))],
        out_specs=[pl.BlockSpec((tm, n), lambda i: (i, 0)),
                   pl.BlockSpec((tm, LANES), lambda i: (i, 0))],
        out_shape=[jax.ShapeDtypeStruct((t, n), F32), jax.ShapeDtypeStruct((t, LANES), F32)],
        compiler_params=_params("parallel"),
        name="in_proj",
    )(x, w_main, w_small)


def _mlstm_kernel(q_ref, k_ref, v_ref, o_ref, gc_ref, gr_ref, bc_ref, br_ref, nw_ref,
                  y_ref, c_ref, m_ref, *, chunk, n_chunks):
    L = chunk

    @pl.when(pl.program_id(1) == 0)
    def _():
        c_ref[...] = jnp.zeros_like(c_ref)
        m_ref[...] = jnp.zeros_like(m_ref)

    ri = lax.broadcasted_iota(jnp.int32, (L, L), 0)
    ci = lax.broadcasted_iota(jnp.int32, (L, L), 1)
    causal = ci <= ri
    tril = causal.astype(BF16)
    triu = (ri <= ci).astype(BF16)
    ones_v = jnp.ones((L, HEAD_DIM), BF16)
    scale = HEAD_DIM ** -0.5

    def body(c, carry):
        r0 = pl.multiple_of(c * L, L)
        g_col = gc_ref[pl.ds(r0, L), :] + bc_ref[...]
        g_row = gr_ref[c] + br_ref[...]
        b_col_all = _exact_left01(tril, _log_sigmoid(g_col))
        b_row_all = _exact_right01(_log_sigmoid(g_row), triu)
        for h in range(HEADS):
            lo = h * HEAD_DIM
            q = q_ref[pl.ds(r0, L), lo:lo + HEAD_DIM].astype(BF16)
            k = k_ref[pl.ds(r0, L), lo:lo + HEAD_DIM] * scale
            v = v_ref[pl.ds(r0, L), lo:lo + HEAD_DIM].astype(BF16)
            v_aug = jnp.concatenate([v, ones_v], axis=1)
            i_col = g_col[:, h:h + 1]
            i_row = g_row[h:h + 1, :]
            b_col = b_col_all[:, HEADS + h:HEADS + h + 1]
            b_row = b_row_all[HEADS + h:HEADS + h + 1, :]
            b_last = b_col[L - 1:L, :]
            m_prev = m_ref[h:h + 1, 0:1]
            c_prev = c_ref[h]

            d_mat = jnp.where(causal, b_col - b_row + i_row, NEG_INF)
            m_inter = b_col + m_prev
            m_i = jnp.maximum(m_inter, jnp.max(d_mat, axis=1, keepdims=True))
            s = _dot_nt(q, k.astype(BF16)) * jnp.exp(d_mat - m_i)
            w_inter = jnp.exp(m_inter - m_i)
            both = _dot(s.astype(BF16), v_aug) + w_inter * _dot(q, c_prev.astype(BF16))
            num = both[:, :HEAD_DIM]
            den = both[:, HEAD_DIM:]
            hh = num / jnp.maximum(jnp.abs(den), jnp.exp(-m_i))

            w_loc = b_last - b_col + i_col
            m_loc = jnp.max(w_loc, axis=0, keepdims=True)
            kp = (k * jnp.exp(w_loc - m_loc)).astype(BF16)
            c_loc = _dot_tn(kp, v_aug)
            m_new = jnp.maximum(b_last + m_prev, m_loc)
            c_ref[h] = jnp.exp(b_last + m_prev - m_new) * c_prev + jnp.exp(m_loc - m_new) * c_loc
            m_ref[h:h + 1, :] = jnp.broadcast_to(m_new, (1, LANES))

            hc = hh - jnp.mean(hh, axis=-1, keepdims=True)
            yn = hc * lax.rsqrt(jnp.mean(hc * hc, axis=-1, keepdims=True) + EPS)
            og = o_ref[pl.ds(r0, L), lo:lo + HEAD_DIM]
            y_ref[pl.ds(r0, L), lo:lo + HEAD_DIM] = yn * nw_ref[:, lo:lo + HEAD_DIM] * _sigmoid(og)
        return carry

    lax.fori_loop(0, n_chunks, body, 0)


def _mlstm(proj, gates, gate_b, norm_w, batch, seq):
    t = batch * seq
    L = MLSTM_CHUNK
    ts = MLSTM_TILE
    nj = seq // ts
    nc = ts // L
    g_row = gates[:, :2 * HEADS].reshape(t // L, L, 2 * HEADS).transpose(0, 2, 1)
    b_col = jnp.zeros((1, LANES), F32).at[0, :2 * HEADS].set(gate_b)
    b_row = gate_b.reshape(2 * HEADS, 1)
    blk = lambda col: pl.BlockSpec((ts, MIX_HALF), lambda b, j, col=col: (b * nj + j, col))
    kern = functools.partial(_mlstm_kernel, chunk=L, n_chunks=nc)
    return pl.pallas_call(
        kern,
        grid=(batch, nj),
        in_specs=[blk(0), blk(1), blk(2), blk(3),
                  pl.BlockSpec((ts, LANES), lambda b, j: (b * nj + j, 0)),
                  pl.BlockSpec((nc, 2 * HEADS, L), lambda b, j: (b * nj + j, 0, 0)),
                  pl.BlockSpec((1, LANES), lambda b, j: (0, 0)),
                  pl.BlockSpec((2 * HEADS, 1), lambda b, j: (0, 0)),
                  pl.BlockSpec((1, MIX_HALF), lambda b, j: (0, 0))],
        out_specs=pl.BlockSpec((ts, MIX_HALF), lambda b, j: (b * nj + j, 0)),
        out_shape=jax.ShapeDtypeStruct((t, MIX_HALF), F32),
        scratch_shapes=[pltpu.VMEM((HEADS, HEAD_DIM, 2 * HEAD_DIM), F32),
                        pltpu.VMEM((8, LANES), F32)],
        compiler_params=_params("arbitrary", "arbitrary"),
        name="mlstm",
    )(proj, proj, proj, proj, gates, g_row, b_col, b_row, norm_w.reshape(1, MIX_HALF))


def _rglru_kernel(xb_ref, gb_ref, cw_ref, cb_ref, wa_ref, ba_ref, wx_ref, bx_ref, lam_ref,
                  y_ref, xext_ref, h_ref, a_ref, u_ref, *, tile):
    @pl.when(pl.program_id(1) == 0)
    def _():
        xext_ref[0:8, :] = jnp.zeros((8, MIX_HALF), F32)
        h_ref[...] = jnp.zeros_like(h_ref)

    x = xb_ref[...]
    xext_ref[8:8 + tile, :] = x
    xc = cb_ref[...] + cw_ref[LRU_CONV - 1:LRU_CONV, :] * x
    for tap in range(LRU_CONV - 1):
        back = LRU_CONV - 1 - tap
        xc = xc + cw_ref[tap:tap + 1, :] * xext_ref[8 - back:8 - back + tile, :]
    xext_ref[0:8, :] = x[tile - 8:tile, :]

    xc16 = xc.astype(BF16)
    r_parts, i_parts = [], []
    for h in range(HEADS):
        lo = h * HEAD_DIM
        xh = xc16[:, lo:lo + HEAD_DIM]
        r_parts.append(_dot(xh, wa_ref[h]))
        i_parts.append(_dot(xh, wx_ref[h]))
    r = _sigmoid(jnp.concatenate(r_parts, axis=1) + ba_ref[...])
    ig = _sigmoid(jnp.concatenate(i_parts, axis=1) + bx_ref[...])
    lam = lam_ref[...]
    softplus_neg = jnp.maximum(-lam, 0.0) + jnp.log(1.0 + jnp.exp(-jnp.abs(lam)))
    log_a = -LRU_C * r * softplus_neg
    a = jnp.exp(log_a)
    th = jnp.tanh(log_a)
    u = jnp.sqrt(-2.0 * th / (1.0 - th)) * ig * xc

    a_ref[...] = a
    u_ref[...] = u
    rows = lax.broadcasted_iota(jnp.int32, (SUBLANES, MIX_HALF), 0)

    def group(i, h_prev):
        r0 = pl.multiple_of(i * SUBLANES, SUBLANES)
        ag = a_ref[pl.ds(r0, SUBLANES), :]
        ug = u_ref[pl.ds(r0, SUBLANES), :]
        for k in range(LRU_LOG_STEPS):
            keep = rows >= (1 << k)
            ug = ag * jnp.where(keep, pltpu.roll(ug, 1 << k, 0), 0.0) + ug
            ag = ag * jnp.where(keep, pltpu.roll(ag, 1 << k, 0), 1.0)
        hg = ug + ag * h_prev
        u_ref[pl.ds(r0, SUBLANES), :] = hg
        return hg[SUBLANES - 1:SUBLANES, :]

    h_last = lax.fori_loop(0, tile // SUBLANES, group, h_ref[0:1, :], unroll=LRU_UNROLL)
    h_ref[...] = jnp.broadcast_to(h_last, h_ref.shape)
    y_ref[...] = u_ref[...] * _gelu_tanh(gb_ref[...])


def _rglru(proj, conv_w, conv_b, wa, ba, wx, bx, lam, batch, seq):
    t = batch * seq
    ts = LRU_TILE
    nj = seq // ts
    row = lambda a: a.reshape(1, MIX_HALF)
    const2 = lambda shape: pl.BlockSpec(shape, lambda b, j: (0, 0))
    const3 = lambda shape: pl.BlockSpec(shape, lambda b, j: (0, 0, 0))
    blk = lambda col: pl.BlockSpec((ts, MIX_HALF), lambda b, j, col=col: (b * nj + j, col))
    return pl.pallas_call(
        functools.partial(_rglru_kernel, tile=ts),
        grid=(batch, nj),
        in_specs=[blk(4), blk(5), const2((LRU_CONV, MIX_HALF)), const2((1, MIX_HALF)),
                  const3((HEADS, HEAD_DIM, HEAD_DIM)), const2((1, MIX_HALF)),
                  const3((HEADS, HEAD_DIM, HEAD_DIM)), const2((1, MIX_HALF)), const2((1, MIX_HALF))],
        out_specs=pl.BlockSpec((ts, MIX_HALF), lambda b, j: (b * nj + j, 0)),
        out_shape=jax.ShapeDtypeStruct((t, MIX_HALF), F32),
        scratch_shapes=[pltpu.VMEM((ts + 8, MIX_HALF), F32), pltpu.VMEM((8, MIX_HALF), F32),
                        pltpu.VMEM((ts, MIX_HALF), F32), pltpu.VMEM((ts, MIX_HALF), F32)],
        compiler_params=_params("arbitrary", "arbitrary"),
        name="rglru",
    )(proj, proj, conv_w, row(conv_b), wa.astype(BF16), row(ba), wx.astype(BF16), row(bx), row(lam))


def _s5_kernel(u_ref, bre_ref, bim_ref, cre_ref, cim_ref, mre_ref, mim_ref, pre_ref, pim_ref, d_ref, gw_ref,
               gb_ref, y_ref, xr_ref, xi_ref, cr_ref, ci_ref, *, tile):
    @pl.when(pl.program_id(1) == 0)
    def _():
        cr_ref[...] = jnp.zeros_like(cr_ref)
        ci_ref[...] = jnp.zeros_like(ci_ref)

    u = u_ref[...]
    u16 = u.astype(BF16)
    blk_c = MIX_HALF // S5_BLOCKS
    blk_s = S5_LANES // S5_BLOCKS
    for j in range(S5_BLOCKS):
        uj = u16[:, j * blk_c:(j + 1) * blk_c]
        xr_ref[:, j * blk_s:(j + 1) * blk_s] = _dot(uj, bre_ref[j])
        xi_ref[:, j * blk_s:(j + 1) * blk_s] = _dot(uj, bim_ref[j])

    for s0 in range(0, S5_LANES, S5_STRIP):
        lanes = slice(s0, s0 + S5_STRIP)

        def group(i, carry, lanes=lanes):
            cr, ci = carry
            r0 = pl.multiple_of(i * SUBLANES, SUBLANES)
            xr = xr_ref[pl.ds(r0, SUBLANES), lanes]
            xi = xi_ref[pl.ds(r0, SUBLANES), lanes]
            for k in range(S5_LOG_STEPS):
                sr = pltpu.roll(xr, 1 << k, 0)
                si = pltpu.roll(xi, 1 << k, 0)
                mr = mre_ref[k, :, lanes]
                mi = mim_ref[k, :, lanes]
                xr, xi = xr + mr * sr - mi * si, xi + mr * si + mi * sr
            pr = pre_ref[:, lanes]
            pi = pim_ref[:, lanes]
            xr, xi = xr + pr * cr - pi * ci, xi + pr * ci + pi * cr
            xr_ref[pl.ds(r0, SUBLANES), lanes] = xr
            xi_ref[pl.ds(r0, SUBLANES), lanes] = xi
            return xr[SUBLANES - 1:SUBLANES, :], xi[SUBLANES - 1:SUBLANES, :]

        cr, ci = lax.fori_loop(0, tile // SUBLANES, group, (cr_ref[0:1, lanes], ci_ref[0:1, lanes]),
                               unroll=S5_UNROLL)
        cr_ref[0:1, lanes] = cr
        ci_ref[0:1, lanes] = ci

    parts = []
    for j in range(S5_BLOCKS):
        xrj = xr_ref[:, j * blk_s:(j + 1) * blk_s].astype(BF16)
        xij = xi_ref[:, j * blk_s:(j + 1) * blk_s].astype(BF16)
        parts.append(_dot(xrj, cre_ref[j]) - _dot(xij, cim_ref[j]))
    y = jnp.concatenate(parts, axis=1) + d_ref[...] * u
    g = _gelu_tanh(y)
    y_ref[...] = g * _sigmoid(_dot(g.astype(BF16), gw_ref[...]) + gb_ref[...])


def _s5_tables(lam_re, lam_im, b_re, b_im, c_re, c_im, log_dt):
    lr, li = lam_re.astype(F32), lam_im.astype(F32)
    dt = jnp.exp(log_dt.astype(F32))[:, None]
    mag = jnp.exp(lr * dt)
    abar_re = mag * jnp.cos(li * dt)
    abar_im = mag * jnp.sin(li * dt)
    den = lr * lr + li * li
    nr = abar_re - 1.0
    coef_re = (nr * lr + abar_im * li) / den
    coef_im = (abar_im * lr - nr * li) / den
    bbar_re = coef_re[..., None] * b_re - coef_im[..., None] * b_im
    bbar_im = coef_re[..., None] * b_im + coef_im[..., None] * b_re
    gpb = S5_GROUPS // S5_BLOCKS
    eye = jnp.eye(gpb, dtype=F32)

    def in_map(bb):
        bb = bb.reshape(S5_BLOCKS, gpb, S5_STATE, S5_GROUP)
        return jnp.einsum("jgph,gk->jghkp", bb, eye).reshape(S5_BLOCKS, gpb * S5_GROUP, gpb * S5_STATE)

    def out_map(cc):
        cc = cc.reshape(S5_BLOCKS, gpb, S5_GROUP, S5_STATE)
        return jnp.einsum("jghp,gk->jgpkh", cc, eye).reshape(S5_BLOCKS, gpb * S5_STATE, gpb * S5_GROUP)

    def power(n):
        n = jnp.asarray(n, F32)[..., None, None]
        pmag = jnp.exp(n * (lr * dt))
        shape = n.shape[:-2] + (S5_LANES,)
        return (pmag * jnp.cos(n * (li * dt))).reshape(shape), (pmag * jnp.sin(n * (li * dt))).reshape(shape)

    row = jnp.arange(SUBLANES)
    step = 2 ** jnp.arange(S5_LOG_STEPS)
    s_re, s_im = power(step)
    keep = (row[None, :] >= step[:, None])[..., None]
    m_re = jnp.where(keep, s_re[:, None, :], 0.0)
    m_im = jnp.where(keep, s_im[:, None, :], 0.0)
    p_re, p_im = power(row + 1)
    return (in_map(bbar_re).astype(BF16), in_map(bbar_im).astype(BF16),
            out_map(c_re.astype(F32)).astype(BF16), out_map(c_im.astype(F32)).astype(BF16),
            m_re, m_im, p_re, p_im)


def _s5(proj, tables, d_skip, glu_w, glu_b, batch, seq):
    t = batch * seq
    ts = S5_TILE
    nj = seq // ts
    bre, bim, cre, cim, m_re, m_im, p_re, p_im = tables
    blk_c = MIX_HALF // S5_BLOCKS
    blk_s = S5_LANES // S5_BLOCKS
    const2 = lambda shape: pl.BlockSpec(shape, lambda b, j: (0, 0))
    const3 = lambda shape: pl.BlockSpec(shape, lambda b, j: (0, 0, 0))
    return pl.pallas_call(
        functools.partial(_s5_kernel, tile=ts),
        grid=(batch, nj),
        in_specs=[pl.BlockSpec((ts, MIX_HALF), lambda b, j: (b * nj + j, 0)),
                  const3((S5_BLOCKS, blk_c, blk_s)), const3((S5_BLOCKS, blk_c, blk_s)),
                  const3((S5_BLOCKS, blk_s, blk_c)), const3((S5_BLOCKS, blk_s, blk_c)),
                  const3(m_re.shape), const3(m_im.shape), const2(p_re.shape), const2(p_im.shape),
                  const2((1, MIX_HALF)), const2((MIX_HALF, MIX_HALF)), const2((1, MIX_HALF))],
        out_specs=pl.BlockSpec((ts, MIX_HALF), lambda b, j: (b * nj + j, 0)),
        out_shape=jax.ShapeDtypeStruct((t, MIX_HALF), F32),
        scratch_shapes=[pltpu.VMEM((ts, S5_LANES), F32), pltpu.VMEM((ts, S5_LANES), F32),
                        pltpu.VMEM((8, S5_LANES), F32), pltpu.VMEM((8, S5_LANES), F32)],
        compiler_params=_params("arbitrary", "arbitrary"),
        name="s5",
    )(proj, bre, bim, cre, cim, m_re, m_im, p_re, p_im, d_skip.reshape(1, MIX_HALF), glu_w.astype(BF16),
      glu_b.reshape(1, MIX_HALF))


def _gla_kernel(q_ref, k_ref, v_ref, r_ref, gl_ref, gw_ref, gb_ref, nw_ref, y_ref,
                st_ref, qd_ref, ki_ref, ke_ref, v16_ref, dec_ref, o_ref, *, tile, chunk):
    L = chunk
    nc = tile // L

    @pl.when(pl.program_id(1) == 0)
    def _():
        st_ref[...] = jnp.zeros_like(st_ref)

    z = _dot(gl_ref[...].astype(BF16), gw_ref[...]) + gb_ref[...]
    bcum = _log_sigmoid(z) * (1.0 / GLA_GATE_TEMP)
    row_in_chunk = lax.broadcasted_iota(jnp.int32, bcum.shape, 0) & (L - 1)
    s = 1
    while s < L:
        bcum = bcum + jnp.where(row_in_chunk >= s, pltpu.roll(bcum, s, 0), 0.0)
        s *= 2
    b3 = bcum.reshape(nc, L, MIX_HALF)
    b_last = b3[:, L - 1:L, :]
    k = k_ref[...]
    qd_ref[...] = (q_ref[...] * (GLA_DK ** -0.5) * jnp.exp(bcum)).astype(BF16)
    ki_ref[...] = (k * jnp.exp(-bcum)).astype(BF16)
    ke_ref[...] = (k.reshape(nc, L, MIX_HALF) * jnp.exp(b_last - b3)).reshape(tile, MIX_HALF).astype(BF16)
    v16_ref[...] = v_ref[...].astype(BF16)
    dec_ref[...] = jnp.exp(b_last)

    ri = lax.broadcasted_iota(jnp.int32, (L, L), 0)
    ci = lax.broadcasted_iota(jnp.int32, (L, L), 1)
    causal = ci <= ri

    def body(c, carry):
        r0 = pl.multiple_of(c * L, L)
        dec = dec_ref[c]
        for h in range(HEADS):
            lo = h * HEAD_DIM
            q_dec = qd_ref[pl.ds(r0, L), lo:lo + HEAD_DIM]
            v = v16_ref[pl.ds(r0, L), lo:lo + HEAD_DIM]
            st = st_ref[h]
            att = jnp.where(causal, _dot_nt(q_dec, ki_ref[pl.ds(r0, L), lo:lo + HEAD_DIM]), 0.0)
            o_ref[pl.ds(r0, L), lo:lo + HEAD_DIM] = (_dot(att.astype(BF16), v)
                                                     + _dot_nt(q_dec, st.astype(BF16)))
            st_ref[h] = dec[:, lo:lo + HEAD_DIM] * st + _dot_tn(v, ke_ref[pl.ds(r0, L), lo:lo + HEAD_DIM])
        return carry

    lax.fori_loop(0, nc, body, 0, unroll=GLA_UNROLL)

    rg = r_ref[...]
    gate = nw_ref[...] * (rg * _sigmoid(rg))
    for h in range(HEADS):
        lo = h * HEAD_DIM
        o = o_ref[:, lo:lo + HEAD_DIM]
        yn = o * lax.rsqrt(jnp.mean(o * o, axis=-1, keepdims=True) + EPS)
        y_ref[:, lo:lo + HEAD_DIM] = yn * gate[:, lo:lo + HEAD_DIM]


def _gla(proj, glow, gate_w, gate_b, norm_w, batch, seq):
    t = batch * seq
    ts = GLA_TILE
    nj = seq // ts
    blk = lambda col: pl.BlockSpec((ts, MIX_HALF), lambda b, j, col=col: (b * nj + j, col))
    const2 = lambda shape: pl.BlockSpec(shape, lambda b, j: (0, 0))
    return pl.pallas_call(
        functools.partial(_gla_kernel, tile=ts, chunk=GLA_CHUNK),
        grid=(batch, nj),
        in_specs=[blk(1), blk(2), blk(3), blk(4),
                  pl.BlockSpec((ts, LANES), lambda b, j: (b * nj + j, 0)),
                  const2((LANES, MIX_HALF)), const2((1, MIX_HALF)), const2((1, MIX_HALF))],
        out_specs=pl.BlockSpec((ts, MIX_HALF), lambda b, j: (b * nj + j, 0)),
        out_shape=jax.ShapeDtypeStruct((t, MIX_HALF), F32),
        scratch_shapes=[pltpu.VMEM((HEADS, HEAD_DIM, HEAD_DIM), F32),
                        pltpu.VMEM((ts, MIX_HALF), BF16), pltpu.VMEM((ts, MIX_HALF), BF16),
                        pltpu.VMEM((ts, MIX_HALF), BF16), pltpu.VMEM((ts, MIX_HALF), BF16),
                        pltpu.VMEM((ts // GLA_CHUNK, 1, MIX_HALF), F32),
                        pltpu.VMEM((ts, MIX_HALF), F32)],
        compiler_params=_params("arbitrary", "arbitrary"),
        name="gla",
    )(proj, proj, proj, proj, glow, gate_w, gate_b, norm_w.reshape(1, MIX_HALF))


def _pad_heads(w, axis):
    shape = list(w.shape)
    shape[axis:axis + 1] = [HEADS, GLA_DK]
    w = w.reshape(shape)
    pad = [(0, 0)] * w.ndim
    pad[axis + 1] = (0, HEAD_DIM - GLA_DK)
    w = jnp.pad(w, pad)
    shape[axis:axis + 2] = [HEADS * HEAD_DIM]
    return w.reshape(shape)


def _pack_bf16_pairs(z):
    hi = lax.bitcast_convert_type(z[:, :PACKED].astype(BF16).astype(F32), jnp.uint32)
    lo = lax.bitcast_convert_type(z[:, PACKED:].astype(BF16).astype(F32), jnp.uint32)
    word = (hi & jnp.uint32(0xFFFF0000)) | lax.shift_right_logical(lo, jnp.uint32(16))
    return lax.bitcast_convert_type(word, jnp.int32)


def _unpack_bf16_pairs(p):
    word = lax.bitcast_convert_type(p, jnp.uint32)
    hi = lax.bitcast_convert_type(word & jnp.uint32(0xFFFF0000), F32)
    lo = lax.bitcast_convert_type(lax.shift_left(word, jnp.uint32(16)), F32)
    return hi, lo


def _out_kernel(ya_ref, yb_ref, h_ref, w_ref, lw_ref, lb_ref, rw_ref, rb_ref,
                o_ref, opk_ref, idx_ref, rank_ref, wk_ref, cnt_ref, base_ref, *, tile):
    mixed = jnp.concatenate([ya_ref[...], yb_ref[...]], axis=1).astype(BF16)
    z = ALPHA * h_ref[...] + _dot(mixed, w_ref[...])
    out = _layer_norm(z, lw_ref[...], lb_ref[...])
    o_ref[...] = out
    opk_ref[...] = _pack_bf16_pairs(out)
    _route_tile(out, rw_ref, rb_ref, idx_ref, rank_ref, wk_ref, cnt_ref, base_ref, tile)


def _out_proj_ln_route(ya, yb, h, w_out, ln_w, ln_b, router_w, router_bias):
    t = h.shape[0]
    tm = OUT_TILE
    const = lambda shape: pl.BlockSpec(shape, lambda i: (0, 0))
    per_tok = lambda dt: jax.ShapeDtypeStruct((TOP_K, t), dt)
    tok_blk = pl.BlockSpec((TOP_K, tm), lambda i: (0, i))
    return pl.pallas_call(
        functools.partial(_out_kernel, tile=tm),
        grid=(t // tm,),
        in_specs=[pl.BlockSpec((tm, MIX_HALF), lambda i: (i, 0)),
                  pl.BlockSpec((tm, MIX_HALF), lambda i: (i, 0)),
                  pl.BlockSpec((tm, D_MODEL), lambda i: (i, 0)),
                  const((D_MODEL, D_MODEL)), const((1, D_MODEL)), const((1, D_MODEL)),
                  const((N_EXPERTS, D_MODEL)), const((N_EXPERTS, 1))],
        out_specs=[pl.BlockSpec((tm, D_MODEL), lambda i: (i, 0)),
                   pl.BlockSpec((tm, PACKED), lambda i: (i, 0)),
                   tok_blk, tok_blk, tok_blk, const((N_EXPERTS, LANES))],
        out_shape=[jax.ShapeDtypeStruct((t, D_MODEL), F32), jax.ShapeDtypeStruct((t, PACKED), jnp.int32),
                   per_tok(jnp.int32), per_tok(jnp.int32), per_tok(F32),
                   jax.ShapeDtypeStruct((N_EXPERTS, LANES), F32)],
        scratch_shapes=[pltpu.VMEM((N_EXPERTS, LANES), F32)],
        compiler_params=_params("arbitrary"),
        name="out_proj_ln_route",
    )(ya, yb, h, w_out.astype(BF16), ln_w.reshape(1, D_MODEL), ln_b.reshape(1, D_MODEL),
      router_w.T, router_bias.reshape(N_EXPERTS, 1))


def _first_index(hit, idx, big):
    return jnp.min(jnp.where(hit, idx, big), axis=0, keepdims=True)


def _route_tile(h, w_ref, b_ref, idx_ref, rank_ref, wk_ref, cnt_ref, base_ref, tile):
    @pl.when(pl.program_id(0) == 0)
    def _():
        base_ref[...] = jnp.zeros_like(base_ref)

    h_hi, h_mid, _ = _split3(h)
    w_hi, w_mid, _ = _split3(w_ref[...])
    logits = _dot_nt(w_hi, h_hi) + _dot_nt(w_hi, h_mid) + _dot_nt(w_mid, h_hi)
    scores = _sigmoid(logits)
    biased = scores + b_ref[...]

    sub = lax.broadcasted_iota(jnp.int32, (GROUP_SIZE, tile), 0)
    grp_rows = []
    for g in range(N_GROUPS):
        xg = biased[g * GROUP_SIZE:(g + 1) * GROUP_SIZE, :]
        m1 = jnp.max(xg, axis=0, keepdims=True)
        i1 = _first_index(xg == m1, sub, GROUP_SIZE)
        m2 = jnp.max(jnp.where(sub == i1, NEG_INF, xg), axis=0, keepdims=True)
        grp_rows.append(m1 + m2)
    gs = jnp.concatenate(grp_rows, axis=0)
    gsel = jnp.zeros((N_GROUPS, tile), F32)
    for _ in range(TOPK_GROUPS):
        mx = jnp.max(gs, axis=0, keepdims=True)
        hit = sub == _first_index(gs == mx, sub, N_GROUPS)
        gsel = jnp.where(hit, 1.0, gsel)
        gs = jnp.where(hit, NEG_INF, gs)
    emask = jnp.concatenate(
        [jnp.broadcast_to(gsel[g:g + 1, :], (GROUP_SIZE, tile)) for g in range(N_GROUPS)], axis=0)

    eidx = lax.broadcasted_iota(jnp.int32, (N_EXPERTS, tile), 0)
    cand = jnp.where(emask > 0.5, biased, NEG_INF)
    sel = jnp.zeros((N_EXPERTS, tile), F32)
    hits, picks = [], []
    for _ in range(TOP_K):
        mx = jnp.max(cand, axis=0, keepdims=True)
        first = _first_index(cand == mx, eidx, N_EXPERTS)
        hit = eidx == first
        hits.append(hit)
        picks.append(first)
        sel = jnp.where(hit, 1.0, sel)
        cand = jnp.where(hit, NEG_INF, cand)
    picked = jnp.where(sel > 0.5, scores, 0.0)
    wts = picked / jnp.sum(picked, axis=0, keepdims=True) * ROUTED_SCALE

    ri = lax.broadcasted_iota(jnp.int32, (tile, tile), 0)
    ci = lax.broadcasted_iota(jnp.int32, (tile, tile), 1)
    before = (ri < ci).astype(BF16)
    prior = _dot(sel.astype(BF16), before) + base_ref[:, 0:1]
    ranks = [jnp.sum(jnp.where(hit, prior, 0.0), axis=0, keepdims=True) for hit in hits]
    wsel = [jnp.sum(jnp.where(hit, wts, 0.0), axis=0, keepdims=True) for hit in hits]
    idx_ref[...] = jnp.concatenate(picks, axis=0)
    rank_ref[...] = jnp.concatenate(ranks, axis=0).astype(jnp.int32)
    wk_ref[...] = jnp.concatenate(wsel, axis=0)
    total = base_ref[...] + jnp.sum(sel, axis=1, keepdims=True)
    base_ref[...] = total
    cnt_ref[...] = total


def _silu(x):
    return x * _sigmoid(x)


def _sc_mesh():
    return plsc.VectorSubcoreMesh(core_axis_name="c", subcore_axis_name="s")


def _sc_worker_id():
    return lax.axis_index("s") * SC_CORES + lax.axis_index("c")


def _dispatch_rows(xpk, pos_chunks, n_rows):
    t = xpk.shape[0]
    n_ch = t // SC_WORKERS // SC_CHUNK

    @functools.partial(
        pl.kernel, mesh=_sc_mesh(),
        out_type=jax.ShapeDtypeStruct((n_rows, PACKED), jnp.int32),
        scratch_types=[pltpu.VMEM((TOP_K, SC_CHUNK), jnp.int32),
                       pltpu.VMEM((SC_CHUNK, PACKED), jnp.int32),
                       pltpu.SemaphoreType.DMA],
        name="moe_dispatch",
    )
    def scatter(x_hbm, pos_hbm, out_hbm, idx_v, rows_v, sem):
        wid = _sc_worker_id()

        @pl.loop(0, n_ch)
        def _(c):
            chunk = wid * n_ch + c
            off = pl.multiple_of(chunk * SC_CHUNK, SC_CHUNK)
            pltpu.sync_copy(pos_hbm.at[chunk], idx_v)
            pltpu.sync_copy(x_hbm.at[pl.ds(off, SC_CHUNK)], rows_v)
            copies = [pltpu.async_copy(rows_v, out_hbm.at[idx_v.at[k]], sem) for k in range(TOP_K)]
            for cp in copies:
                cp.wait()

    return scatter(xpk, pos_chunks)


def _gather_rows(table, idx):
    n = idx.shape[0]
    per_w = n // SC_WORKERS
    n_ch = per_w // SC_CHUNK

    @functools.partial(
        pl.kernel, mesh=_sc_mesh(),
        out_type=jax.ShapeDtypeStruct((n, PACKED), jnp.int32),
        scratch_types=[pltpu.VMEM((SC_CHUNK,), jnp.int32),
                       pltpu.VMEM((SC_CHUNK, PACKED), jnp.int32),
                       pltpu.SemaphoreType.DMA],
        name="moe_gather",
    )
    def gather(table_hbm, idx_hbm, out_hbm, idx_v, rows_v, sem):
        base = _sc_worker_id() * per_w

        @pl.loop(0, n_ch)
        def _(c):
            off = pl.multiple_of(base + c * SC_CHUNK, SC_CHUNK)
            pltpu.sync_copy(idx_hbm.at[pl.ds(off, SC_CHUNK)], idx_v)
            pltpu.async_copy(table_hbm.at[idx_v], rows_v, sem).wait()
            pltpu.sync_copy(rows_v, out_hbm.at[pl.ds(off, SC_CHUNK)])

    return gather(table, idx)


def _unpacked_bf16(p):
    hi, lo = _unpack_bf16_pairs(p)
    return jnp.concatenate([hi.astype(BF16), lo.astype(BF16)], axis=1)


def _expert_kernel(be_ref, nu_ref, x_ref, wg_ref, wu_ref, wd_ref, y_ref, g16_ref, u16_ref, d16_ref):
    i = pl.program_id(0)
    changed = jnp.logical_or(i == 0, be_ref[i] != be_ref[jnp.maximum(i - 1, 0)])

    @pl.when(changed)
    def _():
        g16_ref[...] = wg_ref[0, 0].astype(BF16)
        u16_ref[...] = wu_ref[0, 0].astype(BF16)
        d16_ref[...] = wd_ref[0, 0].astype(BF16)

    @pl.when(i < nu_ref[0])
    def _():
        x = _unpacked_bf16(x_ref[...])
        hh = _silu(_dot(x, g16_ref[...])) * _dot(x, u16_ref[...])
        y_ref[...] = _pack_bf16_pairs(_dot(hh.astype(BF16), d16_ref[...]))


def _experts(block_e, n_used, xs, wg, wu, wd, layer):
    nb = block_e.shape[0]
    bm = MOE_BLOCK
    grid_spec = pltpu.PrefetchScalarGridSpec(
        num_scalar_prefetch=2,
        grid=(nb,),
        in_specs=[pl.BlockSpec((bm, PACKED), lambda i, be, nu: (i, 0)),
                  pl.BlockSpec((1, 1, D_MODEL, D_EXPERT), lambda i, be, nu: (layer, be[i], 0, 0)),
                  pl.BlockSpec((1, 1, D_MODEL, D_EXPERT), lambda i, be, nu: (layer, be[i], 0, 0)),
                  pl.BlockSpec((1, 1, D_EXPERT, D_MODEL), lambda i, be, nu: (layer, be[i], 0, 0))],
        out_specs=pl.BlockSpec((bm, PACKED), lambda i, be, nu: (i, 0)),
        scratch_shapes=[pltpu.VMEM((D_MODEL, D_EXPERT), BF16), pltpu.VMEM((D_MODEL, D_EXPERT), BF16),
                        pltpu.VMEM((D_EXPERT, D_MODEL), BF16)],
    )
    return pl.pallas_call(
        _expert_kernel,
        grid_spec=grid_spec,
        out_shape=jax.ShapeDtypeStruct((nb * bm, PACKED), jnp.int32),
        compiler_params=_params("arbitrary"),
        name="moe_experts",
    )(block_e, n_used, xs, wg, wu, wd)


def _combine_kernel(g_ref, wk_ref, h_ref, xpk_ref, sg_ref, su_ref, sd_ref, lw_ref, lb_ref, o_ref):
    x = _unpacked_bf16(xpk_ref[...])
    hs = _silu(_dot(x, sg_ref[...])) * _dot(x, su_ref[...])
    shared = _dot(hs.astype(BF16), sd_ref[...])
    acc_hi = shared[:, :PACKED]
    acc_lo = shared[:, PACKED:]
    wk = wk_ref[...]
    for k in range(TOP_K):
        y_hi, y_lo = _unpack_bf16_pairs(g_ref[k])
        w = wk[:, k:k + 1]
        acc_hi = acc_hi + w * y_hi
        acc_lo = acc_lo + w * y_lo
    ffn = jnp.concatenate([acc_hi, acc_lo], axis=1)
    o_ref[...] = _layer_norm(ALPHA * h_ref[...] + ffn, lw_ref[...], lb_ref[...])


def _combine_ln(g, wk, h, xpk, sg, su, sd, ln_w, ln_b):
    t = h.shape[0]
    tm = COMBINE_TILE
    const = lambda shape: pl.BlockSpec(shape, lambda i: (0, 0))
    return pl.pallas_call(
        _combine_kernel,
        grid=(t // tm,),
        in_specs=[pl.BlockSpec((TOP_K, tm, PACKED), lambda i: (0, i, 0)),
                  pl.BlockSpec((tm, TOP_K), lambda i: (i, 0)),
                  pl.BlockSpec((tm, D_MODEL), lambda i: (i, 0)),
                  pl.BlockSpec((tm, PACKED), lambda i: (i, 0)),
                  const((D_MODEL, D_EXPERT)), const((D_MODEL, D_EXPERT)), const((D_EXPERT, D_MODEL)),
                  const((1, D_MODEL)), const((1, D_MODEL))],
        out_specs=pl.BlockSpec((tm, D_MODEL), lambda i: (i, 0)),
        out_shape=jax.ShapeDtypeStruct((t, D_MODEL), F32),
        compiler_params=_params("parallel"),
        name="moe_combine_ln",
    )(g, wk, h, xpk, sg.astype(BF16), su.astype(BF16), sd.astype(BF16),
      ln_w.reshape(1, D_MODEL), ln_b.reshape(1, D_MODEL))


def _moe_ln(h, hpk, idx, rank, wk, counts, wg, wu, wd, layer, sg, su, sd, ln_w, ln_b):
    t = h.shape[0]
    cnt = counts[:, 0].astype(jnp.int32)
    padded = (cnt + MOE_BLOCK - 1) // MOE_BLOCK * MOE_BLOCK
    pend = jnp.cumsum(padded)
    experts = jnp.arange(N_EXPERTS, dtype=jnp.int32)
    pstart_of_pick = jnp.sum(jnp.where(idx[:, :, None] == experts, pend - padded, 0), axis=-1)
    pos = pstart_of_pick + rank
    nb = -(-(t * TOP_K + N_EXPERTS * (MOE_BLOCK - 1)) // MOE_BLOCK)
    starts = jnp.arange(nb, dtype=jnp.int32) * MOE_BLOCK
    block_e = jnp.minimum(jnp.sum((pend[None, :] <= starts[:, None]).astype(jnp.int32), axis=1), N_EXPERTS - 1)
    n_used = (pend[-1] // MOE_BLOCK).astype(jnp.int32).reshape(1)
    pos_chunks = pos.reshape(TOP_K, t // SC_CHUNK, SC_CHUNK).transpose(1, 0, 2)
    xs = _dispatch_rows(hpk, pos_chunks, nb * MOE_BLOCK)
    ys = _experts(block_e, n_used, xs, wg, wu, wd, layer)
    g = _gather_rows(ys, pos.reshape(-1)).reshape(TOP_K, t, PACKED)
    return _combine_ln(g, wk.T, h, hpk, sg, su, sd, ln_w, ln_b)


def _pad_cols(w, width=LANES):
    return jnp.pad(w, ((0, 0), (0, width - w.shape[1])))


def _even_mixer(h, batch, seq, w_in, gate_b, norm_w, conv_w, conv_b, wa, ba, wx, bx, lam):
    a4 = 4 * MIX_HALF
    ng = 2 * HEADS
    w_main = jnp.concatenate([w_in[:, :a4], w_in[:, a4 + ng:]], axis=1).astype(BF16)
    w_gate = _pad_cols(w_in[:, a4:a4 + ng]).astype(BF16)
    proj, gates = _proj(h, w_main, w_gate)
    ya = _mlstm(proj, gates, gate_b, norm_w, batch, seq)
    yb = _rglru(proj, conv_w, conv_b, wa, ba, wx, bx, lam, batch, seq)
    return ya, yb


def _odd_mixer(h, batch, seq, w_in, lam_re, lam_im, b_re, b_im, c_re, c_im, d_skip, log_dt,
               glu_w, glu_b, gate_w, gate_b, norm_w):
    c0 = MIX_HALF
    c1 = c0 + HEADS * GLA_DK
    c2 = c1 + HEADS * GLA_DK
    c3 = c2 + MIX_HALF
    c4 = c3 + MIX_HALF
    w_main = jnp.concatenate([w_in[:, :c0], _pad_heads(w_in[:, c0:c1], 1), _pad_heads(w_in[:, c1:c2], 1),
                              w_in[:, c2:c4]], axis=1).astype(BF16)
    w_low = _pad_cols(w_in[:, c4:]).astype(BF16)
    proj, glow = _proj(h, w_main, w_low)
    tables = _s5_tables(lam_re, lam_im, b_re, b_im, c_re, c_im, log_dt)
    yc = _s5(proj, tables, d_skip, glu_w, glu_b, batch, seq)
    gw = jnp.pad(_pad_heads(gate_w, 1), ((0, LANES - GLA_GATE_RANK), (0, 0))).astype(BF16)
    gb = _pad_heads(gate_b.reshape(1, -1), 1)
    yd = _gla(proj, glow, gw, gb, norm_w, batch, seq)
    return yc, yd


def kernel(x, ln1_w, ln1_b, ln2_w, ln2_b, w_out, w_in_even, mlstm_gate_b, mlstm_norm_w, lru_conv_w, lru_conv_b, lru_wa, lru_ba, lru_wx, lru_bx, lru_lambda, w_in_odd, s5_lam_re, s5_lam_im, s5_b_re, s5_b_im, s5_c_re, s5_c_im, s5_d, s5_log_dt, s5_glu_w, s5_glu_b, gla_gate_w, gla_gate_b, gla_norm_w, router_w, router_bias, exp_w_gate, exp_w_up, exp_w_down, sh_w_gate, sh_w_up, sh_w_down):
    batch, seq, d = x.shape
    h = x.reshape(batch * seq, d)
    for layer in range(DEPTH):
        j = layer // 2
        if layer % 2 == 0:
            y1, y2 = _even_mixer(h, batch, seq, w_in_even[j], mlstm_gate_b[j], mlstm_norm_w[j],
                                 lru_conv_w[j], lru_conv_b[j], lru_wa[j], lru_ba[j], lru_wx[j],
                                 lru_bx[j], lru_lambda[j])
        else:
            y1, y2 = _odd_mixer(h, batch, seq, w_in_odd[j], s5_lam_re[j], s5_lam_im[j], s5_b_re[j],
                                s5_b_im[j], s5_c_re[j], s5_c_im[j], s5_d[j], s5_log_dt[j],
                                s5_glu_w[j], s5_glu_b[j], gla_gate_w[j], gla_gate_b[j], gla_norm_w[j])
        h, hpk, idx, rank, wk, counts = _out_proj_ln_route(y1, y2, h, w_out[layer], ln1_w[layer], ln1_b[layer],
                                                           router_w[layer], router_bias[layer])
        h = _moe_ln(h, hpk, idx, rank, wk, counts, exp_w_gate, exp_w_up, exp_w_down, layer,
                    sh_w_gate[layer], sh_w_up[layer], sh_w_down[layer], ln2_w[layer], ln2_b[layer])
    return h.reshape(batch, seq, d)
```

```python
import functools
import math

import jax
import jax.numpy as jnp
from jax import lax
from jax.experimental import pallas as pl
from jax.experimental.pallas import tpu as pltpu
from jax.experimental.pallas import tpu_sc as plsc

F32 = jnp.float32
BF16 = jnp.bfloat16

D_MODEL = 1024
DEPTH = 2
MIX_HALF = 512
HEADS = 4
HEAD_DIM = 128
GLA_DK = 64
GLA_CHUNK = 64
GLA_GATE_RANK = 16
GLA_GATE_TEMP = 16.0
LRU_C = 8.0
LRU_CONV = 4
S5_GROUP = 16
S5_GROUPS = 32
S5_STATE = 64
S5_LANES = S5_GROUPS * S5_STATE
S5_BLOCKS = 4
N_EXPERTS = 64
N_GROUPS = 8
GROUP_SIZE = N_EXPERTS // N_GROUPS
TOP_K = 8
TOPK_GROUPS = 4
D_EXPERT = 256
ROUTED_SCALE = 2.5
ALPHA = (2.0 * DEPTH) ** 0.25
EPS = 1e-5
LANES = 128
SUBLANES = 8
NEG_INF = float("-inf")

VMEM_LIMIT = 56 * 1024 * 1024

MLSTM_CHUNK = 128
MLSTM_TILE = 1024
LRU_TILE = 1024
LRU_LOG_STEPS = 3
LRU_UNROLL = 4
S5_TILE = 512
S5_LOG_STEPS = 3
S5_UNROLL = True
GLA_UNROLL = 4
GLA_TILE = 1024
PROJ_TILE = 512
OUT_TILE = 512
MOE_BLOCK = 1024
COMBINE_TILE = 256
PACKED = D_MODEL // 2
SC_CHUNK = 64
SC_CORES = 2
SC_SUBCORES = 16
SC_WORKERS = SC_CORES * SC_SUBCORES


def _params(*sem):
    return pltpu.CompilerParams(dimension_semantics=sem, vmem_limit_bytes=VMEM_LIMIT)


def _split3(x):
    hi = x.astype(BF16)
    r1 = x - hi.astype(F32)
    mid = r1.astype(BF16)
    lo = (r1 - mid.astype(F32)).astype(BF16)
    return hi, mid, lo


def _dot(a, b):
    return jnp.dot(a, b, preferred_element_type=F32)


def _dot_nt(a, b):
    return lax.dot_general(a, b, (((1,), (1,)), ((), ())), preferred_element_type=F32)


def _dot_tn(a, b):
    return lax.dot_general(a, b, (((0,), (0,)), ((), ())), preferred_element_type=F32)


def _exact_left01(mask01_bf16, x):
    hi, mid, lo = _split3(x)
    return _dot(mask01_bf16, hi) + _dot(mask01_bf16, mid) + _dot(mask01_bf16, lo)


def _exact_right01(x, mask01_bf16):
    hi, mid, lo = _split3(x)
    return _dot(hi, mask01_bf16) + _dot(mid, mask01_bf16) + _dot(lo, mask01_bf16)


def _log_sigmoid(x):
    return jnp.minimum(x, 0.0) - jnp.log(1.0 + jnp.exp(-jnp.abs(x)))


def _sigmoid(x):
    return 1.0 / (1.0 + jnp.exp(-x))


def _gelu_tanh(x):
    c = math.sqrt(2.0 / math.pi)
    return 0.5 * x * (1.0 + jnp.tanh(c * (x + 0.044715 * (x * x * x))))


def _layer_norm(z, w, b):
    mu = jnp.mean(z, axis=-1, keepdims=True)
    zc = z - mu
    return zc * lax.rsqrt(jnp.mean(zc * zc, axis=-1, keepdims=True) + EPS) * w + b


def _proj_kernel(x_ref, w_ref, wg_ref, o_ref, og_ref):
    x = x_ref[...].astype(BF16)
    o_ref[...] = _dot(x, w_ref[...])
    og_ref[...] = _dot(x, wg_ref[...])


def _proj(x, w_main, w_small):
    t, d = x.shape
    n = w_main.shape[1]
    tm = PROJ_TILE
    return pl.pallas_call(
        _proj_kernel,
        grid=(t // tm,),
        in_specs=[pl.BlockSpec((tm, d), lambda i: (i, 0)),
                  pl.BlockSpec((d, n), lambda i: (0, 0)),
                  pl.BlockSpec((d, LANES), lambda i: (0, 0))],
        out_specs=[pl.BlockSpec((tm, n), lambda i: (i, 0)),
                   pl.BlockSpec((tm, LANES), lambda i: (i, 0))],
        out_shape=[jax.ShapeDtypeStruct((t, n), F32), jax.ShapeDtypeStruct((t, LANES), F32)],
        compiler_params=_params("parallel"),
        name="in_proj",
    )(x, w_main, w_small)


def _mlstm_kernel(q_ref, k_ref, v_ref, o_ref, gc_ref, gr_ref, bc_ref, br_ref, nw_ref,
                  y_ref, c_ref, m_ref, *, chunk, n_chunks):
    L = chunk

    @pl.when(pl.program_id(1) == 0)
    def _():
        c_ref[...] = jnp.zeros_like(c_ref)
        m_ref[...] = jnp.zeros_like(m_ref)

    ri = lax.broadcasted_iota(jnp.int32, (L, L), 0)
    ci = lax.broadcasted_iota(jnp.int32, (L, L), 1)
    causal = ci <= ri
    tril = causal.astype(BF16)
    triu = (ri <= ci).astype(BF16)
    ones_v = jnp.ones((L, HEAD_DIM), BF16)
    scale = HEAD_DIM ** -0.5

    def body(c, carry):
        r0 = pl.multiple_of(c * L, L)
        g_col = gc_ref[pl.ds(r0, L), :] + bc_ref[...]
        g_row = gr_ref[c] + br_ref[...]
        b_col_all = _exact_left01(tril, _log_sigmoid(g_col))
        b_row_all = _exact_right01(_log_sigmoid(g_row), triu)
        for h in range(HEADS):
            lo = h * HEAD_DIM
            q = q_ref[pl.ds(r0, L), lo:lo + HEAD_DIM].astype(BF16)
            k = k_ref[pl.ds(r0, L), lo:lo + HEAD_DIM] * scale
            v = v_ref[pl.ds(r0, L), lo:lo + HEAD_DIM].astype(BF16)
            v_aug = jnp.concatenate([v, ones_v], axis=1)
            i_rep = jnp.broadcast_to(g_col[:, h:h + 1], (L, LANES))
            b_rep = jnp.broadcast_to(b_col_all[:, HEADS + h:HEADS + h + 1], (L, LANES))
            i_row = g_row[h:h + 1, :]
            b_row = b_row_all[HEADS + h:HEADS + h + 1, :]
            b_last = b_rep[L - 1:L, :]
            m_prev = m_ref[h:h + 1, :]
            c_prev = c_ref[h]

            d_mat = jnp.where(causal, b_rep - b_row + i_row, NEG_INF)
            m_inter = b_rep + m_prev
            m_i = jnp.maximum(m_inter, jnp.max(d_mat, axis=1, keepdims=True))
            s = _dot_nt(q, k.astype(BF16)) * jnp.exp(d_mat - m_i)
            w_inter = jnp.exp(m_inter - m_i)
            intra = _dot(s.astype(BF16), v_aug)
            inter = _dot(q, c_prev.astype(BF16))
            num = intra[:, :HEAD_DIM] + w_inter * inter[:, :HEAD_DIM]
            den = intra[:, HEAD_DIM:] + w_inter * inter[:, HEAD_DIM:]
            hh = num / jnp.maximum(jnp.abs(den), jnp.exp(-m_i))

            w_loc = b_last - b_rep + i_rep
            m_loc = jnp.max(w_loc, axis=0, keepdims=True)
            kp = (k * jnp.exp(w_loc - m_loc)).astype(BF16)
            c_loc = _dot_tn(kp, v_aug)
            m_new = jnp.maximum(b_last + m_prev, m_loc)
            keep = jnp.exp(b_last + m_prev - m_new)
            add = jnp.exp(m_loc - m_new)
            c_ref[h] = (jnp.concatenate([keep, keep], axis=1) * c_prev
                        + jnp.concatenate([add, add], axis=1) * c_loc)
            m_ref[h:h + 1, :] = m_new

            hc = hh - jnp.mean(hh, axis=-1, keepdims=True)
            yn = hc * lax.rsqrt(jnp.mean(hc * hc, axis=-1, keepdims=True) + EPS)
            og = o_ref[pl.ds(r0, L), lo:lo + HEAD_DIM]
            y_ref[pl.ds(r0, L), lo:lo + HEAD_DIM] = yn * nw_ref[:, lo:lo + HEAD_DIM] * _sigmoid(og)
        return carry

    lax.fori_loop(0, n_chunks, body, 0)


def _mlstm(proj, gates, gate_b, norm_w, batch, seq):
    t = batch * seq
    L = MLSTM_CHUNK
    assert L == LANES, "the kernel keeps per-row gate terms replicated over one vreg of lanes"
    ts = MLSTM_TILE
    nj = seq // ts
    nc = ts // L
    g_row = gates[:, :2 * HEADS].reshape(t // L, L, 2 * HEADS).transpose(0, 2, 1)
    b_col = jnp.zeros((1, LANES), F32).at[0, :2 * HEADS].set(gate_b)
    b_row = gate_b.reshape(2 * HEADS, 1)
    blk = lambda col: pl.BlockSpec((ts, MIX_HALF), lambda b, j, col=col: (b * nj + j, col))
    kern = functools.partial(_mlstm_kernel, chunk=L, n_chunks=nc)
    return pl.pallas_call(
        kern,
        grid=(batch, nj),
        in_specs=[blk(0), blk(1), blk(2), blk(3),
                  pl.BlockSpec((ts, LANES), lambda b, j: (b * nj + j, 0)),
                  pl.BlockSpec((nc, 2 * HEADS, L), lambda b, j: (b * nj + j, 0, 0)),
                  pl.BlockSpec((1, LANES), lambda b, j: (0, 0)),
                  pl.BlockSpec((2 * HEADS, 1), lambda b, j: (0, 0)),
                  pl.BlockSpec((1, MIX_HALF), lambda b, j: (0, 0))],
        out_specs=pl.BlockSpec((ts, MIX_HALF), lambda b, j: (b * nj + j, 0)),
        out_shape=jax.ShapeDtypeStruct((t, MIX_HALF), F32),
        scratch_shapes=[pltpu.VMEM((HEADS, HEAD_DIM, 2 * HEAD_DIM), F32),
                        pltpu.VMEM((8, LANES), F32)],
        compiler_params=_params("arbitrary", "arbitrary"),
        name="mlstm",
    )(proj, proj, proj, proj, gates, g_row, b_col, b_row, norm_w.reshape(1, MIX_HALF))


def _rglru_kernel(xb_ref, gb_ref, cw_ref, cb_ref, wa_ref, ba_ref, wx_ref, bx_ref, lam_ref,
                  y_ref, xext_ref, h_ref, a_ref, u_ref, *, tile):
    @pl.when(pl.program_id(1) == 0)
    def _():
        xext_ref[0:8, :] = jnp.zeros((8, MIX_HALF), F32)
        h_ref[...] = jnp.zeros_like(h_ref)

    x = xb_ref[...]
    xext_ref[8:8 + tile, :] = x
    xc = cb_ref[...] + cw_ref[LRU_CONV - 1:LRU_CONV, :] * x
    for tap in range(LRU_CONV - 1):
        back = LRU_CONV - 1 - tap
        xc = xc + cw_ref[tap:tap + 1, :] * xext_ref[8 - back:8 - back + tile, :]
    xext_ref[0:8, :] = x[tile - 8:tile, :]

    xc16 = xc.astype(BF16)
    r_parts, i_parts = [], []
    for h in range(HEADS):
        lo = h * HEAD_DIM
        xh = xc16[:, lo:lo + HEAD_DIM]
        r_parts.append(_dot(xh, wa_ref[h]))
        i_parts.append(_dot(xh, wx_ref[h]))
    r = _sigmoid(jnp.concatenate(r_parts, axis=1) + ba_ref[...])
    ig = _sigmoid(jnp.concatenate(i_parts, axis=1) + bx_ref[...])
    lam = lam_ref[...]
    softplus_neg = jnp.maximum(-lam, 0.0) + jnp.log(1.0 + jnp.exp(-jnp.abs(lam)))
    log_a = -LRU_C * r * softplus_neg
    a = jnp.exp(log_a)
    th = jnp.tanh(log_a)
    u = jnp.sqrt(-2.0 * th / (1.0 - th)) * ig * xc

    a_ref[...] = a
    u_ref[...] = u
    rows = lax.broadcasted_iota(jnp.int32, (SUBLANES, MIX_HALF), 0)

    def group(i, h_prev):
        r0 = pl.multiple_of(i * SUBLANES, SUBLANES)
        ag = a_ref[pl.ds(r0, SUBLANES), :]
        ug = u_ref[pl.ds(r0, SUBLANES), :]
        for k in range(LRU_LOG_STEPS):
            keep = rows >= (1 << k)
            ug = ag * jnp.where(keep, pltpu.roll(ug, 1 << k, 0), 0.0) + ug
            ag = ag * jnp.where(keep, pltpu.roll(ag, 1 << k, 0), 1.0)
        hg = ug + ag * h_prev
        u_ref[pl.ds(r0, SUBLANES), :] = hg
        return hg[SUBLANES - 1:SUBLANES, :]

    h_last = lax.fori_loop(0, tile // SUBLANES, group, h_ref[0:1, :], unroll=LRU_UNROLL)
    h_ref[...] = jnp.broadcast_to(h_last, h_ref.shape)
    y_ref[...] = u_ref[...] * _gelu_tanh(gb_ref[...])


def _rglru(proj, conv_w, conv_b, wa, ba, wx, bx, lam, batch, seq):
    t = batch * seq
    ts = LRU_TILE
    nj = seq // ts
    row = lambda a: a.reshape(1, MIX_HALF)
    const2 = lambda shape: pl.BlockSpec(shape, lambda b, j: (0, 0))
    const3 = lambda shape: pl.BlockSpec(shape, lambda b, j: (0, 0, 0))
    blk = lambda col: pl.BlockSpec((ts, MIX_HALF), lambda b, j, col=col: (b * nj + j, col))
    return pl.pallas_call(
        functools.partial(_rglru_kernel, tile=ts),
        grid=(batch, nj),
        in_specs=[blk(4), blk(5), const2((LRU_CONV, MIX_HALF)), const2((1, MIX_HALF)),
                  const3((HEADS, HEAD_DIM, HEAD_DIM)), const2((1, MIX_HALF)),
                  const3((HEADS, HEAD_DIM, HEAD_DIM)), const2((1, MIX_HALF)), const2((1, MIX_HALF))],
        out_specs=pl.BlockSpec((ts, MIX_HALF), lambda b, j: (b * nj + j, 0)),
        out_shape=jax.ShapeDtypeStruct((t, MIX_HALF), F32),
        scratch_shapes=[pltpu.VMEM((ts + 8, MIX_HALF), F32), pltpu.VMEM((8, MIX_HALF), F32),
                        pltpu.VMEM((ts, MIX_HALF), F32), pltpu.VMEM((ts, MIX_HALF), F32)],
        compiler_params=_params("arbitrary", "arbitrary"),
        name="rglru",
    )(proj, proj, conv_w, row(conv_b), wa.astype(BF16), row(ba), wx.astype(BF16), row(bx), row(lam))


def _s5_kernel(u_ref, bre_ref, bim_ref, cre_ref, cim_ref, mre_ref, mim_ref, pre_ref, pim_ref, d_ref, gw_ref,
               gb_ref, y_ref, xr_ref, xi_ref, cr_ref, ci_ref, *, tile):
    @pl.when(pl.program_id(1) == 0)
    def _():
        cr_ref[...] = jnp.zeros_like(cr_ref)
        ci_ref[...] = jnp.zeros_like(ci_ref)

    u = u_ref[...]
    u16 = u.astype(BF16)
    blk_c = MIX_HALF // S5_BLOCKS
    blk_s = S5_LANES // S5_BLOCKS
    parts = []
    for j in range(S5_BLOCKS):
        lanes = slice(j * blk_s, (j + 1) * blk_s)
        uj = u16[:, j * blk_c:(j + 1) * blk_c]
        xr_ref[:, lanes] = _dot(uj, bre_ref[j])
        xi_ref[:, lanes] = _dot(uj, bim_ref[j])

        def group(i, carry, lanes=lanes):
            cr, ci = carry
            r0 = pl.multiple_of(i * SUBLANES, SUBLANES)
            xr = xr_ref[pl.ds(r0, SUBLANES), lanes]
            xi = xi_ref[pl.ds(r0, SUBLANES), lanes]
            for k in range(S5_LOG_STEPS):
                sr = pltpu.roll(xr, 1 << k, 0)
                si = pltpu.roll(xi, 1 << k, 0)
                mr = mre_ref[k, :, lanes]
                mi = mim_ref[k, :, lanes]
                xr, xi = xr + mr * sr - mi * si, xi + mr * si + mi * sr
            pr = pre_ref[:, lanes]
            pi = pim_ref[:, lanes]
            xr, xi = xr + pr * cr - pi * ci, xi + pr * ci + pi * cr
            xr_ref[pl.ds(r0, SUBLANES), lanes] = xr
            xi_ref[pl.ds(r0, SUBLANES), lanes] = xi
            return xr[SUBLANES - 1:SUBLANES, :], xi[SUBLANES - 1:SUBLANES, :]

        cr, ci = lax.fori_loop(0, tile // SUBLANES, group, (cr_ref[0:1, lanes], ci_ref[0:1, lanes]),
                               unroll=S5_UNROLL)
        cr_ref[0:1, lanes] = cr
        ci_ref[0:1, lanes] = ci
        parts.append(_dot(xr_ref[:, lanes].astype(BF16), cre_ref[j])
                     - _dot(xi_ref[:, lanes].astype(BF16), cim_ref[j]))
    y = jnp.concatenate(parts, axis=1) + d_ref[...] * u
    g = _gelu_tanh(y)
    y_ref[...] = g * _sigmoid(_dot(g.astype(BF16), gw_ref[...]) + gb_ref[...])


def _s5_tables(lam_re, lam_im, b_re, b_im, c_re, c_im, log_dt):
    lr, li = lam_re.astype(F32), lam_im.astype(F32)
    dt = jnp.exp(log_dt.astype(F32))[:, None]
    mag = jnp.exp(lr * dt)
    abar_re = mag * jnp.cos(li * dt)
    abar_im = mag * jnp.sin(li * dt)
    den = lr * lr + li * li
    nr = abar_re - 1.0
    coef_re = (nr * lr + abar_im * li) / den
    coef_im = (abar_im * lr - nr * li) / den
    bbar_re = coef_re[..., None] * b_re - coef_im[..., None] * b_im
    bbar_im = coef_re[..., None] * b_im + coef_im[..., None] * b_re
    gpb = S5_GROUPS // S5_BLOCKS
    eye = jnp.eye(gpb, dtype=F32)

    def in_map(bb):
        bb = bb.reshape(S5_BLOCKS, gpb, S5_STATE, S5_GROUP)
        return jnp.einsum("jgph,gk->jghkp", bb, eye).reshape(S5_BLOCKS, gpb * S5_GROUP, gpb * S5_STATE)

    def out_map(cc):
        cc = cc.reshape(S5_BLOCKS, gpb, S5_GROUP, S5_STATE)
        return jnp.einsum("jghp,gk->jgpkh", cc, eye).reshape(S5_BLOCKS, gpb * S5_STATE, gpb * S5_GROUP)

    def power(n):
        n = jnp.asarray(n, F32)[..., None, None]
        pmag = jnp.exp(n * (lr * dt))
        shape = n.shape[:-2] + (S5_LANES,)
        return (pmag * jnp.cos(n * (li * dt))).reshape(shape), (pmag * jnp.sin(n * (li * dt))).reshape(shape)

    row = jnp.arange(SUBLANES)
    step = 2 ** jnp.arange(S5_LOG_STEPS)
    s_re, s_im = power(step)
    keep = (row[None, :] >= step[:, None])[..., None]
    m_re = jnp.where(keep, s_re[:, None, :], 0.0)
    m_im = jnp.where(keep, s_im[:, None, :], 0.0)
    p_re, p_im = power(row + 1)
    return (in_map(bbar_re).astype(BF16), in_map(bbar_im).astype(BF16),
            out_map(c_re.astype(F32)).astype(BF16), out_map(c_im.astype(F32)).astype(BF16),
            m_re, m_im, p_re, p_im)


def _s5(proj, tables, d_skip, glu_w, glu_b, batch, seq):
    t = batch * seq
    ts = S5_TILE
    nj = seq // ts
    bre, bim, cre, cim, m_re, m_im, p_re, p_im = tables
    blk_c = MIX_HALF // S5_BLOCKS
    blk_s = S5_LANES // S5_BLOCKS
    const2 = lambda shape: pl.BlockSpec(shape, lambda b, j: (0, 0))
    const3 = lambda shape: pl.BlockSpec(shape, lambda b, j: (0, 0, 0))
    return pl.pallas_call(
        functools.partial(_s5_kernel, tile=ts),
        grid=(batch, nj),
        in_specs=[pl.BlockSpec((ts, MIX_HALF), lambda b, j: (b * nj + j, 0)),
                  const3((S5_BLOCKS, blk_c, blk_s)), const3((S5_BLOCKS, blk_c, blk_s)),
                  const3((S5_BLOCKS, blk_s, blk_c)), const3((S5_BLOCKS, blk_s, blk_c)),
                  const3(m_re.shape), const3(m_im.shape), const2(p_re.shape), const2(p_im.shape),
                  const2((1, MIX_HALF)), const2((MIX_HALF, MIX_HALF)), const2((1, MIX_HALF))],
        out_specs=pl.BlockSpec((ts, MIX_HALF), lambda b, j: (b * nj + j, 0)),
        out_shape=jax.ShapeDtypeStruct((t, MIX_HALF), F32),
        scratch_shapes=[pltpu.VMEM((ts, S5_LANES), F32), pltpu.VMEM((ts, S5_LANES), F32),
                        pltpu.VMEM((8, S5_LANES), F32), pltpu.VMEM((8, S5_LANES), F32)],
        compiler_params=_params("arbitrary", "arbitrary"),
        name="s5",
    )(proj, bre, bim, cre, cim, m_re, m_im, p_re, p_im, d_skip.reshape(1, MIX_HALF), glu_w.astype(BF16),
      glu_b.reshape(1, MIX_HALF))


def _gla_kernel(q_ref, k_ref, v_ref, r_ref, gl_ref, gw_ref, gb_ref, nw_ref, y_ref,
                st_ref, qd_ref, ki_ref, ke_ref, v16_ref, dec_ref, o_ref, *, tile, chunk):
    L = chunk
    nc = tile // L

    @pl.when(pl.program_id(1) == 0)
    def _():
        st_ref[...] = jnp.zeros_like(st_ref)

    z = _dot(gl_ref[...].astype(BF16), gw_ref[...]) + gb_ref[...]
    bcum = _log_sigmoid(z) * (1.0 / GLA_GATE_TEMP)
    row_in_chunk = lax.broadcasted_iota(jnp.int32, bcum.shape, 0) & (L - 1)
    s = 1
    while s < L:
        bcum = bcum + jnp.where(row_in_chunk >= s, pltpu.roll(bcum, s, 0), 0.0)
        s *= 2
    b3 = bcum.reshape(nc, L, MIX_HALF)
    b_last = b3[:, L - 1:L, :]
    k = k_ref[...]
    qd_ref[...] = (q_ref[...] * (GLA_DK ** -0.5) * jnp.exp(bcum)).astype(BF16)
    ki_ref[...] = (k * jnp.exp(-bcum)).astype(BF16)
    ke_ref[...] = (k.reshape(nc, L, MIX_HALF) * jnp.exp(b_last - b3)).reshape(tile, MIX_HALF).astype(BF16)
    v16_ref[...] = v_ref[...].astype(BF16)
    dec_ref[...] = jnp.exp(b_last)

    ri = lax.broadcasted_iota(jnp.int32, (L, L), 0)
    ci = lax.broadcasted_iota(jnp.int32, (L, L), 1)
    causal = ci <= ri

    def body(c, carry):
        r0 = pl.multiple_of(c * L, L)
        dec = dec_ref[c]
        for h in range(HEADS):
            lo = h * HEAD_DIM
            q_dec = qd_ref[pl.ds(r0, L), lo:lo + HEAD_DIM]
            v = v16_ref[pl.ds(r0, L), lo:lo + HEAD_DIM]
            st = st_ref[h]
            att = jnp.where(causal, _dot_nt(q_dec, ki_ref[pl.ds(r0, L), lo:lo + HEAD_DIM]), 0.0)
            o_ref[pl.ds(r0, L), lo:lo + HEAD_DIM] = (_dot(att.astype(BF16), v)
                                                     + _dot_nt(q_dec, st.astype(BF16)))
            st_ref[h] = dec[:, lo:lo + HEAD_DIM] * st + _dot_tn(v, ke_ref[pl.ds(r0, L), lo:lo + HEAD_DIM])
        return carry

    lax.fori_loop(0, nc, body, 0, unroll=GLA_UNROLL)

    rg = r_ref[...]
    gate = nw_ref[...] * (rg * _sigmoid(rg))
    for h in range(HEADS):
        lo = h * HEAD_DIM
        o = o_ref[:, lo:lo + HEAD_DIM]
        yn = o * lax.rsqrt(jnp.mean(o * o, axis=-1, keepdims=True) + EPS)
        y_ref[:, lo:lo + HEAD_DIM] = yn * gate[:, lo:lo + HEAD_DIM]


def _gla(proj, glow, gate_w, gate_b, norm_w, batch, seq):
    t = batch * seq
    ts = GLA_TILE
    nj = seq // ts
    blk = lambda col: pl.BlockSpec((ts, MIX_HALF), lambda b, j, col=col: (b * nj + j, col))
    const2 = lambda shape: pl.BlockSpec(shape, lambda b, j: (0, 0))
    return pl.pallas_call(
        functools.partial(_gla_kernel, tile=ts, chunk=GLA_CHUNK),
        grid=(batch, nj),
        in_specs=[blk(1), blk(2), blk(3), blk(4),
                  pl.BlockSpec((ts, LANES), lambda b, j: (b * nj + j, 0)),
                  const2((LANES, MIX_HALF)), const2((1, MIX_HALF)), const2((1, MIX_HALF))],
        out_specs=pl.BlockSpec((ts, MIX_HALF), lambda b, j: (b * nj + j, 0)),
        out_shape=jax.ShapeDtypeStruct((t, MIX_HALF), F32),
        scratch_shapes=[pltpu.VMEM((HEADS, HEAD_DIM, HEAD_DIM), F32),
                        pltpu.VMEM((ts, MIX_HALF), BF16), pltpu.VMEM((ts, MIX_HALF), BF16),
                        pltpu.VMEM((ts, MIX_HALF), BF16), pltpu.VMEM((ts, MIX_HALF), BF16),
                        pltpu.VMEM((ts // GLA_CHUNK, 1, MIX_HALF), F32),
                        pltpu.VMEM((ts, MIX_HALF), F32)],
        compiler_params=_params("arbitrary", "arbitrary"),
        name="gla",
    )(proj, proj, proj, proj, glow, gate_w, gate_b, norm_w.reshape(1, MIX_HALF))


def _pad_heads(w, axis):
    shape = list(w.shape)
    shape[axis:axis + 1] = [HEADS, GLA_DK]
    w = w.reshape(shape)
    pad = [(0, 0)] * w.ndim
    pad[axis + 1] = (0, HEAD_DIM - GLA_DK)
    w = jnp.pad(w, pad)
    shape[axis:axis + 2] = [HEADS * HEAD_DIM]
    return w.reshape(shape)


def _pack_bf16_pairs(z):
    hi = lax.bitcast_convert_type(z[:, :PACKED].astype(BF16).astype(F32), jnp.uint32)
    lo = lax.bitcast_convert_type(z[:, PACKED:].astype(BF16).astype(F32), jnp.uint32)
    word = (hi & jnp.uint32(0xFFFF0000)) | lax.shift_right_logical(lo, jnp.uint32(16))
    return lax.bitcast_convert_type(word, jnp.int32)


def _unpack_bf16_pairs(p):
    word = lax.bitcast_convert_type(p, jnp.uint32)
    hi = lax.bitcast_convert_type(word & jnp.uint32(0xFFFF0000), F32)
    lo = lax.bitcast_convert_type(lax.shift_left(word, jnp.uint32(16)), F32)
    return hi, lo


def _out_kernel(ya_ref, yb_ref, h_ref, w_ref, lw_ref, lb_ref, rw_ref, rb_ref,
                o_ref, opk_ref, idx_ref, rank_ref, wk_ref, cnt_ref, base_ref, *, tile):
    mixed = jnp.concatenate([ya_ref[...], yb_ref[...]], axis=1).astype(BF16)
    z = ALPHA * h_ref[...] + _dot(mixed, w_ref[...])
    out = _layer_norm(z, lw_ref[...], lb_ref[...])
    o_ref[...] = out
    opk_ref[...] = _pack_bf16_pairs(out)
    _route_tile(out, rw_ref, rb_ref, idx_ref, rank_ref, wk_ref, cnt_ref, base_ref, tile)


def _out_proj_ln_route(ya, yb, h, w_out, ln_w, ln_b, router_w, router_bias):
    t = h.shape[0]
    tm = OUT_TILE
    const = lambda shape: pl.BlockSpec(shape, lambda i: (0, 0))
    per_tok = lambda dt: jax.ShapeDtypeStruct((TOP_K, t), dt)
    tok_blk = pl.BlockSpec((TOP_K, tm), lambda i: (0, i))
    return pl.pallas_call(
        functools.partial(_out_kernel, tile=tm),
        grid=(t // tm,),
        in_specs=[pl.BlockSpec((tm, MIX_HALF), lambda i: (i, 0)),
                  pl.BlockSpec((tm, MIX_HALF), lambda i: (i, 0)),
                  pl.BlockSpec((tm, D_MODEL), lambda i: (i, 0)),
                  const((D_MODEL, D_MODEL)), const((1, D_MODEL)), const((1, D_MODEL)),
                  const((N_EXPERTS, D_MODEL)), const((N_EXPERTS, 1))],
        out_specs=[pl.BlockSpec((tm, D_MODEL), lambda i: (i, 0)),
                   pl.BlockSpec((tm, PACKED), lambda i: (i, 0)),
                   tok_blk, tok_blk, tok_blk, const((N_EXPERTS, LANES))],
        out_shape=[jax.ShapeDtypeStruct((t, D_MODEL), F32), jax.ShapeDtypeStruct((t, PACKED), jnp.int32),
                   per_tok(jnp.int32), per_tok(jnp.int32), per_tok(F32),
                   jax.ShapeDtypeStruct((N_EXPERTS, LANES), F32)],
        scratch_shapes=[pltpu.VMEM((N_EXPERTS, LANES), F32)],
        compiler_params=_params("arbitrary"),
        name="out_proj_ln_route",
    )(ya, yb, h, w_out.astype(BF16), ln_w.reshape(1, D_MODEL), ln_b.reshape(1, D_MODEL),
      router_w.T, router_bias.reshape(N_EXPERTS, 1))


def _first_index(hit, idx, big):
    return jnp.min(jnp.where(hit, idx, big), axis=0, keepdims=True)


def _route_tile(h, w_ref, b_ref, idx_ref, rank_ref, wk_ref, cnt_ref, base_ref, tile):
    @pl.when(pl.program_id(0) == 0)
    def _():
        base_ref[...] = jnp.zeros_like(base_ref)

    h_hi, h_mid, _ = _split3(h)
    w_hi, w_mid, _ = _split3(w_ref[...])
    logits = _dot_nt(w_hi, h_hi) + _dot_nt(w_hi, h_mid) + _dot_nt(w_mid, h_hi)
    scores = _sigmoid(logits)
    biased = scores + b_ref[...]

    sub = lax.broadcasted_iota(jnp.int32, (GROUP_SIZE, tile), 0)
    grp_rows = []
    for g in range(N_GROUPS):
        xg = biased[g * GROUP_SIZE:(g + 1) * GROUP_SIZE, :]
        m1 = jnp.max(xg, axis=0, keepdims=True)
        i1 = _first_index(xg == m1, sub, GROUP_SIZE)
        m2 = jnp.max(jnp.where(sub == i1, NEG_INF, xg), axis=0, keepdims=True)
        grp_rows.append(m1 + m2)
    gs = jnp.concatenate(grp_rows, axis=0)
    gsel = jnp.zeros((N_GROUPS, tile), F32)
    for _ in range(TOPK_GROUPS):
        mx = jnp.max(gs, axis=0, keepdims=True)
        hit = sub == _first_index(gs == mx, sub, N_GROUPS)
        gsel = jnp.where(hit, 1.0, gsel)
        gs = jnp.where(hit, NEG_INF, gs)
    emask = jnp.concatenate(
        [jnp.broadcast_to(gsel[g:g + 1, :], (GROUP_SIZE, tile)) for g in range(N_GROUPS)], axis=0)

    eidx = lax.broadcasted_iota(jnp.int32, (N_EXPERTS, tile), 0)
    cand = jnp.where(emask > 0.5, biased, NEG_INF)
    sel = jnp.zeros((N_EXPERTS, tile), F32)
    hits, picks = [], []
    for _ in range(TOP_K):
        mx = jnp.max(cand, axis=0, keepdims=True)
        first = _first_index(cand == mx, eidx, N_EXPERTS)
        hit = eidx == first
        hits.append(hit)
        picks.append(first)
        sel = jnp.where(hit, 1.0, sel)
        cand = jnp.where(hit, NEG_INF, cand)
    picked = jnp.where(sel > 0.5, scores, 0.0)
    wts = picked / jnp.sum(picked, axis=0, keepdims=True) * ROUTED_SCALE

    ri = lax.broadcasted_iota(jnp.int32, (tile, tile), 0)
    ci = lax.broadcasted_iota(jnp.int32, (tile, tile), 1)
    before = (ri < ci).astype(BF16)
    prior = _dot(sel.astype(BF16), before) + base_ref[:, 0:1]
    ranks = [jnp.sum(jnp.where(hit, prior, 0.0), axis=0, keepdims=True) for hit in hits]
    wsel = [jnp.sum(jnp.where(hit, wts, 0.0), axis=0, keepdims=True) for hit in hits]
    idx_ref[...] = jnp.concatenate(picks, axis=0)
    rank_ref[...] = jnp.concatenate(ranks, axis=0).astype(jnp.int32)
    wk_ref[...] = jnp.concatenate(wsel, axis=0)
    total = base_ref[...] + jnp.sum(sel, axis=1, keepdims=True)
    base_ref[...] = total
    cnt_ref[...] = total


def _silu(x):
    return x * _sigmoid(x)


def _sc_mesh():
    return plsc.VectorSubcoreMesh(core_axis_name="c", subcore_axis_name="s")


def _sc_worker_id():
    return lax.axis_index("s") * SC_CORES + lax.axis_index("c")


def _dispatch_rows(xpk, pos_chunks, n_rows):
    t = xpk.shape[0]
    n_ch = t // SC_WORKERS // SC_CHUNK

    @functools.partial(
        pl.kernel, mesh=_sc_mesh(),
        out_type=jax.ShapeDtypeStruct((n_rows, PACKED), jnp.int32),
        scratch_types=[pltpu.VMEM((TOP_K, SC_CHUNK), jnp.int32),
                       pltpu.VMEM((SC_CHUNK, PACKED), jnp.int32),
                       pltpu.SemaphoreType.DMA],
        name="moe_dispatch",
    )
    def scatter(x_hbm, pos_hbm, out_hbm, idx_v, rows_v, sem):
        wid = _sc_worker_id()

        @pl.loop(0, n_ch)
        def _(c):
            chunk = wid * n_ch + c
            off = pl.multiple_of(chunk * SC_CHUNK, SC_CHUNK)
            pltpu.sync_copy(pos_hbm.at[chunk], idx_v)
            pltpu.sync_copy(x_hbm.at[pl.ds(off, SC_CHUNK)], rows_v)
            copies = [pltpu.async_copy(rows_v, out_hbm.at[idx_v.at[k]], sem) for k in range(TOP_K)]
            for cp in copies:
                cp.wait()

    return scatter(xpk, pos_chunks)


def _gather_rows(table, idx):
    n = idx.shape[0]
    per_w = n // SC_WORKERS
    n_ch = per_w // SC_CHUNK

    @functools.partial(
        pl.kernel, mesh=_sc_mesh(),
        out_type=jax.ShapeDtypeStruct((n, PACKED), jnp.int32),
        scratch_types=[pltpu.VMEM((SC_CHUNK,), jnp.int32),
                       pltpu.VMEM((SC_CHUNK, PACKED), jnp.int32),
                       pltpu.SemaphoreType.DMA],
        name="moe_gather",
    )
    def gather(table_hbm, idx_hbm, out_hbm, idx_v, rows_v, sem):
        base = _sc_worker_id() * per_w

        @pl.loop(0, n_ch)
        def _(c):
            off = pl.multiple_of(base + c * SC_CHUNK, SC_CHUNK)
            pltpu.sync_copy(idx_hbm.at[pl.ds(off, SC_CHUNK)], idx_v)
            pltpu.async_copy(table_hbm.at[idx_v], rows_v, sem).wait()
            pltpu.sync_copy(rows_v, out_hbm.at[pl.ds(off, SC_CHUNK)])

    return gather(table, idx)


def _unpacked_bf16(p):
    hi, lo = _unpack_bf16_pairs(p)
    return jnp.concatenate([hi.astype(BF16), lo.astype(BF16)], axis=1)


def _expert_kernel(be_ref, nu_ref, x_ref, wg_ref, wu_ref, wd_ref, y_ref, g16_ref, u16_ref, d16_ref):
    i = pl.program_id(0)
    changed = jnp.logical_or(i == 0, be_ref[i] != be_ref[jnp.maximum(i - 1, 0)])

    @pl.when(changed)
    def _():
        g16_ref[...] = wg_ref[0, 0].astype(BF16)
        u16_ref[...] = wu_ref[0, 0].astype(BF16)
        d16_ref[...] = wd_ref[0, 0].astype(BF16)

    @pl.when(i < nu_ref[0])
    def _():
        x = _unpacked_bf16(x_ref[...])
        hh = _silu(_dot(x, g16_ref[...])) * _dot(x, u16_ref[...])
        y_ref[...] = _pack_bf16_pairs(_dot(hh.astype(BF16), d16_ref[...]))


def _experts(block_e, n_used, xs, wg, wu, wd, layer):
    nb = block_e.shape[0]
    bm = MOE_BLOCK
    grid_spec = pltpu.PrefetchScalarGridSpec(
        num_scalar_prefetch=2,
        grid=(nb,),
        in_specs=[pl.BlockSpec((bm, PACKED), lambda i, be, nu: (i, 0)),
                  pl.BlockSpec((1, 1, D_MODEL, D_EXPERT), lambda i, be, nu: (layer, be[i], 0, 0)),
                  pl.BlockSpec((1, 1, D_MODEL, D_EXPERT), lambda i, be, nu: (layer, be[i], 0, 0)),
                  pl.BlockSpec((1, 1, D_EXPERT, D_MODEL), lambda i, be, nu: (layer, be[i], 0, 0))],
        out_specs=pl.BlockSpec((bm, PACKED), lambda i, be, nu: (i, 0)),
        scratch_shapes=[pltpu.VMEM((D_MODEL, D_EXPERT), BF16), pltpu.VMEM((D_MODEL, D_EXPERT), BF16),
                        pltpu.VMEM((D_EXPERT, D_MODEL), BF16)],
    )
    return pl.pallas_call(
        _expert_kernel,
        grid_spec=grid_spec,
        out_shape=jax.ShapeDtypeStruct((nb * bm, PACKED), jnp.int32),
        compiler_params=_params("arbitrary"),
        name="moe_experts",
    )(block_e, n_used, xs, wg, wu, wd)


def _combine_kernel(g_ref, wk_ref, h_ref, xpk_ref, sg_ref, su_ref, sd_ref, lw_ref, lb_ref, o_ref):
    x = _unpacked_bf16(xpk_ref[...])
    hs = _silu(_dot(x, sg_ref[...])) * _dot(x, su_ref[...])
    shared = _dot(hs.astype(BF16), sd_ref[...])
    acc_hi = shared[:, :PACKED]
    acc_lo = shared[:, PACKED:]
    wk = wk_ref[...]
    for k in range(TOP_K):
        y_hi, y_lo = _unpack_bf16_pairs(g_ref[k])
        w = wk[:, k:k + 1]
        acc_hi = acc_hi + w * y_hi
        acc_lo = acc_lo + w * y_lo
    ffn = jnp.concatenate([acc_hi, acc_lo], axis=1)
    o_ref[...] = _layer_norm(ALPHA * h_ref[...] + ffn, lw_ref[...], lb_ref[...])


def _combine_ln(g, wk, h, xpk, sg, su, sd, ln_w, ln_b):
    t = h.shape[0]
    tm = COMBINE_TILE
    const = lambda shape: pl.BlockSpec(shape, lambda i: (0, 0))
    return pl.pallas_call(
        _combine_kernel,
        grid=(t // tm,),
        in_specs=[pl.BlockSpec((TOP_K, tm, PACKED), lambda i: (0, i, 0)),
                  pl.BlockSpec((tm, TOP_K), lambda i: (i, 0)),
                  pl.BlockSpec((tm, D_MODEL), lambda i: (i, 0)),
                  pl.BlockSpec((tm, PACKED), lambda i: (i, 0)),
                  const((D_MODEL, D_EXPERT)), const((D_MODEL, D_EXPERT)), const((D_EXPERT, D_MODEL)),
                  const((1, D_MODEL)), const((1, D_MODEL))],
        out_specs=pl.BlockSpec((tm, D_MODEL), lambda i: (i, 0)),
        out_shape=jax.ShapeDtypeStruct((t, D_MODEL), F32),
        compiler_params=_params("parallel"),
        name="moe_combine_ln",
    )(g, wk, h, xpk, sg.astype(BF16), su.astype(BF16), sd.astype(BF16),
      ln_w.reshape(1, D_MODEL), ln_b.reshape(1, D_MODEL))


def _moe_ln(h, hpk, idx, rank, wk, counts, wg, wu, wd, layer, sg, su, sd, ln_w, ln_b):
    t = h.shape[0]
    cnt = counts[:, 0].astype(jnp.int32)
    padded = (cnt + MOE_BLOCK - 1) // MOE_BLOCK * MOE_BLOCK
    pend = jnp.cumsum(padded)
    experts = jnp.arange(N_EXPERTS, dtype=jnp.int32)
    pstart_of_pick = jnp.sum(jnp.where(idx[:, :, None] == experts, pend - padded, 0), axis=-1)
    pos = pstart_of_pick + rank
    nb = -(-(t * TOP_K + N_EXPERTS * (MOE_BLOCK - 1)) // MOE_BLOCK)
    starts = jnp.arange(nb, dtype=jnp.int32) * MOE_BLOCK
    block_e = jnp.minimum(jnp.sum((pend[None, :] <= starts[:, None]).astype(jnp.int32), axis=1), N_EXPERTS - 1)
    n_used = (pend[-1] // MOE_BLOCK).astype(jnp.int32).reshape(1)
    pos_chunks = pos.reshape(TOP_K, t // SC_CHUNK, SC_CHUNK).transpose(1, 0, 2)
    xs = _dispatch_rows(hpk, pos_chunks, nb * MOE_BLOCK)
    ys = _experts(block_e, n_used, xs, wg, wu, wd, layer)
    g = _gather_rows(ys, pos.reshape(-1)).reshape(TOP_K, t, PACKED)
    return _combine_ln(g, wk.T, h, hpk, sg, su, sd, ln_w, ln_b)


def _pad_cols(w, width=LANES):
    return jnp.pad(w, ((0, 0), (0, width - w.shape[1])))


def _even_mixer(h, batch, seq, w_in, gate_b, norm_w, conv_w, conv_b, wa, ba, wx, bx, lam):
    a4 = 4 * MIX_HALF
    ng = 2 * HEADS
    w_main = jnp.concatenate([w_in[:, :a4], w_in[:, a4 + ng:]], axis=1).astype(BF16)
    w_gate = _pad_cols(w_in[:, a4:a4 + ng]).astype(BF16)
    proj, gates = _proj(h, w_main, w_gate)
    ya = _mlstm(proj, gates, gate_b, norm_w, batch, seq)
    yb = _rglru(proj, conv_w, conv_b, wa, ba, wx, bx, lam, batch, seq)
    return ya, yb


def _odd_mixer(h, batch, seq, w_in, lam_re, lam_im, b_re, b_im, c_re, c_im, d_skip, log_dt,
               glu_w, glu_b, gate_w, gate_b, norm_w):
    c0 = MIX_HALF
    c1 = c0 + HEADS * GLA_DK
    c2 = c1 + HEADS * GLA_DK
    c3 = c2 + MIX_HALF
    c4 = c3 + MIX_HALF
    w_main = jnp.concatenate([w_in[:, :c0], _pad_heads(w_in[:, c0:c1], 1), _pad_heads(w_in[:, c1:c2], 1),
                              w_in[:, c2:c4]], axis=1).astype(BF16)
    w_low = _pad_cols(w_in[:, c4:]).astype(BF16)
    proj, glow = _proj(h, w_main, w_low)
    tables = _s5_tables(lam_re, lam_im, b_re, b_im, c_re, c_im, log_dt)
    yc = _s5(proj, tables, d_skip, glu_w, glu_b, batch, seq)
    gw = jnp.pad(_pad_heads(gate_w, 1), ((0, LANES - GLA_GATE_RANK), (0, 0))).astype(BF16)
    gb = _pad_heads(gate_b.reshape(1, -1), 1)
    yd = _gla(proj, glow, gw, gb, norm_w, batch, seq)
    return yc, yd


def kernel(x, ln1_w, ln1_b, ln2_w, ln2_b, w_out, w_in_even, mlstm_gate_b, mlstm_norm_w, lru_conv_w, lru_conv_b, lru_wa, lru_ba, lru_wx, lru_bx, lru_lambda, w_in_odd, s5_lam_re, s5_lam_im, s5_b_re, s5_b_im, s5_c_re, s5_c_im, s5_d, s5_log_dt, s5_glu_w, s5_glu_b, gla_gate_w, gla_gate_b, gla_norm_w, router_w, router_bias, exp_w_gate, exp_w_up, exp_w_down, sh_w_gate, sh_w_up, sh_w_down):
    batch, seq, d = x.shape
    h = x.reshape(batch * seq, d)
    for layer in range(DEPTH):
        j = layer // 2
        if layer % 2 == 0:
            y1, y2 = _even_mixer(h, batch, seq, w_in_even[j], mlstm_gate_b[j], mlstm_norm_w[j],
                                 lru_conv_w[j], lru_conv_b[j], lru_wa[j], lru_ba[j], lru_wx[j],
                                 lru_bx[j], lru_lambda[j])
        else:
            y1, y2 = _odd_mixer(h, batch, seq, w_in_odd[j], s5_lam_re[j], s5_lam_im[j], s5_b_re[j],
                                s5_b_im[j], s5_c_re[j], s5_c_im[j], s5_d[j], s5_log_dt[j],
                                s5_glu_w[j], s5_glu_b[j], gla_gate_w[j], gla_gate_b[j], gla_norm_w[j])
        h, hpk, idx, rank, wk, counts = _out_proj_ln_route(y1, y2, h, w_out[layer], ln1_w[layer], ln1_b[layer],
                                                           router_w[layer], router_bias[layer])
        h = _moe_ln(h, hpk, idx, rank, wk, counts, exp_w_gate, exp_w_up, exp_w_down, layer,
                    sh_w_gate[layer], sh_w_up[layer], sh_w_down[layer], ln2_w[layer], ln2_b[layer])
    return h.reshape(batch, seq, d)
```

```python
import functools
import math

import jax
import jax.numpy as jnp
from jax import lax
from jax.experimental import pallas as pl
from jax.experimental.pallas import tpu as pltpu
from jax.experimental.pallas import tpu_sc as plsc

F32 = jnp.float32
BF16 = jnp.bfloat16

D_MODEL = 1024
DEPTH = 2
MIX_HALF = 512
HEADS = 4
HEAD_DIM = 128
GLA_DK = 64
GLA_CHUNK = 64
GLA_GATE_RANK = 16
GLA_GATE_TEMP = 16.0
LRU_C = 8.0
LRU_CONV = 4
S5_GROUP = 16
S5_GROUPS = 32
S5_STATE = 64
S5_LANES = S5_GROUPS * S5_STATE
S5_BLOCKS = 4
N_EXPERTS = 64
N_GROUPS = 8
GROUP_SIZE = N_EXPERTS // N_GROUPS
TOP_K = 8
TOPK_GROUPS = 4
D_EXPERT = 256
ROUTED_SCALE = 2.5
ALPHA = (2.0 * DEPTH) ** 0.25
EPS = 1e-5
LANES = 128
SUBLANES = 8
NEG_INF = float("-inf")

VMEM_LIMIT = 56 * 1024 * 1024

MLSTM_CHUNK = 128
MLSTM_TILE = 1024
LRU_TILE = 1024
LRU_LOG_STEPS = 3
LRU_UNROLL = 4
S5_TILE = 512
S5_LOG_STEPS = 3
S5_UNROLL = True
GLA_UNROLL = 4
GLA_TILE = 1024
PROJ_TILE = 512
OUT_TILE = 512
MOE_BLOCK = 1024
COMBINE_TILE = 256
PACKED = D_MODEL // 2
SC_CHUNK = 64
SC_CORES = 2
SC_SUBCORES = 16
SC_WORKERS = SC_CORES * SC_SUBCORES


def _params(*sem):
    return pltpu.CompilerParams(dimension_semantics=sem, vmem_limit_bytes=VMEM_LIMIT)


def _split3(x):
    hi = x.astype(BF16)
    r1 = x - hi.astype(F32)
    mid = r1.astype(BF16)
    lo = (r1 - mid.astype(F32)).astype(BF16)
    return hi, mid, lo


def _dot(a, b):
    return jnp.dot(a, b, preferred_element_type=F32)


def _dot_nt(a, b):
    return lax.dot_general(a, b, (((1,), (1,)), ((), ())), preferred_element_type=F32)


def _dot_tn(a, b):
    return lax.dot_general(a, b, (((0,), (0,)), ((), ())), preferred_element_type=F32)


def _exact_left01(mask01_bf16, x):
    hi, mid, lo = _split3(x)
    return _dot(mask01_bf16, hi) + _dot(mask01_bf16, mid) + _dot(mask01_bf16, lo)


def _exact_right01(x, mask01_bf16):
    hi, mid, lo = _split3(x)
    return _dot(hi, mask01_bf16) + _dot(mid, mask01_bf16) + _dot(lo, mask01_bf16)


def _log_sigmoid(x):
    return jnp.minimum(x, 0.0) - jnp.log(1.0 + jnp.exp(-jnp.abs(x)))


def _sigmoid(x):
    return 1.0 / (1.0 + jnp.exp(-x))


def _gelu_tanh(x):
    c = math.sqrt(2.0 / math.pi)
    return 0.5 * x * (1.0 + jnp.tanh(c * (x + 0.044715 * (x * x * x))))


def _layer_norm(z, w, b):
    mu = jnp.mean(z, axis=-1, keepdims=True)
    zc = z - mu
    return zc * lax.rsqrt(jnp.mean(zc * zc, axis=-1, keepdims=True) + EPS) * w + b


def _proj_kernel(x_ref, w_ref, wg_ref, o_ref, og_ref):
    x = x_ref[...].astype(BF16)
    o_ref[...] = _dot(x, w_ref[...])
    og_ref[...] = _dot(x, wg_ref[...])


def _proj(x, w_main, w_small):
    t, d = x.shape
    n = w_main.shape[1]
    tm = PROJ_TILE
    return pl.pallas_call(
        _proj_kernel,
        grid=(t // tm,),
        in_specs=[pl.BlockSpec((tm, d), lambda i: (i, 0)),
                  pl.BlockSpec((d, n), lambda i: (0, 0)),
                  pl.BlockSpec((d, LANES), lambda i: (0, 0))],
        out_specs=[pl.BlockSpec((tm, n), lambda i: (i, 0)),
                   pl.BlockSpec((tm, LANES), lambda i: (i, 0))],
        out_shape=[jax.ShapeDtypeStruct((t, n), F32), jax.ShapeDtypeStruct((t, LANES), F32)],
        compiler_params=_params("parallel"),
        name="in_proj",
    )(x, w_main, w_small)


def _mlstm_kernel(q_ref, k_ref, v_ref, o_ref, gc_ref, gr_ref, bc_ref, br_ref, nw_ref,
                  y_ref, c_ref, m_ref, *, chunk, n_chunks):
    L = chunk

    @pl.when(pl.program_id(1) == 0)
    def _():
        c_ref[...] = jnp.zeros_like(c_ref)
        m_ref[...] = jnp.zeros_like(m_ref)

    ri = lax.broadcasted_iota(jnp.int32, (L, L), 0)
    ci = lax.broadcasted_iota(jnp.int32, (L, L), 1)
    causal = ci <= ri
    tril = causal.astype(BF16)
    triu = (ri <= ci).astype(BF16)
    ones_v = jnp.ones((L, HEAD_DIM), BF16)
    scale = HEAD_DIM ** -0.5

    def body(c, carry):
        r0 = pl.multiple_of(c * L, L)
        g_col = gc_ref[pl.ds(r0, L), :] + bc_ref[...]
        g_row = gr_ref[c] + br_ref[...]
        b_col_all = _exact_left01(tril, _log_sigmoid(g_col))
        b_row_all = _exact_right01(_log_sigmoid(g_row), triu)
        for h in range(HEADS):
            lo = h * HEAD_DIM
            q = q_ref[pl.ds(r0, L), lo:lo + HEAD_DIM].astype(BF16)
            k = k_ref[pl.ds(r0, L), lo:lo + HEAD_DIM] * scale
            v = v_ref[pl.ds(r0, L), lo:lo + HEAD_DIM].astype(BF16)
            v_aug = jnp.concatenate([v, ones_v], axis=1)
            i_rep = jnp.broadcast_to(g_col[:, h:h + 1], (L, LANES))
            b_rep = jnp.broadcast_to(b_col_all[:, HEADS + h:HEADS + h + 1], (L, LANES))
            i_row = g_row[h:h + 1, :]
            b_row = b_row_all[HEADS + h:HEADS + h + 1, :]
            b_last = b_rep[L - 1:L, :]
            m_prev = m_ref[h:h + 1, :]
            c_prev = c_ref[h]

            d_mat = jnp.where(causal, b_rep - b_row + i_row, NEG_INF)
            m_inter = b_rep + m_prev
            m_i = jnp.maximum(m_inter, jnp.max(d_mat, axis=1, keepdims=True))
            s = _dot_nt(q, k.astype(BF16)) * jnp.exp(d_mat - m_i)
            w_inter = jnp.exp(m_inter - m_i)
            intra = _dot(s.astype(BF16), v_aug)
            inter = _dot(q, c_prev.astype(BF16))
            num = intra[:, :HEAD_DIM] + w_inter * inter[:, :HEAD_DIM]
            den = intra[:, HEAD_DIM:] + w_inter * inter[:, HEAD_DIM:]
            hh = num / jnp.maximum(jnp.abs(den), jnp.exp(-m_i))

            w_loc = b_last - b_rep + i_rep
            m_loc = jnp.max(w_loc, axis=0, keepdims=True)
            kp = (k * jnp.exp(w_loc - m_loc)).astype(BF16)
            c_loc = _dot_tn(kp, v_aug)
            m_new = jnp.maximum(b_last + m_prev, m_loc)
            keep = jnp.exp(b_last + m_prev - m_new)
            add = jnp.exp(m_loc - m_new)
            c_ref[h] = (jnp.concatenate([keep, keep], axis=1) * c_prev
                        + jnp.concatenate([add, add], axis=1) * c_loc)
            m_ref[h:h + 1, :] = m_new

            hc = hh - jnp.mean(hh, axis=-1, keepdims=True)
            yn = hc * lax.rsqrt(jnp.mean(hc * hc, axis=-1, keepdims=True) + EPS)
            og = o_ref[pl.ds(r0, L), lo:lo + HEAD_DIM]
            y_ref[pl.ds(r0, L), lo:lo + HEAD_DIM] = yn * nw_ref[:, lo:lo + HEAD_DIM] * _sigmoid(og)
        return carry

    lax.fori_loop(0, n_chunks, body, 0)


def _mlstm(proj, gates, gate_b, norm_w, batch, seq):
    t = batch * seq
    L = MLSTM_CHUNK
    assert L == LANES, "the kernel keeps per-row gate terms replicated over one vreg of lanes"
    ts = MLSTM_TILE
    nj = seq // ts
    nc = ts // L
    g_row = gates[:, :2 * HEADS].reshape(t // L, L, 2 * HEADS).transpose(0, 2, 1)
    b_col = jnp.zeros((1, LANES), F32).at[0, :2 * HEADS].set(gate_b)
    b_row = gate_b.reshape(2 * HEADS, 1)
    blk = lambda col: pl.BlockSpec((ts, MIX_HALF), lambda b, j, col=col: (b * nj + j, col))
    kern = functools.partial(_mlstm_kernel, chunk=L, n_chunks=nc)
    return pl.pallas_call(
        kern,
        grid=(batch, nj),
        in_specs=[blk(0), blk(1), blk(2), blk(3),
                  pl.BlockSpec((ts, LANES), lambda b, j: (b * nj + j, 0)),
                  pl.BlockSpec((nc, 2 * HEADS, L), lambda b, j: (b * nj + j, 0, 0)),
                  pl.BlockSpec((1, LANES), lambda b, j: (0, 0)),
                  pl.BlockSpec((2 * HEADS, 1), lambda b, j: (0, 0)),
                  pl.BlockSpec((1, MIX_HALF), lambda b, j: (0, 0))],
        out_specs=pl.BlockSpec((ts, MIX_HALF), lambda b, j: (b * nj + j, 0)),
        out_shape=jax.ShapeDtypeStruct((t, MIX_HALF), F32),
        scratch_shapes=[pltpu.VMEM((HEADS, HEAD_DIM, 2 * HEAD_DIM), F32),
                        pltpu.VMEM((8, LANES), F32)],
        compiler_params=_params("arbitrary", "arbitrary"),
        name="mlstm",
    )(proj, proj, proj, proj, gates, g_row, b_col, b_row, norm_w.reshape(1, MIX_HALF))


def _rglru_kernel(xb_ref, gb_ref, cw_ref, cb_ref, wa_ref, ba_ref, wx_ref, bx_ref, lam_ref,
                  y_ref, xext_ref, h_ref, a_ref, u_ref, *, tile):
    @pl.when(pl.program_id(1) == 0)
    def _():
        xext_ref[0:8, :] = jnp.zeros((8, MIX_HALF), F32)
        h_ref[...] = jnp.zeros_like(h_ref)

    x = xb_ref[...]
    xext_ref[8:8 + tile, :] = x
    xc = cb_ref[...] + cw_ref[LRU_CONV - 1:LRU_CONV, :] * x
    for tap in range(LRU_CONV - 1):
        back = LRU_CONV - 1 - tap
        xc = xc + cw_ref[tap:tap + 1, :] * xext_ref[8 - back:8 - back + tile, :]
    xext_ref[0:8, :] = x[tile - 8:tile, :]

    xc16 = xc.astype(BF16)
    r_parts, i_parts = [], []
    for h in range(HEADS):
        lo = h * HEAD_DIM
        xh = xc16[:, lo:lo + HEAD_DIM]
        r_parts.append(_dot(xh, wa_ref[h]))
        i_parts.append(_dot(xh, wx_ref[h]))
    r = _sigmoid(jnp.concatenate(r_parts, axis=1) + ba_ref[...])
    ig = _sigmoid(jnp.concatenate(i_parts, axis=1) + bx_ref[...])
    lam = lam_ref[...]
    softplus_neg = jnp.maximum(-lam, 0.0) + jnp.log(1.0 + jnp.exp(-jnp.abs(lam)))
    log_a = -LRU_C * r * softplus_neg
    a = jnp.exp(log_a)
    th = jnp.tanh(log_a)
    u = jnp.sqrt(-2.0 * th / (1.0 - th)) * ig * xc

    a_ref[...] = a
    u_ref[...] = u
    rows = lax.broadcasted_iota(jnp.int32, (SUBLANES, MIX_HALF), 0)

    def group(i, h_prev):
        r0 = pl.multiple_of(i * SUBLANES, SUBLANES)
        ag = a_ref[pl.ds(r0, SUBLANES), :]
        ug = u_ref[pl.ds(r0, SUBLANES), :]
        for k in range(LRU_LOG_STEPS):
            keep = rows >= (1 << k)
            ug = ag * jnp.where(keep, pltpu.roll(ug, 1 << k, 0), 0.0) + ug
            ag = ag * jnp.where(keep, pltpu.roll(ag, 1 << k, 0), 1.0)
        hg = ug + ag * h_prev
        u_ref[pl.ds(r0, SUBLANES), :] = hg
        return hg[SUBLANES - 1:SUBLANES, :]

    h_last = lax.fori_loop(0, tile // SUBLANES, group, h_ref[0:1, :], unroll=LRU_UNROLL)
    h_ref[...] = jnp.broadcast_to(h_last, h_ref.shape)
    y_ref[...] = u_ref[...] * _gelu_tanh(gb_ref[...])


def _rglru(proj, conv_w, conv_b, wa, ba, wx, bx, lam, batch, seq):
    t = batch * seq
    ts = LRU_TILE
    nj = seq // ts
    row = lambda a: a.reshape(1, MIX_HALF)
    const2 = lambda shape: pl.BlockSpec(shape, lambda b, j: (0, 0))
    const3 = lambda shape: pl.BlockSpec(shape, lambda b, j: (0, 0, 0))
    blk = lambda col: pl.BlockSpec((ts, MIX_HALF), lambda b, j, col=col: (b * nj + j, col))
    return pl.pallas_call(
        functools.partial(_rglru_kernel, tile=ts),
        grid=(batch, nj),
        in_specs=[blk(4), blk(5), const2((LRU_CONV, MIX_HALF)), const2((1, MIX_HALF)),
                  const3((HEADS, HEAD_DIM, HEAD_DIM)), const2((1, MIX_HALF)),
                  const3((HEADS, HEAD_DIM, HEAD_DIM)), const2((1, MIX_HALF)), const2((1, MIX_HALF))],
        out_specs=pl.BlockSpec((ts, MIX_HALF), lambda b, j: (b * nj + j, 0)),
        out_shape=jax.ShapeDtypeStruct((t, MIX_HALF), F32),
        scratch_shapes=[pltpu.VMEM((ts + 8, MIX_HALF), F32), pltpu.VMEM((8, MIX_HALF), F32),
                        pltpu.VMEM((ts, MIX_HALF), F32), pltpu.VMEM((ts, MIX_HALF), F32)],
        compiler_params=_params("arbitrary", "arbitrary"),
        name="rglru",
    )(proj, proj, conv_w, row(conv_b), wa.astype(BF16), row(ba), wx.astype(BF16), row(bx), row(lam))


def _s5_kernel(u_ref, bre_ref, bim_ref, cre_ref, cim_ref, mre_ref, mim_ref, pre_ref, pim_ref, d_ref, gw_ref,
               gb_ref, y_ref, xr_ref, xi_ref, cr_ref, ci_ref, *, tile):
    @pl.when(pl.program_id(1) == 0)
    def _():
        cr_ref[...] = jnp.zeros_like(cr_ref)
        ci_ref[...] = jnp.zeros_like(ci_ref)

    u = u_ref[...]
    u16 = u.astype(BF16)
    blk_c = MIX_HALF // S5_BLOCKS
    blk_s = S5_LANES // S5_BLOCKS
    parts = []
    for j in range(S5_BLOCKS):
        lanes = slice(j * blk_s, (j + 1) * blk_s)
        uj = u16[:, j * blk_c:(j + 1) * blk_c]
        xr_ref[:, lanes] = _dot(uj, bre_ref[j])
        xi_ref[:, lanes] = _dot(uj, bim_ref[j])

        def group(i, carry, lanes=lanes):
            cr, ci = carry
            r0 = pl.multiple_of(i * SUBLANES, SUBLANES)
            xr = xr_ref[pl.ds(r0, SUBLANES), lanes]
            xi = xi_ref[pl.ds(r0, SUBLANES), lanes]
            for k in range(S5_LOG_STEPS):
                sr = pltpu.roll(xr, 1 << k, 0)
                si = pltpu.roll(xi, 1 << k, 0)
                mr = mre_ref[k, :, lanes]
                mi = mim_ref[k, :, lanes]
                xr, xi = xr + mr * sr - mi * si, xi + mr * si + mi * sr
            pr = pre_ref[:, lanes]
            pi = pim_ref[:, lanes]
            xr, xi = xr + pr * cr - pi * ci, xi + pr * ci + pi * cr
            xr_ref[pl.ds(r0, SUBLANES), lanes] = xr
            xi_ref[pl.ds(r0, SUBLANES), lanes] = xi
            return xr[SUBLANES - 1:SUBLANES, :], xi[SUBLANES - 1:SUBLANES, :]

        cr, ci = lax.fori_loop(0, tile // SUBLANES, group, (cr_ref[0:1, lanes], ci_ref[0:1, lanes]),
                               unroll=S5_UNROLL)
        cr_ref[0:1, lanes] = cr
        ci_ref[0:1, lanes] = ci
        parts.append(_dot(xr_ref[:, lanes].astype(BF16), cre_ref[j])
                     - _dot(xi_ref[:, lanes].astype(BF16), cim_ref[j]))
    y = jnp.concatenate(parts, axis=1) + d_ref[...] * u
    g = _gelu_tanh(y)
    y_ref[...] = g * _sigmoid(_dot(g.astype(BF16), gw_ref[...]) + gb_ref[...])


def _s5_tables(lam_re, lam_im, b_re, b_im, c_re, c_im, log_dt):
    lr, li = lam_re.astype(F32), lam_im.astype(F32)
    dt = jnp.exp(log_dt.astype(F32))[:, None]
    mag = jnp.exp(lr * dt)
    abar_re = mag * jnp.cos(li * dt)
    abar_im = mag * jnp.sin(li * dt)
    den = lr * lr + li * li
    nr = abar_re - 1.0
    coef_re = (nr * lr + abar_im * li) / den
    coef_im = (abar_im * lr - nr * li) / den
    bbar_re = coef_re[..., None] * b_re - coef_im[..., None] * b_im
    bbar_im = coef_re[..., None] * b_im + coef_im[..., None] * b_re
    gpb = S5_GROUPS // S5_BLOCKS
    eye = jnp.eye(gpb, dtype=F32)

    def in_map(bb):
        bb = bb.reshape(S5_BLOCKS, gpb, S5_STATE, S5_GROUP)
        return jnp.einsum("jgph,gk->jghkp", bb, eye).reshape(S5_BLOCKS, gpb * S5_GROUP, gpb * S5_STATE)

    def out_map(cc):
        cc = cc.reshape(S5_BLOCKS, gpb, S5_GROUP, S5_STATE)
        return jnp.einsum("jghp,gk->jgpkh", cc, eye).reshape(S5_BLOCKS, gpb * S5_STATE, gpb * S5_GROUP)

    def power(n):
        n = jnp.asarray(n, F32)[..., None, None]
        pmag = jnp.exp(n * (lr * dt))
        shape = n.shape[:-2] + (S5_LANES,)
        return (pmag * jnp.cos(n * (li * dt))).reshape(shape), (pmag * jnp.sin(n * (li * dt))).reshape(shape)

    row = jnp.arange(SUBLANES)
    step = 2 ** jnp.arange(S5_LOG_STEPS)
    s_re, s_im = power(step)
    keep = (row[None, :] >= step[:, None])[..., None]
    m_re = jnp.where(keep, s_re[:, None, :], 0.0)
    m_im = jnp.where(keep, s_im[:, None, :], 0.0)
    p_re, p_im = power(row + 1)
    return (in_map(bbar_re).astype(BF16), in_map(bbar_im).astype(BF16),
            out_map(c_re.astype(F32)).astype(BF16), out_map(c_im.astype(F32)).astype(BF16),
            m_re, m_im, p_re, p_im)


def _s5(proj, tables, d_skip, glu_w, glu_b, batch, seq):
    t = batch * seq
    ts = S5_TILE
    nj = seq // ts
    bre, bim, cre, cim, m_re, m_im, p_re, p_im = tables
    blk_c = MIX_HALF // S5_BLOCKS
    blk_s = S5_LANES // S5_BLOCKS
    const2 = lambda shape: pl.BlockSpec(shape, lambda b, j: (0, 0))
    const3 = lambda shape: pl.BlockSpec(shape, lambda b, j: (0, 0, 0))
    return pl.pallas_call(
        functools.partial(_s5_kernel, tile=ts),
        grid=(batch, nj),
        in_specs=[pl.BlockSpec((ts, MIX_HALF), lambda b, j: (b * nj + j, 0)),
                  const3((S5_BLOCKS, blk_c, blk_s)), const3((S5_BLOCKS, blk_c, blk_s)),
                  const3((S5_BLOCKS, blk_s, blk_c)), const3((S5_BLOCKS, blk_s, blk_c)),
                  const3(m_re.shape), const3(m_im.shape), const2(p_re.shape), const2(p_im.shape),
                  const2((1, MIX_HALF)), const2((MIX_HALF, MIX_HALF)), const2((1, MIX_HALF))],
        out_specs=pl.BlockSpec((ts, MIX_HALF), lambda b, j: (b * nj + j, 0)),
        out_shape=jax.ShapeDtypeStruct((t, MIX_HALF), F32),
        scratch_shapes=[pltpu.VMEM((ts, S5_LANES), F32), pltpu.VMEM((ts, S5_LANES), F32),
                        pltpu.VMEM((8, S5_LANES), F32), pltpu.VMEM((8, S5_LANES), F32)],
        compiler_params=_params("arbitrary", "arbitrary"),
        name="s5",
    )(proj, bre, bim, cre, cim, m_re, m_im, p_re, p_im, d_skip.reshape(1, MIX_HALF), glu_w.astype(BF16),
      glu_b.reshape(1, MIX_HALF))


def _gla_kernel(q_ref, k_ref, v_ref, r_ref, gl_ref, gw_ref, gb_ref, nw_ref, y_ref,
                st_ref, qd_ref, ki_ref, ke_ref, v16_ref, dec_ref, o_ref, *, tile, chunk):
    L = chunk
    nc = tile // L

    @pl.when(pl.program_id(1) == 0)
    def _():
        st_ref[...] = jnp.zeros_like(st_ref)

    z = _dot(gl_ref[...].astype(BF16), gw_ref[...]) + gb_ref[...]
    bcum = _log_sigmoid(z) * (1.0 / GLA_GATE_TEMP)
    row_in_chunk = lax.broadcasted_iota(jnp.int32, bcum.shape, 0) & (L - 1)
    s = 1
    while s < L:
        bcum = bcum + jnp.where(row_in_chunk >= s, pltpu.roll(bcum, s, 0), 0.0)
        s *= 2
    b3 = bcum.reshape(nc, L, MIX_HALF)
    b_last = b3[:, L - 1:L, :]
    k = k_ref[...]
    qd_ref[...] = (q_ref[...] * (GLA_DK ** -0.5) * jnp.exp(bcum)).astype(BF16)
    ki_ref[...] = (k * jnp.exp(-bcum)).astype(BF16)
    ke_ref[...] = (k.reshape(nc, L, MIX_HALF) * jnp.exp(b_last - b3)).reshape(tile, MIX_HALF).astype(BF16)
    v16_ref[...] = v_ref[...].astype(BF16)
    dec_ref[...] = jnp.exp(b_last)

    ri = lax.broadcasted_iota(jnp.int32, (L, L), 0)
    ci = lax.broadcasted_iota(jnp.int32, (L, L), 1)
    causal = ci <= ri

    def body(c, carry):
        r0 = pl.multiple_of(c * L, L)
        dec = dec_ref[c]
        for h in range(HEADS):
            lo = h * HEAD_DIM
            q_dec = qd_ref[pl.ds(r0, L), lo:lo + HEAD_DIM]
            v = v16_ref[pl.ds(r0, L), lo:lo + HEAD_DIM]
            st = st_ref[h]
            att = jnp.where(causal, _dot_nt(q_dec, ki_ref[pl.ds(r0, L), lo:lo + HEAD_DIM]), 0.0)
            o_ref[pl.ds(r0, L), lo:lo + HEAD_DIM] = (_dot(att.astype(BF16), v)
                                                     + _dot_nt(q_dec, st.astype(BF16)))
            st_ref[h] = dec[:, lo:lo + HEAD_DIM] * st + _dot_tn(v, ke_ref[pl.ds(r0, L), lo:lo + HEAD_DIM])
        return carry

    lax.fori_loop(0, nc, body, 0, unroll=GLA_UNROLL)

    rg = r_ref[...]
    gate = nw_ref[...] * (rg * _sigmoid(rg))
    for h in range(HEADS):
        lo = h * HEAD_DIM
        o = o_ref[:, lo:lo + HEAD_DIM]
        yn = o * lax.rsqrt(jnp.mean(o * o, axis=-1, keepdims=True) + EPS)
        y_ref[:, lo:lo + HEAD_DIM] = yn * gate[:, lo:lo + HEAD_DIM]


def _gla(proj, glow, gate_w, gate_b, norm_w, batch, seq):
    t = batch * seq
    ts = GLA_TILE
    nj = seq // ts
    blk = lambda col: pl.BlockSpec((ts, MIX_HALF), lambda b, j, col=col: (b * nj + j, col))
    const2 = lambda shape: pl.BlockSpec(shape, lambda b, j: (0, 0))
    return pl.pallas_call(
        functools.partial(_gla_kernel, tile=ts, chunk=GLA_CHUNK),
        grid=(batch, nj),
        in_specs=[blk(1), blk(2), blk(3), blk(4),
                  pl.BlockSpec((ts, LANES), lambda b, j: (b * nj + j, 0)),
                  const2((LANES, MIX_HALF)), const2((1, MIX_HALF)), const2((1, MIX_HALF))],
        out_specs=pl.BlockSpec((ts, MIX_HALF), lambda b, j: (b * nj + j, 0)),
        out_shape=jax.ShapeDtypeStruct((t, MIX_HALF), F32),
        scratch_shapes=[pltpu.VMEM((HEADS, HEAD_DIM, HEAD_DIM), F32),
                        pltpu.VMEM((ts, MIX_HALF), BF16), pltpu.VMEM((ts, MIX_HALF), BF16),
                        pltpu.VMEM((ts, MIX_HALF), BF16), pltpu.VMEM((ts, MIX_HALF), BF16),
                        pltpu.VMEM((ts // GLA_CHUNK, 1, MIX_HALF), F32),
                        pltpu.VMEM((ts, MIX_HALF), F32)],
        compiler_params=_params("arbitrary", "arbitrary"),
        name="gla",
    )(proj, proj, proj, proj, glow, gate_w, gate_b, norm_w.reshape(1, MIX_HALF))


def _pad_heads(w, axis):
    shape = list(w.shape)
    shape[axis:axis + 1] = [HEADS, GLA_DK]
    w = w.reshape(shape)
    pad = [(0, 0)] * w.ndim
    pad[axis + 1] = (0, HEAD_DIM - GLA_DK)
    w = jnp.pad(w, pad)
    shape[axis:axis + 2] = [HEADS * HEAD_DIM]
    return w.reshape(shape)


def _pack_bf16_pairs(z):
    hi = lax.bitcast_convert_type(z[:, :PACKED].astype(BF16).astype(F32), jnp.uint32)
    lo = lax.bitcast_convert_type(z[:, PACKED:].astype(BF16).astype(F32), jnp.uint32)
    word = (hi & jnp.uint32(0xFFFF0000)) | lax.shift_right_logical(lo, jnp.uint32(16))
    return lax.bitcast_convert_type(word, jnp.int32)


def _unpack_bf16_pairs(p):
    word = lax.bitcast_convert_type(p, jnp.uint32)
    hi = lax.bitcast_convert_type(word & jnp.uint32(0xFFFF0000), F32)
    lo = lax.bitcast_convert_type(lax.shift_left(word, jnp.uint32(16)), F32)
    return hi, lo


def _out_kernel(ya_ref, yb_ref, h_ref, w_ref, lw_ref, lb_ref, rw_ref, rb_ref,
                o_ref, opk_ref, idx_ref, rank_ref, wk_ref, cnt_ref, base_ref, *, tile):
    mixed = jnp.concatenate([ya_ref[...], yb_ref[...]], axis=1).astype(BF16)
    z = ALPHA * h_ref[...] + _dot(mixed, w_ref[...])
    out = _layer_norm(z, lw_ref[...], lb_ref[...])
    o_ref[...] = out
    opk_ref[...] = _pack_bf16_pairs(out)
    _route_tile(out, rw_ref, rb_ref, idx_ref, rank_ref, wk_ref, cnt_ref, base_ref, tile)


def _out_proj_ln_route(ya, yb, h, w_out, ln_w, ln_b, router_w, router_bias):
    t = h.shape[0]
    tm = OUT_TILE
    const = lambda shape: pl.BlockSpec(shape, lambda i: (0, 0))
    per_tok = lambda dt: jax.ShapeDtypeStruct((TOP_K, t), dt)
    tok_blk = pl.BlockSpec((TOP_K, tm), lambda i: (0, i))
    return pl.pallas_call(
        functools.partial(_out_kernel, tile=tm),
        grid=(t // tm,),
        in_specs=[pl.BlockSpec((tm, MIX_HALF), lambda i: (i, 0)),
                  pl.BlockSpec((tm, MIX_HALF), lambda i: (i, 0)),
                  pl.BlockSpec((tm, D_MODEL), lambda i: (i, 0)),
                  const((D_MODEL, D_MODEL)), const((1, D_MODEL)), const((1, D_MODEL)),
                  const((N_EXPERTS, D_MODEL)), const((N_EXPERTS, 1))],
        out_specs=[pl.BlockSpec((tm, D_MODEL), lambda i: (i, 0)),
                   pl.BlockSpec((tm, PACKED), lambda i: (i, 0)),
                   tok_blk, tok_blk, tok_blk, const((N_EXPERTS, LANES))],
        out_shape=[jax.ShapeDtypeStruct((t, D_MODEL), F32), jax.ShapeDtypeStruct((t, PACKED), jnp.int32),
                   per_tok(jnp.int32), per_tok(jnp.int32), per_tok(F32),
                   jax.ShapeDtypeStruct((N_EXPERTS, LANES), F32)],
        scratch_shapes=[pltpu.VMEM((N_EXPERTS, LANES), F32)],
        compiler_params=_params("arbitrary"),
        name="out_proj_ln_route",
    )(ya, yb, h, w_out.astype(BF16), ln_w.reshape(1, D_MODEL), ln_b.reshape(1, D_MODEL),
      router_w.T, router_bias.reshape(N_EXPERTS, 1))


def _first_index(hit, idx, big):
    return jnp.min(jnp.where(hit, idx, big), axis=0, keepdims=True)


def _route_tile(h, w_ref, b_ref, idx_ref, rank_ref, wk_ref, cnt_ref, base_ref, tile):
    @pl.when(pl.program_id(0) == 0)
    def _():
        base_ref[...] = jnp.zeros_like(base_ref)

    h_hi, h_mid, _ = _split3(h)
    w_hi, w_mid, _ = _split3(w_ref[...])
    logits = _dot_nt(w_hi, h_hi) + _dot_nt(w_hi, h_mid) + _dot_nt(w_mid, h_hi)
    scores = _sigmoid(logits)
    biased = scores + b_ref[...]

    sub = lax.broadcasted_iota(jnp.int32, (GROUP_SIZE, tile), 0)
    grp_rows = []
    for g in range(N_GROUPS):
        xg = biased[g * GROUP_SIZE:(g + 1) * GROUP_SIZE, :]
        m1 = jnp.max(xg, axis=0, keepdims=True)
        i1 = _first_index(xg == m1, sub, GROUP_SIZE)
        m2 = jnp.max(jnp.where(sub == i1, NEG_INF, xg), axis=0, keepdims=True)
        grp_rows.append(m1 + m2)
    gs = jnp.concatenate(grp_rows, axis=0)
    gsel = jnp.zeros((N_GROUPS, tile), F32)
    for _ in range(TOPK_GROUPS):
        mx = jnp.max(gs, axis=0, keepdims=True)
        hit = sub == _first_index(gs == mx, sub, N_GROUPS)
        gsel = jnp.where(hit, 1.0, gsel)
        gs = jnp.where(hit, NEG_INF, gs)
    emask = jnp.concatenate(
        [jnp.broadcast_to(gsel[g:g + 1, :], (GROUP_SIZE, tile)) for g in range(N_GROUPS)], axis=0)

    eidx = lax.broadcasted_iota(jnp.int32, (N_EXPERTS, tile), 0)
    cand = jnp.where(emask > 0.5, biased, NEG_INF)
    sel = jnp.zeros((N_EXPERTS, tile), F32)
    hits, picks = [], []
    for _ in range(TOP_K):
        mx = jnp.max(cand, axis=0, keepdims=True)
        first = _first_index(cand == mx, eidx, N_EXPERTS)
        hit = eidx == first
        hits.append(hit)
        picks.append(first)
        sel = jnp.where(hit, 1.0, sel)
        cand = jnp.where(hit, NEG_INF, cand)
    picked = jnp.where(sel > 0.5, scores, 0.0)
    wts = picked / jnp.sum(picked, axis=0, keepdims=True) * ROUTED_SCALE

    ri = lax.broadcasted_iota(jnp.int32, (tile, tile), 0)
    ci = lax.broadcasted_iota(jnp.int32, (tile, tile), 1)
    before = (ri < ci).astype(BF16)
    prior = _dot(sel.astype(BF16), before) + base_ref[:, 0:1]
    ranks = [jnp.sum(jnp.where(hit, prior, 0.0), axis=0, keepdims=True) for hit in hits]
    wsel = [jnp.sum(jnp.where(hit, wts, 0.0), axis=0, keepdims=True) for hit in hits]
    idx_ref[...] = jnp.concatenate(picks, axis=0)
    rank_ref[...] = jnp.concatenate(ranks, axis=0).astype(jnp.int32)
    wk_ref[...] = jnp.concatenate(wsel, axis=0)
    total = base_ref[...] + jnp.sum(sel, axis=1, keepdims=True)
    base_ref[...] = total
    cnt_ref[...] = total


def _silu(x):
    return x * _sigmoid(x)


def _sc_mesh():
    return plsc.VectorSubcoreMesh(core_axis_name="c", subcore_axis_name="s")


def _sc_worker_id():
    return lax.axis_index("s") * SC_CORES + lax.axis_index("c")


def _dispatch_rows(xpk, pos_chunks, n_rows):
    t = xpk.shape[0]
    n_ch = t // SC_WORKERS // SC_CHUNK

    @functools.partial(
        pl.kernel, mesh=_sc_mesh(),
        out_type=jax.ShapeDtypeStruct((n_rows, PACKED), jnp.int32),
        scratch_types=[pltpu.VMEM((TOP_K, SC_CHUNK), jnp.int32),
                       pltpu.VMEM((SC_CHUNK, PACKED), jnp.int32),
                       pltpu.SemaphoreType.DMA],
        name="moe_dispatch",
    )
    def scatter(x_hbm, pos_hbm, out_hbm, idx_v, rows_v, sem):
        wid = _sc_worker_id()

        @pl.loop(0, n_ch)
        def _(c):
            chunk = wid * n_ch + c
            off = pl.multiple_of(chunk * SC_CHUNK, SC_CHUNK)
            pltpu.sync_copy(pos_hbm.at[chunk], idx_v)
            pltpu.sync_copy(x_hbm.at[pl.ds(off, SC_CHUNK)], rows_v)
            copies = [pltpu.async_copy(rows_v, out_hbm.at[idx_v.at[k]], sem) for k in range(TOP_K)]
            for cp in copies:
                cp.wait()

    return scatter(xpk, pos_chunks)


def _gather_rows(table, idx):
    n = idx.shape[0]
    per_w = n // SC_WORKERS
    n_ch = per_w // SC_CHUNK
    assert n_ch % 2 == 0 and n_ch >= 2

    @functools.partial(
        pl.kernel, mesh=_sc_mesh(),
        out_type=jax.ShapeDtypeStruct((n, PACKED), jnp.int32),
        scratch_types=[pltpu.VMEM((n_ch, SC_CHUNK), jnp.int32),
                       pltpu.VMEM((SC_CHUNK, PACKED), jnp.int32), pltpu.VMEM((SC_CHUNK, PACKED), jnp.int32),
                       pltpu.SemaphoreType.DMA, pltpu.SemaphoreType.DMA,
                       pltpu.SemaphoreType.DMA, pltpu.SemaphoreType.DMA],
        name="moe_gather",
    )
    def gather(table_hbm, idx_hbm, out_hbm, idx_v, rows0, rows1, g0, g1, w0, w1):
        wid = _sc_worker_id()
        base = wid * per_w
        rows, g_sem, w_sem = (rows0, rows1), (g0, g1), (w0, w1)
        pltpu.sync_copy(idx_hbm.at[wid], idx_v)

        def fetch(c, b):
            return pltpu.make_async_copy(table_hbm.at[idx_v.at[c]], rows[b], g_sem[b])

        def flush(c, b):
            off = pl.multiple_of(base + c * SC_CHUNK, SC_CHUNK)
            return pltpu.make_async_copy(rows[b], out_hbm.at[pl.ds(off, SC_CHUNK)], w_sem[b])

        fetch(0, 0).start()

        @pl.loop(0, n_ch, step=2)
        def _(c0):
            for b in range(2):
                c = c0 + b
                fetch(c, b).wait()
                flush(c, b).start()

                @pl.when(c + 1 < n_ch)
                def _():
                    @pl.when(c >= 1)
                    def _():
                        flush(c - 1, 1 - b).wait()
                    fetch(c + 1, 1 - b).start()

        flush(n_ch - 2, 0).wait()
        flush(n_ch - 1, 1).wait()

    return gather(table, idx.reshape(SC_WORKERS, n_ch, SC_CHUNK))


def _unpacked_bf16(p):
    hi, lo = _unpack_bf16_pairs(p)
    return jnp.concatenate([hi.astype(BF16), lo.astype(BF16)], axis=1)


def _expert_kernel(be_ref, nu_ref, x_ref, wg_ref, wu_ref, wd_ref, y_ref, g16_ref, u16_ref, d16_ref):
    i = pl.program_id(0)
    changed = jnp.logical_or(i == 0, be_ref[i] != be_ref[jnp.maximum(i - 1, 0)])

    @pl.when(changed)
    def _():
        g16_ref[...] = wg_ref[0, 0].astype(BF16)
        u16_ref[...] = wu_ref[0, 0].astype(BF16)
        d16_ref[...] = wd_ref[0, 0].astype(BF16)

    @pl.when(i < nu_ref[0])
    def _():
        x = _unpacked_bf16(x_ref[...])
        hh = _silu(_dot(x, g16_ref[...])) * _dot(x, u16_ref[...])
        y_ref[...] = _pack_bf16_pairs(_dot(hh.astype(BF16), d16_ref[...]))


def _experts(block_e, n_used, xs, wg, wu, wd, layer):
    nb = block_e.shape[0]
    bm = MOE_BLOCK
    grid_spec = pltpu.PrefetchScalarGridSpec(
        num_scalar_prefetch=2,
        grid=(nb,),
        in_specs=[pl.BlockSpec((bm, PACKED), lambda i, be, nu: (jnp.minimum(i, nu[0] - 1), 0)),
                  pl.BlockSpec((1, 1, D_MODEL, D_EXPERT), lambda i, be, nu: (layer, be[i], 0, 0)),
                  pl.BlockSpec((1, 1, D_MODEL, D_EXPERT), lambda i, be, nu: (layer, be[i], 0, 0)),
                  pl.BlockSpec((1, 1, D_EXPERT, D_MODEL), lambda i, be, nu: (layer, be[i], 0, 0))],
        out_specs=pl.BlockSpec((bm, PACKED), lambda i, be, nu: (jnp.minimum(i, nu[0] - 1), 0)),
        scratch_shapes=[pltpu.VMEM((D_MODEL, D_EXPERT), BF16), pltpu.VMEM((D_MODEL, D_EXPERT), BF16),
                        pltpu.VMEM((D_EXPERT, D_MODEL), BF16)],
    )
    return pl.pallas_call(
        _expert_kernel,
        grid_spec=grid_spec,
        out_shape=jax.ShapeDtypeStruct((nb * bm, PACKED), jnp.int32),
        compiler_params=_params("arbitrary"),
        name="moe_experts",
    )(block_e, n_used, xs, wg, wu, wd)


def _combine_kernel(g_ref, wk_ref, h_ref, xpk_ref, sg_ref, su_ref, sd_ref, lw_ref, lb_ref, o_ref):
    x = _unpacked_bf16(xpk_ref[...])
    hs = _silu(_dot(x, sg_ref[...])) * _dot(x, su_ref[...])
    shared = _dot(hs.astype(BF16), sd_ref[...])
    acc_hi = shared[:, :PACKED]
    acc_lo = shared[:, PACKED:]
    wk = wk_ref[...]
    for k in range(TOP_K):
        y_hi, y_lo = _unpack_bf16_pairs(g_ref[k])
        w = wk[:, k:k + 1]
        acc_hi = acc_hi + w * y_hi
        acc_lo = acc_lo + w * y_lo
    ffn = jnp.concatenate([acc_hi, acc_lo], axis=1)
    o_ref[...] = _layer_norm(ALPHA * h_ref[...] + ffn, lw_ref[...], lb_ref[...])


def _combine_ln(g, wk, h, xpk, sg, su, sd, ln_w, ln_b):
    t = h.shape[0]
    tm = COMBINE_TILE
    const = lambda shape: pl.BlockSpec(shape, lambda i: (0, 0))
    return pl.pallas_call(
        _combine_kernel,
        grid=(t // tm,),
        in_specs=[pl.BlockSpec((TOP_K, tm, PACKED), lambda i: (0, i, 0)),
                  pl.BlockSpec((tm, TOP_K), lambda i: (i, 0)),
                  pl.BlockSpec((tm, D_MODEL), lambda i: (i, 0)),
                  pl.BlockSpec((tm, PACKED), lambda i: (i, 0)),
                  const((D_MODEL, D_EXPERT)), const((D_MODEL, D_EXPERT)), const((D_EXPERT, D_MODEL)),
                  const((1, D_MODEL)), const((1, D_MODEL))],
        out_specs=pl.BlockSpec((tm, D_MODEL), lambda i: (i, 0)),
        out_shape=jax.ShapeDtypeStruct((t, D_MODEL), F32),
        compiler_params=_params("parallel"),
        name="moe_combine_ln",
    )(g, wk, h, xpk, sg.astype(BF16), su.astype(BF16), sd.astype(BF16),
      ln_w.reshape(1, D_MODEL), ln_b.reshape(1, D_MODEL))


def _moe_ln(h, hpk, idx, rank, wk, counts, wg, wu, wd, layer, sg, su, sd, ln_w, ln_b):
    t = h.shape[0]
    cnt = counts[:, 0].astype(jnp.int32)
    padded = (cnt + MOE_BLOCK - 1) // MOE_BLOCK * MOE_BLOCK
    pend = jnp.cumsum(padded)
    experts = jnp.arange(N_EXPERTS, dtype=jnp.int32)
    pstart_of_pick = jnp.sum(jnp.where(idx[:, :, None] == experts, pend - padded, 0), axis=-1)
    pos = pstart_of_pick + rank
    nb = -(-(t * TOP_K + N_EXPERTS * (MOE_BLOCK - 1)) // MOE_BLOCK)
    starts = jnp.arange(nb, dtype=jnp.int32) * MOE_BLOCK
    block_e = jnp.minimum(jnp.sum((pend[None, :] <= starts[:, None]).astype(jnp.int32), axis=1), N_EXPERTS - 1)
    n_used = (pend[-1] // MOE_BLOCK).astype(jnp.int32).reshape(1)
    pos_chunks = pos.reshape(TOP_K, t // SC_CHUNK, SC_CHUNK).transpose(1, 0, 2)
    xs = _dispatch_rows(hpk, pos_chunks, nb * MOE_BLOCK)
    ys = _experts(block_e, n_used, xs, wg, wu, wd, layer)
    g = _gather_rows(ys, pos.reshape(-1)).reshape(TOP_K, t, PACKED)
    return _combine_ln(g, wk.T, h, hpk, sg, su, sd, ln_w, ln_b)


def _pad_cols(w, width=LANES):
    return jnp.pad(w, ((0, 0), (0, width - w.shape[1])))


def _even_mixer(h, batch, seq, w_in, gate_b, norm_w, conv_w, conv_b, wa, ba, wx, bx, lam):
    a4 = 4 * MIX_HALF
    ng = 2 * HEADS
    w_main = jnp.concatenate([w_in[:, :a4], w_in[:, a4 + ng:]], axis=1).astype(BF16)
    w_gate = _pad_cols(w_in[:, a4:a4 + ng]).astype(BF16)
    proj, gates = _proj(h, w_main, w_gate)
    ya = _mlstm(proj, gates, gate_b, norm_w, batch, seq)
    yb = _rglru(proj, conv_w, conv_b, wa, ba, wx, bx, lam, batch, seq)
    return ya, yb


def _odd_mixer(h, batch, seq, w_in, lam_re, lam_im, b_re, b_im, c_re, c_im, d_skip, log_dt,
               glu_w, glu_b, gate_w, gate_b, norm_w):
    c0 = MIX_HALF
    c1 = c0 + HEADS * GLA_DK
    c2 = c1 + HEADS * GLA_DK
    c3 = c2 + MIX_HALF
    c4 = c3 + MIX_HALF
    w_main = jnp.concatenate([w_in[:, :c0], _pad_heads(w_in[:, c0:c1], 1), _pad_heads(w_in[:, c1:c2], 1),
                              w_in[:, c2:c4]], axis=1).astype(BF16)
    w_low = _pad_cols(w_in[:, c4:]).astype(BF16)
    proj, glow = _proj(h, w_main, w_low)
    tables = _s5_tables(lam_re, lam_im, b_re, b_im, c_re, c_im, log_dt)
    yc = _s5(proj, tables, d_skip, glu_w, glu_b, batch, seq)
    gw = jnp.pad(_pad_heads(gate_w, 1), ((0, LANES - GLA_GATE_RANK), (0, 0))).astype(BF16)
    gb = _pad_heads(gate_b.reshape(1, -1), 1)
    yd = _gla(proj, glow, gw, gb, norm_w, batch, seq)
    return yc, yd


def kernel(x, ln1_w, ln1_b, ln2_w, ln2_b, w_out, w_in_even, mlstm_gate_b, mlstm_norm_w, lru_conv_w, lru_conv_b, lru_wa, lru_ba, lru_wx, lru_bx, lru_lambda, w_in_odd, s5_lam_re, s5_lam_im, s5_b_re, s5_b_im, s5_c_re, s5_c_im, s5_d, s5_log_dt, s5_glu_w, s5_glu_b, gla_gate_w, gla_gate_b, gla_norm_w, router_w, router_bias, exp_w_gate, exp_w_up, exp_w_down, sh_w_gate, sh_w_up, sh_w_down):
    batch, seq, d = x.shape
    h = x.reshape(batch * seq, d)
    for layer in range(DEPTH):
        j = layer // 2
        if layer % 2 == 0:
            y1, y2 = _even_mixer(h, batch, seq, w_in_even[j], mlstm_gate_b[j], mlstm_norm_w[j],
                                 lru_conv_w[j], lru_conv_b[j], lru_wa[j], lru_ba[j], lru_wx[j],
                                 lru_bx[j], lru_lambda[j])
        else:
            y1, y2 = _odd_mixer(h, batch, seq, w_in_odd[j], s5_lam_re[j], s5_lam_im[j], s5_b_re[j],
                                s5_b_im[j], s5_c_re[j], s5_c_im[j], s5_d[j], s5_log_dt[j],
                                s5_glu_w[j], s5_glu_b[j], gla_gate_w[j], gla_gate_b[j], gla_norm_w[j])
        h, hpk, idx, rank, wk, counts = _out_proj_ln_route(y1, y2, h, w_out[layer], ln1_w[layer], ln1_b[layer],
                                                           router_w[layer], router_bias[layer])
        h = _moe_ln(h, hpk, idx, rank, wk, counts, exp_w_gate, exp_w_up, exp_w_down, layer,
                    sh_w_gate[layer], sh_w_up[layer], sh_w_down[layer], ln2_w[layer], ln2_b[layer])
    return h.reshape(batch, seq, d)
```

```python
import functools
import math

import jax
import jax.numpy as jnp
from jax import lax
from jax.experimental import pallas as pl
from jax.experimental.pallas import tpu as pltpu
from jax.experimental.pallas import tpu_sc as plsc

F32 = jnp.float32
BF16 = jnp.bfloat16

D_MODEL = 1024
DEPTH = 2
MIX_HALF = 512
HEADS = 4
HEAD_DIM = 128
GLA_DK = 64
GLA_CHUNK = 64
GLA_GATE_RANK = 16
GLA_GATE_TEMP = 16.0
LRU_C = 8.0
LRU_CONV = 4
S5_GROUP = 16
S5_GROUPS = 32
S5_STATE = 64
S5_LANES = S5_GROUPS * S5_STATE
S5_BLOCKS = 4
N_EXPERTS = 64
N_GROUPS = 8
GROUP_SIZE = N_EXPERTS // N_GROUPS
TOP_K = 8
TOPK_GROUPS = 4
D_EXPERT = 256
ROUTED_SCALE = 2.5
ALPHA = (2.0 * DEPTH) ** 0.25
EPS = 1e-5
LANES = 128
SUBLANES = 8
NEG_INF = float("-inf")

VMEM_LIMIT = 56 * 1024 * 1024

MLSTM_CHUNK = 128
MLSTM_TILE = 1024
LRU_TILE = 1024
LRU_LOG_STEPS = 3
LRU_UNROLL = 4
S5_TILE = 512
S5_LOG_STEPS = 3
S5_UNROLL = True
GLA_UNROLL = 4
GLA_TILE = 1024
PROJ_TILE = 512
OUT_TILE = 512
MOE_BLOCK = 768
COMBINE_TILE = 256
PACKED = D_MODEL // 2
SC_CHUNK = 64
SC_CORES = 2
SC_SUBCORES = 16
SC_WORKERS = SC_CORES * SC_SUBCORES


def _params(*sem):
    return pltpu.CompilerParams(dimension_semantics=sem, vmem_limit_bytes=VMEM_LIMIT)


def _split3(x):
    hi = x.astype(BF16)
    r1 = x - hi.astype(F32)
    mid = r1.astype(BF16)
    lo = (r1 - mid.astype(F32)).astype(BF16)
    return hi, mid, lo


def _dot(a, b):
    return jnp.dot(a, b, preferred_element_type=F32)


def _dot_nt(a, b):
    return lax.dot_general(a, b, (((1,), (1,)), ((), ())), preferred_element_type=F32)


def _dot_tn(a, b):
    return lax.dot_general(a, b, (((0,), (0,)), ((), ())), preferred_element_type=F32)


def _exact_left01(mask01_bf16, x):
    hi, mid, lo = _split3(x)
    return _dot(mask01_bf16, hi) + _dot(mask01_bf16, mid) + _dot(mask01_bf16, lo)


def _exact_right01(x, mask01_bf16):
    hi, mid, lo = _split3(x)
    return _dot(hi, mask01_bf16) + _dot(mid, mask01_bf16) + _dot(lo, mask01_bf16)


def _log_sigmoid(x):
    return jnp.minimum(x, 0.0) - jnp.log(1.0 + jnp.exp(-jnp.abs(x)))


def _sigmoid(x):
    return 1.0 / (1.0 + jnp.exp(-x))


def _gelu_tanh(x):
    c = math.sqrt(2.0 / math.pi)
    return 0.5 * x * (1.0 + jnp.tanh(c * (x + 0.044715 * (x * x * x))))


def _layer_norm(z, w, b):
    mu = jnp.mean(z, axis=-1, keepdims=True)
    zc = z - mu
    return zc * lax.rsqrt(jnp.mean(zc * zc, axis=-1, keepdims=True) + EPS) * w + b


def _proj_kernel(x_ref, w_ref, wg_ref, o_ref, og_ref):
    x = x_ref[...].astype(BF16)
    o_ref[...] = _dot(x, w_ref[...])
    og_ref[...] = _dot(x, wg_ref[...])


def _proj(x, w_main, w_small):
    t, d = x.shape
    n = w_main.shape[1]
    tm = PROJ_TILE
    return pl.pallas_call(
        _proj_kernel,
        grid=(t // tm,),
        in_specs=[pl.BlockSpec((tm, d), lambda i: (i, 0)),
                  pl.BlockSpec((d, n), lambda i: (0, 0)),
                  pl.BlockSpec((d, LANES), lambda i: (0, 0))],
        out_specs=[pl.BlockSpec((tm, n), lambda i: (i, 0)),
                   pl.BlockSpec((tm, LANES), lambda i: (i, 0))],
        out_shape=[jax.ShapeDtypeStruct((t, n), F32), jax.ShapeDtypeStruct((t, LANES), F32)],
        compiler_params=_params("parallel"),
        name="in_proj",
    )(x, w_main, w_small)


def _mlstm_kernel(q_ref, k_ref, v_ref, o_ref, gc_ref, gr_ref, bc_ref, br_ref, nw_ref,
                  y_ref, c_ref, m_ref, *, chunk, n_chunks):
    L = chunk

    @pl.when(pl.program_id(1) == 0)
    def _():
        c_ref[...] = jnp.zeros_like(c_ref)
        m_ref[...] = jnp.zeros_like(m_ref)

    ri = lax.broadcasted_iota(jnp.int32, (L, L), 0)
    ci = lax.broadcasted_iota(jnp.int32, (L, L), 1)
    causal = ci <= ri
    tril = causal.astype(BF16)
    triu = (ri <= ci).astype(BF16)
    ones_v = jnp.ones((L, HEAD_DIM), BF16)
    scale = HEAD_DIM ** -0.5

    def body(c, carry):
        r0 = pl.multiple_of(c * L, L)
        g_col = gc_ref[pl.ds(r0, L), :] + bc_ref[...]
        g_row = gr_ref[c] + br_ref[...]
        b_col_all = _exact_left01(tril, _log_sigmoid(g_col))
        b_row_all = _exact_right01(_log_sigmoid(g_row), triu)
        for h in range(HEADS):
            lo = h * HEAD_DIM
            q = q_ref[pl.ds(r0, L), lo:lo + HEAD_DIM].astype(BF16)
            k = k_ref[pl.ds(r0, L), lo:lo + HEAD_DIM] * scale
            v = v_ref[pl.ds(r0, L), lo:lo + HEAD_DIM].astype(BF16)
            v_aug = jnp.concatenate([v, ones_v], axis=1)
            i_rep = jnp.broadcast_to(g_col[:, h:h + 1], (L, LANES))
            b_rep = jnp.broadcast_to(b_col_all[:, HEADS + h:HEADS + h + 1], (L, LANES))
            i_row = g_row[h:h + 1, :]
            b_row = b_row_all[HEADS + h:HEADS + h + 1, :]
            b_last = b_rep[L - 1:L, :]
            m_prev = m_ref[h:h + 1, :]
            c_prev = c_ref[h]

            d_mat = jnp.where(causal, b_rep - b_row + i_row, NEG_INF)
            m_inter = b_rep + m_prev
            m_i = jnp.maximum(m_inter, jnp.max(d_mat, axis=1, keepdims=True))
            s = _dot_nt(q, k.astype(BF16)) * jnp.exp(d_mat - m_i)
            w_inter = jnp.exp(m_inter - m_i)
            intra = _dot(s.astype(BF16), v_aug)
            inter = _dot(q, c_prev.astype(BF16))
            num = intra[:, :HEAD_DIM] + w_inter * inter[:, :HEAD_DIM]
            den = intra[:, HEAD_DIM:] + w_inter * inter[:, HEAD_DIM:]
            hh = num / jnp.maximum(jnp.abs(den), jnp.exp(-m_i))

            w_loc = b_last - b_rep + i_rep
            m_loc = jnp.max(w_loc, axis=0, keepdims=True)
            kp = (k * jnp.exp(w_loc - m_loc)).astype(BF16)
            c_loc = _dot_tn(kp, v_aug)
            m_new = jnp.maximum(b_last + m_prev, m_loc)
            keep = jnp.exp(b_last + m_prev - m_new)
            add = jnp.exp(m_loc - m_new)
            c_ref[h] = (jnp.concatenate([keep, keep], axis=1) * c_prev
                        + jnp.concatenate([add, add], axis=1) * c_loc)
            m_ref[h:h + 1, :] = m_new

            hc = hh - jnp.mean(hh, axis=-1, keepdims=True)
            yn = hc * lax.rsqrt(jnp.mean(hc * hc, axis=-1, keepdims=True) + EPS)
            og = o_ref[pl.ds(r0, L), lo:lo + HEAD_DIM]
            y_ref[pl.ds(r0, L), lo:lo + HEAD_DIM] = yn * nw_ref[:, lo:lo + HEAD_DIM] * _sigmoid(og)
        return carry

    lax.fori_loop(0, n_chunks, body, 0)


def _mlstm(proj, gates, gate_b, norm_w, batch, seq):
    t = batch * seq
    L = MLSTM_CHUNK
    assert L == LANES, "the kernel keeps per-row gate terms replicated over one vreg of lanes"
    ts = MLSTM_TILE
    nj = seq // ts
    nc = ts // L
    g_row = gates[:, :2 * HEADS].reshape(t // L, L, 2 * HEADS).transpose(0, 2, 1)
    b_col = jnp.zeros((1, LANES), F32).at[0, :2 * HEADS].set(gate_b)
    b_row = gate_b.reshape(2 * HEADS, 1)
    blk = lambda col: pl.BlockSpec((ts, MIX_HALF), lambda b, j, col=col: (b * nj + j, col))
    kern = functools.partial(_mlstm_kernel, chunk=L, n_chunks=nc)
    return pl.pallas_call(
        kern,
        grid=(batch, nj),
        in_specs=[blk(0), blk(1), blk(2), blk(3),
                  pl.BlockSpec((ts, LANES), lambda b, j: (b * nj + j, 0)),
                  pl.BlockSpec((nc, 2 * HEADS, L), lambda b, j: (b * nj + j, 0, 0)),
                  pl.BlockSpec((1, LANES), lambda b, j: (0, 0)),
                  pl.BlockSpec((2 * HEADS, 1), lambda b, j: (0, 0)),
                  pl.BlockSpec((1, MIX_HALF), lambda b, j: (0, 0))],
        out_specs=pl.BlockSpec((ts, MIX_HALF), lambda b, j: (b * nj + j, 0)),
        out_shape=jax.ShapeDtypeStruct((t, MIX_HALF), F32),
        scratch_shapes=[pltpu.VMEM((HEADS, HEAD_DIM, 2 * HEAD_DIM), F32),
                        pltpu.VMEM((8, LANES), F32)],
        compiler_params=_params("arbitrary", "arbitrary"),
        name="mlstm",
    )(proj, proj, proj, proj, gates, g_row, b_col, b_row, norm_w.reshape(1, MIX_HALF))


def _rglru_kernel(xb_ref, gb_ref, cw_ref, cb_ref, wa_ref, ba_ref, wx_ref, bx_ref, lam_ref,
                  y_ref, xext_ref, h_ref, a_ref, u_ref, *, tile):
    @pl.when(pl.program_id(1) == 0)
    def _():
        xext_ref[0:8, :] = jnp.zeros((8, MIX_HALF), F32)
        h_ref[...] = jnp.zeros_like(h_ref)

    x = xb_ref[...]
    xext_ref[8:8 + tile, :] = x
    xc = cb_ref[...] + cw_ref[LRU_CONV - 1:LRU_CONV, :] * x
    for tap in range(LRU_CONV - 1):
        back = LRU_CONV - 1 - tap
        xc = xc + cw_ref[tap:tap + 1, :] * xext_ref[8 - back:8 - back + tile, :]
    xext_ref[0:8, :] = x[tile - 8:tile, :]

    xc16 = xc.astype(BF16)
    r_parts, i_parts = [], []
    for h in range(HEADS):
        lo = h * HEAD_DIM
        xh = xc16[:, lo:lo + HEAD_DIM]
        r_parts.append(_dot(xh, wa_ref[h]))
        i_parts.append(_dot(xh, wx_ref[h]))
    r = _sigmoid(jnp.concatenate(r_parts, axis=1) + ba_ref[...])
    ig = _sigmoid(jnp.concatenate(i_parts, axis=1) + bx_ref[...])
    lam = lam_ref[...]
    softplus_neg = jnp.maximum(-lam, 0.0) + jnp.log(1.0 + jnp.exp(-jnp.abs(lam)))
    log_a = -LRU_C * r * softplus_neg
    a = jnp.exp(log_a)
    th = jnp.tanh(log_a)
    u = jnp.sqrt(-2.0 * th / (1.0 - th)) * ig * xc

    a_ref[...] = a
    u_ref[...] = u
    rows = lax.broadcasted_iota(jnp.int32, (SUBLANES, MIX_HALF), 0)

    def group(i, h_prev):
        r0 = pl.multiple_of(i * SUBLANES, SUBLANES)
        ag = a_ref[pl.ds(r0, SUBLANES), :]
        ug = u_ref[pl.ds(r0, SUBLANES), :]
        for k in range(LRU_LOG_STEPS):
            keep = rows >= (1 << k)
            ug = ag * jnp.where(keep, pltpu.roll(ug, 1 << k, 0), 0.0) + ug
            ag = ag * jnp.where(keep, pltpu.roll(ag, 1 << k, 0), 1.0)
        hg = ug + ag * h_prev
        u_ref[pl.ds(r0, SUBLANES), :] = hg
        return hg[SUBLANES - 1:SUBLANES, :]

    h_last = lax.fori_loop(0, tile // SUBLANES, group, h_ref[0:1, :], unroll=LRU_UNROLL)
    h_ref[...] = jnp.broadcast_to(h_last, h_ref.shape)
    y_ref[...] = u_ref[...] * _gelu_tanh(gb_ref[...])


def _rglru(proj, conv_w, conv_b, wa, ba, wx, bx, lam, batch, seq):
    t = batch * seq
    ts = LRU_TILE
    nj = seq // ts
    row = lambda a: a.reshape(1, MIX_HALF)
    const2 = lambda shape: pl.BlockSpec(shape, lambda b, j: (0, 0))
    const3 = lambda shape: pl.BlockSpec(shape, lambda b, j: (0, 0, 0))
    blk = lambda col: pl.BlockSpec((ts, MIX_HALF), lambda b, j, col=col: (b * nj + j, col))
    return pl.pallas_call(
        functools.partial(_rglru_kernel, tile=ts),
        grid=(batch, nj),
        in_specs=[blk(4), blk(5), const2((LRU_CONV, MIX_HALF)), const2((1, MIX_HALF)),
                  const3((HEADS, HEAD_DIM, HEAD_DIM)), const2((1, MIX_HALF)),
                  const3((HEADS, HEAD_DIM, HEAD_DIM)), const2((1, MIX_HALF)), const2((1, MIX_HALF))],
        out_specs=pl.BlockSpec((ts, MIX_HALF), lambda b, j: (b * nj + j, 0)),
        out_shape=jax.ShapeDtypeStruct((t, MIX_HALF), F32),
        scratch_shapes=[pltpu.VMEM((ts + 8, MIX_HALF), F32), pltpu.VMEM((8, MIX_HALF), F32),
                        pltpu.VMEM((ts, MIX_HALF), F32), pltpu.VMEM((ts, MIX_HALF), F32)],
        compiler_params=_params("arbitrary", "arbitrary"),
        name="rglru",
    )(proj, proj, conv_w, row(conv_b), wa.astype(BF16), row(ba), wx.astype(BF16), row(bx), row(lam))


def _s5_kernel(u_ref, bre_ref, bim_ref, cre_ref, cim_ref, mre_ref, mim_ref, pre_ref, pim_ref, d_ref, gw_ref,
               gb_ref, y_ref, xr_ref, xi_ref, cr_ref, ci_ref, *, tile):
    @pl.when(pl.program_id(1) == 0)
    def _():
        cr_ref[...] = jnp.zeros_like(cr_ref)
        ci_ref[...] = jnp.zeros_like(ci_ref)

    u = u_ref[...]
    u16 = u.astype(BF16)
    blk_c = MIX_HALF // S5_BLOCKS
    blk_s = S5_LANES // S5_BLOCKS
    parts = []
    for j in range(S5_BLOCKS):
        lanes = slice(j * blk_s, (j + 1) * blk_s)
        uj = u16[:, j * blk_c:(j + 1) * blk_c]
        xr_ref[:, lanes] = _dot(uj, bre_ref[j])
        xi_ref[:, lanes] = _dot(uj, bim_ref[j])

        def group(i, carry, lanes=lanes):
            cr, ci = carry
            r0 = pl.multiple_of(i * SUBLANES, SUBLANES)
            xr = xr_ref[pl.ds(r0, SUBLANES), lanes]
            xi = xi_ref[pl.ds(r0, SUBLANES), lanes]
            for k in range(S5_LOG_STEPS):
                sr = pltpu.roll(xr, 1 << k, 0)
                si = pltpu.roll(xi, 1 << k, 0)
                mr = mre_ref[k, :, lanes]
                mi = mim_ref[k, :, lanes]
                xr, xi = xr + mr * sr - mi * si, xi + mr * si + mi * sr
            pr = pre_ref[:, lanes]
            pi = pim_ref[:, lanes]
            xr, xi = xr + pr * cr - pi * ci, xi + pr * ci + pi * cr
            xr_ref[pl.ds(r0, SUBLANES), lanes] = xr
            xi_ref[pl.ds(r0, SUBLANES), lanes] = xi
            return xr[SUBLANES - 1:SUBLANES, :], xi[SUBLANES - 1:SUBLANES, :]

        cr, ci = lax.fori_loop(0, tile // SUBLANES, group, (cr_ref[0:1, lanes], ci_ref[0:1, lanes]),
                               unroll=S5_UNROLL)
        cr_ref[0:1, lanes] = cr
        ci_ref[0:1, lanes] = ci
        parts.append(_dot(xr_ref[:, lanes].astype(BF16), cre_ref[j])
                     - _dot(xi_ref[:, lanes].astype(BF16), cim_ref[j]))
    y = jnp.concatenate(parts, axis=1) + d_ref[...] * u
    g = _gelu_tanh(y)
    y_ref[...] = g * _sigmoid(_dot(g.astype(BF16), gw_ref[...]) + gb_ref[...])


def _s5_tables(lam_re, lam_im, b_re, b_im, c_re, c_im, log_dt):
    lr, li = lam_re.astype(F32), lam_im.astype(F32)
    dt = jnp.exp(log_dt.astype(F32))[:, None]
    mag = jnp.exp(lr * dt)
    abar_re = mag * jnp.cos(li * dt)
    abar_im = mag * jnp.sin(li * dt)
    den = lr * lr + li * li
    nr = abar_re - 1.0
    coef_re = (nr * lr + abar_im * li) / den
    coef_im = (abar_im * lr - nr * li) / den
    bbar_re = coef_re[..., None] * b_re - coef_im[..., None] * b_im
    bbar_im = coef_re[..., None] * b_im + coef_im[..., None] * b_re
    gpb = S5_GROUPS // S5_BLOCKS
    eye = jnp.eye(gpb, dtype=F32)

    def in_map(bb):
        bb = bb.reshape(S5_BLOCKS, gpb, S5_STATE, S5_GROUP)
        return jnp.einsum("jgph,gk->jghkp", bb, eye).reshape(S5_BLOCKS, gpb * S5_GROUP, gpb * S5_STATE)

    def out_map(cc):
        cc = cc.reshape(S5_BLOCKS, gpb, S5_GROUP, S5_STATE)
        return jnp.einsum("jghp,gk->jgpkh", cc, eye).reshape(S5_BLOCKS, gpb * S5_STATE, gpb * S5_GROUP)

    def power(n):
        n = jnp.asarray(n, F32)[..., None, None]
        pmag = jnp.exp(n * (lr * dt))
        shape = n.shape[:-2] + (S5_LANES,)
        return (pmag * jnp.cos(n * (li * dt))).reshape(shape), (pmag * jnp.sin(n * (li * dt))).reshape(shape)

    row = jnp.arange(SUBLANES)
    step = 2 ** jnp.arange(S5_LOG_STEPS)
    s_re, s_im = power(step)
    keep = (row[None, :] >= step[:, None])[..., None]
    m_re = jnp.where(keep, s_re[:, None, :], 0.0)
    m_im = jnp.where(keep, s_im[:, None, :], 0.0)
    p_re, p_im = power(row + 1)
    return (in_map(bbar_re).astype(BF16), in_map(bbar_im).astype(BF16),
            out_map(c_re.astype(F32)).astype(BF16), out_map(c_im.astype(F32)).astype(BF16),
            m_re, m_im, p_re, p_im)


def _s5(proj, tables, d_skip, glu_w, glu_b, batch, seq):
    t = batch * seq
    ts = S5_TILE
    nj = seq // ts
    bre, bim, cre, cim, m_re, m_im, p_re, p_im = tables
    blk_c = MIX_HALF // S5_BLOCKS
    blk_s = S5_LANES // S5_BLOCKS
    const2 = lambda shape: pl.BlockSpec(shape, lambda b, j: (0, 0))
    const3 = lambda shape: pl.BlockSpec(shape, lambda b, j: (0, 0, 0))
    return pl.pallas_call(
        functools.partial(_s5_kernel, tile=ts),
        grid=(batch, nj),
        in_specs=[pl.BlockSpec((ts, MIX_HALF), lambda b, j: (b * nj + j, 0)),
                  const3((S5_BLOCKS, blk_c, blk_s)), const3((S5_BLOCKS, blk_c, blk_s)),
                  const3((S5_BLOCKS, blk_s, blk_c)), const3((S5_BLOCKS, blk_s, blk_c)),
                  const3(m_re.shape), const3(m_im.shape), const2(p_re.shape), const2(p_im.shape),
                  const2((1, MIX_HALF)), const2((MIX_HALF, MIX_HALF)), const2((1, MIX_HALF))],
        out_specs=pl.BlockSpec((ts, MIX_HALF), lambda b, j: (b * nj + j, 0)),
        out_shape=jax.ShapeDtypeStruct((t, MIX_HALF), F32),
        scratch_shapes=[pltpu.VMEM((ts, S5_LANES), F32), pltpu.VMEM((ts, S5_LANES), F32),
                        pltpu.VMEM((8, S5_LANES), F32), pltpu.VMEM((8, S5_LANES), F32)],
        compiler_params=_params("arbitrary", "arbitrary"),
        name="s5",
    )(proj, bre, bim, cre, cim, m_re, m_im, p_re, p_im, d_skip.reshape(1, MIX_HALF), glu_w.astype(BF16),
      glu_b.reshape(1, MIX_HALF))


def _gla_kernel(q_ref, k_ref, v_ref, r_ref, gl_ref, gw_ref, gb_ref, nw_ref, y_ref,
                st_ref, qd_ref, ki_ref, ke_ref, v16_ref, dec_ref, o_ref, *, tile, chunk):
    L = chunk
    nc = tile // L

    @pl.when(pl.program_id(1) == 0)
    def _():
        st_ref[...] = jnp.zeros_like(st_ref)

    z = _dot(gl_ref[...].astype(BF16), gw_ref[...]) + gb_ref[...]
    bcum = _log_sigmoid(z) * (1.0 / GLA_GATE_TEMP)
    row_in_chunk = lax.broadcasted_iota(jnp.int32, bcum.shape, 0) & (L - 1)
    s = 1
    while s < L:
        bcum = bcum + jnp.where(row_in_chunk >= s, pltpu.roll(bcum, s, 0), 0.0)
        s *= 2
    b3 = bcum.reshape(nc, L, MIX_HALF)
    b_last = b3[:, L - 1:L, :]
    k = k_ref[...]
    qd_ref[...] = (q_ref[...] * (GLA_DK ** -0.5) * jnp.exp(bcum)).astype(BF16)
    ki_ref[...] = (k * jnp.exp(-bcum)).astype(BF16)
    ke_ref[...] = (k.reshape(nc, L, MIX_HALF) * jnp.exp(b_last - b3)).reshape(tile, MIX_HALF).astype(BF16)
    v16_ref[...] = v_ref[...].astype(BF16)
    dec_ref[...] = jnp.exp(b_last)

    ri = lax.broadcasted_iota(jnp.int32, (L, L), 0)
    ci = lax.broadcasted_iota(jnp.int32, (L, L), 1)
    causal = ci <= ri

    def body(c, carry):
        r0 = pl.multiple_of(c * L, L)
        dec = dec_ref[c]
        for h in range(HEADS):
            lo = h * HEAD_DIM
            q_dec = qd_ref[pl.ds(r0, L), lo:lo + HEAD_DIM]
            v = v16_ref[pl.ds(r0, L), lo:lo + HEAD_DIM]
            st = st_ref[h]
            att = jnp.where(causal, _dot_nt(q_dec, ki_ref[pl.ds(r0, L), lo:lo + HEAD_DIM]), 0.0)
            o_ref[pl.ds(r0, L), lo:lo + HEAD_DIM] = (_dot(att.astype(BF16), v)
                                                     + _dot_nt(q_dec, st.astype(BF16)))
            st_ref[h] = dec[:, lo:lo + HEAD_DIM] * st + _dot_tn(v, ke_ref[pl.ds(r0, L), lo:lo + HEAD_DIM])
        return carry

    lax.fori_loop(0, nc, body, 0, unroll=GLA_UNROLL)

    rg = r_ref[...]
    gate = nw_ref[...] * (rg * _sigmoid(rg))
    for h in range(HEADS):
        lo = h * HEAD_DIM
        o = o_ref[:, lo:lo + HEAD_DIM]
        yn = o * lax.rsqrt(jnp.mean(o * o, axis=-1, keepdims=True) + EPS)
        y_ref[:, lo:lo + HEAD_DIM] = yn * gate[:, lo:lo + HEAD_DIM]


def _gla(proj, glow, gate_w, gate_b, norm_w, batch, seq):
    t = batch * seq
    ts = GLA_TILE
    nj = seq // ts
    blk = lambda col: pl.BlockSpec((ts, MIX_HALF), lambda b, j, col=col: (b * nj + j, col))
    const2 = lambda shape: pl.BlockSpec(shape, lambda b, j: (0, 0))
    return pl.pallas_call(
        functools.partial(_gla_kernel, tile=ts, chunk=GLA_CHUNK),
        grid=(batch, nj),
        in_specs=[blk(1), blk(2), blk(3), blk(4),
                  pl.BlockSpec((ts, LANES), lambda b, j: (b * nj + j, 0)),
                  const2((LANES, MIX_HALF)), const2((1, MIX_HALF)), const2((1, MIX_HALF))],
        out_specs=pl.BlockSpec((ts, MIX_HALF), lambda b, j: (b * nj + j, 0)),
        out_shape=jax.ShapeDtypeStruct((t, MIX_HALF), F32),
        scratch_shapes=[pltpu.VMEM((HEADS, HEAD_DIM, HEAD_DIM), F32),
                        pltpu.VMEM((ts, MIX_HALF), BF16), pltpu.VMEM((ts, MIX_HALF), BF16),
                        pltpu.VMEM((ts, MIX_HALF), BF16), pltpu.VMEM((ts, MIX_HALF), BF16),
                        pltpu.VMEM((ts // GLA_CHUNK, 1, MIX_HALF), F32),
                        pltpu.VMEM((ts, MIX_HALF), F32)],
        compiler_params=_params("arbitrary", "arbitrary"),
        name="gla",
    )(proj, proj, proj, proj, glow, gate_w, gate_b, norm_w.reshape(1, MIX_HALF))


def _pad_heads(w, axis):
    shape = list(w.shape)
    shape[axis:axis + 1] = [HEADS, GLA_DK]
    w = w.reshape(shape)
    pad = [(0, 0)] * w.ndim
    pad[axis + 1] = (0, HEAD_DIM - GLA_DK)
    w = jnp.pad(w, pad)
    shape[axis:axis + 2] = [HEADS * HEAD_DIM]
    return w.reshape(shape)


def _pack_bf16_pairs(z):
    hi = lax.bitcast_convert_type(z[:, :PACKED].astype(BF16).astype(F32), jnp.uint32)
    lo = lax.bitcast_convert_type(z[:, PACKED:].astype(BF16).astype(F32), jnp.uint32)
    word = (hi & jnp.uint32(0xFFFF0000)) | lax.shift_right_logical(lo, jnp.uint32(16))
    return lax.bitcast_convert_type(word, jnp.int32)


def _unpack_bf16_pairs(p):
    word = lax.bitcast_convert_type(p, jnp.uint32)
    hi = lax.bitcast_convert_type(word & jnp.uint32(0xFFFF0000), F32)
    lo = lax.bitcast_convert_type(lax.shift_left(word, jnp.uint32(16)), F32)
    return hi, lo


def _out_kernel(ya_ref, yb_ref, h_ref, w_ref, lw_ref, lb_ref, rw_ref, rb_ref,
                o_ref, opk_ref, idx_ref, rank_ref, wk_ref, cnt_ref, base_ref, *, tile):
    mixed = jnp.concatenate([ya_ref[...], yb_ref[...]], axis=1).astype(BF16)
    z = ALPHA * h_ref[...] + _dot(mixed, w_ref[...])
    out = _layer_norm(z, lw_ref[...], lb_ref[...])
    o_ref[...] = out
    opk_ref[...] = _pack_bf16_pairs(out)
    _route_tile(out, rw_ref, rb_ref, idx_ref, rank_ref, wk_ref, cnt_ref, base_ref, tile)


def _out_proj_ln_route(ya, yb, h, w_out, ln_w, ln_b, router_w, router_bias):
    t = h.shape[0]
    tm = OUT_TILE
    const = lambda shape: pl.BlockSpec(shape, lambda i: (0, 0))
    per_tok = lambda dt: jax.ShapeDtypeStruct((TOP_K, t), dt)
    tok_blk = pl.BlockSpec((TOP_K, tm), lambda i: (0, i))
    return pl.pallas_call(
        functools.partial(_out_kernel, tile=tm),
        grid=(t // tm,),
        in_specs=[pl.BlockSpec((tm, MIX_HALF), lambda i: (i, 0)),
                  pl.BlockSpec((tm, MIX_HALF), lambda i: (i, 0)),
                  pl.BlockSpec((tm, D_MODEL), lambda i: (i, 0)),
                  const((D_MODEL, D_MODEL)), const((1, D_MODEL)), const((1, D_MODEL)),
                  const((N_EXPERTS, D_MODEL)), const((N_EXPERTS, 1))],
        out_specs=[pl.BlockSpec((tm, D_MODEL), lambda i: (i, 0)),
                   pl.BlockSpec((tm, PACKED), lambda i: (i, 0)),
                   tok_blk, tok_blk, tok_blk, const((N_EXPERTS, LANES))],
        out_shape=[jax.ShapeDtypeStruct((t, D_MODEL), F32), jax.ShapeDtypeStruct((t, PACKED), jnp.int32),
                   per_tok(jnp.int32), per_tok(jnp.int32), per_tok(F32),
                   jax.ShapeDtypeStruct((N_EXPERTS, LANES), F32)],
        scratch_shapes=[pltpu.VMEM((N_EXPERTS, LANES), F32)],
        compiler_params=_params("arbitrary"),
        name="out_proj_ln_route",
    )(ya, yb, h, w_out.astype(BF16), ln_w.reshape(1, D_MODEL), ln_b.reshape(1, D_MODEL),
      router_w.T, router_bias.reshape(N_EXPERTS, 1))


def _first_index(hit, idx, big):
    return jnp.min(jnp.where(hit, idx, big), axis=0, keepdims=True)


def _route_tile(h, w_ref, b_ref, idx_ref, rank_ref, wk_ref, cnt_ref, base_ref, tile):
    @pl.when(pl.program_id(0) == 0)
    def _():
        base_ref[...] = jnp.zeros_like(base_ref)

    h_hi, h_mid, _ = _split3(h)
    w_hi, w_mid, _ = _split3(w_ref[...])
    logits = _dot_nt(w_hi, h_hi) + _dot_nt(w_hi, h_mid) + _dot_nt(w_mid, h_hi)
    scores = _sigmoid(logits)
    biased = scores + b_ref[...]

    sub = lax.broadcasted_iota(jnp.int32, (GROUP_SIZE, tile), 0)
    grp_rows = []
    for g in range(N_GROUPS):
        xg = biased[g * GROUP_SIZE:(g + 1) * GROUP_SIZE, :]
        m1 = jnp.max(xg, axis=0, keepdims=True)
        i1 = _first_index(xg == m1, sub, GROUP_SIZE)
        m2 = jnp.max(jnp.where(sub == i1, NEG_INF, xg), axis=0, keepdims=True)
        grp_rows.append(m1 + m2)
    gs = jnp.concatenate(grp_rows, axis=0)
    gsel = jnp.zeros((N_GROUPS, tile), F32)
    for _ in range(TOPK_GROUPS):
        mx = jnp.max(gs, axis=0, keepdims=True)
        hit = sub == _first_index(gs == mx, sub, N_GROUPS)
        gsel = jnp.where(hit, 1.0, gsel)
        gs = jnp.where(hit, NEG_INF, gs)
    emask = jnp.concatenate(
        [jnp.broadcast_to(gsel[g:g + 1, :], (GROUP_SIZE, tile)) for g in range(N_GROUPS)], axis=0)

    eidx = lax.broadcasted_iota(jnp.int32, (N_EXPERTS, tile), 0)
    cand = jnp.where(emask > 0.5, biased, NEG_INF)
    sel = jnp.zeros((N_EXPERTS, tile), F32)
    hits, picks = [], []
    for _ in range(TOP_K):
        mx = jnp.max(cand, axis=0, keepdims=True)
        first = _first_index(cand == mx, eidx, N_EXPERTS)
        hit = eidx == first
        hits.append(hit)
        picks.append(first)
        sel = jnp.where(hit, 1.0, sel)
        cand = jnp.where(hit, NEG_INF, cand)
    picked = jnp.where(sel > 0.5, scores, 0.0)
    wts = picked / jnp.sum(picked, axis=0, keepdims=True) * ROUTED_SCALE

    ri = lax.broadcasted_iota(jnp.int32, (tile, tile), 0)
    ci = lax.broadcasted_iota(jnp.int32, (tile, tile), 1)
    before = (ri < ci).astype(BF16)
    prior = _dot(sel.astype(BF16), before) + base_ref[:, 0:1]
    ranks = [jnp.sum(jnp.where(hit, prior, 0.0), axis=0, keepdims=True) for hit in hits]
    wsel = [jnp.sum(jnp.where(hit, wts, 0.0), axis=0, keepdims=True) for hit in hits]
    idx_ref[...] = jnp.concatenate(picks, axis=0)
    rank_ref[...] = jnp.concatenate(ranks, axis=0).astype(jnp.int32)
    wk_ref[...] = jnp.concatenate(wsel, axis=0)
    total = base_ref[...] + jnp.sum(sel, axis=1, keepdims=True)
    base_ref[...] = total
    cnt_ref[...] = total


def _silu(x):
    return x * _sigmoid(x)


def _sc_mesh():
    return plsc.VectorSubcoreMesh(core_axis_name="c", subcore_axis_name="s")


def _sc_worker_id():
    return lax.axis_index("s") * SC_CORES + lax.axis_index("c")


def _dispatch_rows(xpk, pos_chunks, n_rows):
    t = xpk.shape[0]
    n_ch = t // SC_WORKERS // SC_CHUNK

    @functools.partial(
        pl.kernel, mesh=_sc_mesh(),
        out_type=jax.ShapeDtypeStruct((n_rows, PACKED), jnp.int32),
        scratch_types=[pltpu.VMEM((TOP_K, SC_CHUNK), jnp.int32),
                       pltpu.VMEM((SC_CHUNK, PACKED), jnp.int32),
                       pltpu.SemaphoreType.DMA],
        name="moe_dispatch",
    )
    def scatter(x_hbm, pos_hbm, out_hbm, idx_v, rows_v, sem):
        wid = _sc_worker_id()

        @pl.loop(0, n_ch)
        def _(c):
            chunk = wid * n_ch + c
            off = pl.multiple_of(chunk * SC_CHUNK, SC_CHUNK)
            pltpu.sync_copy(pos_hbm.at[chunk], idx_v)
            pltpu.sync_copy(x_hbm.at[pl.ds(off, SC_CHUNK)], rows_v)
            copies = [pltpu.async_copy(rows_v, out_hbm.at[idx_v.at[k]], sem) for k in range(TOP_K)]
            for cp in copies:
                cp.wait()

    return scatter(xpk, pos_chunks)


def _gather_rows(table, idx):
    n = idx.shape[0]
    per_w = n // SC_WORKERS
    n_ch = per_w // SC_CHUNK
    assert n_ch % 2 == 0 and n_ch >= 2

    @functools.partial(
        pl.kernel, mesh=_sc_mesh(),
        out_type=jax.ShapeDtypeStruct((n, PACKED), jnp.int32),
        scratch_types=[pltpu.VMEM((n_ch, SC_CHUNK), jnp.int32),
                       pltpu.VMEM((SC_CHUNK, PACKED), jnp.int32), pltpu.VMEM((SC_CHUNK, PACKED), jnp.int32),
                       pltpu.SemaphoreType.DMA, pltpu.SemaphoreType.DMA,
                       pltpu.SemaphoreType.DMA, pltpu.SemaphoreType.DMA],
        name="moe_gather",
    )
    def gather(table_hbm, idx_hbm, out_hbm, idx_v, rows0, rows1, g0, g1, w0, w1):
        wid = _sc_worker_id()
        base = wid * per_w
        rows, g_sem, w_sem = (rows0, rows1), (g0, g1), (w0, w1)
        pltpu.sync_copy(idx_hbm.at[wid], idx_v)

        def fetch(c, b):
            return pltpu.make_async_copy(table_hbm.at[idx_v.at[c]], rows[b], g_sem[b])

        def flush(c, b):
            off = pl.multiple_of(base + c * SC_CHUNK, SC_CHUNK)
            return pltpu.make_async_copy(rows[b], out_hbm.at[pl.ds(off, SC_CHUNK)], w_sem[b])

        fetch(0, 0).start()

        @pl.loop(0, n_ch, step=2)
        def _(c0):
            for b in range(2):
                c = c0 + b
                fetch(c, b).wait()
                flush(c, b).start()

                @pl.when(c + 1 < n_ch)
                def _():
                    @pl.when(c >= 1)
                    def _():
                        flush(c - 1, 1 - b).wait()
                    fetch(c + 1, 1 - b).start()

        flush(n_ch - 2, 0).wait()
        flush(n_ch - 1, 1).wait()

    return gather(table, idx.reshape(SC_WORKERS, n_ch, SC_CHUNK))


def _unpacked_bf16(p):
    hi, lo = _unpack_bf16_pairs(p)
    return jnp.concatenate([hi.astype(BF16), lo.astype(BF16)], axis=1)


def _expert_kernel(be_ref, nu_ref, x_ref, wg_ref, wu_ref, wd_ref, y_ref, g16_ref, u16_ref, d16_ref):
    i = pl.program_id(0)
    changed = jnp.logical_or(i == 0, be_ref[i] != be_ref[jnp.maximum(i - 1, 0)])

    @pl.when(changed)
    def _():
        g16_ref[...] = wg_ref[0, 0].astype(BF16)
        u16_ref[...] = wu_ref[0, 0].astype(BF16)
        d16_ref[...] = wd_ref[0, 0].astype(BF16)

    @pl.when(i < nu_ref[0])
    def _():
        x = _unpacked_bf16(x_ref[...])
        hh = _silu(_dot(x, g16_ref[...])) * _dot(x, u16_ref[...])
        y_ref[...] = _pack_bf16_pairs(_dot(hh.astype(BF16), d16_ref[...]))


def _experts(block_e, n_used, xs, wg, wu, wd, layer):
    nb = block_e.shape[0]
    bm = MOE_BLOCK
    grid_spec = pltpu.PrefetchScalarGridSpec(
        num_scalar_prefetch=2,
        grid=(nb,),
        in_specs=[pl.BlockSpec((bm, PACKED), lambda i, be, nu: (jnp.minimum(i, nu[0] - 1), 0)),
                  pl.BlockSpec((1, 1, D_MODEL, D_EXPERT), lambda i, be, nu: (layer, be[i], 0, 0)),
                  pl.BlockSpec((1, 1, D_MODEL, D_EXPERT), lambda i, be, nu: (layer, be[i], 0, 0)),
                  pl.BlockSpec((1, 1, D_EXPERT, D_MODEL), lambda i, be, nu: (layer, be[i], 0, 0))],
        out_specs=pl.BlockSpec((bm, PACKED), lambda i, be, nu: (jnp.minimum(i, nu[0] - 1), 0)),
        scratch_shapes=[pltpu.VMEM((D_MODEL, D_EXPERT), BF16), pltpu.VMEM((D_MODEL, D_EXPERT), BF16),
                        pltpu.VMEM((D_EXPERT, D_MODEL), BF16)],
    )
    return pl.pallas_call(
        _expert_kernel,
        grid_spec=grid_spec,
        out_shape=jax.ShapeDtypeStruct((nb * bm, PACKED), jnp.int32),
        compiler_params=_params("arbitrary"),
        name="moe_experts",
    )(block_e, n_used, xs, wg, wu, wd)


def _combine_kernel(g_ref, wk_ref, h_ref, xpk_ref, sg_ref, su_ref, sd_ref, lw_ref, lb_ref, *rest):
    if len(rest) == 1:
        (o_ref,), next_proj = rest, None
    else:
        wm_ref, ws_ref, o_ref, proj_ref, small_ref = rest
        next_proj = (wm_ref, ws_ref, proj_ref, small_ref)
    x = _unpacked_bf16(xpk_ref[...])
    hs = _silu(_dot(x, sg_ref[...])) * _dot(x, su_ref[...])
    shared = _dot(hs.astype(BF16), sd_ref[...])
    acc_hi = shared[:, :PACKED]
    acc_lo = shared[:, PACKED:]
    wk = wk_ref[...]
    for k in range(TOP_K):
        y_hi, y_lo = _unpack_bf16_pairs(g_ref[k])
        w = wk[:, k:k + 1]
        acc_hi = acc_hi + w * y_hi
        acc_lo = acc_lo + w * y_lo
    ffn = jnp.concatenate([acc_hi, acc_lo], axis=1)
    out = _layer_norm(ALPHA * h_ref[...] + ffn, lw_ref[...], lb_ref[...])
    o_ref[...] = out
    if next_proj is not None:
        wm_ref, ws_ref, proj_ref, small_ref = next_proj
        out16 = out.astype(BF16)
        proj_ref[...] = _dot(out16, wm_ref[...])
        small_ref[...] = _dot(out16, ws_ref[...])


def _combine_ln(g, wk, h, xpk, sg, su, sd, ln_w, ln_b, next_w=None):
    t = h.shape[0]
    tm = COMBINE_TILE
    const = lambda shape: pl.BlockSpec(shape, lambda i: (0, 0))
    rows = lambda width: pl.BlockSpec((tm, width), lambda i: (i, 0))
    in_specs = [pl.BlockSpec((TOP_K, tm, PACKED), lambda i: (0, i, 0)), rows(TOP_K), rows(D_MODEL), rows(PACKED),
                const((D_MODEL, D_EXPERT)), const((D_MODEL, D_EXPERT)), const((D_EXPERT, D_MODEL)),
                const((1, D_MODEL)), const((1, D_MODEL))]
    args = [g, wk, h, xpk, sg.astype(BF16), su.astype(BF16), sd.astype(BF16),
            ln_w.reshape(1, D_MODEL), ln_b.reshape(1, D_MODEL)]
    out_specs = [rows(D_MODEL)]
    out_shape = [jax.ShapeDtypeStruct((t, D_MODEL), F32)]
    if next_w is not None:
        w_main, w_small = next_w
        n = w_main.shape[1]
        in_specs += [const((D_MODEL, n)), const((D_MODEL, LANES))]
        args += [w_main, w_small]
        out_specs += [rows(n), rows(LANES)]
        out_shape += [jax.ShapeDtypeStruct((t, n), F32), jax.ShapeDtypeStruct((t, LANES), F32)]
    res = pl.pallas_call(
        _combine_kernel,
        grid=(t // tm,),
        in_specs=in_specs,
        out_specs=out_specs,
        out_shape=out_shape,
        compiler_params=_params("parallel"),
        name="moe_combine_ln",
    )(*args)
    return res if next_w is not None else res[0]


def _moe_ln(h, hpk, idx, rank, wk, counts, wg, wu, wd, layer, sg, su, sd, ln_w, ln_b, next_w):
    t = h.shape[0]
    cnt = counts[:, 0].astype(jnp.int32)
    padded = (cnt + MOE_BLOCK - 1) // MOE_BLOCK * MOE_BLOCK
    pend = jnp.cumsum(padded)
    experts = jnp.arange(N_EXPERTS, dtype=jnp.int32)
    pstart_of_pick = jnp.sum(jnp.where(idx[:, :, None] == experts, pend - padded, 0), axis=-1)
    pos = pstart_of_pick + rank
    nb = -(-(t * TOP_K + N_EXPERTS * (MOE_BLOCK - 1)) // MOE_BLOCK)
    starts = jnp.arange(nb, dtype=jnp.int32) * MOE_BLOCK
    block_e = jnp.minimum(jnp.sum((pend[None, :] <= starts[:, None]).astype(jnp.int32), axis=1), N_EXPERTS - 1)
    n_used = (pend[-1] // MOE_BLOCK).astype(jnp.int32).reshape(1)
    pos_chunks = pos.reshape(TOP_K, t // SC_CHUNK, SC_CHUNK).transpose(1, 0, 2)
    xs = _dispatch_rows(hpk, pos_chunks, nb * MOE_BLOCK)
    ys = _experts(block_e, n_used, xs, wg, wu, wd, layer)
    g = _gather_rows(ys, pos.reshape(-1)).reshape(TOP_K, t, PACKED)
    return _combine_ln(g, wk.T, h, hpk, sg, su, sd, ln_w, ln_b, next_w)


def _pad_cols(w, width=LANES):
    return jnp.pad(w, ((0, 0), (0, width - w.shape[1])))


def _even_proj_weights(w_in):
    a4 = 4 * MIX_HALF
    ng = 2 * HEADS
    w_main = jnp.concatenate([w_in[:, :a4], w_in[:, a4 + ng:]], axis=1).astype(BF16)
    w_gate = _pad_cols(w_in[:, a4:a4 + ng]).astype(BF16)
    return w_main, w_gate


def _even_mixer(proj, gates, batch, seq, gate_b, norm_w, conv_w, conv_b, wa, ba, wx, bx, lam):
    ya = _mlstm(proj, gates, gate_b, norm_w, batch, seq)
    yb = _rglru(proj, conv_w, conv_b, wa, ba, wx, bx, lam, batch, seq)
    return ya, yb


def _odd_proj_weights(w_in):
    c0 = MIX_HALF
    c1 = c0 + HEADS * GLA_DK
    c2 = c1 + HEADS * GLA_DK
    c3 = c2 + MIX_HALF
    c4 = c3 + MIX_HALF
    w_main = jnp.concatenate([w_in[:, :c0], _pad_heads(w_in[:, c0:c1], 1), _pad_heads(w_in[:, c1:c2], 1),
                              w_in[:, c2:c4]], axis=1).astype(BF16)
    w_low = _pad_cols(w_in[:, c4:]).astype(BF16)
    return w_main, w_low


def _odd_mixer(proj, glow, batch, seq, lam_re, lam_im, b_re, b_im, c_re, c_im, d_skip, log_dt,
               glu_w, glu_b, gate_w, gate_b, norm_w):
    tables = _s5_tables(lam_re, lam_im, b_re, b_im, c_re, c_im, log_dt)
    yc = _s5(proj, tables, d_skip, glu_w, glu_b, batch, seq)
    gw = jnp.pad(_pad_heads(gate_w, 1), ((0, LANES - GLA_GATE_RANK), (0, 0))).astype(BF16)
    gb = _pad_heads(gate_b.reshape(1, -1), 1)
    yd = _gla(proj, glow, gw, gb, norm_w, batch, seq)
    return yc, yd


def kernel(x, ln1_w, ln1_b, ln2_w, ln2_b, w_out, w_in_even, mlstm_gate_b, mlstm_norm_w, lru_conv_w, lru_conv_b, lru_wa, lru_ba, lru_wx, lru_bx, lru_lambda, w_in_odd, s5_lam_re, s5_lam_im, s5_b_re, s5_b_im, s5_c_re, s5_c_im, s5_d, s5_log_dt, s5_glu_w, s5_glu_b, gla_gate_w, gla_gate_b, gla_norm_w, router_w, router_bias, exp_w_gate, exp_w_up, exp_w_down, sh_w_gate, sh_w_up, sh_w_down):
    batch, seq, d = x.shape
    proj_w = [_even_proj_weights(w_in_even[layer // 2]) if layer % 2 == 0 else _odd_proj_weights(w_in_odd[layer // 2])
              for layer in range(DEPTH)]
    h = x.reshape(batch * seq, d)
    proj, small = _proj(h, *proj_w[0])
    for layer in range(DEPTH):
        j = layer // 2
        if layer % 2 == 0:
            y1, y2 = _even_mixer(proj, small, batch, seq, mlstm_gate_b[j], mlstm_norm_w[j],
                                 lru_conv_w[j], lru_conv_b[j], lru_wa[j], lru_ba[j], lru_wx[j],
                                 lru_bx[j], lru_lambda[j])
        else:
            y1, y2 = _odd_mixer(proj, small, batch, seq, s5_lam_re[j], s5_lam_im[j], s5_b_re[j],
                                s5_b_im[j], s5_c_re[j], s5_c_im[j], s5_d[j], s5_log_dt[j],
                                s5_glu_w[j], s5_glu_b[j], gla_gate_w[j], gla_gate_b[j], gla_norm_w[j])
        h, hpk, idx, rank, wk, counts = _out_proj_ln_route(y1, y2, h, w_out[layer], ln1_w[layer], ln1_b[layer],
                                                           router_w[layer], router_bias[layer])
        next_w = proj_w[layer + 1] if layer + 1 < DEPTH else None
        res = _moe_ln(h, hpk, idx, rank, wk, counts, exp_w_gate, exp_w_up, exp_w_down, layer,
                      sh_w_gate[layer], sh_w_up[layer], sh_w_down[layer], ln2_w[layer], ln2_b[layer], next_w)
        if next_w is None:
            h = res
        else:
            h, proj, small = res
    return h.reshape(batch, seq, d)
```

```python
import functools
import math

import jax
import jax.numpy as jnp
from jax import lax
from jax.experimental import pallas as pl
from jax.experimental.pallas import tpu as pltpu
from jax.experimental.pallas import tpu_sc as plsc

F32 = jnp.float32
BF16 = jnp.bfloat16

D_MODEL = 1024
DEPTH = 2
MIX_HALF = 512
HEADS = 4
HEAD_DIM = 128
GLA_DK = 64
GLA_CHUNK = 64
GLA_GATE_RANK = 16
GLA_GATE_TEMP = 16.0
LRU_C = 8.0
LRU_CONV = 4
S5_GROUP = 16
S5_GROUPS = 32
S5_STATE = 64
S5_LANES = S5_GROUPS * S5_STATE
S5_BLOCKS = 4
N_EXPERTS = 64
N_GROUPS = 8
GROUP_SIZE = N_EXPERTS // N_GROUPS
TOP_K = 8
TOPK_GROUPS = 4
D_EXPERT = 256
ROUTED_SCALE = 2.5
ALPHA = (2.0 * DEPTH) ** 0.25
EPS = 1e-5
LANES = 128
SUBLANES = 8
NEG_INF = float("-inf")

VMEM_LIMIT = 56 * 1024 * 1024

MLSTM_CHUNK = 128
MLSTM_TILE = 1024
LRU_TILE = 1024
LRU_LOG_STEPS = 3
LRU_UNROLL = 4
S5_TILE = 512
S5_LOG_STEPS = 3
S5_UNROLL = True
GLA_UNROLL = 4
GLA_TILE = 1024
PROJ_TILE = 512
OUT_TILE = 512
MOE_BLOCK = 1024
COMBINE_TILE = 256
COMBINE_PARTS = 2
PACKED = D_MODEL // 2
SC_CHUNK = 64
SC_CORES = 2
SC_SUBCORES = 16
SC_WORKERS = SC_CORES * SC_SUBCORES


def _params(*sem):
    return pltpu.CompilerParams(dimension_semantics=sem, vmem_limit_bytes=VMEM_LIMIT)


def _split3(x):
    hi = x.astype(BF16)
    r1 = x - hi.astype(F32)
    mid = r1.astype(BF16)
    lo = (r1 - mid.astype(F32)).astype(BF16)
    return hi, mid, lo


def _dot(a, b):
    return jnp.dot(a, b, preferred_element_type=F32)


def _dot_nt(a, b):
    return lax.dot_general(a, b, (((1,), (1,)), ((), ())), preferred_element_type=F32)


def _dot_tn(a, b):
    return lax.dot_general(a, b, (((0,), (0,)), ((), ())), preferred_element_type=F32)


def _exact_left01(mask01_bf16, x):
    hi, mid, lo = _split3(x)
    return _dot(mask01_bf16, hi) + _dot(mask01_bf16, mid) + _dot(mask01_bf16, lo)


def _exact_right01(x, mask01_bf16):
    hi, mid, lo = _split3(x)
    return _dot(hi, mask01_bf16) + _dot(mid, mask01_bf16) + _dot(lo, mask01_bf16)


def _log_sigmoid(x):
    return jnp.minimum(x, 0.0) - jnp.log(1.0 + jnp.exp(-jnp.abs(x)))


def _sigmoid(x):
    return 1.0 / (1.0 + jnp.exp(-x))


def _gelu_tanh(x):
    c = math.sqrt(2.0 / math.pi)
    return 0.5 * x * (1.0 + jnp.tanh(c * (x + 0.044715 * (x * x * x))))


def _layer_norm(z, w, b):
    mu = jnp.mean(z, axis=-1, keepdims=True)
    zc = z - mu
    return zc * lax.rsqrt(jnp.mean(zc * zc, axis=-1, keepdims=True) + EPS) * w + b


def _proj_kernel(x_ref, w_ref, wg_ref, o_ref, og_ref):
    x = x_ref[...].astype(BF16)
    o_ref[...] = _dot(x, w_ref[...])
    og_ref[...] = _dot(x, wg_ref[...])


def _proj(x, w_main, w_small):
    t, d = x.shape
    n = w_main.shape[1]
    tm = PROJ_TILE
    return pl.pallas_call(
        _proj_kernel,
        grid=(t // tm,),
        in_specs=[pl.BlockSpec((tm, d), lambda i: (i, 0)),
                  pl.BlockSpec((d, n), lambda i: (0, 0)),
                  pl.BlockSpec((d, LANES), lambda i: (0, 0))],
        out_specs=[pl.BlockSpec((tm, n), lambda i: (i, 0)),
                   pl.BlockSpec((tm, LANES), lambda i: (i, 0))],
        out_shape=[jax.ShapeDtypeStruct((t, n), F32), jax.ShapeDtypeStruct((t, LANES), F32)],
        compiler_params=_params("parallel"),
        name="in_proj",
    )(x, w_main, w_small)


def _mlstm_kernel(q_ref, k_ref, v_ref, o_ref, gc_ref, gr_ref, bc_ref, br_ref, nw_ref,
                  y_ref, c_ref, m_ref, *, chunk, n_chunks):
    L = chunk

    @pl.when(pl.program_id(1) == 0)
    def _():
        c_ref[...] = jnp.zeros_like(c_ref)
        m_ref[...] = jnp.zeros_like(m_ref)

    ri = lax.broadcasted_iota(jnp.int32, (L, L), 0)
    ci = lax.broadcasted_iota(jnp.int32, (L, L), 1)
    causal = ci <= ri
    tril = causal.astype(BF16)
    triu = (ri <= ci).astype(BF16)
    ones_v = jnp.ones((L, HEAD_DIM), BF16)
    scale = HEAD_DIM ** -0.5

    def body(c, carry):
        r0 = pl.multiple_of(c * L, L)
        g_col = gc_ref[pl.ds(r0, L), :] + bc_ref[...]
        g_row = gr_ref[c] + br_ref[...]
        b_col_all = _exact_left01(tril, _log_sigmoid(g_col))
        b_row_all = _exact_right01(_log_sigmoid(g_row), triu)
        for h in range(HEADS):
            lo = h * HEAD_DIM
            q = q_ref[pl.ds(r0, L), lo:lo + HEAD_DIM].astype(BF16)
            k = k_ref[pl.ds(r0, L), lo:lo + HEAD_DIM] * scale
            v = v_ref[pl.ds(r0, L), lo:lo + HEAD_DIM].astype(BF16)
            v_aug = jnp.concatenate([v, ones_v], axis=1)
            i_rep = jnp.broadcast_to(g_col[:, h:h + 1], (L, LANES))
            b_rep = jnp.broadcast_to(b_col_all[:, HEADS + h:HEADS + h + 1], (L, LANES))
            i_row = g_row[h:h + 1, :]
            b_row = b_row_all[HEADS + h:HEADS + h + 1, :]
            b_last = b_rep[L - 1:L, :]
            m_prev = m_ref[h:h + 1, :]
            c_prev = c_ref[h]

            d_mat = jnp.where(causal, b_rep - b_row + i_row, NEG_INF)
            m_inter = b_rep + m_prev
            m_i = jnp.maximum(m_inter, jnp.max(d_mat, axis=1, keepdims=True))
            s = _dot_nt(q, k.astype(BF16)) * jnp.exp(d_mat - m_i)
            w_inter = jnp.exp(m_inter - m_i)
            intra = _dot(s.astype(BF16), v_aug)
            inter = _dot(q, c_prev.astype(BF16))
            num = intra[:, :HEAD_DIM] + w_inter * inter[:, :HEAD_DIM]
            den = intra[:, HEAD_DIM:] + w_inter * inter[:, HEAD_DIM:]
            hh = num / jnp.maximum(jnp.abs(den), jnp.exp(-m_i))

            w_loc = b_last - b_rep + i_rep
            m_loc = jnp.max(w_loc, axis=0, keepdims=True)
            kp = (k * jnp.exp(w_loc - m_loc)).astype(BF16)
            c_loc = _dot_tn(kp, v_aug)
            m_new = jnp.maximum(b_last + m_prev, m_loc)
            keep = jnp.exp(b_last + m_prev - m_new)
            add = jnp.exp(m_loc - m_new)
            c_ref[h] = (jnp.concatenate([keep, keep], axis=1) * c_prev
                        + jnp.concatenate([add, add], axis=1) * c_loc)
            m_ref[h:h + 1, :] = m_new

            hc = hh - jnp.mean(hh, axis=-1, keepdims=True)
            yn = hc * lax.rsqrt(jnp.mean(hc * hc, axis=-1, keepdims=True) + EPS)
            og = o_ref[pl.ds(r0, L), lo:lo + HEAD_DIM]
            y_ref[pl.ds(r0, L), lo:lo + HEAD_DIM] = yn * nw_ref[:, lo:lo + HEAD_DIM] * _sigmoid(og)
        return carry

    lax.fori_loop(0, n_chunks, body, 0)


def _mlstm(proj, gates, gate_b, norm_w, batch, seq):
    t = batch * seq
    L = MLSTM_CHUNK
    assert L == LANES, "the kernel keeps per-row gate terms replicated over one vreg of lanes"
    ts = MLSTM_TILE
    nj = seq // ts
    nc = ts // L
    g_row = gates[:, :2 * HEADS].reshape(t // L, L, 2 * HEADS).transpose(0, 2, 1)
    b_col = jnp.zeros((1, LANES), F32).at[0, :2 * HEADS].set(gate_b)
    b_row = gate_b.reshape(2 * HEADS, 1)
    blk = lambda col: pl.BlockSpec((ts, MIX_HALF), lambda b, j, col=col: (b * nj + j, col))
    kern = functools.partial(_mlstm_kernel, chunk=L, n_chunks=nc)
    return pl.pallas_call(
        kern,
        grid=(batch, nj),
        in_specs=[blk(0), blk(1), blk(2), blk(3),
                  pl.BlockSpec((ts, LANES), lambda b, j: (b * nj + j, 0)),
                  pl.BlockSpec((nc, 2 * HEADS, L), lambda b, j: (b * nj + j, 0, 0)),
                  pl.BlockSpec((1, LANES), lambda b, j: (0, 0)),
                  pl.BlockSpec((2 * HEADS, 1), lambda b, j: (0, 0)),
                  pl.BlockSpec((1, MIX_HALF), lambda b, j: (0, 0))],
        out_specs=pl.BlockSpec((ts, MIX_HALF), lambda b, j: (b * nj + j, 0)),
        out_shape=jax.ShapeDtypeStruct((t, MIX_HALF), F32),
        scratch_shapes=[pltpu.VMEM((HEADS, HEAD_DIM, 2 * HEAD_DIM), F32),
                        pltpu.VMEM((8, LANES), F32)],
        compiler_params=_params("arbitrary", "arbitrary"),
        name="mlstm",
    )(proj, proj, proj, proj, gates, g_row, b_col, b_row, norm_w.reshape(1, MIX_HALF))


def _rglru_kernel(xb_ref, gb_ref, cw_ref, cb_ref, wa_ref, ba_ref, wx_ref, bx_ref, lam_ref,
                  y_ref, xext_ref, h_ref, a_ref, u_ref, *, tile):
    @pl.when(pl.program_id(1) == 0)
    def _():
        xext_ref[0:8, :] = jnp.zeros((8, MIX_HALF), F32)
        h_ref[...] = jnp.zeros_like(h_ref)

    x = xb_ref[...]
    xext_ref[8:8 + tile, :] = x
    xc = cb_ref[...] + cw_ref[LRU_CONV - 1:LRU_CONV, :] * x
    for tap in range(LRU_CONV - 1):
        back = LRU_CONV - 1 - tap
        xc = xc + cw_ref[tap:tap + 1, :] * xext_ref[8 - back:8 - back + tile, :]
    xext_ref[0:8, :] = x[tile - 8:tile, :]

    xc16 = xc.astype(BF16)
    r_parts, i_parts = [], []
    for h in range(HEADS):
        lo = h * HEAD_DIM
        xh = xc16[:, lo:lo + HEAD_DIM]
        r_parts.append(_dot(xh, wa_ref[h]))
        i_parts.append(_dot(xh, wx_ref[h]))
    r = _sigmoid(jnp.concatenate(r_parts, axis=1) + ba_ref[...])
    ig = _sigmoid(jnp.concatenate(i_parts, axis=1) + bx_ref[...])
    lam = lam_ref[...]
    softplus_neg = jnp.maximum(-lam, 0.0) + jnp.log(1.0 + jnp.exp(-jnp.abs(lam)))
    log_a = -LRU_C * r * softplus_neg
    a = jnp.exp(log_a)
    th = jnp.tanh(log_a)
    u = jnp.sqrt(-2.0 * th / (1.0 - th)) * ig * xc

    a_ref[...] = a
    u_ref[...] = u
    rows = lax.broadcasted_iota(jnp.int32, (SUBLANES, MIX_HALF), 0)

    def group(i, h_prev):
        r0 = pl.multiple_of(i * SUBLANES, SUBLANES)
        ag = a_ref[pl.ds(r0, SUBLANES), :]
        ug = u_ref[pl.ds(r0, SUBLANES), :]
        for k in range(LRU_LOG_STEPS):
            keep = rows >= (1 << k)
            ug = ag * jnp.where(keep, pltpu.roll(ug, 1 << k, 0), 0.0) + ug
            ag = ag * jnp.where(keep, pltpu.roll(ag, 1 << k, 0), 1.0)
        hg = ug + ag * h_prev
        u_ref[pl.ds(r0, SUBLANES), :] = hg
        return hg[SUBLANES - 1:SUBLANES, :]

    h_last = lax.fori_loop(0, tile // SUBLANES, group, h_ref[0:1, :], unroll=LRU_UNROLL)
    h_ref[...] = jnp.broadcast_to(h_last, h_ref.shape)
    y_ref[...] = u_ref[...] * _gelu_tanh(gb_ref[...])


def _rglru(proj, conv_w, conv_b, wa, ba, wx, bx, lam, batch, seq):
    t = batch * seq
    ts = LRU_TILE
    nj = seq // ts
    row = lambda a: a.reshape(1, MIX_HALF)
    const2 = lambda shape: pl.BlockSpec(shape, lambda b, j: (0, 0))
    const3 = lambda shape: pl.BlockSpec(shape, lambda b, j: (0, 0, 0))
    blk = lambda col: pl.BlockSpec((ts, MIX_HALF), lambda b, j, col=col: (b * nj + j, col))
    return pl.pallas_call(
        functools.partial(_rglru_kernel, tile=ts),
        grid=(batch, nj),
        in_specs=[blk(4), blk(5), const2((LRU_CONV, MIX_HALF)), const2((1, MIX_HALF)),
                  const3((HEADS, HEAD_DIM, HEAD_DIM)), const2((1, MIX_HALF)),
                  const3((HEADS, HEAD_DIM, HEAD_DIM)), const2((1, MIX_HALF)), const2((1, MIX_HALF))],
        out_specs=pl.BlockSpec((ts, MIX_HALF), lambda b, j: (b * nj + j, 0)),
        out_shape=jax.ShapeDtypeStruct((t, MIX_HALF), F32),
        scratch_shapes=[pltpu.VMEM((ts + 8, MIX_HALF), F32), pltpu.VMEM((8, MIX_HALF), F32),
                        pltpu.VMEM((ts, MIX_HALF), F32), pltpu.VMEM((ts, MIX_HALF), F32)],
        compiler_params=_params("arbitrary", "arbitrary"),
        name="rglru",
    )(proj, proj, conv_w, row(conv_b), wa.astype(BF16), row(ba), wx.astype(BF16), row(bx), row(lam))


def _s5_kernel(u_ref, bre_ref, bim_ref, cre_ref, cim_ref, mre_ref, mim_ref, pre_ref, pim_ref, d_ref, gw_ref,
               gb_ref, y_ref, xr_ref, xi_ref, cr_ref, ci_ref, *, tile):
    @pl.when(pl.program_id(1) == 0)
    def _():
        cr_ref[...] = jnp.zeros_like(cr_ref)
        ci_ref[...] = jnp.zeros_like(ci_ref)

    u = u_ref[...]
    u16 = u.astype(BF16)
    blk_c = MIX_HALF // S5_BLOCKS
    blk_s = S5_LANES // S5_BLOCKS
    parts = []
    for j in range(S5_BLOCKS):
        lanes = slice(j * blk_s, (j + 1) * blk_s)
        uj = u16[:, j * blk_c:(j + 1) * blk_c]
        xr_ref[:, lanes] = _dot(uj, bre_ref[j])
        xi_ref[:, lanes] = _dot(uj, bim_ref[j])

        def group(i, carry, lanes=lanes):
            cr, ci = carry
            r0 = pl.multiple_of(i * SUBLANES, SUBLANES)
            xr = xr_ref[pl.ds(r0, SUBLANES), lanes]
            xi = xi_ref[pl.ds(r0, SUBLANES), lanes]
            for k in range(S5_LOG_STEPS):
                sr = pltpu.roll(xr, 1 << k, 0)
                si = pltpu.roll(xi, 1 << k, 0)
                mr = mre_ref[k, :, lanes]
                mi = mim_ref[k, :, lanes]
                xr, xi = xr + mr * sr - mi * si, xi + mr * si + mi * sr
            pr = pre_ref[:, lanes]
            pi = pim_ref[:, lanes]
            xr, xi = xr + pr * cr - pi * ci, xi + pr * ci + pi * cr
            xr_ref[pl.ds(r0, SUBLANES), lanes] = xr
            xi_ref[pl.ds(r0, SUBLANES), lanes] = xi
            return xr[SUBLANES - 1:SUBLANES, :], xi[SUBLANES - 1:SUBLANES, :]

        cr, ci = lax.fori_loop(0, tile // SUBLANES, group, (cr_ref[0:1, lanes], ci_ref[0:1, lanes]),
                               unroll=S5_UNROLL)
        cr_ref[0:1, lanes] = cr
        ci_ref[0:1, lanes] = ci
        parts.append(_dot(xr_ref[:, lanes].astype(BF16), cre_ref[j])
                     - _dot(xi_ref[:, lanes].astype(BF16), cim_ref[j]))
    y = jnp.concatenate(parts, axis=1) + d_ref[...] * u
    g = _gelu_tanh(y)
    y_ref[...] = g * _sigmoid(_dot(g.astype(BF16), gw_ref[...]) + gb_ref[...])


def _s5_tables(lam_re, lam_im, b_re, b_im, c_re, c_im, log_dt):
    lr, li = lam_re.astype(F32), lam_im.astype(F32)
    dt = jnp.exp(log_dt.astype(F32))[:, None]
    mag = jnp.exp(lr * dt)
    abar_re = mag * jnp.cos(li * dt)
    abar_im = mag * jnp.sin(li * dt)
    den = lr * lr + li * li
    nr = abar_re - 1.0
    coef_re = (nr * lr + abar_im * li) / den
    coef_im = (abar_im * lr - nr * li) / den
    bbar_re = coef_re[..., None] * b_re - coef_im[..., None] * b_im
    bbar_im = coef_re[..., None] * b_im + coef_im[..., None] * b_re
    gpb = S5_GROUPS // S5_BLOCKS
    eye = jnp.eye(gpb, dtype=F32)

    def in_map(bb):
        bb = bb.reshape(S5_BLOCKS, gpb, S5_STATE, S5_GROUP)
        return jnp.einsum("jgph,gk->jghkp", bb, eye).reshape(S5_BLOCKS, gpb * S5_GROUP, gpb * S5_STATE)

    def out_map(cc):
        cc = cc.reshape(S5_BLOCKS, gpb, S5_GROUP, S5_STATE)
        return jnp.einsum("jghp,gk->jgpkh", cc, eye).reshape(S5_BLOCKS, gpb * S5_STATE, gpb * S5_GROUP)

    def power(n):
        n = jnp.asarray(n, F32)[..., None, None]
        pmag = jnp.exp(n * (lr * dt))
        shape = n.shape[:-2] + (S5_LANES,)
        return (pmag * jnp.cos(n * (li * dt))).reshape(shape), (pmag * jnp.sin(n * (li * dt))).reshape(shape)

    row = jnp.arange(SUBLANES)
    step = 2 ** jnp.arange(S5_LOG_STEPS)
    s_re, s_im = power(step)
    keep = (row[None, :] >= step[:, None])[..., None]
    m_re = jnp.where(keep, s_re[:, None, :], 0.0)
    m_im = jnp.where(keep, s_im[:, None, :], 0.0)
    p_re, p_im = power(row + 1)
    return (in_map(bbar_re).astype(BF16), in_map(bbar_im).astype(BF16),
            out_map(c_re.astype(F32)).astype(BF16), out_map(c_im.astype(F32)).astype(BF16),
            m_re, m_im, p_re, p_im)


def _s5(proj, tables, d_skip, glu_w, glu_b, batch, seq):
    t = batch * seq
    ts = S5_TILE
    nj = seq // ts
    bre, bim, cre, cim, m_re, m_im, p_re, p_im = tables
    blk_c = MIX_HALF // S5_BLOCKS
    blk_s = S5_LANES // S5_BLOCKS
    const2 = lambda shape: pl.BlockSpec(shape, lambda b, j: (0, 0))
    const3 = lambda shape: pl.BlockSpec(shape, lambda b, j: (0, 0, 0))
    return pl.pallas_call(
        functools.partial(_s5_kernel, tile=ts),
        grid=(batch, nj),
        in_specs=[pl.BlockSpec((ts, MIX_HALF), lambda b, j: (b * nj + j, 0)),
                  const3((S5_BLOCKS, blk_c, blk_s)), const3((S5_BLOCKS, blk_c, blk_s)),
                  const3((S5_BLOCKS, blk_s, blk_c)), const3((S5_BLOCKS, blk_s, blk_c)),
                  const3(m_re.shape), const3(m_im.shape), const2(p_re.shape), const2(p_im.shape),
                  const2((1, MIX_HALF)), const2((MIX_HALF, MIX_HALF)), const2((1, MIX_HALF))],
        out_specs=pl.BlockSpec((ts, MIX_HALF), lambda b, j: (b * nj + j, 0)),
        out_shape=jax.ShapeDtypeStruct((t, MIX_HALF), F32),
        scratch_shapes=[pltpu.VMEM((ts, S5_LANES), F32), pltpu.VMEM((ts, S5_LANES), F32),
                        pltpu.VMEM((8, S5_LANES), F32), pltpu.VMEM((8, S5_LANES), F32)],
        compiler_params=_params("arbitrary", "arbitrary"),
        name="s5",
    )(proj, bre, bim, cre, cim, m_re, m_im, p_re, p_im, d_skip.reshape(1, MIX_HALF), glu_w.astype(BF16),
      glu_b.reshape(1, MIX_HALF))


def _gla_kernel(q_ref, k_ref, v_ref, r_ref, gl_ref, gw_ref, gb_ref, nw_ref, y_ref,
                st_ref, qd_ref, ki_ref, ke_ref, v16_ref, dec_ref, o_ref, *, tile, chunk):
    L = chunk
    nc = tile // L

    @pl.when(pl.program_id(1) == 0)
    def _():
        st_ref[...] = jnp.zeros_like(st_ref)

    z = _dot(gl_ref[...].astype(BF16), gw_ref[...]) + gb_ref[...]
    bcum = _log_sigmoid(z) * (1.0 / GLA_GATE_TEMP)
    row_in_chunk = lax.broadcasted_iota(jnp.int32, bcum.shape, 0) & (L - 1)
    s = 1
    while s < L:
        bcum = bcum + jnp.where(row_in_chunk >= s, pltpu.roll(bcum, s, 0), 0.0)
        s *= 2
    b3 = bcum.reshape(nc, L, MIX_HALF)
    b_last = b3[:, L - 1:L, :]
    k = k_ref[...]
    qd_ref[...] = (q_ref[...] * (GLA_DK ** -0.5) * jnp.exp(bcum)).astype(BF16)
    ki_ref[...] = (k * jnp.exp(-bcum)).astype(BF16)
    ke_ref[...] = (k.reshape(nc, L, MIX_HALF) * jnp.exp(b_last - b3)).reshape(tile, MIX_HALF).astype(BF16)
    v16_ref[...] = v_ref[...].astype(BF16)
    dec_ref[...] = jnp.exp(b_last)

    ri = lax.broadcasted_iota(jnp.int32, (L, L), 0)
    ci = lax.broadcasted_iota(jnp.int32, (L, L), 1)
    causal = ci <= ri

    def body(c, carry):
        r0 = pl.multiple_of(c * L, L)
        dec = dec_ref[c]
        for h in range(HEADS):
            lo = h * HEAD_DIM
            q_dec = qd_ref[pl.ds(r0, L), lo:lo + HEAD_DIM]
            v = v16_ref[pl.ds(r0, L), lo:lo + HEAD_DIM]
            st = st_ref[h]
            att = jnp.where(causal, _dot_nt(q_dec, ki_ref[pl.ds(r0, L), lo:lo + HEAD_DIM]), 0.0)
            o_ref[pl.ds(r0, L), lo:lo + HEAD_DIM] = (_dot(att.astype(BF16), v)
                                                     + _dot_nt(q_dec, st.astype(BF16)))
            st_ref[h] = dec[:, lo:lo + HEAD_DIM] * st + _dot_tn(v, ke_ref[pl.ds(r0, L), lo:lo + HEAD_DIM])
        return carry

    lax.fori_loop(0, nc, body, 0, unroll=GLA_UNROLL)

    rg = r_ref[...]
    gate = nw_ref[...] * (rg * _sigmoid(rg))
    for h in range(HEADS):
        lo = h * HEAD_DIM
        o = o_ref[:, lo:lo + HEAD_DIM]
        yn = o * lax.rsqrt(jnp.mean(o * o, axis=-1, keepdims=True) + EPS)
        y_ref[:, lo:lo + HEAD_DIM] = yn * gate[:, lo:lo + HEAD_DIM]


def _gla(proj, glow, gate_w, gate_b, norm_w, batch, seq):
    t = batch * seq
    ts = GLA_TILE
    nj = seq // ts
    blk = lambda col: pl.BlockSpec((ts, MIX_HALF), lambda b, j, col=col: (b * nj + j, col))
    const2 = lambda shape: pl.BlockSpec(shape, lambda b, j: (0, 0))
    return pl.pallas_call(
        functools.partial(_gla_kernel, tile=ts, chunk=GLA_CHUNK),
        grid=(batch, nj),
        in_specs=[blk(1), blk(2), blk(3), blk(4),
                  pl.BlockSpec((ts, LANES), lambda b, j: (b * nj + j, 0)),
                  const2((LANES, MIX_HALF)), const2((1, MIX_HALF)), const2((1, MIX_HALF))],
        out_specs=pl.BlockSpec((ts, MIX_HALF), lambda b, j: (b * nj + j, 0)),
        out_shape=jax.ShapeDtypeStruct((t, MIX_HALF), F32),
        scratch_shapes=[pltpu.VMEM((HEADS, HEAD_DIM, HEAD_DIM), F32),
                        pltpu.VMEM((ts, MIX_HALF), BF16), pltpu.VMEM((ts, MIX_HALF), BF16),
                        pltpu.VMEM((ts, MIX_HALF), BF16), pltpu.VMEM((ts, MIX_HALF), BF16),
                        pltpu.VMEM((ts // GLA_CHUNK, 1, MIX_HALF), F32),
                        pltpu.VMEM((ts, MIX_HALF), F32)],
        compiler_params=_params("arbitrary", "arbitrary"),
        name="gla",
    )(proj, proj, proj, proj, glow, gate_w, gate_b, norm_w.reshape(1, MIX_HALF))


def _pad_heads(w, axis):
    shape = list(w.shape)
    shape[axis:axis + 1] = [HEADS, GLA_DK]
    w = w.reshape(shape)
    pad = [(0, 0)] * w.ndim
    pad[axis + 1] = (0, HEAD_DIM - GLA_DK)
    w = jnp.pad(w, pad)
    shape[axis:axis + 2] = [HEADS * HEAD_DIM]
    return w.reshape(shape)


def _pack_bf16_pairs(z):
    hi = lax.bitcast_convert_type(z[:, :PACKED].astype(BF16).astype(F32), jnp.uint32)
    lo = lax.bitcast_convert_type(z[:, PACKED:].astype(BF16).astype(F32), jnp.uint32)
    word = (hi & jnp.uint32(0xFFFF0000)) | lax.shift_right_logical(lo, jnp.uint32(16))
    return lax.bitcast_convert_type(word, jnp.int32)


def _unpack_bf16_pairs(p):
    word = lax.bitcast_convert_type(p, jnp.uint32)
    hi = lax.bitcast_convert_type(word & jnp.uint32(0xFFFF0000), F32)
    lo = lax.bitcast_convert_type(lax.shift_left(word, jnp.uint32(16)), F32)
    return hi, lo


def _out_kernel(ya_ref, yb_ref, h_ref, w_ref, lw_ref, lb_ref, rw_ref, rb_ref,
                o_ref, opk_ref, idx_ref, rank_ref, wk_ref, cnt_ref, base_ref, *, tile):
    mixed = jnp.concatenate([ya_ref[...], yb_ref[...]], axis=1).astype(BF16)
    z = ALPHA * h_ref[...] + _dot(mixed, w_ref[...])
    out = _layer_norm(z, lw_ref[...], lb_ref[...])
    o_ref[...] = out
    opk_ref[...] = _pack_bf16_pairs(out)
    _route_tile(out, rw_ref, rb_ref, idx_ref, rank_ref, wk_ref, cnt_ref, base_ref, tile)


def _out_proj_ln_route(ya, yb, h, w_out, ln_w, ln_b, router_w, router_bias):
    t = h.shape[0]
    tm = OUT_TILE
    const = lambda shape: pl.BlockSpec(shape, lambda i: (0, 0))
    per_tok = lambda dt: jax.ShapeDtypeStruct((TOP_K, t), dt)
    tok_blk = pl.BlockSpec((TOP_K, tm), lambda i: (0, i))
    return pl.pallas_call(
        functools.partial(_out_kernel, tile=tm),
        grid=(t // tm,),
        in_specs=[pl.BlockSpec((tm, MIX_HALF), lambda i: (i, 0)),
                  pl.BlockSpec((tm, MIX_HALF), lambda i: (i, 0)),
                  pl.BlockSpec((tm, D_MODEL), lambda i: (i, 0)),
                  const((D_MODEL, D_MODEL)), const((1, D_MODEL)), const((1, D_MODEL)),
                  const((N_EXPERTS, D_MODEL)), const((N_EXPERTS, 1))],
        out_specs=[pl.BlockSpec((tm, D_MODEL), lambda i: (i, 0)),
                   pl.BlockSpec((tm, PACKED), lambda i: (i, 0)),
                   tok_blk, tok_blk, tok_blk, const((N_EXPERTS, LANES))],
        out_shape=[jax.ShapeDtypeStruct((t, D_MODEL), F32), jax.ShapeDtypeStruct((t, PACKED), jnp.int32),
                   per_tok(jnp.int32), per_tok(jnp.int32), per_tok(F32),
                   jax.ShapeDtypeStruct((N_EXPERTS, LANES), F32)],
        scratch_shapes=[pltpu.VMEM((N_EXPERTS, LANES), F32)],
        compiler_params=_params("arbitrary"),
        name="out_proj_ln_route",
    )(ya, yb, h, w_out.astype(BF16), ln_w.reshape(1, D_MODEL), ln_b.reshape(1, D_MODEL),
      router_w.T, router_bias.reshape(N_EXPERTS, 1))


def _first_index(hit, idx, big):
    return jnp.min(jnp.where(hit, idx, big), axis=0, keepdims=True)


def _route_tile(h, w_ref, b_ref, idx_ref, rank_ref, wk_ref, cnt_ref, base_ref, tile):
    @pl.when(pl.program_id(0) == 0)
    def _():
        base_ref[...] = jnp.zeros_like(base_ref)

    h_hi, h_mid, _ = _split3(h)
    w_hi, w_mid, _ = _split3(w_ref[...])
    logits = _dot_nt(w_hi, h_hi) + _dot_nt(w_hi, h_mid) + _dot_nt(w_mid, h_hi)
    scores = _sigmoid(logits)
    biased = scores + b_ref[...]

    sub = lax.broadcasted_iota(jnp.int32, (GROUP_SIZE, tile), 0)
    grp_rows = []
    for g in range(N_GROUPS):
        xg = biased[g * GROUP_SIZE:(g + 1) * GROUP_SIZE, :]
        m1 = jnp.max(xg, axis=0, keepdims=True)
        i1 = _first_index(xg == m1, sub, GROUP_SIZE)
        m2 = jnp.max(jnp.where(sub == i1, NEG_INF, xg), axis=0, keepdims=True)
        grp_rows.append(m1 + m2)
    gs = jnp.concatenate(grp_rows, axis=0)
    gsel = jnp.zeros((N_GROUPS, tile), F32)
    for _ in range(TOPK_GROUPS):
        mx = jnp.max(gs, axis=0, keepdims=True)
        hit = sub == _first_index(gs == mx, sub, N_GROUPS)
        gsel = jnp.where(hit, 1.0, gsel)
        gs = jnp.where(hit, NEG_INF, gs)
    emask = jnp.concatenate(
        [jnp.broadcast_to(gsel[g:g + 1, :], (GROUP_SIZE, tile)) for g in range(N_GROUPS)], axis=0)

    eidx = lax.broadcasted_iota(jnp.int32, (N_EXPERTS, tile), 0)
    cand = jnp.where(emask > 0.5, biased, NEG_INF)
    sel = jnp.zeros((N_EXPERTS, tile), F32)
    hits, picks = [], []
    for _ in range(TOP_K):
        mx = jnp.max(cand, axis=0, keepdims=True)
        first = _first_index(cand == mx, eidx, N_EXPERTS)
        hit = eidx == first
        hits.append(hit)
        picks.append(first)
        sel = jnp.where(hit, 1.0, sel)
        cand = jnp.where(hit, NEG_INF, cand)
    picked = jnp.where(sel > 0.5, scores, 0.0)
    wts = picked / jnp.sum(picked, axis=0, keepdims=True) * ROUTED_SCALE

    ri = lax.broadcasted_iota(jnp.int32, (tile, tile), 0)
    ci = lax.broadcasted_iota(jnp.int32, (tile, tile), 1)
    before = (ri < ci).astype(BF16)
    prior = _dot(sel.astype(BF16), before) + base_ref[:, 0:1]
    ranks = [jnp.sum(jnp.where(hit, prior, 0.0), axis=0, keepdims=True) for hit in hits]
    wsel = [jnp.sum(jnp.where(hit, wts, 0.0), axis=0, keepdims=True) for hit in hits]
    idx_ref[...] = jnp.concatenate(picks, axis=0)
    rank_ref[...] = jnp.concatenate(ranks, axis=0).astype(jnp.int32)
    wk_ref[...] = jnp.concatenate(wsel, axis=0)
    total = base_ref[...] + jnp.sum(sel, axis=1, keepdims=True)
    base_ref[...] = total
    cnt_ref[...] = total


def _silu(x):
    return x * _sigmoid(x)


def _sc_mesh():
    return plsc.VectorSubcoreMesh(core_axis_name="c", subcore_axis_name="s")


def _sc_worker_id():
    return lax.axis_index("s") * SC_CORES + lax.axis_index("c")


def _dispatch_rows(xpk, pos_chunks, n_rows):
    t = xpk.shape[0]
    n_ch = t // SC_WORKERS // SC_CHUNK

    @functools.partial(
        pl.kernel, mesh=_sc_mesh(),
        out_type=jax.ShapeDtypeStruct((n_rows, PACKED), jnp.int32),
        scratch_types=[pltpu.VMEM((TOP_K, SC_CHUNK), jnp.int32),
                       pltpu.VMEM((SC_CHUNK, PACKED), jnp.int32),
                       pltpu.SemaphoreType.DMA],
        name="moe_dispatch",
    )
    def scatter(x_hbm, pos_hbm, out_hbm, idx_v, rows_v, sem):
        wid = _sc_worker_id()

        @pl.loop(0, n_ch)
        def _(c):
            chunk = wid * n_ch + c
            off = pl.multiple_of(chunk * SC_CHUNK, SC_CHUNK)
            pltpu.sync_copy(pos_hbm.at[chunk], idx_v)
            pltpu.sync_copy(x_hbm.at[pl.ds(off, SC_CHUNK)], rows_v)
            copies = [pltpu.async_copy(rows_v, out_hbm.at[idx_v.at[k]], sem) for k in range(TOP_K)]
            for cp in copies:
                cp.wait()

    return scatter(xpk, pos_chunks)


def _gather_rows(table, idx):
    n = idx.shape[0]
    per_w = n // SC_WORKERS
    n_ch = per_w // SC_CHUNK
    assert n_ch % 2 == 0 and n_ch >= 2

    @functools.partial(
        pl.kernel, mesh=_sc_mesh(),
        out_type=jax.ShapeDtypeStruct((n, PACKED), jnp.int32),
        scratch_types=[pltpu.VMEM((n_ch, SC_CHUNK), jnp.int32),
                       pltpu.VMEM((SC_CHUNK, PACKED), jnp.int32), pltpu.VMEM((SC_CHUNK, PACKED), jnp.int32),
                       pltpu.SemaphoreType.DMA, pltpu.SemaphoreType.DMA,
                       pltpu.SemaphoreType.DMA, pltpu.SemaphoreType.DMA],
        name="moe_gather",
    )
    def gather(table_hbm, idx_hbm, out_hbm, idx_v, rows0, rows1, g0, g1, w0, w1):
        wid = _sc_worker_id()
        base = wid * per_w
        rows, g_sem, w_sem = (rows0, rows1), (g0, g1), (w0, w1)
        pltpu.sync_copy(idx_hbm.at[wid], idx_v)

        def fetch(c, b):
            return pltpu.make_async_copy(table_hbm.at[idx_v.at[c]], rows[b], g_sem[b])

        def flush(c, b):
            off = pl.multiple_of(base + c * SC_CHUNK, SC_CHUNK)
            return pltpu.make_async_copy(rows[b], out_hbm.at[pl.ds(off, SC_CHUNK)], w_sem[b])

        fetch(0, 0).start()

        @pl.loop(0, n_ch, step=2)
        def _(c0):
            for b in range(2):
                c = c0 + b
                fetch(c, b).wait()
                flush(c, b).start()

                @pl.when(c + 1 < n_ch)
                def _():
                    @pl.when(c >= 1)
                    def _():
                        flush(c - 1, 1 - b).wait()
                    fetch(c + 1, 1 - b).start()

        flush(n_ch - 2, 0).wait()
        flush(n_ch - 1, 1).wait()

    return gather(table, idx.reshape(SC_WORKERS, n_ch, SC_CHUNK))


def _unpacked_bf16(p):
    hi, lo = _unpack_bf16_pairs(p)
    return jnp.concatenate([hi.astype(BF16), lo.astype(BF16)], axis=1)


def _expert_kernel(be_ref, nu_ref, x_ref, wg_ref, wu_ref, wd_ref, y_ref, g16_ref, u16_ref, d16_ref):
    i = pl.program_id(0)
    changed = jnp.logical_or(i == 0, be_ref[i] != be_ref[jnp.maximum(i - 1, 0)])

    @pl.when(changed)
    def _():
        g16_ref[...] = wg_ref[0, 0].astype(BF16)
        u16_ref[...] = wu_ref[0, 0].astype(BF16)
        d16_ref[...] = wd_ref[0, 0].astype(BF16)

    @pl.when(i < nu_ref[0])
    def _():
        x = _unpacked_bf16(x_ref[...])
        hh = _silu(_dot(x, g16_ref[...])) * _dot(x, u16_ref[...])
        y_ref[...] = _pack_bf16_pairs(_dot(hh.astype(BF16), d16_ref[...]))


def _experts(block_e, n_used, xs, wg, wu, wd, layer):
    nb = block_e.shape[0]
    bm = MOE_BLOCK
    grid_spec = pltpu.PrefetchScalarGridSpec(
        num_scalar_prefetch=2,
        grid=(nb,),
        in_specs=[pl.BlockSpec((bm, PACKED), lambda i, be, nu: (jnp.minimum(i, nu[0] - 1), 0)),
                  pl.BlockSpec((1, 1, D_MODEL, D_EXPERT), lambda i, be, nu: (layer, be[i], 0, 0)),
                  pl.BlockSpec((1, 1, D_MODEL, D_EXPERT), lambda i, be, nu: (layer, be[i], 0, 0)),
                  pl.BlockSpec((1, 1, D_EXPERT, D_MODEL), lambda i, be, nu: (layer, be[i], 0, 0))],
        out_specs=pl.BlockSpec((bm, PACKED), lambda i, be, nu: (jnp.minimum(i, nu[0] - 1), 0)),
        scratch_shapes=[pltpu.VMEM((D_MODEL, D_EXPERT), BF16), pltpu.VMEM((D_MODEL, D_EXPERT), BF16),
                        pltpu.VMEM((D_EXPERT, D_MODEL), BF16)],
    )
    return pl.pallas_call(
        _expert_kernel,
        grid_spec=grid_spec,
        out_shape=jax.ShapeDtypeStruct((nb * bm, PACKED), jnp.int32),
        compiler_params=_params("arbitrary"),
        name="moe_experts",
    )(block_e, n_used, xs, wg, wu, wd)


def _combine_kernel(g_ref, wk_ref, h_ref, xpk_ref, sg_ref, su_ref, sd_ref, lw_ref, lb_ref, *rest, with_proj):
    if with_proj:
        (wm_ref, ws_ref), (o_ref, proj_ref, small_ref) = rest[:2], rest[-3:]
        next_proj = (wm_ref, ws_ref, proj_ref, small_ref)
    else:
        o_ref, next_proj = rest[-1], None
    x = _unpacked_bf16(xpk_ref[...])
    hs = _silu(_dot(x, sg_ref[...])) * _dot(x, su_ref[...])
    shared = _dot(hs.astype(BF16), sd_ref[...])
    acc_hi = shared[:, :PACKED]
    acc_lo = shared[:, PACKED:]
    wk = wk_ref[...]
    for k in range(TOP_K):
        y_hi, y_lo = _unpack_bf16_pairs(g_ref[k])
        w = wk[:, k:k + 1]
        acc_hi = acc_hi + w * y_hi
        acc_lo = acc_lo + w * y_lo
    ffn = jnp.concatenate([acc_hi, acc_lo], axis=1)
    out = _layer_norm(ALPHA * h_ref[...] + ffn, lw_ref[...], lb_ref[...])
    o_ref[...] = out
    if next_proj is not None:
        wm_ref, ws_ref, proj_ref, small_ref = next_proj
        out16 = out.astype(BF16)
        proj_ref[...] = _dot(out16, wm_ref[...])
        small_ref[...] = _dot(out16, ws_ref[...])


def _combine_ln(g, wk, h, xpk, sg, su, sd, ln_w, ln_b, next_w, part, prev):
    t = h.shape[0]
    tm = COMBINE_TILE
    n_blk = g.shape[1] // tm
    first = part * n_blk
    const = lambda shape: pl.BlockSpec(shape, lambda i: (0, 0))
    rows = lambda width: pl.BlockSpec((tm, width), lambda i: (i + first, 0))
    in_specs = [pl.BlockSpec((TOP_K, tm, PACKED), lambda i: (0, i, 0)), rows(TOP_K), rows(D_MODEL), rows(PACKED),
                const((D_MODEL, D_EXPERT)), const((D_MODEL, D_EXPERT)), const((D_EXPERT, D_MODEL)),
                const((1, D_MODEL)), const((1, D_MODEL))]
    args = [g, wk, h, xpk, sg.astype(BF16), su.astype(BF16), sd.astype(BF16),
            ln_w.reshape(1, D_MODEL), ln_b.reshape(1, D_MODEL)]
    out_specs = [rows(D_MODEL)]
    out_shape = [jax.ShapeDtypeStruct((t, D_MODEL), F32)]
    if next_w is not None:
        w_main, w_small = next_w
        n = w_main.shape[1]
        in_specs += [const((D_MODEL, n)), const((D_MODEL, LANES))]
        args += [w_main, w_small]
        out_specs += [rows(n), rows(LANES)]
        out_shape += [jax.ShapeDtypeStruct((t, n), F32), jax.ShapeDtypeStruct((t, LANES), F32)]
    aliases = {}
    if prev is not None:
        aliases = {len(args) + k: k for k in range(len(prev))}
        in_specs += [pl.BlockSpec(memory_space=pl.ANY)] * len(prev)
        args += list(prev)
    return pl.pallas_call(
        functools.partial(_combine_kernel, with_proj=next_w is not None),
        grid=(n_blk,),
        in_specs=in_specs,
        out_specs=out_specs,
        out_shape=out_shape,
        input_output_aliases=aliases,
        compiler_params=_params("parallel"),
        name="moe_combine_ln",
    )(*args)


def _moe_ln(h, hpk, idx, rank, wk, counts, wg, wu, wd, layer, sg, su, sd, ln_w, ln_b, next_w):
    t = h.shape[0]
    cnt = counts[:, 0].astype(jnp.int32)
    padded = (cnt + MOE_BLOCK - 1) // MOE_BLOCK * MOE_BLOCK
    pend = jnp.cumsum(padded)
    experts = jnp.arange(N_EXPERTS, dtype=jnp.int32)
    pstart_of_pick = jnp.sum(jnp.where(idx[:, :, None] == experts, pend - padded, 0), axis=-1)
    pos = pstart_of_pick + rank
    nb = -(-(t * TOP_K + N_EXPERTS * (MOE_BLOCK - 1)) // MOE_BLOCK)
    starts = jnp.arange(nb, dtype=jnp.int32) * MOE_BLOCK
    block_e = jnp.minimum(jnp.sum((pend[None, :] <= starts[:, None]).astype(jnp.int32), axis=1), N_EXPERTS - 1)
    n_used = (pend[-1] // MOE_BLOCK).astype(jnp.int32).reshape(1)
    pos_chunks = pos.reshape(TOP_K, t // SC_CHUNK, SC_CHUNK).transpose(1, 0, 2)
    xs = _dispatch_rows(hpk, pos_chunks, nb * MOE_BLOCK)
    ys = _experts(block_e, n_used, xs, wg, wu, wd, layer)
    part = t // COMBINE_PARTS
    gathered = [_gather_rows(ys, pos[:, p * part:(p + 1) * part].reshape(-1)).reshape(TOP_K, part, PACKED)
                for p in range(COMBINE_PARTS)]
    wk_t = wk.T
    outs = None
    for p in range(COMBINE_PARTS):
        outs = _combine_ln(gathered[p], wk_t, h, hpk, sg, su, sd, ln_w, ln_b, next_w, p, outs)
    return outs


def _pad_cols(w, width=LANES):
    return jnp.pad(w, ((0, 0), (0, width - w.shape[1])))


def _even_proj_weights(w_in):
    a4 = 4 * MIX_HALF
    ng = 2 * HEADS
    w_main = jnp.concatenate([w_in[:, :a4], w_in[:, a4 + ng:]], axis=1).astype(BF16)
    w_gate = _pad_cols(w_in[:, a4:a4 + ng]).astype(BF16)
    return w_main, w_gate


def _even_mixer(proj, gates, batch, seq, gate_b, norm_w, conv_w, conv_b, wa, ba, wx, bx, lam):
    ya = _mlstm(proj, gates, gate_b, norm_w, batch, seq)
    yb = _rglru(proj, conv_w, conv_b, wa, ba, wx, bx, lam, batch, seq)
    return ya, yb


def _odd_proj_weights(w_in):
    c0 = MIX_HALF
    c1 = c0 + HEADS * GLA_DK
    c2 = c1 + HEADS * GLA_DK
    c3 = c2 + MIX_HALF
    c4 = c3 + MIX_HALF
    w_main = jnp.concatenate([w_in[:, :c0], _pad_heads(w_in[:, c0:c1], 1), _pad_heads(w_in[:, c1:c2], 1),
                              w_in[:, c2:c4]], axis=1).astype(BF16)
    w_low = _pad_cols(w_in[:, c4:]).astype(BF16)
    return w_main, w_low


def _odd_mixer(proj, glow, batch, seq, lam_re, lam_im, b_re, b_im, c_re, c_im, d_skip, log_dt,
               glu_w, glu_b, gate_w, gate_b, norm_w):
    tables = _s5_tables(lam_re, lam_im, b_re, b_im, c_re, c_im, log_dt)
    yc = _s5(proj, tables, d_skip, glu_w, glu_b, batch, seq)
    gw = jnp.pad(_pad_heads(gate_w, 1), ((0, LANES - GLA_GATE_RANK), (0, 0))).astype(BF16)
    gb = _pad_heads(gate_b.reshape(1, -1), 1)
    yd = _gla(proj, glow, gw, gb, norm_w, batch, seq)
    return yc, yd


def kernel(x, ln1_w, ln1_b, ln2_w, ln2_b, w_out, w_in_even, mlstm_gate_b, mlstm_norm_w, lru_conv_w, lru_conv_b, lru_wa, lru_ba, lru_wx, lru_bx, lru_lambda, w_in_odd, s5_lam_re, s5_lam_im, s5_b_re, s5_b_im, s5_c_re, s5_c_im, s5_d, s5_log_dt, s5_glu_w, s5_glu_b, gla_gate_w, gla_gate_b, gla_norm_w, router_w, router_bias, exp_w_gate, exp_w_up, exp_w_down, sh_w_gate, sh_w_up, sh_w_down):
    batch, seq, d = x.shape
    proj_w = [_even_proj_weights(w_in_even[layer // 2]) if layer % 2 == 0 else _odd_proj_weights(w_in_odd[layer // 2])
              for layer in range(DEPTH)]
    h = x.reshape(batch * seq, d)
    proj, small = _proj(h, *proj_w[0])
    for layer in range(DEPTH):
        j = layer // 2
        if layer % 2 == 0:
            y1, y2 = _even_mixer(proj, small, batch, seq, mlstm_gate_b[j], mlstm_norm_w[j],
                                 lru_conv_w[j], lru_conv_b[j], lru_wa[j], lru_ba[j], lru_wx[j],
                                 lru_bx[j], lru_lambda[j])
        else:
            y1, y2 = _odd_mixer(proj, small, batch, seq, s5_lam_re[j], s5_lam_im[j], s5_b_re[j],
                                s5_b_im[j], s5_c_re[j], s5_c_im[j], s5_d[j], s5_log_dt[j],
                                s5_glu_w[j], s5_glu_b[j], gla_gate_w[j], gla_gate_b[j], gla_norm_w[j])
        h, hpk, idx, rank, wk, counts = _out_proj_ln_route(y1, y2, h, w_out[layer], ln1_w[layer], ln1_b[layer],
                                                           router_w[layer], router_bias[layer])
        next_w = proj_w[layer + 1] if layer + 1 < DEPTH else None
        res = _moe_ln(h, hpk, idx, rank, wk, counts, exp_w_gate, exp_w_up, exp_w_down, layer,
                      sh_w_gate[layer], sh_w_up[layer], sh_w_down[layer], ln2_w[layer], ln2_b[layer], next_w)
        if next_w is None:
            (h,) = res
        else:
            h, proj, small = res
    return h.reshape(batch, seq, d)
```

```python
import functools
import math

import jax
import jax.numpy as jnp
from jax import lax
from jax.experimental import pallas as pl
from jax.experimental.pallas import tpu as pltpu
from jax.experimental.pallas import tpu_sc as plsc

F32 = jnp.float32
BF16 = jnp.bfloat16

D_MODEL = 1024
DEPTH = 2
MIX_HALF = 512
HEADS = 4
HEAD_DIM = 128
GLA_DK = 64
GLA_CHUNK = 64
GLA_GATE_RANK = 16
GLA_GATE_TEMP = 16.0
LRU_C = 8.0
LRU_CONV = 4
S5_GROUP = 16
S5_GROUPS = 32
S5_STATE = 64
S5_LANES = S5_GROUPS * S5_STATE
S5_BLOCKS = 4
N_EXPERTS = 64
N_GROUPS = 8
GROUP_SIZE = N_EXPERTS // N_GROUPS
TOP_K = 8
TOPK_GROUPS = 4
D_EXPERT = 256
ROUTED_SCALE = 2.5
ALPHA = (2.0 * DEPTH) ** 0.25
EPS = 1e-5
LANES = 128
SUBLANES = 8
NEG_INF = float("-inf")

VMEM_LIMIT = 56 * 1024 * 1024

MLSTM_CHUNK = 128
MLSTM_TILE = 1024
LRU_TILE = 1024
LRU_LOG_STEPS = 3
LRU_UNROLL = 4
S5_TILE = 512
S5_LOG_STEPS = 3
S5_UNROLL = True
GLA_UNROLL = 4
GLA_TILE = 1024
PROJ_TILE = 512
OUT_TILE = 512
MOE_BLOCK = 1024
XS_SLOTS = 3
COMBINE_TILE = 256
COMBINE_PARTS = 2
PACKED = D_MODEL // 2
SC_CHUNK = 64
SC_CORES = 2
SC_SUBCORES = 16
SC_WORKERS = SC_CORES * SC_SUBCORES


def _params(*sem):
    return pltpu.CompilerParams(dimension_semantics=sem, vmem_limit_bytes=VMEM_LIMIT)


def _split3(x):
    hi = x.astype(BF16)
    r1 = x - hi.astype(F32)
    mid = r1.astype(BF16)
    lo = (r1 - mid.astype(F32)).astype(BF16)
    return hi, mid, lo


def _dot(a, b):
    return jnp.dot(a, b, preferred_element_type=F32)


def _dot_nt(a, b):
    return lax.dot_general(a, b, (((1,), (1,)), ((), ())), preferred_element_type=F32)


def _dot_tn(a, b):
    return lax.dot_general(a, b, (((0,), (0,)), ((), ())), preferred_element_type=F32)


def _exact_left01(mask01_bf16, x):
    hi, mid, lo = _split3(x)
    return _dot(mask01_bf16, hi) + _dot(mask01_bf16, mid) + _dot(mask01_bf16, lo)


def _exact_right01(x, mask01_bf16):
    hi, mid, lo = _split3(x)
    return _dot(hi, mask01_bf16) + _dot(mid, mask01_bf16) + _dot(lo, mask01_bf16)


def _log_sigmoid(x):
    return jnp.minimum(x, 0.0) - jnp.log(1.0 + jnp.exp(-jnp.abs(x)))


def _sigmoid(x):
    return 1.0 / (1.0 + jnp.exp(-x))


def _gelu_tanh(x):
    c = math.sqrt(2.0 / math.pi)
    return 0.5 * x * (1.0 + jnp.tanh(c * (x + 0.044715 * (x * x * x))))


def _layer_norm(z, w, b):
    mu = jnp.mean(z, axis=-1, keepdims=True)
    zc = z - mu
    return zc * lax.rsqrt(jnp.mean(zc * zc, axis=-1, keepdims=True) + EPS) * w + b


def _proj_kernel(x_ref, w_ref, wg_ref, o_ref, og_ref):
    x = x_ref[...].astype(BF16)
    o_ref[...] = _dot(x, w_ref[...])
    og_ref[...] = _dot(x, wg_ref[...])


def _proj(x, w_main, w_small):
    t, d = x.shape
    n = w_main.shape[1]
    tm = PROJ_TILE
    return pl.pallas_call(
        _proj_kernel,
        grid=(t // tm,),
        in_specs=[pl.BlockSpec((tm, d), lambda i: (i, 0)),
                  pl.BlockSpec((d, n), lambda i: (0, 0)),
                  pl.BlockSpec((d, LANES), lambda i: (0, 0))],
        out_specs=[pl.BlockSpec((tm, n), lambda i: (i, 0)),
                   pl.BlockSpec((tm, LANES), lambda i: (i, 0))],
        out_shape=[jax.ShapeDtypeStruct((t, n), F32), jax.ShapeDtypeStruct((t, LANES), F32)],
        compiler_params=_params("parallel"),
        name="in_proj",
    )(x, w_main, w_small)


def _mlstm_kernel(q_ref, k_ref, v_ref, o_ref, gc_ref, gr_ref, bc_ref, br_ref, nw_ref,
                  y_ref, c_ref, m_ref, *, chunk, n_chunks):
    L = chunk

    @pl.when(pl.program_id(1) == 0)
    def _():
        c_ref[...] = jnp.zeros_like(c_ref)
        m_ref[...] = jnp.zeros_like(m_ref)

    ri = lax.broadcasted_iota(jnp.int32, (L, L), 0)
    ci = lax.broadcasted_iota(jnp.int32, (L, L), 1)
    causal = ci <= ri
    tril = causal.astype(BF16)
    triu = (ri <= ci).astype(BF16)
    ones_v = jnp.ones((L, HEAD_DIM), BF16)
    scale = HEAD_DIM ** -0.5

    def body(c, carry):
        r0 = pl.multiple_of(c * L, L)
        g_col = gc_ref[pl.ds(r0, L), :] + bc_ref[...]
        g_row = gr_ref[c] + br_ref[...]
        b_col_all = _exact_left01(tril, _log_sigmoid(g_col))
        b_row_all = _exact_right01(_log_sigmoid(g_row), triu)
        for h in range(HEADS):
            lo = h * HEAD_DIM
            q = q_ref[pl.ds(r0, L), lo:lo + HEAD_DIM].astype(BF16)
            k = k_ref[pl.ds(r0, L), lo:lo + HEAD_DIM] * scale
            v = v_ref[pl.ds(r0, L), lo:lo + HEAD_DIM].astype(BF16)
            v_aug = jnp.concatenate([v, ones_v], axis=1)
            i_rep = jnp.broadcast_to(g_col[:, h:h + 1], (L, LANES))
            b_rep = jnp.broadcast_to(b_col_all[:, HEADS + h:HEADS + h + 1], (L, LANES))
            i_row = g_row[h:h + 1, :]
            b_row = b_row_all[HEADS + h:HEADS + h + 1, :]
            b_last = b_rep[L - 1:L, :]
            m_prev = m_ref[h:h + 1, :]
            c_prev = c_ref[h]

            d_mat = jnp.where(causal, b_rep - b_row + i_row, NEG_INF)
            m_inter = b_rep + m_prev
            m_i = jnp.maximum(m_inter, jnp.max(d_mat, axis=1, keepdims=True))
            s = _dot_nt(q, k.astype(BF16)) * jnp.exp(d_mat - m_i)
            w_inter = jnp.exp(m_inter - m_i)
            intra = _dot(s.astype(BF16), v_aug)
            inter = _dot(q, c_prev.astype(BF16))
            num = intra[:, :HEAD_DIM] + w_inter * inter[:, :HEAD_DIM]
            den = intra[:, HEAD_DIM:] + w_inter * inter[:, HEAD_DIM:]
            hh = num / jnp.maximum(jnp.abs(den), jnp.exp(-m_i))

            w_loc = b_last - b_rep + i_rep
            m_loc = jnp.max(w_loc, axis=0, keepdims=True)
            kp = (k * jnp.exp(w_loc - m_loc)).astype(BF16)
            c_loc = _dot_tn(kp, v_aug)
            m_new = jnp.maximum(b_last + m_prev, m_loc)
            keep = jnp.exp(b_last + m_prev - m_new)
            add = jnp.exp(m_loc - m_new)
            c_ref[h] = (jnp.concatenate([keep, keep], axis=1) * c_prev
                        + jnp.concatenate([add, add], axis=1) * c_loc)
            m_ref[h:h + 1, :] = m_new

            hc = hh - jnp.mean(hh, axis=-1, keepdims=True)
            yn = hc * lax.rsqrt(jnp.mean(hc * hc, axis=-1, keepdims=True) + EPS)
            og = o_ref[pl.ds(r0, L), lo:lo + HEAD_DIM]
            y_ref[pl.ds(r0, L), lo:lo + HEAD_DIM] = yn * nw_ref[:, lo:lo + HEAD_DIM] * _sigmoid(og)
        return carry

    lax.fori_loop(0, n_chunks, body, 0)


def _mlstm(proj, gates, gate_b, norm_w, batch, seq):
    t = batch * seq
    L = MLSTM_CHUNK
    assert L == LANES, "the kernel keeps per-row gate terms replicated over one vreg of lanes"
    ts = MLSTM_TILE
    nj = seq // ts
    nc = ts // L
    g_row = gates[:, :2 * HEADS].reshape(t // L, L, 2 * HEADS).transpose(0, 2, 1)
    b_col = jnp.zeros((1, LANES), F32).at[0, :2 * HEADS].set(gate_b)
    b_row = gate_b.reshape(2 * HEADS, 1)
    blk = lambda col: pl.BlockSpec((ts, MIX_HALF), lambda b, j, col=col: (b * nj + j, col))
    kern = functools.partial(_mlstm_kernel, chunk=L, n_chunks=nc)
    return pl.pallas_call(
        kern,
        grid=(batch, nj),
        in_specs=[blk(0), blk(1), blk(2), blk(3),
                  pl.BlockSpec((ts, LANES), lambda b, j: (b * nj + j, 0)),
                  pl.BlockSpec((nc, 2 * HEADS, L), lambda b, j: (b * nj + j, 0, 0)),
                  pl.BlockSpec((1, LANES), lambda b, j: (0, 0)),
                  pl.BlockSpec((2 * HEADS, 1), lambda b, j: (0, 0)),
                  pl.BlockSpec((1, MIX_HALF), lambda b, j: (0, 0))],
        out_specs=pl.BlockSpec((ts, MIX_HALF), lambda b, j: (b * nj + j, 0)),
        out_shape=jax.ShapeDtypeStruct((t, MIX_HALF), F32),
        scratch_shapes=[pltpu.VMEM((HEADS, HEAD_DIM, 2 * HEAD_DIM), F32),
                        pltpu.VMEM((8, LANES), F32)],
        compiler_params=_params("arbitrary", "arbitrary"),
        name="mlstm",
    )(proj, proj, proj, proj, gates, g_row, b_col, b_row, norm_w.reshape(1, MIX_HALF))


def _rglru_kernel(xb_ref, gb_ref, cw_ref, cb_ref, wa_ref, ba_ref, wx_ref, bx_ref, lam_ref,
                  y_ref, xext_ref, h_ref, a_ref, u_ref, *, tile):
    @pl.when(pl.program_id(1) == 0)
    def _():
        xext_ref[0:8, :] = jnp.zeros((8, MIX_HALF), F32)
        h_ref[...] = jnp.zeros_like(h_ref)

    x = xb_ref[...]
    xext_ref[8:8 + tile, :] = x
    xc = cb_ref[...] + cw_ref[LRU_CONV - 1:LRU_CONV, :] * x
    for tap in range(LRU_CONV - 1):
        back = LRU_CONV - 1 - tap
        xc = xc + cw_ref[tap:tap + 1, :] * xext_ref[8 - back:8 - back + tile, :]
    xext_ref[0:8, :] = x[tile - 8:tile, :]

    xc16 = xc.astype(BF16)
    r_parts, i_parts = [], []
    for h in range(HEADS):
        lo = h * HEAD_DIM
        xh = xc16[:, lo:lo + HEAD_DIM]
        r_parts.append(_dot(xh, wa_ref[h]))
        i_parts.append(_dot(xh, wx_ref[h]))
    r = _sigmoid(jnp.concatenate(r_parts, axis=1) + ba_ref[...])
    ig = _sigmoid(jnp.concatenate(i_parts, axis=1) + bx_ref[...])
    lam = lam_ref[...]
    softplus_neg = jnp.maximum(-lam, 0.0) + jnp.log(1.0 + jnp.exp(-jnp.abs(lam)))
    log_a = -LRU_C * r * softplus_neg
    a = jnp.exp(log_a)
    th = jnp.tanh(log_a)
    u = jnp.sqrt(-2.0 * th / (1.0 - th)) * ig * xc

    a_ref[...] = a
    u_ref[...] = u
    rows = lax.broadcasted_iota(jnp.int32, (SUBLANES, MIX_HALF), 0)

    def group(i, h_prev):
        r0 = pl.multiple_of(i * SUBLANES, SUBLANES)
        ag = a_ref[pl.ds(r0, SUBLANES), :]
        ug = u_ref[pl.ds(r0, SUBLANES), :]
        for k in range(LRU_LOG_STEPS):
            keep = rows >= (1 << k)
            ug = ag * jnp.where(keep, pltpu.roll(ug, 1 << k, 0), 0.0) + ug
            ag = ag * jnp.where(keep, pltpu.roll(ag, 1 << k, 0), 1.0)
        hg = ug + ag * h_prev
        u_ref[pl.ds(r0, SUBLANES), :] = hg
        return hg[SUBLANES - 1:SUBLANES, :]

    h_last = lax.fori_loop(0, tile // SUBLANES, group, h_ref[0:1, :], unroll=LRU_UNROLL)
    h_ref[...] = jnp.broadcast_to(h_last, h_ref.shape)
    y_ref[...] = u_ref[...] * _gelu_tanh(gb_ref[...])


def _rglru(proj, conv_w, conv_b, wa, ba, wx, bx, lam, batch, seq):
    t = batch * seq
    ts = LRU_TILE
    nj = seq // ts
    row = lambda a: a.reshape(1, MIX_HALF)
    const2 = lambda shape: pl.BlockSpec(shape, lambda b, j: (0, 0))
    const3 = lambda shape: pl.BlockSpec(shape, lambda b, j: (0, 0, 0))
    blk = lambda col: pl.BlockSpec((ts, MIX_HALF), lambda b, j, col=col: (b * nj + j, col))
    return pl.pallas_call(
        functools.partial(_rglru_kernel, tile=ts),
        grid=(batch, nj),
        in_specs=[blk(4), blk(5), const2((LRU_CONV, MIX_HALF)), const2((1, MIX_HALF)),
                  const3((HEADS, HEAD_DIM, HEAD_DIM)), const2((1, MIX_HALF)),
                  const3((HEADS, HEAD_DIM, HEAD_DIM)), const2((1, MIX_HALF)), const2((1, MIX_HALF))],
        out_specs=pl.BlockSpec((ts, MIX_HALF), lambda b, j: (b * nj + j, 0)),
        out_shape=jax.ShapeDtypeStruct((t, MIX_HALF), F32),
        scratch_shapes=[pltpu.VMEM((ts + 8, MIX_HALF), F32), pltpu.VMEM((8, MIX_HALF), F32),
                        pltpu.VMEM((ts, MIX_HALF), F32), pltpu.VMEM((ts, MIX_HALF), F32)],
        compiler_params=_params("arbitrary", "arbitrary"),
        name="rglru",
    )(proj, proj, conv_w, row(conv_b), wa.astype(BF16), row(ba), wx.astype(BF16), row(bx), row(lam))


def _s5_kernel(u_ref, bre_ref, bim_ref, cre_ref, cim_ref, mre_ref, mim_ref, pre_ref, pim_ref, d_ref, gw_ref,
               gb_ref, y_ref, xr_ref, xi_ref, cr_ref, ci_ref, *, tile):
    @pl.when(pl.program_id(1) == 0)
    def _():
        cr_ref[...] = jnp.zeros_like(cr_ref)
        ci_ref[...] = jnp.zeros_like(ci_ref)

    u = u_ref[...]
    u16 = u.astype(BF16)
    blk_c = MIX_HALF // S5_BLOCKS
    blk_s = S5_LANES // S5_BLOCKS
    parts = []
    for j in range(S5_BLOCKS):
        lanes = slice(j * blk_s, (j + 1) * blk_s)
        uj = u16[:, j * blk_c:(j + 1) * blk_c]
        xr_ref[:, lanes] = _dot(uj, bre_ref[j])
        xi_ref[:, lanes] = _dot(uj, bim_ref[j])

        def group(i, carry, lanes=lanes):
            cr, ci = carry
            r0 = pl.multiple_of(i * SUBLANES, SUBLANES)
            xr = xr_ref[pl.ds(r0, SUBLANES), lanes]
            xi = xi_ref[pl.ds(r0, SUBLANES), lanes]
            for k in range(S5_LOG_STEPS):
                sr = pltpu.roll(xr, 1 << k, 0)
                si = pltpu.roll(xi, 1 << k, 0)
                mr = mre_ref[k, :, lanes]
                mi = mim_ref[k, :, lanes]
                xr, xi = xr + mr * sr - mi * si, xi + mr * si + mi * sr
            pr = pre_ref[:, lanes]
            pi = pim_ref[:, lanes]
            xr, xi = xr + pr * cr - pi * ci, xi + pr * ci + pi * cr
            xr_ref[pl.ds(r0, SUBLANES), lanes] = xr
            xi_ref[pl.ds(r0, SUBLANES), lanes] = xi
            return xr[SUBLANES - 1:SUBLANES, :], xi[SUBLANES - 1:SUBLANES, :]

        cr, ci = lax.fori_loop(0, tile // SUBLANES, group, (cr_ref[0:1, lanes], ci_ref[0:1, lanes]),
                               unroll=S5_UNROLL)
        cr_ref[0:1, lanes] = cr
        ci_ref[0:1, lanes] = ci
        parts.append(_dot(xr_ref[:, lanes].astype(BF16), cre_ref[j])
                     - _dot(xi_ref[:, lanes].astype(BF16), cim_ref[j]))
    y = jnp.concatenate(parts, axis=1) + d_ref[...] * u
    g = _gelu_tanh(y)
    y_ref[...] = g * _sigmoid(_dot(g.astype(BF16), gw_ref[...]) + gb_ref[...])


def _s5_tables(lam_re, lam_im, b_re, b_im, c_re, c_im, log_dt):
    lr, li = lam_re.astype(F32), lam_im.astype(F32)
    dt = jnp.exp(log_dt.astype(F32))[:, None]
    mag = jnp.exp(lr * dt)
    abar_re = mag * jnp.cos(li * dt)
    abar_im = mag * jnp.sin(li * dt)
    den = lr * lr + li * li
    nr = abar_re - 1.0
    coef_re = (nr * lr + abar_im * li) / den
    coef_im = (abar_im * lr - nr * li) / den
    bbar_re = coef_re[..., None] * b_re - coef_im[..., None] * b_im
    bbar_im = coef_re[..., None] * b_im + coef_im[..., None] * b_re
    gpb = S5_GROUPS // S5_BLOCKS
    eye = jnp.eye(gpb, dtype=F32)

    def in_map(bb):
        bb = bb.reshape(S5_BLOCKS, gpb, S5_STATE, S5_GROUP)
        return jnp.einsum("jgph,gk->jghkp", bb, eye).reshape(S5_BLOCKS, gpb * S5_GROUP, gpb * S5_STATE)

    def out_map(cc):
        cc = cc.reshape(S5_BLOCKS, gpb, S5_GROUP, S5_STATE)
        return jnp.einsum("jghp,gk->jgpkh", cc, eye).reshape(S5_BLOCKS, gpb * S5_STATE, gpb * S5_GROUP)

    def power(n):
        n = jnp.asarray(n, F32)[..., None, None]
        pmag = jnp.exp(n * (lr * dt))
        shape = n.shape[:-2] + (S5_LANES,)
        return (pmag * jnp.cos(n * (li * dt))).reshape(shape), (pmag * jnp.sin(n * (li * dt))).reshape(shape)

    row = jnp.arange(SUBLANES)
    step = 2 ** jnp.arange(S5_LOG_STEPS)
    s_re, s_im = power(step)
    keep = (row[None, :] >= step[:, None])[..., None]
    m_re = jnp.where(keep, s_re[:, None, :], 0.0)
    m_im = jnp.where(keep, s_im[:, None, :], 0.0)
    p_re, p_im = power(row + 1)
    return (in_map(bbar_re).astype(BF16), in_map(bbar_im).astype(BF16),
            out_map(c_re.astype(F32)).astype(BF16), out_map(c_im.astype(F32)).astype(BF16),
            m_re, m_im, p_re, p_im)


def _s5(proj, tables, d_skip, glu_w, glu_b, batch, seq):
    t = batch * seq
    ts = S5_TILE
    nj = seq // ts
    bre, bim, cre, cim, m_re, m_im, p_re, p_im = tables
    blk_c = MIX_HALF // S5_BLOCKS
    blk_s = S5_LANES // S5_BLOCKS
    const2 = lambda shape: pl.BlockSpec(shape, lambda b, j: (0, 0))
    const3 = lambda shape: pl.BlockSpec(shape, lambda b, j: (0, 0, 0))
    return pl.pallas_call(
        functools.partial(_s5_kernel, tile=ts),
        grid=(batch, nj),
        in_specs=[pl.BlockSpec((ts, MIX_HALF), lambda b, j: (b * nj + j, 0)),
                  const3((S5_BLOCKS, blk_c, blk_s)), const3((S5_BLOCKS, blk_c, blk_s)),
                  const3((S5_BLOCKS, blk_s, blk_c)), const3((S5_BLOCKS, blk_s, blk_c)),
                  const3(m_re.shape), const3(m_im.shape), const2(p_re.shape), const2(p_im.shape),
                  const2((1, MIX_HALF)), const2((MIX_HALF, MIX_HALF)), const2((1, MIX_HALF))],
        out_specs=pl.BlockSpec((ts, MIX_HALF), lambda b, j: (b * nj + j, 0)),
        out_shape=jax.ShapeDtypeStruct((t, MIX_HALF), F32),
        scratch_shapes=[pltpu.VMEM((ts, S5_LANES), F32), pltpu.VMEM((ts, S5_LANES), F32),
                        pltpu.VMEM((8, S5_LANES), F32), pltpu.VMEM((8, S5_LANES), F32)],
        compiler_params=_params("arbitrary", "arbitrary"),
        name="s5",
    )(proj, bre, bim, cre, cim, m_re, m_im, p_re, p_im, d_skip.reshape(1, MIX_HALF), glu_w.astype(BF16),
      glu_b.reshape(1, MIX_HALF))


def _gla_kernel(q_ref, k_ref, v_ref, r_ref, gl_ref, gw_ref, gb_ref, nw_ref, y_ref,
                st_ref, qd_ref, ki_ref, ke_ref, v16_ref, dec_ref, o_ref, *, tile, chunk):
    L = chunk
    nc = tile // L

    @pl.when(pl.program_id(1) == 0)
    def _():
        st_ref[...] = jnp.zeros_like(st_ref)

    z = _dot(gl_ref[...].astype(BF16), gw_ref[...]) + gb_ref[...]
    bcum = _log_sigmoid(z) * (1.0 / GLA_GATE_TEMP)
    row_in_chunk = lax.broadcasted_iota(jnp.int32, bcum.shape, 0) & (L - 1)
    s = 1
    while s < L:
        bcum = bcum + jnp.where(row_in_chunk >= s, pltpu.roll(bcum, s, 0), 0.0)
        s *= 2
    b3 = bcum.reshape(nc, L, MIX_HALF)
    b_last = b3[:, L - 1:L, :]
    k = k_ref[...]
    qd_ref[...] = (q_ref[...] * (GLA_DK ** -0.5) * jnp.exp(bcum)).astype(BF16)
    ki_ref[...] = (k * jnp.exp(-bcum)).astype(BF16)
    ke_ref[...] = (k.reshape(nc, L, MIX_HALF) * jnp.exp(b_last - b3)).reshape(tile, MIX_HALF).astype(BF16)
    v16_ref[...] = v_ref[...].astype(BF16)
    dec_ref[...] = jnp.exp(b_last)

    ri = lax.broadcasted_iota(jnp.int32, (L, L), 0)
    ci = lax.broadcasted_iota(jnp.int32, (L, L), 1)
    causal = ci <= ri

    def body(c, carry):
        r0 = pl.multiple_of(c * L, L)
        dec = dec_ref[c]
        for h in range(HEADS):
            lo = h * HEAD_DIM
            q_dec = qd_ref[pl.ds(r0, L), lo:lo + HEAD_DIM]
            v = v16_ref[pl.ds(r0, L), lo:lo + HEAD_DIM]
            st = st_ref[h]
            att = jnp.where(causal, _dot_nt(q_dec, ki_ref[pl.ds(r0, L), lo:lo + HEAD_DIM]), 0.0)
            o_ref[pl.ds(r0, L), lo:lo + HEAD_DIM] = (_dot(att.astype(BF16), v)
                                                     + _dot_nt(q_dec, st.astype(BF16)))
            st_ref[h] = dec[:, lo:lo + HEAD_DIM] * st + _dot_tn(v, ke_ref[pl.ds(r0, L), lo:lo + HEAD_DIM])
        return carry

    lax.fori_loop(0, nc, body, 0, unroll=GLA_UNROLL)

    rg = r_ref[...]
    gate = nw_ref[...] * (rg * _sigmoid(rg))
    for h in range(HEADS):
        lo = h * HEAD_DIM
        o = o_ref[:, lo:lo + HEAD_DIM]
        yn = o * lax.rsqrt(jnp.mean(o * o, axis=-1, keepdims=True) + EPS)
        y_ref[:, lo:lo + HEAD_DIM] = yn * gate[:, lo:lo + HEAD_DIM]


def _gla(proj, glow, gate_w, gate_b, norm_w, batch, seq):
    t = batch * seq
    ts = GLA_TILE
    nj = seq // ts
    blk = lambda col: pl.BlockSpec((ts, MIX_HALF), lambda b, j, col=col: (b * nj + j, col))
    const2 = lambda shape: pl.BlockSpec(shape, lambda b, j: (0, 0))
    return pl.pallas_call(
        functools.partial(_gla_kernel, tile=ts, chunk=GLA_CHUNK),
        grid=(batch, nj),
        in_specs=[blk(1), blk(2), blk(3), blk(4),
                  pl.BlockSpec((ts, LANES), lambda b, j: (b * nj + j, 0)),
                  const2((LANES, MIX_HALF)), const2((1, MIX_HALF)), const2((1, MIX_HALF))],
        out_specs=pl.BlockSpec((ts, MIX_HALF), lambda b, j: (b * nj + j, 0)),
        out_shape=jax.ShapeDtypeStruct((t, MIX_HALF), F32),
        scratch_shapes=[pltpu.VMEM((HEADS, HEAD_DIM, HEAD_DIM), F32),
                        pltpu.VMEM((ts, MIX_HALF), BF16), pltpu.VMEM((ts, MIX_HALF), BF16),
                        pltpu.VMEM((ts, MIX_HALF), BF16), pltpu.VMEM((ts, MIX_HALF), BF16),
                        pltpu.VMEM((ts // GLA_CHUNK, 1, MIX_HALF), F32),
                        pltpu.VMEM((ts, MIX_HALF), F32)],
        compiler_params=_params("arbitrary", "arbitrary"),
        name="gla",
    )(proj, proj, proj, proj, glow, gate_w, gate_b, norm_w.reshape(1, MIX_HALF))


def _pad_heads(w, axis):
    shape = list(w.shape)
    shape[axis:axis + 1] = [HEADS, GLA_DK]
    w = w.reshape(shape)
    pad = [(0, 0)] * w.ndim
    pad[axis + 1] = (0, HEAD_DIM - GLA_DK)
    w = jnp.pad(w, pad)
    shape[axis:axis + 2] = [HEADS * HEAD_DIM]
    return w.reshape(shape)


def _pack_bf16_pairs(z):
    hi = lax.bitcast_convert_type(z[:, :PACKED].astype(BF16).astype(F32), jnp.uint32)
    lo = lax.bitcast_convert_type(z[:, PACKED:].astype(BF16).astype(F32), jnp.uint32)
    word = (hi & jnp.uint32(0xFFFF0000)) | lax.shift_right_logical(lo, jnp.uint32(16))
    return lax.bitcast_convert_type(word, jnp.int32)


def _unpack_bf16_pairs(p):
    word = lax.bitcast_convert_type(p, jnp.uint32)
    hi = lax.bitcast_convert_type(word & jnp.uint32(0xFFFF0000), F32)
    lo = lax.bitcast_convert_type(lax.shift_left(word, jnp.uint32(16)), F32)
    return hi, lo


def _out_kernel(ya_ref, yb_ref, h_ref, w_ref, lw_ref, lb_ref, rw_ref, rb_ref,
                o_ref, opk_ref, idx_ref, rank_ref, wk_ref, cnt_ref, base_ref, *, tile):
    mixed = jnp.concatenate([ya_ref[...], yb_ref[...]], axis=1).astype(BF16)
    z = ALPHA * h_ref[...] + _dot(mixed, w_ref[...])
    out = _layer_norm(z, lw_ref[...], lb_ref[...])
    o_ref[...] = out
    opk_ref[...] = _pack_bf16_pairs(out)
    _route_tile(out, rw_ref, rb_ref, idx_ref, rank_ref, wk_ref, cnt_ref, base_ref, tile)


def _out_proj_ln_route(ya, yb, h, w_out, ln_w, ln_b, router_w, router_bias):
    t = h.shape[0]
    tm = OUT_TILE
    const = lambda shape: pl.BlockSpec(shape, lambda i: (0, 0))
    per_tok = lambda dt: jax.ShapeDtypeStruct((TOP_K, t), dt)
    tok_blk = pl.BlockSpec((TOP_K, tm), lambda i: (0, i))
    return pl.pallas_call(
        functools.partial(_out_kernel, tile=tm),
        grid=(t // tm,),
        in_specs=[pl.BlockSpec((tm, MIX_HALF), lambda i: (i, 0)),
                  pl.BlockSpec((tm, MIX_HALF), lambda i: (i, 0)),
                  pl.BlockSpec((tm, D_MODEL), lambda i: (i, 0)),
                  const((D_MODEL, D_MODEL)), const((1, D_MODEL)), const((1, D_MODEL)),
                  const((N_EXPERTS, D_MODEL)), const((N_EXPERTS, 1))],
        out_specs=[pl.BlockSpec((tm, D_MODEL), lambda i: (i, 0)),
                   pl.BlockSpec((tm, PACKED), lambda i: (i, 0)),
                   tok_blk, tok_blk, tok_blk, const((N_EXPERTS, LANES))],
        out_shape=[jax.ShapeDtypeStruct((t, D_MODEL), F32), jax.ShapeDtypeStruct((t, PACKED), jnp.int32),
                   per_tok(jnp.int32), per_tok(jnp.int32), per_tok(F32),
                   jax.ShapeDtypeStruct((N_EXPERTS, LANES), F32)],
        scratch_shapes=[pltpu.VMEM((N_EXPERTS, LANES), F32)],
        compiler_params=_params("arbitrary"),
        name="out_proj_ln_route",
    )(ya, yb, h, w_out.astype(BF16), ln_w.reshape(1, D_MODEL), ln_b.reshape(1, D_MODEL),
      router_w.T, router_bias.reshape(N_EXPERTS, 1))


def _first_index(hit, idx, big):
    return jnp.min(jnp.where(hit, idx, big), axis=0, keepdims=True)


def _route_tile(h, w_ref, b_ref, idx_ref, rank_ref, wk_ref, cnt_ref, base_ref, tile):
    @pl.when(pl.program_id(0) == 0)
    def _():
        base_ref[...] = jnp.zeros_like(base_ref)

    h_hi, h_mid, _ = _split3(h)
    w_hi, w_mid, _ = _split3(w_ref[...])
    logits = _dot_nt(w_hi, h_hi) + _dot_nt(w_hi, h_mid) + _dot_nt(w_mid, h_hi)
    scores = _sigmoid(logits)
    biased = scores + b_ref[...]

    sub = lax.broadcasted_iota(jnp.int32, (GROUP_SIZE, tile), 0)
    grp_rows = []
    for g in range(N_GROUPS):
        xg = biased[g * GROUP_SIZE:(g + 1) * GROUP_SIZE, :]
        m1 = jnp.max(xg, axis=0, keepdims=True)
        i1 = _first_index(xg == m1, sub, GROUP_SIZE)
        m2 = jnp.max(jnp.where(sub == i1, NEG_INF, xg), axis=0, keepdims=True)
        grp_rows.append(m1 + m2)
    gs = jnp.concatenate(grp_rows, axis=0)
    gsel = jnp.zeros((N_GROUPS, tile), F32)
    for _ in range(TOPK_GROUPS):
        mx = jnp.max(gs, axis=0, keepdims=True)
        hit = sub == _first_index(gs == mx, sub, N_GROUPS)
        gsel = jnp.where(hit, 1.0, gsel)
        gs = jnp.where(hit, NEG_INF, gs)
    emask = jnp.concatenate(
        [jnp.broadcast_to(gsel[g:g + 1, :], (GROUP_SIZE, tile)) for g in range(N_GROUPS)], axis=0)

    eidx = lax.broadcasted_iota(jnp.int32, (N_EXPERTS, tile), 0)
    cand = jnp.where(emask > 0.5, biased, NEG_INF)
    sel = jnp.zeros((N_EXPERTS, tile), F32)
    picks = []
    for _ in range(TOP_K):
        mx = jnp.max(cand, axis=0, keepdims=True)
        first = _first_index(cand == mx, eidx, N_EXPERTS)
        hit = eidx == first
        picks.append(first)
        sel = jnp.where(hit, 1.0, sel)
        cand = jnp.where(hit, NEG_INF, cand)
    picked = jnp.where(sel > 0.5, scores, 0.0)
    wts = picked / jnp.sum(picked, axis=0, keepdims=True) * ROUTED_SCALE

    ri = lax.broadcasted_iota(jnp.int32, (tile, tile), 0)
    ci = lax.broadcasted_iota(jnp.int32, (tile, tile), 1)
    before = (ri < ci).astype(BF16)
    prior = _dot(sel.astype(BF16), before) + base_ref[:, 0:1]
    ranks = [jnp.sum(jnp.where(eidx == p, prior, 0.0), axis=0, keepdims=True) for p in picks]
    wsel = [jnp.sum(jnp.where(eidx == p, wts, 0.0), axis=0, keepdims=True) for p in picks]
    idx_ref[...] = jnp.concatenate(picks, axis=0)
    rank_ref[...] = jnp.concatenate(ranks, axis=0).astype(jnp.int32)
    wk_ref[...] = jnp.concatenate(wsel, axis=0)
    total = base_ref[...] + jnp.sum(sel, axis=1, keepdims=True)
    base_ref[...] = total
    cnt_ref[...] = total


def _silu(x):
    return x * _sigmoid(x)


def _sc_mesh():
    return plsc.VectorSubcoreMesh(core_axis_name="c", subcore_axis_name="s")


def _sc_worker_id():
    return lax.axis_index("s") * SC_CORES + lax.axis_index("c")


def _dispatch_rows(xpk, pos_chunks, n_rows):
    t = xpk.shape[0]
    n_ch = t // SC_WORKERS // SC_CHUNK

    @functools.partial(
        pl.kernel, mesh=_sc_mesh(),
        out_type=jax.ShapeDtypeStruct((n_rows, PACKED), jnp.int32),
        scratch_types=[pltpu.VMEM((TOP_K, SC_CHUNK), jnp.int32),
                       pltpu.VMEM((SC_CHUNK, PACKED), jnp.int32),
                       pltpu.SemaphoreType.DMA],
        name="moe_dispatch",
    )
    def scatter(x_hbm, pos_hbm, out_hbm, idx_v, rows_v, sem):
        wid = _sc_worker_id()

        @pl.loop(0, n_ch)
        def _(c):
            chunk = wid * n_ch + c
            off = pl.multiple_of(chunk * SC_CHUNK, SC_CHUNK)
            pltpu.sync_copy(pos_hbm.at[chunk], idx_v)
            pltpu.sync_copy(x_hbm.at[pl.ds(off, SC_CHUNK)], rows_v)
            copies = [pltpu.async_copy(rows_v, out_hbm.at[idx_v.at[k]], sem) for k in range(TOP_K)]
            for cp in copies:
                cp.wait()

    return scatter(xpk, pos_chunks)


def _gather_rows(table, idx):
    n = idx.shape[0]
    per_w = n // SC_WORKERS
    n_ch = per_w // SC_CHUNK
    assert n_ch % 2 == 0 and n_ch >= 2

    @functools.partial(
        pl.kernel, mesh=_sc_mesh(),
        out_type=jax.ShapeDtypeStruct((n, PACKED), jnp.int32),
        scratch_types=[pltpu.VMEM((n_ch, SC_CHUNK), jnp.int32),
                       pltpu.VMEM((SC_CHUNK, PACKED), jnp.int32), pltpu.VMEM((SC_CHUNK, PACKED), jnp.int32),
                       pltpu.SemaphoreType.DMA, pltpu.SemaphoreType.DMA,
                       pltpu.SemaphoreType.DMA, pltpu.SemaphoreType.DMA],
        name="moe_gather",
    )
    def gather(table_hbm, idx_hbm, out_hbm, idx_v, rows0, rows1, g0, g1, w0, w1):
        wid = _sc_worker_id()
        base = wid * per_w
        rows, g_sem, w_sem = (rows0, rows1), (g0, g1), (w0, w1)
        pltpu.sync_copy(idx_hbm.at[wid], idx_v)

        def fetch(c, b):
            return pltpu.make_async_copy(table_hbm.at[idx_v.at[c]], rows[b], g_sem[b])

        def flush(c, b):
            off = pl.multiple_of(base + c * SC_CHUNK, SC_CHUNK)
            return pltpu.make_async_copy(rows[b], out_hbm.at[pl.ds(off, SC_CHUNK)], w_sem[b])

        fetch(0, 0).start()

        @pl.loop(0, n_ch, step=2)
        def _(c0):
            for b in range(2):
                c = c0 + b
                fetch(c, b).wait()
                flush(c, b).start()

                @pl.when(c + 1 < n_ch)
                def _():
                    @pl.when(c >= 1)
                    def _():
                        flush(c - 1, 1 - b).wait()
                    fetch(c + 1, 1 - b).start()

        flush(n_ch - 2, 0).wait()
        flush(n_ch - 1, 1).wait()

    return gather(table, idx.reshape(SC_WORKERS, n_ch, SC_CHUNK))


def _unpacked_bf16(p):
    hi, lo = _unpack_bf16_pairs(p)
    return jnp.concatenate([hi.astype(BF16), lo.astype(BF16)], axis=1)


def _expert_kernel(be_ref, nu_ref, xs_hbm, wg_ref, wu_ref, wd_ref, y_ref, g16_ref, u16_ref, d16_ref,
                   xbuf_ref, xsem):
    i = pl.program_id(0)
    n_used = nu_ref[0]
    bm = xbuf_ref.shape[1]

    def fetch(b):
        slot = lax.rem(b, XS_SLOTS)
        rows = pl.ds(pl.multiple_of(b * bm, bm), bm)
        return pltpu.make_async_copy(xs_hbm.at[rows], xbuf_ref.at[slot], xsem.at[slot])

    @pl.when(i == 0)
    def _():
        fetch(0).start()

        @pl.when(n_used > 1)
        def _():
            fetch(1).start()

    @pl.when(i + 2 < n_used)
    def _():
        fetch(i + 2).start()

    changed = jnp.logical_or(i == 0, be_ref[i] != be_ref[jnp.maximum(i - 1, 0)])

    @pl.when(changed)
    def _():
        g16_ref[...] = wg_ref[0, 0].astype(BF16)
        u16_ref[...] = wu_ref[0, 0].astype(BF16)
        d16_ref[...] = wd_ref[0, 0].astype(BF16)

    @pl.when(i < n_used)
    def _():
        fetch(i).wait()
        x = _unpacked_bf16(xbuf_ref[lax.rem(i, XS_SLOTS)])
        hh = _silu(_dot(x, g16_ref[...])) * _dot(x, u16_ref[...])
        y_ref[...] = _pack_bf16_pairs(_dot(hh.astype(BF16), d16_ref[...]))


def _experts(block_e, n_used, xs, wg, wu, wd, layer):
    nb = block_e.shape[0]
    bm = MOE_BLOCK
    grid_spec = pltpu.PrefetchScalarGridSpec(
        num_scalar_prefetch=2,
        grid=(nb,),
        in_specs=[pl.BlockSpec(memory_space=pl.ANY),
                  pl.BlockSpec((1, 1, D_MODEL, D_EXPERT), lambda i, be, nu: (layer, be[i], 0, 0)),
                  pl.BlockSpec((1, 1, D_MODEL, D_EXPERT), lambda i, be, nu: (layer, be[i], 0, 0)),
                  pl.BlockSpec((1, 1, D_EXPERT, D_MODEL), lambda i, be, nu: (layer, be[i], 0, 0))],
        out_specs=pl.BlockSpec((bm, PACKED), lambda i, be, nu: (jnp.minimum(i, nu[0] - 1), 0)),
        scratch_shapes=[pltpu.VMEM((D_MODEL, D_EXPERT), BF16), pltpu.VMEM((D_MODEL, D_EXPERT), BF16),
                        pltpu.VMEM((D_EXPERT, D_MODEL), BF16),
                        pltpu.VMEM((XS_SLOTS, bm, PACKED), jnp.int32), pltpu.SemaphoreType.DMA((XS_SLOTS,))],
    )
    return pl.pallas_call(
        _expert_kernel,
        grid_spec=grid_spec,
        out_shape=jax.ShapeDtypeStruct((nb * bm, PACKED), jnp.int32),
        compiler_params=_params("arbitrary"),
        name="moe_experts",
    )(block_e, n_used, xs, wg, wu, wd)


def _combine_kernel(g_ref, wk_ref, h_ref, xpk_ref, sg_ref, su_ref, sd_ref, lw_ref, lb_ref, *rest, with_proj):
    if with_proj:
        (wm_ref, ws_ref), (o_ref, proj_ref, small_ref) = rest[:2], rest[-3:]
        next_proj = (wm_ref, ws_ref, proj_ref, small_ref)
    else:
        o_ref, next_proj = rest[-1], None
    x = _unpacked_bf16(xpk_ref[...])
    hs = _silu(_dot(x, sg_ref[...])) * _dot(x, su_ref[...])
    shared = _dot(hs.astype(BF16), sd_ref[...])
    acc_hi = shared[:, :PACKED]
    acc_lo = shared[:, PACKED:]
    wk = wk_ref[...]
    for k in range(TOP_K):
        y_hi, y_lo = _unpack_bf16_pairs(g_ref[k])
        w = wk[:, k:k + 1]
        acc_hi = acc_hi + w * y_hi
        acc_lo = acc_lo + w * y_lo
    ffn = jnp.concatenate([acc_hi, acc_lo], axis=1)
    out = _layer_norm(ALPHA * h_ref[...] + ffn, lw_ref[...], lb_ref[...])
    o_ref[...] = out
    if next_proj is not None:
        wm_ref, ws_ref, proj_ref, small_ref = next_proj
        out16 = out.astype(BF16)
        proj_ref[...] = _dot(out16, wm_ref[...])
        small_ref[...] = _dot(out16, ws_ref[...])


def _combine_ln(g, wk, h, xpk, sg, su, sd, ln_w, ln_b, next_w, part, prev):
    t = h.shape[0]
    tm = COMBINE_TILE
    n_blk = g.shape[1] // tm
    first = part * n_blk
    const = lambda shape: pl.BlockSpec(shape, lambda i: (0, 0))
    rows = lambda width: pl.BlockSpec((tm, width), lambda i: (i + first, 0))
    in_specs = [pl.BlockSpec((TOP_K, tm, PACKED), lambda i: (0, i, 0)), rows(TOP_K), rows(D_MODEL), rows(PACKED),
                const((D_MODEL, D_EXPERT)), const((D_MODEL, D_EXPERT)), const((D_EXPERT, D_MODEL)),
                const((1, D_MODEL)), const((1, D_MODEL))]
    args = [g, wk, h, xpk, sg.astype(BF16), su.astype(BF16), sd.astype(BF16),
            ln_w.reshape(1, D_MODEL), ln_b.reshape(1, D_MODEL)]
    out_specs = [rows(D_MODEL)]
    out_shape = [jax.ShapeDtypeStruct((t, D_MODEL), F32)]
    if next_w is not None:
        w_main, w_small = next_w
        n = w_main.shape[1]
        in_specs += [const((D_MODEL, n)), const((D_MODEL, LANES))]
        args += [w_main, w_small]
        out_specs += [rows(n), rows(LANES)]
        out_shape += [jax.ShapeDtypeStruct((t, n), F32), jax.ShapeDtypeStruct((t, LANES), F32)]
    aliases = {}
    if prev is not None:
        aliases = {len(args) + k: k for k in range(len(prev))}
        in_specs += [pl.BlockSpec(memory_space=pl.ANY)] * len(prev)
        args += list(prev)
    return pl.pallas_call(
        functools.partial(_combine_kernel, with_proj=next_w is not None),
        grid=(n_blk,),
        in_specs=in_specs,
        out_specs=out_specs,
        out_shape=out_shape,
        input_output_aliases=aliases,
        compiler_params=_params("parallel"),
        name="moe_combine_ln",
    )(*args)


def _moe_ln(h, hpk, idx, rank, wk, counts, wg, wu, wd, layer, sg, su, sd, ln_w, ln_b, next_w):
    t = h.shape[0]
    cnt = counts[:, 0].astype(jnp.int32)
    padded = (cnt + MOE_BLOCK - 1) // MOE_BLOCK * MOE_BLOCK
    pend = jnp.cumsum(padded)
    experts = jnp.arange(N_EXPERTS, dtype=jnp.int32)
    pstart_of_pick = jnp.sum(jnp.where(idx[:, :, None] == experts, pend - padded, 0), axis=-1)
    pos = pstart_of_pick + rank
    nb = -(-(t * TOP_K + N_EXPERTS * (MOE_BLOCK - 1)) // MOE_BLOCK)
    starts = jnp.arange(nb, dtype=jnp.int32) * MOE_BLOCK
    block_e = jnp.minimum(jnp.sum((pend[None, :] <= starts[:, None]).astype(jnp.int32), axis=1), N_EXPERTS - 1)
    n_used = (pend[-1] // MOE_BLOCK).astype(jnp.int32).reshape(1)
    pos_chunks = pos.reshape(TOP_K, t // SC_CHUNK, SC_CHUNK).transpose(1, 0, 2)
    xs = _dispatch_rows(hpk, pos_chunks, nb * MOE_BLOCK)
    ys = _experts(block_e, n_used, xs, wg, wu, wd, layer)
    part = t // COMBINE_PARTS
    gathered = [_gather_rows(ys, pos[:, p * part:(p + 1) * part].reshape(-1)).reshape(TOP_K, part, PACKED)
                for p in range(COMBINE_PARTS)]
    wk_t = wk.T
    outs = None
    for p in range(COMBINE_PARTS):
        outs = _combine_ln(gathered[p], wk_t, h, hpk, sg, su, sd, ln_w, ln_b, next_w, p, outs)
    return outs


def _pad_cols(w, width=LANES):
    return jnp.pad(w, ((0, 0), (0, width - w.shape[1])))


def _even_proj_weights(w_in):
    a4 = 4 * MIX_HALF
    ng = 2 * HEADS
    w_main = jnp.concatenate([w_in[:, :a4], w_in[:, a4 + ng:]], axis=1).astype(BF16)
    w_gate = _pad_cols(w_in[:, a4:a4 + ng]).astype(BF16)
    return w_main, w_gate


def _even_mixer(proj, gates, batch, seq, gate_b, norm_w, conv_w, conv_b, wa, ba, wx, bx, lam):
    ya = _mlstm(proj, gates, gate_b, norm_w, batch, seq)
    yb = _rglru(proj, conv_w, conv_b, wa, ba, wx, bx, lam, batch, seq)
    return ya, yb


def _odd_proj_weights(w_in):
    c0 = MIX_HALF
    c1 = c0 + HEADS * GLA_DK
    c2 = c1 + HEADS * GLA_DK
    c3 = c2 + MIX_HALF
    c4 = c3 + MIX_HALF
    w_main = jnp.concatenate([w_in[:, :c0], _pad_heads(w_in[:, c0:c1], 1), _pad_heads(w_in[:, c1:c2], 1),
                              w_in[:, c2:c4]], axis=1).astype(BF16)
    w_low = _pad_cols(w_in[:, c4:]).astype(BF16)
    return w_main, w_low


def _odd_mixer(proj, glow, batch, seq, lam_re, lam_im, b_re, b_im, c_re, c_im, d_skip, log_dt,
               glu_w, glu_b, gate_w, gate_b, norm_w):
    tables = _s5_tables(lam_re, lam_im, b_re, b_im, c_re, c_im, log_dt)
    yc = _s5(proj, tables, d_skip, glu_w, glu_b, batch, seq)
    gw = jnp.pad(_pad_heads(gate_w, 1), ((0, LANES - GLA_GATE_RANK), (0, 0))).astype(BF16)
    gb = _pad_heads(gate_b.reshape(1, -1), 1)
    yd = _gla(proj, glow, gw, gb, norm_w, batch, seq)
    return yc, yd


def kernel(x, ln1_w, ln1_b, ln2_w, ln2_b, w_out, w_in_even, mlstm_gate_b, mlstm_norm_w, lru_conv_w, lru_conv_b, lru_wa, lru_ba, lru_wx, lru_bx, lru_lambda, w_in_odd, s5_lam_re, s5_lam_im, s5_b_re, s5_b_im, s5_c_re, s5_c_im, s5_d, s5_log_dt, s5_glu_w, s5_glu_b, gla_gate_w, gla_gate_b, gla_norm_w, router_w, router_bias, exp_w_gate, exp_w_up, exp_w_down, sh_w_gate, sh_w_up, sh_w_down):
    batch, seq, d = x.shape
    proj_w = [_even_proj_weights(w_in_even[layer // 2]) if layer % 2 == 0 else _odd_proj_weights(w_in_odd[layer // 2])
              for layer in range(DEPTH)]
    h = x.reshape(batch * seq, d)
    proj, small = _proj(h, *proj_w[0])
    for layer in range(DEPTH):
        j = layer // 2
        if layer % 2 == 0:
            y1, y2 = _even_mixer(proj, small, batch, seq, mlstm_gate_b[j], mlstm_norm_w[j],
                                 lru_conv_w[j], lru_conv_b[j], lru_wa[j], lru_ba[j], lru_wx[j],
                                 lru_bx[j], lru_lambda[j])
        else:
            y1, y2 = _odd_mixer(proj, small, batch, seq, s5_lam_re[j], s5_lam_im[j], s5_b_re[j],
                                s5_b_im[j], s5_c_re[j], s5_c_im[j], s5_d[j], s5_log_dt[j],
                                s5_glu_w[j], s5_glu_b[j], gla_gate_w[j], gla_gate_b[j], gla_norm_w[j])
        h, hpk, idx, rank, wk, counts = _out_proj_ln_route(y1, y2, h, w_out[layer], ln1_w[layer], ln1_b[layer],
                                                           router_w[layer], router_bias[layer])
        next_w = proj_w[layer + 1] if layer + 1 < DEPTH else None
        res = _moe_ln(h, hpk, idx, rank, wk, counts, exp_w_gate, exp_w_up, exp_w_down, layer,
                      sh_w_gate[layer], sh_w_up[layer], sh_w_down[layer], ln2_w[layer], ln2_b[layer], next_w)
        if next_w is None:
            (h,) = res
        else:
            h, proj, small = res
    return h.reshape(batch, seq, d)
```

```python
import functools
import math

import jax
import jax.numpy as jnp
from jax import lax
from jax.experimental import pallas as pl
from jax.experimental.pallas import tpu as pltpu
from jax.experimental.pallas import tpu_sc as plsc

F32 = jnp.float32
BF16 = jnp.bfloat16

D_MODEL = 1024
DEPTH = 2
MIX_HALF = 512
HEADS = 4
HEAD_DIM = 128
GLA_DK = 64
GLA_CHUNK = 64
GLA_GATE_RANK = 16
GLA_GATE_TEMP = 16.0
LRU_C = 8.0
LRU_CONV = 4
S5_GROUP = 16
S5_GROUPS = 32
S5_STATE = 64
S5_LANES = S5_GROUPS * S5_STATE
S5_BLOCKS = 4
N_EXPERTS = 64
N_GROUPS = 8
GROUP_SIZE = N_EXPERTS // N_GROUPS
TOP_K = 8
TOPK_GROUPS = 4
D_EXPERT = 256
ROUTED_SCALE = 2.5
ALPHA = (2.0 * DEPTH) ** 0.25
EPS = 1e-5
LANES = 128
SUBLANES = 8
NEG_INF = float("-inf")

VMEM_LIMIT = 56 * 1024 * 1024

MLSTM_CHUNK = 128
MLSTM_TILE = 1024
LRU_TILE = 1024
LRU_LOG_STEPS = 3
LRU_UNROLL = 4
S5_TILE = 512
S5_LOG_STEPS = 3
S5_UNROLL = True
GLA_UNROLL = 4
GLA_TILE = 1024
PROJ_TILE = 512
OUT_TILE = 512
MOE_BLOCK = 1024
XS_SLOTS = 3
COMBINE_TILE = 256
COMBINE_PARTS = 2
PACKED = D_MODEL // 2
SC_CHUNK = 64
SC_CORES = 2
SC_SUBCORES = 16
SC_WORKERS = SC_CORES * SC_SUBCORES


def _params(*sem):
    return pltpu.CompilerParams(dimension_semantics=sem, vmem_limit_bytes=VMEM_LIMIT)


def _split3(x):
    hi = x.astype(BF16)
    r1 = x - hi.astype(F32)
    mid = r1.astype(BF16)
    lo = (r1 - mid.astype(F32)).astype(BF16)
    return hi, mid, lo


def _dot(a, b):
    return jnp.dot(a, b, preferred_element_type=F32)


def _dot_nt(a, b):
    return lax.dot_general(a, b, (((1,), (1,)), ((), ())), preferred_element_type=F32)


def _dot_tn(a, b):
    return lax.dot_general(a, b, (((0,), (0,)), ((), ())), preferred_element_type=F32)


def _exact_left01(mask01_bf16, x):
    hi, mid, lo = _split3(x)
    return _dot(mask01_bf16, hi) + _dot(mask01_bf16, mid) + _dot(mask01_bf16, lo)


def _exact_right01(x, mask01_bf16):
    hi, mid, lo = _split3(x)
    return _dot(hi, mask01_bf16) + _dot(mid, mask01_bf16) + _dot(lo, mask01_bf16)


def _log_sigmoid(x):
    return jnp.minimum(x, 0.0) - jnp.log(1.0 + jnp.exp(-jnp.abs(x)))


def _sigmoid(x):
    return 1.0 / (1.0 + jnp.exp(-x))


def _gelu_tanh(x):
    c = math.sqrt(2.0 / math.pi)
    return 0.5 * x * (1.0 + jnp.tanh(c * (x + 0.044715 * (x * x * x))))


def _layer_norm(z, w, b):
    mu = jnp.mean(z, axis=-1, keepdims=True)
    zc = z - mu
    return zc * lax.rsqrt(jnp.mean(zc * zc, axis=-1, keepdims=True) + EPS) * w + b


def _proj_kernel(x_ref, w_ref, wg_ref, o_ref, og_ref):
    x = x_ref[...].astype(BF16)
    o_ref[...] = _dot(x, w_ref[...])
    og_ref[...] = _dot(x, wg_ref[...])


def _proj(x, w_main, w_small):
    t, d = x.shape
    n = w_main.shape[1]
    tm = PROJ_TILE
    return pl.pallas_call(
        _proj_kernel,
        grid=(t // tm,),
        in_specs=[pl.BlockSpec((tm, d), lambda i: (i, 0)),
                  pl.BlockSpec((d, n), lambda i: (0, 0)),
                  pl.BlockSpec((d, LANES), lambda i: (0, 0))],
        out_specs=[pl.BlockSpec((tm, n), lambda i: (i, 0)),
                   pl.BlockSpec((tm, LANES), lambda i: (i, 0))],
        out_shape=[jax.ShapeDtypeStruct((t, n), F32), jax.ShapeDtypeStruct((t, LANES), F32)],
        compiler_params=_params("parallel"),
        name="in_proj",
    )(x, w_main, w_small)


def _mlstm_kernel(q_ref, k_ref, v_ref, o_ref, gc_ref, gr_ref, bc_ref, br_ref, nw_ref,
                  y_ref, c_ref, m_ref, *, chunk, n_chunks):
    L = chunk

    @pl.when(pl.program_id(1) == 0)
    def _():
        c_ref[...] = jnp.zeros_like(c_ref)
        m_ref[...] = jnp.zeros_like(m_ref)

    ri = lax.broadcasted_iota(jnp.int32, (L, L), 0)
    ci = lax.broadcasted_iota(jnp.int32, (L, L), 1)
    causal = ci <= ri
    tril = causal.astype(BF16)
    triu = (ri <= ci).astype(BF16)
    ones_v = jnp.ones((L, HEAD_DIM), BF16)
    scale = HEAD_DIM ** -0.5

    def body(c, carry):
        r0 = pl.multiple_of(c * L, L)
        g_col = gc_ref[pl.ds(r0, L), :] + bc_ref[...]
        g_row = gr_ref[c] + br_ref[...]
        b_col_all = _exact_left01(tril, _log_sigmoid(g_col))
        b_row_all = _exact_right01(_log_sigmoid(g_row), triu)
        for h in range(HEADS):
            lo = h * HEAD_DIM
            q = q_ref[pl.ds(r0, L), lo:lo + HEAD_DIM].astype(BF16)
            k = k_ref[pl.ds(r0, L), lo:lo + HEAD_DIM] * scale
            v = v_ref[pl.ds(r0, L), lo:lo + HEAD_DIM].astype(BF16)
            v_aug = jnp.concatenate([v, ones_v], axis=1)
            i_rep = jnp.broadcast_to(g_col[:, h:h + 1], (L, LANES))
            b_rep = jnp.broadcast_to(b_col_all[:, HEADS + h:HEADS + h + 1], (L, LANES))
            i_row = g_row[h:h + 1, :]
            b_row = b_row_all[HEADS + h:HEADS + h + 1, :]
            b_last = b_rep[L - 1:L, :]
            m_prev = m_ref[h:h + 1, :]
            c_prev = c_ref[h]

            d_mat = jnp.where(causal, b_rep - b_row + i_row, NEG_INF)
            m_inter = b_rep + m_prev
            m_i = jnp.maximum(m_inter, jnp.max(d_mat, axis=1, keepdims=True))
            s = _dot_nt(q, k.astype(BF16)) * jnp.exp(d_mat - m_i)
            w_inter = jnp.exp(m_inter - m_i)
            intra = _dot(s.astype(BF16), v_aug)
            inter = _dot(q, c_prev.astype(BF16))
            num = intra[:, :HEAD_DIM] + w_inter * inter[:, :HEAD_DIM]
            den = intra[:, HEAD_DIM:] + w_inter * inter[:, HEAD_DIM:]
            hh = num / jnp.maximum(jnp.abs(den), jnp.exp(-m_i))

            w_loc = b_last - b_rep + i_rep
            m_loc = jnp.max(w_loc, axis=0, keepdims=True)
            kp = (k * jnp.exp(w_loc - m_loc)).astype(BF16)
            c_loc = _dot_tn(kp, v_aug)
            m_new = jnp.maximum(b_last + m_prev, m_loc)
            keep = jnp.exp(b_last + m_prev - m_new)
            add = jnp.exp(m_loc - m_new)
            c_ref[h] = (jnp.concatenate([keep, keep], axis=1) * c_prev
                        + jnp.concatenate([add, add], axis=1) * c_loc)
            m_ref[h:h + 1, :] = m_new

            hc = hh - jnp.mean(hh, axis=-1, keepdims=True)
            yn = hc * lax.rsqrt(jnp.mean(hc * hc, axis=-1, keepdims=True) + EPS)
            og = o_ref[pl.ds(r0, L), lo:lo + HEAD_DIM]
            y_ref[pl.ds(r0, L), lo:lo + HEAD_DIM] = yn * nw_ref[:, lo:lo + HEAD_DIM] * _sigmoid(og)
        return carry

    lax.fori_loop(0, n_chunks, body, 0)


def _mlstm(proj, gates, gate_b, norm_w, batch, seq):
    t = batch * seq
    L = MLSTM_CHUNK
    assert L == LANES, "the kernel keeps per-row gate terms replicated over one vreg of lanes"
    ts = MLSTM_TILE
    nj = seq // ts
    nc = ts // L
    g_row = gates[:, :2 * HEADS].reshape(t // L, L, 2 * HEADS).transpose(0, 2, 1)
    b_col = jnp.zeros((1, LANES), F32).at[0, :2 * HEADS].set(gate_b)
    b_row = gate_b.reshape(2 * HEADS, 1)
    blk = lambda col: pl.BlockSpec((ts, MIX_HALF), lambda b, j, col=col: (b * nj + j, col))
    kern = functools.partial(_mlstm_kernel, chunk=L, n_chunks=nc)
    return pl.pallas_call(
        kern,
        grid=(batch, nj),
        in_specs=[blk(0), blk(1), blk(2), blk(3),
                  pl.BlockSpec((ts, LANES), lambda b, j: (b * nj + j, 0)),
                  pl.BlockSpec((nc, 2 * HEADS, L), lambda b, j: (b * nj + j, 0, 0)),
                  pl.BlockSpec((1, LANES), lambda b, j: (0, 0)),
                  pl.BlockSpec((2 * HEADS, 1), lambda b, j: (0, 0)),
                  pl.BlockSpec((1, MIX_HALF), lambda b, j: (0, 0))],
        out_specs=pl.BlockSpec((ts, MIX_HALF), lambda b, j: (b * nj + j, 0)),
        out_shape=jax.ShapeDtypeStruct((t, MIX_HALF), F32),
        scratch_shapes=[pltpu.VMEM((HEADS, HEAD_DIM, 2 * HEAD_DIM), F32),
                        pltpu.VMEM((8, LANES), F32)],
        compiler_params=_params("arbitrary", "arbitrary"),
        name="mlstm",
    )(proj, proj, proj, proj, gates, g_row, b_col, b_row, norm_w.reshape(1, MIX_HALF))


def _rglru_kernel(xb_ref, gb_ref, cw_ref, cb_ref, wa_ref, ba_ref, wx_ref, bx_ref, lam_ref,
                  y_ref, xext_ref, h_ref, a_ref, u_ref, *, tile):
    @pl.when(pl.program_id(1) == 0)
    def _():
        xext_ref[0:8, :] = jnp.zeros((8, MIX_HALF), F32)
        h_ref[...] = jnp.zeros_like(h_ref)

    x = xb_ref[...]
    xext_ref[8:8 + tile, :] = x
    xc = cb_ref[...] + cw_ref[LRU_CONV - 1:LRU_CONV, :] * x
    for tap in range(LRU_CONV - 1):
        back = LRU_CONV - 1 - tap
        xc = xc + cw_ref[tap:tap + 1, :] * xext_ref[8 - back:8 - back + tile, :]
    xext_ref[0:8, :] = x[tile - 8:tile, :]

    xc16 = xc.astype(BF16)
    r_parts, i_parts = [], []
    for h in range(HEADS):
        lo = h * HEAD_DIM
        xh = xc16[:, lo:lo + HEAD_DIM]
        r_parts.append(_dot(xh, wa_ref[h]))
        i_parts.append(_dot(xh, wx_ref[h]))
    r = _sigmoid(jnp.concatenate(r_parts, axis=1) + ba_ref[...])
    ig = _sigmoid(jnp.concatenate(i_parts, axis=1) + bx_ref[...])
    lam = lam_ref[...]
    softplus_neg = jnp.maximum(-lam, 0.0) + jnp.log(1.0 + jnp.exp(-jnp.abs(lam)))
    log_a = -LRU_C * r * softplus_neg
    a = jnp.exp(log_a)
    th = jnp.tanh(log_a)
    u = jnp.sqrt(-2.0 * th / (1.0 - th)) * ig * xc

    a_ref[...] = a
    u_ref[...] = u
    rows = lax.broadcasted_iota(jnp.int32, (SUBLANES, MIX_HALF), 0)

    def group(i, h_prev):
        r0 = pl.multiple_of(i * SUBLANES, SUBLANES)
        ag = a_ref[pl.ds(r0, SUBLANES), :]
        ug = u_ref[pl.ds(r0, SUBLANES), :]
        for k in range(LRU_LOG_STEPS):
            keep = rows >= (1 << k)
            ug = ag * jnp.where(keep, pltpu.roll(ug, 1 << k, 0), 0.0) + ug
            ag = ag * jnp.where(keep, pltpu.roll(ag, 1 << k, 0), 1.0)
        hg = ug + ag * h_prev
        u_ref[pl.ds(r0, SUBLANES), :] = hg
        return hg[SUBLANES - 1:SUBLANES, :]

    h_last = lax.fori_loop(0, tile // SUBLANES, group, h_ref[0:1, :], unroll=LRU_UNROLL)
    h_ref[...] = jnp.broadcast_to(h_last, h_ref.shape)
    y_ref[...] = u_ref[...] * _gelu_tanh(gb_ref[...])


def _rglru(proj, conv_w, conv_b, wa, ba, wx, bx, lam, batch, seq):
    t = batch * seq
    ts = LRU_TILE
    nj = seq // ts
    row = lambda a: a.reshape(1, MIX_HALF)
    const2 = lambda shape: pl.BlockSpec(shape, lambda b, j: (0, 0))
    const3 = lambda shape: pl.BlockSpec(shape, lambda b, j: (0, 0, 0))
    blk = lambda col: pl.BlockSpec((ts, MIX_HALF), lambda b, j, col=col: (b * nj + j, col))
    return pl.pallas_call(
        functools.partial(_rglru_kernel, tile=ts),
        grid=(batch, nj),
        in_specs=[blk(4), blk(5), const2((LRU_CONV, MIX_HALF)), const2((1, MIX_HALF)),
                  const3((HEADS, HEAD_DIM, HEAD_DIM)), const2((1, MIX_HALF)),
                  const3((HEADS, HEAD_DIM, HEAD_DIM)), const2((1, MIX_HALF)), const2((1, MIX_HALF))],
        out_specs=pl.BlockSpec((ts, MIX_HALF), lambda b, j: (b * nj + j, 0)),
        out_shape=jax.ShapeDtypeStruct((t, MIX_HALF), F32),
        scratch_shapes=[pltpu.VMEM((ts + 8, MIX_HALF), F32), pltpu.VMEM((8, MIX_HALF), F32),
                        pltpu.VMEM((ts, MIX_HALF), F32), pltpu.VMEM((ts, MIX_HALF), F32)],
        compiler_params=_params("arbitrary", "arbitrary"),
        name="rglru",
    )(proj, proj, conv_w, row(conv_b), wa.astype(BF16), row(ba), wx.astype(BF16), row(bx), row(lam))


def _s5_kernel(u_ref, bre_ref, bim_ref, cre_ref, cim_ref, mre_ref, mim_ref, pre_ref, pim_ref, d_ref, gw_ref,
               gb_ref, y_ref, xr_ref, xi_ref, cr_ref, ci_ref, *, tile):
    @pl.when(pl.program_id(1) == 0)
    def _():
        cr_ref[...] = jnp.zeros_like(cr_ref)
        ci_ref[...] = jnp.zeros_like(ci_ref)

    u = u_ref[...]
    u16 = u.astype(BF16)
    blk_c = MIX_HALF // S5_BLOCKS
    blk_s = S5_LANES // S5_BLOCKS
    parts = []
    for j in range(S5_BLOCKS):
        lanes = slice(j * blk_s, (j + 1) * blk_s)
        uj = u16[:, j * blk_c:(j + 1) * blk_c]
        xr_ref[:, lanes] = _dot(uj, bre_ref[j])
        xi_ref[:, lanes] = _dot(uj, bim_ref[j])

        def group(i, carry, lanes=lanes):
            cr, ci = carry
            r0 = pl.multiple_of(i * SUBLANES, SUBLANES)
            xr = xr_ref[pl.ds(r0, SUBLANES), lanes]
            xi = xi_ref[pl.ds(r0, SUBLANES), lanes]
            for k in range(S5_LOG_STEPS):
                sr = pltpu.roll(xr, 1 << k, 0)
                si = pltpu.roll(xi, 1 << k, 0)
                mr = mre_ref[k, :, lanes]
                mi = mim_ref[k, :, lanes]
                xr, xi = xr + mr * sr - mi * si, xi + mr * si + mi * sr
            pr = pre_ref[:, lanes]
            pi = pim_ref[:, lanes]
            xr, xi = xr + pr * cr - pi * ci, xi + pr * ci + pi * cr
            xr_ref[pl.ds(r0, SUBLANES), lanes] = xr
            xi_ref[pl.ds(r0, SUBLANES), lanes] = xi
            return xr[SUBLANES - 1:SUBLANES, :], xi[SUBLANES - 1:SUBLANES, :]

        cr, ci = lax.fori_loop(0, tile // SUBLANES, group, (cr_ref[0:1, lanes], ci_ref[0:1, lanes]),
                               unroll=S5_UNROLL)
        cr_ref[0:1, lanes] = cr
        ci_ref[0:1, lanes] = ci
        parts.append(_dot(xr_ref[:, lanes].astype(BF16), cre_ref[j])
                     - _dot(xi_ref[:, lanes].astype(BF16), cim_ref[j]))
    y = jnp.concatenate(parts, axis=1) + d_ref[...] * u
    g = _gelu_tanh(y)
    y_ref[...] = g * _sigmoid(_dot(g.astype(BF16), gw_ref[...]) + gb_ref[...])


def _s5_tables(lam_re, lam_im, b_re, b_im, c_re, c_im, log_dt):
    lr, li = lam_re.astype(F32), lam_im.astype(F32)
    dt = jnp.exp(log_dt.astype(F32))[:, None]
    mag = jnp.exp(lr * dt)
    abar_re = mag * jnp.cos(li * dt)
    abar_im = mag * jnp.sin(li * dt)
    den = lr * lr + li * li
    nr = abar_re - 1.0
    coef_re = (nr * lr + abar_im * li) / den
    coef_im = (abar_im * lr - nr * li) / den
    bbar_re = coef_re[..., None] * b_re - coef_im[..., None] * b_im
    bbar_im = coef_re[..., None] * b_im + coef_im[..., None] * b_re
    gpb = S5_GROUPS // S5_BLOCKS
    eye = jnp.eye(gpb, dtype=F32)

    def in_map(bb):
        bb = bb.reshape(S5_BLOCKS, gpb, S5_STATE, S5_GROUP)
        return jnp.einsum("jgph,gk->jghkp", bb, eye).reshape(S5_BLOCKS, gpb * S5_GROUP, gpb * S5_STATE)

    def out_map(cc):
        cc = cc.reshape(S5_BLOCKS, gpb, S5_GROUP, S5_STATE)
        return jnp.einsum("jghp,gk->jgpkh", cc, eye).reshape(S5_BLOCKS, gpb * S5_STATE, gpb * S5_GROUP)

    def power(n):
        n = jnp.asarray(n, F32)[..., None, None]
        pmag = jnp.exp(n * (lr * dt))
        shape = n.shape[:-2] + (S5_LANES,)
        return (pmag * jnp.cos(n * (li * dt))).reshape(shape), (pmag * jnp.sin(n * (li * dt))).reshape(shape)

    row = jnp.arange(SUBLANES)
    step = 2 ** jnp.arange(S5_LOG_STEPS)
    s_re, s_im = power(step)
    keep = (row[None, :] >= step[:, None])[..., None]
    m_re = jnp.where(keep, s_re[:, None, :], 0.0)
    m_im = jnp.where(keep, s_im[:, None, :], 0.0)
    p_re, p_im = power(row + 1)
    return (in_map(bbar_re).astype(BF16), in_map(bbar_im).astype(BF16),
            out_map(c_re.astype(F32)).astype(BF16), out_map(c_im.astype(F32)).astype(BF16),
            m_re, m_im, p_re, p_im)


def _s5(proj, tables, d_skip, glu_w, glu_b, batch, seq):
    t = batch * seq
    ts = S5_TILE
    nj = seq // ts
    bre, bim, cre, cim, m_re, m_im, p_re, p_im = tables
    blk_c = MIX_HALF // S5_BLOCKS
    blk_s = S5_LANES // S5_BLOCKS
    const2 = lambda shape: pl.BlockSpec(shape, lambda b, j: (0, 0))
    const3 = lambda shape: pl.BlockSpec(shape, lambda b, j: (0, 0, 0))
    return pl.pallas_call(
        functools.partial(_s5_kernel, tile=ts),
        grid=(batch, nj),
        in_specs=[pl.BlockSpec((ts, MIX_HALF), lambda b, j: (b * nj + j, 0)),
                  const3((S5_BLOCKS, blk_c, blk_s)), const3((S5_BLOCKS, blk_c, blk_s)),
                  const3((S5_BLOCKS, blk_s, blk_c)), const3((S5_BLOCKS, blk_s, blk_c)),
                  const3(m_re.shape), const3(m_im.shape), const2(p_re.shape), const2(p_im.shape),
                  const2((1, MIX_HALF)), const2((MIX_HALF, MIX_HALF)), const2((1, MIX_HALF))],
        out_specs=pl.BlockSpec((ts, MIX_HALF), lambda b, j: (b * nj + j, 0)),
        out_shape=jax.ShapeDtypeStruct((t, MIX_HALF), F32),
        scratch_shapes=[pltpu.VMEM((ts, S5_LANES), F32), pltpu.VMEM((ts, S5_LANES), F32),
                        pltpu.VMEM((8, S5_LANES), F32), pltpu.VMEM((8, S5_LANES), F32)],
        compiler_params=_params("arbitrary", "arbitrary"),
        name="s5",
    )(proj, bre, bim, cre, cim, m_re, m_im, p_re, p_im, d_skip.reshape(1, MIX_HALF), glu_w.astype(BF16),
      glu_b.reshape(1, MIX_HALF))


def _gla_kernel(q_ref, k_ref, v_ref, r_ref, gl_ref, gw_ref, gb_ref, nw_ref, y_ref,
                st_ref, qd_ref, ki_ref, ke_ref, v16_ref, dec_ref, o_ref, *, tile, chunk):
    L = chunk
    nc = tile // L

    @pl.when(pl.program_id(1) == 0)
    def _():
        st_ref[...] = jnp.zeros_like(st_ref)

    z = _dot(gl_ref[...].astype(BF16), gw_ref[...]) + gb_ref[...]
    bcum = _log_sigmoid(z) * (1.0 / GLA_GATE_TEMP)
    row_in_chunk = lax.broadcasted_iota(jnp.int32, bcum.shape, 0) & (L - 1)
    s = 1
    while s < L:
        bcum = bcum + jnp.where(row_in_chunk >= s, pltpu.roll(bcum, s, 0), 0.0)
        s *= 2
    b3 = bcum.reshape(nc, L, MIX_HALF)
    b_last = b3[:, L - 1:L, :]
    k = k_ref[...]
    qd_ref[...] = (q_ref[...] * (GLA_DK ** -0.5) * jnp.exp(bcum)).astype(BF16)
    ki_ref[...] = (k * jnp.exp(-bcum)).astype(BF16)
    ke_ref[...] = (k.reshape(nc, L, MIX_HALF) * jnp.exp(b_last - b3)).reshape(tile, MIX_HALF).astype(BF16)
    v16_ref[...] = v_ref[...].astype(BF16)
    dec_ref[...] = jnp.exp(b_last)

    ri = lax.broadcasted_iota(jnp.int32, (L, L), 0)
    ci = lax.broadcasted_iota(jnp.int32, (L, L), 1)
    causal = ci <= ri

    def body(c, carry):
        r0 = pl.multiple_of(c * L, L)
        dec = dec_ref[c]
        for h in range(HEADS):
            lo = h * HEAD_DIM
            q_dec = qd_ref[pl.ds(r0, L), lo:lo + HEAD_DIM]
            v = v16_ref[pl.ds(r0, L), lo:lo + HEAD_DIM]
            st = st_ref[h]
            att = jnp.where(causal, _dot_nt(q_dec, ki_ref[pl.ds(r0, L), lo:lo + HEAD_DIM]), 0.0)
            o_ref[pl.ds(r0, L), lo:lo + HEAD_DIM] = (_dot(att.astype(BF16), v)
                                                     + _dot_nt(q_dec, st.astype(BF16)))
            st_ref[h] = dec[:, lo:lo + HEAD_DIM] * st + _dot_tn(v, ke_ref[pl.ds(r0, L), lo:lo + HEAD_DIM])
        return carry

    lax.fori_loop(0, nc, body, 0, unroll=GLA_UNROLL)

    rg = r_ref[...]
    gate = nw_ref[...] * (rg * _sigmoid(rg))
    for h in range(HEADS):
        lo = h * HEAD_DIM
        o = o_ref[:, lo:lo + HEAD_DIM]
        yn = o * lax.rsqrt(jnp.mean(o * o, axis=-1, keepdims=True) + EPS)
        y_ref[:, lo:lo + HEAD_DIM] = yn * gate[:, lo:lo + HEAD_DIM]


def _gla(proj, glow, gate_w, gate_b, norm_w, batch, seq):
    t = batch * seq
    ts = GLA_TILE
    nj = seq // ts
    blk = lambda col: pl.BlockSpec((ts, MIX_HALF), lambda b, j, col=col: (b * nj + j, col))
    const2 = lambda shape: pl.BlockSpec(shape, lambda b, j: (0, 0))
    return pl.pallas_call(
        functools.partial(_gla_kernel, tile=ts, chunk=GLA_CHUNK),
        grid=(batch, nj),
        in_specs=[blk(1), blk(2), blk(3), blk(4),
                  pl.BlockSpec((ts, LANES), lambda b, j: (b * nj + j, 0)),
                  const2((LANES, MIX_HALF)), const2((1, MIX_HALF)), const2((1, MIX_HALF))],
        out_specs=pl.BlockSpec((ts, MIX_HALF), lambda b, j: (b * nj + j, 0)),
        out_shape=jax.ShapeDtypeStruct((t, MIX_HALF), F32),
        scratch_shapes=[pltpu.VMEM((HEADS, HEAD_DIM, HEAD_DIM), F32),
                        pltpu.VMEM((ts, MIX_HALF), BF16), pltpu.VMEM((ts, MIX_HALF), BF16),
                        pltpu.VMEM((ts, MIX_HALF), BF16), pltpu.VMEM((ts, MIX_HALF), BF16),
                        pltpu.VMEM((ts // GLA_CHUNK, 1, MIX_HALF), F32),
                        pltpu.VMEM((ts, MIX_HALF), F32)],
        compiler_params=_params("arbitrary", "arbitrary"),
        name="gla",
    )(proj, proj, proj, proj, glow, gate_w, gate_b, norm_w.reshape(1, MIX_HALF))


def _pad_heads(w, axis):
    shape = list(w.shape)
    shape[axis:axis + 1] = [HEADS, GLA_DK]
    w = w.reshape(shape)
    pad = [(0, 0)] * w.ndim
    pad[axis + 1] = (0, HEAD_DIM - GLA_DK)
    w = jnp.pad(w, pad)
    shape[axis:axis + 2] = [HEADS * HEAD_DIM]
    return w.reshape(shape)


def _pack_bf16_pairs(z):
    hi = lax.bitcast_convert_type(z[:, :PACKED].astype(BF16).astype(F32), jnp.uint32)
    lo = lax.bitcast_convert_type(z[:, PACKED:].astype(BF16).astype(F32), jnp.uint32)
    word = (hi & jnp.uint32(0xFFFF0000)) | lax.shift_right_logical(lo, jnp.uint32(16))
    return lax.bitcast_convert_type(word, jnp.int32)


def _unpack_bf16_pairs(p):
    word = lax.bitcast_convert_type(p, jnp.uint32)
    hi = lax.bitcast_convert_type(word & jnp.uint32(0xFFFF0000), F32)
    lo = lax.bitcast_convert_type(lax.shift_left(word, jnp.uint32(16)), F32)
    return hi, lo


def _out_kernel(ya_ref, yb_ref, h_ref, w_ref, lw_ref, lb_ref, rw_ref, rb_ref,
                o_ref, opk_ref, idx_ref, rank_ref, wk_ref, cnt_ref, base_ref, *, tile):
    mixed = jnp.concatenate([ya_ref[...], yb_ref[...]], axis=1).astype(BF16)
    z = ALPHA * h_ref[...] + _dot(mixed, w_ref[...])
    out = _layer_norm(z, lw_ref[...], lb_ref[...])
    o_ref[...] = out
    opk_ref[...] = _pack_bf16_pairs(out)
    _route_tile(out, rw_ref, rb_ref, idx_ref, rank_ref, wk_ref, cnt_ref, base_ref, tile)


def _out_proj_ln_route(ya, yb, h, w_out, ln_w, ln_b, router_w, router_bias):
    t = h.shape[0]
    tm = OUT_TILE
    const = lambda shape: pl.BlockSpec(shape, lambda i: (0, 0))
    per_tok = lambda dt: jax.ShapeDtypeStruct((TOP_K, t), dt)
    tok_blk = pl.BlockSpec((TOP_K, tm), lambda i: (0, i))
    return pl.pallas_call(
        functools.partial(_out_kernel, tile=tm),
        grid=(t // tm,),
        in_specs=[pl.BlockSpec((tm, MIX_HALF), lambda i: (i, 0)),
                  pl.BlockSpec((tm, MIX_HALF), lambda i: (i, 0)),
                  pl.BlockSpec((tm, D_MODEL), lambda i: (i, 0)),
                  const((D_MODEL, D_MODEL)), const((1, D_MODEL)), const((1, D_MODEL)),
                  const((N_EXPERTS, D_MODEL)), const((N_EXPERTS, 1))],
        out_specs=[pl.BlockSpec((tm, D_MODEL), lambda i: (i, 0)),
                   pl.BlockSpec((tm, PACKED), lambda i: (i, 0)),
                   tok_blk, tok_blk, pl.BlockSpec((tm, TOP_K), lambda i: (i, 0)), const((N_EXPERTS, LANES))],
        out_shape=[jax.ShapeDtypeStruct((t, D_MODEL), F32), jax.ShapeDtypeStruct((t, PACKED), jnp.int32),
                   per_tok(jnp.int32), per_tok(jnp.int32), jax.ShapeDtypeStruct((t, TOP_K), F32),
                   jax.ShapeDtypeStruct((N_EXPERTS, LANES), F32)],
        scratch_shapes=[pltpu.VMEM((N_EXPERTS, LANES), F32)],
        compiler_params=_params("arbitrary"),
        name="out_proj_ln_route",
    )(ya, yb, h, w_out.astype(BF16), ln_w.reshape(1, D_MODEL), ln_b.reshape(1, D_MODEL),
      router_w.T, router_bias.reshape(N_EXPERTS, 1))


def _first_index(hit, idx, big):
    return jnp.min(jnp.where(hit, idx, big), axis=0, keepdims=True)


def _route_tile(h, w_ref, b_ref, idx_ref, rank_ref, wk_ref, cnt_ref, base_ref, tile):
    @pl.when(pl.program_id(0) == 0)
    def _():
        base_ref[...] = jnp.zeros_like(base_ref)

    h_hi, h_mid, _ = _split3(h)
    w_hi, w_mid, _ = _split3(w_ref[...])
    logits = _dot_nt(w_hi, h_hi) + _dot_nt(w_hi, h_mid) + _dot_nt(w_mid, h_hi)
    scores = _sigmoid(logits)
    biased = scores + b_ref[...]

    sub = lax.broadcasted_iota(jnp.int32, (GROUP_SIZE, tile), 0)
    grp_rows = []
    for g in range(N_GROUPS):
        xg = biased[g * GROUP_SIZE:(g + 1) * GROUP_SIZE, :]
        m1 = jnp.max(xg, axis=0, keepdims=True)
        i1 = _first_index(xg == m1, sub, GROUP_SIZE)
        m2 = jnp.max(jnp.where(sub == i1, NEG_INF, xg), axis=0, keepdims=True)
        grp_rows.append(m1 + m2)
    gs = jnp.concatenate(grp_rows, axis=0)
    gsel = jnp.zeros((N_GROUPS, tile), F32)
    for _ in range(TOPK_GROUPS):
        mx = jnp.max(gs, axis=0, keepdims=True)
        hit = sub == _first_index(gs == mx, sub, N_GROUPS)
        gsel = jnp.where(hit, 1.0, gsel)
        gs = jnp.where(hit, NEG_INF, gs)
    emask = jnp.concatenate(
        [jnp.broadcast_to(gsel[g:g + 1, :], (GROUP_SIZE, tile)) for g in range(N_GROUPS)], axis=0)

    eidx = lax.broadcasted_iota(jnp.int32, (N_EXPERTS, tile), 0)
    cand = jnp.where(emask > 0.5, biased, NEG_INF)
    sel = jnp.zeros((N_EXPERTS, tile), F32)
    picks = []
    for _ in range(TOP_K):
        mx = jnp.max(cand, axis=0, keepdims=True)
        first = _first_index(cand == mx, eidx, N_EXPERTS)
        hit = eidx == first
        picks.append(first)
        sel = jnp.where(hit, 1.0, sel)
        cand = jnp.where(hit, NEG_INF, cand)
    picked = jnp.where(sel > 0.5, scores, 0.0)
    wts = picked / jnp.sum(picked, axis=0, keepdims=True) * ROUTED_SCALE

    ri = lax.broadcasted_iota(jnp.int32, (tile, tile), 0)
    ci = lax.broadcasted_iota(jnp.int32, (tile, tile), 1)
    before = (ri < ci).astype(BF16)
    prior = _dot(sel.astype(BF16), before) + base_ref[:, 0:1]
    ranks = [jnp.sum(jnp.where(eidx == p, prior, 0.0), axis=0, keepdims=True) for p in picks]
    wsel = [jnp.sum(jnp.where(eidx == p, wts, 0.0), axis=0, keepdims=True) for p in picks]
    idx_ref[...] = jnp.concatenate(picks, axis=0)
    rank_ref[...] = jnp.concatenate(ranks, axis=0).astype(jnp.int32)
    wk_ref[...] = jnp.concatenate(wsel, axis=0).T
    total = base_ref[...] + jnp.sum(sel, axis=1, keepdims=True)
    base_ref[...] = total
    cnt_ref[...] = total


def _silu(x):
    return x * _sigmoid(x)


def _sc_mesh():
    return plsc.VectorSubcoreMesh(core_axis_name="c", subcore_axis_name="s")


def _sc_worker_id():
    return lax.axis_index("s") * SC_CORES + lax.axis_index("c")


def _dispatch_rows(xpk, pos_chunks, n_rows):
    t = xpk.shape[0]
    n_ch = t // SC_WORKERS // SC_CHUNK

    @functools.partial(
        pl.kernel, mesh=_sc_mesh(),
        out_type=jax.ShapeDtypeStruct((n_rows, PACKED), jnp.int32),
        scratch_types=[pltpu.VMEM((TOP_K, SC_CHUNK), jnp.int32),
                       pltpu.VMEM((SC_CHUNK, PACKED), jnp.int32),
                       pltpu.SemaphoreType.DMA],
        name="moe_dispatch",
    )
    def scatter(x_hbm, pos_hbm, out_hbm, idx_v, rows_v, sem):
        wid = _sc_worker_id()

        @pl.loop(0, n_ch)
        def _(c):
            chunk = wid * n_ch + c
            off = pl.multiple_of(chunk * SC_CHUNK, SC_CHUNK)
            pltpu.sync_copy(pos_hbm.at[chunk], idx_v)
            pltpu.sync_copy(x_hbm.at[pl.ds(off, SC_CHUNK)], rows_v)
            copies = [pltpu.async_copy(rows_v, out_hbm.at[idx_v.at[k]], sem) for k in range(TOP_K)]
            for cp in copies:
                cp.wait()

    return scatter(xpk, pos_chunks)


def _gather_rows(table, idx):
    n = idx.shape[0]
    per_w = n // SC_WORKERS
    n_ch = per_w // SC_CHUNK
    assert n_ch % 2 == 0 and n_ch >= 2

    @functools.partial(
        pl.kernel, mesh=_sc_mesh(),
        out_type=jax.ShapeDtypeStruct((n, PACKED), jnp.int32),
        scratch_types=[pltpu.VMEM((n_ch, SC_CHUNK), jnp.int32),
                       pltpu.VMEM((SC_CHUNK, PACKED), jnp.int32), pltpu.VMEM((SC_CHUNK, PACKED), jnp.int32),
                       pltpu.SemaphoreType.DMA, pltpu.SemaphoreType.DMA,
                       pltpu.SemaphoreType.DMA, pltpu.SemaphoreType.DMA],
        name="moe_gather",
    )
    def gather(table_hbm, idx_hbm, out_hbm, idx_v, rows0, rows1, g0, g1, w0, w1):
        wid = _sc_worker_id()
        base = wid * per_w
        rows, g_sem, w_sem = (rows0, rows1), (g0, g1), (w0, w1)
        pltpu.sync_copy(idx_hbm.at[wid], idx_v)

        def fetch(c, b):
            return pltpu.make_async_copy(table_hbm.at[idx_v.at[c]], rows[b], g_sem[b])

        def flush(c, b):
            off = pl.multiple_of(base + c * SC_CHUNK, SC_CHUNK)
            return pltpu.make_async_copy(rows[b], out_hbm.at[pl.ds(off, SC_CHUNK)], w_sem[b])

        fetch(0, 0).start()

        @pl.loop(0, n_ch, step=2)
        def _(c0):
            for b in range(2):
                c = c0 + b
                fetch(c, b).wait()
                flush(c, b).start()

                @pl.when(c + 1 < n_ch)
                def _():
                    @pl.when(c >= 1)
                    def _():
                        flush(c - 1, 1 - b).wait()
                    fetch(c + 1, 1 - b).start()

        flush(n_ch - 2, 0).wait()
        flush(n_ch - 1, 1).wait()

    return gather(table, idx.reshape(SC_WORKERS, n_ch, SC_CHUNK))


def _unpacked_bf16(p):
    hi, lo = _unpack_bf16_pairs(p)
    return jnp.concatenate([hi.astype(BF16), lo.astype(BF16)], axis=1)


def _expert_kernel(be_ref, nu_ref, next_ref, slot_ref, xs_hbm, wg_hbm, wu_hbm, wd_hbm, y_ref,
                   g16_ref, u16_ref, d16_ref, gf_ref, uf_ref, df_ref, xbuf_ref, xsem, wsem, *, layer):
    i = pl.program_id(0)
    n_used = nu_ref[0]
    bm = xbuf_ref.shape[1]

    def fetch(b):
        slot = lax.rem(b, XS_SLOTS)
        rows = pl.ds(pl.multiple_of(b * bm, bm), bm)
        return pltpu.make_async_copy(xs_hbm.at[rows], xbuf_ref.at[slot], xsem.at[slot])

    @pl.when(i == 0)
    def _():
        fetch(0).start()

        @pl.when(n_used > 1)
        def _():
            fetch(1).start()

    @pl.when(i + 2 < n_used)
    def _():
        fetch(i + 2).start()

    e = be_ref[i]
    slot = slot_ref[e]
    f32_bufs = (gf_ref, uf_ref, df_ref)

    def wfetch(expert, dst_slot):
        return [pltpu.make_async_copy(w_hbm.at[layer, expert], buf.at[dst_slot], wsem.at[dst_slot, j])
                for j, (w_hbm, buf) in enumerate(zip((wg_hbm, wu_hbm, wd_hbm), f32_bufs))]

    @pl.when(i == 0)
    def _():
        for cp in wfetch(e, slot):
            cp.start()

    first_block_of_expert = jnp.logical_or(i == 0, e != be_ref[jnp.maximum(i - 1, 0)])

    @pl.when(jnp.logical_and(first_block_of_expert, i < n_used))
    def _():
        for cp in wfetch(e, slot):
            cp.wait()
        g16_ref[slot] = gf_ref[slot].astype(BF16)
        u16_ref[slot] = uf_ref[slot].astype(BF16)
        d16_ref[slot] = df_ref[slot].astype(BF16)
        nxt = next_ref[e]

        @pl.when(nxt >= 0)
        def _():
            for cp in wfetch(nxt, 1 - slot):
                cp.start()

    @pl.when(i < n_used)
    def _():
        fetch(i).wait()
        x = _unpacked_bf16(xbuf_ref[lax.rem(i, XS_SLOTS)])
        hh = _silu(_dot(x, g16_ref[slot])) * _dot(x, u16_ref[slot])
        y_ref[...] = _pack_bf16_pairs(_dot(hh.astype(BF16), d16_ref[slot]))


def _experts(block_e, n_used, next_expert, weight_slot, xs, wg, wu, wd, layer):
    nb = block_e.shape[0]
    bm = MOE_BLOCK
    hbm = pl.BlockSpec(memory_space=pl.ANY)
    two = lambda shape, dt: pltpu.VMEM((2,) + shape, dt)
    grid_spec = pltpu.PrefetchScalarGridSpec(
        num_scalar_prefetch=4,
        grid=(nb,),
        in_specs=[hbm, hbm, hbm, hbm],
        out_specs=pl.BlockSpec((bm, PACKED), lambda i, be, nu, nx, sl: (jnp.minimum(i, nu[0] - 1), 0)),
        scratch_shapes=[two((D_MODEL, D_EXPERT), BF16), two((D_MODEL, D_EXPERT), BF16), two((D_EXPERT, D_MODEL), BF16),
                        two((D_MODEL, D_EXPERT), F32), two((D_MODEL, D_EXPERT), F32), two((D_EXPERT, D_MODEL), F32),
                        pltpu.VMEM((XS_SLOTS, bm, PACKED), jnp.int32), pltpu.SemaphoreType.DMA((XS_SLOTS,)),
                        pltpu.SemaphoreType.DMA((2, 3))],
    )
    return pl.pallas_call(
        functools.partial(_expert_kernel, layer=layer),
        grid_spec=grid_spec,
        out_shape=jax.ShapeDtypeStruct((nb * bm, PACKED), jnp.int32),
        compiler_params=_params("arbitrary"),
        name="moe_experts",
    )(block_e, n_used, next_expert, weight_slot, xs, wg, wu, wd)


def _combine_kernel(g_ref, wk_ref, h_ref, xpk_ref, sg_ref, su_ref, sd_ref, lw_ref, lb_ref, *rest, with_proj):
    if with_proj:
        (wm_ref, ws_ref), (o_ref, proj_ref, small_ref) = rest[:2], rest[-3:]
        next_proj = (wm_ref, ws_ref, proj_ref, small_ref)
    else:
        o_ref, next_proj = rest[-1], None
    x = _unpacked_bf16(xpk_ref[...])
    hs = _silu(_dot(x, sg_ref[...])) * _dot(x, su_ref[...])
    shared = _dot(hs.astype(BF16), sd_ref[...])
    acc_hi = shared[:, :PACKED]
    acc_lo = shared[:, PACKED:]
    wk = wk_ref[...]
    for k in range(TOP_K):
        y_hi, y_lo = _unpack_bf16_pairs(g_ref[k])
        w = wk[:, k:k + 1]
        acc_hi = acc_hi + w * y_hi
        acc_lo = acc_lo + w * y_lo
    ffn = jnp.concatenate([acc_hi, acc_lo], axis=1)
    out = _layer_norm(ALPHA * h_ref[...] + ffn, lw_ref[...], lb_ref[...])
    o_ref[...] = out
    if next_proj is not None:
        wm_ref, ws_ref, proj_ref, small_ref = next_proj
        out16 = out.astype(BF16)
        proj_ref[...] = _dot(out16, wm_ref[...])
        small_ref[...] = _dot(out16, ws_ref[...])


def _combine_ln(g, wk, h, xpk, sg, su, sd, ln_w, ln_b, next_w, part, prev):
    t = h.shape[0]
    tm = COMBINE_TILE
    n_blk = g.shape[1] // tm
    first = part * n_blk
    const = lambda shape: pl.BlockSpec(shape, lambda i: (0, 0))
    rows = lambda width: pl.BlockSpec((tm, width), lambda i: (i + first, 0))
    in_specs = [pl.BlockSpec((TOP_K, tm, PACKED), lambda i: (0, i, 0)), rows(TOP_K), rows(D_MODEL), rows(PACKED),
                const((D_MODEL, D_EXPERT)), const((D_MODEL, D_EXPERT)), const((D_EXPERT, D_MODEL)),
                const((1, D_MODEL)), const((1, D_MODEL))]
    args = [g, wk, h, xpk, sg.astype(BF16), su.astype(BF16), sd.astype(BF16),
            ln_w.reshape(1, D_MODEL), ln_b.reshape(1, D_MODEL)]
    out_specs = [rows(D_MODEL)]
    out_shape = [jax.ShapeDtypeStruct((t, D_MODEL), F32)]
    if next_w is not None:
        w_main, w_small = next_w
        n = w_main.shape[1]
        in_specs += [const((D_MODEL, n)), const((D_MODEL, LANES))]
        args += [w_main, w_small]
        out_specs += [rows(n), rows(LANES)]
        out_shape += [jax.ShapeDtypeStruct((t, n), F32), jax.ShapeDtypeStruct((t, LANES), F32)]
    aliases = {}
    if prev is not None:
        aliases = {len(args) + k: k for k in range(len(prev))}
        in_specs += [pl.BlockSpec(memory_space=pl.ANY)] * len(prev)
        args += list(prev)
    return pl.pallas_call(
        functools.partial(_combine_kernel, with_proj=next_w is not None),
        grid=(n_blk,),
        in_specs=in_specs,
        out_specs=out_specs,
        out_shape=out_shape,
        input_output_aliases=aliases,
        compiler_params=_params("parallel"),
        name="moe_combine_ln",
    )(*args)


def _moe_ln(h, hpk, idx, rank, wk, counts, wg, wu, wd, layer, sg, su, sd, ln_w, ln_b, next_w):
    t = h.shape[0]
    cnt = counts[:, 0].astype(jnp.int32)
    padded = (cnt + MOE_BLOCK - 1) // MOE_BLOCK * MOE_BLOCK
    pend = jnp.cumsum(padded)
    experts = jnp.arange(N_EXPERTS, dtype=jnp.int32)
    pstart_of_pick = jnp.sum(jnp.where(idx[:, :, None] == experts, pend - padded, 0), axis=-1)
    pos = pstart_of_pick + rank
    nb = -(-(t * TOP_K + N_EXPERTS * (MOE_BLOCK - 1)) // MOE_BLOCK)
    starts = jnp.arange(nb, dtype=jnp.int32) * MOE_BLOCK
    block_e = jnp.minimum(jnp.sum((pend[None, :] <= starts[:, None]).astype(jnp.int32), axis=1), N_EXPERTS - 1)
    n_used = (pend[-1] // MOE_BLOCK).astype(jnp.int32).reshape(1)
    has_rows = cnt > 0
    later = jnp.logical_and(has_rows[None, :], experts[None, :] > experts[:, None])
    next_expert = jnp.min(jnp.where(later, experts[None, :], N_EXPERTS), axis=1)
    next_expert = jnp.where(next_expert == N_EXPERTS, -1, next_expert).astype(jnp.int32)
    weight_slot = ((jnp.cumsum(has_rows) - has_rows) % 2).astype(jnp.int32)
    pos_chunks = pos.reshape(TOP_K, t // SC_CHUNK, SC_CHUNK).transpose(1, 0, 2)
    xs = _dispatch_rows(hpk, pos_chunks, nb * MOE_BLOCK)
    ys = _experts(block_e, n_used, next_expert, weight_slot, xs, wg, wu, wd, layer)
    part = t // COMBINE_PARTS
    gathered = [_gather_rows(ys, pos[:, p * part:(p + 1) * part].reshape(-1)).reshape(TOP_K, part, PACKED)
                for p in range(COMBINE_PARTS)]
    outs = None
    for p in range(COMBINE_PARTS):
        outs = _combine_ln(gathered[p], wk, h, hpk, sg, su, sd, ln_w, ln_b, next_w, p, outs)
    return outs


def _pad_cols(w, width=LANES):
    return jnp.pad(w, ((0, 0), (0, width - w.shape[1])))


def _even_proj_weights(w_in):
    a4 = 4 * MIX_HALF
    ng = 2 * HEADS
    w_main = jnp.concatenate([w_in[:, :a4], w_in[:, a4 + ng:]], axis=1).astype(BF16)
    w_gate = _pad_cols(w_in[:, a4:a4 + ng]).astype(BF16)
    return w_main, w_gate


def _even_mixer(proj, gates, batch, seq, gate_b, norm_w, conv_w, conv_b, wa, ba, wx, bx, lam):
    ya = _mlstm(proj, gates, gate_b, norm_w, batch, seq)
    yb = _rglru(proj, conv_w, conv_b, wa, ba, wx, bx, lam, batch, seq)
    return ya, yb


def _odd_proj_weights(w_in):
    c0 = MIX_HALF
    c1 = c0 + HEADS * GLA_DK
    c2 = c1 + HEADS * GLA_DK
    c3 = c2 + MIX_HALF
    c4 = c3 + MIX_HALF
    w_main = jnp.concatenate([w_in[:, :c0], _pad_heads(w_in[:, c0:c1], 1), _pad_heads(w_in[:, c1:c2], 1),
                              w_in[:, c2:c4]], axis=1).astype(BF16)
    w_low = _pad_cols(w_in[:, c4:]).astype(BF16)
    return w_main, w_low


def _odd_mixer(proj, glow, batch, seq, lam_re, lam_im, b_re, b_im, c_re, c_im, d_skip, log_dt,
               glu_w, glu_b, gate_w, gate_b, norm_w):
    tables = _s5_tables(lam_re, lam_im, b_re, b_im, c_re, c_im, log_dt)
    yc = _s5(proj, tables, d_skip, glu_w, glu_b, batch, seq)
    gw = jnp.pad(_pad_heads(gate_w, 1), ((0, LANES - GLA_GATE_RANK), (0, 0))).astype(BF16)
    gb = _pad_heads(gate_b.reshape(1, -1), 1)
    yd = _gla(proj, glow, gw, gb, norm_w, batch, seq)
    return yc, yd


def kernel(x, ln1_w, ln1_b, ln2_w, ln2_b, w_out, w_in_even, mlstm_gate_b, mlstm_norm_w, lru_conv_w, lru_conv_b, lru_wa, lru_ba, lru_wx, lru_bx, lru_lambda, w_in_odd, s5_lam_re, s5_lam_im, s5_b_re, s5_b_im, s5_c_re, s5_c_im, s5_d, s5_log_dt, s5_glu_w, s5_glu_b, gla_gate_w, gla_gate_b, gla_norm_w, router_w, router_bias, exp_w_gate, exp_w_up, exp_w_down, sh_w_gate, sh_w_up, sh_w_down):
    batch, seq, d = x.shape
    proj_w = [_even_proj_weights(w_in_even[layer // 2]) if layer % 2 == 0 else _odd_proj_weights(w_in_odd[layer // 2])
              for layer in range(DEPTH)]
    h = x.reshape(batch * seq, d)
    proj, small = _proj(h, *proj_w[0])
    for layer in range(DEPTH):
        j = layer // 2
        if layer % 2 == 0:
            y1, y2 = _even_mixer(proj, small, batch, seq, mlstm_gate_b[j], mlstm_norm_w[j],
                                 lru_conv_w[j], lru_conv_b[j], lru_wa[j], lru_ba[j], lru_wx[j],
                                 lru_bx[j], lru_lambda[j])
        else:
            y1, y2 = _odd_mixer(proj, small, batch, seq, s5_lam_re[j], s5_lam_im[j], s5_b_re[j],
                                s5_b_im[j], s5_c_re[j], s5_c_im[j], s5_d[j], s5_log_dt[j],
                                s5_glu_w[j], s5_glu_b[j], gla_gate_w[j], gla_gate_b[j], gla_norm_w[j])
        h, hpk, idx, rank, wk, counts = _out_proj_ln_route(y1, y2, h, w_out[layer], ln1_w[layer], ln1_b[layer],
                                                           router_w[layer], router_bias[layer])
        next_w = proj_w[layer + 1] if layer + 1 < DEPTH else None
        res = _moe_ln(h, hpk, idx, rank, wk, counts, exp_w_gate, exp_w_up, exp_w_down, layer,
                      sh_w_gate[layer], sh_w_up[layer], sh_w_down[layer], ln2_w[layer], ln2_b[layer], next_w)
        if next_w is None:
            (h,) = res
        else:
            h, proj, small = res
    return h.reshape(batch, seq, d)
```

```python
import functools
import math

import jax
import jax.numpy as jnp
from jax import lax
from jax.experimental import pallas as pl
from jax.experimental.pallas import tpu as pltpu
from jax.experimental.pallas import tpu_sc as plsc

F32 = jnp.float32
BF16 = jnp.bfloat16

D_MODEL = 1024
DEPTH = 2
MIX_HALF = 512
HEADS = 4
HEAD_DIM = 128
GLA_DK = 64
GLA_CHUNK = 64
GLA_GATE_RANK = 16
GLA_GATE_TEMP = 16.0
LRU_C = 8.0
LRU_CONV = 4
S5_GROUP = 16
S5_GROUPS = 32
S5_STATE = 64
S5_LANES = S5_GROUPS * S5_STATE
S5_BLOCKS = 4
N_EXPERTS = 64
N_GROUPS = 8
GROUP_SIZE = N_EXPERTS // N_GROUPS
TOP_K = 8
TOPK_GROUPS = 4
D_EXPERT = 256
ROUTED_SCALE = 2.5
ALPHA = (2.0 * DEPTH) ** 0.25
EPS = 1e-5
LANES = 128
SUBLANES = 8
NEG_INF = float("-inf")

VMEM_LIMIT = 56 * 1024 * 1024

MLSTM_CHUNK = 128
MLSTM_TILE = 1024
LRU_TILE = 1024
LRU_LOG_STEPS = 3
LRU_UNROLL = 4
S5_TILE = 512
S5_LOG_STEPS = 3
S5_UNROLL = True
GLA_UNROLL = 4
GLA_TILE = 1024
PROJ_TILE = 512
OUT_TILE = 512
MOE_BLOCK = 1024
XS_SLOTS = 3
COMBINE_TILE = 256
COMBINE_PARTS = 4
PACKED = D_MODEL // 2
SC_CHUNK = 64
SC_CORES = 2
SC_SUBCORES = 16
SC_WORKERS = SC_CORES * SC_SUBCORES


def _params(*sem):
    return pltpu.CompilerParams(dimension_semantics=sem, vmem_limit_bytes=VMEM_LIMIT)


def _split3(x):
    hi = x.astype(BF16)
    r1 = x - hi.astype(F32)
    mid = r1.astype(BF16)
    lo = (r1 - mid.astype(F32)).astype(BF16)
    return hi, mid, lo


def _dot(a, b):
    return jnp.dot(a, b, preferred_element_type=F32)


def _dot_nt(a, b):
    return lax.dot_general(a, b, (((1,), (1,)), ((), ())), preferred_element_type=F32)


def _dot_tn(a, b):
    return lax.dot_general(a, b, (((0,), (0,)), ((), ())), preferred_element_type=F32)


def _exact_left01(mask01_bf16, x):
    hi, mid, lo = _split3(x)
    return _dot(mask01_bf16, hi) + _dot(mask01_bf16, mid) + _dot(mask01_bf16, lo)


def _exact_right01(x, mask01_bf16):
    hi, mid, lo = _split3(x)
    return _dot(hi, mask01_bf16) + _dot(mid, mask01_bf16) + _dot(lo, mask01_bf16)


def _log_sigmoid(x):
    return jnp.minimum(x, 0.0) - jnp.log(1.0 + jnp.exp(-jnp.abs(x)))


def _sigmoid(x):
    return 1.0 / (1.0 + jnp.exp(-x))


def _gelu_tanh(x):
    c = math.sqrt(2.0 / math.pi)
    return 0.5 * x * (1.0 + jnp.tanh(c * (x + 0.044715 * (x * x * x))))


def _layer_norm(z, w, b):
    mu = jnp.mean(z, axis=-1, keepdims=True)
    zc = z - mu
    return zc * lax.rsqrt(jnp.mean(zc * zc, axis=-1, keepdims=True) + EPS) * w + b


def _proj_kernel(x_ref, w_ref, wg_ref, o_ref, og_ref):
    x = x_ref[...].astype(BF16)
    o_ref[...] = _dot(x, w_ref[...])
    og_ref[...] = _dot(x, wg_ref[...])


def _proj(x, w_main, w_small):
    t, d = x.shape
    n = w_main.shape[1]
    tm = PROJ_TILE
    return pl.pallas_call(
        _proj_kernel,
        grid=(t // tm,),
        in_specs=[pl.BlockSpec((tm, d), lambda i: (i, 0)),
                  pl.BlockSpec((d, n), lambda i: (0, 0)),
                  pl.BlockSpec((d, LANES), lambda i: (0, 0))],
        out_specs=[pl.BlockSpec((tm, n), lambda i: (i, 0)),
                   pl.BlockSpec((tm, LANES), lambda i: (i, 0))],
        out_shape=[jax.ShapeDtypeStruct((t, n), F32), jax.ShapeDtypeStruct((t, LANES), F32)],
        compiler_params=_params("parallel"),
        name="in_proj",
    )(x, w_main, w_small)


def _mlstm_kernel(q_ref, k_ref, v_ref, o_ref, gc_ref, gr_ref, bc_ref, br_ref, nw_ref,
                  y_ref, c_ref, m_ref, *, chunk, n_chunks):
    L = chunk

    @pl.when(pl.program_id(1) == 0)
    def _():
        c_ref[...] = jnp.zeros_like(c_ref)
        m_ref[...] = jnp.zeros_like(m_ref)

    ri = lax.broadcasted_iota(jnp.int32, (L, L), 0)
    ci = lax.broadcasted_iota(jnp.int32, (L, L), 1)
    causal = ci <= ri
    tril = causal.astype(BF16)
    triu = (ri <= ci).astype(BF16)
    ones_v = jnp.ones((L, HEAD_DIM), BF16)
    scale = HEAD_DIM ** -0.5

    def body(c, carry):
        r0 = pl.multiple_of(c * L, L)
        g_col = gc_ref[pl.ds(r0, L), :] + bc_ref[...]
        g_row = gr_ref[c] + br_ref[...]
        b_col_all = _exact_left01(tril, _log_sigmoid(g_col))
        b_row_all = _exact_right01(_log_sigmoid(g_row), triu)
        for h in range(HEADS):
            lo = h * HEAD_DIM
            q = q_ref[pl.ds(r0, L), lo:lo + HEAD_DIM].astype(BF16)
            k = k_ref[pl.ds(r0, L), lo:lo + HEAD_DIM] * scale
            v = v_ref[pl.ds(r0, L), lo:lo + HEAD_DIM].astype(BF16)
            v_aug = jnp.concatenate([v, ones_v], axis=1)
            i_rep = jnp.broadcast_to(g_col[:, h:h + 1], (L, LANES))
            b_rep = jnp.broadcast_to(b_col_all[:, HEADS + h:HEADS + h + 1], (L, LANES))
            i_row = g_row[h:h + 1, :]
            b_row = b_row_all[HEADS + h:HEADS + h + 1, :]
            b_last = b_rep[L - 1:L, :]
            m_prev = m_ref[h:h + 1, :]
            c_prev = c_ref[h]

            d_mat = jnp.where(causal, b_rep - b_row + i_row, NEG_INF)
            m_inter = b_rep + m_prev
            m_i = jnp.maximum(m_inter, jnp.max(d_mat, axis=1, keepdims=True))
            s = _dot_nt(q, k.astype(BF16)) * jnp.exp(d_mat - m_i)
            w_inter = jnp.exp(m_inter - m_i)
            intra = _dot(s.astype(BF16), v_aug)
            inter = _dot(q, c_prev.astype(BF16))
            num = intra[:, :HEAD_DIM] + w_inter * inter[:, :HEAD_DIM]
            den = intra[:, HEAD_DIM:] + w_inter * inter[:, HEAD_DIM:]
            hh = num / jnp.maximum(jnp.abs(den), jnp.exp(-m_i))

            w_loc = b_last - b_rep + i_rep
            m_loc = jnp.max(w_loc, axis=0, keepdims=True)
            kp = (k * jnp.exp(w_loc - m_loc)).astype(BF16)
            c_loc = _dot_tn(kp, v_aug)
            m_new = jnp.maximum(b_last + m_prev, m_loc)
            keep = jnp.exp(b_last + m_prev - m_new)
            add = jnp.exp(m_loc - m_new)
            c_ref[h] = (jnp.concatenate([keep, keep], axis=1) * c_prev
                        + jnp.concatenate([add, add], axis=1) * c_loc)
            m_ref[h:h + 1, :] = m_new

            hc = hh - jnp.mean(hh, axis=-1, keepdims=True)
            yn = hc * lax.rsqrt(jnp.mean(hc * hc, axis=-1, keepdims=True) + EPS)
            og = o_ref[pl.ds(r0, L), lo:lo + HEAD_DIM]
            y_ref[pl.ds(r0, L), lo:lo + HEAD_DIM] = yn * nw_ref[:, lo:lo + HEAD_DIM] * _sigmoid(og)
        return carry

    lax.fori_loop(0, n_chunks, body, 0)


def _mlstm(proj, gates, gate_b, norm_w, batch, seq):
    t = batch * seq
    L = MLSTM_CHUNK
    assert L == LANES, "the kernel keeps per-row gate terms replicated over one vreg of lanes"
    ts = MLSTM_TILE
    nj = seq // ts
    nc = ts // L
    g_row = gates[:, :2 * HEADS].reshape(t // L, L, 2 * HEADS).transpose(0, 2, 1)
    b_col = jnp.zeros((1, LANES), F32).at[0, :2 * HEADS].set(gate_b)
    b_row = gate_b.reshape(2 * HEADS, 1)
    blk = lambda col: pl.BlockSpec((ts, MIX_HALF), lambda b, j, col=col: (b * nj + j, col))
    kern = functools.partial(_mlstm_kernel, chunk=L, n_chunks=nc)
    return pl.pallas_call(
        kern,
        grid=(batch, nj),
        in_specs=[blk(0), blk(1), blk(2), blk(3),
                  pl.BlockSpec((ts, LANES), lambda b, j: (b * nj + j, 0)),
                  pl.BlockSpec((nc, 2 * HEADS, L), lambda b, j: (b * nj + j, 0, 0)),
                  pl.BlockSpec((1, LANES), lambda b, j: (0, 0)),
                  pl.BlockSpec((2 * HEADS, 1), lambda b, j: (0, 0)),
                  pl.BlockSpec((1, MIX_HALF), lambda b, j: (0, 0))],
        out_specs=pl.BlockSpec((ts, MIX_HALF), lambda b, j: (b * nj + j, 0)),
        out_shape=jax.ShapeDtypeStruct((t, MIX_HALF), F32),
        scratch_shapes=[pltpu.VMEM((HEADS, HEAD_DIM, 2 * HEAD_DIM), F32),
                        pltpu.VMEM((8, LANES), F32)],
        compiler_params=_params("arbitrary", "arbitrary"),
        name="mlstm",
    )(proj, proj, proj, proj, gates, g_row, b_col, b_row, norm_w.reshape(1, MIX_HALF))


def _rglru_kernel(xb_ref, gb_ref, cw_ref, cb_ref, wa_ref, ba_ref, wx_ref, bx_ref, lam_ref,
                  y_ref, xext_ref, h_ref, a_ref, u_ref, *, tile):
    @pl.when(pl.program_id(1) == 0)
    def _():
        xext_ref[0:8, :] = jnp.zeros((8, MIX_HALF), F32)
        h_ref[...] = jnp.zeros_like(h_ref)

    x = xb_ref[...]
    xext_ref[8:8 + tile, :] = x
    xc = cb_ref[...] + cw_ref[LRU_CONV - 1:LRU_CONV, :] * x
    for tap in range(LRU_CONV - 1):
        back = LRU_CONV - 1 - tap
        xc = xc + cw_ref[tap:tap + 1, :] * xext_ref[8 - back:8 - back + tile, :]
    xext_ref[0:8, :] = x[tile - 8:tile, :]

    xc16 = xc.astype(BF16)
    r_parts, i_parts = [], []
    for h in range(HEADS):
        lo = h * HEAD_DIM
        xh = xc16[:, lo:lo + HEAD_DIM]
        r_parts.append(_dot(xh, wa_ref[h]))
        i_parts.append(_dot(xh, wx_ref[h]))
    r = _sigmoid(jnp.concatenate(r_parts, axis=1) + ba_ref[...])
    ig = _sigmoid(jnp.concatenate(i_parts, axis=1) + bx_ref[...])
    lam = lam_ref[...]
    softplus_neg = jnp.maximum(-lam, 0.0) + jnp.log(1.0 + jnp.exp(-jnp.abs(lam)))
    log_a = -LRU_C * r * softplus_neg
    a = jnp.exp(log_a)
    th = jnp.tanh(log_a)
    u = jnp.sqrt(-2.0 * th / (1.0 - th)) * ig * xc

    a_ref[...] = a
    u_ref[...] = u
    rows = lax.broadcasted_iota(jnp.int32, (SUBLANES, MIX_HALF), 0)

    def group(i, h_prev):
        r0 = pl.multiple_of(i * SUBLANES, SUBLANES)
        ag = a_ref[pl.ds(r0, SUBLANES), :]
        ug = u_ref[pl.ds(r0, SUBLANES), :]
        for k in range(LRU_LOG_STEPS):
            keep = rows >= (1 << k)
            ug = ag * jnp.where(keep, pltpu.roll(ug, 1 << k, 0), 0.0) + ug
            ag = ag * jnp.where(keep, pltpu.roll(ag, 1 << k, 0), 1.0)
        hg = ug + ag * h_prev
        u_ref[pl.ds(r0, SUBLANES), :] = hg
        return hg[SUBLANES - 1:SUBLANES, :]

    h_last = lax.fori_loop(0, tile // SUBLANES, group, h_ref[0:1, :], unroll=LRU_UNROLL)
    h_ref[...] = jnp.broadcast_to(h_last, h_ref.shape)
    y_ref[...] = u_ref[...] * _gelu_tanh(gb_ref[...])


def _rglru(proj, conv_w, conv_b, wa, ba, wx, bx, lam, batch, seq):
    t = batch * seq
    ts = LRU_TILE
    nj = seq // ts
    row = lambda a: a.reshape(1, MIX_HALF)
    const2 = lambda shape: pl.BlockSpec(shape, lambda b, j: (0, 0))
    const3 = lambda shape: pl.BlockSpec(shape, lambda b, j: (0, 0, 0))
    blk = lambda col: pl.BlockSpec((ts, MIX_HALF), lambda b, j, col=col: (b * nj + j, col))
    return pl.pallas_call(
        functools.partial(_rglru_kernel, tile=ts),
        grid=(batch, nj),
        in_specs=[blk(4), blk(5), const2((LRU_CONV, MIX_HALF)), const2((1, MIX_HALF)),
                  const3((HEADS, HEAD_DIM, HEAD_DIM)), const2((1, MIX_HALF)),
                  const3((HEADS, HEAD_DIM, HEAD_DIM)), const2((1, MIX_HALF)), const2((1, MIX_HALF))],
        out_specs=pl.BlockSpec((ts, MIX_HALF), lambda b, j: (b * nj + j, 0)),
        out_shape=jax.ShapeDtypeStruct((t, MIX_HALF), F32),
        scratch_shapes=[pltpu.VMEM((ts + 8, MIX_HALF), F32), pltpu.VMEM((8, MIX_HALF), F32),
                        pltpu.VMEM((ts, MIX_HALF), F32), pltpu.VMEM((ts, MIX_HALF), F32)],
        compiler_params=_params("arbitrary", "arbitrary"),
        name="rglru",
    )(proj, proj, conv_w, row(conv_b), wa.astype(BF16), row(ba), wx.astype(BF16), row(bx), row(lam))


def _s5_kernel(u_ref, bre_ref, bim_ref, cre_ref, cim_ref, mre_ref, mim_ref, pre_ref, pim_ref, d_ref, gw_ref,
               gb_ref, y_ref, xr_ref, xi_ref, cr_ref, ci_ref, *, tile):
    @pl.when(pl.program_id(1) == 0)
    def _():
        cr_ref[...] = jnp.zeros_like(cr_ref)
        ci_ref[...] = jnp.zeros_like(ci_ref)

    u = u_ref[...]
    u16 = u.astype(BF16)
    blk_c = MIX_HALF // S5_BLOCKS
    blk_s = S5_LANES // S5_BLOCKS
    parts = []
    for j in range(S5_BLOCKS):
        lanes = slice(j * blk_s, (j + 1) * blk_s)
        uj = u16[:, j * blk_c:(j + 1) * blk_c]
        xr_ref[:, lanes] = _dot(uj, bre_ref[j])
        xi_ref[:, lanes] = _dot(uj, bim_ref[j])

        def group(i, carry, lanes=lanes):
            cr, ci = carry
            r0 = pl.multiple_of(i * SUBLANES, SUBLANES)
            xr = xr_ref[pl.ds(r0, SUBLANES), lanes]
            xi = xi_ref[pl.ds(r0, SUBLANES), lanes]
            for k in range(S5_LOG_STEPS):
                sr = pltpu.roll(xr, 1 << k, 0)
                si = pltpu.roll(xi, 1 << k, 0)
                mr = mre_ref[k, :, lanes]
                mi = mim_ref[k, :, lanes]
                xr, xi = xr + mr * sr - mi * si, xi + mr * si + mi * sr
            pr = pre_ref[:, lanes]
            pi = pim_ref[:, lanes]
            xr, xi = xr + pr * cr - pi * ci, xi + pr * ci + pi * cr
            xr_ref[pl.ds(r0, SUBLANES), lanes] = xr
            xi_ref[pl.ds(r0, SUBLANES), lanes] = xi
            return xr[SUBLANES - 1:SUBLANES, :], xi[SUBLANES - 1:SUBLANES, :]

        cr, ci = lax.fori_loop(0, tile // SUBLANES, group, (cr_ref[0:1, lanes], ci_ref[0:1, lanes]),
                               unroll=S5_UNROLL)
        cr_ref[0:1, lanes] = cr
        ci_ref[0:1, lanes] = ci
        parts.append(_dot(xr_ref[:, lanes].astype(BF16), cre_ref[j])
                     - _dot(xi_ref[:, lanes].astype(BF16), cim_ref[j]))
    y = jnp.concatenate(parts, axis=1) + d_ref[...] * u
    g = _gelu_tanh(y)
    y_ref[...] = g * _sigmoid(_dot(g.astype(BF16), gw_ref[...]) + gb_ref[...])


def _s5_tables(lam_re, lam_im, b_re, b_im, c_re, c_im, log_dt):
    lr, li = lam_re.astype(F32), lam_im.astype(F32)
    dt = jnp.exp(log_dt.astype(F32))[:, None]
    mag = jnp.exp(lr * dt)
    abar_re = mag * jnp.cos(li * dt)
    abar_im = mag * jnp.sin(li * dt)
    den = lr * lr + li * li
    nr = abar_re - 1.0
    coef_re = (nr * lr + abar_im * li) / den
    coef_im = (abar_im * lr - nr * li) / den
    bbar_re = coef_re[..., None] * b_re - coef_im[..., None] * b_im
    bbar_im = coef_re[..., None] * b_im + coef_im[..., None] * b_re
    gpb = S5_GROUPS // S5_BLOCKS
    eye = jnp.eye(gpb, dtype=F32)

    def in_map(bb):
        bb = bb.reshape(S5_BLOCKS, gpb, S5_STATE, S5_GROUP)
        return jnp.einsum("jgph,gk->jghkp", bb, eye).reshape(S5_BLOCKS, gpb * S5_GROUP, gpb * S5_STATE)

    def out_map(cc):
        cc = cc.reshape(S5_BLOCKS, gpb, S5_GROUP, S5_STATE)
        return jnp.einsum("jghp,gk->jgpkh", cc, eye).reshape(S5_BLOCKS, gpb * S5_STATE, gpb * S5_GROUP)

    def power(n):
        n = jnp.asarray(n, F32)[..., None, None]
        pmag = jnp.exp(n * (lr * dt))
        shape = n.shape[:-2] + (S5_LANES,)
        return (pmag * jnp.cos(n * (li * dt))).reshape(shape), (pmag * jnp.sin(n * (li * dt))).reshape(shape)

    row = jnp.arange(SUBLANES)
    step = 2 ** jnp.arange(S5_LOG_STEPS)
    s_re, s_im = power(step)
    keep = (row[None, :] >= step[:, None])[..., None]
    m_re = jnp.where(keep, s_re[:, None, :], 0.0)
    m_im = jnp.where(keep, s_im[:, None, :], 0.0)
    p_re, p_im = power(row + 1)
    return (in_map(bbar_re).astype(BF16), in_map(bbar_im).astype(BF16),
            out_map(c_re.astype(F32)).astype(BF16), out_map(c_im.astype(F32)).astype(BF16),
            m_re, m_im, p_re, p_im)


def _s5(proj, tables, d_skip, glu_w, glu_b, batch, seq):
    t = batch * seq
    ts = S5_TILE
    nj = seq // ts
    bre, bim, cre, cim, m_re, m_im, p_re, p_im = tables
    blk_c = MIX_HALF // S5_BLOCKS
    blk_s = S5_LANES // S5_BLOCKS
    const2 = lambda shape: pl.BlockSpec(shape, lambda b, j: (0, 0))
    const3 = lambda shape: pl.BlockSpec(shape, lambda b, j: (0, 0, 0))
    return pl.pallas_call(
        functools.partial(_s5_kernel, tile=ts),
        grid=(batch, nj),
        in_specs=[pl.BlockSpec((ts, MIX_HALF), lambda b, j: (b * nj + j, 0)),
                  const3((S5_BLOCKS, blk_c, blk_s)), const3((S5_BLOCKS, blk_c, blk_s)),
                  const3((S5_BLOCKS, blk_s, blk_c)), const3((S5_BLOCKS, blk_s, blk_c)),
                  const3(m_re.shape), const3(m_im.shape), const2(p_re.shape), const2(p_im.shape),
                  const2((1, MIX_HALF)), const2((MIX_HALF, MIX_HALF)), const2((1, MIX_HALF))],
        out_specs=pl.BlockSpec((ts, MIX_HALF), lambda b, j: (b * nj + j, 0)),
        out_shape=jax.ShapeDtypeStruct((t, MIX_HALF), F32),
        scratch_shapes=[pltpu.VMEM((ts, S5_LANES), F32), pltpu.VMEM((ts, S5_LANES), F32),
                        pltpu.VMEM((8, S5_LANES), F32), pltpu.VMEM((8, S5_LANES), F32)],
        compiler_params=_params("arbitrary", "arbitrary"),
        name="s5",
    )(proj, bre, bim, cre, cim, m_re, m_im, p_re, p_im, d_skip.reshape(1, MIX_HALF), glu_w.astype(BF16),
      glu_b.reshape(1, MIX_HALF))


def _gla_kernel(q_ref, k_ref, v_ref, r_ref, gl_ref, gw_ref, gb_ref, nw_ref, y_ref,
                st_ref, qd_ref, ki_ref, ke_ref, v16_ref, dec_ref, o_ref, *, tile, chunk):
    L = chunk
    nc = tile // L

    @pl.when(pl.program_id(1) == 0)
    def _():
        st_ref[...] = jnp.zeros_like(st_ref)

    z = _dot(gl_ref[...].astype(BF16), gw_ref[...]) + gb_ref[...]
    bcum = _log_sigmoid(z) * (1.0 / GLA_GATE_TEMP)
    row_in_chunk = lax.broadcasted_iota(jnp.int32, bcum.shape, 0) & (L - 1)
    s = 1
    while s < L:
        bcum = bcum + jnp.where(row_in_chunk >= s, pltpu.roll(bcum, s, 0), 0.0)
        s *= 2
    b3 = bcum.reshape(nc, L, MIX_HALF)
    b_last = b3[:, L - 1:L, :]
    k = k_ref[...]
    qd_ref[...] = (q_ref[...] * (GLA_DK ** -0.5) * jnp.exp(bcum)).astype(BF16)
    ki_ref[...] = (k * jnp.exp(-bcum)).astype(BF16)
    ke_ref[...] = (k.reshape(nc, L, MIX_HALF) * jnp.exp(b_last - b3)).reshape(tile, MIX_HALF).astype(BF16)
    v16_ref[...] = v_ref[...].astype(BF16)
    dec_ref[...] = jnp.exp(b_last)

    ri = lax.broadcasted_iota(jnp.int32, (L, L), 0)
    ci = lax.broadcasted_iota(jnp.int32, (L, L), 1)
    causal = ci <= ri

    def body(c, carry):
        r0 = pl.multiple_of(c * L, L)
        dec = dec_ref[c]
        for h in range(HEADS):
            lo = h * HEAD_DIM
            q_dec = qd_ref[pl.ds(r0, L), lo:lo + HEAD_DIM]
            v = v16_ref[pl.ds(r0, L), lo:lo + HEAD_DIM]
            st = st_ref[h]
            att = jnp.where(causal, _dot_nt(q_dec, ki_ref[pl.ds(r0, L), lo:lo + HEAD_DIM]), 0.0)
            o_ref[pl.ds(r0, L), lo:lo + HEAD_DIM] = (_dot(att.astype(BF16), v)
                                                     + _dot_nt(q_dec, st.astype(BF16)))
            st_ref[h] = dec[:, lo:lo + HEAD_DIM] * st + _dot_tn(v, ke_ref[pl.ds(r0, L), lo:lo + HEAD_DIM])
        return carry

    lax.fori_loop(0, nc, body, 0, unroll=GLA_UNROLL)

    rg = r_ref[...]
    gate = nw_ref[...] * (rg * _sigmoid(rg))
    for h in range(HEADS):
        lo = h * HEAD_DIM
        o = o_ref[:, lo:lo + HEAD_DIM]
        yn = o * lax.rsqrt(jnp.mean(o * o, axis=-1, keepdims=True) + EPS)
        y_ref[:, lo:lo + HEAD_DIM] = yn * gate[:, lo:lo + HEAD_DIM]


def _gla(proj, glow, gate_w, gate_b, norm_w, batch, seq):
    t = batch * seq
    ts = GLA_TILE
    nj = seq // ts
    blk = lambda col: pl.BlockSpec((ts, MIX_HALF), lambda b, j, col=col: (b * nj + j, col))
    const2 = lambda shape: pl.BlockSpec(shape, lambda b, j: (0, 0))
    return pl.pallas_call(
        functools.partial(_gla_kernel, tile=ts, chunk=GLA_CHUNK),
        grid=(batch, nj),
        in_specs=[blk(1), blk(2), blk(3), blk(4),
                  pl.BlockSpec((ts, LANES), lambda b, j: (b * nj + j, 0)),
                  const2((LANES, MIX_HALF)), const2((1, MIX_HALF)), const2((1, MIX_HALF))],
        out_specs=pl.BlockSpec((ts, MIX_HALF), lambda b, j: (b * nj + j, 0)),
        out_shape=jax.ShapeDtypeStruct((t, MIX_HALF), F32),
        scratch_shapes=[pltpu.VMEM((HEADS, HEAD_DIM, HEAD_DIM), F32),
                        pltpu.VMEM((ts, MIX_HALF), BF16), pltpu.VMEM((ts, MIX_HALF), BF16),
                        pltpu.VMEM((ts, MIX_HALF), BF16), pltpu.VMEM((ts, MIX_HALF), BF16),
                        pltpu.VMEM((ts // GLA_CHUNK, 1, MIX_HALF), F32),
                        pltpu.VMEM((ts, MIX_HALF), F32)],
        compiler_params=_params("arbitrary", "arbitrary"),
        name="gla",
    )(proj, proj, proj, proj, glow, gate_w, gate_b, norm_w.reshape(1, MIX_HALF))


def _pad_heads(w, axis):
    shape = list(w.shape)
    shape[axis:axis + 1] = [HEADS, GLA_DK]
    w = w.reshape(shape)
    pad = [(0, 0)] * w.ndim
    pad[axis + 1] = (0, HEAD_DIM - GLA_DK)
    w = jnp.pad(w, pad)
    shape[axis:axis + 2] = [HEADS * HEAD_DIM]
    return w.reshape(shape)


def _pack_bf16_pairs(z):
    hi = lax.bitcast_convert_type(z[:, :PACKED].astype(BF16).astype(F32), jnp.uint32)
    lo = lax.bitcast_convert_type(z[:, PACKED:].astype(BF16).astype(F32), jnp.uint32)
    word = (hi & jnp.uint32(0xFFFF0000)) | lax.shift_right_logical(lo, jnp.uint32(16))
    return lax.bitcast_convert_type(word, jnp.int32)


def _unpack_bf16_pairs(p):
    word = lax.bitcast_convert_type(p, jnp.uint32)
    hi = lax.bitcast_convert_type(word & jnp.uint32(0xFFFF0000), F32)
    lo = lax.bitcast_convert_type(lax.shift_left(word, jnp.uint32(16)), F32)
    return hi, lo


def _out_kernel(ya_ref, yb_ref, h_ref, w_ref, lw_ref, lb_ref, rw_ref, rb_ref,
                o_ref, opk_ref, idx_ref, rank_ref, wk_ref, cnt_ref, base_ref, *, tile):
    mixed = jnp.concatenate([ya_ref[...], yb_ref[...]], axis=1).astype(BF16)
    z = ALPHA * h_ref[...] + _dot(mixed, w_ref[...])
    out = _layer_norm(z, lw_ref[...], lb_ref[...])
    o_ref[...] = out
    opk_ref[...] = _pack_bf16_pairs(out)
    _route_tile(out, rw_ref, rb_ref, idx_ref, rank_ref, wk_ref, cnt_ref, base_ref, tile)


def _out_proj_ln_route(ya, yb, h, w_out, ln_w, ln_b, router_w, router_bias):
    t = h.shape[0]
    tm = OUT_TILE
    const = lambda shape: pl.BlockSpec(shape, lambda i: (0, 0))
    per_tok = lambda dt: jax.ShapeDtypeStruct((TOP_K, t), dt)
    tok_blk = pl.BlockSpec((TOP_K, tm), lambda i: (0, i))
    return pl.pallas_call(
        functools.partial(_out_kernel, tile=tm),
        grid=(t // tm,),
        in_specs=[pl.BlockSpec((tm, MIX_HALF), lambda i: (i, 0)),
                  pl.BlockSpec((tm, MIX_HALF), lambda i: (i, 0)),
                  pl.BlockSpec((tm, D_MODEL), lambda i: (i, 0)),
                  const((D_MODEL, D_MODEL)), const((1, D_MODEL)), const((1, D_MODEL)),
                  const((N_EXPERTS, D_MODEL)), const((N_EXPERTS, 1))],
        out_specs=[pl.BlockSpec((tm, D_MODEL), lambda i: (i, 0)),
                   pl.BlockSpec((tm, PACKED), lambda i: (i, 0)),
                   tok_blk, tok_blk, pl.BlockSpec((tm, TOP_K), lambda i: (i, 0)), const((N_EXPERTS, LANES))],
        out_shape=[jax.ShapeDtypeStruct((t, D_MODEL), F32), jax.ShapeDtypeStruct((t, PACKED), jnp.int32),
                   per_tok(jnp.int32), per_tok(jnp.int32), jax.ShapeDtypeStruct((t, TOP_K), F32),
                   jax.ShapeDtypeStruct((N_EXPERTS, LANES), F32)],
        scratch_shapes=[pltpu.VMEM((N_EXPERTS, LANES), F32)],
        compiler_params=_params("arbitrary"),
        name="out_proj_ln_route",
    )(ya, yb, h, w_out.astype(BF16), ln_w.reshape(1, D_MODEL), ln_b.reshape(1, D_MODEL),
      router_w.T, router_bias.reshape(N_EXPERTS, 1))


def _first_index(hit, idx, big):
    return jnp.min(jnp.where(hit, idx, big), axis=0, keepdims=True)


def _route_tile(h, w_ref, b_ref, idx_ref, rank_ref, wk_ref, cnt_ref, base_ref, tile):
    @pl.when(pl.program_id(0) == 0)
    def _():
        base_ref[...] = jnp.zeros_like(base_ref)

    h_hi, h_mid, _ = _split3(h)
    w_hi, w_mid, _ = _split3(w_ref[...])
    logits = _dot_nt(w_hi, h_hi) + _dot_nt(w_hi, h_mid) + _dot_nt(w_mid, h_hi)
    scores = _sigmoid(logits)
    biased = scores + b_ref[...]

    sub = lax.broadcasted_iota(jnp.int32, (GROUP_SIZE, tile), 0)
    grp_rows = []
    for g in range(N_GROUPS):
        xg = biased[g * GROUP_SIZE:(g + 1) * GROUP_SIZE, :]
        m1 = jnp.max(xg, axis=0, keepdims=True)
        i1 = _first_index(xg == m1, sub, GROUP_SIZE)
        m2 = jnp.max(jnp.where(sub == i1, NEG_INF, xg), axis=0, keepdims=True)
        grp_rows.append(m1 + m2)
    gs = jnp.concatenate(grp_rows, axis=0)
    gsel = jnp.zeros((N_GROUPS, tile), F32)
    for _ in range(TOPK_GROUPS):
        mx = jnp.max(gs, axis=0, keepdims=True)
        hit = sub == _first_index(gs == mx, sub, N_GROUPS)
        gsel = jnp.where(hit, 1.0, gsel)
        gs = jnp.where(hit, NEG_INF, gs)
    emask = jnp.concatenate(
        [jnp.broadcast_to(gsel[g:g + 1, :], (GROUP_SIZE, tile)) for g in range(N_GROUPS)], axis=0)

    eidx = lax.broadcasted_iota(jnp.int32, (N_EXPERTS, tile), 0)
    cand = jnp.where(emask > 0.5, biased, NEG_INF)
    sel = jnp.zeros((N_EXPERTS, tile), F32)
    picks = []
    for _ in range(TOP_K):
        mx = jnp.max(cand, axis=0, keepdims=True)
        first = _first_index(cand == mx, eidx, N_EXPERTS)
        hit = eidx == first
        picks.append(first)
        sel = jnp.where(hit, 1.0, sel)
        cand = jnp.where(hit, NEG_INF, cand)
    picked = jnp.where(sel > 0.5, scores, 0.0)
    wts = picked / jnp.sum(picked, axis=0, keepdims=True) * ROUTED_SCALE

    ri = lax.broadcasted_iota(jnp.int32, (tile, tile), 0)
    ci = lax.broadcasted_iota(jnp.int32, (tile, tile), 1)
    before = (ri < ci).astype(BF16)
    prior = _dot(sel.astype(BF16), before) + base_ref[:, 0:1]
    ranks = [jnp.sum(jnp.where(eidx == p, prior, 0.0), axis=0, keepdims=True) for p in picks]
    wsel = [jnp.sum(jnp.where(eidx == p, wts, 0.0), axis=0, keepdims=True) for p in picks]
    idx_ref[...] = jnp.concatenate(picks, axis=0)
    rank_ref[...] = jnp.concatenate(ranks, axis=0).astype(jnp.int32)
    wk_ref[...] = jnp.concatenate(wsel, axis=0).T
    total = base_ref[...] + jnp.sum(sel, axis=1, keepdims=True)
    base_ref[...] = total
    cnt_ref[...] = total


def _silu(x):
    return x * _sigmoid(x)


def _sc_mesh():
    return plsc.VectorSubcoreMesh(core_axis_name="c", subcore_axis_name="s")


def _sc_worker_id():
    return lax.axis_index("s") * SC_CORES + lax.axis_index("c")


def _dispatch_rows(xpk, pos_chunks, n_rows):
    t = xpk.shape[0]
    n_ch = t // SC_WORKERS // SC_CHUNK

    @functools.partial(
        pl.kernel, mesh=_sc_mesh(),
        out_type=jax.ShapeDtypeStruct((n_rows, PACKED), jnp.int32),
        scratch_types=[pltpu.VMEM((TOP_K, SC_CHUNK), jnp.int32),
                       pltpu.VMEM((SC_CHUNK, PACKED), jnp.int32),
                       pltpu.SemaphoreType.DMA],
        name="moe_dispatch",
    )
    def scatter(x_hbm, pos_hbm, out_hbm, idx_v, rows_v, sem):
        wid = _sc_worker_id()

        @pl.loop(0, n_ch)
        def _(c):
            chunk = wid * n_ch + c
            off = pl.multiple_of(chunk * SC_CHUNK, SC_CHUNK)
            pltpu.sync_copy(pos_hbm.at[chunk], idx_v)
            pltpu.sync_copy(x_hbm.at[pl.ds(off, SC_CHUNK)], rows_v)
            copies = [pltpu.async_copy(rows_v, out_hbm.at[idx_v.at[k]], sem) for k in range(TOP_K)]
            for cp in copies:
                cp.wait()

    return scatter(xpk, pos_chunks)


def _gather_rows(table, idx):
    n = idx.shape[0]
    per_w = n // SC_WORKERS
    n_ch = per_w // SC_CHUNK
    assert n_ch % 2 == 0 and n_ch >= 2

    @functools.partial(
        pl.kernel, mesh=_sc_mesh(),
        out_type=jax.ShapeDtypeStruct((n, PACKED), jnp.int32),
        scratch_types=[pltpu.VMEM((n_ch, SC_CHUNK), jnp.int32),
                       pltpu.VMEM((SC_CHUNK, PACKED), jnp.int32), pltpu.VMEM((SC_CHUNK, PACKED), jnp.int32),
                       pltpu.SemaphoreType.DMA, pltpu.SemaphoreType.DMA,
                       pltpu.SemaphoreType.DMA, pltpu.SemaphoreType.DMA],
        name="moe_gather",
    )
    def gather(table_hbm, idx_hbm, out_hbm, idx_v, rows0, rows1, g0, g1, w0, w1):
        wid = _sc_worker_id()
        base = wid * per_w
        rows, g_sem, w_sem = (rows0, rows1), (g0, g1), (w0, w1)
        pltpu.sync_copy(idx_hbm.at[wid], idx_v)

        def fetch(c, b):
            return pltpu.make_async_copy(table_hbm.at[idx_v.at[c]], rows[b], g_sem[b])

        def flush(c, b):
            off = pl.multiple_of(base + c * SC_CHUNK, SC_CHUNK)
            return pltpu.make_async_copy(rows[b], out_hbm.at[pl.ds(off, SC_CHUNK)], w_sem[b])

        fetch(0, 0).start()

        @pl.loop(0, n_ch, step=2)
        def _(c0):
            for b in range(2):
                c = c0 + b
                fetch(c, b).wait()
                flush(c, b).start()

                @pl.when(c + 1 < n_ch)
                def _():
                    @pl.when(c >= 1)
                    def _():
                        flush(c - 1, 1 - b).wait()
                    fetch(c + 1, 1 - b).start()

        flush(n_ch - 2, 0).wait()
        flush(n_ch - 1, 1).wait()

    return gather(table, idx.reshape(SC_WORKERS, n_ch, SC_CHUNK))


def _unpacked_bf16(p):
    hi, lo = _unpack_bf16_pairs(p)
    return jnp.concatenate([hi.astype(BF16), lo.astype(BF16)], axis=1)


def _expert_kernel(be_ref, nu_ref, next_ref, slot_ref, xs_hbm, wg_hbm, wu_hbm, wd_hbm, y_ref,
                   g16_ref, u16_ref, d16_ref, gf_ref, uf_ref, df_ref, xbuf_ref, xsem, wsem, *, layer):
    i = pl.program_id(0)
    n_used = nu_ref[0]
    bm = xbuf_ref.shape[1]

    def fetch(b):
        slot = lax.rem(b, XS_SLOTS)
        rows = pl.ds(pl.multiple_of(b * bm, bm), bm)
        return pltpu.make_async_copy(xs_hbm.at[rows], xbuf_ref.at[slot], xsem.at[slot])

    @pl.when(i == 0)
    def _():
        fetch(0).start()

        @pl.when(n_used > 1)
        def _():
            fetch(1).start()

    @pl.when(i + 2 < n_used)
    def _():
        fetch(i + 2).start()

    e = be_ref[i]
    slot = slot_ref[e]
    f32_bufs = (gf_ref, uf_ref, df_ref)

    def wfetch(expert, dst_slot):
        return [pltpu.make_async_copy(w_hbm.at[layer, expert], buf.at[dst_slot], wsem.at[dst_slot, j])
                for j, (w_hbm, buf) in enumerate(zip((wg_hbm, wu_hbm, wd_hbm), f32_bufs))]

    @pl.when(i == 0)
    def _():
        for cp in wfetch(e, slot):
            cp.start()

    first_block_of_expert = jnp.logical_or(i == 0, e != be_ref[jnp.maximum(i - 1, 0)])

    @pl.when(jnp.logical_and(first_block_of_expert, i < n_used))
    def _():
        for cp in wfetch(e, slot):
            cp.wait()
        g16_ref[slot] = gf_ref[slot].astype(BF16)
        u16_ref[slot] = uf_ref[slot].astype(BF16)
        d16_ref[slot] = df_ref[slot].astype(BF16)
        nxt = next_ref[e]

        @pl.when(nxt >= 0)
        def _():
            for cp in wfetch(nxt, 1 - slot):
                cp.start()

    @pl.when(i < n_used)
    def _():
        fetch(i).wait()
        x = _unpacked_bf16(xbuf_ref[lax.rem(i, XS_SLOTS)])
        hh = _silu(_dot(x, g16_ref[slot])) * _dot(x, u16_ref[slot])
        y_ref[...] = _pack_bf16_pairs(_dot(hh.astype(BF16), d16_ref[slot]))


def _experts(block_e, n_used, next_expert, weight_slot, xs, wg, wu, wd, layer):
    nb = block_e.shape[0]
    bm = MOE_BLOCK
    hbm = pl.BlockSpec(memory_space=pl.ANY)
    two = lambda shape, dt: pltpu.VMEM((2,) + shape, dt)
    grid_spec = pltpu.PrefetchScalarGridSpec(
        num_scalar_prefetch=4,
        grid=(nb,),
        in_specs=[hbm, hbm, hbm, hbm],
        out_specs=pl.BlockSpec((bm, PACKED), lambda i, be, nu, nx, sl: (jnp.minimum(i, nu[0] - 1), 0)),
        scratch_shapes=[two((D_MODEL, D_EXPERT), BF16), two((D_MODEL, D_EXPERT), BF16), two((D_EXPERT, D_MODEL), BF16),
                        two((D_MODEL, D_EXPERT), F32), two((D_MODEL, D_EXPERT), F32), two((D_EXPERT, D_MODEL), F32),
                        pltpu.VMEM((XS_SLOTS, bm, PACKED), jnp.int32), pltpu.SemaphoreType.DMA((XS_SLOTS,)),
                        pltpu.SemaphoreType.DMA((2, 3))],
    )
    return pl.pallas_call(
        functools.partial(_expert_kernel, layer=layer),
        grid_spec=grid_spec,
        out_shape=jax.ShapeDtypeStruct((nb * bm, PACKED), jnp.int32),
        compiler_params=_params("arbitrary"),
        name="moe_experts",
    )(block_e, n_used, next_expert, weight_slot, xs, wg, wu, wd)


def _combine_kernel(g_ref, wk_ref, h_ref, xpk_ref, sg_ref, su_ref, sd_ref, lw_ref, lb_ref, *rest, with_proj):
    if with_proj:
        (wm_ref, ws_ref), (o_ref, proj_ref, small_ref) = rest[:2], rest[-3:]
        next_proj = (wm_ref, ws_ref, proj_ref, small_ref)
    else:
        o_ref, next_proj = rest[-1], None
    x = _unpacked_bf16(xpk_ref[...])
    hs = _silu(_dot(x, sg_ref[...])) * _dot(x, su_ref[...])
    shared = _dot(hs.astype(BF16), sd_ref[...])
    acc_hi = shared[:, :PACKED]
    acc_lo = shared[:, PACKED:]
    wk = wk_ref[...]
    for k in range(TOP_K):
        y_hi, y_lo = _unpack_bf16_pairs(g_ref[k])
        w = wk[:, k:k + 1]
        acc_hi = acc_hi + w * y_hi
        acc_lo = acc_lo + w * y_lo
    ffn = jnp.concatenate([acc_hi, acc_lo], axis=1)
    out = _layer_norm(ALPHA * h_ref[...] + ffn, lw_ref[...], lb_ref[...])
    o_ref[...] = out
    if next_proj is not None:
        wm_ref, ws_ref, proj_ref, small_ref = next_proj
        out16 = out.astype(BF16)
        proj_ref[...] = _dot(out16, wm_ref[...])
        small_ref[...] = _dot(out16, ws_ref[...])


def _combine_ln(g, wk, h, xpk, sg, su, sd, ln_w, ln_b, next_w, part, prev):
    t = h.shape[0]
    tm = COMBINE_TILE
    n_blk = g.shape[1] // tm
    first = part * n_blk
    const = lambda shape: pl.BlockSpec(shape, lambda i: (0, 0))
    rows = lambda width: pl.BlockSpec((tm, width), lambda i: (i + first, 0))
    in_specs = [pl.BlockSpec((TOP_K, tm, PACKED), lambda i: (0, i, 0)), rows(TOP_K), rows(D_MODEL), rows(PACKED),
                const((D_MODEL, D_EXPERT)), const((D_MODEL, D_EXPERT)), const((D_EXPERT, D_MODEL)),
                const((1, D_MODEL)), const((1, D_MODEL))]
    args = [g, wk, h, xpk, sg.astype(BF16), su.astype(BF16), sd.astype(BF16),
            ln_w.reshape(1, D_MODEL), ln_b.reshape(1, D_MODEL)]
    out_specs = [rows(D_MODEL)]
    out_shape = [jax.ShapeDtypeStruct((t, D_MODEL), F32)]
    if next_w is not None:
        w_main, w_small = next_w
        n = w_main.shape[1]
        in_specs += [const((D_MODEL, n)), const((D_MODEL, LANES))]
        args += [w_main, w_small]
        out_specs += [rows(n), rows(LANES)]
        out_shape += [jax.ShapeDtypeStruct((t, n), F32), jax.ShapeDtypeStruct((t, LANES), F32)]
    aliases = {}
    if prev is not None:
        aliases = {len(args) + k: k for k in range(len(prev))}
        in_specs += [pl.BlockSpec(memory_space=pl.ANY)] * len(prev)
        args += list(prev)
    return pl.pallas_call(
        functools.partial(_combine_kernel, with_proj=next_w is not None),
        grid=(n_blk,),
        in_specs=in_specs,
        out_specs=out_specs,
        out_shape=out_shape,
        input_output_aliases=aliases,
        compiler_params=_params("parallel"),
        name="moe_combine_ln",
    )(*args)


def _moe_ln(h, hpk, idx, rank, wk, counts, wg, wu, wd, layer, sg, su, sd, ln_w, ln_b, next_w):
    t = h.shape[0]
    cnt = counts[:, 0].astype(jnp.int32)
    padded = (cnt + MOE_BLOCK - 1) // MOE_BLOCK * MOE_BLOCK
    pend = jnp.cumsum(padded)
    experts = jnp.arange(N_EXPERTS, dtype=jnp.int32)
    pstart_of_pick = jnp.sum(jnp.where(idx[:, :, None] == experts, pend - padded, 0), axis=-1)
    pos = pstart_of_pick + rank
    nb = -(-(t * TOP_K + N_EXPERTS * (MOE_BLOCK - 1)) // MOE_BLOCK)
    starts = jnp.arange(nb, dtype=jnp.int32) * MOE_BLOCK
    block_e = jnp.minimum(jnp.sum((pend[None, :] <= starts[:, None]).astype(jnp.int32), axis=1), N_EXPERTS - 1)
    n_used = (pend[-1] // MOE_BLOCK).astype(jnp.int32).reshape(1)
    has_rows = cnt > 0
    later = jnp.logical_and(has_rows[None, :], experts[None, :] > experts[:, None])
    next_expert = jnp.min(jnp.where(later, experts[None, :], N_EXPERTS), axis=1)
    next_expert = jnp.where(next_expert == N_EXPERTS, -1, next_expert).astype(jnp.int32)
    weight_slot = ((jnp.cumsum(has_rows) - has_rows) % 2).astype(jnp.int32)
    pos_chunks = pos.reshape(TOP_K, t // SC_CHUNK, SC_CHUNK).transpose(1, 0, 2)
    xs = _dispatch_rows(hpk, pos_chunks, nb * MOE_BLOCK)
    ys = _experts(block_e, n_used, next_expert, weight_slot, xs, wg, wu, wd, layer)
    part = t // COMBINE_PARTS
    gathered = [_gather_rows(ys, pos[:, p * part:(p + 1) * part].reshape(-1)).reshape(TOP_K, part, PACKED)
                for p in range(COMBINE_PARTS)]
    outs = None
    for p in range(COMBINE_PARTS):
        outs = _combine_ln(gathered[p], wk, h, hpk, sg, su, sd, ln_w, ln_b, next_w, p, outs)
    return outs


def _pad_cols(w, width=LANES):
    return jnp.pad(w, ((0, 0), (0, width - w.shape[1])))


def _even_proj_weights(w_in):
    a4 = 4 * MIX_HALF
    ng = 2 * HEADS
    w_main = jnp.concatenate([w_in[:, :a4], w_in[:, a4 + ng:]], axis=1).astype(BF16)
    w_gate = _pad_cols(w_in[:, a4:a4 + ng]).astype(BF16)
    return w_main, w_gate


def _even_mixer(proj, gates, batch, seq, gate_b, norm_w, conv_w, conv_b, wa, ba, wx, bx, lam):
    ya = _mlstm(proj, gates, gate_b, norm_w, batch, seq)
    yb = _rglru(proj, conv_w, conv_b, wa, ba, wx, bx, lam, batch, seq)
    return ya, yb


def _odd_proj_weights(w_in):
    c0 = MIX_HALF
    c1 = c0 + HEADS * GLA_DK
    c2 = c1 + HEADS * GLA_DK
    c3 = c2 + MIX_HALF
    c4 = c3 + MIX_HALF
    w_main = jnp.concatenate([w_in[:, :c0], _pad_heads(w_in[:, c0:c1], 1), _pad_heads(w_in[:, c1:c2], 1),
                              w_in[:, c2:c4]], axis=1).astype(BF16)
    w_low = _pad_cols(w_in[:, c4:]).astype(BF16)
    return w_main, w_low


def _odd_mixer(proj, glow, batch, seq, lam_re, lam_im, b_re, b_im, c_re, c_im, d_skip, log_dt,
               glu_w, glu_b, gate_w, gate_b, norm_w):
    tables = _s5_tables(lam_re, lam_im, b_re, b_im, c_re, c_im, log_dt)
    yc = _s5(proj, tables, d_skip, glu_w, glu_b, batch, seq)
    gw = jnp.pad(_pad_heads(gate_w, 1), ((0, LANES - GLA_GATE_RANK), (0, 0))).astype(BF16)
    gb = _pad_heads(gate_b.reshape(1, -1), 1)
    yd = _gla(proj, glow, gw, gb, norm_w, batch, seq)
    return yc, yd


def kernel(x, ln1_w, ln1_b, ln2_w, ln2_b, w_out, w_in_even, mlstm_gate_b, mlstm_norm_w, lru_conv_w, lru_conv_b, lru_wa, lru_ba, lru_wx, lru_bx, lru_lambda, w_in_odd, s5_lam_re, s5_lam_im, s5_b_re, s5_b_im, s5_c_re, s5_c_im, s5_d, s5_log_dt, s5_glu_w, s5_glu_b, gla_gate_w, gla_gate_b, gla_norm_w, router_w, router_bias, exp_w_gate, exp_w_up, exp_w_down, sh_w_gate, sh_w_up, sh_w_down):
    batch, seq, d = x.shape
    proj_w = [_even_proj_weights(w_in_even[layer // 2]) if layer % 2 == 0 else _odd_proj_weights(w_in_odd[layer // 2])
              for layer in range(DEPTH)]
    h = x.reshape(batch * seq, d)
    proj, small = _proj(h, *proj_w[0])
    for layer in range(DEPTH):
        j = layer // 2
        if layer % 2 == 0:
            y1, y2 = _even_mixer(proj, small, batch, seq, mlstm_gate_b[j], mlstm_norm_w[j],
                                 lru_conv_w[j], lru_conv_b[j], lru_wa[j], lru_ba[j], lru_wx[j],
                                 lru_bx[j], lru_lambda[j])
        else:
            y1, y2 = _odd_mixer(proj, small, batch, seq, s5_lam_re[j], s5_lam_im[j], s5_b_re[j],
                                s5_b_im[j], s5_c_re[j], s5_c_im[j], s5_d[j], s5_log_dt[j],
                                s5_glu_w[j], s5_glu_b[j], gla_gate_w[j], gla_gate_b[j], gla_norm_w[j])
        h, hpk, idx, rank, wk, counts = _out_proj_ln_route(y1, y2, h, w_out[layer], ln1_w[layer], ln1_b[layer],
                                                           router_w[layer], router_bias[layer])
        next_w = proj_w[layer + 1] if layer + 1 < DEPTH else None
        res = _moe_ln(h, hpk, idx, rank, wk, counts, exp_w_gate, exp_w_up, exp_w_down, layer,
                      sh_w_gate[layer], sh_w_up[layer], sh_w_down[layer], ln2_w[layer], ln2_b[layer], next_w)
        if next_w is None:
            (h,) = res
        else:
            h, proj, small = res
    return h.reshape(batch, seq, d)
```

```python
import functools
import math

import jax
import jax.numpy as jnp
from jax import lax
from jax.experimental import pallas as pl
from jax.experimental.pallas import tpu as pltpu
from jax.experimental.pallas import tpu_sc as plsc

F32 = jnp.float32
BF16 = jnp.bfloat16

D_MODEL = 1024
DEPTH = 2
MIX_HALF = 512
HEADS = 4
HEAD_DIM = 128
GLA_DK = 64
GLA_CHUNK = 64
GLA_GATE_RANK = 16
GLA_GATE_TEMP = 16.0
LRU_C = 8.0
LRU_CONV = 4
S5_GROUP = 16
S5_GROUPS = 32
S5_STATE = 64
S5_LANES = S5_GROUPS * S5_STATE
S5_BLOCKS = 4
N_EXPERTS = 64
N_GROUPS = 8
GROUP_SIZE = N_EXPERTS // N_GROUPS
TOP_K = 8
TOPK_GROUPS = 4
D_EXPERT = 256
ROUTED_SCALE = 2.5
ALPHA = (2.0 * DEPTH) ** 0.25
EPS = 1e-5
LANES = 128
SUBLANES = 8
NEG_INF = float("-inf")

VMEM_LIMIT = 56 * 1024 * 1024

MLSTM_CHUNK = 128
MLSTM_TILE = 1024
LRU_TILE = 1024
LRU_LOG_STEPS = 3
LRU_UNROLL = 4
S5_TILE = 512
S5_LOG_STEPS = 3
S5_UNROLL = True
GLA_UNROLL = 4
GLA_TILE = 1024
PROJ_TILE = 1024
OUT_TILE = 512
MOE_BLOCK = 1024
XS_SLOTS = 3
COMBINE_TILE = 256
COMBINE_PARTS = 2
PACKED = D_MODEL // 2
SC_CHUNK = 64
SC_CORES = 2
SC_SUBCORES = 16
SC_WORKERS = SC_CORES * SC_SUBCORES


def _params(*sem):
    return pltpu.CompilerParams(dimension_semantics=sem, vmem_limit_bytes=VMEM_LIMIT)


def _split3(x):
    hi = x.astype(BF16)
    r1 = x - hi.astype(F32)
    mid = r1.astype(BF16)
    lo = (r1 - mid.astype(F32)).astype(BF16)
    return hi, mid, lo


def _dot(a, b):
    return jnp.dot(a, b, preferred_element_type=F32)


def _dot_nt(a, b):
    return lax.dot_general(a, b, (((1,), (1,)), ((), ())), preferred_element_type=F32)


def _dot_tn(a, b):
    return lax.dot_general(a, b, (((0,), (0,)), ((), ())), preferred_element_type=F32)


def _exact_left01(mask01_bf16, x):
    hi, mid, lo = _split3(x)
    return _dot(mask01_bf16, hi) + _dot(mask01_bf16, mid) + _dot(mask01_bf16, lo)


def _exact_right01(x, mask01_bf16):
    hi, mid, lo = _split3(x)
    return _dot(hi, mask01_bf16) + _dot(mid, mask01_bf16) + _dot(lo, mask01_bf16)


def _log_sigmoid(x):
    return jnp.minimum(x, 0.0) - jnp.log(1.0 + jnp.exp(-jnp.abs(x)))


def _sigmoid(x):
    return 1.0 / (1.0 + jnp.exp(-x))


def _gelu_tanh(x):
    c = math.sqrt(2.0 / math.pi)
    return 0.5 * x * (1.0 + jnp.tanh(c * (x + 0.044715 * (x * x * x))))


def _layer_norm(z, w, b):
    mu = jnp.mean(z, axis=-1, keepdims=True)
    zc = z - mu
    return zc * lax.rsqrt(jnp.mean(zc * zc, axis=-1, keepdims=True) + EPS) * w + b


def _proj_kernel(x_ref, w_ref, wg_ref, o_ref, og_ref):
    x = x_ref[...].astype(BF16)
    o_ref[...] = _dot(x, w_ref[...])
    og_ref[...] = _dot(x, wg_ref[...])


def _proj(x, w_main, w_small):
    t, d = x.shape
    n = w_main.shape[1]
    tm = PROJ_TILE
    return pl.pallas_call(
        _proj_kernel,
        grid=(t // tm,),
        in_specs=[pl.BlockSpec((tm, d), lambda i: (i, 0)),
                  pl.BlockSpec((d, n), lambda i: (0, 0)),
                  pl.BlockSpec((d, LANES), lambda i: (0, 0))],
        out_specs=[pl.BlockSpec((tm, n), lambda i: (i, 0)),
                   pl.BlockSpec((tm, LANES), lambda i: (i, 0))],
        out_shape=[jax.ShapeDtypeStruct((t, n), F32), jax.ShapeDtypeStruct((t, LANES), F32)],
        compiler_params=_params("parallel"),
        name="in_proj",
    )(x, w_main, w_small)


def _mlstm_kernel(q_ref, k_ref, v_ref, o_ref, gc_ref, gr_ref, bc_ref, br_ref, nw_ref,
                  y_ref, c_ref, m_ref, *, chunk, n_chunks):
    L = chunk

    @pl.when(pl.program_id(1) == 0)
    def _():
        c_ref[...] = jnp.zeros_like(c_ref)
        m_ref[...] = jnp.zeros_like(m_ref)

    ri = lax.broadcasted_iota(jnp.int32, (L, L), 0)
    ci = lax.broadcasted_iota(jnp.int32, (L, L), 1)
    causal = ci <= ri
    tril = causal.astype(BF16)
    triu = (ri <= ci).astype(BF16)
    ones_v = jnp.ones((L, HEAD_DIM), BF16)
    scale = HEAD_DIM ** -0.5

    def body(c, carry):
        r0 = pl.multiple_of(c * L, L)
        g_col = gc_ref[pl.ds(r0, L), :] + bc_ref[...]
        g_row = gr_ref[c] + br_ref[...]
        b_col_all = _exact_left01(tril, _log_sigmoid(g_col))
        b_row_all = _exact_right01(_log_sigmoid(g_row), triu)
        for h in range(HEADS):
            lo = h * HEAD_DIM
            q = q_ref[pl.ds(r0, L), lo:lo + HEAD_DIM].astype(BF16)
            k = k_ref[pl.ds(r0, L), lo:lo + HEAD_DIM] * scale
            v = v_ref[pl.ds(r0, L), lo:lo + HEAD_DIM].astype(BF16)
            v_aug = jnp.concatenate([v, ones_v], axis=1)
            i_rep = jnp.broadcast_to(g_col[:, h:h + 1], (L, LANES))
            b_rep = jnp.broadcast_to(b_col_all[:, HEADS + h:HEADS + h + 1], (L, LANES))
            i_row = g_row[h:h + 1, :]
            b_row = b_row_all[HEADS + h:HEADS + h + 1, :]
            b_last = b_rep[L - 1:L, :]
            m_prev = m_ref[h:h + 1, :]
            c_prev = c_ref[h]

            d_mat = jnp.where(causal, b_rep - b_row + i_row, NEG_INF)
            m_inter = b_rep + m_prev
            m_i = jnp.maximum(m_inter, jnp.max(d_mat, axis=1, keepdims=True))
            s = _dot_nt(q, k.astype(BF16)) * jnp.exp(d_mat - m_i)
            w_inter = jnp.exp(m_inter - m_i)
            intra = _dot(s.astype(BF16), v_aug)
            inter = _dot(q, c_prev.astype(BF16))
            num = intra[:, :HEAD_DIM] + w_inter * inter[:, :HEAD_DIM]
            den = intra[:, HEAD_DIM:] + w_inter * inter[:, HEAD_DIM:]
            hh = num / jnp.maximum(jnp.abs(den), jnp.exp(-m_i))

            w_loc = b_last - b_rep + i_rep
            m_loc = jnp.max(w_loc, axis=0, keepdims=True)
            kp = (k * jnp.exp(w_loc - m_loc)).astype(BF16)
            c_loc = _dot_tn(kp, v_aug)
            m_new = jnp.maximum(b_last + m_prev, m_loc)
            keep = jnp.exp(b_last + m_prev - m_new)
            add = jnp.exp(m_loc - m_new)
            c_ref[h] = (jnp.concatenate([keep, keep], axis=1) * c_prev
                        + jnp.concatenate([add, add], axis=1) * c_loc)
            m_ref[h:h + 1, :] = m_new

            hc = hh - jnp.mean(hh, axis=-1, keepdims=True)
            yn = hc * lax.rsqrt(jnp.mean(hc * hc, axis=-1, keepdims=True) + EPS)
            og = o_ref[pl.ds(r0, L), lo:lo + HEAD_DIM]
            y_ref[pl.ds(r0, L), lo:lo + HEAD_DIM] = (yn * nw_ref[:, lo:lo + HEAD_DIM] * _sigmoid(og)).astype(BF16)
        return carry

    lax.fori_loop(0, n_chunks, body, 0)


def _mlstm(proj, gates, gate_b, norm_w, batch, seq):
    t = batch * seq
    L = MLSTM_CHUNK
    assert L == LANES, "the kernel keeps per-row gate terms replicated over one vreg of lanes"
    ts = MLSTM_TILE
    nj = seq // ts
    nc = ts // L
    g_row = gates[:, :2 * HEADS].reshape(t // L, L, 2 * HEADS).transpose(0, 2, 1)
    b_col = jnp.zeros((1, LANES), F32).at[0, :2 * HEADS].set(gate_b)
    b_row = gate_b.reshape(2 * HEADS, 1)
    blk = lambda col: pl.BlockSpec((ts, MIX_HALF), lambda b, j, col=col: (b * nj + j, col))
    kern = functools.partial(_mlstm_kernel, chunk=L, n_chunks=nc)
    return pl.pallas_call(
        kern,
        grid=(batch, nj),
        in_specs=[blk(0), blk(1), blk(2), blk(3),
                  pl.BlockSpec((ts, LANES), lambda b, j: (b * nj + j, 0)),
                  pl.BlockSpec((nc, 2 * HEADS, L), lambda b, j: (b * nj + j, 0, 0)),
                  pl.BlockSpec((1, LANES), lambda b, j: (0, 0)),
                  pl.BlockSpec((2 * HEADS, 1), lambda b, j: (0, 0)),
                  pl.BlockSpec((1, MIX_HALF), lambda b, j: (0, 0))],
        out_specs=pl.BlockSpec((ts, MIX_HALF), lambda b, j: (b * nj + j, 0)),
        out_shape=jax.ShapeDtypeStruct((t, MIX_HALF), BF16),
        scratch_shapes=[pltpu.VMEM((HEADS, HEAD_DIM, 2 * HEAD_DIM), F32),
                        pltpu.VMEM((8, LANES), F32)],
        compiler_params=_params("arbitrary", "arbitrary"),
        name="mlstm",
    )(proj, proj, proj, proj, gates, g_row, b_col, b_row, norm_w.reshape(1, MIX_HALF))


def _rglru_kernel(xb_ref, gb_ref, cw_ref, cb_ref, wa_ref, ba_ref, wx_ref, bx_ref, lam_ref,
                  y_ref, xext_ref, h_ref, a_ref, u_ref, *, tile):
    @pl.when(pl.program_id(1) == 0)
    def _():
        xext_ref[0:8, :] = jnp.zeros((8, MIX_HALF), F32)
        h_ref[...] = jnp.zeros_like(h_ref)

    x = xb_ref[...]
    xext_ref[8:8 + tile, :] = x
    xc = cb_ref[...] + cw_ref[LRU_CONV - 1:LRU_CONV, :] * x
    for tap in range(LRU_CONV - 1):
        back = LRU_CONV - 1 - tap
        xc = xc + cw_ref[tap:tap + 1, :] * xext_ref[8 - back:8 - back + tile, :]
    xext_ref[0:8, :] = x[tile - 8:tile, :]

    xc16 = xc.astype(BF16)
    r_parts, i_parts = [], []
    for h in range(HEADS):
        lo = h * HEAD_DIM
        xh = xc16[:, lo:lo + HEAD_DIM]
        r_parts.append(_dot(xh, wa_ref[h]))
        i_parts.append(_dot(xh, wx_ref[h]))
    r = _sigmoid(jnp.concatenate(r_parts, axis=1) + ba_ref[...])
    ig = _sigmoid(jnp.concatenate(i_parts, axis=1) + bx_ref[...])
    lam = lam_ref[...]
    softplus_neg = jnp.maximum(-lam, 0.0) + jnp.log(1.0 + jnp.exp(-jnp.abs(lam)))
    log_a = -LRU_C * r * softplus_neg
    a = jnp.exp(log_a)
    th = jnp.tanh(log_a)
    u = jnp.sqrt(-2.0 * th / (1.0 - th)) * ig * xc

    a_ref[...] = a
    u_ref[...] = u
    rows = lax.broadcasted_iota(jnp.int32, (SUBLANES, MIX_HALF), 0)

    def group(i, h_prev):
        r0 = pl.multiple_of(i * SUBLANES, SUBLANES)
        ag = a_ref[pl.ds(r0, SUBLANES), :]
        ug = u_ref[pl.ds(r0, SUBLANES), :]
        for k in range(LRU_LOG_STEPS):
            keep = rows >= (1 << k)
            ug = ag * jnp.where(keep, pltpu.roll(ug, 1 << k, 0), 0.0) + ug
            ag = ag * jnp.where(keep, pltpu.roll(ag, 1 << k, 0), 1.0)
        hg = ug + ag * h_prev
        u_ref[pl.ds(r0, SUBLANES), :] = hg
        return hg[SUBLANES - 1:SUBLANES, :]

    h_last = lax.fori_loop(0, tile // SUBLANES, group, h_ref[0:1, :], unroll=LRU_UNROLL)
    h_ref[...] = jnp.broadcast_to(h_last, h_ref.shape)
    y_ref[...] = (u_ref[...] * _gelu_tanh(gb_ref[...])).astype(BF16)


def _rglru(proj, conv_w, conv_b, wa, ba, wx, bx, lam, batch, seq):
    t = batch * seq
    ts = LRU_TILE
    nj = seq // ts
    row = lambda a: a.reshape(1, MIX_HALF)
    const2 = lambda shape: pl.BlockSpec(shape, lambda b, j: (0, 0))
    const3 = lambda shape: pl.BlockSpec(shape, lambda b, j: (0, 0, 0))
    blk = lambda col: pl.BlockSpec((ts, MIX_HALF), lambda b, j, col=col: (b * nj + j, col))
    return pl.pallas_call(
        functools.partial(_rglru_kernel, tile=ts),
        grid=(batch, nj),
        in_specs=[blk(4), blk(5), const2((LRU_CONV, MIX_HALF)), const2((1, MIX_HALF)),
                  const3((HEADS, HEAD_DIM, HEAD_DIM)), const2((1, MIX_HALF)),
                  const3((HEADS, HEAD_DIM, HEAD_DIM)), const2((1, MIX_HALF)), const2((1, MIX_HALF))],
        out_specs=pl.BlockSpec((ts, MIX_HALF), lambda b, j: (b * nj + j, 0)),
        out_shape=jax.ShapeDtypeStruct((t, MIX_HALF), BF16),
        scratch_shapes=[pltpu.VMEM((ts + 8, MIX_HALF), F32), pltpu.VMEM((8, MIX_HALF), F32),
                        pltpu.VMEM((ts, MIX_HALF), F32), pltpu.VMEM((ts, MIX_HALF), F32)],
        compiler_params=_params("arbitrary", "arbitrary"),
        name="rglru",
    )(proj, proj, conv_w, row(conv_b), wa.astype(BF16), row(ba), wx.astype(BF16), row(bx), row(lam))


def _s5_kernel(u_ref, bre_ref, bim_ref, cre_ref, cim_ref, mre_ref, mim_ref, pre_ref, pim_ref, d_ref, gw_ref,
               gb_ref, y_ref, xr_ref, xi_ref, cr_ref, ci_ref, *, tile):
    @pl.when(pl.program_id(1) == 0)
    def _():
        cr_ref[...] = jnp.zeros_like(cr_ref)
        ci_ref[...] = jnp.zeros_like(ci_ref)

    u = u_ref[...]
    u16 = u.astype(BF16)
    blk_c = MIX_HALF // S5_BLOCKS
    blk_s = S5_LANES // S5_BLOCKS
    parts = []
    for j in range(S5_BLOCKS):
        lanes = slice(j * blk_s, (j + 1) * blk_s)
        uj = u16[:, j * blk_c:(j + 1) * blk_c]
        xr_ref[:, lanes] = _dot(uj, bre_ref[j])
        xi_ref[:, lanes] = _dot(uj, bim_ref[j])

        def group(i, carry, lanes=lanes):
            cr, ci = carry
            r0 = pl.multiple_of(i * SUBLANES, SUBLANES)
            xr = xr_ref[pl.ds(r0, SUBLANES), lanes]
            xi = xi_ref[pl.ds(r0, SUBLANES), lanes]
            for k in range(S5_LOG_STEPS):
                sr = pltpu.roll(xr, 1 << k, 0)
                si = pltpu.roll(xi, 1 << k, 0)
                mr = mre_ref[k, :, lanes]
                mi = mim_ref[k, :, lanes]
                xr, xi = xr + mr * sr - mi * si, xi + mr * si + mi * sr
            pr = pre_ref[:, lanes]
            pi = pim_ref[:, lanes]
            xr, xi = xr + pr * cr - pi * ci, xi + pr * ci + pi * cr
            xr_ref[pl.ds(r0, SUBLANES), lanes] = xr
            xi_ref[pl.ds(r0, SUBLANES), lanes] = xi
            return xr[SUBLANES - 1:SUBLANES, :], xi[SUBLANES - 1:SUBLANES, :]

        cr, ci = lax.fori_loop(0, tile // SUBLANES, group, (cr_ref[0:1, lanes], ci_ref[0:1, lanes]),
                               unroll=S5_UNROLL)
        cr_ref[0:1, lanes] = cr
        ci_ref[0:1, lanes] = ci
        parts.append(_dot(xr_ref[:, lanes].astype(BF16), cre_ref[j])
                     - _dot(xi_ref[:, lanes].astype(BF16), cim_ref[j]))
    y = jnp.concatenate(parts, axis=1) + d_ref[...] * u
    g = _gelu_tanh(y)
    y_ref[...] = (g * _sigmoid(_dot(g.astype(BF16), gw_ref[...]) + gb_ref[...])).astype(BF16)


def _s5_tables(lam_re, lam_im, b_re, b_im, c_re, c_im, log_dt):
    lr, li = lam_re.astype(F32), lam_im.astype(F32)
    dt = jnp.exp(log_dt.astype(F32))[:, None]
    mag = jnp.exp(lr * dt)
    abar_re = mag * jnp.cos(li * dt)
    abar_im = mag * jnp.sin(li * dt)
    den = lr * lr + li * li
    nr = abar_re - 1.0
    coef_re = (nr * lr + abar_im * li) / den
    coef_im = (abar_im * lr - nr * li) / den
    bbar_re = coef_re[..., None] * b_re - coef_im[..., None] * b_im
    bbar_im = coef_re[..., None] * b_im + coef_im[..., None] * b_re
    gpb = S5_GROUPS // S5_BLOCKS
    eye = jnp.eye(gpb, dtype=F32)

    def in_map(bb):
        bb = bb.reshape(S5_BLOCKS, gpb, S5_STATE, S5_GROUP)
        return jnp.einsum("jgph,gk->jghkp", bb, eye).reshape(S5_BLOCKS, gpb * S5_GROUP, gpb * S5_STATE)

    def out_map(cc):
        cc = cc.reshape(S5_BLOCKS, gpb, S5_GROUP, S5_STATE)
        return jnp.einsum("jghp,gk->jgpkh", cc, eye).reshape(S5_BLOCKS, gpb * S5_STATE, gpb * S5_GROUP)

    def power(n):
        n = jnp.asarray(n, F32)[..., None, None]
        pmag = jnp.exp(n * (lr * dt))
        shape = n.shape[:-2] + (S5_LANES,)
        return (pmag * jnp.cos(n * (li * dt))).reshape(shape), (pmag * jnp.sin(n * (li * dt))).reshape(shape)

    row = jnp.arange(SUBLANES)
    step = 2 ** jnp.arange(S5_LOG_STEPS)
    s_re, s_im = power(step)
    keep = (row[None, :] >= step[:, None])[..., None]
    m_re = jnp.where(keep, s_re[:, None, :], 0.0)
    m_im = jnp.where(keep, s_im[:, None, :], 0.0)
    p_re, p_im = power(row + 1)
    return (in_map(bbar_re).astype(BF16), in_map(bbar_im).astype(BF16),
            out_map(c_re.astype(F32)).astype(BF16), out_map(c_im.astype(F32)).astype(BF16),
            m_re, m_im, p_re, p_im)


def _s5(proj, tables, d_skip, glu_w, glu_b, batch, seq):
    t = batch * seq
    ts = S5_TILE
    nj = seq // ts
    bre, bim, cre, cim, m_re, m_im, p_re, p_im = tables
    blk_c = MIX_HALF // S5_BLOCKS
    blk_s = S5_LANES // S5_BLOCKS
    const2 = lambda shape: pl.BlockSpec(shape, lambda b, j: (0, 0))
    const3 = lambda shape: pl.BlockSpec(shape, lambda b, j: (0, 0, 0))
    return pl.pallas_call(
        functools.partial(_s5_kernel, tile=ts),
        grid=(batch, nj),
        in_specs=[pl.BlockSpec((ts, MIX_HALF), lambda b, j: (b * nj + j, 0)),
                  const3((S5_BLOCKS, blk_c, blk_s)), const3((S5_BLOCKS, blk_c, blk_s)),
                  const3((S5_BLOCKS, blk_s, blk_c)), const3((S5_BLOCKS, blk_s, blk_c)),
                  const3(m_re.shape), const3(m_im.shape), const2(p_re.shape), const2(p_im.shape),
                  const2((1, MIX_HALF)), const2((MIX_HALF, MIX_HALF)), const2((1, MIX_HALF))],
        out_specs=pl.BlockSpec((ts, MIX_HALF), lambda b, j: (b * nj + j, 0)),
        out_shape=jax.ShapeDtypeStruct((t, MIX_HALF), BF16),
        scratch_shapes=[pltpu.VMEM((ts, S5_LANES), F32), pltpu.VMEM((ts, S5_LANES), F32),
                        pltpu.VMEM((8, S5_LANES), F32), pltpu.VMEM((8, S5_LANES), F32)],
        compiler_params=_params("arbitrary", "arbitrary"),
        name="s5",
    )(proj, bre, bim, cre, cim, m_re, m_im, p_re, p_im, d_skip.reshape(1, MIX_HALF), glu_w.astype(BF16),
      glu_b.reshape(1, MIX_HALF))


def _gla_kernel(q_ref, k_ref, v_ref, r_ref, gl_ref, gw_ref, gb_ref, nw_ref, y_ref,
                st_ref, qd_ref, ki_ref, ke_ref, v16_ref, dec_ref, o_ref, *, tile, chunk):
    L = chunk
    nc = tile // L

    @pl.when(pl.program_id(1) == 0)
    def _():
        st_ref[...] = jnp.zeros_like(st_ref)

    z = _dot(gl_ref[...].astype(BF16), gw_ref[...]) + gb_ref[...]
    bcum = _log_sigmoid(z) * (1.0 / GLA_GATE_TEMP)
    row_in_chunk = lax.broadcasted_iota(jnp.int32, bcum.shape, 0) & (L - 1)
    s = 1
    while s < L:
        bcum = bcum + jnp.where(row_in_chunk >= s, pltpu.roll(bcum, s, 0), 0.0)
        s *= 2
    b3 = bcum.reshape(nc, L, MIX_HALF)
    b_last = b3[:, L - 1:L, :]
    k = k_ref[...]
    qd_ref[...] = (q_ref[...] * (GLA_DK ** -0.5) * jnp.exp(bcum)).astype(BF16)
    ki_ref[...] = (k * jnp.exp(-bcum)).astype(BF16)
    ke_ref[...] = (k.reshape(nc, L, MIX_HALF) * jnp.exp(b_last - b3)).reshape(tile, MIX_HALF).astype(BF16)
    v16_ref[...] = v_ref[...].astype(BF16)
    dec_ref[...] = jnp.exp(b_last)

    ri = lax.broadcasted_iota(jnp.int32, (L, L), 0)
    ci = lax.broadcasted_iota(jnp.int32, (L, L), 1)
    causal = ci <= ri

    def body(c, carry):
        r0 = pl.multiple_of(c * L, L)
        dec = dec_ref[c]
        for h in range(HEADS):
            lo = h * HEAD_DIM
            q_dec = qd_ref[pl.ds(r0, L), lo:lo + HEAD_DIM]
            v = v16_ref[pl.ds(r0, L), lo:lo + HEAD_DIM]
            st = st_ref[h]
            att = jnp.where(causal, _dot_nt(q_dec, ki_ref[pl.ds(r0, L), lo:lo + HEAD_DIM]), 0.0)
            o_ref[pl.ds(r0, L), lo:lo + HEAD_DIM] = (_dot(att.astype(BF16), v)
                                                     + _dot_nt(q_dec, st.astype(BF16)))
            st_ref[h] = dec[:, lo:lo + HEAD_DIM] * st + _dot_tn(v, ke_ref[pl.ds(r0, L), lo:lo + HEAD_DIM])
        return carry

    lax.fori_loop(0, nc, body, 0, unroll=GLA_UNROLL)

    rg = r_ref[...]
    gate = nw_ref[...] * (rg * _sigmoid(rg))
    for h in range(HEADS):
        lo = h * HEAD_DIM
        o = o_ref[:, lo:lo + HEAD_DIM]
        yn = o * lax.rsqrt(jnp.mean(o * o, axis=-1, keepdims=True) + EPS)
        y_ref[:, lo:lo + HEAD_DIM] = (yn * gate[:, lo:lo + HEAD_DIM]).astype(BF16)


def _gla(proj, glow, gate_w, gate_b, norm_w, batch, seq):
    t = batch * seq
    ts = GLA_TILE
    nj = seq // ts
    blk = lambda col: pl.BlockSpec((ts, MIX_HALF), lambda b, j, col=col: (b * nj + j, col))
    const2 = lambda shape: pl.BlockSpec(shape, lambda b, j: (0, 0))
    return pl.pallas_call(
        functools.partial(_gla_kernel, tile=ts, chunk=GLA_CHUNK),
        grid=(batch, nj),
        in_specs=[blk(1), blk(2), blk(3), blk(4),
                  pl.BlockSpec((ts, LANES), lambda b, j: (b * nj + j, 0)),
                  const2((LANES, MIX_HALF)), const2((1, MIX_HALF)), const2((1, MIX_HALF))],
        out_specs=pl.BlockSpec((ts, MIX_HALF), lambda b, j: (b * nj + j, 0)),
        out_shape=jax.ShapeDtypeStruct((t, MIX_HALF), BF16),
        scratch_shapes=[pltpu.VMEM((HEADS, HEAD_DIM, HEAD_DIM), F32),
                        pltpu.VMEM((ts, MIX_HALF), BF16), pltpu.VMEM((ts, MIX_HALF), BF16),
                        pltpu.VMEM((ts, MIX_HALF), BF16), pltpu.VMEM((ts, MIX_HALF), BF16),
                        pltpu.VMEM((ts // GLA_CHUNK, 1, MIX_HALF), F32),
                        pltpu.VMEM((ts, MIX_HALF), F32)],
        compiler_params=_params("arbitrary", "arbitrary"),
        name="gla",
    )(proj, proj, proj, proj, glow, gate_w, gate_b, norm_w.reshape(1, MIX_HALF))


def _pad_heads(w, axis):
    shape = list(w.shape)
    shape[axis:axis + 1] = [HEADS, GLA_DK]
    w = w.reshape(shape)
    pad = [(0, 0)] * w.ndim
    pad[axis + 1] = (0, HEAD_DIM - GLA_DK)
    w = jnp.pad(w, pad)
    shape[axis:axis + 2] = [HEADS * HEAD_DIM]
    return w.reshape(shape)


def _pack_bf16_pairs(z):
    hi = lax.bitcast_convert_type(z[:, :PACKED].astype(BF16).astype(F32), jnp.uint32)
    lo = lax.bitcast_convert_type(z[:, PACKED:].astype(BF16).astype(F32), jnp.uint32)
    word = (hi & jnp.uint32(0xFFFF0000)) | lax.shift_right_logical(lo, jnp.uint32(16))
    return lax.bitcast_convert_type(word, jnp.int32)


def _unpack_bf16_pairs(p):
    word = lax.bitcast_convert_type(p, jnp.uint32)
    hi = lax.bitcast_convert_type(word & jnp.uint32(0xFFFF0000), F32)
    lo = lax.bitcast_convert_type(lax.shift_left(word, jnp.uint32(16)), F32)
    return hi, lo


def _out_kernel(ya_ref, yb_ref, h_ref, w_ref, lw_ref, lb_ref, rw_ref, rb_ref,
                o_ref, opk_ref, idx_ref, rank_ref, wk_ref, cnt_ref, base_ref, *, tile):
    mixed = jnp.concatenate([ya_ref[...], yb_ref[...]], axis=1)
    z = ALPHA * h_ref[...] + _dot(mixed, w_ref[...])
    out = _layer_norm(z, lw_ref[...], lb_ref[...])
    o_ref[...] = out
    opk_ref[...] = _pack_bf16_pairs(out)
    _route_tile(out, rw_ref, rb_ref, idx_ref, rank_ref, wk_ref, cnt_ref, base_ref, tile)


def _out_proj_ln_route(ya, yb, h, w_out, ln_w, ln_b, router_w, router_bias):
    t = h.shape[0]
    tm = OUT_TILE
    const = lambda shape: pl.BlockSpec(shape, lambda i: (0, 0))
    per_tok = lambda dt: jax.ShapeDtypeStruct((TOP_K, t), dt)
    tok_blk = pl.BlockSpec((TOP_K, tm), lambda i: (0, i))
    return pl.pallas_call(
        functools.partial(_out_kernel, tile=tm),
        grid=(t // tm,),
        in_specs=[pl.BlockSpec((tm, MIX_HALF), lambda i: (i, 0)),
                  pl.BlockSpec((tm, MIX_HALF), lambda i: (i, 0)),
                  pl.BlockSpec((tm, D_MODEL), lambda i: (i, 0)),
                  const((D_MODEL, D_MODEL)), const((1, D_MODEL)), const((1, D_MODEL)),
                  const((N_EXPERTS, D_MODEL)), const((N_EXPERTS, 1))],
        out_specs=[pl.BlockSpec((tm, D_MODEL), lambda i: (i, 0)),
                   pl.BlockSpec((tm, PACKED), lambda i: (i, 0)),
                   tok_blk, tok_blk, pl.BlockSpec((tm, TOP_K), lambda i: (i, 0)), const((N_EXPERTS, LANES))],
        out_shape=[jax.ShapeDtypeStruct((t, D_MODEL), F32), jax.ShapeDtypeStruct((t, PACKED), jnp.int32),
                   per_tok(jnp.int32), per_tok(jnp.int32), jax.ShapeDtypeStruct((t, TOP_K), F32),
                   jax.ShapeDtypeStruct((N_EXPERTS, LANES), F32)],
        scratch_shapes=[pltpu.VMEM((N_EXPERTS, LANES), F32)],
        compiler_params=_params("arbitrary"),
        name="out_proj_ln_route",
    )(ya, yb, h, w_out.astype(BF16), ln_w.reshape(1, D_MODEL), ln_b.reshape(1, D_MODEL),
      router_w.T, router_bias.reshape(N_EXPERTS, 1))


def _first_index(hit, idx, big):
    return jnp.min(jnp.where(hit, idx, big), axis=0, keepdims=True)


def _route_tile(h, w_ref, b_ref, idx_ref, rank_ref, wk_ref, cnt_ref, base_ref, tile):
    @pl.when(pl.program_id(0) == 0)
    def _():
        base_ref[...] = jnp.zeros_like(base_ref)

    h_hi, h_mid, _ = _split3(h)
    w_hi, w_mid, _ = _split3(w_ref[...])
    logits = _dot_nt(w_hi, h_hi) + _dot_nt(w_hi, h_mid) + _dot_nt(w_mid, h_hi)
    scores = _sigmoid(logits)
    biased = scores + b_ref[...]

    sub = lax.broadcasted_iota(jnp.int32, (GROUP_SIZE, tile), 0)
    grp_rows = []
    for g in range(N_GROUPS):
        xg = biased[g * GROUP_SIZE:(g + 1) * GROUP_SIZE, :]
        m1 = jnp.max(xg, axis=0, keepdims=True)
        i1 = _first_index(xg == m1, sub, GROUP_SIZE)
        m2 = jnp.max(jnp.where(sub == i1, NEG_INF, xg), axis=0, keepdims=True)
        grp_rows.append(m1 + m2)
    gs = jnp.concatenate(grp_rows, axis=0)
    gsel = jnp.zeros((N_GROUPS, tile), F32)
    for _ in range(TOPK_GROUPS):
        mx = jnp.max(gs, axis=0, keepdims=True)
        hit = sub == _first_index(gs == mx, sub, N_GROUPS)
        gsel = jnp.where(hit, 1.0, gsel)
        gs = jnp.where(hit, NEG_INF, gs)
    emask = jnp.concatenate(
        [jnp.broadcast_to(gsel[g:g + 1, :], (GROUP_SIZE, tile)) for g in range(N_GROUPS)], axis=0)

    eidx = lax.broadcasted_iota(jnp.int32, (N_EXPERTS, tile), 0)
    cand = jnp.where(emask > 0.5, biased, NEG_INF)
    sel = jnp.zeros((N_EXPERTS, tile), F32)
    picks = []
    for _ in range(TOP_K):
        mx = jnp.max(cand, axis=0, keepdims=True)
        first = _first_index(cand == mx, eidx, N_EXPERTS)
        hit = eidx == first
        picks.append(first)
        sel = jnp.where(hit, 1.0, sel)
        cand = jnp.where(hit, NEG_INF, cand)
    picked = jnp.where(sel > 0.5, scores, 0.0)
    wts = picked / jnp.sum(picked, axis=0, keepdims=True) * ROUTED_SCALE

    ri = lax.broadcasted_iota(jnp.int32, (tile, tile), 0)
    ci = lax.broadcasted_iota(jnp.int32, (tile, tile), 1)
    before = (ri < ci).astype(BF16)
    prior = _dot(sel.astype(BF16), before) + base_ref[:, 0:1]
    ranks = [jnp.sum(jnp.where(eidx == p, prior, 0.0), axis=0, keepdims=True) for p in picks]
    wsel = [jnp.sum(jnp.where(eidx == p, wts, 0.0), axis=0, keepdims=True) for p in picks]
    idx_ref[...] = jnp.concatenate(picks, axis=0)
    rank_ref[...] = jnp.concatenate(ranks, axis=0).astype(jnp.int32)
    wk_ref[...] = jnp.concatenate(wsel, axis=0).T
    total = base_ref[...] + jnp.sum(sel, axis=1, keepdims=True)
    base_ref[...] = total
    cnt_ref[...] = total


def _silu(x):
    return x * _sigmoid(x)


def _sc_mesh():
    return plsc.VectorSubcoreMesh(core_axis_name="c", subcore_axis_name="s")


def _sc_worker_id():
    return lax.axis_index("s") * SC_CORES + lax.axis_index("c")


def _dispatch_rows(xpk, pos_chunks, n_rows):
    t = xpk.shape[0]
    n_ch = t // SC_WORKERS // SC_CHUNK

    @functools.partial(
        pl.kernel, mesh=_sc_mesh(),
        out_type=jax.ShapeDtypeStruct((n_rows, PACKED), jnp.int32),
        scratch_types=[pltpu.VMEM((TOP_K, SC_CHUNK), jnp.int32),
                       pltpu.VMEM((SC_CHUNK, PACKED), jnp.int32),
                       pltpu.SemaphoreType.DMA],
        name="moe_dispatch",
    )
    def scatter(x_hbm, pos_hbm, out_hbm, idx_v, rows_v, sem):
        wid = _sc_worker_id()

        @pl.loop(0, n_ch)
        def _(c):
            chunk = wid * n_ch + c
            off = pl.multiple_of(chunk * SC_CHUNK, SC_CHUNK)
            pltpu.sync_copy(pos_hbm.at[chunk], idx_v)
            pltpu.sync_copy(x_hbm.at[pl.ds(off, SC_CHUNK)], rows_v)
            copies = [pltpu.async_copy(rows_v, out_hbm.at[idx_v.at[k]], sem) for k in range(TOP_K)]
            for cp in copies:
                cp.wait()

    return scatter(xpk, pos_chunks)


def _gather_rows(table, idx):
    n = idx.shape[0]
    per_w = n // SC_WORKERS
    n_ch = per_w // SC_CHUNK
    assert n_ch % 2 == 0 and n_ch >= 2

    @functools.partial(
        pl.kernel, mesh=_sc_mesh(),
        out_type=jax.ShapeDtypeStruct((n, PACKED), jnp.int32),
        scratch_types=[pltpu.VMEM((n_ch, SC_CHUNK), jnp.int32),
                       pltpu.VMEM((SC_CHUNK, PACKED), jnp.int32), pltpu.VMEM((SC_CHUNK, PACKED), jnp.int32),
                       pltpu.SemaphoreType.DMA, pltpu.SemaphoreType.DMA,
                       pltpu.SemaphoreType.DMA, pltpu.SemaphoreType.DMA],
        name="moe_gather",
    )
    def gather(table_hbm, idx_hbm, out_hbm, idx_v, rows0, rows1, g0, g1, w0, w1):
        wid = _sc_worker_id()
        base = wid * per_w
        rows, g_sem, w_sem = (rows0, rows1), (g0, g1), (w0, w1)
        pltpu.sync_copy(idx_hbm.at[wid], idx_v)

        def fetch(c, b):
            return pltpu.make_async_copy(table_hbm.at[idx_v.at[c]], rows[b], g_sem[b])

        def flush(c, b):
            off = pl.multiple_of(base + c * SC_CHUNK, SC_CHUNK)
            return pltpu.make_async_copy(rows[b], out_hbm.at[pl.ds(off, SC_CHUNK)], w_sem[b])

        fetch(0, 0).start()

        @pl.loop(0, n_ch, step=2)
        def _(c0):
            for b in range(2):
                c = c0 + b
                fetch(c, b).wait()
                flush(c, b).start()

                @pl.when(c + 1 < n_ch)
                def _():
                    @pl.when(c >= 1)
                    def _():
                        flush(c - 1, 1 - b).wait()
                    fetch(c + 1, 1 - b).start()

        flush(n_ch - 2, 0).wait()
        flush(n_ch - 1, 1).wait()

    return gather(table, idx.reshape(SC_WORKERS, n_ch, SC_CHUNK))


def _unpacked_bf16(p):
    hi, lo = _unpack_bf16_pairs(p)
    return jnp.concatenate([hi.astype(BF16), lo.astype(BF16)], axis=1)


def _expert_kernel(be_ref, nu_ref, next_ref, slot_ref, xs_hbm, wg_hbm, wu_hbm, wd_hbm, y_ref,
                   g16_ref, u16_ref, d16_ref, gf_ref, uf_ref, df_ref, xbuf_ref, xsem, wsem, *, layer):
    i = pl.program_id(0)
    n_used = nu_ref[0]
    bm = xbuf_ref.shape[1]

    def fetch(b):
        slot = lax.rem(b, XS_SLOTS)
        rows = pl.ds(pl.multiple_of(b * bm, bm), bm)
        return pltpu.make_async_copy(xs_hbm.at[rows], xbuf_ref.at[slot], xsem.at[slot])

    @pl.when(i == 0)
    def _():
        fetch(0).start()

        @pl.when(n_used > 1)
        def _():
            fetch(1).start()

    @pl.when(i + 2 < n_used)
    def _():
        fetch(i + 2).start()

    e = be_ref[i]
    slot = slot_ref[e]
    f32_bufs = (gf_ref, uf_ref, df_ref)

    def wfetch(expert, dst_slot):
        return [pltpu.make_async_copy(w_hbm.at[layer, expert], buf.at[dst_slot], wsem.at[dst_slot, j])
                for j, (w_hbm, buf) in enumerate(zip((wg_hbm, wu_hbm, wd_hbm), f32_bufs))]

    @pl.when(i == 0)
    def _():
        for cp in wfetch(e, slot):
            cp.start()

    first_block_of_expert = jnp.logical_or(i == 0, e != be_ref[jnp.maximum(i - 1, 0)])

    @pl.when(jnp.logical_and(first_block_of_expert, i < n_used))
    def _():
        for cp in wfetch(e, slot):
            cp.wait()
        g16_ref[slot] = gf_ref[slot].astype(BF16)
        u16_ref[slot] = uf_ref[slot].astype(BF16)
        d16_ref[slot] = df_ref[slot].astype(BF16)
        nxt = next_ref[e]

        @pl.when(nxt >= 0)
        def _():
            for cp in wfetch(nxt, 1 - slot):
                cp.start()

    @pl.when(i < n_used)
    def _():
        fetch(i).wait()
        x = _unpacked_bf16(xbuf_ref[lax.rem(i, XS_SLOTS)])
        hh = _silu(_dot(x, g16_ref[slot])) * _dot(x, u16_ref[slot])
        y_ref[...] = _pack_bf16_pairs(_dot(hh.astype(BF16), d16_ref[slot]))


def _experts(block_e, n_used, next_expert, weight_slot, xs, wg, wu, wd, layer):
    nb = block_e.shape[0]
    bm = MOE_BLOCK
    hbm = pl.BlockSpec(memory_space=pl.ANY)
    two = lambda shape, dt: pltpu.VMEM((2,) + shape, dt)
    grid_spec = pltpu.PrefetchScalarGridSpec(
        num_scalar_prefetch=4,
        grid=(nb,),
        in_specs=[hbm, hbm, hbm, hbm],
        out_specs=pl.BlockSpec((bm, PACKED), lambda i, be, nu, nx, sl: (jnp.minimum(i, nu[0] - 1), 0)),
        scratch_shapes=[two((D_MODEL, D_EXPERT), BF16), two((D_MODEL, D_EXPERT), BF16), two((D_EXPERT, D_MODEL), BF16),
                        two((D_MODEL, D_EXPERT), F32), two((D_MODEL, D_EXPERT), F32), two((D_EXPERT, D_MODEL), F32),
                        pltpu.VMEM((XS_SLOTS, bm, PACKED), jnp.int32), pltpu.SemaphoreType.DMA((XS_SLOTS,)),
                        pltpu.SemaphoreType.DMA((2, 3))],
    )
    return pl.pallas_call(
        functools.partial(_expert_kernel, layer=layer),
        grid_spec=grid_spec,
        out_shape=jax.ShapeDtypeStruct((nb * bm, PACKED), jnp.int32),
        compiler_params=_params("arbitrary"),
        name="moe_experts",
    )(block_e, n_used, next_expert, weight_slot, xs, wg, wu, wd)


def _combine_kernel(g_ref, wk_ref, h_ref, xpk_ref, sg_ref, su_ref, sd_ref, lw_ref, lb_ref, *rest, with_proj):
    if with_proj:
        (wm_ref, ws_ref), (o_ref, proj_ref, small_ref) = rest[:2], rest[-3:]
        next_proj = (wm_ref, ws_ref, proj_ref, small_ref)
    else:
        o_ref, next_proj = rest[-1], None
    x = _unpacked_bf16(xpk_ref[...])
    hs = _silu(_dot(x, sg_ref[...])) * _dot(x, su_ref[...])
    shared = _dot(hs.astype(BF16), sd_ref[...])
    acc_hi = shared[:, :PACKED]
    acc_lo = shared[:, PACKED:]
    wk = wk_ref[...]
    for k in range(TOP_K):
        y_hi, y_lo = _unpack_bf16_pairs(g_ref[k])
        w = wk[:, k:k + 1]
        acc_hi = acc_hi + w * y_hi
        acc_lo = acc_lo + w * y_lo
    ffn = jnp.concatenate([acc_hi, acc_lo], axis=1)
    out = _layer_norm(ALPHA * h_ref[...] + ffn, lw_ref[...], lb_ref[...])
    o_ref[...] = out
    if next_proj is not None:
        wm_ref, ws_ref, proj_ref, small_ref = next_proj
        out16 = out.astype(BF16)
        proj_ref[...] = _dot(out16, wm_ref[...])
        small_ref[...] = _dot(out16, ws_ref[...])


def _combine_ln(g, wk, h, xpk, sg, su, sd, ln_w, ln_b, next_w, part, prev):
    t = h.shape[0]
    tm = COMBINE_TILE
    n_blk = g.shape[1] // tm
    first = part * n_blk
    const = lambda shape: pl.BlockSpec(shape, lambda i: (0, 0))
    rows = lambda width: pl.BlockSpec((tm, width), lambda i: (i + first, 0))
    in_specs = [pl.BlockSpec((TOP_K, tm, PACKED), lambda i: (0, i, 0)), rows(TOP_K), rows(D_MODEL), rows(PACKED),
                const((D_MODEL, D_EXPERT)), const((D_MODEL, D_EXPERT)), const((D_EXPERT, D_MODEL)),
                const((1, D_MODEL)), const((1, D_MODEL))]
    args = [g, wk, h, xpk, sg.astype(BF16), su.astype(BF16), sd.astype(BF16),
            ln_w.reshape(1, D_MODEL), ln_b.reshape(1, D_MODEL)]
    out_specs = [rows(D_MODEL)]
    out_shape = [jax.ShapeDtypeStruct((t, D_MODEL), F32)]
    if next_w is not None:
        w_main, w_small = next_w
        n = w_main.shape[1]
        in_specs += [const((D_MODEL, n)), const((D_MODEL, LANES))]
        args += [w_main, w_small]
        out_specs += [rows(n), rows(LANES)]
        out_shape += [jax.ShapeDtypeStruct((t, n), F32), jax.ShapeDtypeStruct((t, LANES), F32)]
    aliases = {}
    if prev is not None:
        aliases = {len(args) + k: k for k in range(len(prev))}
        in_specs += [pl.BlockSpec(memory_space=pl.ANY)] * len(prev)
        args += list(prev)
    return pl.pallas_call(
        functools.partial(_combine_kernel, with_proj=next_w is not None),
        grid=(n_blk,),
        in_specs=in_specs,
        out_specs=out_specs,
        out_shape=out_shape,
        input_output_aliases=aliases,
        compiler_params=_params("parallel"),
        name="moe_combine_ln",
    )(*args)


def _moe_ln(h, hpk, idx, rank, wk, counts, wg, wu, wd, layer, sg, su, sd, ln_w, ln_b, next_w):
    t = h.shape[0]
    cnt = counts[:, 0].astype(jnp.int32)
    padded = (cnt + MOE_BLOCK - 1) // MOE_BLOCK * MOE_BLOCK
    pend = jnp.cumsum(padded)
    experts = jnp.arange(N_EXPERTS, dtype=jnp.int32)
    pstart_of_pick = jnp.sum(jnp.where(idx[:, :, None] == experts, pend - padded, 0), axis=-1)
    pos = pstart_of_pick + rank
    nb = -(-(t * TOP_K + N_EXPERTS * (MOE_BLOCK - 1)) // MOE_BLOCK)
    starts = jnp.arange(nb, dtype=jnp.int32) * MOE_BLOCK
    block_e = jnp.minimum(jnp.sum((pend[None, :] <= starts[:, None]).astype(jnp.int32), axis=1), N_EXPERTS - 1)
    n_used = (pend[-1] // MOE_BLOCK).astype(jnp.int32).reshape(1)
    has_rows = cnt > 0
    later = jnp.logical_and(has_rows[None, :], experts[None, :] > experts[:, None])
    next_expert = jnp.min(jnp.where(later, experts[None, :], N_EXPERTS), axis=1)
    next_expert = jnp.where(next_expert == N_EXPERTS, -1, next_expert).astype(jnp.int32)
    weight_slot = ((jnp.cumsum(has_rows) - has_rows) % 2).astype(jnp.int32)
    pos_chunks = pos.reshape(TOP_K, t // SC_CHUNK, SC_CHUNK).transpose(1, 0, 2)
    xs = _dispatch_rows(hpk, pos_chunks, nb * MOE_BLOCK)
    ys = _experts(block_e, n_used, next_expert, weight_slot, xs, wg, wu, wd, layer)
    part = t // COMBINE_PARTS
    gathered = [_gather_rows(ys, pos[:, p * part:(p + 1) * part].reshape(-1)).reshape(TOP_K, part, PACKED)
                for p in range(COMBINE_PARTS)]
    outs = None
    for p in range(COMBINE_PARTS):
        outs = _combine_ln(gathered[p], wk, h, hpk, sg, su, sd, ln_w, ln_b, next_w, p, outs)
    return outs


def _pad_cols(w, width=LANES):
    return jnp.pad(w, ((0, 0), (0, width - w.shape[1])))


def _even_proj_weights(w_in):
    a4 = 4 * MIX_HALF
    ng = 2 * HEADS
    w_main = jnp.concatenate([w_in[:, :a4], w_in[:, a4 + ng:]], axis=1).astype(BF16)
    w_gate = _pad_cols(w_in[:, a4:a4 + ng]).astype(BF16)
    return w_main, w_gate


def _even_mixer(proj, gates, batch, seq, gate_b, norm_w, conv_w, conv_b, wa, ba, wx, bx, lam):
    ya = _mlstm(proj, gates, gate_b, norm_w, batch, seq)
    yb = _rglru(proj, conv_w, conv_b, wa, ba, wx, bx, lam, batch, seq)
    return ya, yb


def _odd_proj_weights(w_in):
    c0 = MIX_HALF
    c1 = c0 + HEADS * GLA_DK
    c2 = c1 + HEADS * GLA_DK
    c3 = c2 + MIX_HALF
    c4 = c3 + MIX_HALF
    w_main = jnp.concatenate([w_in[:, :c0], _pad_heads(w_in[:, c0:c1], 1), _pad_heads(w_in[:, c1:c2], 1),
                              w_in[:, c2:c4]], axis=1).astype(BF16)
    w_low = _pad_cols(w_in[:, c4:]).astype(BF16)
    return w_main, w_low


def _odd_mixer(proj, glow, batch, seq, lam_re, lam_im, b_re, b_im, c_re, c_im, d_skip, log_dt,
               glu_w, glu_b, gate_w, gate_b, norm_w):
    tables = _s5_tables(lam_re, lam_im, b_re, b_im, c_re, c_im, log_dt)
    yc = _s5(proj, tables, d_skip, glu_w, glu_b, batch, seq)
    gw = jnp.pad(_pad_heads(gate_w, 1), ((0, LANES - GLA_GATE_RANK), (0, 0))).astype(BF16)
    gb = _pad_heads(gate_b.reshape(1, -1), 1)
    yd = _gla(proj, glow, gw, gb, norm_w, batch, seq)
    return yc, yd


def kernel(x, ln1_w, ln1_b, ln2_w, ln2_b, w_out, w_in_even, mlstm_gate_b, mlstm_norm_w, lru_conv_w, lru_conv_b, lru_wa, lru_ba, lru_wx, lru_bx, lru_lambda, w_in_odd, s5_lam_re, s5_lam_im, s5_b_re, s5_b_im, s5_c_re, s5_c_im, s5_d, s5_log_dt, s5_glu_w, s5_glu_b, gla_gate_w, gla_gate_b, gla_norm_w, router_w, router_bias, exp_w_gate, exp_w_up, exp_w_down, sh_w_gate, sh_w_up, sh_w_down):
    batch, seq, d = x.shape
    proj_w = [_even_proj_weights(w_in_even[layer // 2]) if layer % 2 == 0 else _odd_proj_weights(w_in_odd[layer // 2])
              for layer in range(DEPTH)]
    h = x.reshape(batch * seq, d)
    proj, small = _proj(h, *proj_w[0])
    for layer in range(DEPTH):
        j = layer // 2
        if layer % 2 == 0:
            y1, y2 = _even_mixer(proj, small, batch, seq, mlstm_gate_b[j], mlstm_norm_w[j],
                                 lru_conv_w[j], lru_conv_b[j], lru_wa[j], lru_ba[j], lru_wx[j],
                                 lru_bx[j], lru_lambda[j])
        else:
            y1, y2 = _odd_mixer(proj, small, batch, seq, s5_lam_re[j], s5_lam_im[j], s5_b_re[j],
                                s5_b_im[j], s5_c_re[j], s5_c_im[j], s5_d[j], s5_log_dt[j],
                                s5_glu_w[j], s5_glu_b[j], gla_gate_w[j], gla_gate_b[j], gla_norm_w[j])
        h, hpk, idx, rank, wk, counts = _out_proj_ln_route(y1, y2, h, w_out[layer], ln1_w[layer], ln1_b[layer],
                                                           router_w[layer], router_bias[layer])
        next_w = proj_w[layer + 1] if layer + 1 < DEPTH else None
        res = _moe_ln(h, hpk, idx, rank, wk, counts, exp_w_gate, exp_w_up, exp_w_down, layer,
                      sh_w_gate[layer], sh_w_up[layer], sh_w_down[layer], ln2_w[layer], ln2_b[layer], next_w)
        if next_w is None:
            (h,) = res
        else:
            h, proj, small = res
    return h.reshape(batch, seq, d)
```

```python
import functools
import math

import jax
import jax.numpy as jnp
from jax import lax
from jax.experimental import pallas as pl
from jax.experimental.pallas import tpu as pltpu
from jax.experimental.pallas import tpu_sc as plsc

F32 = jnp.float32
BF16 = jnp.bfloat16

D_MODEL = 1024
DEPTH = 2
MIX_HALF = 512
HEADS = 4
HEAD_DIM = 128
GLA_DK = 64
GLA_CHUNK = 64
GLA_GATE_RANK = 16
GLA_GATE_TEMP = 16.0
LRU_C = 8.0
LRU_CONV = 4
S5_GROUP = 16
S5_GROUPS = 32
S5_STATE = 64
S5_LANES = S5_GROUPS * S5_STATE
S5_BLOCKS = 4
N_EXPERTS = 64
N_GROUPS = 8
GROUP_SIZE = N_EXPERTS // N_GROUPS
TOP_K = 8
TOPK_GROUPS = 4
D_EXPERT = 256
ROUTED_SCALE = 2.5
ALPHA = (2.0 * DEPTH) ** 0.25
EPS = 1e-5
LANES = 128
SUBLANES = 8
NEG_INF = float("-inf")

VMEM_LIMIT = 56 * 1024 * 1024

MLSTM_CHUNK = 128
MLSTM_TILE = 1024
LRU_TILE = 1024
LRU_LOG_STEPS = 3
LRU_UNROLL = 4
S5_TILE = 512
S5_LOG_STEPS = 3
S5_UNROLL = True
GLA_UNROLL = 4
GLA_TILE = 1024
PROJ_TILE = 1024
OUT_TILE = 512
MOE_BLOCK = 1152
XS_SLOTS = 3
COMBINE_TILE = 256
COMBINE_PARTS = 2
PACKED = D_MODEL // 2
SC_CHUNK = 64
SC_CORES = 2
SC_SUBCORES = 16
SC_WORKERS = SC_CORES * SC_SUBCORES


def _params(*sem):
    return pltpu.CompilerParams(dimension_semantics=sem, vmem_limit_bytes=VMEM_LIMIT)


def _split3(x):
    hi = x.astype(BF16)
    r1 = x - hi.astype(F32)
    mid = r1.astype(BF16)
    lo = (r1 - mid.astype(F32)).astype(BF16)
    return hi, mid, lo


def _dot(a, b):
    return jnp.dot(a, b, preferred_element_type=F32)


def _dot_nt(a, b):
    return lax.dot_general(a, b, (((1,), (1,)), ((), ())), preferred_element_type=F32)


def _dot_tn(a, b):
    return lax.dot_general(a, b, (((0,), (0,)), ((), ())), preferred_element_type=F32)


def _exact_left01(mask01_bf16, x):
    hi, mid, lo = _split3(x)
    return _dot(mask01_bf16, hi) + _dot(mask01_bf16, mid) + _dot(mask01_bf16, lo)


def _exact_right01(x, mask01_bf16):
    hi, mid, lo = _split3(x)
    return _dot(hi, mask01_bf16) + _dot(mid, mask01_bf16) + _dot(lo, mask01_bf16)


def _log_sigmoid(x):
    return jnp.minimum(x, 0.0) - jnp.log(1.0 + jnp.exp(-jnp.abs(x)))


def _sigmoid(x):
    return 1.0 / (1.0 + jnp.exp(-x))


def _gelu_tanh(x):
    c = math.sqrt(2.0 / math.pi)
    return 0.5 * x * (1.0 + jnp.tanh(c * (x + 0.044715 * (x * x * x))))


def _layer_norm(z, w, b):
    mu = jnp.mean(z, axis=-1, keepdims=True)
    zc = z - mu
    return zc * lax.rsqrt(jnp.mean(zc * zc, axis=-1, keepdims=True) + EPS) * w + b


def _proj_kernel(x_ref, w_ref, wg_ref, o_ref, og_ref):
    x = x_ref[...].astype(BF16)
    o_ref[...] = _dot(x, w_ref[...])
    og_ref[...] = _dot(x, wg_ref[...])


def _proj(x, w_main, w_small):
    t, d = x.shape
    n = w_main.shape[1]
    tm = PROJ_TILE
    return pl.pallas_call(
        _proj_kernel,
        grid=(t // tm,),
        in_specs=[pl.BlockSpec((tm, d), lambda i: (i, 0)),
                  pl.BlockSpec((d, n), lambda i: (0, 0)),
                  pl.BlockSpec((d, LANES), lambda i: (0, 0))],
        out_specs=[pl.BlockSpec((tm, n), lambda i: (i, 0)),
                   pl.BlockSpec((tm, LANES), lambda i: (i, 0))],
        out_shape=[jax.ShapeDtypeStruct((t, n), F32), jax.ShapeDtypeStruct((t, LANES), F32)],
        compiler_params=_params("parallel"),
        name="in_proj",
    )(x, w_main, w_small)


def _mlstm_kernel(q_ref, k_ref, v_ref, o_ref, gc_ref, gr_ref, bc_ref, br_ref, nw_ref,
                  y_ref, c_ref, m_ref, *, chunk, n_chunks):
    L = chunk

    @pl.when(pl.program_id(1) == 0)
    def _():
        c_ref[...] = jnp.zeros_like(c_ref)
        m_ref[...] = jnp.zeros_like(m_ref)

    ri = lax.broadcasted_iota(jnp.int32, (L, L), 0)
    ci = lax.broadcasted_iota(jnp.int32, (L, L), 1)
    causal = ci <= ri
    tril = causal.astype(BF16)
    triu = (ri <= ci).astype(BF16)
    ones_v = jnp.ones((L, HEAD_DIM), BF16)
    scale = HEAD_DIM ** -0.5

    def body(c, carry):
        r0 = pl.multiple_of(c * L, L)
        g_col = gc_ref[pl.ds(r0, L), :] + bc_ref[...]
        g_row = gr_ref[c] + br_ref[...]
        b_col_all = _exact_left01(tril, _log_sigmoid(g_col))
        b_row_all = _exact_right01(_log_sigmoid(g_row), triu)
        for h in range(HEADS):
            lo = h * HEAD_DIM
            q = q_ref[pl.ds(r0, L), lo:lo + HEAD_DIM].astype(BF16)
            k = k_ref[pl.ds(r0, L), lo:lo + HEAD_DIM] * scale
            v = v_ref[pl.ds(r0, L), lo:lo + HEAD_DIM].astype(BF16)
            v_aug = jnp.concatenate([v, ones_v], axis=1)
            i_rep = jnp.broadcast_to(g_col[:, h:h + 1], (L, LANES))
            b_rep = jnp.broadcast_to(b_col_all[:, HEADS + h:HEADS + h + 1], (L, LANES))
            i_row = g_row[h:h + 1, :]
            b_row = b_row_all[HEADS + h:HEADS + h + 1, :]
            b_last = b_rep[L - 1:L, :]
            m_prev = m_ref[h:h + 1, :]
            c_prev = c_ref[h]

            d_mat = jnp.where(causal, b_rep - b_row + i_row, NEG_INF)
            m_inter = b_rep + m_prev
            m_i = jnp.maximum(m_inter, jnp.max(d_mat, axis=1, keepdims=True))
            s = _dot_nt(q, k.astype(BF16)) * jnp.exp(d_mat - m_i)
            w_inter = jnp.exp(m_inter - m_i)
            intra = _dot(s.astype(BF16), v_aug)
            inter = _dot(q, c_prev.astype(BF16))
            num = intra[:, :HEAD_DIM] + w_inter * inter[:, :HEAD_DIM]
            den = intra[:, HEAD_DIM:] + w_inter * inter[:, HEAD_DIM:]
            hh = num / jnp.maximum(jnp.abs(den), jnp.exp(-m_i))

            w_loc = b_last - b_rep + i_rep
            m_loc = jnp.max(w_loc, axis=0, keepdims=True)
            kp = (k * jnp.exp(w_loc - m_loc)).astype(BF16)
            c_loc = _dot_tn(kp, v_aug)
            m_new = jnp.maximum(b_last + m_prev, m_loc)
            keep = jnp.exp(b_last + m_prev - m_new)
            add = jnp.exp(m_loc - m_new)
            c_ref[h] = (jnp.concatenate([keep, keep], axis=1) * c_prev
                        + jnp.concatenate([add, add], axis=1) * c_loc)
            m_ref[h:h + 1, :] = m_new

            hc = hh - jnp.mean(hh, axis=-1, keepdims=True)
            yn = hc * lax.rsqrt(jnp.mean(hc * hc, axis=-1, keepdims=True) + EPS)
            og = o_ref[pl.ds(r0, L), lo:lo + HEAD_DIM]
            y_ref[pl.ds(r0, L), lo:lo + HEAD_DIM] = (yn * nw_ref[:, lo:lo + HEAD_DIM] * _sigmoid(og)).astype(BF16)
        return carry

    lax.fori_loop(0, n_chunks, body, 0)


def _mlstm(proj, gates, gate_b, norm_w, batch, seq):
    t = batch * seq
    L = MLSTM_CHUNK
    assert L == LANES, "the kernel keeps per-row gate terms replicated over one vreg of lanes"
    ts = MLSTM_TILE
    nj = seq // ts
    nc = ts // L
    g_row = gates[:, :2 * HEADS].reshape(t // L, L, 2 * HEADS).transpose(0, 2, 1)
    b_col = jnp.zeros((1, LANES), F32).at[0, :2 * HEADS].set(gate_b)
    b_row = gate_b.reshape(2 * HEADS, 1)
    blk = lambda col: pl.BlockSpec((ts, MIX_HALF), lambda b, j, col=col: (b * nj + j, col))
    kern = functools.partial(_mlstm_kernel, chunk=L, n_chunks=nc)
    return pl.pallas_call(
        kern,
        grid=(batch, nj),
        in_specs=[blk(0), blk(1), blk(2), blk(3),
                  pl.BlockSpec((ts, LANES), lambda b, j: (b * nj + j, 0)),
                  pl.BlockSpec((nc, 2 * HEADS, L), lambda b, j: (b * nj + j, 0, 0)),
                  pl.BlockSpec((1, LANES), lambda b, j: (0, 0)),
                  pl.BlockSpec((2 * HEADS, 1), lambda b, j: (0, 0)),
                  pl.BlockSpec((1, MIX_HALF), lambda b, j: (0, 0))],
        out_specs=pl.BlockSpec((ts, MIX_HALF), lambda b, j: (b * nj + j, 0)),
        out_shape=jax.ShapeDtypeStruct((t, MIX_HALF), BF16),
        scratch_shapes=[pltpu.VMEM((HEADS, HEAD_DIM, 2 * HEAD_DIM), F32),
                        pltpu.VMEM((8, LANES), F32)],
        compiler_params=_params("arbitrary", "arbitrary"),
        name="mlstm",
    )(proj, proj, proj, proj, gates, g_row, b_col, b_row, norm_w.reshape(1, MIX_HALF))


def _rglru_kernel(xb_ref, gb_ref, cw_ref, cb_ref, wa_ref, ba_ref, wx_ref, bx_ref, lam_ref,
                  y_ref, xext_ref, h_ref, a_ref, u_ref, *, tile):
    @pl.when(pl.program_id(1) == 0)
    def _():
        xext_ref[0:8, :] = jnp.zeros((8, MIX_HALF), F32)
        h_ref[...] = jnp.zeros_like(h_ref)

    x = xb_ref[...]
    xext_ref[8:8 + tile, :] = x
    xc = cb_ref[...] + cw_ref[LRU_CONV - 1:LRU_CONV, :] * x
    for tap in range(LRU_CONV - 1):
        back = LRU_CONV - 1 - tap
        xc = xc + cw_ref[tap:tap + 1, :] * xext_ref[8 - back:8 - back + tile, :]
    xext_ref[0:8, :] = x[tile - 8:tile, :]

    xc16 = xc.astype(BF16)
    r_parts, i_parts = [], []
    for h in range(HEADS):
        lo = h * HEAD_DIM
        xh = xc16[:, lo:lo + HEAD_DIM]
        r_parts.append(_dot(xh, wa_ref[h]))
        i_parts.append(_dot(xh, wx_ref[h]))
    r = _sigmoid(jnp.concatenate(r_parts, axis=1) + ba_ref[...])
    ig = _sigmoid(jnp.concatenate(i_parts, axis=1) + bx_ref[...])
    lam = lam_ref[...]
    softplus_neg = jnp.maximum(-lam, 0.0) + jnp.log(1.0 + jnp.exp(-jnp.abs(lam)))
    log_a = -LRU_C * r * softplus_neg
    a = jnp.exp(log_a)
    th = jnp.tanh(log_a)
    u = jnp.sqrt(-2.0 * th / (1.0 - th)) * ig * xc

    a_ref[...] = a
    u_ref[...] = u
    rows = lax.broadcasted_iota(jnp.int32, (SUBLANES, MIX_HALF), 0)

    def group(i, h_prev):
        r0 = pl.multiple_of(i * SUBLANES, SUBLANES)
        ag = a_ref[pl.ds(r0, SUBLANES), :]
        ug = u_ref[pl.ds(r0, SUBLANES), :]
        for k in range(LRU_LOG_STEPS):
            keep = rows >= (1 << k)
            ug = ag * jnp.where(keep, pltpu.roll(ug, 1 << k, 0), 0.0) + ug
            ag = ag * jnp.where(keep, pltpu.roll(ag, 1 << k, 0), 1.0)
        hg = ug + ag * h_prev
        u_ref[pl.ds(r0, SUBLANES), :] = hg
        return hg[SUBLANES - 1:SUBLANES, :]

    h_last = lax.fori_loop(0, tile // SUBLANES, group, h_ref[0:1, :], unroll=LRU_UNROLL)
    h_ref[...] = jnp.broadcast_to(h_last, h_ref.shape)
    y_ref[...] = (u_ref[...] * _gelu_tanh(gb_ref[...])).astype(BF16)


def _rglru(proj, conv_w, conv_b, wa, ba, wx, bx, lam, batch, seq):
    t = batch * seq
    ts = LRU_TILE
    nj = seq // ts
    row = lambda a: a.reshape(1, MIX_HALF)
    const2 = lambda shape: pl.BlockSpec(shape, lambda b, j: (0, 0))
    const3 = lambda shape: pl.BlockSpec(shape, lambda b, j: (0, 0, 0))
    blk = lambda col: pl.BlockSpec((ts, MIX_HALF), lambda b, j, col=col: (b * nj + j, col))
    return pl.pallas_call(
        functools.partial(_rglru_kernel, tile=ts),
        grid=(batch, nj),
        in_specs=[blk(4), blk(5), const2((LRU_CONV, MIX_HALF)), const2((1, MIX_HALF)),
                  const3((HEADS, HEAD_DIM, HEAD_DIM)), const2((1, MIX_HALF)),
                  const3((HEADS, HEAD_DIM, HEAD_DIM)), const2((1, MIX_HALF)), const2((1, MIX_HALF))],
        out_specs=pl.BlockSpec((ts, MIX_HALF), lambda b, j: (b * nj + j, 0)),
        out_shape=jax.ShapeDtypeStruct((t, MIX_HALF), BF16),
        scratch_shapes=[pltpu.VMEM((ts + 8, MIX_HALF), F32), pltpu.VMEM((8, MIX_HALF), F32),
                        pltpu.VMEM((ts, MIX_HALF), F32), pltpu.VMEM((ts, MIX_HALF), F32)],
        compiler_params=_params("arbitrary", "arbitrary"),
        name="rglru",
    )(proj, proj, conv_w, row(conv_b), wa.astype(BF16), row(ba), wx.astype(BF16), row(bx), row(lam))


def _s5_kernel(u_ref, bre_ref, bim_ref, cre_ref, cim_ref, mre_ref, mim_ref, pre_ref, pim_ref, d_ref, gw_ref,
               gb_ref, y_ref, xr_ref, xi_ref, cr_ref, ci_ref, *, tile):
    @pl.when(pl.program_id(1) == 0)
    def _():
        cr_ref[...] = jnp.zeros_like(cr_ref)
        ci_ref[...] = jnp.zeros_like(ci_ref)

    u = u_ref[...]
    u16 = u.astype(BF16)
    blk_c = MIX_HALF // S5_BLOCKS
    blk_s = S5_LANES // S5_BLOCKS
    parts = []
    for j in range(S5_BLOCKS):
        lanes = slice(j * blk_s, (j + 1) * blk_s)
        uj = u16[:, j * blk_c:(j + 1) * blk_c]
        xr_ref[:, lanes] = _dot(uj, bre_ref[j])
        xi_ref[:, lanes] = _dot(uj, bim_ref[j])

        def group(i, carry, lanes=lanes):
            cr, ci = carry
            r0 = pl.multiple_of(i * SUBLANES, SUBLANES)
            xr = xr_ref[pl.ds(r0, SUBLANES), lanes]
            xi = xi_ref[pl.ds(r0, SUBLANES), lanes]
            for k in range(S5_LOG_STEPS):
                sr = pltpu.roll(xr, 1 << k, 0)
                si = pltpu.roll(xi, 1 << k, 0)
                mr = mre_ref[k, :, lanes]
                mi = mim_ref[k, :, lanes]
                xr, xi = xr + mr * sr - mi * si, xi + mr * si + mi * sr
            pr = pre_ref[:, lanes]
            pi = pim_ref[:, lanes]
            xr, xi = xr + pr * cr - pi * ci, xi + pr * ci + pi * cr
            xr_ref[pl.ds(r0, SUBLANES), lanes] = xr
            xi_ref[pl.ds(r0, SUBLANES), lanes] = xi
            return xr[SUBLANES - 1:SUBLANES, :], xi[SUBLANES - 1:SUBLANES, :]

        cr, ci = lax.fori_loop(0, tile // SUBLANES, group, (cr_ref[0:1, lanes], ci_ref[0:1, lanes]),
                               unroll=S5_UNROLL)
        cr_ref[0:1, lanes] = cr
        ci_ref[0:1, lanes] = ci
        parts.append(_dot(xr_ref[:, lanes].astype(BF16), cre_ref[j])
                     - _dot(xi_ref[:, lanes].astype(BF16), cim_ref[j]))
    y = jnp.concatenate(parts, axis=1) + d_ref[...] * u
    g = _gelu_tanh(y)
    y_ref[...] = (g * _sigmoid(_dot(g.astype(BF16), gw_ref[...]) + gb_ref[...])).astype(BF16)


def _s5_tables(lam_re, lam_im, b_re, b_im, c_re, c_im, log_dt):
    lr, li = lam_re.astype(F32), lam_im.astype(F32)
    dt = jnp.exp(log_dt.astype(F32))[:, None]
    mag = jnp.exp(lr * dt)
    abar_re = mag * jnp.cos(li * dt)
    abar_im = mag * jnp.sin(li * dt)
    den = lr * lr + li * li
    nr = abar_re - 1.0
    coef_re = (nr * lr + abar_im * li) / den
    coef_im = (abar_im * lr - nr * li) / den
    bbar_re = coef_re[..., None] * b_re - coef_im[..., None] * b_im
    bbar_im = coef_re[..., None] * b_im + coef_im[..., None] * b_re
    gpb = S5_GROUPS // S5_BLOCKS
    eye = jnp.eye(gpb, dtype=F32)

    def in_map(bb):
        bb = bb.reshape(S5_BLOCKS, gpb, S5_STATE, S5_GROUP)
        return jnp.einsum("jgph,gk->jghkp", bb, eye).reshape(S5_BLOCKS, gpb * S5_GROUP, gpb * S5_STATE)

    def out_map(cc):
        cc = cc.reshape(S5_BLOCKS, gpb, S5_GROUP, S5_STATE)
        return jnp.einsum("jghp,gk->jgpkh", cc, eye).reshape(S5_BLOCKS, gpb * S5_STATE, gpb * S5_GROUP)

    def power(n):
        n = jnp.asarray(n, F32)[..., None, None]
        pmag = jnp.exp(n * (lr * dt))
        shape = n.shape[:-2] + (S5_LANES,)
        return (pmag * jnp.cos(n * (li * dt))).reshape(shape), (pmag * jnp.sin(n * (li * dt))).reshape(shape)

    row = jnp.arange(SUBLANES)
    step = 2 ** jnp.arange(S5_LOG_STEPS)
    s_re, s_im = power(step)
    keep = (row[None, :] >= step[:, None])[..., None]
    m_re = jnp.where(keep, s_re[:, None, :], 0.0)
    m_im = jnp.where(keep, s_im[:, None, :], 0.0)
    p_re, p_im = power(row + 1)
    return (in_map(bbar_re).astype(BF16), in_map(bbar_im).astype(BF16),
            out_map(c_re.astype(F32)).astype(BF16), out_map(c_im.astype(F32)).astype(BF16),
            m_re, m_im, p_re, p_im)


def _s5(proj, tables, d_skip, glu_w, glu_b, batch, seq):
    t = batch * seq
    ts = S5_TILE
    nj = seq // ts
    bre, bim, cre, cim, m_re, m_im, p_re, p_im = tables
    blk_c = MIX_HALF // S5_BLOCKS
    blk_s = S5_LANES // S5_BLOCKS
    const2 = lambda shape: pl.BlockSpec(shape, lambda b, j: (0, 0))
    const3 = lambda shape: pl.BlockSpec(shape, lambda b, j: (0, 0, 0))
    return pl.pallas_call(
        functools.partial(_s5_kernel, tile=ts),
        grid=(batch, nj),
        in_specs=[pl.BlockSpec((ts, MIX_HALF), lambda b, j: (b * nj + j, 0)),
                  const3((S5_BLOCKS, blk_c, blk_s)), const3((S5_BLOCKS, blk_c, blk_s)),
                  const3((S5_BLOCKS, blk_s, blk_c)), const3((S5_BLOCKS, blk_s, blk_c)),
                  const3(m_re.shape), const3(m_im.shape), const2(p_re.shape), const2(p_im.shape),
                  const2((1, MIX_HALF)), const2((MIX_HALF, MIX_HALF)), const2((1, MIX_HALF))],
        out_specs=pl.BlockSpec((ts, MIX_HALF), lambda b, j: (b * nj + j, 0)),
        out_shape=jax.ShapeDtypeStruct((t, MIX_HALF), BF16),
        scratch_shapes=[pltpu.VMEM((ts, S5_LANES), F32), pltpu.VMEM((ts, S5_LANES), F32),
                        pltpu.VMEM((8, S5_LANES), F32), pltpu.VMEM((8, S5_LANES), F32)],
        compiler_params=_params("arbitrary", "arbitrary"),
        name="s5",
    )(proj, bre, bim, cre, cim, m_re, m_im, p_re, p_im, d_skip.reshape(1, MIX_HALF), glu_w.astype(BF16),
      glu_b.reshape(1, MIX_HALF))


def _gla_kernel(q_ref, k_ref, v_ref, r_ref, gl_ref, gw_ref, gb_ref, nw_ref, y_ref,
                st_ref, qd_ref, ki_ref, ke_ref, v16_ref, dec_ref, o_ref, *, tile, chunk):
    L = chunk
    nc = tile // L

    @pl.when(pl.program_id(1) == 0)
    def _():
        st_ref[...] = jnp.zeros_like(st_ref)

    z = _dot(gl_ref[...].astype(BF16), gw_ref[...]) + gb_ref[...]
    bcum = _log_sigmoid(z) * (1.0 / GLA_GATE_TEMP)
    row_in_chunk = lax.broadcasted_iota(jnp.int32, bcum.shape, 0) & (L - 1)
    s = 1
    while s < L:
        bcum = bcum + jnp.where(row_in_chunk >= s, pltpu.roll(bcum, s, 0), 0.0)
        s *= 2
    b3 = bcum.reshape(nc, L, MIX_HALF)
    b_last = b3[:, L - 1:L, :]
    k = k_ref[...]
    qd_ref[...] = (q_ref[...] * (GLA_DK ** -0.5) * jnp.exp(bcum)).astype(BF16)
    ki_ref[...] = (k * jnp.exp(-bcum)).astype(BF16)
    ke_ref[...] = (k.reshape(nc, L, MIX_HALF) * jnp.exp(b_last - b3)).reshape(tile, MIX_HALF).astype(BF16)
    v16_ref[...] = v_ref[...].astype(BF16)
    dec_ref[...] = jnp.exp(b_last)

    ri = lax.broadcasted_iota(jnp.int32, (L, L), 0)
    ci = lax.broadcasted_iota(jnp.int32, (L, L), 1)
    causal = ci <= ri

    def body(c, carry):
        r0 = pl.multiple_of(c * L, L)
        dec = dec_ref[c]
        for h in range(HEADS):
            lo = h * HEAD_DIM
            q_dec = qd_ref[pl.ds(r0, L), lo:lo + HEAD_DIM]
            v = v16_ref[pl.ds(r0, L), lo:lo + HEAD_DIM]
            st = st_ref[h]
            att = jnp.where(causal, _dot_nt(q_dec, ki_ref[pl.ds(r0, L), lo:lo + HEAD_DIM]), 0.0)
            o_ref[pl.ds(r0, L), lo:lo + HEAD_DIM] = (_dot(att.astype(BF16), v)
                                                     + _dot_nt(q_dec, st.astype(BF16)))
            st_ref[h] = dec[:, lo:lo + HEAD_DIM] * st + _dot_tn(v, ke_ref[pl.ds(r0, L), lo:lo + HEAD_DIM])
        return carry

    lax.fori_loop(0, nc, body, 0, unroll=GLA_UNROLL)

    rg = r_ref[...]
    gate = nw_ref[...] * (rg * _sigmoid(rg))
    for h in range(HEADS):
        lo = h * HEAD_DIM
        o = o_ref[:, lo:lo + HEAD_DIM]
        yn = o * lax.rsqrt(jnp.mean(o * o, axis=-1, keepdims=True) + EPS)
        y_ref[:, lo:lo + HEAD_DIM] = (yn * gate[:, lo:lo + HEAD_DIM]).astype(BF16)


def _gla(proj, glow, gate_w, gate_b, norm_w, batch, seq):
    t = batch * seq
    ts = GLA_TILE
    nj = seq // ts
    blk = lambda col: pl.BlockSpec((ts, MIX_HALF), lambda b, j, col=col: (b * nj + j, col))
    const2 = lambda shape: pl.BlockSpec(shape, lambda b, j: (0, 0))
    return pl.pallas_call(
        functools.partial(_gla_kernel, tile=ts, chunk=GLA_CHUNK),
        grid=(batch, nj),
        in_specs=[blk(1), blk(2), blk(3), blk(4),
                  pl.BlockSpec((ts, LANES), lambda b, j: (b * nj + j, 0)),
                  const2((LANES, MIX_HALF)), const2((1, MIX_HALF)), const2((1, MIX_HALF))],
        out_specs=pl.BlockSpec((ts, MIX_HALF), lambda b, j: (b * nj + j, 0)),
        out_shape=jax.ShapeDtypeStruct((t, MIX_HALF), BF16),
        scratch_shapes=[pltpu.VMEM((HEADS, HEAD_DIM, HEAD_DIM), F32),
                        pltpu.VMEM((ts, MIX_HALF), BF16), pltpu.VMEM((ts, MIX_HALF), BF16),
                        pltpu.VMEM((ts, MIX_HALF), BF16), pltpu.VMEM((ts, MIX_HALF), BF16),
                        pltpu.VMEM((ts // GLA_CHUNK, 1, MIX_HALF), F32),
                        pltpu.VMEM((ts, MIX_HALF), F32)],
        compiler_params=_params("arbitrary", "arbitrary"),
        name="gla",
    )(proj, proj, proj, proj, glow, gate_w, gate_b, norm_w.reshape(1, MIX_HALF))


def _pad_heads(w, axis):
    shape = list(w.shape)
    shape[axis:axis + 1] = [HEADS, GLA_DK]
    w = w.reshape(shape)
    pad = [(0, 0)] * w.ndim
    pad[axis + 1] = (0, HEAD_DIM - GLA_DK)
    w = jnp.pad(w, pad)
    shape[axis:axis + 2] = [HEADS * HEAD_DIM]
    return w.reshape(shape)


def _pack_bf16_pairs(z):
    hi = lax.bitcast_convert_type(z[:, :PACKED].astype(BF16).astype(F32), jnp.uint32)
    lo = lax.bitcast_convert_type(z[:, PACKED:].astype(BF16).astype(F32), jnp.uint32)
    word = (hi & jnp.uint32(0xFFFF0000)) | lax.shift_right_logical(lo, jnp.uint32(16))
    return lax.bitcast_convert_type(word, jnp.int32)


def _unpack_bf16_pairs(p):
    word = lax.bitcast_convert_type(p, jnp.uint32)
    hi = lax.bitcast_convert_type(word & jnp.uint32(0xFFFF0000), F32)
    lo = lax.bitcast_convert_type(lax.shift_left(word, jnp.uint32(16)), F32)
    return hi, lo


def _out_kernel(ya_ref, yb_ref, h_ref, w_ref, lw_ref, lb_ref, rw_ref, rb_ref,
                o_ref, opk_ref, idx_ref, rank_ref, wk_ref, cnt_ref, base_ref, *, tile):
    mixed = jnp.concatenate([ya_ref[...], yb_ref[...]], axis=1)
    z = ALPHA * h_ref[...] + _dot(mixed, w_ref[...])
    out = _layer_norm(z, lw_ref[...], lb_ref[...])
    o_ref[...] = out
    opk_ref[...] = _pack_bf16_pairs(out)
    _route_tile(out, rw_ref, rb_ref, idx_ref, rank_ref, wk_ref, cnt_ref, base_ref, tile)


def _out_proj_ln_route(ya, yb, h, w_out, ln_w, ln_b, router_w, router_bias):
    t = h.shape[0]
    tm = OUT_TILE
    const = lambda shape: pl.BlockSpec(shape, lambda i: (0, 0))
    per_tok = lambda dt: jax.ShapeDtypeStruct((TOP_K, t), dt)
    tok_blk = pl.BlockSpec((TOP_K, tm), lambda i: (0, i))
    return pl.pallas_call(
        functools.partial(_out_kernel, tile=tm),
        grid=(t // tm,),
        in_specs=[pl.BlockSpec((tm, MIX_HALF), lambda i: (i, 0)),
                  pl.BlockSpec((tm, MIX_HALF), lambda i: (i, 0)),
                  pl.BlockSpec((tm, D_MODEL), lambda i: (i, 0)),
                  const((D_MODEL, D_MODEL)), const((1, D_MODEL)), const((1, D_MODEL)),
                  const((N_EXPERTS, D_MODEL)), const((N_EXPERTS, 1))],
        out_specs=[pl.BlockSpec((tm, D_MODEL), lambda i: (i, 0)),
                   pl.BlockSpec((tm, PACKED), lambda i: (i, 0)),
                   tok_blk, tok_blk, pl.BlockSpec((tm, TOP_K), lambda i: (i, 0)), const((N_EXPERTS, LANES))],
        out_shape=[jax.ShapeDtypeStruct((t, D_MODEL), F32), jax.ShapeDtypeStruct((t, PACKED), jnp.int32),
                   per_tok(jnp.int32), per_tok(jnp.int32), jax.ShapeDtypeStruct((t, TOP_K), F32),
                   jax.ShapeDtypeStruct((N_EXPERTS, LANES), F32)],
        scratch_shapes=[pltpu.VMEM((N_EXPERTS, LANES), F32)],
        compiler_params=_params("arbitrary"),
        name="out_proj_ln_route",
    )(ya, yb, h, w_out.astype(BF16), ln_w.reshape(1, D_MODEL), ln_b.reshape(1, D_MODEL),
      router_w.T, router_bias.reshape(N_EXPERTS, 1))


def _first_index(hit, idx, big):
    return jnp.min(jnp.where(hit, idx, big), axis=0, keepdims=True)


def _route_tile(h, w_ref, b_ref, idx_ref, rank_ref, wk_ref, cnt_ref, base_ref, tile):
    @pl.when(pl.program_id(0) == 0)
    def _():
        base_ref[...] = jnp.zeros_like(base_ref)

    h_hi, h_mid, _ = _split3(h)
    w_hi, w_mid, _ = _split3(w_ref[...])
    logits = _dot_nt(w_hi, h_hi) + _dot_nt(w_hi, h_mid) + _dot_nt(w_mid, h_hi)
    scores = _sigmoid(logits)
    biased = scores + b_ref[...]

    sub = lax.broadcasted_iota(jnp.int32, (GROUP_SIZE, tile), 0)
    grp_rows = []
    for g in range(N_GROUPS):
        xg = biased[g * GROUP_SIZE:(g + 1) * GROUP_SIZE, :]
        m1 = jnp.max(xg, axis=0, keepdims=True)
        i1 = _first_index(xg == m1, sub, GROUP_SIZE)
        m2 = jnp.max(jnp.where(sub == i1, NEG_INF, xg), axis=0, keepdims=True)
        grp_rows.append(m1 + m2)
    gs = jnp.concatenate(grp_rows, axis=0)
    gsel = jnp.zeros((N_GROUPS, tile), F32)
    for _ in range(TOPK_GROUPS):
        mx = jnp.max(gs, axis=0, keepdims=True)
        hit = sub == _first_index(gs == mx, sub, N_GROUPS)
        gsel = jnp.where(hit, 1.0, gsel)
        gs = jnp.where(hit, NEG_INF, gs)
    emask = jnp.concatenate(
        [jnp.broadcast_to(gsel[g:g + 1, :], (GROUP_SIZE, tile)) for g in range(N_GROUPS)], axis=0)

    eidx = lax.broadcasted_iota(jnp.int32, (N_EXPERTS, tile), 0)
    cand = jnp.where(emask > 0.5, biased, NEG_INF)
    sel = jnp.zeros((N_EXPERTS, tile), F32)
    picks = []
    for _ in range(TOP_K):
        mx = jnp.max(cand, axis=0, keepdims=True)
        first = _first_index(cand == mx, eidx, N_EXPERTS)
        hit = eidx == first
        picks.append(first)
        sel = jnp.where(hit, 1.0, sel)
        cand = jnp.where(hit, NEG_INF, cand)
    picked = jnp.where(sel > 0.5, scores, 0.0)
    wts = picked / jnp.sum(picked, axis=0, keepdims=True) * ROUTED_SCALE

    ri = lax.broadcasted_iota(jnp.int32, (tile, tile), 0)
    ci = lax.broadcasted_iota(jnp.int32, (tile, tile), 1)
    before = (ri < ci).astype(BF16)
    prior = _dot(sel.astype(BF16), before) + base_ref[:, 0:1]
    ranks = [jnp.sum(jnp.where(eidx == p, prior, 0.0), axis=0, keepdims=True) for p in picks]
    wsel = [jnp.sum(jnp.where(eidx == p, wts, 0.0), axis=0, keepdims=True) for p in picks]
    idx_ref[...] = jnp.concatenate(picks, axis=0)
    rank_ref[...] = jnp.concatenate(ranks, axis=0).astype(jnp.int32)
    wk_ref[...] = jnp.concatenate(wsel, axis=0).T
    total = base_ref[...] + jnp.sum(sel, axis=1, keepdims=True)
    base_ref[...] = total
    cnt_ref[...] = total


def _silu(x):
    return x * _sigmoid(x)


def _sc_mesh():
    return plsc.VectorSubcoreMesh(core_axis_name="c", subcore_axis_name="s")


def _sc_worker_id():
    return lax.axis_index("s") * SC_CORES + lax.axis_index("c")


def _dispatch_rows(xpk, pos_chunks, n_rows):
    t = xpk.shape[0]
    n_ch = t // SC_WORKERS // SC_CHUNK

    @functools.partial(
        pl.kernel, mesh=_sc_mesh(),
        out_type=jax.ShapeDtypeStruct((n_rows, PACKED), jnp.int32),
        scratch_types=[pltpu.VMEM((TOP_K, SC_CHUNK), jnp.int32),
                       pltpu.VMEM((SC_CHUNK, PACKED), jnp.int32),
                       pltpu.SemaphoreType.DMA],
        name="moe_dispatch",
    )
    def scatter(x_hbm, pos_hbm, out_hbm, idx_v, rows_v, sem):
        wid = _sc_worker_id()

        @pl.loop(0, n_ch)
        def _(c):
            chunk = wid * n_ch + c
            off = pl.multiple_of(chunk * SC_CHUNK, SC_CHUNK)
            pltpu.sync_copy(pos_hbm.at[chunk], idx_v)
            pltpu.sync_copy(x_hbm.at[pl.ds(off, SC_CHUNK)], rows_v)
            copies = [pltpu.async_copy(rows_v, out_hbm.at[idx_v.at[k]], sem) for k in range(TOP_K)]
            for cp in copies:
                cp.wait()

    return scatter(xpk, pos_chunks)


def _gather_rows(table, idx):
    n = idx.shape[0]
    per_w = n // SC_WORKERS
    n_ch = per_w // SC_CHUNK
    assert n_ch % 2 == 0 and n_ch >= 2

    @functools.partial(
        pl.kernel, mesh=_sc_mesh(),
        out_type=jax.ShapeDtypeStruct((n, PACKED), jnp.int32),
        scratch_types=[pltpu.VMEM((n_ch, SC_CHUNK), jnp.int32),
                       pltpu.VMEM((SC_CHUNK, PACKED), jnp.int32), pltpu.VMEM((SC_CHUNK, PACKED), jnp.int32),
                       pltpu.SemaphoreType.DMA, pltpu.SemaphoreType.DMA,
                       pltpu.SemaphoreType.DMA, pltpu.SemaphoreType.DMA],
        name="moe_gather",
    )
    def gather(table_hbm, idx_hbm, out_hbm, idx_v, rows0, rows1, g0, g1, w0, w1):
        wid = _sc_worker_id()
        base = wid * per_w
        rows, g_sem, w_sem = (rows0, rows1), (g0, g1), (w0, w1)
        pltpu.sync_copy(idx_hbm.at[wid], idx_v)

        def fetch(c, b):
            return pltpu.make_async_copy(table_hbm.at[idx_v.at[c]], rows[b], g_sem[b])

        def flush(c, b):
            off = pl.multiple_of(base + c * SC_CHUNK, SC_CHUNK)
            return pltpu.make_async_copy(rows[b], out_hbm.at[pl.ds(off, SC_CHUNK)], w_sem[b])

        fetch(0, 0).start()

        @pl.loop(0, n_ch, step=2)
        def _(c0):
            for b in range(2):
                c = c0 + b
                fetch(c, b).wait()
                flush(c, b).start()

                @pl.when(c + 1 < n_ch)
                def _():
                    @pl.when(c >= 1)
                    def _():
                        flush(c - 1, 1 - b).wait()
                    fetch(c + 1, 1 - b).start()

        flush(n_ch - 2, 0).wait()
        flush(n_ch - 1, 1).wait()

    return gather(table, idx.reshape(SC_WORKERS, n_ch, SC_CHUNK))


def _unpacked_bf16(p):
    hi, lo = _unpack_bf16_pairs(p)
    return jnp.concatenate([hi.astype(BF16), lo.astype(BF16)], axis=1)


def _expert_kernel(be_ref, nu_ref, next_ref, slot_ref, xs_hbm, wg_hbm, wu_hbm, wd_hbm, y_ref,
                   g16_ref, u16_ref, d16_ref, gf_ref, uf_ref, df_ref, xbuf_ref, xsem, wsem, *, layer):
    i = pl.program_id(0)
    n_used = nu_ref[0]
    bm = xbuf_ref.shape[1]

    def fetch(b):
        slot = lax.rem(b, XS_SLOTS)
        rows = pl.ds(pl.multiple_of(b * bm, bm), bm)
        return pltpu.make_async_copy(xs_hbm.at[rows], xbuf_ref.at[slot], xsem.at[slot])

    @pl.when(i == 0)
    def _():
        fetch(0).start()

        @pl.when(n_used > 1)
        def _():
            fetch(1).start()

    @pl.when(i + 2 < n_used)
    def _():
        fetch(i + 2).start()

    e = be_ref[i]
    slot = slot_ref[e]
    f32_bufs = (gf_ref, uf_ref, df_ref)

    def wfetch(expert, dst_slot):
        return [pltpu.make_async_copy(w_hbm.at[layer, expert], buf.at[dst_slot], wsem.at[dst_slot, j])
                for j, (w_hbm, buf) in enumerate(zip((wg_hbm, wu_hbm, wd_hbm), f32_bufs))]

    @pl.when(i == 0)
    def _():
        for cp in wfetch(e, slot):
            cp.start()

    first_block_of_expert = jnp.logical_or(i == 0, e != be_ref[jnp.maximum(i - 1, 0)])

    @pl.when(jnp.logical_and(first_block_of_expert, i < n_used))
    def _():
        for cp in wfetch(e, slot):
            cp.wait()
        g16_ref[slot] = gf_ref[slot].astype(BF16)
        u16_ref[slot] = uf_ref[slot].astype(BF16)
        d16_ref[slot] = df_ref[slot].astype(BF16)
        nxt = next_ref[e]

        @pl.when(nxt >= 0)
        def _():
            for cp in wfetch(nxt, 1 - slot):
                cp.start()

    @pl.when(i < n_used)
    def _():
        fetch(i).wait()
        x = _unpacked_bf16(xbuf_ref[lax.rem(i, XS_SLOTS)])
        hh = _silu(_dot(x, g16_ref[slot])) * _dot(x, u16_ref[slot])
        y_ref[...] = _pack_bf16_pairs(_dot(hh.astype(BF16), d16_ref[slot]))


def _experts(block_e, n_used, next_expert, weight_slot, xs, wg, wu, wd, layer):
    nb = block_e.shape[0]
    bm = MOE_BLOCK
    hbm = pl.BlockSpec(memory_space=pl.ANY)
    two = lambda shape, dt: pltpu.VMEM((2,) + shape, dt)
    grid_spec = pltpu.PrefetchScalarGridSpec(
        num_scalar_prefetch=4,
        grid=(nb,),
        in_specs=[hbm, hbm, hbm, hbm],
        out_specs=pl.BlockSpec((bm, PACKED), lambda i, be, nu, nx, sl: (jnp.minimum(i, nu[0] - 1), 0)),
        scratch_shapes=[two((D_MODEL, D_EXPERT), BF16), two((D_MODEL, D_EXPERT), BF16), two((D_EXPERT, D_MODEL), BF16),
                        two((D_MODEL, D_EXPERT), F32), two((D_MODEL, D_EXPERT), F32), two((D_EXPERT, D_MODEL), F32),
                        pltpu.VMEM((XS_SLOTS, bm, PACKED), jnp.int32), pltpu.SemaphoreType.DMA((XS_SLOTS,)),
                        pltpu.SemaphoreType.DMA((2, 3))],
    )
    return pl.pallas_call(
        functools.partial(_expert_kernel, layer=layer),
        grid_spec=grid_spec,
        out_shape=jax.ShapeDtypeStruct((nb * bm, PACKED), jnp.int32),
        compiler_params=_params("arbitrary"),
        name="moe_experts",
    )(block_e, n_used, next_expert, weight_slot, xs, wg, wu, wd)


def _combine_kernel(g_ref, wk_ref, h_ref, xpk_ref, sg_ref, su_ref, sd_ref, lw_ref, lb_ref, *rest, with_proj):
    if with_proj:
        (wm_ref, ws_ref), (o_ref, proj_ref, small_ref) = rest[:2], rest[-3:]
        next_proj = (wm_ref, ws_ref, proj_ref, small_ref)
    else:
        o_ref, next_proj = rest[-1], None
    x = _unpacked_bf16(xpk_ref[...])
    hs = _silu(_dot(x, sg_ref[...])) * _dot(x, su_ref[...])
    shared = _dot(hs.astype(BF16), sd_ref[...])
    acc_hi = shared[:, :PACKED]
    acc_lo = shared[:, PACKED:]
    wk = wk_ref[...]
    for k in range(TOP_K):
        y_hi, y_lo = _unpack_bf16_pairs(g_ref[k])
        w = wk[:, k:k + 1]
        acc_hi = acc_hi + w * y_hi
        acc_lo = acc_lo + w * y_lo
    ffn = jnp.concatenate([acc_hi, acc_lo], axis=1)
    out = _layer_norm(ALPHA * h_ref[...] + ffn, lw_ref[...], lb_ref[...])
    o_ref[...] = out
    if next_proj is not None:
        wm_ref, ws_ref, proj_ref, small_ref = next_proj
        out16 = out.astype(BF16)
        proj_ref[...] = _dot(out16, wm_ref[...])
        small_ref[...] = _dot(out16, ws_ref[...])


def _combine_ln(g, wk, h, xpk, sg, su, sd, ln_w, ln_b, next_w, part, prev):
    t = h.shape[0]
    tm = COMBINE_TILE
    n_blk = g.shape[1] // tm
    first = part * n_blk
    const = lambda shape: pl.BlockSpec(shape, lambda i: (0, 0))
    rows = lambda width: pl.BlockSpec((tm, width), lambda i: (i + first, 0))
    in_specs = [pl.BlockSpec((TOP_K, tm, PACKED), lambda i: (0, i, 0)), rows(TOP_K), rows(D_MODEL), rows(PACKED),
                const((D_MODEL, D_EXPERT)), const((D_MODEL, D_EXPERT)), const((D_EXPERT, D_MODEL)),
                const((1, D_MODEL)), const((1, D_MODEL))]
    args = [g, wk, h, xpk, sg.astype(BF16), su.astype(BF16), sd.astype(BF16),
            ln_w.reshape(1, D_MODEL), ln_b.reshape(1, D_MODEL)]
    out_specs = [rows(D_MODEL)]
    out_shape = [jax.ShapeDtypeStruct((t, D_MODEL), F32)]
    if next_w is not None:
        w_main, w_small = next_w
        n = w_main.shape[1]
        in_specs += [const((D_MODEL, n)), const((D_MODEL, LANES))]
        args += [w_main, w_small]
        out_specs += [rows(n), rows(LANES)]
        out_shape += [jax.ShapeDtypeStruct((t, n), F32), jax.ShapeDtypeStruct((t, LANES), F32)]
    aliases = {}
    if prev is not None:
        aliases = {len(args) + k: k for k in range(len(prev))}
        in_specs += [pl.BlockSpec(memory_space=pl.ANY)] * len(prev)
        args += list(prev)
    return pl.pallas_call(
        functools.partial(_combine_kernel, with_proj=next_w is not None),
        grid=(n_blk,),
        in_specs=in_specs,
        out_specs=out_specs,
        out_shape=out_shape,
        input_output_aliases=aliases,
        compiler_params=_params("parallel"),
        name="moe_combine_ln",
    )(*args)


def _moe_ln(h, hpk, idx, rank, wk, counts, wg, wu, wd, layer, sg, su, sd, ln_w, ln_b, next_w):
    t = h.shape[0]
    cnt = counts[:, 0].astype(jnp.int32)
    padded = (cnt + MOE_BLOCK - 1) // MOE_BLOCK * MOE_BLOCK
    pend = jnp.cumsum(padded)
    experts = jnp.arange(N_EXPERTS, dtype=jnp.int32)
    pstart_of_pick = jnp.sum(jnp.where(idx[:, :, None] == experts, pend - padded, 0), axis=-1)
    pos = pstart_of_pick + rank
    nb = -(-(t * TOP_K + N_EXPERTS * (MOE_BLOCK - 1)) // MOE_BLOCK)
    starts = jnp.arange(nb, dtype=jnp.int32) * MOE_BLOCK
    block_e = jnp.minimum(jnp.sum((pend[None, :] <= starts[:, None]).astype(jnp.int32), axis=1), N_EXPERTS - 1)
    n_used = (pend[-1] // MOE_BLOCK).astype(jnp.int32).reshape(1)
    has_rows = cnt > 0
    later = jnp.logical_and(has_rows[None, :], experts[None, :] > experts[:, None])
    next_expert = jnp.min(jnp.where(later, experts[None, :], N_EXPERTS), axis=1)
    next_expert = jnp.where(next_expert == N_EXPERTS, -1, next_expert).astype(jnp.int32)
    weight_slot = ((jnp.cumsum(has_rows) - has_rows) % 2).astype(jnp.int32)
    pos_chunks = pos.reshape(TOP_K, t // SC_CHUNK, SC_CHUNK).transpose(1, 0, 2)
    xs = _dispatch_rows(hpk, pos_chunks, nb * MOE_BLOCK)
    ys = _experts(block_e, n_used, next_expert, weight_slot, xs, wg, wu, wd, layer)
    part = t // COMBINE_PARTS
    gathered = [_gather_rows(ys, pos[:, p * part:(p + 1) * part].reshape(-1)).reshape(TOP_K, part, PACKED)
                for p in range(COMBINE_PARTS)]
    outs = None
    for p in range(COMBINE_PARTS):
        outs = _combine_ln(gathered[p], wk, h, hpk, sg, su, sd, ln_w, ln_b, next_w, p, outs)
    return outs


def _pad_cols(w, width=LANES):
    return jnp.pad(w, ((0, 0), (0, width - w.shape[1])))


def _even_proj_weights(w_in):
    a4 = 4 * MIX_HALF
    ng = 2 * HEADS
    w_main = jnp.concatenate([w_in[:, :a4], w_in[:, a4 + ng:]], axis=1).astype(BF16)
    w_gate = _pad_cols(w_in[:, a4:a4 + ng]).astype(BF16)
    return w_main, w_gate


def _even_mixer(proj, gates, batch, seq, gate_b, norm_w, conv_w, conv_b, wa, ba, wx, bx, lam):
    ya = _mlstm(proj, gates, gate_b, norm_w, batch, seq)
    yb = _rglru(proj, conv_w, conv_b, wa, ba, wx, bx, lam, batch, seq)
    return ya, yb


def _odd_proj_weights(w_in):
    c0 = MIX_HALF
    c1 = c0 + HEADS * GLA_DK
    c2 = c1 + HEADS * GLA_DK
    c3 = c2 + MIX_HALF
    c4 = c3 + MIX_HALF
    w_main = jnp.concatenate([w_in[:, :c0], _pad_heads(w_in[:, c0:c1], 1), _pad_heads(w_in[:, c1:c2], 1),
                              w_in[:, c2:c4]], axis=1).astype(BF16)
    w_low = _pad_cols(w_in[:, c4:]).astype(BF16)
    return w_main, w_low


def _odd_mixer(proj, glow, batch, seq, lam_re, lam_im, b_re, b_im, c_re, c_im, d_skip, log_dt,
               glu_w, glu_b, gate_w, gate_b, norm_w):
    tables = _s5_tables(lam_re, lam_im, b_re, b_im, c_re, c_im, log_dt)
    yc = _s5(proj, tables, d_skip, glu_w, glu_b, batch, seq)
    gw = jnp.pad(_pad_heads(gate_w, 1), ((0, LANES - GLA_GATE_RANK), (0, 0))).astype(BF16)
    gb = _pad_heads(gate_b.reshape(1, -1), 1)
    yd = _gla(proj, glow, gw, gb, norm_w, batch, seq)
    return yc, yd


def kernel(x, ln1_w, ln1_b, ln2_w, ln2_b, w_out, w_in_even, mlstm_gate_b, mlstm_norm_w, lru_conv_w, lru_conv_b, lru_wa, lru_ba, lru_wx, lru_bx, lru_lambda, w_in_odd, s5_lam_re, s5_lam_im, s5_b_re, s5_b_im, s5_c_re, s5_c_im, s5_d, s5_log_dt, s5_glu_w, s5_glu_b, gla_gate_w, gla_gate_b, gla_norm_w, router_w, router_bias, exp_w_gate, exp_w_up, exp_w_down, sh_w_gate, sh_w_up, sh_w_down):
    batch, seq, d = x.shape
    proj_w = [_even_proj_weights(w_in_even[layer // 2]) if layer % 2 == 0 else _odd_proj_weights(w_in_odd[layer // 2])
              for layer in range(DEPTH)]
    h = x.reshape(batch * seq, d)
    proj, small = _proj(h, *proj_w[0])
    for layer in range(DEPTH):
        j = layer // 2
        if layer % 2 == 0:
            y1, y2 = _even_mixer(proj, small, batch, seq, mlstm_gate_b[j], mlstm_norm_w[j],
                                 lru_conv_w[j], lru_conv_b[j], lru_wa[j], lru_ba[j], lru_wx[j],
                                 lru_bx[j], lru_lambda[j])
        else:
            y1, y2 = _odd_mixer(proj, small, batch, seq, s5_lam_re[j], s5_lam_im[j], s5_b_re[j],
                                s5_b_im[j], s5_c_re[j], s5_c_im[j], s5_d[j], s5_log_dt[j],
                                s5_glu_w[j], s5_glu_b[j], gla_gate_w[j], gla_gate_b[j], gla_norm_w[j])
        h, hpk, idx, rank, wk, counts = _out_proj_ln_route(y1, y2, h, w_out[layer], ln1_w[layer], ln1_b[layer],
                                                           router_w[layer], router_bias[layer])
        next_w = proj_w[layer + 1] if layer + 1 < DEPTH else None
        res = _moe_ln(h, hpk, idx, rank, wk, counts, exp_w_gate, exp_w_up, exp_w_down, layer,
                      sh_w_gate[layer], sh_w_up[layer], sh_w_down[layer], ln2_w[layer], ln2_b[layer], next_w)
        if next_w is None:
            (h,) = res
        else:
            h, proj, small = res
    return h.reshape(batch, seq, d)
```

```python
import functools
import math

import jax
import jax.numpy as jnp
from jax import lax
from jax.experimental import pallas as pl
from jax.experimental.pallas import tpu as pltpu
from jax.experimental.pallas import tpu_sc as plsc

F32 = jnp.float32
BF16 = jnp.bfloat16

D_MODEL = 1024
DEPTH = 2
MIX_HALF = 512
HEADS = 4
HEAD_DIM = 128
GLA_DK = 64
GLA_CHUNK = 64
GLA_GATE_RANK = 16
GLA_GATE_TEMP = 16.0
LRU_C = 8.0
LRU_CONV = 4
S5_GROUP = 16
S5_GROUPS = 32
S5_STATE = 64
S5_LANES = S5_GROUPS * S5_STATE
S5_BLOCKS = 4
N_EXPERTS = 64
N_GROUPS = 8
GROUP_SIZE = N_EXPERTS // N_GROUPS
TOP_K = 8
TOPK_GROUPS = 4
D_EXPERT = 256
ROUTED_SCALE = 2.5
ALPHA = (2.0 * DEPTH) ** 0.25
EPS = 1e-5
LANES = 128
SUBLANES = 8
NEG_INF = float("-inf")

VMEM_LIMIT = 56 * 1024 * 1024

MLSTM_CHUNK = 128
MLSTM_TILE = 1024
LRU_TILE = 1024
LRU_LOG_STEPS = 3
LRU_UNROLL = 4
S5_TILE = 512
S5_LOG_STEPS = 3
S5_UNROLL = True
GLA_UNROLL = 4
GLA_TILE = 1024
PROJ_TILE = 1024
OUT_TILE = 512
MOE_BLOCK = 1152
XS_SLOTS = 3
COMBINE_TILE = 256
COMBINE_PARTS = 2
PACKED = D_MODEL // 2
SC_CHUNK = 64
SC_CORES = 2
SC_SUBCORES = 16
SC_WORKERS = SC_CORES * SC_SUBCORES


def _params(*sem):
    return pltpu.CompilerParams(dimension_semantics=sem, vmem_limit_bytes=VMEM_LIMIT)


def _split3(x):
    hi = x.astype(BF16)
    r1 = x - hi.astype(F32)
    mid = r1.astype(BF16)
    lo = (r1 - mid.astype(F32)).astype(BF16)
    return hi, mid, lo


def _dot(a, b):
    return jnp.dot(a, b, preferred_element_type=F32)


def _dot_nt(a, b):
    return lax.dot_general(a, b, (((1,), (1,)), ((), ())), preferred_element_type=F32)


def _dot_tn(a, b):
    return lax.dot_general(a, b, (((0,), (0,)), ((), ())), preferred_element_type=F32)


def _exact_left01(mask01_bf16, x):
    hi, mid, lo = _split3(x)
    return _dot(mask01_bf16, hi) + _dot(mask01_bf16, mid) + _dot(mask01_bf16, lo)


def _exact_right01(x, mask01_bf16):
    hi, mid, lo = _split3(x)
    return _dot(hi, mask01_bf16) + _dot(mid, mask01_bf16) + _dot(lo, mask01_bf16)


def _log_sigmoid(x):
    return jnp.minimum(x, 0.0) - jnp.log(1.0 + jnp.exp(-jnp.abs(x)))


def _sigmoid(x):
    return 1.0 / (1.0 + jnp.exp(-x))


def _gelu_tanh(x):
    c = math.sqrt(2.0 / math.pi)
    return 0.5 * x * (1.0 + jnp.tanh(c * (x + 0.044715 * (x * x * x))))


def _layer_norm(z, w, b):
    mu = jnp.mean(z, axis=-1, keepdims=True)
    zc = z - mu
    return zc * lax.rsqrt(jnp.mean(zc * zc, axis=-1, keepdims=True) + EPS) * w + b


def _proj_kernel(x_ref, w_ref, wg_ref, o_ref, og_ref):
    x = x_ref[...].astype(BF16)
    o_ref[...] = _dot(x, w_ref[...])
    og_ref[...] = _dot(x, wg_ref[...])


def _proj(x, w_main, w_small):
    t, d = x.shape
    n = w_main.shape[1]
    tm = PROJ_TILE
    return pl.pallas_call(
        _proj_kernel,
        grid=(t // tm,),
        in_specs=[pl.BlockSpec((tm, d), lambda i: (i, 0)),
                  pl.BlockSpec((d, n), lambda i: (0, 0)),
                  pl.BlockSpec((d, LANES), lambda i: (0, 0))],
        out_specs=[pl.BlockSpec((tm, n), lambda i: (i, 0)),
                   pl.BlockSpec((tm, LANES), lambda i: (i, 0))],
        out_shape=[jax.ShapeDtypeStruct((t, n), F32), jax.ShapeDtypeStruct((t, LANES), F32)],
        compiler_params=_params("parallel"),
        name="in_proj",
    )(x, w_main, w_small)


def _mlstm_kernel(q_ref, k_ref, v_ref, o_ref, gc_ref, gr_ref, bc_ref, br_ref, nw_ref,
                  y_ref, c_ref, m_ref, *, chunk, n_chunks):
    L = chunk

    @pl.when(pl.program_id(1) == 0)
    def _():
        c_ref[...] = jnp.zeros_like(c_ref)
        m_ref[...] = jnp.zeros_like(m_ref)

    ri = lax.broadcasted_iota(jnp.int32, (L, L), 0)
    ci = lax.broadcasted_iota(jnp.int32, (L, L), 1)
    causal = ci <= ri
    tril = causal.astype(BF16)
    triu = (ri <= ci).astype(BF16)
    ones_v = jnp.ones((L, HEAD_DIM), BF16)
    scale = HEAD_DIM ** -0.5

    def body(c, carry):
        r0 = pl.multiple_of(c * L, L)
        g_col = gc_ref[pl.ds(r0, L), :] + bc_ref[...]
        g_row = gr_ref[c] + br_ref[...]
        b_col_all = _exact_left01(tril, _log_sigmoid(g_col))
        b_row_all = _exact_right01(_log_sigmoid(g_row), triu)
        for h in range(HEADS):
            lo = h * HEAD_DIM
            q = q_ref[pl.ds(r0, L), lo:lo + HEAD_DIM].astype(BF16)
            k = k_ref[pl.ds(r0, L), lo:lo + HEAD_DIM] * scale
            v = v_ref[pl.ds(r0, L), lo:lo + HEAD_DIM].astype(BF16)
            v_aug = jnp.concatenate([v, ones_v], axis=1)
            i_rep = jnp.broadcast_to(g_col[:, h:h + 1], (L, LANES))
            b_rep = jnp.broadcast_to(b_col_all[:, HEADS + h:HEADS + h + 1], (L, LANES))
            i_row = g_row[h:h + 1, :]
            b_row = b_row_all[HEADS + h:HEADS + h + 1, :]
            b_last = b_rep[L - 1:L, :]
            m_prev = m_ref[h:h + 1, :]
            c_prev = c_ref[h]

            d_mat = jnp.where(causal, b_rep - b_row + i_row, NEG_INF)
            m_inter = b_rep + m_prev
            m_i = jnp.maximum(m_inter, jnp.max(d_mat, axis=1, keepdims=True))
            s = _dot_nt(q, k.astype(BF16)) * jnp.exp(d_mat - m_i)
            w_inter = jnp.exp(m_inter - m_i)
            intra = _dot(s.astype(BF16), v_aug)
            inter = _dot(q, c_prev.astype(BF16))
            num = intra[:, :HEAD_DIM] + w_inter * inter[:, :HEAD_DIM]
            den = intra[:, HEAD_DIM:] + w_inter * inter[:, HEAD_DIM:]
            hh = num / jnp.maximum(jnp.abs(den), jnp.exp(-m_i))

            w_loc = b_last - b_rep + i_rep
            m_loc = jnp.max(w_loc, axis=0, keepdims=True)
            kp = (k * jnp.exp(w_loc - m_loc)).astype(BF16)
            c_loc = _dot_tn(kp, v_aug)
            m_new = jnp.maximum(b_last + m_prev, m_loc)
            keep = jnp.exp(b_last + m_prev - m_new)
            add = jnp.exp(m_loc - m_new)
            c_ref[h] = (jnp.concatenate([keep, keep], axis=1) * c_prev
                        + jnp.concatenate([add, add], axis=1) * c_loc)
            m_ref[h:h + 1, :] = m_new

            hc = hh - jnp.mean(hh, axis=-1, keepdims=True)
            yn = hc * lax.rsqrt(jnp.mean(hc * hc, axis=-1, keepdims=True) + EPS)
            og = o_ref[pl.ds(r0, L), lo:lo + HEAD_DIM]
            y_ref[pl.ds(r0, L), lo:lo + HEAD_DIM] = (yn * nw_ref[:, lo:lo + HEAD_DIM] * _sigmoid(og)).astype(BF16)
        return carry

    lax.fori_loop(0, n_chunks, body, 0)


def _mlstm(proj, gates, gate_b, norm_w, batch, seq):
    t = batch * seq
    L = MLSTM_CHUNK
    assert L == LANES, "the kernel keeps per-row gate terms replicated over one vreg of lanes"
    ts = MLSTM_TILE
    nj = seq // ts
    nc = ts // L
    g_row = gates[:, :2 * HEADS].reshape(t // L, L, 2 * HEADS).transpose(0, 2, 1)
    b_col = jnp.zeros((1, LANES), F32).at[0, :2 * HEADS].set(gate_b)
    b_row = gate_b.reshape(2 * HEADS, 1)
    blk = lambda col: pl.BlockSpec((ts, MIX_HALF), lambda b, j, col=col: (b * nj + j, col))
    kern = functools.partial(_mlstm_kernel, chunk=L, n_chunks=nc)
    return pl.pallas_call(
        kern,
        grid=(batch, nj),
        in_specs=[blk(0), blk(1), blk(2), blk(3),
                  pl.BlockSpec((ts, LANES), lambda b, j: (b * nj + j, 0)),
                  pl.BlockSpec((nc, 2 * HEADS, L), lambda b, j: (b * nj + j, 0, 0)),
                  pl.BlockSpec((1, LANES), lambda b, j: (0, 0)),
                  pl.BlockSpec((2 * HEADS, 1), lambda b, j: (0, 0)),
                  pl.BlockSpec((1, MIX_HALF), lambda b, j: (0, 0))],
        out_specs=pl.BlockSpec((ts, MIX_HALF), lambda b, j: (b * nj + j, 0)),
        out_shape=jax.ShapeDtypeStruct((t, MIX_HALF), BF16),
        scratch_shapes=[pltpu.VMEM((HEADS, HEAD_DIM, 2 * HEAD_DIM), F32),
                        pltpu.VMEM((8, LANES), F32)],
        compiler_params=_params("arbitrary", "arbitrary"),
        name="mlstm",
    )(proj, proj, proj, proj, gates, g_row, b_col, b_row, norm_w.reshape(1, MIX_HALF))


def _rglru_kernel(xb_ref, gb_ref, cw_ref, cb_ref, wa_ref, ba_ref, wx_ref, bx_ref, lam_ref,
                  y_ref, xext_ref, h_ref, a_ref, u_ref, *, tile):
    @pl.when(pl.program_id(1) == 0)
    def _():
        xext_ref[0:8, :] = jnp.zeros((8, MIX_HALF), F32)
        h_ref[...] = jnp.zeros_like(h_ref)

    x = xb_ref[...]
    xext_ref[8:8 + tile, :] = x
    xc = cb_ref[...] + cw_ref[LRU_CONV - 1:LRU_CONV, :] * x
    for tap in range(LRU_CONV - 1):
        back = LRU_CONV - 1 - tap
        xc = xc + cw_ref[tap:tap + 1, :] * xext_ref[8 - back:8 - back + tile, :]
    xext_ref[0:8, :] = x[tile - 8:tile, :]

    xc16 = xc.astype(BF16)
    r_parts, i_parts = [], []
    for h in range(HEADS):
        lo = h * HEAD_DIM
        xh = xc16[:, lo:lo + HEAD_DIM]
        r_parts.append(_dot(xh, wa_ref[h]))
        i_parts.append(_dot(xh, wx_ref[h]))
    r = _sigmoid(jnp.concatenate(r_parts, axis=1) + ba_ref[...])
    ig = _sigmoid(jnp.concatenate(i_parts, axis=1) + bx_ref[...])
    lam = lam_ref[...]
    softplus_neg = jnp.maximum(-lam, 0.0) + jnp.log(1.0 + jnp.exp(-jnp.abs(lam)))
    log_a = -LRU_C * r * softplus_neg
    a = jnp.exp(log_a)
    th = jnp.tanh(log_a)
    u = jnp.sqrt(-2.0 * th / (1.0 - th)) * ig * xc

    a_ref[...] = a
    u_ref[...] = u
    rows = lax.broadcasted_iota(jnp.int32, (SUBLANES, MIX_HALF), 0)

    def group(i, h_prev):
        r0 = pl.multiple_of(i * SUBLANES, SUBLANES)
        ag = a_ref[pl.ds(r0, SUBLANES), :]
        ug = u_ref[pl.ds(r0, SUBLANES), :]
        for k in range(LRU_LOG_STEPS):
            keep = rows >= (1 << k)
            ug = ag * jnp.where(keep, pltpu.roll(ug, 1 << k, 0), 0.0) + ug
            ag = ag * jnp.where(keep, pltpu.roll(ag, 1 << k, 0), 1.0)
        hg = ug + ag * h_prev
        u_ref[pl.ds(r0, SUBLANES), :] = hg
        return hg[SUBLANES - 1:SUBLANES, :]

    h_last = lax.fori_loop(0, tile // SUBLANES, group, h_ref[0:1, :], unroll=LRU_UNROLL)
    h_ref[...] = jnp.broadcast_to(h_last, h_ref.shape)
    y_ref[...] = (u_ref[...] * _gelu_tanh(gb_ref[...])).astype(BF16)


def _rglru(proj, conv_w, conv_b, wa, ba, wx, bx, lam, batch, seq):
    t = batch * seq
    ts = LRU_TILE
    nj = seq // ts
    row = lambda a: a.reshape(1, MIX_HALF)
    const2 = lambda shape: pl.BlockSpec(shape, lambda b, j: (0, 0))
    const3 = lambda shape: pl.BlockSpec(shape, lambda b, j: (0, 0, 0))
    blk = lambda col: pl.BlockSpec((ts, MIX_HALF), lambda b, j, col=col: (b * nj + j, col))
    return pl.pallas_call(
        functools.partial(_rglru_kernel, tile=ts),
        grid=(batch, nj),
        in_specs=[blk(4), blk(5), const2((LRU_CONV, MIX_HALF)), const2((1, MIX_HALF)),
                  const3((HEADS, HEAD_DIM, HEAD_DIM)), const2((1, MIX_HALF)),
                  const3((HEADS, HEAD_DIM, HEAD_DIM)), const2((1, MIX_HALF)), const2((1, MIX_HALF))],
        out_specs=pl.BlockSpec((ts, MIX_HALF), lambda b, j: (b * nj + j, 0)),
        out_shape=jax.ShapeDtypeStruct((t, MIX_HALF), BF16),
        scratch_shapes=[pltpu.VMEM((ts + 8, MIX_HALF), F32), pltpu.VMEM((8, MIX_HALF), F32),
                        pltpu.VMEM((ts, MIX_HALF), F32), pltpu.VMEM((ts, MIX_HALF), F32)],
        compiler_params=_params("arbitrary", "arbitrary"),
        name="rglru",
    )(proj, proj, conv_w, row(conv_b), wa.astype(BF16), row(ba), wx.astype(BF16), row(bx), row(lam))


def _s5_kernel(u_ref, bre_ref, bim_ref, cre_ref, cim_ref, mre_ref, mim_ref, pre_ref, pim_ref, d_ref, gw_ref,
               gb_ref, y_ref, xr_ref, xi_ref, cr_ref, ci_ref, *, tile):
    @pl.when(pl.program_id(1) == 0)
    def _():
        cr_ref[...] = jnp.zeros_like(cr_ref)
        ci_ref[...] = jnp.zeros_like(ci_ref)

    u = u_ref[...]
    u16 = u.astype(BF16)
    blk_c = MIX_HALF // S5_BLOCKS
    blk_s = S5_LANES // S5_BLOCKS
    parts = []
    for j in range(S5_BLOCKS):
        lanes = slice(j * blk_s, (j + 1) * blk_s)
        uj = u16[:, j * blk_c:(j + 1) * blk_c]
        xr_ref[:, lanes] = _dot(uj, bre_ref[j])
        xi_ref[:, lanes] = _dot(uj, bim_ref[j])

        def group(i, carry, lanes=lanes):
            cr, ci = carry
            r0 = pl.multiple_of(i * SUBLANES, SUBLANES)
            xr = xr_ref[pl.ds(r0, SUBLANES), lanes]
            xi = xi_ref[pl.ds(r0, SUBLANES), lanes]
            for k in range(S5_LOG_STEPS):
                sr = pltpu.roll(xr, 1 << k, 0)
                si = pltpu.roll(xi, 1 << k, 0)
                mr = mre_ref[k, :, lanes]
                mi = mim_ref[k, :, lanes]
                xr, xi = xr + mr * sr - mi * si, xi + mr * si + mi * sr
            pr = pre_ref[:, lanes]
            pi = pim_ref[:, lanes]
            xr, xi = xr + pr * cr - pi * ci, xi + pr * ci + pi * cr
            xr_ref[pl.ds(r0, SUBLANES), lanes] = xr
            xi_ref[pl.ds(r0, SUBLANES), lanes] = xi
            return xr[SUBLANES - 1:SUBLANES, :], xi[SUBLANES - 1:SUBLANES, :]

        cr, ci = lax.fori_loop(0, tile // SUBLANES, group, (cr_ref[0:1, lanes], ci_ref[0:1, lanes]),
                               unroll=S5_UNROLL)
        cr_ref[0:1, lanes] = cr
        ci_ref[0:1, lanes] = ci
        parts.append(_dot(xr_ref[:, lanes].astype(BF16), cre_ref[j])
                     - _dot(xi_ref[:, lanes].astype(BF16), cim_ref[j]))
    y = jnp.concatenate(parts, axis=1) + d_ref[...] * u
    g = _gelu_tanh(y)
    y_ref[...] = (g * _sigmoid(_dot(g.astype(BF16), gw_ref[...]) + gb_ref[...])).astype(BF16)


def _s5_tables(lam_re, lam_im, b_re, b_im, c_re, c_im, log_dt):
    lr, li = lam_re.astype(F32), lam_im.astype(F32)
    dt = jnp.exp(log_dt.astype(F32))[:, None]
    mag = jnp.exp(lr * dt)
    abar_re = mag * jnp.cos(li * dt)
    abar_im = mag * jnp.sin(li * dt)
    den = lr * lr + li * li
    nr = abar_re - 1.0
    coef_re = (nr * lr + abar_im * li) / den
    coef_im = (abar_im * lr - nr * li) / den
    bbar_re = coef_re[..., None] * b_re - coef_im[..., None] * b_im
    bbar_im = coef_re[..., None] * b_im + coef_im[..., None] * b_re
    gpb = S5_GROUPS // S5_BLOCKS
    eye = jnp.eye(gpb, dtype=F32)

    def in_map(bb):
        bb = bb.reshape(S5_BLOCKS, gpb, S5_STATE, S5_GROUP)
        return jnp.einsum("jgph,gk->jghkp", bb, eye).reshape(S5_BLOCKS, gpb * S5_GROUP, gpb * S5_STATE)

    def out_map(cc):
        cc = cc.reshape(S5_BLOCKS, gpb, S5_GROUP, S5_STATE)
        return jnp.einsum("jghp,gk->jgpkh", cc, eye).reshape(S5_BLOCKS, gpb * S5_STATE, gpb * S5_GROUP)

    def power(n):
        n = jnp.asarray(n, F32)[..., None, None]
        pmag = jnp.exp(n * (lr * dt))
        shape = n.shape[:-2] + (S5_LANES,)
        return (pmag * jnp.cos(n * (li * dt))).reshape(shape), (pmag * jnp.sin(n * (li * dt))).reshape(shape)

    row = jnp.arange(SUBLANES)
    step = 2 ** jnp.arange(S5_LOG_STEPS)
    s_re, s_im = power(step)
    keep = (row[None, :] >= step[:, None])[..., None]
    m_re = jnp.where(keep, s_re[:, None, :], 0.0)
    m_im = jnp.where(keep, s_im[:, None, :], 0.0)
    p_re, p_im = power(row + 1)
    return (in_map(bbar_re).astype(BF16), in_map(bbar_im).astype(BF16),
            out_map(c_re.astype(F32)).astype(BF16), out_map(c_im.astype(F32)).astype(BF16),
            m_re, m_im, p_re, p_im)


def _s5(proj, tables, d_skip, glu_w, glu_b, batch, seq):
    t = batch * seq
    ts = S5_TILE
    nj = seq // ts
    bre, bim, cre, cim, m_re, m_im, p_re, p_im = tables
    blk_c = MIX_HALF // S5_BLOCKS
    blk_s = S5_LANES // S5_BLOCKS
    const2 = lambda shape: pl.BlockSpec(shape, lambda b, j: (0, 0))
    const3 = lambda shape: pl.BlockSpec(shape, lambda b, j: (0, 0, 0))
    return pl.pallas_call(
        functools.partial(_s5_kernel, tile=ts),
        grid=(batch, nj),
        in_specs=[pl.BlockSpec((ts, MIX_HALF), lambda b, j: (b * nj + j, 0)),
                  const3((S5_BLOCKS, blk_c, blk_s)), const3((S5_BLOCKS, blk_c, blk_s)),
                  const3((S5_BLOCKS, blk_s, blk_c)), const3((S5_BLOCKS, blk_s, blk_c)),
                  const3(m_re.shape), const3(m_im.shape), const2(p_re.shape), const2(p_im.shape),
                  const2((1, MIX_HALF)), const2((MIX_HALF, MIX_HALF)), const2((1, MIX_HALF))],
        out_specs=pl.BlockSpec((ts, MIX_HALF), lambda b, j: (b * nj + j, 0)),
        out_shape=jax.ShapeDtypeStruct((t, MIX_HALF), BF16),
        scratch_shapes=[pltpu.VMEM((ts, S5_LANES), F32), pltpu.VMEM((ts, S5_LANES), F32),
                        pltpu.VMEM((8, S5_LANES), F32), pltpu.VMEM((8, S5_LANES), F32)],
        compiler_params=_params("arbitrary", "arbitrary"),
        name="s5",
    )(proj, bre, bim, cre, cim, m_re, m_im, p_re, p_im, d_skip.reshape(1, MIX_HALF), glu_w.astype(BF16),
      glu_b.reshape(1, MIX_HALF))


def _gla_kernel(q_ref, k_ref, v_ref, r_ref, gl_ref, gw_ref, gb_ref, nw_ref, y_ref,
                st_ref, qd_ref, ki_ref, ke_ref, v16_ref, dec_ref, o_ref, *, tile, chunk):
    L = chunk
    nc = tile // L

    @pl.when(pl.program_id(1) == 0)
    def _():
        st_ref[...] = jnp.zeros_like(st_ref)

    z = _dot(gl_ref[...].astype(BF16), gw_ref[...]) + gb_ref[...]
    bcum = _log_sigmoid(z) * (1.0 / GLA_GATE_TEMP)
    row_in_chunk = lax.broadcasted_iota(jnp.int32, bcum.shape, 0) & (L - 1)
    s = 1
    while s < L:
        bcum = bcum + jnp.where(row_in_chunk >= s, pltpu.roll(bcum, s, 0), 0.0)
        s *= 2
    b3 = bcum.reshape(nc, L, MIX_HALF)
    b_last = b3[:, L - 1:L, :]
    k = k_ref[...]
    qd_ref[...] = (q_ref[...] * (GLA_DK ** -0.5) * jnp.exp(bcum)).astype(BF16)
    ki_ref[...] = (k * jnp.exp(-bcum)).astype(BF16)
    ke_ref[...] = (k.reshape(nc, L, MIX_HALF) * jnp.exp(b_last - b3)).reshape(tile, MIX_HALF).astype(BF16)
    v16_ref[...] = v_ref[...].astype(BF16)
    dec_ref[...] = jnp.exp(b_last)

    ri = lax.broadcasted_iota(jnp.int32, (L, L), 0)
    ci = lax.broadcasted_iota(jnp.int32, (L, L), 1)
    causal = ci <= ri

    def body(c, carry):
        r0 = pl.multiple_of(c * L, L)
        dec = dec_ref[c]
        for h in range(HEADS):
            lo = h * HEAD_DIM
            q_dec = qd_ref[pl.ds(r0, L), lo:lo + HEAD_DIM]
            v = v16_ref[pl.ds(r0, L), lo:lo + HEAD_DIM]
            st = st_ref[h]
            att = jnp.where(causal, _dot_nt(q_dec, ki_ref[pl.ds(r0, L), lo:lo + HEAD_DIM]), 0.0)
            o_ref[pl.ds(r0, L), lo:lo + HEAD_DIM] = (_dot(att.astype(BF16), v)
                                                     + _dot_nt(q_dec, st.astype(BF16)))
            st_ref[h] = dec[:, lo:lo + HEAD_DIM] * st + _dot_tn(v, ke_ref[pl.ds(r0, L), lo:lo + HEAD_DIM])
        return carry

    lax.fori_loop(0, nc, body, 0, unroll=GLA_UNROLL)

    rg = r_ref[...]
    gate = nw_ref[...] * (rg * _sigmoid(rg))
    for h in range(HEADS):
        lo = h * HEAD_DIM
        o = o_ref[:, lo:lo + HEAD_DIM]
        yn = o * lax.rsqrt(jnp.mean(o * o, axis=-1, keepdims=True) + EPS)
        y_ref[:, lo:lo + HEAD_DIM] = (yn * gate[:, lo:lo + HEAD_DIM]).astype(BF16)


def _gla(proj, glow, gate_w, gate_b, norm_w, batch, seq):
    t = batch * seq
    ts = GLA_TILE
    nj = seq // ts
    blk = lambda col: pl.BlockSpec((ts, MIX_HALF), lambda b, j, col=col: (b * nj + j, col))
    const2 = lambda shape: pl.BlockSpec(shape, lambda b, j: (0, 0))
    return pl.pallas_call(
        functools.partial(_gla_kernel, tile=ts, chunk=GLA_CHUNK),
        grid=(batch, nj),
        in_specs=[blk(1), blk(2), blk(3), blk(4),
                  pl.BlockSpec((ts, LANES), lambda b, j: (b * nj + j, 0)),
                  const2((LANES, MIX_HALF)), const2((1, MIX_HALF)), const2((1, MIX_HALF))],
        out_specs=pl.BlockSpec((ts, MIX_HALF), lambda b, j: (b * nj + j, 0)),
        out_shape=jax.ShapeDtypeStruct((t, MIX_HALF), BF16),
        scratch_shapes=[pltpu.VMEM((HEADS, HEAD_DIM, HEAD_DIM), F32),
                        pltpu.VMEM((ts, MIX_HALF), BF16), pltpu.VMEM((ts, MIX_HALF), BF16),
                        pltpu.VMEM((ts, MIX_HALF), BF16), pltpu.VMEM((ts, MIX_HALF), BF16),
                        pltpu.VMEM((ts // GLA_CHUNK, 1, MIX_HALF), F32),
                        pltpu.VMEM((ts, MIX_HALF), F32)],
        compiler_params=_params("arbitrary", "arbitrary"),
        name="gla",
    )(proj, proj, proj, proj, glow, gate_w, gate_b, norm_w.reshape(1, MIX_HALF))


def _pad_heads(w, axis):
    shape = list(w.shape)
    shape[axis:axis + 1] = [HEADS, GLA_DK]
    w = w.reshape(shape)
    pad = [(0, 0)] * w.ndim
    pad[axis + 1] = (0, HEAD_DIM - GLA_DK)
    w = jnp.pad(w, pad)
    shape[axis:axis + 2] = [HEADS * HEAD_DIM]
    return w.reshape(shape)


def _pack_bf16_pairs(z):
    hi = lax.bitcast_convert_type(z[:, :PACKED].astype(BF16).astype(F32), jnp.uint32)
    lo = lax.bitcast_convert_type(z[:, PACKED:].astype(BF16).astype(F32), jnp.uint32)
    word = (hi & jnp.uint32(0xFFFF0000)) | lax.shift_right_logical(lo, jnp.uint32(16))
    return lax.bitcast_convert_type(word, jnp.int32)


def _unpack_bf16_pairs(p):
    word = lax.bitcast_convert_type(p, jnp.uint32)
    hi = lax.bitcast_convert_type(word & jnp.uint32(0xFFFF0000), F32)
    lo = lax.bitcast_convert_type(lax.shift_left(word, jnp.uint32(16)), F32)
    return hi, lo


def _out_kernel(ya_ref, yb_ref, h_ref, w_ref, lw_ref, lb_ref, rw_ref, rb_ref,
                o_ref, opk_ref, idx_ref, rank_ref, wk_ref, cnt_ref, base_ref, *, tile):
    mixed = jnp.concatenate([ya_ref[...], yb_ref[...]], axis=1)
    z = ALPHA * h_ref[...] + _dot(mixed, w_ref[...])
    out = _layer_norm(z, lw_ref[...], lb_ref[...])
    o_ref[...] = out
    opk_ref[...] = _pack_bf16_pairs(out)
    _route_tile(out, rw_ref, rb_ref, idx_ref, rank_ref, wk_ref, cnt_ref, base_ref, tile)


def _out_proj_ln_route(ya, yb, h, w_out, ln_w, ln_b, router_w, router_bias):
    t = h.shape[0]
    tm = OUT_TILE
    const = lambda shape: pl.BlockSpec(shape, lambda i: (0, 0))
    per_tok = lambda dt: jax.ShapeDtypeStruct((TOP_K, t), dt)
    tok_blk = pl.BlockSpec((TOP_K, tm), lambda i: (0, i))
    return pl.pallas_call(
        functools.partial(_out_kernel, tile=tm),
        grid=(t // tm,),
        in_specs=[pl.BlockSpec((tm, MIX_HALF), lambda i: (i, 0)),
                  pl.BlockSpec((tm, MIX_HALF), lambda i: (i, 0)),
                  pl.BlockSpec((tm, D_MODEL), lambda i: (i, 0)),
                  const((D_MODEL, D_MODEL)), const((1, D_MODEL)), const((1, D_MODEL)),
                  const((N_EXPERTS, D_MODEL)), const((N_EXPERTS, 1))],
        out_specs=[pl.BlockSpec((tm, D_MODEL), lambda i: (i, 0)),
                   pl.BlockSpec((tm, PACKED), lambda i: (i, 0)),
                   tok_blk, tok_blk, pl.BlockSpec((tm, TOP_K), lambda i: (i, 0)), const((N_EXPERTS, LANES))],
        out_shape=[jax.ShapeDtypeStruct((t, D_MODEL), F32), jax.ShapeDtypeStruct((t, PACKED), jnp.int32),
                   per_tok(jnp.int32), per_tok(jnp.int32), jax.ShapeDtypeStruct((t, TOP_K), F32),
                   jax.ShapeDtypeStruct((N_EXPERTS, LANES), F32)],
        scratch_shapes=[pltpu.VMEM((N_EXPERTS, LANES), F32)],
        compiler_params=_params("arbitrary"),
        name="out_proj_ln_route",
    )(ya, yb, h, w_out.astype(BF16), ln_w.reshape(1, D_MODEL), ln_b.reshape(1, D_MODEL),
      router_w.T, router_bias.reshape(N_EXPERTS, 1))


def _first_index(hit, idx, big):
    return jnp.min(jnp.where(hit, idx, big), axis=0, keepdims=True)


def _route_tile(h, w_ref, b_ref, idx_ref, rank_ref, wk_ref, cnt_ref, base_ref, tile):
    @pl.when(pl.program_id(0) == 0)
    def _():
        base_ref[...] = jnp.zeros_like(base_ref)

    h_hi, h_mid, _ = _split3(h)
    w_hi, w_mid, _ = _split3(w_ref[...])
    logits = _dot_nt(w_hi, h_hi) + _dot_nt(w_hi, h_mid) + _dot_nt(w_mid, h_hi)
    scores = _sigmoid(logits)
    biased = scores + b_ref[...]

    sub = lax.broadcasted_iota(jnp.int32, (GROUP_SIZE, tile), 0)
    grp_rows = []
    for g in range(N_GROUPS):
        xg = biased[g * GROUP_SIZE:(g + 1) * GROUP_SIZE, :]
        m1 = jnp.max(xg, axis=0, keepdims=True)
        i1 = _first_index(xg == m1, sub, GROUP_SIZE)
        m2 = jnp.max(jnp.where(sub == i1, NEG_INF, xg), axis=0, keepdims=True)
        grp_rows.append(m1 + m2)
    gs = jnp.concatenate(grp_rows, axis=0)
    gsel = jnp.zeros((N_GROUPS, tile), F32)
    for _ in range(TOPK_GROUPS):
        mx = jnp.max(gs, axis=0, keepdims=True)
        hit = sub == _first_index(gs == mx, sub, N_GROUPS)
        gsel = jnp.where(hit, 1.0, gsel)
        gs = jnp.where(hit, NEG_INF, gs)
    emask = jnp.concatenate(
        [jnp.broadcast_to(gsel[g:g + 1, :], (GROUP_SIZE, tile)) for g in range(N_GROUPS)], axis=0)

    eidx = lax.broadcasted_iota(jnp.int32, (N_EXPERTS, tile), 0)
    cand = jnp.where(emask > 0.5, biased, NEG_INF)
    sel = jnp.zeros((N_EXPERTS, tile), F32)
    picks = []
    for _ in range(TOP_K):
        mx = jnp.max(cand, axis=0, keepdims=True)
        first = _first_index(cand == mx, eidx, N_EXPERTS)
        hit = eidx == first
        picks.append(first)
        sel = jnp.where(hit, 1.0, sel)
        cand = jnp.where(hit, NEG_INF, cand)
    picked = jnp.where(sel > 0.5, scores, 0.0)
    wts = picked / jnp.sum(picked, axis=0, keepdims=True) * ROUTED_SCALE

    ri = lax.broadcasted_iota(jnp.int32, (tile, tile), 0)
    ci = lax.broadcasted_iota(jnp.int32, (tile, tile), 1)
    before = (ri < ci).astype(BF16)
    prior = _dot(sel.astype(BF16), before) + base_ref[:, 0:1]
    ranks = [jnp.sum(jnp.where(eidx == p, prior, 0.0), axis=0, keepdims=True) for p in picks]
    wsel = [jnp.sum(jnp.where(eidx == p, wts, 0.0), axis=0, keepdims=True) for p in picks]
    idx_ref[...] = jnp.concatenate(picks, axis=0)
    rank_ref[...] = jnp.concatenate(ranks, axis=0).astype(jnp.int32)
    wk_ref[...] = jnp.concatenate(wsel, axis=0).T
    total = base_ref[...] + jnp.sum(sel, axis=1, keepdims=True)
    base_ref[...] = total
    cnt_ref[...] = total


def _silu(x):
    return x * _sigmoid(x)


def _sc_mesh():
    return plsc.VectorSubcoreMesh(core_axis_name="c", subcore_axis_name="s")


def _sc_worker_id():
    return lax.axis_index("s") * SC_CORES + lax.axis_index("c")


def _dispatch_rows(xpk, pos_chunks, n_rows):
    t = xpk.shape[0]
    n_ch = t // SC_WORKERS // SC_CHUNK

    @functools.partial(
        pl.kernel, mesh=_sc_mesh(),
        out_type=jax.ShapeDtypeStruct((n_rows, PACKED), jnp.int32),
        scratch_types=[pltpu.VMEM((TOP_K, SC_CHUNK), jnp.int32),
                       pltpu.VMEM((SC_CHUNK, PACKED), jnp.int32),
                       pltpu.SemaphoreType.DMA],
        name="moe_dispatch",
    )
    def scatter(x_hbm, pos_hbm, out_hbm, idx_v, rows_v, sem):
        wid = _sc_worker_id()

        @pl.loop(0, n_ch)
        def _(c):
            chunk = wid * n_ch + c
            off = pl.multiple_of(chunk * SC_CHUNK, SC_CHUNK)
            pltpu.sync_copy(pos_hbm.at[chunk], idx_v)
            pltpu.sync_copy(x_hbm.at[pl.ds(off, SC_CHUNK)], rows_v)
            copies = [pltpu.async_copy(rows_v, out_hbm.at[idx_v.at[k]], sem) for k in range(TOP_K)]
            for cp in copies:
                cp.wait()

    return scatter(xpk, pos_chunks)


def _gather_rows(table, idx):
    n = idx.shape[0]
    per_w = n // SC_WORKERS
    n_ch = per_w // SC_CHUNK
    assert n_ch % 2 == 0 and n_ch >= 2

    @functools.partial(
        pl.kernel, mesh=_sc_mesh(),
        out_type=jax.ShapeDtypeStruct((n, PACKED), jnp.int32),
        scratch_types=[pltpu.VMEM((n_ch, SC_CHUNK), jnp.int32),
                       pltpu.VMEM((SC_CHUNK, PACKED), jnp.int32), pltpu.VMEM((SC_CHUNK, PACKED), jnp.int32),
                       pltpu.SemaphoreType.DMA, pltpu.SemaphoreType.DMA,
                       pltpu.SemaphoreType.DMA, pltpu.SemaphoreType.DMA],
        name="moe_gather",
    )
    def gather(table_hbm, idx_hbm, out_hbm, idx_v, rows0, rows1, g0, g1, w0, w1):
        wid = _sc_worker_id()
        base = wid * per_w
        rows, g_sem, w_sem = (rows0, rows1), (g0, g1), (w0, w1)
        pltpu.sync_copy(idx_hbm.at[wid], idx_v)

        def fetch(c, b):
            return pltpu.make_async_copy(table_hbm.at[idx_v.at[c]], rows[b], g_sem[b])

        def flush(c, b):
            off = pl.multiple_of(base + c * SC_CHUNK, SC_CHUNK)
            return pltpu.make_async_copy(rows[b], out_hbm.at[pl.ds(off, SC_CHUNK)], w_sem[b])

        fetch(0, 0).start()

        @pl.loop(0, n_ch, step=2)
        def _(c0):
            for b in range(2):
                c = c0 + b
                fetch(c, b).wait()
                flush(c, b).start()

                @pl.when(c + 1 < n_ch)
                def _():
                    @pl.when(c >= 1)
                    def _():
                        flush(c - 1, 1 - b).wait()
                    fetch(c + 1, 1 - b).start()

        flush(n_ch - 2, 0).wait()
        flush(n_ch - 1, 1).wait()

    return gather(table, idx.reshape(SC_WORKERS, n_ch, SC_CHUNK))


def _unpacked_bf16(p):
    hi, lo = _unpack_bf16_pairs(p)
    return jnp.concatenate([hi.astype(BF16), lo.astype(BF16)], axis=1)


def _expert_kernel(be_ref, nu_ref, next_ref, slot_ref, xs_hbm, wg_hbm, wu_hbm, wd_hbm, y_ref,
                   g16_ref, u16_ref, d16_ref, gf_ref, uf_ref, df_ref, xbuf_ref, xsem, wsem, *, layer):
    i = pl.program_id(0)
    n_used = nu_ref[0]
    bm = xbuf_ref.shape[1]

    def fetch(b):
        slot = lax.rem(b, XS_SLOTS)
        rows = pl.ds(pl.multiple_of(b * bm, bm), bm)
        return pltpu.make_async_copy(xs_hbm.at[rows], xbuf_ref.at[slot], xsem.at[slot])

    @pl.when(i == 0)
    def _():
        fetch(0).start()

        @pl.when(n_used > 1)
        def _():
            fetch(1).start()

    @pl.when(i + 2 < n_used)
    def _():
        fetch(i + 2).start()

    e = be_ref[i]
    slot = slot_ref[e]
    f32_bufs = (gf_ref, uf_ref, df_ref)

    def wfetch(expert, dst_slot):
        return [pltpu.make_async_copy(w_hbm.at[layer, expert], buf.at[dst_slot], wsem.at[dst_slot, j])
                for j, (w_hbm, buf) in enumerate(zip((wg_hbm, wu_hbm, wd_hbm), f32_bufs))]

    @pl.when(i == 0)
    def _():
        for cp in wfetch(e, slot):
            cp.start()

    first_block_of_expert = jnp.logical_or(i == 0, e != be_ref[jnp.maximum(i - 1, 0)])

    @pl.when(jnp.logical_and(first_block_of_expert, i < n_used))
    def _():
        for cp in wfetch(e, slot):
            cp.wait()
        g16_ref[slot] = gf_ref[slot].astype(BF16)
        u16_ref[slot] = uf_ref[slot].astype(BF16)
        d16_ref[slot] = df_ref[slot].astype(BF16)
        nxt = next_ref[e]

        @pl.when(nxt >= 0)
        def _():
            for cp in wfetch(nxt, 1 - slot):
                cp.start()

    @pl.when(i < n_used)
    def _():
        fetch(i).wait()
        x = _unpacked_bf16(xbuf_ref[lax.rem(i, XS_SLOTS)])
        hh = _silu(_dot(x, g16_ref[slot])) * _dot(x, u16_ref[slot])
        y_ref[...] = _pack_bf16_pairs(_dot(hh.astype(BF16), d16_ref[slot]))


def _experts(block_e, n_used, next_expert, weight_slot, xs, wg, wu, wd, layer):
    nb = block_e.shape[0]
    bm = MOE_BLOCK
    hbm = pl.BlockSpec(memory_space=pl.ANY)
    two = lambda shape, dt: pltpu.VMEM((2,) + shape, dt)
    grid_spec = pltpu.PrefetchScalarGridSpec(
        num_scalar_prefetch=4,
        grid=(nb,),
        in_specs=[hbm, hbm, hbm, hbm],
        out_specs=pl.BlockSpec((bm, PACKED), lambda i, be, nu, nx, sl: (jnp.minimum(i, nu[0] - 1), 0)),
        scratch_shapes=[two((D_MODEL, D_EXPERT), BF16), two((D_MODEL, D_EXPERT), BF16), two((D_EXPERT, D_MODEL), BF16),
                        two((D_MODEL, D_EXPERT), F32), two((D_MODEL, D_EXPERT), F32), two((D_EXPERT, D_MODEL), F32),
                        pltpu.VMEM((XS_SLOTS, bm, PACKED), jnp.int32), pltpu.SemaphoreType.DMA((XS_SLOTS,)),
                        pltpu.SemaphoreType.DMA((2, 3))],
    )
    return pl.pallas_call(
        functools.partial(_expert_kernel, layer=layer),
        grid_spec=grid_spec,
        out_shape=jax.ShapeDtypeStruct((nb * bm, PACKED), jnp.int32),
        compiler_params=_params("arbitrary"),
        name="moe_experts",
    )(block_e, n_used, next_expert, weight_slot, xs, wg, wu, wd)


def _combine_kernel(g_ref, wk_ref, h_ref, xpk_ref, sg_ref, su_ref, sd_ref, lw_ref, lb_ref, *rest, with_proj):
    if with_proj:
        (wm_ref, ws_ref), (o_ref, proj_ref, small_ref) = rest[:2], rest[-3:]
        next_proj = (wm_ref, ws_ref, proj_ref, small_ref)
    else:
        o_ref, next_proj = rest[-1], None
    x = _unpacked_bf16(xpk_ref[...])
    hs = _silu(_dot(x, sg_ref[...])) * _dot(x, su_ref[...])
    shared = _dot(hs.astype(BF16), sd_ref[...])
    acc_hi = shared[:, :PACKED]
    acc_lo = shared[:, PACKED:]
    wk = wk_ref[...]
    for k in range(TOP_K):
        y_hi, y_lo = _unpack_bf16_pairs(g_ref[k])
        w = wk[:, k:k + 1]
        acc_hi = acc_hi + w * y_hi
        acc_lo = acc_lo + w * y_lo
    ffn = jnp.concatenate([acc_hi, acc_lo], axis=1)
    out = _layer_norm(ALPHA * h_ref[...] + ffn, lw_ref[...], lb_ref[...])
    o_ref[...] = out
    if next_proj is not None:
        wm_ref, ws_ref, proj_ref, small_ref = next_proj
        out16 = out.astype(BF16)
        proj_ref[...] = _dot(out16, wm_ref[...])
        small_ref[...] = _dot(out16, ws_ref[...])


def _combine_ln(g, wk, h, xpk, sg, su, sd, ln_w, ln_b, next_w, part, prev):
    t = h.shape[0]
    tm = COMBINE_TILE if next_w is not None else 2 * COMBINE_TILE
    n_blk = g.shape[1] // tm
    first = part * n_blk
    const = lambda shape: pl.BlockSpec(shape, lambda i: (0, 0))
    rows = lambda width: pl.BlockSpec((tm, width), lambda i: (i + first, 0))
    in_specs = [pl.BlockSpec((TOP_K, tm, PACKED), lambda i: (0, i, 0)), rows(TOP_K), rows(D_MODEL), rows(PACKED),
                const((D_MODEL, D_EXPERT)), const((D_MODEL, D_EXPERT)), const((D_EXPERT, D_MODEL)),
                const((1, D_MODEL)), const((1, D_MODEL))]
    args = [g, wk, h, xpk, sg.astype(BF16), su.astype(BF16), sd.astype(BF16),
            ln_w.reshape(1, D_MODEL), ln_b.reshape(1, D_MODEL)]
    out_specs = [rows(D_MODEL)]
    out_shape = [jax.ShapeDtypeStruct((t, D_MODEL), F32)]
    if next_w is not None:
        w_main, w_small = next_w
        n = w_main.shape[1]
        in_specs += [const((D_MODEL, n)), const((D_MODEL, LANES))]
        args += [w_main, w_small]
        out_specs += [rows(n), rows(LANES)]
        out_shape += [jax.ShapeDtypeStruct((t, n), F32), jax.ShapeDtypeStruct((t, LANES), F32)]
    aliases = {}
    if prev is not None:
        aliases = {len(args) + k: k for k in range(len(prev))}
        in_specs += [pl.BlockSpec(memory_space=pl.ANY)] * len(prev)
        args += list(prev)
    return pl.pallas_call(
        functools.partial(_combine_kernel, with_proj=next_w is not None),
        grid=(n_blk,),
        in_specs=in_specs,
        out_specs=out_specs,
        out_shape=out_shape,
        input_output_aliases=aliases,
        compiler_params=_params("parallel"),
        name="moe_combine_ln",
    )(*args)


def _moe_ln(h, hpk, idx, rank, wk, counts, wg, wu, wd, layer, sg, su, sd, ln_w, ln_b, next_w):
    t = h.shape[0]
    cnt = counts[:, 0].astype(jnp.int32)
    padded = (cnt + MOE_BLOCK - 1) // MOE_BLOCK * MOE_BLOCK
    pend = jnp.cumsum(padded)
    experts = jnp.arange(N_EXPERTS, dtype=jnp.int32)
    pstart_of_pick = jnp.sum(jnp.where(idx[:, :, None] == experts, pend - padded, 0), axis=-1)
    pos = pstart_of_pick + rank
    nb = -(-(t * TOP_K + N_EXPERTS * (MOE_BLOCK - 1)) // MOE_BLOCK)
    starts = jnp.arange(nb, dtype=jnp.int32) * MOE_BLOCK
    block_e = jnp.minimum(jnp.sum((pend[None, :] <= starts[:, None]).astype(jnp.int32), axis=1), N_EXPERTS - 1)
    n_used = (pend[-1] // MOE_BLOCK).astype(jnp.int32).reshape(1)
    has_rows = cnt > 0
    later = jnp.logical_and(has_rows[None, :], experts[None, :] > experts[:, None])
    next_expert = jnp.min(jnp.where(later, experts[None, :], N_EXPERTS), axis=1)
    next_expert = jnp.where(next_expert == N_EXPERTS, -1, next_expert).astype(jnp.int32)
    weight_slot = ((jnp.cumsum(has_rows) - has_rows) % 2).astype(jnp.int32)
    pos_chunks = pos.reshape(TOP_K, t // SC_CHUNK, SC_CHUNK).transpose(1, 0, 2)
    xs = _dispatch_rows(hpk, pos_chunks, nb * MOE_BLOCK)
    ys = _experts(block_e, n_used, next_expert, weight_slot, xs, wg, wu, wd, layer)
    part = t // COMBINE_PARTS
    gathered = [_gather_rows(ys, pos[:, p * part:(p + 1) * part].reshape(-1)).reshape(TOP_K, part, PACKED)
                for p in range(COMBINE_PARTS)]
    outs = None
    for p in range(COMBINE_PARTS):
        outs = _combine_ln(gathered[p], wk, h, hpk, sg, su, sd, ln_w, ln_b, next_w, p, outs)
    return outs


def _pad_cols(w, width=LANES):
    return jnp.pad(w, ((0, 0), (0, width - w.shape[1])))


def _even_proj_weights(w_in):
    a4 = 4 * MIX_HALF
    ng = 2 * HEADS
    w_main = jnp.concatenate([w_in[:, :a4], w_in[:, a4 + ng:]], axis=1).astype(BF16)
    w_gate = _pad_cols(w_in[:, a4:a4 + ng]).astype(BF16)
    return w_main, w_gate


def _even_mixer(proj, gates, batch, seq, gate_b, norm_w, conv_w, conv_b, wa, ba, wx, bx, lam):
    ya = _mlstm(proj, gates, gate_b, norm_w, batch, seq)
    yb = _rglru(proj, conv_w, conv_b, wa, ba, wx, bx, lam, batch, seq)
    return ya, yb


def _odd_proj_weights(w_in):
    c0 = MIX_HALF
    c1 = c0 + HEADS * GLA_DK
    c2 = c1 + HEADS * GLA_DK
    c3 = c2 + MIX_HALF
    c4 = c3 + MIX_HALF
    w_main = jnp.concatenate([w_in[:, :c0], _pad_heads(w_in[:, c0:c1], 1), _pad_heads(w_in[:, c1:c2], 1),
                              w_in[:, c2:c4]], axis=1).astype(BF16)
    w_low = _pad_cols(w_in[:, c4:]).astype(BF16)
    return w_main, w_low


def _odd_mixer(proj, glow, batch, seq, lam_re, lam_im, b_re, b_im, c_re, c_im, d_skip, log_dt,
               glu_w, glu_b, gate_w, gate_b, norm_w):
    tables = _s5_tables(lam_re, lam_im, b_re, b_im, c_re, c_im, log_dt)
    yc = _s5(proj, tables, d_skip, glu_w, glu_b, batch, seq)
    gw = jnp.pad(_pad_heads(gate_w, 1), ((0, LANES - GLA_GATE_RANK), (0, 0))).astype(BF16)
    gb = _pad_heads(gate_b.reshape(1, -1), 1)
    yd = _gla(proj, glow, gw, gb, norm_w, batch, seq)
    return yc, yd


def kernel(x, ln1_w, ln1_b, ln2_w, ln2_b, w_out, w_in_even, mlstm_gate_b, mlstm_norm_w, lru_conv_w, lru_conv_b, lru_wa, lru_ba, lru_wx, lru_bx, lru_lambda, w_in_odd, s5_lam_re, s5_lam_im, s5_b_re, s5_b_im, s5_c_re, s5_c_im, s5_d, s5_log_dt, s5_glu_w, s5_glu_b, gla_gate_w, gla_gate_b, gla_norm_w, router_w, router_bias, exp_w_gate, exp_w_up, exp_w_down, sh_w_gate, sh_w_up, sh_w_down):
    batch, seq, d = x.shape
    proj_w = [_even_proj_weights(w_in_even[layer // 2]) if layer % 2 == 0 else _odd_proj_weights(w_in_odd[layer // 2])
              for layer in range(DEPTH)]
    h = x.reshape(batch * seq, d)
    proj, small = _proj(h, *proj_w[0])
    for layer in range(DEPTH):
        j = layer // 2
        if layer % 2 == 0:
            y1, y2 = _even_mixer(proj, small, batch, seq, mlstm_gate_b[j], mlstm_norm_w[j],
                                 lru_conv_w[j], lru_conv_b[j], lru_wa[j], lru_ba[j], lru_wx[j],
                                 lru_bx[j], lru_lambda[j])
        else:
            y1, y2 = _odd_mixer(proj, small, batch, seq, s5_lam_re[j], s5_lam_im[j], s5_b_re[j],
                                s5_b_im[j], s5_c_re[j], s5_c_im[j], s5_d[j], s5_log_dt[j],
                                s5_glu_w[j], s5_glu_b[j], gla_gate_w[j], gla_gate_b[j], gla_norm_w[j])
        h, hpk, idx, rank, wk, counts = _out_proj_ln_route(y1, y2, h, w_out[layer], ln1_w[layer], ln1_b[layer],
                                                           router_w[layer], router_bias[layer])
        next_w = proj_w[layer + 1] if layer + 1 < DEPTH else None
        res = _moe_ln(h, hpk, idx, rank, wk, counts, exp_w_gate, exp_w_up, exp_w_down, layer,
                      sh_w_gate[layer], sh_w_up[layer], sh_w_down[layer], ln2_w[layer], ln2_b[layer], next_w)
        if next_w is None:
            (h,) = res
        else:
            h, proj, small = res
    return h.reshape(batch, seq, d)
```

```python
import functools
import math

import jax
import jax.numpy as jnp
from jax import lax
from jax.experimental import pallas as pl
from jax.experimental.pallas import tpu as pltpu
from jax.experimental.pallas import tpu_sc as plsc

F32 = jnp.float32
BF16 = jnp.bfloat16

D_MODEL = 1024
DEPTH = 2
MIX_HALF = 512
HEADS = 4
HEAD_DIM = 128
GLA_DK = 64
GLA_CHUNK = 64
GLA_GATE_RANK = 16
GLA_GATE_TEMP = 16.0
LRU_C = 8.0
LRU_CONV = 4
S5_GROUP = 16
S5_GROUPS = 32
S5_STATE = 64
S5_LANES = S5_GROUPS * S5_STATE
S5_BLOCKS = 4
N_EXPERTS = 64
N_GROUPS = 8
GROUP_SIZE = N_EXPERTS // N_GROUPS
TOP_K = 8
TOPK_GROUPS = 4
D_EXPERT = 256
ROUTED_SCALE = 2.5
ALPHA = (2.0 * DEPTH) ** 0.25
EPS = 1e-5
LANES = 128
SUBLANES = 8
NEG_INF = float("-inf")

VMEM_LIMIT = 56 * 1024 * 1024

MLSTM_CHUNK = 128
MLSTM_TILE = 1024
LRU_TILE = 1024
LRU_LOG_STEPS = 3
LRU_UNROLL = 4
S5_TILE = 512
S5_LOG_STEPS = 3
S5_UNROLL = True
GLA_UNROLL = 4
GLA_TILE = 1024
PROJ_TILE = 1024
OUT_TILE = 512
MOE_BLOCK = 1152
XS_SLOTS = 3
BASE_TILE = 512
COMBINE_TILE = 512
COMBINE_PARTS = 2
PACKED = D_MODEL // 2
SC_CHUNK = 64
SC_CORES = 2
SC_SUBCORES = 16
SC_WORKERS = SC_CORES * SC_SUBCORES


def _params(*sem):
    return pltpu.CompilerParams(dimension_semantics=sem, vmem_limit_bytes=VMEM_LIMIT)


def _split3(x):
    hi = x.astype(BF16)
    r1 = x - hi.astype(F32)
    mid = r1.astype(BF16)
    lo = (r1 - mid.astype(F32)).astype(BF16)
    return hi, mid, lo


def _dot(a, b):
    return jnp.dot(a, b, preferred_element_type=F32)


def _dot_nt(a, b):
    return lax.dot_general(a, b, (((1,), (1,)), ((), ())), preferred_element_type=F32)


def _dot_tn(a, b):
    return lax.dot_general(a, b, (((0,), (0,)), ((), ())), preferred_element_type=F32)


def _exact_left01(mask01_bf16, x):
    hi, mid, lo = _split3(x)
    return _dot(mask01_bf16, hi) + _dot(mask01_bf16, mid) + _dot(mask01_bf16, lo)


def _exact_right01(x, mask01_bf16):
    hi, mid, lo = _split3(x)
    return _dot(hi, mask01_bf16) + _dot(mid, mask01_bf16) + _dot(lo, mask01_bf16)


def _log_sigmoid(x):
    return jnp.minimum(x, 0.0) - jnp.log(1.0 + jnp.exp(-jnp.abs(x)))


def _sigmoid(x):
    return 1.0 / (1.0 + jnp.exp(-x))


def _gelu_tanh(x):
    c = math.sqrt(2.0 / math.pi)
    return 0.5 * x * (1.0 + jnp.tanh(c * (x + 0.044715 * (x * x * x))))


def _layer_norm(z, w, b):
    mu = jnp.mean(z, axis=-1, keepdims=True)
    zc = z - mu
    return zc * lax.rsqrt(jnp.mean(zc * zc, axis=-1, keepdims=True) + EPS) * w + b


def _proj_kernel(x_ref, w_ref, wg_ref, o_ref, og_ref):
    x = x_ref[...].astype(BF16)
    o_ref[...] = _dot(x, w_ref[...])
    og_ref[...] = _dot(x, wg_ref[...])


def _proj(x, w_main, w_small):
    t, d = x.shape
    n = w_main.shape[1]
    tm = PROJ_TILE
    return pl.pallas_call(
        _proj_kernel,
        grid=(t // tm,),
        in_specs=[pl.BlockSpec((tm, d), lambda i: (i, 0)),
                  pl.BlockSpec((d, n), lambda i: (0, 0)),
                  pl.BlockSpec((d, LANES), lambda i: (0, 0))],
        out_specs=[pl.BlockSpec((tm, n), lambda i: (i, 0)),
                   pl.BlockSpec((tm, LANES), lambda i: (i, 0))],
        out_shape=[jax.ShapeDtypeStruct((t, n), F32), jax.ShapeDtypeStruct((t, LANES), F32)],
        compiler_params=_params("parallel"),
        name="in_proj",
    )(x, w_main, w_small)


def _mlstm_kernel(q_ref, k_ref, v_ref, o_ref, gc_ref, gr_ref, bc_ref, br_ref, nw_ref,
                  y_ref, c_ref, m_ref, *, chunk, n_chunks):
    L = chunk

    @pl.when(pl.program_id(1) == 0)
    def _():
        c_ref[...] = jnp.zeros_like(c_ref)
        m_ref[...] = jnp.zeros_like(m_ref)

    ri = lax.broadcasted_iota(jnp.int32, (L, L), 0)
    ci = lax.broadcasted_iota(jnp.int32, (L, L), 1)
    causal = ci <= ri
    tril = causal.astype(BF16)
    triu = (ri <= ci).astype(BF16)
    ones_v = jnp.ones((L, HEAD_DIM), BF16)
    scale = HEAD_DIM ** -0.5

    def body(c, carry):
        r0 = pl.multiple_of(c * L, L)
        g_col = gc_ref[pl.ds(r0, L), :] + bc_ref[...]
        g_row = gr_ref[c] + br_ref[...]
        b_col_all = _exact_left01(tril, _log_sigmoid(g_col))
        b_row_all = _exact_right01(_log_sigmoid(g_row), triu)
        for h in range(HEADS):
            lo = h * HEAD_DIM
            q = q_ref[pl.ds(r0, L), lo:lo + HEAD_DIM].astype(BF16)
            k = k_ref[pl.ds(r0, L), lo:lo + HEAD_DIM] * scale
            v = v_ref[pl.ds(r0, L), lo:lo + HEAD_DIM].astype(BF16)
            v_aug = jnp.concatenate([v, ones_v], axis=1)
            i_rep = jnp.broadcast_to(g_col[:, h:h + 1], (L, LANES))
            b_rep = jnp.broadcast_to(b_col_all[:, HEADS + h:HEADS + h + 1], (L, LANES))
            i_row = g_row[h:h + 1, :]
            b_row = b_row_all[HEADS + h:HEADS + h + 1, :]
            b_last = b_rep[L - 1:L, :]
            m_prev = m_ref[h:h + 1, :]
            c_prev = c_ref[h]

            d_mat = jnp.where(causal, b_rep - b_row + i_row, NEG_INF)
            m_inter = b_rep + m_prev
            m_i = jnp.maximum(m_inter, jnp.max(d_mat, axis=1, keepdims=True))
            s = _dot_nt(q, k.astype(BF16)) * jnp.exp(d_mat - m_i)
            w_inter = jnp.exp(m_inter - m_i)
            intra = _dot(s.astype(BF16), v_aug)
            inter = _dot(q, c_prev.astype(BF16))
            num = intra[:, :HEAD_DIM] + w_inter * inter[:, :HEAD_DIM]
            den = intra[:, HEAD_DIM:] + w_inter * inter[:, HEAD_DIM:]
            hh = num / jnp.maximum(jnp.abs(den), jnp.exp(-m_i))

            w_loc = b_last - b_rep + i_rep
            m_loc = jnp.max(w_loc, axis=0, keepdims=True)
            kp = (k * jnp.exp(w_loc - m_loc)).astype(BF16)
            c_loc = _dot_tn(kp, v_aug)
            m_new = jnp.maximum(b_last + m_prev, m_loc)
            keep = jnp.exp(b_last + m_prev - m_new)
            add = jnp.exp(m_loc - m_new)
            c_ref[h] = (jnp.concatenate([keep, keep], axis=1) * c_prev
                        + jnp.concatenate([add, add], axis=1) * c_loc)
            m_ref[h:h + 1, :] = m_new

            hc = hh - jnp.mean(hh, axis=-1, keepdims=True)
            yn = hc * lax.rsqrt(jnp.mean(hc * hc, axis=-1, keepdims=True) + EPS)
            og = o_ref[pl.ds(r0, L), lo:lo + HEAD_DIM]
            y_ref[pl.ds(r0, L), lo:lo + HEAD_DIM] = (yn * nw_ref[:, lo:lo + HEAD_DIM] * _sigmoid(og)).astype(BF16)
        return carry

    lax.fori_loop(0, n_chunks, body, 0)


def _mlstm(proj, gates, gate_b, norm_w, batch, seq):
    t = batch * seq
    L = MLSTM_CHUNK
    assert L == LANES, "the kernel keeps per-row gate terms replicated over one vreg of lanes"
    ts = MLSTM_TILE
    nj = seq // ts
    nc = ts // L
    g_row = gates[:, :2 * HEADS].reshape(t // L, L, 2 * HEADS).transpose(0, 2, 1)
    b_col = jnp.zeros((1, LANES), F32).at[0, :2 * HEADS].set(gate_b)
    b_row = gate_b.reshape(2 * HEADS, 1)
    blk = lambda col: pl.BlockSpec((ts, MIX_HALF), lambda b, j, col=col: (b * nj + j, col))
    kern = functools.partial(_mlstm_kernel, chunk=L, n_chunks=nc)
    return pl.pallas_call(
        kern,
        grid=(batch, nj),
        in_specs=[blk(0), blk(1), blk(2), blk(3),
                  pl.BlockSpec((ts, LANES), lambda b, j: (b * nj + j, 0)),
                  pl.BlockSpec((nc, 2 * HEADS, L), lambda b, j: (b * nj + j, 0, 0)),
                  pl.BlockSpec((1, LANES), lambda b, j: (0, 0)),
                  pl.BlockSpec((2 * HEADS, 1), lambda b, j: (0, 0)),
                  pl.BlockSpec((1, MIX_HALF), lambda b, j: (0, 0))],
        out_specs=pl.BlockSpec((ts, MIX_HALF), lambda b, j: (b * nj + j, 0)),
        out_shape=jax.ShapeDtypeStruct((t, MIX_HALF), BF16),
        scratch_shapes=[pltpu.VMEM((HEADS, HEAD_DIM, 2 * HEAD_DIM), F32),
                        pltpu.VMEM((8, LANES), F32)],
        compiler_params=_params("arbitrary", "arbitrary"),
        name="mlstm",
    )(proj, proj, proj, proj, gates, g_row, b_col, b_row, norm_w.reshape(1, MIX_HALF))


def _rglru_kernel(xb_ref, gb_ref, cw_ref, cb_ref, wa_ref, ba_ref, wx_ref, bx_ref, lam_ref,
                  y_ref, xext_ref, h_ref, a_ref, u_ref, *, tile):
    @pl.when(pl.program_id(1) == 0)
    def _():
        xext_ref[0:8, :] = jnp.zeros((8, MIX_HALF), F32)
        h_ref[...] = jnp.zeros_like(h_ref)

    x = xb_ref[...]
    xext_ref[8:8 + tile, :] = x
    xc = cb_ref[...] + cw_ref[LRU_CONV - 1:LRU_CONV, :] * x
    for tap in range(LRU_CONV - 1):
        back = LRU_CONV - 1 - tap
        xc = xc + cw_ref[tap:tap + 1, :] * xext_ref[8 - back:8 - back + tile, :]
    xext_ref[0:8, :] = x[tile - 8:tile, :]

    xc16 = xc.astype(BF16)
    r_parts, i_parts = [], []
    for h in range(HEADS):
        lo = h * HEAD_DIM
        xh = xc16[:, lo:lo + HEAD_DIM]
        r_parts.append(_dot(xh, wa_ref[h]))
        i_parts.append(_dot(xh, wx_ref[h]))
    r = _sigmoid(jnp.concatenate(r_parts, axis=1) + ba_ref[...])
    ig = _sigmoid(jnp.concatenate(i_parts, axis=1) + bx_ref[...])
    lam = lam_ref[...]
    softplus_neg = jnp.maximum(-lam, 0.0) + jnp.log(1.0 + jnp.exp(-jnp.abs(lam)))
    log_a = -LRU_C * r * softplus_neg
    a = jnp.exp(log_a)
    th = jnp.tanh(log_a)
    u = jnp.sqrt(-2.0 * th / (1.0 - th)) * ig * xc

    a_ref[...] = a
    u_ref[...] = u
    rows = lax.broadcasted_iota(jnp.int32, (SUBLANES, MIX_HALF), 0)

    def group(i, h_prev):
        r0 = pl.multiple_of(i * SUBLANES, SUBLANES)
        ag = a_ref[pl.ds(r0, SUBLANES), :]
        ug = u_ref[pl.ds(r0, SUBLANES), :]
        for k in range(LRU_LOG_STEPS):
            keep = rows >= (1 << k)
            ug = ag * jnp.where(keep, pltpu.roll(ug, 1 << k, 0), 0.0) + ug
            ag = ag * jnp.where(keep, pltpu.roll(ag, 1 << k, 0), 1.0)
        hg = ug + ag * h_prev
        u_ref[pl.ds(r0, SUBLANES), :] = hg
        return hg[SUBLANES - 1:SUBLANES, :]

    h_last = lax.fori_loop(0, tile // SUBLANES, group, h_ref[0:1, :], unroll=LRU_UNROLL)
    h_ref[...] = jnp.broadcast_to(h_last, h_ref.shape)
    y_ref[...] = (u_ref[...] * _gelu_tanh(gb_ref[...])).astype(BF16)


def _rglru(proj, conv_w, conv_b, wa, ba, wx, bx, lam, batch, seq):
    t = batch * seq
    ts = LRU_TILE
    nj = seq // ts
    row = lambda a: a.reshape(1, MIX_HALF)
    const2 = lambda shape: pl.BlockSpec(shape, lambda b, j: (0, 0))
    const3 = lambda shape: pl.BlockSpec(shape, lambda b, j: (0, 0, 0))
    blk = lambda col: pl.BlockSpec((ts, MIX_HALF), lambda b, j, col=col: (b * nj + j, col))
    return pl.pallas_call(
        functools.partial(_rglru_kernel, tile=ts),
        grid=(batch, nj),
        in_specs=[blk(4), blk(5), const2((LRU_CONV, MIX_HALF)), const2((1, MIX_HALF)),
                  const3((HEADS, HEAD_DIM, HEAD_DIM)), const2((1, MIX_HALF)),
                  const3((HEADS, HEAD_DIM, HEAD_DIM)), const2((1, MIX_HALF)), const2((1, MIX_HALF))],
        out_specs=pl.BlockSpec((ts, MIX_HALF), lambda b, j: (b * nj + j, 0)),
        out_shape=jax.ShapeDtypeStruct((t, MIX_HALF), BF16),
        scratch_shapes=[pltpu.VMEM((ts + 8, MIX_HALF), F32), pltpu.VMEM((8, MIX_HALF), F32),
                        pltpu.VMEM((ts, MIX_HALF), F32), pltpu.VMEM((ts, MIX_HALF), F32)],
        compiler_params=_params("arbitrary", "arbitrary"),
        name="rglru",
    )(proj, proj, conv_w, row(conv_b), wa.astype(BF16), row(ba), wx.astype(BF16), row(bx), row(lam))


def _s5_kernel(u_ref, bre_ref, bim_ref, cre_ref, cim_ref, mre_ref, mim_ref, pre_ref, pim_ref, d_ref, gw_ref,
               gb_ref, y_ref, xr_ref, xi_ref, cr_ref, ci_ref, *, tile):
    @pl.when(pl.program_id(1) == 0)
    def _():
        cr_ref[...] = jnp.zeros_like(cr_ref)
        ci_ref[...] = jnp.zeros_like(ci_ref)

    u = u_ref[...]
    u16 = u.astype(BF16)
    blk_c = MIX_HALF // S5_BLOCKS
    blk_s = S5_LANES // S5_BLOCKS
    parts = []
    for j in range(S5_BLOCKS):
        lanes = slice(j * blk_s, (j + 1) * blk_s)
        uj = u16[:, j * blk_c:(j + 1) * blk_c]
        xr_ref[:, lanes] = _dot(uj, bre_ref[j])
        xi_ref[:, lanes] = _dot(uj, bim_ref[j])

        def group(i, carry, lanes=lanes):
            cr, ci = carry
            r0 = pl.multiple_of(i * SUBLANES, SUBLANES)
            xr = xr_ref[pl.ds(r0, SUBLANES), lanes]
            xi = xi_ref[pl.ds(r0, SUBLANES), lanes]
            for k in range(S5_LOG_STEPS):
                sr = pltpu.roll(xr, 1 << k, 0)
                si = pltpu.roll(xi, 1 << k, 0)
                mr = mre_ref[k, :, lanes]
                mi = mim_ref[k, :, lanes]
                xr, xi = xr + mr * sr - mi * si, xi + mr * si + mi * sr
            pr = pre_ref[:, lanes]
            pi = pim_ref[:, lanes]
            xr, xi = xr + pr * cr - pi * ci, xi + pr * ci + pi * cr
            xr_ref[pl.ds(r0, SUBLANES), lanes] = xr
            xi_ref[pl.ds(r0, SUBLANES), lanes] = xi
            return xr[SUBLANES - 1:SUBLANES, :], xi[SUBLANES - 1:SUBLANES, :]

        cr, ci = lax.fori_loop(0, tile // SUBLANES, group, (cr_ref[0:1, lanes], ci_ref[0:1, lanes]),
                               unroll=S5_UNROLL)
        cr_ref[0:1, lanes] = cr
        ci_ref[0:1, lanes] = ci
        parts.append(_dot(xr_ref[:, lanes].astype(BF16), cre_ref[j])
                     - _dot(xi_ref[:, lanes].astype(BF16), cim_ref[j]))
    y = jnp.concatenate(parts, axis=1) + d_ref[...] * u
    g = _gelu_tanh(y)
    y_ref[...] = (g * _sigmoid(_dot(g.astype(BF16), gw_ref[...]) + gb_ref[...])).astype(BF16)


def _s5_tables(lam_re, lam_im, b_re, b_im, c_re, c_im, log_dt):
    lr, li = lam_re.astype(F32), lam_im.astype(F32)
    dt = jnp.exp(log_dt.astype(F32))[:, None]
    mag = jnp.exp(lr * dt)
    abar_re = mag * jnp.cos(li * dt)
    abar_im = mag * jnp.sin(li * dt)
    den = lr * lr + li * li
    nr = abar_re - 1.0
    coef_re = (nr * lr + abar_im * li) / den
    coef_im = (abar_im * lr - nr * li) / den
    bbar_re = coef_re[..., None] * b_re - coef_im[..., None] * b_im
    bbar_im = coef_re[..., None] * b_im + coef_im[..., None] * b_re
    gpb = S5_GROUPS // S5_BLOCKS
    eye = jnp.eye(gpb, dtype=F32)

    def in_map(bb):
        bb = bb.reshape(S5_BLOCKS, gpb, S5_STATE, S5_GROUP)
        return jnp.einsum("jgph,gk->jghkp", bb, eye).reshape(S5_BLOCKS, gpb * S5_GROUP, gpb * S5_STATE)

    def out_map(cc):
        cc = cc.reshape(S5_BLOCKS, gpb, S5_GROUP, S5_STATE)
        return jnp.einsum("jghp,gk->jgpkh", cc, eye).reshape(S5_BLOCKS, gpb * S5_STATE, gpb * S5_GROUP)

    def power(n):
        n = jnp.asarray(n, F32)[..., None, None]
        pmag = jnp.exp(n * (lr * dt))
        shape = n.shape[:-2] + (S5_LANES,)
        return (pmag * jnp.cos(n * (li * dt))).reshape(shape), (pmag * jnp.sin(n * (li * dt))).reshape(shape)

    row = jnp.arange(SUBLANES)
    step = 2 ** jnp.arange(S5_LOG_STEPS)
    s_re, s_im = power(step)
    keep = (row[None, :] >= step[:, None])[..., None]
    m_re = jnp.where(keep, s_re[:, None, :], 0.0)
    m_im = jnp.where(keep, s_im[:, None, :], 0.0)
    p_re, p_im = power(row + 1)
    return (in_map(bbar_re).astype(BF16), in_map(bbar_im).astype(BF16),
            out_map(c_re.astype(F32)).astype(BF16), out_map(c_im.astype(F32)).astype(BF16),
            m_re, m_im, p_re, p_im)


def _s5(proj, tables, d_skip, glu_w, glu_b, batch, seq):
    t = batch * seq
    ts = S5_TILE
    nj = seq // ts
    bre, bim, cre, cim, m_re, m_im, p_re, p_im = tables
    blk_c = MIX_HALF // S5_BLOCKS
    blk_s = S5_LANES // S5_BLOCKS
    const2 = lambda shape: pl.BlockSpec(shape, lambda b, j: (0, 0))
    const3 = lambda shape: pl.BlockSpec(shape, lambda b, j: (0, 0, 0))
    return pl.pallas_call(
        functools.partial(_s5_kernel, tile=ts),
        grid=(batch, nj),
        in_specs=[pl.BlockSpec((ts, MIX_HALF), lambda b, j: (b * nj + j, 0)),
                  const3((S5_BLOCKS, blk_c, blk_s)), const3((S5_BLOCKS, blk_c, blk_s)),
                  const3((S5_BLOCKS, blk_s, blk_c)), const3((S5_BLOCKS, blk_s, blk_c)),
                  const3(m_re.shape), const3(m_im.shape), const2(p_re.shape), const2(p_im.shape),
                  const2((1, MIX_HALF)), const2((MIX_HALF, MIX_HALF)), const2((1, MIX_HALF))],
        out_specs=pl.BlockSpec((ts, MIX_HALF), lambda b, j: (b * nj + j, 0)),
        out_shape=jax.ShapeDtypeStruct((t, MIX_HALF), BF16),
        scratch_shapes=[pltpu.VMEM((ts, S5_LANES), F32), pltpu.VMEM((ts, S5_LANES), F32),
                        pltpu.VMEM((8, S5_LANES), F32), pltpu.VMEM((8, S5_LANES), F32)],
        compiler_params=_params("arbitrary", "arbitrary"),
        name="s5",
    )(proj, bre, bim, cre, cim, m_re, m_im, p_re, p_im, d_skip.reshape(1, MIX_HALF), glu_w.astype(BF16),
      glu_b.reshape(1, MIX_HALF))


def _gla_kernel(q_ref, k_ref, v_ref, r_ref, gl_ref, gw_ref, gb_ref, nw_ref, y_ref,
                st_ref, qd_ref, ki_ref, ke_ref, v16_ref, dec_ref, o_ref, *, tile, chunk):
    L = chunk
    nc = tile // L

    @pl.when(pl.program_id(1) == 0)
    def _():
        st_ref[...] = jnp.zeros_like(st_ref)

    z = _dot(gl_ref[...].astype(BF16), gw_ref[...]) + gb_ref[...]
    bcum = _log_sigmoid(z) * (1.0 / GLA_GATE_TEMP)
    row_in_chunk = lax.broadcasted_iota(jnp.int32, bcum.shape, 0) & (L - 1)
    s = 1
    while s < L:
        bcum = bcum + jnp.where(row_in_chunk >= s, pltpu.roll(bcum, s, 0), 0.0)
        s *= 2
    b3 = bcum.reshape(nc, L, MIX_HALF)
    b_last = b3[:, L - 1:L, :]
    k = k_ref[...]
    qd_ref[...] = (q_ref[...] * (GLA_DK ** -0.5) * jnp.exp(bcum)).astype(BF16)
    ki_ref[...] = (k * jnp.exp(-bcum)).astype(BF16)
    ke_ref[...] = (k.reshape(nc, L, MIX_HALF) * jnp.exp(b_last - b3)).reshape(tile, MIX_HALF).astype(BF16)
    v16_ref[...] = v_ref[...].astype(BF16)
    dec_ref[...] = jnp.exp(b_last)

    ri = lax.broadcasted_iota(jnp.int32, (L, L), 0)
    ci = lax.broadcasted_iota(jnp.int32, (L, L), 1)
    causal = ci <= ri

    def body(c, carry):
        r0 = pl.multiple_of(c * L, L)
        dec = dec_ref[c]
        for h in range(HEADS):
            lo = h * HEAD_DIM
            q_dec = qd_ref[pl.ds(r0, L), lo:lo + HEAD_DIM]
            v = v16_ref[pl.ds(r0, L), lo:lo + HEAD_DIM]
            st = st_ref[h]
            att = jnp.where(causal, _dot_nt(q_dec, ki_ref[pl.ds(r0, L), lo:lo + HEAD_DIM]), 0.0)
            o_ref[pl.ds(r0, L), lo:lo + HEAD_DIM] = (_dot(att.astype(BF16), v)
                                                     + _dot_nt(q_dec, st.astype(BF16)))
            st_ref[h] = dec[:, lo:lo + HEAD_DIM] * st + _dot_tn(v, ke_ref[pl.ds(r0, L), lo:lo + HEAD_DIM])
        return carry

    lax.fori_loop(0, nc, body, 0, unroll=GLA_UNROLL)

    rg = r_ref[...]
    gate = nw_ref[...] * (rg * _sigmoid(rg))
    for h in range(HEADS):
        lo = h * HEAD_DIM
        o = o_ref[:, lo:lo + HEAD_DIM]
        yn = o * lax.rsqrt(jnp.mean(o * o, axis=-1, keepdims=True) + EPS)
        y_ref[:, lo:lo + HEAD_DIM] = (yn * gate[:, lo:lo + HEAD_DIM]).astype(BF16)


def _gla(proj, glow, gate_w, gate_b, norm_w, batch, seq):
    t = batch * seq
    ts = GLA_TILE
    nj = seq // ts
    blk = lambda col: pl.BlockSpec((ts, MIX_HALF), lambda b, j, col=col: (b * nj + j, col))
    const2 = lambda shape: pl.BlockSpec(shape, lambda b, j: (0, 0))
    return pl.pallas_call(
        functools.partial(_gla_kernel, tile=ts, chunk=GLA_CHUNK),
        grid=(batch, nj),
        in_specs=[blk(1), blk(2), blk(3), blk(4),
                  pl.BlockSpec((ts, LANES), lambda b, j: (b * nj + j, 0)),
                  const2((LANES, MIX_HALF)), const2((1, MIX_HALF)), const2((1, MIX_HALF))],
        out_specs=pl.BlockSpec((ts, MIX_HALF), lambda b, j: (b * nj + j, 0)),
        out_shape=jax.ShapeDtypeStruct((t, MIX_HALF), BF16),
        scratch_shapes=[pltpu.VMEM((HEADS, HEAD_DIM, HEAD_DIM), F32),
                        pltpu.VMEM((ts, MIX_HALF), BF16), pltpu.VMEM((ts, MIX_HALF), BF16),
                        pltpu.VMEM((ts, MIX_HALF), BF16), pltpu.VMEM((ts, MIX_HALF), BF16),
                        pltpu.VMEM((ts // GLA_CHUNK, 1, MIX_HALF), F32),
                        pltpu.VMEM((ts, MIX_HALF), F32)],
        compiler_params=_params("arbitrary", "arbitrary"),
        name="gla",
    )(proj, proj, proj, proj, glow, gate_w, gate_b, norm_w.reshape(1, MIX_HALF))


def _pad_heads(w, axis):
    shape = list(w.shape)
    shape[axis:axis + 1] = [HEADS, GLA_DK]
    w = w.reshape(shape)
    pad = [(0, 0)] * w.ndim
    pad[axis + 1] = (0, HEAD_DIM - GLA_DK)
    w = jnp.pad(w, pad)
    shape[axis:axis + 2] = [HEADS * HEAD_DIM]
    return w.reshape(shape)


def _pack_bf16_pairs(z):
    hi = lax.bitcast_convert_type(z[:, :PACKED].astype(BF16).astype(F32), jnp.uint32)
    lo = lax.bitcast_convert_type(z[:, PACKED:].astype(BF16).astype(F32), jnp.uint32)
    word = (hi & jnp.uint32(0xFFFF0000)) | lax.shift_right_logical(lo, jnp.uint32(16))
    return lax.bitcast_convert_type(word, jnp.int32)


def _unpack_bf16_pairs(p):
    word = lax.bitcast_convert_type(p, jnp.uint32)
    hi = lax.bitcast_convert_type(word & jnp.uint32(0xFFFF0000), F32)
    lo = lax.bitcast_convert_type(lax.shift_left(word, jnp.uint32(16)), F32)
    return hi, lo


def _out_kernel(ya_ref, yb_ref, h_ref, w_ref, lw_ref, lb_ref, rw_ref, rb_ref,
                o_ref, opk_ref, idx_ref, rank_ref, wk_ref, cnt_ref, base_ref, *, tile):
    mixed = jnp.concatenate([ya_ref[...], yb_ref[...]], axis=1)
    z = ALPHA * h_ref[...] + _dot(mixed, w_ref[...])
    out = _layer_norm(z, lw_ref[...], lb_ref[...])
    o_ref[...] = out
    opk_ref[...] = _pack_bf16_pairs(out)
    _route_tile(out, rw_ref, rb_ref, idx_ref, rank_ref, wk_ref, cnt_ref, base_ref, tile)


def _out_proj_ln_route(ya, yb, h, w_out, ln_w, ln_b, router_w, router_bias):
    t = h.shape[0]
    tm = OUT_TILE
    const = lambda shape: pl.BlockSpec(shape, lambda i: (0, 0))
    per_tok = lambda dt: jax.ShapeDtypeStruct((TOP_K, t), dt)
    tok_blk = pl.BlockSpec((TOP_K, tm), lambda i: (0, i))
    return pl.pallas_call(
        functools.partial(_out_kernel, tile=tm),
        grid=(t // tm,),
        in_specs=[pl.BlockSpec((tm, MIX_HALF), lambda i: (i, 0)),
                  pl.BlockSpec((tm, MIX_HALF), lambda i: (i, 0)),
                  pl.BlockSpec((tm, D_MODEL), lambda i: (i, 0)),
                  const((D_MODEL, D_MODEL)), const((1, D_MODEL)), const((1, D_MODEL)),
                  const((N_EXPERTS, D_MODEL)), const((N_EXPERTS, 1))],
        out_specs=[pl.BlockSpec((tm, D_MODEL), lambda i: (i, 0)),
                   pl.BlockSpec((tm, PACKED), lambda i: (i, 0)),
                   tok_blk, tok_blk, pl.BlockSpec((tm, TOP_K), lambda i: (i, 0)), const((N_EXPERTS, LANES))],
        out_shape=[jax.ShapeDtypeStruct((t, D_MODEL), F32), jax.ShapeDtypeStruct((t, PACKED), jnp.int32),
                   per_tok(jnp.int32), per_tok(jnp.int32), jax.ShapeDtypeStruct((t, TOP_K), F32),
                   jax.ShapeDtypeStruct((N_EXPERTS, LANES), F32)],
        scratch_shapes=[pltpu.VMEM((N_EXPERTS, LANES), F32)],
        compiler_params=_params("arbitrary"),
        name="out_proj_ln_route",
    )(ya, yb, h, w_out.astype(BF16), ln_w.reshape(1, D_MODEL), ln_b.reshape(1, D_MODEL),
      router_w.T, router_bias.reshape(N_EXPERTS, 1))


def _first_index(hit, idx, big):
    return jnp.min(jnp.where(hit, idx, big), axis=0, keepdims=True)


def _route_tile(h, w_ref, b_ref, idx_ref, rank_ref, wk_ref, cnt_ref, base_ref, tile):
    @pl.when(pl.program_id(0) == 0)
    def _():
        base_ref[...] = jnp.zeros_like(base_ref)

    h_hi, h_mid, _ = _split3(h)
    w_hi, w_mid, _ = _split3(w_ref[...])
    logits = _dot_nt(w_hi, h_hi) + _dot_nt(w_hi, h_mid) + _dot_nt(w_mid, h_hi)
    scores = _sigmoid(logits)
    biased = scores + b_ref[...]

    sub = lax.broadcasted_iota(jnp.int32, (GROUP_SIZE, tile), 0)
    grp_rows = []
    for g in range(N_GROUPS):
        xg = biased[g * GROUP_SIZE:(g + 1) * GROUP_SIZE, :]
        m1 = jnp.max(xg, axis=0, keepdims=True)
        i1 = _first_index(xg == m1, sub, GROUP_SIZE)
        m2 = jnp.max(jnp.where(sub == i1, NEG_INF, xg), axis=0, keepdims=True)
        grp_rows.append(m1 + m2)
    gs = jnp.concatenate(grp_rows, axis=0)
    gsel = jnp.zeros((N_GROUPS, tile), F32)
    for _ in range(TOPK_GROUPS):
        mx = jnp.max(gs, axis=0, keepdims=True)
        hit = sub == _first_index(gs == mx, sub, N_GROUPS)
        gsel = jnp.where(hit, 1.0, gsel)
        gs = jnp.where(hit, NEG_INF, gs)
    emask = jnp.concatenate(
        [jnp.broadcast_to(gsel[g:g + 1, :], (GROUP_SIZE, tile)) for g in range(N_GROUPS)], axis=0)

    eidx = lax.broadcasted_iota(jnp.int32, (N_EXPERTS, tile), 0)
    cand = jnp.where(emask > 0.5, biased, NEG_INF)
    sel = jnp.zeros((N_EXPERTS, tile), F32)
    picks = []
    for _ in range(TOP_K):
        mx = jnp.max(cand, axis=0, keepdims=True)
        first = _first_index(cand == mx, eidx, N_EXPERTS)
        hit = eidx == first
        picks.append(first)
        sel = jnp.where(hit, 1.0, sel)
        cand = jnp.where(hit, NEG_INF, cand)
    picked = jnp.where(sel > 0.5, scores, 0.0)
    wts = picked / jnp.sum(picked, axis=0, keepdims=True) * ROUTED_SCALE

    ri = lax.broadcasted_iota(jnp.int32, (tile, tile), 0)
    ci = lax.broadcasted_iota(jnp.int32, (tile, tile), 1)
    before = (ri < ci).astype(BF16)
    prior = _dot(sel.astype(BF16), before) + base_ref[:, 0:1]
    ranks = [jnp.sum(jnp.where(eidx == p, prior, 0.0), axis=0, keepdims=True) for p in picks]
    wsel = [jnp.sum(jnp.where(eidx == p, wts, 0.0), axis=0, keepdims=True) for p in picks]
    idx_ref[...] = jnp.concatenate(picks, axis=0)
    rank_ref[...] = jnp.concatenate(ranks, axis=0).astype(jnp.int32)
    wk_ref[...] = jnp.concatenate(wsel, axis=0).T
    total = base_ref[...] + jnp.sum(sel, axis=1, keepdims=True)
    base_ref[...] = total
    cnt_ref[...] = total


def _silu(x):
    return x * _sigmoid(x)


def _sc_mesh():
    return plsc.VectorSubcoreMesh(core_axis_name="c", subcore_axis_name="s")


def _sc_worker_id():
    return lax.axis_index("s") * SC_CORES + lax.axis_index("c")


def _dispatch_rows(xpk, pos_chunks, n_rows):
    t = xpk.shape[0]
    n_ch = t // SC_WORKERS // SC_CHUNK

    @functools.partial(
        pl.kernel, mesh=_sc_mesh(),
        out_type=jax.ShapeDtypeStruct((n_rows, PACKED), jnp.int32),
        scratch_types=[pltpu.VMEM((TOP_K, SC_CHUNK), jnp.int32),
                       pltpu.VMEM((SC_CHUNK, PACKED), jnp.int32),
                       pltpu.SemaphoreType.DMA],
        name="moe_dispatch",
    )
    def scatter(x_hbm, pos_hbm, out_hbm, idx_v, rows_v, sem):
        wid = _sc_worker_id()

        @pl.loop(0, n_ch)
        def _(c):
            chunk = wid * n_ch + c
            off = pl.multiple_of(chunk * SC_CHUNK, SC_CHUNK)
            pltpu.sync_copy(pos_hbm.at[chunk], idx_v)
            pltpu.sync_copy(x_hbm.at[pl.ds(off, SC_CHUNK)], rows_v)
            copies = [pltpu.async_copy(rows_v, out_hbm.at[idx_v.at[k]], sem) for k in range(TOP_K)]
            for cp in copies:
                cp.wait()

    return scatter(xpk, pos_chunks)


def _gather_rows(table, idx):
    n = idx.shape[0]
    per_w = n // SC_WORKERS
    n_ch = per_w // SC_CHUNK
    assert n_ch % 2 == 0 and n_ch >= 2

    @functools.partial(
        pl.kernel, mesh=_sc_mesh(),
        out_type=jax.ShapeDtypeStruct((n, PACKED), jnp.int32),
        scratch_types=[pltpu.VMEM((n_ch, SC_CHUNK), jnp.int32),
                       pltpu.VMEM((SC_CHUNK, PACKED), jnp.int32), pltpu.VMEM((SC_CHUNK, PACKED), jnp.int32),
                       pltpu.SemaphoreType.DMA, pltpu.SemaphoreType.DMA,
                       pltpu.SemaphoreType.DMA, pltpu.SemaphoreType.DMA],
        name="moe_gather",
    )
    def gather(table_hbm, idx_hbm, out_hbm, idx_v, rows0, rows1, g0, g1, w0, w1):
        wid = _sc_worker_id()
        base = wid * per_w
        rows, g_sem, w_sem = (rows0, rows1), (g0, g1), (w0, w1)
        pltpu.sync_copy(idx_hbm.at[wid], idx_v)

        def fetch(c, b):
            return pltpu.make_async_copy(table_hbm.at[idx_v.at[c]], rows[b], g_sem[b])

        def flush(c, b):
            off = pl.multiple_of(base + c * SC_CHUNK, SC_CHUNK)
            return pltpu.make_async_copy(rows[b], out_hbm.at[pl.ds(off, SC_CHUNK)], w_sem[b])

        fetch(0, 0).start()

        @pl.loop(0, n_ch, step=2)
        def _(c0):
            for b in range(2):
                c = c0 + b
                fetch(c, b).wait()
                flush(c, b).start()

                @pl.when(c + 1 < n_ch)
                def _():
                    @pl.when(c >= 1)
                    def _():
                        flush(c - 1, 1 - b).wait()
                    fetch(c + 1, 1 - b).start()

        flush(n_ch - 2, 0).wait()
        flush(n_ch - 1, 1).wait()

    return gather(table, idx.reshape(SC_WORKERS, n_ch, SC_CHUNK))


def _unpacked_bf16(p):
    hi, lo = _unpack_bf16_pairs(p)
    return jnp.concatenate([hi.astype(BF16), lo.astype(BF16)], axis=1)


def _expert_kernel(be_ref, nu_ref, next_ref, slot_ref, xs_hbm, wg_hbm, wu_hbm, wd_hbm, y_ref,
                   g16_ref, u16_ref, d16_ref, gf_ref, uf_ref, df_ref, xbuf_ref, xsem, wsem, *, layer):
    i = pl.program_id(0)
    n_used = nu_ref[0]
    bm = xbuf_ref.shape[1]

    def fetch(b):
        slot = lax.rem(b, XS_SLOTS)
        rows = pl.ds(pl.multiple_of(b * bm, bm), bm)
        return pltpu.make_async_copy(xs_hbm.at[rows], xbuf_ref.at[slot], xsem.at[slot])

    @pl.when(i == 0)
    def _():
        fetch(0).start()

        @pl.when(n_used > 1)
        def _():
            fetch(1).start()

    @pl.when(i + 2 < n_used)
    def _():
        fetch(i + 2).start()

    e = be_ref[i]
    slot = slot_ref[e]
    f32_bufs = (gf_ref, uf_ref, df_ref)

    def wfetch(expert, dst_slot):
        return [pltpu.make_async_copy(w_hbm.at[layer, expert], buf.at[dst_slot], wsem.at[dst_slot, j])
                for j, (w_hbm, buf) in enumerate(zip((wg_hbm, wu_hbm, wd_hbm), f32_bufs))]

    @pl.when(i == 0)
    def _():
        for cp in wfetch(e, slot):
            cp.start()

    first_block_of_expert = jnp.logical_or(i == 0, e != be_ref[jnp.maximum(i - 1, 0)])

    @pl.when(jnp.logical_and(first_block_of_expert, i < n_used))
    def _():
        for cp in wfetch(e, slot):
            cp.wait()
        g16_ref[slot] = gf_ref[slot].astype(BF16)
        u16_ref[slot] = uf_ref[slot].astype(BF16)
        d16_ref[slot] = df_ref[slot].astype(BF16)
        nxt = next_ref[e]

        @pl.when(nxt >= 0)
        def _():
            for cp in wfetch(nxt, 1 - slot):
                cp.start()

    @pl.when(i < n_used)
    def _():
        fetch(i).wait()
        x = _unpacked_bf16(xbuf_ref[lax.rem(i, XS_SLOTS)])
        hh = _silu(_dot(x, g16_ref[slot])) * _dot(x, u16_ref[slot])
        y_ref[...] = _pack_bf16_pairs(_dot(hh.astype(BF16), d16_ref[slot]))


def _experts(block_e, n_used, next_expert, weight_slot, xs, wg, wu, wd, layer):
    nb = block_e.shape[0]
    bm = MOE_BLOCK
    hbm = pl.BlockSpec(memory_space=pl.ANY)
    two = lambda shape, dt: pltpu.VMEM((2,) + shape, dt)
    grid_spec = pltpu.PrefetchScalarGridSpec(
        num_scalar_prefetch=4,
        grid=(nb,),
        in_specs=[hbm, hbm, hbm, hbm],
        out_specs=pl.BlockSpec((bm, PACKED), lambda i, be, nu, nx, sl: (jnp.minimum(i, nu[0] - 1), 0)),
        scratch_shapes=[two((D_MODEL, D_EXPERT), BF16), two((D_MODEL, D_EXPERT), BF16), two((D_EXPERT, D_MODEL), BF16),
                        two((D_MODEL, D_EXPERT), F32), two((D_MODEL, D_EXPERT), F32), two((D_EXPERT, D_MODEL), F32),
                        pltpu.VMEM((XS_SLOTS, bm, PACKED), jnp.int32), pltpu.SemaphoreType.DMA((XS_SLOTS,)),
                        pltpu.SemaphoreType.DMA((2, 3))],
    )
    return pl.pallas_call(
        functools.partial(_expert_kernel, layer=layer),
        grid_spec=grid_spec,
        out_shape=jax.ShapeDtypeStruct((nb * bm, PACKED), jnp.int32),
        compiler_params=_params("arbitrary"),
        name="moe_experts",
    )(block_e, n_used, next_expert, weight_slot, xs, wg, wu, wd)


def _base_kernel(h_ref, xpk_ref, sg_ref, su_ref, sd_ref, o_ref):
    x = _unpacked_bf16(xpk_ref[...])
    hs = _silu(_dot(x, sg_ref[...])) * _dot(x, su_ref[...])
    o_ref[...] = ALPHA * h_ref[...] + _dot(hs.astype(BF16), sd_ref[...])


def _residual_plus_shared(h, xpk, sg, su, sd):
    t = h.shape[0]
    tm = BASE_TILE
    const = lambda shape: pl.BlockSpec(shape, lambda i: (0, 0))
    return pl.pallas_call(
        _base_kernel,
        grid=(t // tm,),
        in_specs=[pl.BlockSpec((tm, D_MODEL), lambda i: (i, 0)), pl.BlockSpec((tm, PACKED), lambda i: (i, 0)),
                  const((D_MODEL, D_EXPERT)), const((D_MODEL, D_EXPERT)), const((D_EXPERT, D_MODEL))],
        out_specs=pl.BlockSpec((tm, D_MODEL), lambda i: (i, 0)),
        out_shape=jax.ShapeDtypeStruct((t, D_MODEL), F32),
        compiler_params=_params("parallel"),
        name="moe_shared_base",
    )(h, xpk, sg.astype(BF16), su.astype(BF16), sd.astype(BF16))


def _combine_kernel(g_ref, wk_ref, base_ref, lw_ref, lb_ref, *rest, with_proj):
    if with_proj:
        (wm_ref, ws_ref), (o_ref, proj_ref, small_ref) = rest[:2], rest[-3:]
        next_proj = (wm_ref, ws_ref, proj_ref, small_ref)
    else:
        o_ref, next_proj = rest[-1], None
    acc_hi = base_ref[:, :PACKED]
    acc_lo = base_ref[:, PACKED:]
    wk = wk_ref[...]
    for k in range(TOP_K):
        y_hi, y_lo = _unpack_bf16_pairs(g_ref[k])
        w = wk[:, k:k + 1]
        acc_hi = acc_hi + w * y_hi
        acc_lo = acc_lo + w * y_lo
    out = _layer_norm(jnp.concatenate([acc_hi, acc_lo], axis=1), lw_ref[...], lb_ref[...])
    o_ref[...] = out
    if next_proj is not None:
        wm_ref, ws_ref, proj_ref, small_ref = next_proj
        out16 = out.astype(BF16)
        proj_ref[...] = _dot(out16, wm_ref[...])
        small_ref[...] = _dot(out16, ws_ref[...])


def _combine_ln(g, wk, base, ln_w, ln_b, next_w, part, prev):
    t = base.shape[0]
    tm = COMBINE_TILE
    n_blk = g.shape[1] // tm
    first = part * n_blk
    const = lambda shape: pl.BlockSpec(shape, lambda i: (0, 0))
    rows = lambda width: pl.BlockSpec((tm, width), lambda i: (i + first, 0))
    in_specs = [pl.BlockSpec((TOP_K, tm, PACKED), lambda i: (0, i, 0)), rows(TOP_K), rows(D_MODEL),
                const((1, D_MODEL)), const((1, D_MODEL))]
    args = [g, wk, base, ln_w.reshape(1, D_MODEL), ln_b.reshape(1, D_MODEL)]
    out_specs = [rows(D_MODEL)]
    out_shape = [jax.ShapeDtypeStruct((t, D_MODEL), F32)]
    if next_w is not None:
        w_main, w_small = next_w
        n = w_main.shape[1]
        in_specs += [const((D_MODEL, n)), const((D_MODEL, LANES))]
        args += [w_main, w_small]
        out_specs += [rows(n), rows(LANES)]
        out_shape += [jax.ShapeDtypeStruct((t, n), F32), jax.ShapeDtypeStruct((t, LANES), F32)]
    aliases = {}
    if prev is not None:
        aliases = {len(args) + k: k for k in range(len(prev))}
        in_specs += [pl.BlockSpec(memory_space=pl.ANY)] * len(prev)
        args += list(prev)
    return pl.pallas_call(
        functools.partial(_combine_kernel, with_proj=next_w is not None),
        grid=(n_blk,),
        in_specs=in_specs,
        out_specs=out_specs,
        out_shape=out_shape,
        input_output_aliases=aliases,
        compiler_params=_params("parallel"),
        name="moe_combine_ln",
    )(*args)


def _moe_ln(h, hpk, idx, rank, wk, counts, wg, wu, wd, layer, sg, su, sd, ln_w, ln_b, next_w):
    t = h.shape[0]
    cnt = counts[:, 0].astype(jnp.int32)
    padded = (cnt + MOE_BLOCK - 1) // MOE_BLOCK * MOE_BLOCK
    pend = jnp.cumsum(padded)
    experts = jnp.arange(N_EXPERTS, dtype=jnp.int32)
    pstart_of_pick = jnp.sum(jnp.where(idx[:, :, None] == experts, pend - padded, 0), axis=-1)
    pos = pstart_of_pick + rank
    nb = -(-(t * TOP_K + N_EXPERTS * (MOE_BLOCK - 1)) // MOE_BLOCK)
    starts = jnp.arange(nb, dtype=jnp.int32) * MOE_BLOCK
    block_e = jnp.minimum(jnp.sum((pend[None, :] <= starts[:, None]).astype(jnp.int32), axis=1), N_EXPERTS - 1)
    n_used = (pend[-1] // MOE_BLOCK).astype(jnp.int32).reshape(1)
    has_rows = cnt > 0
    later = jnp.logical_and(has_rows[None, :], experts[None, :] > experts[:, None])
    next_expert = jnp.min(jnp.where(later, experts[None, :], N_EXPERTS), axis=1)
    next_expert = jnp.where(next_expert == N_EXPERTS, -1, next_expert).astype(jnp.int32)
    weight_slot = ((jnp.cumsum(has_rows) - has_rows) % 2).astype(jnp.int32)
    pos_chunks = pos.reshape(TOP_K, t // SC_CHUNK, SC_CHUNK).transpose(1, 0, 2)
    xs = _dispatch_rows(hpk, pos_chunks, nb * MOE_BLOCK)
    base = _residual_plus_shared(h, hpk, sg, su, sd)
    ys = _experts(block_e, n_used, next_expert, weight_slot, xs, wg, wu, wd, layer)
    part = t // COMBINE_PARTS
    gathered = [_gather_rows(ys, pos[:, p * part:(p + 1) * part].reshape(-1)).reshape(TOP_K, part, PACKED)
                for p in range(COMBINE_PARTS)]
    outs = None
    for p in range(COMBINE_PARTS):
        outs = _combine_ln(gathered[p], wk, base, ln_w, ln_b, next_w, p, outs)
    return outs


def _pad_cols(w, width=LANES):
    return jnp.pad(w, ((0, 0), (0, width - w.shape[1])))


def _even_proj_weights(w_in):
    a4 = 4 * MIX_HALF
    ng = 2 * HEADS
    w_main = jnp.concatenate([w_in[:, :a4], w_in[:, a4 + ng:]], axis=1).astype(BF16)
    w_gate = _pad_cols(w_in[:, a4:a4 + ng]).astype(BF16)
    return w_main, w_gate


def _even_mixer(proj, gates, batch, seq, gate_b, norm_w, conv_w, conv_b, wa, ba, wx, bx, lam):
    ya = _mlstm(proj, gates, gate_b, norm_w, batch, seq)
    yb = _rglru(proj, conv_w, conv_b, wa, ba, wx, bx, lam, batch, seq)
    return ya, yb


def _odd_proj_weights(w_in):
    c0 = MIX_HALF
    c1 = c0 + HEADS * GLA_DK
    c2 = c1 + HEADS * GLA_DK
    c3 = c2 + MIX_HALF
    c4 = c3 + MIX_HALF
    w_main = jnp.concatenate([w_in[:, :c0], _pad_heads(w_in[:, c0:c1], 1), _pad_heads(w_in[:, c1:c2], 1),
                              w_in[:, c2:c4]], axis=1).astype(BF16)
    w_low = _pad_cols(w_in[:, c4:]).astype(BF16)
    return w_main, w_low


def _odd_mixer(proj, glow, batch, seq, lam_re, lam_im, b_re, b_im, c_re, c_im, d_skip, log_dt,
               glu_w, glu_b, gate_w, gate_b, norm_w):
    tables = _s5_tables(lam_re, lam_im, b_re, b_im, c_re, c_im, log_dt)
    yc = _s5(proj, tables, d_skip, glu_w, glu_b, batch, seq)
    gw = jnp.pad(_pad_heads(gate_w, 1), ((0, LANES - GLA_GATE_RANK), (0, 0))).astype(BF16)
    gb = _pad_heads(gate_b.reshape(1, -1), 1)
    yd = _gla(proj, glow, gw, gb, norm_w, batch, seq)
    return yc, yd


def kernel(x, ln1_w, ln1_b, ln2_w, ln2_b, w_out, w_in_even, mlstm_gate_b, mlstm_norm_w, lru_conv_w, lru_conv_b, lru_wa, lru_ba, lru_wx, lru_bx, lru_lambda, w_in_odd, s5_lam_re, s5_lam_im, s5_b_re, s5_b_im, s5_c_re, s5_c_im, s5_d, s5_log_dt, s5_glu_w, s5_glu_b, gla_gate_w, gla_gate_b, gla_norm_w, router_w, router_bias, exp_w_gate, exp_w_up, exp_w_down, sh_w_gate, sh_w_up, sh_w_down):
    batch, seq, d = x.shape
    proj_w = [_even_proj_weights(w_in_even[layer // 2]) if layer % 2 == 0 else _odd_proj_weights(w_in_odd[layer // 2])
              for layer in range(DEPTH)]
    h = x.reshape(batch * seq, d)
    proj, small = _proj(h, *proj_w[0])
    for layer in range(DEPTH):
        j = layer // 2
        if layer % 2 == 0:
            y1, y2 = _even_mixer(proj, small, batch, seq, mlstm_gate_b[j], mlstm_norm_w[j],
                                 lru_conv_w[j], lru_conv_b[j], lru_wa[j], lru_ba[j], lru_wx[j],
                                 lru_bx[j], lru_lambda[j])
        else:
            y1, y2 = _odd_mixer(proj, small, batch, seq, s5_lam_re[j], s5_lam_im[j], s5_b_re[j],
                                s5_b_im[j], s5_c_re[j], s5_c_im[j], s5_d[j], s5_log_dt[j],
                                s5_glu_w[j], s5_glu_b[j], gla_gate_w[j], gla_gate_b[j], gla_norm_w[j])
        h, hpk, idx, rank, wk, counts = _out_proj_ln_route(y1, y2, h, w_out[layer], ln1_w[layer], ln1_b[layer],
                                                           router_w[layer], router_bias[layer])
        next_w = proj_w[layer + 1] if layer + 1 < DEPTH else None
        res = _moe_ln(h, hpk, idx, rank, wk, counts, exp_w_gate, exp_w_up, exp_w_down, layer,
                      sh_w_gate[layer], sh_w_up[layer], sh_w_down[layer], ln2_w[layer], ln2_b[layer], next_w)
        if next_w is None:
            (h,) = res
        else:
            h, proj, small = res
    return h.reshape(batch, seq, d)
```

```python
import functools
import math

import jax
import jax.numpy as jnp
from jax import lax
from jax.experimental import pallas as pl
from jax.experimental.pallas import tpu as pltpu
from jax.experimental.pallas import tpu_sc as plsc

F32 = jnp.float32
BF16 = jnp.bfloat16

D_MODEL = 1024
DEPTH = 2
MIX_HALF = 512
HEADS = 4
HEAD_DIM = 128
GLA_DK = 64
GLA_CHUNK = 64
GLA_GATE_RANK = 16
GLA_GATE_TEMP = 16.0
LRU_C = 8.0
LRU_CONV = 4
S5_GROUP = 16
S5_GROUPS = 32
S5_STATE = 64
S5_LANES = S5_GROUPS * S5_STATE
S5_BLOCKS = 4
N_EXPERTS = 64
N_GROUPS = 8
GROUP_SIZE = N_EXPERTS // N_GROUPS
TOP_K = 8
TOPK_GROUPS = 4
D_EXPERT = 256
ROUTED_SCALE = 2.5
ALPHA = (2.0 * DEPTH) ** 0.25
EPS = 1e-5
LANES = 128
SUBLANES = 8
NEG_INF = float("-inf")

VMEM_LIMIT = 56 * 1024 * 1024

MLSTM_CHUNK = 128
MLSTM_TILE = 1024
MLSTM_UNROLL = 2
LRU_TILE = 1024
LRU_LOG_STEPS = 3
LRU_UNROLL = 4
S5_TILE = 512
S5_LOG_STEPS = 3
S5_UNROLL = True
GLA_UNROLL = 4
GLA_TILE = 1024
PROJ_TILE = 1024
OUT_TILE = 512
MOE_BLOCK = 1152
XS_SLOTS = 3
COMBINE_TILE = 256
COMBINE_PARTS = 2
PACKED = D_MODEL // 2
SC_CHUNK = 64
SC_CORES = 2
SC_SUBCORES = 16
SC_WORKERS = SC_CORES * SC_SUBCORES


def _params(*sem):
    return pltpu.CompilerParams(dimension_semantics=sem, vmem_limit_bytes=VMEM_LIMIT)


def _split3(x):
    hi = x.astype(BF16)
    r1 = x - hi.astype(F32)
    mid = r1.astype(BF16)
    lo = (r1 - mid.astype(F32)).astype(BF16)
    return hi, mid, lo


def _dot(a, b):
    return jnp.dot(a, b, preferred_element_type=F32)


def _dot_nt(a, b):
    return lax.dot_general(a, b, (((1,), (1,)), ((), ())), preferred_element_type=F32)


def _dot_tn(a, b):
    return lax.dot_general(a, b, (((0,), (0,)), ((), ())), preferred_element_type=F32)


def _exact_left01(mask01_bf16, x):
    hi, mid, lo = _split3(x)
    return _dot(mask01_bf16, hi) + _dot(mask01_bf16, mid) + _dot(mask01_bf16, lo)


def _exact_right01(x, mask01_bf16):
    hi, mid, lo = _split3(x)
    return _dot(hi, mask01_bf16) + _dot(mid, mask01_bf16) + _dot(lo, mask01_bf16)


def _log_sigmoid(x):
    return jnp.minimum(x, 0.0) - jnp.log(1.0 + jnp.exp(-jnp.abs(x)))


def _sigmoid(x):
    return 1.0 / (1.0 + jnp.exp(-x))


def _gelu_tanh(x):
    c = math.sqrt(2.0 / math.pi)
    return 0.5 * x * (1.0 + jnp.tanh(c * (x + 0.044715 * (x * x * x))))


def _layer_norm(z, w, b):
    mu = jnp.mean(z, axis=-1, keepdims=True)
    zc = z - mu
    return zc * lax.rsqrt(jnp.mean(zc * zc, axis=-1, keepdims=True) + EPS) * w + b


def _proj_kernel(x_ref, w_ref, wg_ref, o_ref, og_ref):
    x = x_ref[...].astype(BF16)
    o_ref[...] = _dot(x, w_ref[...])
    og_ref[...] = _dot(x, wg_ref[...])


def _proj(x, w_main, w_small):
    t, d = x.shape
    n = w_main.shape[1]
    tm = PROJ_TILE
    return pl.pallas_call(
        _proj_kernel,
        grid=(t // tm,),
        in_specs=[pl.BlockSpec((tm, d), lambda i: (i, 0)),
                  pl.BlockSpec((d, n), lambda i: (0, 0)),
                  pl.BlockSpec((d, LANES), lambda i: (0, 0))],
        out_specs=[pl.BlockSpec((tm, n), lambda i: (i, 0)),
                   pl.BlockSpec((tm, LANES), lambda i: (i, 0))],
        out_shape=[jax.ShapeDtypeStruct((t, n), F32), jax.ShapeDtypeStruct((t, LANES), F32)],
        compiler_params=_params("parallel"),
        name="in_proj",
    )(x, w_main, w_small)


def _mlstm_kernel(q_ref, k_ref, v_ref, o_ref, gc_ref, gr_ref, bc_ref, br_ref, nw_ref,
                  y_ref, c_ref, m_ref, *, chunk, n_chunks):
    L = chunk

    @pl.when(pl.program_id(1) == 0)
    def _():
        c_ref[...] = jnp.zeros_like(c_ref)
        m_ref[...] = jnp.zeros_like(m_ref)

    ri = lax.broadcasted_iota(jnp.int32, (L, L), 0)
    ci = lax.broadcasted_iota(jnp.int32, (L, L), 1)
    causal = ci <= ri
    tril = causal.astype(BF16)
    triu = (ri <= ci).astype(BF16)
    ones_v = jnp.ones((L, HEAD_DIM), BF16)
    scale = HEAD_DIM ** -0.5

    def body(c, carry):
        r0 = pl.multiple_of(c * L, L)
        g_col = gc_ref[pl.ds(r0, L), :] + bc_ref[...]
        g_row = gr_ref[c] + br_ref[...]
        b_col_all = _exact_left01(tril, _log_sigmoid(g_col))
        b_row_all = _exact_right01(_log_sigmoid(g_row), triu)
        for h in range(HEADS):
            lo = h * HEAD_DIM
            q = q_ref[pl.ds(r0, L), lo:lo + HEAD_DIM].astype(BF16)
            k = k_ref[pl.ds(r0, L), lo:lo + HEAD_DIM] * scale
            v = v_ref[pl.ds(r0, L), lo:lo + HEAD_DIM].astype(BF16)
            v_aug = jnp.concatenate([v, ones_v], axis=1)
            i_rep = jnp.broadcast_to(g_col[:, h:h + 1], (L, LANES))
            b_rep = jnp.broadcast_to(b_col_all[:, HEADS + h:HEADS + h + 1], (L, LANES))
            i_row = g_row[h:h + 1, :]
            b_row = b_row_all[HEADS + h:HEADS + h + 1, :]
            b_last = b_rep[L - 1:L, :]
            m_prev = m_ref[h:h + 1, :]
            c_prev = c_ref[h]

            d_mat = jnp.where(causal, b_rep - b_row + i_row, NEG_INF)
            m_inter = b_rep + m_prev
            m_i = jnp.maximum(m_inter, jnp.max(d_mat, axis=1, keepdims=True))
            s = _dot_nt(q, k.astype(BF16)) * jnp.exp(d_mat - m_i)
            w_inter = jnp.exp(m_inter - m_i)
            intra = _dot(s.astype(BF16), v_aug)
            inter = _dot(q, c_prev.astype(BF16))
            num = intra[:, :HEAD_DIM] + w_inter * inter[:, :HEAD_DIM]
            den = intra[:, HEAD_DIM:] + w_inter * inter[:, HEAD_DIM:]
            hh = num / jnp.maximum(jnp.abs(den), jnp.exp(-m_i))

            w_loc = b_last - b_rep + i_rep
            m_loc = jnp.max(w_loc, axis=0, keepdims=True)
            kp = (k * jnp.exp(w_loc - m_loc)).astype(BF16)
            c_loc = _dot_tn(kp, v_aug)
            m_new = jnp.maximum(b_last + m_prev, m_loc)
            keep = jnp.exp(b_last + m_prev - m_new)
            add = jnp.exp(m_loc - m_new)
            c_ref[h] = (jnp.concatenate([keep, keep], axis=1) * c_prev
                        + jnp.concatenate([add, add], axis=1) * c_loc)
            m_ref[h:h + 1, :] = m_new

            hc = hh - jnp.mean(hh, axis=-1, keepdims=True)
            yn = hc * lax.rsqrt(jnp.mean(hc * hc, axis=-1, keepdims=True) + EPS)
            og = o_ref[pl.ds(r0, L), lo:lo + HEAD_DIM]
            y_ref[pl.ds(r0, L), lo:lo + HEAD_DIM] = (yn * nw_ref[:, lo:lo + HEAD_DIM] * _sigmoid(og)).astype(BF16)
        return carry

    lax.fori_loop(0, n_chunks, body, 0, unroll=MLSTM_UNROLL)


def _mlstm(proj, gates, gate_b, norm_w, batch, seq):
    t = batch * seq
    L = MLSTM_CHUNK
    assert L == LANES, "the kernel keeps per-row gate terms replicated over one vreg of lanes"
    ts = MLSTM_TILE
    nj = seq // ts
    nc = ts // L
    g_row = gates[:, :2 * HEADS].reshape(t // L, L, 2 * HEADS).transpose(0, 2, 1)
    b_col = jnp.zeros((1, LANES), F32).at[0, :2 * HEADS].set(gate_b)
    b_row = gate_b.reshape(2 * HEADS, 1)
    blk = lambda col: pl.BlockSpec((ts, MIX_HALF), lambda b, j, col=col: (b * nj + j, col))
    kern = functools.partial(_mlstm_kernel, chunk=L, n_chunks=nc)
    return pl.pallas_call(
        kern,
        grid=(batch, nj),
        in_specs=[blk(0), blk(1), blk(2), blk(3),
                  pl.BlockSpec((ts, LANES), lambda b, j: (b * nj + j, 0)),
                  pl.BlockSpec((nc, 2 * HEADS, L), lambda b, j: (b * nj + j, 0, 0)),
                  pl.BlockSpec((1, LANES), lambda b, j: (0, 0)),
                  pl.BlockSpec((2 * HEADS, 1), lambda b, j: (0, 0)),
                  pl.BlockSpec((1, MIX_HALF), lambda b, j: (0, 0))],
        out_specs=pl.BlockSpec((ts, MIX_HALF), lambda b, j: (b * nj + j, 0)),
        out_shape=jax.ShapeDtypeStruct((t, MIX_HALF), BF16),
        scratch_shapes=[pltpu.VMEM((HEADS, HEAD_DIM, 2 * HEAD_DIM), F32),
                        pltpu.VMEM((8, LANES), F32)],
        compiler_params=_params("arbitrary", "arbitrary"),
        name="mlstm",
    )(proj, proj, proj, proj, gates, g_row, b_col, b_row, norm_w.reshape(1, MIX_HALF))


def _rglru_kernel(xb_ref, gb_ref, cw_ref, cb_ref, wa_ref, ba_ref, wx_ref, bx_ref, lam_ref,
                  y_ref, xext_ref, h_ref, a_ref, u_ref, *, tile):
    @pl.when(pl.program_id(1) == 0)
    def _():
        xext_ref[0:8, :] = jnp.zeros((8, MIX_HALF), F32)
        h_ref[...] = jnp.zeros_like(h_ref)

    x = xb_ref[...]
    xext_ref[8:8 + tile, :] = x
    xc = cb_ref[...] + cw_ref[LRU_CONV - 1:LRU_CONV, :] * x
    for tap in range(LRU_CONV - 1):
        back = LRU_CONV - 1 - tap
        xc = xc + cw_ref[tap:tap + 1, :] * xext_ref[8 - back:8 - back + tile, :]
    xext_ref[0:8, :] = x[tile - 8:tile, :]

    xc16 = xc.astype(BF16)
    r_parts, i_parts = [], []
    for h in range(HEADS):
        lo = h * HEAD_DIM
        xh = xc16[:, lo:lo + HEAD_DIM]
        r_parts.append(_dot(xh, wa_ref[h]))
        i_parts.append(_dot(xh, wx_ref[h]))
    r = _sigmoid(jnp.concatenate(r_parts, axis=1) + ba_ref[...])
    ig = _sigmoid(jnp.concatenate(i_parts, axis=1) + bx_ref[...])
    lam = lam_ref[...]
    softplus_neg = jnp.maximum(-lam, 0.0) + jnp.log(1.0 + jnp.exp(-jnp.abs(lam)))
    log_a = -LRU_C * r * softplus_neg
    a = jnp.exp(log_a)
    th = jnp.tanh(log_a)
    u = jnp.sqrt(-2.0 * th / (1.0 - th)) * ig * xc

    a_ref[...] = a
    u_ref[...] = u
    rows = lax.broadcasted_iota(jnp.int32, (SUBLANES, MIX_HALF), 0)

    def group(i, h_prev):
        r0 = pl.multiple_of(i * SUBLANES, SUBLANES)
        ag = a_ref[pl.ds(r0, SUBLANES), :]
        ug = u_ref[pl.ds(r0, SUBLANES), :]
        for k in range(LRU_LOG_STEPS):
            keep = rows >= (1 << k)
            ug = ag * jnp.where(keep, pltpu.roll(ug, 1 << k, 0), 0.0) + ug
            ag = ag * jnp.where(keep, pltpu.roll(ag, 1 << k, 0), 1.0)
        hg = ug + ag * h_prev
        u_ref[pl.ds(r0, SUBLANES), :] = hg
        return hg[SUBLANES - 1:SUBLANES, :]

    h_last = lax.fori_loop(0, tile // SUBLANES, group, h_ref[0:1, :], unroll=LRU_UNROLL)
    h_ref[...] = jnp.broadcast_to(h_last, h_ref.shape)
    y_ref[...] = (u_ref[...] * _gelu_tanh(gb_ref[...])).astype(BF16)


def _rglru(proj, conv_w, conv_b, wa, ba, wx, bx, lam, batch, seq):
    t = batch * seq
    ts = LRU_TILE
    nj = seq // ts
    row = lambda a: a.reshape(1, MIX_HALF)
    const2 = lambda shape: pl.BlockSpec(shape, lambda b, j: (0, 0))
    const3 = lambda shape: pl.BlockSpec(shape, lambda b, j: (0, 0, 0))
    blk = lambda col: pl.BlockSpec((ts, MIX_HALF), lambda b, j, col=col: (b * nj + j, col))
    return pl.pallas_call(
        functools.partial(_rglru_kernel, tile=ts),
        grid=(batch, nj),
        in_specs=[blk(4), blk(5), const2((LRU_CONV, MIX_HALF)), const2((1, MIX_HALF)),
                  const3((HEADS, HEAD_DIM, HEAD_DIM)), const2((1, MIX_HALF)),
                  const3((HEADS, HEAD_DIM, HEAD_DIM)), const2((1, MIX_HALF)), const2((1, MIX_HALF))],
        out_specs=pl.BlockSpec((ts, MIX_HALF), lambda b, j: (b * nj + j, 0)),
        out_shape=jax.ShapeDtypeStruct((t, MIX_HALF), BF16),
        scratch_shapes=[pltpu.VMEM((ts + 8, MIX_HALF), F32), pltpu.VMEM((8, MIX_HALF), F32),
                        pltpu.VMEM((ts, MIX_HALF), F32), pltpu.VMEM((ts, MIX_HALF), F32)],
        compiler_params=_params("arbitrary", "arbitrary"),
        name="rglru",
    )(proj, proj, conv_w, row(conv_b), wa.astype(BF16), row(ba), wx.astype(BF16), row(bx), row(lam))


def _s5_kernel(u_ref, bre_ref, bim_ref, cre_ref, cim_ref, mre_ref, mim_ref, pre_ref, pim_ref, d_ref, gw_ref,
               gb_ref, y_ref, xr_ref, xi_ref, cr_ref, ci_ref, *, tile):
    @pl.when(pl.program_id(1) == 0)
    def _():
        cr_ref[...] = jnp.zeros_like(cr_ref)
        ci_ref[...] = jnp.zeros_like(ci_ref)

    u = u_ref[...]
    u16 = u.astype(BF16)
    blk_c = MIX_HALF // S5_BLOCKS
    blk_s = S5_LANES // S5_BLOCKS
    parts = []
    for j in range(S5_BLOCKS):
        lanes = slice(j * blk_s, (j + 1) * blk_s)
        uj = u16[:, j * blk_c:(j + 1) * blk_c]
        xr_ref[:, lanes] = _dot(uj, bre_ref[j])
        xi_ref[:, lanes] = _dot(uj, bim_ref[j])

        def group(i, carry, lanes=lanes):
            cr, ci = carry
            r0 = pl.multiple_of(i * SUBLANES, SUBLANES)
            xr = xr_ref[pl.ds(r0, SUBLANES), lanes]
            xi = xi_ref[pl.ds(r0, SUBLANES), lanes]
            for k in range(S5_LOG_STEPS):
                sr = pltpu.roll(xr, 1 << k, 0)
                si = pltpu.roll(xi, 1 << k, 0)
                mr = mre_ref[k, :, lanes]
                mi = mim_ref[k, :, lanes]
                xr, xi = xr + mr * sr - mi * si, xi + mr * si + mi * sr
            pr = pre_ref[:, lanes]
            pi = pim_ref[:, lanes]
            xr, xi = xr + pr * cr - pi * ci, xi + pr * ci + pi * cr
            xr_ref[pl.ds(r0, SUBLANES), lanes] = xr
            xi_ref[pl.ds(r0, SUBLANES), lanes] = xi
            return xr[SUBLANES - 1:SUBLANES, :], xi[SUBLANES - 1:SUBLANES, :]

        cr, ci = lax.fori_loop(0, tile // SUBLANES, group, (cr_ref[0:1, lanes], ci_ref[0:1, lanes]),
                               unroll=S5_UNROLL)
        cr_ref[0:1, lanes] = cr
        ci_ref[0:1, lanes] = ci
        parts.append(_dot(xr_ref[:, lanes].astype(BF16), cre_ref[j])
                     - _dot(xi_ref[:, lanes].astype(BF16), cim_ref[j]))
    y = jnp.concatenate(parts, axis=1) + d_ref[...] * u
    g = _gelu_tanh(y)
    y_ref[...] = (g * _sigmoid(_dot(g.astype(BF16), gw_ref[...]) + gb_ref[...])).astype(BF16)


def _s5_tables(lam_re, lam_im, b_re, b_im, c_re, c_im, log_dt):
    lr, li = lam_re.astype(F32), lam_im.astype(F32)
    dt = jnp.exp(log_dt.astype(F32))[:, None]
    mag = jnp.exp(lr * dt)
    abar_re = mag * jnp.cos(li * dt)
    abar_im = mag * jnp.sin(li * dt)
    den = lr * lr + li * li
    nr = abar_re - 1.0
    coef_re = (nr * lr + abar_im * li) / den
    coef_im = (abar_im * lr - nr * li) / den
    bbar_re = coef_re[..., None] * b_re - coef_im[..., None] * b_im
    bbar_im = coef_re[..., None] * b_im + coef_im[..., None] * b_re
    gpb = S5_GROUPS // S5_BLOCKS
    eye = jnp.eye(gpb, dtype=F32)

    def in_map(bb):
        bb = bb.reshape(S5_BLOCKS, gpb, S5_STATE, S5_GROUP)
        return jnp.einsum("jgph,gk->jghkp", bb, eye).reshape(S5_BLOCKS, gpb * S5_GROUP, gpb * S5_STATE)

    def out_map(cc):
        cc = cc.reshape(S5_BLOCKS, gpb, S5_GROUP, S5_STATE)
        return jnp.einsum("jghp,gk->jgpkh", cc, eye).reshape(S5_BLOCKS, gpb * S5_STATE, gpb * S5_GROUP)

    def power(n):
        n = jnp.asarray(n, F32)[..., None, None]
        pmag = jnp.exp(n * (lr * dt))
        shape = n.shape[:-2] + (S5_LANES,)
        return (pmag * jnp.cos(n * (li * dt))).reshape(shape), (pmag * jnp.sin(n * (li * dt))).reshape(shape)

    row = jnp.arange(SUBLANES)
    step = 2 ** jnp.arange(S5_LOG_STEPS)
    s_re, s_im = power(step)
    keep = (row[None, :] >= step[:, None])[..., None]
    m_re = jnp.where(keep, s_re[:, None, :], 0.0)
    m_im = jnp.where(keep, s_im[:, None, :], 0.0)
    p_re, p_im = power(row + 1)
    return (in_map(bbar_re).astype(BF16), in_map(bbar_im).astype(BF16),
            out_map(c_re.astype(F32)).astype(BF16), out_map(c_im.astype(F32)).astype(BF16),
            m_re, m_im, p_re, p_im)


def _s5(proj, tables, d_skip, glu_w, glu_b, batch, seq):
    t = batch * seq
    ts = S5_TILE
    nj = seq // ts
    bre, bim, cre, cim, m_re, m_im, p_re, p_im = tables
    blk_c = MIX_HALF // S5_BLOCKS
    blk_s = S5_LANES // S5_BLOCKS
    const2 = lambda shape: pl.BlockSpec(shape, lambda b, j: (0, 0))
    const3 = lambda shape: pl.BlockSpec(shape, lambda b, j: (0, 0, 0))
    return pl.pallas_call(
        functools.partial(_s5_kernel, tile=ts),
        grid=(batch, nj),
        in_specs=[pl.BlockSpec((ts, MIX_HALF), lambda b, j: (b * nj + j, 0)),
                  const3((S5_BLOCKS, blk_c, blk_s)), const3((S5_BLOCKS, blk_c, blk_s)),
                  const3((S5_BLOCKS, blk_s, blk_c)), const3((S5_BLOCKS, blk_s, blk_c)),
                  const3(m_re.shape), const3(m_im.shape), const2(p_re.shape), const2(p_im.shape),
                  const2((1, MIX_HALF)), const2((MIX_HALF, MIX_HALF)), const2((1, MIX_HALF))],
        out_specs=pl.BlockSpec((ts, MIX_HALF), lambda b, j: (b * nj + j, 0)),
        out_shape=jax.ShapeDtypeStruct((t, MIX_HALF), BF16),
        scratch_shapes=[pltpu.VMEM((ts, S5_LANES), F32), pltpu.VMEM((ts, S5_LANES), F32),
                        pltpu.VMEM((8, S5_LANES), F32), pltpu.VMEM((8, S5_LANES), F32)],
        compiler_params=_params("arbitrary", "arbitrary"),
        name="s5",
    )(proj, bre, bim, cre, cim, m_re, m_im, p_re, p_im, d_skip.reshape(1, MIX_HALF), glu_w.astype(BF16),
      glu_b.reshape(1, MIX_HALF))


def _gla_kernel(q_ref, k_ref, v_ref, r_ref, gl_ref, gw_ref, gb_ref, nw_ref, y_ref,
                st_ref, qd_ref, ki_ref, ke_ref, v16_ref, dec_ref, o_ref, *, tile, chunk):
    L = chunk
    nc = tile // L

    @pl.when(pl.program_id(1) == 0)
    def _():
        st_ref[...] = jnp.zeros_like(st_ref)

    z = _dot(gl_ref[...].astype(BF16), gw_ref[...]) + gb_ref[...]
    bcum = _log_sigmoid(z) * (1.0 / GLA_GATE_TEMP)
    row_in_chunk = lax.broadcasted_iota(jnp.int32, bcum.shape, 0) & (L - 1)
    s = 1
    while s < L:
        bcum = bcum + jnp.where(row_in_chunk >= s, pltpu.roll(bcum, s, 0), 0.0)
        s *= 2
    b3 = bcum.reshape(nc, L, MIX_HALF)
    b_last = b3[:, L - 1:L, :]
    k = k_ref[...]
    qd_ref[...] = (q_ref[...] * (GLA_DK ** -0.5) * jnp.exp(bcum)).astype(BF16)
    ki_ref[...] = (k * jnp.exp(-bcum)).astype(BF16)
    ke_ref[...] = (k.reshape(nc, L, MIX_HALF) * jnp.exp(b_last - b3)).reshape(tile, MIX_HALF).astype(BF16)
    v16_ref[...] = v_ref[...].astype(BF16)
    dec_ref[...] = jnp.exp(b_last)

    ri = lax.broadcasted_iota(jnp.int32, (L, L), 0)
    ci = lax.broadcasted_iota(jnp.int32, (L, L), 1)
    causal = ci <= ri

    def body(c, carry):
        r0 = pl.multiple_of(c * L, L)
        dec = dec_ref[c]
        for h in range(HEADS):
            lo = h * HEAD_DIM
            q_dec = qd_ref[pl.ds(r0, L), lo:lo + HEAD_DIM]
            v = v16_ref[pl.ds(r0, L), lo:lo + HEAD_DIM]
            st = st_ref[h]
            att = jnp.where(causal, _dot_nt(q_dec, ki_ref[pl.ds(r0, L), lo:lo + HEAD_DIM]), 0.0)
            o_ref[pl.ds(r0, L), lo:lo + HEAD_DIM] = (_dot(att.astype(BF16), v)
                                                     + _dot_nt(q_dec, st.astype(BF16)))
            st_ref[h] = dec[:, lo:lo + HEAD_DIM] * st + _dot_tn(v, ke_ref[pl.ds(r0, L), lo:lo + HEAD_DIM])
        return carry

    lax.fori_loop(0, nc, body, 0, unroll=GLA_UNROLL)

    rg = r_ref[...]
    gate = nw_ref[...] * (rg * _sigmoid(rg))
    for h in range(HEADS):
        lo = h * HEAD_DIM
        o = o_ref[:, lo:lo + HEAD_DIM]
        yn = o * lax.rsqrt(jnp.mean(o * o, axis=-1, keepdims=True) + EPS)
        y_ref[:, lo:lo + HEAD_DIM] = (yn * gate[:, lo:lo + HEAD_DIM]).astype(BF16)


def _gla(proj, glow, gate_w, gate_b, norm_w, batch, seq):
    t = batch * seq
    ts = GLA_TILE
    nj = seq // ts
    blk = lambda col: pl.BlockSpec((ts, MIX_HALF), lambda b, j, col=col: (b * nj + j, col))
    const2 = lambda shape: pl.BlockSpec(shape, lambda b, j: (0, 0))
    return pl.pallas_call(
        functools.partial(_gla_kernel, tile=ts, chunk=GLA_CHUNK),
        grid=(batch, nj),
        in_specs=[blk(1), blk(2), blk(3), blk(4),
                  pl.BlockSpec((ts, LANES), lambda b, j: (b * nj + j, 0)),
                  const2((LANES, MIX_HALF)), const2((1, MIX_HALF)), const2((1, MIX_HALF))],
        out_specs=pl.BlockSpec((ts, MIX_HALF), lambda b, j: (b * nj + j, 0)),
        out_shape=jax.ShapeDtypeStruct((t, MIX_HALF), BF16),
        scratch_shapes=[pltpu.VMEM((HEADS, HEAD_DIM, HEAD_DIM), F32),
                        pltpu.VMEM((ts, MIX_HALF), BF16), pltpu.VMEM((ts, MIX_HALF), BF16),
                        pltpu.VMEM((ts, MIX_HALF), BF16), pltpu.VMEM((ts, MIX_HALF), BF16),
                        pltpu.VMEM((ts // GLA_CHUNK, 1, MIX_HALF), F32),
                        pltpu.VMEM((ts, MIX_HALF), F32)],
        compiler_params=_params("arbitrary", "arbitrary"),
        name="gla",
    )(proj, proj, proj, proj, glow, gate_w, gate_b, norm_w.reshape(1, MIX_HALF))


def _pad_heads(w, axis):
    shape = list(w.shape)
    shape[axis:axis + 1] = [HEADS, GLA_DK]
    w = w.reshape(shape)
    pad = [(0, 0)] * w.ndim
    pad[axis + 1] = (0, HEAD_DIM - GLA_DK)
    w = jnp.pad(w, pad)
    shape[axis:axis + 2] = [HEADS * HEAD_DIM]
    return w.reshape(shape)


def _pack_bf16_pairs(z):
    hi = lax.bitcast_convert_type(z[:, :PACKED].astype(BF16).astype(F32), jnp.uint32)
    lo = lax.bitcast_convert_type(z[:, PACKED:].astype(BF16).astype(F32), jnp.uint32)
    word = (hi & jnp.uint32(0xFFFF0000)) | lax.shift_right_logical(lo, jnp.uint32(16))
    return lax.bitcast_convert_type(word, jnp.int32)


def _unpack_bf16_pairs(p):
    word = lax.bitcast_convert_type(p, jnp.uint32)
    hi = lax.bitcast_convert_type(word & jnp.uint32(0xFFFF0000), F32)
    lo = lax.bitcast_convert_type(lax.shift_left(word, jnp.uint32(16)), F32)
    return hi, lo


def _out_kernel(ya_ref, yb_ref, h_ref, w_ref, lw_ref, lb_ref, rw_ref, rb_ref,
                o_ref, opk_ref, idx_ref, rank_ref, wk_ref, cnt_ref, base_ref, *, tile):
    mixed = jnp.concatenate([ya_ref[...], yb_ref[...]], axis=1)
    z = ALPHA * h_ref[...] + _dot(mixed, w_ref[...])
    out = _layer_norm(z, lw_ref[...], lb_ref[...])
    o_ref[...] = out
    opk_ref[...] = _pack_bf16_pairs(out)
    _route_tile(out, rw_ref, rb_ref, idx_ref, rank_ref, wk_ref, cnt_ref, base_ref, tile)


def _out_proj_ln_route(ya, yb, h, w_out, ln_w, ln_b, router_w, router_bias):
    t = h.shape[0]
    tm = OUT_TILE
    const = lambda shape: pl.BlockSpec(shape, lambda i: (0, 0))
    per_tok = lambda dt: jax.ShapeDtypeStruct((TOP_K, t), dt)
    tok_blk = pl.BlockSpec((TOP_K, tm), lambda i: (0, i))
    return pl.pallas_call(
        functools.partial(_out_kernel, tile=tm),
        grid=(t // tm,),
        in_specs=[pl.BlockSpec((tm, MIX_HALF), lambda i: (i, 0)),
                  pl.BlockSpec((tm, MIX_HALF), lambda i: (i, 0)),
                  pl.BlockSpec((tm, D_MODEL), lambda i: (i, 0)),
                  const((D_MODEL, D_MODEL)), const((1, D_MODEL)), const((1, D_MODEL)),
                  const((N_EXPERTS, D_MODEL)), const((N_EXPERTS, 1))],
        out_specs=[pl.BlockSpec((tm, D_MODEL), lambda i: (i, 0)),
                   pl.BlockSpec((tm, PACKED), lambda i: (i, 0)),
                   tok_blk, tok_blk, pl.BlockSpec((tm, TOP_K), lambda i: (i, 0)), const((N_EXPERTS, LANES))],
        out_shape=[jax.ShapeDtypeStruct((t, D_MODEL), F32), jax.ShapeDtypeStruct((t, PACKED), jnp.int32),
                   per_tok(jnp.int32), per_tok(jnp.int32), jax.ShapeDtypeStruct((t, TOP_K), F32),
                   jax.ShapeDtypeStruct((N_EXPERTS, LANES), F32)],
        scratch_shapes=[pltpu.VMEM((N_EXPERTS, LANES), F32)],
        compiler_params=_params("arbitrary"),
        name="out_proj_ln_route",
    )(ya, yb, h, w_out.astype(BF16), ln_w.reshape(1, D_MODEL), ln_b.reshape(1, D_MODEL),
      router_w.T, router_bias.reshape(N_EXPERTS, 1))


def _first_index(hit, idx, big):
    return jnp.min(jnp.where(hit, idx, big), axis=0, keepdims=True)


def _route_tile(h, w_ref, b_ref, idx_ref, rank_ref, wk_ref, cnt_ref, base_ref, tile):
    @pl.when(pl.program_id(0) == 0)
    def _():
        base_ref[...] = jnp.zeros_like(base_ref)

    h_hi, h_mid, _ = _split3(h)
    w_hi, w_mid, _ = _split3(w_ref[...])
    logits = _dot_nt(w_hi, h_hi) + _dot_nt(w_hi, h_mid) + _dot_nt(w_mid, h_hi)
    scores = _sigmoid(logits)
    biased = scores + b_ref[...]

    sub = lax.broadcasted_iota(jnp.int32, (GROUP_SIZE, tile), 0)
    grp_rows = []
    for g in range(N_GROUPS):
        xg = biased[g * GROUP_SIZE:(g + 1) * GROUP_SIZE, :]
        m1 = jnp.max(xg, axis=0, keepdims=True)
        i1 = _first_index(xg == m1, sub, GROUP_SIZE)
        m2 = jnp.max(jnp.where(sub == i1, NEG_INF, xg), axis=0, keepdims=True)
        grp_rows.append(m1 + m2)
    gs = jnp.concatenate(grp_rows, axis=0)
    gsel = jnp.zeros((N_GROUPS, tile), F32)
    for _ in range(TOPK_GROUPS):
        mx = jnp.max(gs, axis=0, keepdims=True)
        hit = sub == _first_index(gs == mx, sub, N_GROUPS)
        gsel = jnp.where(hit, 1.0, gsel)
        gs = jnp.where(hit, NEG_INF, gs)
    emask = jnp.concatenate(
        [jnp.broadcast_to(gsel[g:g + 1, :], (GROUP_SIZE, tile)) for g in range(N_GROUPS)], axis=0)

    eidx = lax.broadcasted_iota(jnp.int32, (N_EXPERTS, tile), 0)
    cand = jnp.where(emask > 0.5, biased, NEG_INF)
    sel = jnp.zeros((N_EXPERTS, tile), F32)
    picks = []
    for _ in range(TOP_K):
        mx = jnp.max(cand, axis=0, keepdims=True)
        first = _first_index(cand == mx, eidx, N_EXPERTS)
        hit = eidx == first
        picks.append(first)
        sel = jnp.where(hit, 1.0, sel)
        cand = jnp.where(hit, NEG_INF, cand)
    picked = jnp.where(sel > 0.5, scores, 0.0)
    wts = picked / jnp.sum(picked, axis=0, keepdims=True) * ROUTED_SCALE

    ri = lax.broadcasted_iota(jnp.int32, (tile, tile), 0)
    ci = lax.broadcasted_iota(jnp.int32, (tile, tile), 1)
    before = (ri < ci).astype(BF16)
    prior = _dot(sel.astype(BF16), before) + base_ref[:, 0:1]
    ranks = [jnp.sum(jnp.where(eidx == p, prior, 0.0), axis=0, keepdims=True) for p in picks]
    wsel = [jnp.sum(jnp.where(eidx == p, wts, 0.0), axis=0, keepdims=True) for p in picks]
    idx_ref[...] = jnp.concatenate(picks, axis=0)
    rank_ref[...] = jnp.concatenate(ranks, axis=0).astype(jnp.int32)
    wk_ref[...] = jnp.concatenate(wsel, axis=0).T
    total = base_ref[...] + jnp.sum(sel, axis=1, keepdims=True)
    base_ref[...] = total
    cnt_ref[...] = total


def _silu(x):
    return x * _sigmoid(x)


def _sc_mesh():
    return plsc.VectorSubcoreMesh(core_axis_name="c", subcore_axis_name="s")


def _sc_worker_id():
    return lax.axis_index("s") * SC_CORES + lax.axis_index("c")


def _dispatch_rows(xpk, pos_chunks, n_rows):
    t = xpk.shape[0]
    n_ch = t // SC_WORKERS // SC_CHUNK

    @functools.partial(
        pl.kernel, mesh=_sc_mesh(),
        out_type=jax.ShapeDtypeStruct((n_rows, PACKED), jnp.int32),
        scratch_types=[pltpu.VMEM((TOP_K, SC_CHUNK), jnp.int32),
                       pltpu.VMEM((SC_CHUNK, PACKED), jnp.int32),
                       pltpu.SemaphoreType.DMA],
        name="moe_dispatch",
    )
    def scatter(x_hbm, pos_hbm, out_hbm, idx_v, rows_v, sem):
        wid = _sc_worker_id()

        @pl.loop(0, n_ch)
        def _(c):
            chunk = wid * n_ch + c
            off = pl.multiple_of(chunk * SC_CHUNK, SC_CHUNK)
            pltpu.sync_copy(pos_hbm.at[chunk], idx_v)
            pltpu.sync_copy(x_hbm.at[pl.ds(off, SC_CHUNK)], rows_v)
            copies = [pltpu.async_copy(rows_v, out_hbm.at[idx_v.at[k]], sem) for k in range(TOP_K)]
            for cp in copies:
                cp.wait()

    return scatter(xpk, pos_chunks)


def _gather_rows(table, idx):
    n = idx.shape[0]
    per_w = n // SC_WORKERS
    n_ch = per_w // SC_CHUNK
    assert n_ch % 2 == 0 and n_ch >= 2

    @functools.partial(
        pl.kernel, mesh=_sc_mesh(),
        out_type=jax.ShapeDtypeStruct((n, PACKED), jnp.int32),
        scratch_types=[pltpu.VMEM((n_ch, SC_CHUNK), jnp.int32),
                       pltpu.VMEM((SC_CHUNK, PACKED), jnp.int32), pltpu.VMEM((SC_CHUNK, PACKED), jnp.int32),
                       pltpu.SemaphoreType.DMA, pltpu.SemaphoreType.DMA,
                       pltpu.SemaphoreType.DMA, pltpu.SemaphoreType.DMA],
        name="moe_gather",
    )
    def gather(table_hbm, idx_hbm, out_hbm, idx_v, rows0, rows1, g0, g1, w0, w1):
        wid = _sc_worker_id()
        base = wid * per_w
        rows, g_sem, w_sem = (rows0, rows1), (g0, g1), (w0, w1)
        pltpu.sync_copy(idx_hbm.at[wid], idx_v)

        def fetch(c, b):
            return pltpu.make_async_copy(table_hbm.at[idx_v.at[c]], rows[b], g_sem[b])

        def flush(c, b):
            off = pl.multiple_of(base + c * SC_CHUNK, SC_CHUNK)
            return pltpu.make_async_copy(rows[b], out_hbm.at[pl.ds(off, SC_CHUNK)], w_sem[b])

        fetch(0, 0).start()

        @pl.loop(0, n_ch, step=2)
        def _(c0):
            for b in range(2):
                c = c0 + b
                fetch(c, b).wait()
                flush(c, b).start()

                @pl.when(c + 1 < n_ch)
                def _():
                    @pl.when(c >= 1)
                    def _():
                        flush(c - 1, 1 - b).wait()
                    fetch(c + 1, 1 - b).start()

        flush(n_ch - 2, 0).wait()
        flush(n_ch - 1, 1).wait()

    return gather(table, idx.reshape(SC_WORKERS, n_ch, SC_CHUNK))


def _unpacked_bf16(p):
    hi, lo = _unpack_bf16_pairs(p)
    return jnp.concatenate([hi.astype(BF16), lo.astype(BF16)], axis=1)


def _expert_kernel(be_ref, nu_ref, next_ref, slot_ref, xs_hbm, wg_hbm, wu_hbm, wd_hbm, y_ref,
                   g16_ref, u16_ref, d16_ref, gf_ref, uf_ref, df_ref, xbuf_ref, xsem, wsem, *, layer):
    i = pl.program_id(0)
    n_used = nu_ref[0]
    bm = xbuf_ref.shape[1]

    def fetch(b):
        slot = lax.rem(b, XS_SLOTS)
        rows = pl.ds(pl.multiple_of(b * bm, bm), bm)
        return pltpu.make_async_copy(xs_hbm.at[rows], xbuf_ref.at[slot], xsem.at[slot])

    @pl.when(i == 0)
    def _():
        fetch(0).start()

        @pl.when(n_used > 1)
        def _():
            fetch(1).start()

    @pl.when(i + 2 < n_used)
    def _():
        fetch(i + 2).start()

    e = be_ref[i]
    slot = slot_ref[e]
    f32_bufs = (gf_ref, uf_ref, df_ref)

    def wfetch(expert, dst_slot):
        return [pltpu.make_async_copy(w_hbm.at[layer, expert], buf.at[dst_slot], wsem.at[dst_slot, j])
                for j, (w_hbm, buf) in enumerate(zip((wg_hbm, wu_hbm, wd_hbm), f32_bufs))]

    @pl.when(i == 0)
    def _():
        for cp in wfetch(e, slot):
            cp.start()

    first_block_of_expert = jnp.logical_or(i == 0, e != be_ref[jnp.maximum(i - 1, 0)])

    @pl.when(jnp.logical_and(first_block_of_expert, i < n_used))
    def _():
        for cp in wfetch(e, slot):
            cp.wait()
        g16_ref[slot] = gf_ref[slot].astype(BF16)
        u16_ref[slot] = uf_ref[slot].astype(BF16)
        d16_ref[slot] = df_ref[slot].astype(BF16)
        nxt = next_ref[e]

        @pl.when(nxt >= 0)
        def _():
            for cp in wfetch(nxt, 1 - slot):
                cp.start()

    @pl.when(i < n_used)
    def _():
        fetch(i).wait()
        x = _unpacked_bf16(xbuf_ref[lax.rem(i, XS_SLOTS)])
        hh = _silu(_dot(x, g16_ref[slot])) * _dot(x, u16_ref[slot])
        y_ref[...] = _pack_bf16_pairs(_dot(hh.astype(BF16), d16_ref[slot]))


def _experts(block_e, n_used, next_expert, weight_slot, xs, wg, wu, wd, layer):
    nb = block_e.shape[0]
    bm = MOE_BLOCK
    hbm = pl.BlockSpec(memory_space=pl.ANY)
    two = lambda shape, dt: pltpu.VMEM((2,) + shape, dt)
    grid_spec = pltpu.PrefetchScalarGridSpec(
        num_scalar_prefetch=4,
        grid=(nb,),
        in_specs=[hbm, hbm, hbm, hbm],
        out_specs=pl.BlockSpec((bm, PACKED), lambda i, be, nu, nx, sl: (jnp.minimum(i, nu[0] - 1), 0)),
        scratch_shapes=[two((D_MODEL, D_EXPERT), BF16), two((D_MODEL, D_EXPERT), BF16), two((D_EXPERT, D_MODEL), BF16),
                        two((D_MODEL, D_EXPERT), F32), two((D_MODEL, D_EXPERT), F32), two((D_EXPERT, D_MODEL), F32),
                        pltpu.VMEM((XS_SLOTS, bm, PACKED), jnp.int32), pltpu.SemaphoreType.DMA((XS_SLOTS,)),
                        pltpu.SemaphoreType.DMA((2, 3))],
    )
    return pl.pallas_call(
        functools.partial(_expert_kernel, layer=layer),
        grid_spec=grid_spec,
        out_shape=jax.ShapeDtypeStruct((nb * bm, PACKED), jnp.int32),
        compiler_params=_params("arbitrary"),
        name="moe_experts",
    )(block_e, n_used, next_expert, weight_slot, xs, wg, wu, wd)


def _combine_kernel(g_ref, wk_ref, h_ref, xpk_ref, sg_ref, su_ref, sd_ref, lw_ref, lb_ref, *rest, with_proj):
    if with_proj:
        (wm_ref, ws_ref), (o_ref, proj_ref, small_ref) = rest[:2], rest[-3:]
        next_proj = (wm_ref, ws_ref, proj_ref, small_ref)
    else:
        o_ref, next_proj = rest[-1], None
    x = _unpacked_bf16(xpk_ref[...])
    hs = _silu(_dot(x, sg_ref[...])) * _dot(x, su_ref[...])
    shared = _dot(hs.astype(BF16), sd_ref[...])
    acc_hi = shared[:, :PACKED]
    acc_lo = shared[:, PACKED:]
    wk = wk_ref[...]
    for k in range(TOP_K):
        y_hi, y_lo = _unpack_bf16_pairs(g_ref[k])
        w = wk[:, k:k + 1]
        acc_hi = acc_hi + w * y_hi
        acc_lo = acc_lo + w * y_lo
    ffn = jnp.concatenate([acc_hi, acc_lo], axis=1)
    out = _layer_norm(ALPHA * h_ref[...] + ffn, lw_ref[...], lb_ref[...])
    o_ref[...] = out
    if next_proj is not None:
        wm_ref, ws_ref, proj_ref, small_ref = next_proj
        out16 = out.astype(BF16)
        proj_ref[...] = _dot(out16, wm_ref[...])
        small_ref[...] = _dot(out16, ws_ref[...])


def _combine_ln(g, wk, h, xpk, sg, su, sd, ln_w, ln_b, next_w, part, prev):
    t = h.shape[0]
    tm = COMBINE_TILE if next_w is not None else 2 * COMBINE_TILE
    n_blk = g.shape[1] // tm
    first = part * n_blk
    const = lambda shape: pl.BlockSpec(shape, lambda i: (0, 0))
    rows = lambda width: pl.BlockSpec((tm, width), lambda i: (i + first, 0))
    in_specs = [pl.BlockSpec((TOP_K, tm, PACKED), lambda i: (0, i, 0)), rows(TOP_K), rows(D_MODEL), rows(PACKED),
                const((D_MODEL, D_EXPERT)), const((D_MODEL, D_EXPERT)), const((D_EXPERT, D_MODEL)),
                const((1, D_MODEL)), const((1, D_MODEL))]
    args = [g, wk, h, xpk, sg.astype(BF16), su.astype(BF16), sd.astype(BF16),
            ln_w.reshape(1, D_MODEL), ln_b.reshape(1, D_MODEL)]
    out_specs = [rows(D_MODEL)]
    out_shape = [jax.ShapeDtypeStruct((t, D_MODEL), F32)]
    if next_w is not None:
        w_main, w_small = next_w
        n = w_main.shape[1]
        in_specs += [const((D_MODEL, n)), const((D_MODEL, LANES))]
        args += [w_main, w_small]
        out_specs += [rows(n), rows(LANES)]
        out_shape += [jax.ShapeDtypeStruct((t, n), F32), jax.ShapeDtypeStruct((t, LANES), F32)]
    aliases = {}
    if prev is not None:
        aliases = {len(args) + k: k for k in range(len(prev))}
        in_specs += [pl.BlockSpec(memory_space=pl.ANY)] * len(prev)
        args += list(prev)
    return pl.pallas_call(
        functools.partial(_combine_kernel, with_proj=next_w is not None),
        grid=(n_blk,),
        in_specs=in_specs,
        out_specs=out_specs,
        out_shape=out_shape,
        input_output_aliases=aliases,
        compiler_params=_params("parallel"),
        name="moe_combine_ln",
    )(*args)


def _moe_ln(h, hpk, idx, rank, wk, counts, wg, wu, wd, layer, sg, su, sd, ln_w, ln_b, next_w):
    t = h.shape[0]
    cnt = counts[:, 0].astype(jnp.int32)
    padded = (cnt + MOE_BLOCK - 1) // MOE_BLOCK * MOE_BLOCK
    pend = jnp.cumsum(padded)
    experts = jnp.arange(N_EXPERTS, dtype=jnp.int32)
    pstart_of_pick = jnp.sum(jnp.where(idx[:, :, None] == experts, pend - padded, 0), axis=-1)
    pos = pstart_of_pick + rank
    nb = -(-(t * TOP_K + N_EXPERTS * (MOE_BLOCK - 1)) // MOE_BLOCK)
    starts = jnp.arange(nb, dtype=jnp.int32) * MOE_BLOCK
    block_e = jnp.minimum(jnp.sum((pend[None, :] <= starts[:, None]).astype(jnp.int32), axis=1), N_EXPERTS - 1)
    n_used = (pend[-1] // MOE_BLOCK).astype(jnp.int32).reshape(1)
    has_rows = cnt > 0
    later = jnp.logical_and(has_rows[None, :], experts[None, :] > experts[:, None])
    next_expert = jnp.min(jnp.where(later, experts[None, :], N_EXPERTS), axis=1)
    next_expert = jnp.where(next_expert == N_EXPERTS, -1, next_expert).astype(jnp.int32)
    weight_slot = ((jnp.cumsum(has_rows) - has_rows) % 2).astype(jnp.int32)
    pos_chunks = pos.reshape(TOP_K, t // SC_CHUNK, SC_CHUNK).transpose(1, 0, 2)
    xs = _dispatch_rows(hpk, pos_chunks, nb * MOE_BLOCK)
    ys = _experts(block_e, n_used, next_expert, weight_slot, xs, wg, wu, wd, layer)
    part = t // COMBINE_PARTS
    gathered = [_gather_rows(ys, pos[:, p * part:(p + 1) * part].reshape(-1)).reshape(TOP_K, part, PACKED)
                for p in range(COMBINE_PARTS)]
    outs = None
    for p in range(COMBINE_PARTS):
        outs = _combine_ln(gathered[p], wk, h, hpk, sg, su, sd, ln_w, ln_b, next_w, p, outs)
    return outs


def _pad_cols(w, width=LANES):
    return jnp.pad(w, ((0, 0), (0, width - w.shape[1])))


def _even_proj_weights(w_in):
    a4 = 4 * MIX_HALF
    ng = 2 * HEADS
    w_main = jnp.concatenate([w_in[:, :a4], w_in[:, a4 + ng:]], axis=1).astype(BF16)
    w_gate = _pad_cols(w_in[:, a4:a4 + ng]).astype(BF16)
    return w_main, w_gate


def _even_mixer(proj, gates, batch, seq, gate_b, norm_w, conv_w, conv_b, wa, ba, wx, bx, lam):
    ya = _mlstm(proj, gates, gate_b, norm_w, batch, seq)
    yb = _rglru(proj, conv_w, conv_b, wa, ba, wx, bx, lam, batch, seq)
    return ya, yb


def _odd_proj_weights(w_in):
    c0 = MIX_HALF
    c1 = c0 + HEADS * GLA_DK
    c2 = c1 + HEADS * GLA_DK
    c3 = c2 + MIX_HALF
    c4 = c3 + MIX_HALF
    w_main = jnp.concatenate([w_in[:, :c0], _pad_heads(w_in[:, c0:c1], 1), _pad_heads(w_in[:, c1:c2], 1),
                              w_in[:, c2:c4]], axis=1).astype(BF16)
    w_low = _pad_cols(w_in[:, c4:]).astype(BF16)
    return w_main, w_low


def _odd_mixer(proj, glow, batch, seq, lam_re, lam_im, b_re, b_im, c_re, c_im, d_skip, log_dt,
               glu_w, glu_b, gate_w, gate_b, norm_w):
    tables = _s5_tables(lam_re, lam_im, b_re, b_im, c_re, c_im, log_dt)
    yc = _s5(proj, tables, d_skip, glu_w, glu_b, batch, seq)
    gw = jnp.pad(_pad_heads(gate_w, 1), ((0, LANES - GLA_GATE_RANK), (0, 0))).astype(BF16)
    gb = _pad_heads(gate_b.reshape(1, -1), 1)
    yd = _gla(proj, glow, gw, gb, norm_w, batch, seq)
    return yc, yd


def kernel(x, ln1_w, ln1_b, ln2_w, ln2_b, w_out, w_in_even, mlstm_gate_b, mlstm_norm_w, lru_conv_w, lru_conv_b, lru_wa, lru_ba, lru_wx, lru_bx, lru_lambda, w_in_odd, s5_lam_re, s5_lam_im, s5_b_re, s5_b_im, s5_c_re, s5_c_im, s5_d, s5_log_dt, s5_glu_w, s5_glu_b, gla_gate_w, gla_gate_b, gla_norm_w, router_w, router_bias, exp_w_gate, exp_w_up, exp_w_down, sh_w_gate, sh_w_up, sh_w_down):
    batch, seq, d = x.shape
    proj_w = [_even_proj_weights(w_in_even[layer // 2]) if layer % 2 == 0 else _odd_proj_weights(w_in_odd[layer // 2])
              for layer in range(DEPTH)]
    h = x.reshape(batch * seq, d)
    proj, small = _proj(h, *proj_w[0])
    for layer in range(DEPTH):
        j = layer // 2
        if layer % 2 == 0:
            y1, y2 = _even_mixer(proj, small, batch, seq, mlstm_gate_b[j], mlstm_norm_w[j],
                                 lru_conv_w[j], lru_conv_b[j], lru_wa[j], lru_ba[j], lru_wx[j],
                                 lru_bx[j], lru_lambda[j])
        else:
            y1, y2 = _odd_mixer(proj, small, batch, seq, s5_lam_re[j], s5_lam_im[j], s5_b_re[j],
                                s5_b_im[j], s5_c_re[j], s5_c_im[j], s5_d[j], s5_log_dt[j],
                                s5_glu_w[j], s5_glu_b[j], gla_gate_w[j], gla_gate_b[j], gla_norm_w[j])
        h, hpk, idx, rank, wk, counts = _out_proj_ln_route(y1, y2, h, w_out[layer], ln1_w[layer], ln1_b[layer],
                                                           router_w[layer], router_bias[layer])
        next_w = proj_w[layer + 1] if layer + 1 < DEPTH else None
        res = _moe_ln(h, hpk, idx, rank, wk, counts, exp_w_gate, exp_w_up, exp_w_down, layer,
                      sh_w_gate[layer], sh_w_up[layer], sh_w_down[layer], ln2_w[layer], ln2_b[layer], next_w)
        if next_w is None:
            (h,) = res
        else:
            h, proj, small = res
    return h.reshape(batch, seq, d)
```

```python
import functools
import math

import jax
import jax.numpy as jnp
from jax import lax
from jax.experimental import pallas as pl
from jax.experimental.pallas import tpu as pltpu
from jax.experimental.pallas import tpu_sc as plsc

F32 = jnp.float32
BF16 = jnp.bfloat16

D_MODEL = 1024
DEPTH = 2
MIX_HALF = 512
HEADS = 4
HEAD_DIM = 128
GLA_DK = 64
GLA_CHUNK = 64
GLA_GATE_RANK = 16
GLA_GATE_TEMP = 16.0
LRU_C = 8.0
LRU_CONV = 4
S5_GROUP = 16
S5_GROUPS = 32
S5_STATE = 64
S5_LANES = S5_GROUPS * S5_STATE
S5_BLOCKS = 4
N_EXPERTS = 64
N_GROUPS = 8
GROUP_SIZE = N_EXPERTS // N_GROUPS
TOP_K = 8
TOPK_GROUPS = 4
D_EXPERT = 256
ROUTED_SCALE = 2.5
ALPHA = (2.0 * DEPTH) ** 0.25
EPS = 1e-5
LANES = 128
SUBLANES = 8
NEG_INF = float("-inf")

VMEM_LIMIT = 56 * 1024 * 1024

MLSTM_CHUNK = 128
MLSTM_TILE = 1024
MLSTM_UNROLL = 2
LRU_TILE = 1024
LRU_LOG_STEPS = 3
LRU_UNROLL = 4
S5_TILE = 512
S5_LOG_STEPS = 3
S5_UNROLL = True
GLA_UNROLL = 8
GLA_TILE = 1024
PROJ_TILE = 1024
OUT_TILE = 1024
MOE_BLOCK = 1152
XS_SLOTS = 3
COMBINE_TILE = 256
COMBINE_PARTS = 2
PACKED = D_MODEL // 2
SC_CHUNK = 64
SC_CORES = 2
SC_SUBCORES = 16
SC_WORKERS = SC_CORES * SC_SUBCORES


def _params(*sem):
    return pltpu.CompilerParams(dimension_semantics=sem, vmem_limit_bytes=VMEM_LIMIT)


def _split3(x):
    hi = x.astype(BF16)
    r1 = x - hi.astype(F32)
    mid = r1.astype(BF16)
    lo = (r1 - mid.astype(F32)).astype(BF16)
    return hi, mid, lo


def _dot(a, b):
    return jnp.dot(a, b, preferred_element_type=F32)


def _dot_nt(a, b):
    return lax.dot_general(a, b, (((1,), (1,)), ((), ())), preferred_element_type=F32)


def _dot_tn(a, b):
    return lax.dot_general(a, b, (((0,), (0,)), ((), ())), preferred_element_type=F32)


def _exact_left01(mask01_bf16, x):
    hi, mid, lo = _split3(x)
    return _dot(mask01_bf16, hi) + _dot(mask01_bf16, mid) + _dot(mask01_bf16, lo)


def _exact_right01(x, mask01_bf16):
    hi, mid, lo = _split3(x)
    return _dot(hi, mask01_bf16) + _dot(mid, mask01_bf16) + _dot(lo, mask01_bf16)


def _log_sigmoid(x):
    return jnp.minimum(x, 0.0) - jnp.log(1.0 + jnp.exp(-jnp.abs(x)))


def _sigmoid(x):
    return 1.0 / (1.0 + jnp.exp(-x))


def _gelu_tanh(x):
    c = math.sqrt(2.0 / math.pi)
    return 0.5 * x * (1.0 + jnp.tanh(c * (x + 0.044715 * (x * x * x))))


def _layer_norm(z, w, b):
    mu = jnp.mean(z, axis=-1, keepdims=True)
    zc = z - mu
    return zc * lax.rsqrt(jnp.mean(zc * zc, axis=-1, keepdims=True) + EPS) * w + b


def _proj_kernel(x_ref, w_ref, wg_ref, o_ref, og_ref):
    x = x_ref[...].astype(BF16)
    o_ref[...] = _dot(x, w_ref[...])
    og_ref[...] = _dot(x, wg_ref[...])


def _proj(x, w_main, w_small):
    t, d = x.shape
    n = w_main.shape[1]
    tm = PROJ_TILE
    return pl.pallas_call(
        _proj_kernel,
        grid=(t // tm,),
        in_specs=[pl.BlockSpec((tm, d), lambda i: (i, 0)),
                  pl.BlockSpec((d, n), lambda i: (0, 0)),
                  pl.BlockSpec((d, LANES), lambda i: (0, 0))],
        out_specs=[pl.BlockSpec((tm, n), lambda i: (i, 0)),
                   pl.BlockSpec((tm, LANES), lambda i: (i, 0))],
        out_shape=[jax.ShapeDtypeStruct((t, n), F32), jax.ShapeDtypeStruct((t, LANES), F32)],
        compiler_params=_params("parallel"),
        name="in_proj",
    )(x, w_main, w_small)


def _mlstm_kernel(q_ref, k_ref, v_ref, o_ref, gc_ref, gr_ref, bc_ref, br_ref, nw_ref,
                  y_ref, c_ref, m_ref, *, chunk, n_chunks):
    L = chunk

    @pl.when(pl.program_id(1) == 0)
    def _():
        c_ref[...] = jnp.zeros_like(c_ref)
        m_ref[...] = jnp.zeros_like(m_ref)

    ri = lax.broadcasted_iota(jnp.int32, (L, L), 0)
    ci = lax.broadcasted_iota(jnp.int32, (L, L), 1)
    causal = ci <= ri
    tril = causal.astype(BF16)
    triu = (ri <= ci).astype(BF16)
    ones_v = jnp.ones((L, HEAD_DIM), BF16)
    scale = HEAD_DIM ** -0.5

    def body(c, carry):
        r0 = pl.multiple_of(c * L, L)
        g_col = gc_ref[pl.ds(r0, L), :] + bc_ref[...]
        g_row = gr_ref[c] + br_ref[...]
        b_col_all = _exact_left01(tril, _log_sigmoid(g_col))
        b_row_all = _exact_right01(_log_sigmoid(g_row), triu)
        for h in range(HEADS):
            lo = h * HEAD_DIM
            q = q_ref[pl.ds(r0, L), lo:lo + HEAD_DIM].astype(BF16)
            k = k_ref[pl.ds(r0, L), lo:lo + HEAD_DIM] * scale
            v = v_ref[pl.ds(r0, L), lo:lo + HEAD_DIM].astype(BF16)
            v_aug = jnp.concatenate([v, ones_v], axis=1)
            i_rep = jnp.broadcast_to(g_col[:, h:h + 1], (L, LANES))
            b_rep = jnp.broadcast_to(b_col_all[:, HEADS + h:HEADS + h + 1], (L, LANES))
            i_row = g_row[h:h + 1, :]
            b_row = b_row_all[HEADS + h:HEADS + h + 1, :]
            b_last = b_rep[L - 1:L, :]
            m_prev = m_ref[h:h + 1, :]
            c_prev = c_ref[h]

            d_mat = jnp.where(causal, b_rep - b_row + i_row, NEG_INF)
            m_inter = b_rep + m_prev
            m_i = jnp.maximum(m_inter, jnp.max(d_mat, axis=1, keepdims=True))
            s = _dot_nt(q, k.astype(BF16)) * jnp.exp(d_mat - m_i)
            w_inter = jnp.exp(m_inter - m_i)
            intra = _dot(s.astype(BF16), v_aug)
            inter = _dot(q, c_prev.astype(BF16))
            num = intra[:, :HEAD_DIM] + w_inter * inter[:, :HEAD_DIM]
            den = intra[:, HEAD_DIM:] + w_inter * inter[:, HEAD_DIM:]
            hh = num / jnp.maximum(jnp.abs(den), jnp.exp(-m_i))

            w_loc = b_last - b_rep + i_rep
            m_loc = jnp.max(w_loc, axis=0, keepdims=True)
            kp = (k * jnp.exp(w_loc - m_loc)).astype(BF16)
            c_loc = _dot_tn(kp, v_aug)
            m_new = jnp.maximum(b_last + m_prev, m_loc)
            keep = jnp.exp(b_last + m_prev - m_new)
            add = jnp.exp(m_loc - m_new)
            c_ref[h] = (jnp.concatenate([keep, keep], axis=1) * c_prev
                        + jnp.concatenate([add, add], axis=1) * c_loc)
            m_ref[h:h + 1, :] = m_new

            hc = hh - jnp.mean(hh, axis=-1, keepdims=True)
            yn = hc * lax.rsqrt(jnp.mean(hc * hc, axis=-1, keepdims=True) + EPS)
            og = o_ref[pl.ds(r0, L), lo:lo + HEAD_DIM]
            y_ref[pl.ds(r0, L), lo:lo + HEAD_DIM] = (yn * nw_ref[:, lo:lo + HEAD_DIM] * _sigmoid(og)).astype(BF16)
        return carry

    lax.fori_loop(0, n_chunks, body, 0, unroll=MLSTM_UNROLL)


def _mlstm(proj, gates, gate_b, norm_w, batch, seq):
    t = batch * seq
    L = MLSTM_CHUNK
    assert L == LANES, "the kernel keeps per-row gate terms replicated over one vreg of lanes"
    ts = MLSTM_TILE
    nj = seq // ts
    nc = ts // L
    g_row = gates[:, :2 * HEADS].reshape(t // L, L, 2 * HEADS).transpose(0, 2, 1)
    b_col = jnp.zeros((1, LANES), F32).at[0, :2 * HEADS].set(gate_b)
    b_row = gate_b.reshape(2 * HEADS, 1)
    blk = lambda col: pl.BlockSpec((ts, MIX_HALF), lambda b, j, col=col: (b * nj + j, col))
    kern = functools.partial(_mlstm_kernel, chunk=L, n_chunks=nc)
    return pl.pallas_call(
        kern,
        grid=(batch, nj),
        in_specs=[blk(0), blk(1), blk(2), blk(3),
                  pl.BlockSpec((ts, LANES), lambda b, j: (b * nj + j, 0)),
                  pl.BlockSpec((nc, 2 * HEADS, L), lambda b, j: (b * nj + j, 0, 0)),
                  pl.BlockSpec((1, LANES), lambda b, j: (0, 0)),
                  pl.BlockSpec((2 * HEADS, 1), lambda b, j: (0, 0)),
                  pl.BlockSpec((1, MIX_HALF), lambda b, j: (0, 0))],
        out_specs=pl.BlockSpec((ts, MIX_HALF), lambda b, j: (b * nj + j, 0)),
        out_shape=jax.ShapeDtypeStruct((t, MIX_HALF), BF16),
        scratch_shapes=[pltpu.VMEM((HEADS, HEAD_DIM, 2 * HEAD_DIM), F32),
                        pltpu.VMEM((8, LANES), F32)],
        compiler_params=_params("arbitrary", "arbitrary"),
        name="mlstm",
    )(proj, proj, proj, proj, gates, g_row, b_col, b_row, norm_w.reshape(1, MIX_HALF))


def _rglru_kernel(xb_ref, gb_ref, cw_ref, cb_ref, wa_ref, ba_ref, wx_ref, bx_ref, lam_ref,
                  y_ref, xext_ref, h_ref, a_ref, u_ref, *, tile):
    @pl.when(pl.program_id(1) == 0)
    def _():
        xext_ref[0:8, :] = jnp.zeros((8, MIX_HALF), F32)
        h_ref[...] = jnp.zeros_like(h_ref)

    x = xb_ref[...]
    xext_ref[8:8 + tile, :] = x
    xc = cb_ref[...] + cw_ref[LRU_CONV - 1:LRU_CONV, :] * x
    for tap in range(LRU_CONV - 1):
        back = LRU_CONV - 1 - tap
        xc = xc + cw_ref[tap:tap + 1, :] * xext_ref[8 - back:8 - back + tile, :]
    xext_ref[0:8, :] = x[tile - 8:tile, :]

    xc16 = xc.astype(BF16)
    r_parts, i_parts = [], []
    for h in range(HEADS):
        lo = h * HEAD_DIM
        xh = xc16[:, lo:lo + HEAD_DIM]
        r_parts.append(_dot(xh, wa_ref[h]))
        i_parts.append(_dot(xh, wx_ref[h]))
    r = _sigmoid(jnp.concatenate(r_parts, axis=1) + ba_ref[...])
    ig = _sigmoid(jnp.concatenate(i_parts, axis=1) + bx_ref[...])
    lam = lam_ref[...]
    softplus_neg = jnp.maximum(-lam, 0.0) + jnp.log(1.0 + jnp.exp(-jnp.abs(lam)))
    log_a = -LRU_C * r * softplus_neg
    a = jnp.exp(log_a)
    th = jnp.tanh(log_a)
    u = jnp.sqrt(-2.0 * th / (1.0 - th)) * ig * xc

    a_ref[...] = a
    u_ref[...] = u
    rows = lax.broadcasted_iota(jnp.int32, (SUBLANES, MIX_HALF), 0)

    def group(i, h_prev):
        r0 = pl.multiple_of(i * SUBLANES, SUBLANES)
        ag = a_ref[pl.ds(r0, SUBLANES), :]
        ug = u_ref[pl.ds(r0, SUBLANES), :]
        for k in range(LRU_LOG_STEPS):
            keep = rows >= (1 << k)
            ug = ag * jnp.where(keep, pltpu.roll(ug, 1 << k, 0), 0.0) + ug
            ag = ag * jnp.where(keep, pltpu.roll(ag, 1 << k, 0), 1.0)
        hg = ug + ag * h_prev
        u_ref[pl.ds(r0, SUBLANES), :] = hg
        return hg[SUBLANES - 1:SUBLANES, :]

    h_last = lax.fori_loop(0, tile // SUBLANES, group, h_ref[0:1, :], unroll=LRU_UNROLL)
    h_ref[...] = jnp.broadcast_to(h_last, h_ref.shape)
    y_ref[...] = (u_ref[...] * _gelu_tanh(gb_ref[...])).astype(BF16)


def _rglru(proj, conv_w, conv_b, wa, ba, wx, bx, lam, batch, seq):
    t = batch * seq
    ts = LRU_TILE
    nj = seq // ts
    row = lambda a: a.reshape(1, MIX_HALF)
    const2 = lambda shape: pl.BlockSpec(shape, lambda b, j: (0, 0))
    const3 = lambda shape: pl.BlockSpec(shape, lambda b, j: (0, 0, 0))
    blk = lambda col: pl.BlockSpec((ts, MIX_HALF), lambda b, j, col=col: (b * nj + j, col))
    return pl.pallas_call(
        functools.partial(_rglru_kernel, tile=ts),
        grid=(batch, nj),
        in_specs=[blk(4), blk(5), const2((LRU_CONV, MIX_HALF)), const2((1, MIX_HALF)),
                  const3((HEADS, HEAD_DIM, HEAD_DIM)), const2((1, MIX_HALF)),
                  const3((HEADS, HEAD_DIM, HEAD_DIM)), const2((1, MIX_HALF)), const2((1, MIX_HALF))],
        out_specs=pl.BlockSpec((ts, MIX_HALF), lambda b, j: (b * nj + j, 0)),
        out_shape=jax.ShapeDtypeStruct((t, MIX_HALF), BF16),
        scratch_shapes=[pltpu.VMEM((ts + 8, MIX_HALF), F32), pltpu.VMEM((8, MIX_HALF), F32),
                        pltpu.VMEM((ts, MIX_HALF), F32), pltpu.VMEM((ts, MIX_HALF), F32)],
        compiler_params=_params("arbitrary", "arbitrary"),
        name="rglru",
    )(proj, proj, conv_w, row(conv_b), wa.astype(BF16), row(ba), wx.astype(BF16), row(bx), row(lam))


def _s5_kernel(u_ref, bre_ref, bim_ref, cre_ref, cim_ref, mre_ref, mim_ref, pre_ref, pim_ref, d_ref, gw_ref,
               gb_ref, y_ref, xr_ref, xi_ref, cr_ref, ci_ref, *, tile):
    @pl.when(pl.program_id(1) == 0)
    def _():
        cr_ref[...] = jnp.zeros_like(cr_ref)
        ci_ref[...] = jnp.zeros_like(ci_ref)

    u = u_ref[...]
    u16 = u.astype(BF16)
    blk_c = MIX_HALF // S5_BLOCKS
    blk_s = S5_LANES // S5_BLOCKS
    parts = []
    for j in range(S5_BLOCKS):
        lanes = slice(j * blk_s, (j + 1) * blk_s)
        uj = u16[:, j * blk_c:(j + 1) * blk_c]
        xr_ref[:, lanes] = _dot(uj, bre_ref[j])
        xi_ref[:, lanes] = _dot(uj, bim_ref[j])

        def group(i, carry, lanes=lanes):
            cr, ci = carry
            r0 = pl.multiple_of(i * SUBLANES, SUBLANES)
            xr = xr_ref[pl.ds(r0, SUBLANES), lanes]
            xi = xi_ref[pl.ds(r0, SUBLANES), lanes]
            for k in range(S5_LOG_STEPS):
                sr = pltpu.roll(xr, 1 << k, 0)
                si = pltpu.roll(xi, 1 << k, 0)
                mr = mre_ref[k, :, lanes]
                mi = mim_ref[k, :, lanes]
                xr, xi = xr + mr * sr - mi * si, xi + mr * si + mi * sr
            pr = pre_ref[:, lanes]
            pi = pim_ref[:, lanes]
            xr, xi = xr + pr * cr - pi * ci, xi + pr * ci + pi * cr
            xr_ref[pl.ds(r0, SUBLANES), lanes] = xr
            xi_ref[pl.ds(r0, SUBLANES), lanes] = xi
            return xr[SUBLANES - 1:SUBLANES, :], xi[SUBLANES - 1:SUBLANES, :]

        cr, ci = lax.fori_loop(0, tile // SUBLANES, group, (cr_ref[0:1, lanes], ci_ref[0:1, lanes]),
                               unroll=S5_UNROLL)
        cr_ref[0:1, lanes] = cr
        ci_ref[0:1, lanes] = ci
        parts.append(_dot(xr_ref[:, lanes].astype(BF16), cre_ref[j])
                     - _dot(xi_ref[:, lanes].astype(BF16), cim_ref[j]))
    y = jnp.concatenate(parts, axis=1) + d_ref[...] * u
    g = _gelu_tanh(y)
    y_ref[...] = (g * _sigmoid(_dot(g.astype(BF16), gw_ref[...]) + gb_ref[...])).astype(BF16)


def _s5_tables(lam_re, lam_im, b_re, b_im, c_re, c_im, log_dt):
    lr, li = lam_re.astype(F32), lam_im.astype(F32)
    dt = jnp.exp(log_dt.astype(F32))[:, None]
    mag = jnp.exp(lr * dt)
    abar_re = mag * jnp.cos(li * dt)
    abar_im = mag * jnp.sin(li * dt)
    den = lr * lr + li * li
    nr = abar_re - 1.0
    coef_re = (nr * lr + abar_im * li) / den
    coef_im = (abar_im * lr - nr * li) / den
    bbar_re = coef_re[..., None] * b_re - coef_im[..., None] * b_im
    bbar_im = coef_re[..., None] * b_im + coef_im[..., None] * b_re
    gpb = S5_GROUPS // S5_BLOCKS
    eye = jnp.eye(gpb, dtype=F32)

    def in_map(bb):
        bb = bb.reshape(S5_BLOCKS, gpb, S5_STATE, S5_GROUP)
        return jnp.einsum("jgph,gk->jghkp", bb, eye).reshape(S5_BLOCKS, gpb * S5_GROUP, gpb * S5_STATE)

    def out_map(cc):
        cc = cc.reshape(S5_BLOCKS, gpb, S5_GROUP, S5_STATE)
        return jnp.einsum("jghp,gk->jgpkh", cc, eye).reshape(S5_BLOCKS, gpb * S5_STATE, gpb * S5_GROUP)

    def power(n):
        n = jnp.asarray(n, F32)[..., None, None]
        pmag = jnp.exp(n * (lr * dt))
        shape = n.shape[:-2] + (S5_LANES,)
        return (pmag * jnp.cos(n * (li * dt))).reshape(shape), (pmag * jnp.sin(n * (li * dt))).reshape(shape)

    row = jnp.arange(SUBLANES)
    step = 2 ** jnp.arange(S5_LOG_STEPS)
    s_re, s_im = power(step)
    keep = (row[None, :] >= step[:, None])[..., None]
    m_re = jnp.where(keep, s_re[:, None, :], 0.0)
    m_im = jnp.where(keep, s_im[:, None, :], 0.0)
    p_re, p_im = power(row + 1)
    return (in_map(bbar_re).astype(BF16), in_map(bbar_im).astype(BF16),
            out_map(c_re.astype(F32)).astype(BF16), out_map(c_im.astype(F32)).astype(BF16),
            m_re, m_im, p_re, p_im)


def _s5(proj, tables, d_skip, glu_w, glu_b, batch, seq):
    t = batch * seq
    ts = S5_TILE
    nj = seq // ts
    bre, bim, cre, cim, m_re, m_im, p_re, p_im = tables
    blk_c = MIX_HALF // S5_BLOCKS
    blk_s = S5_LANES // S5_BLOCKS
    const2 = lambda shape: pl.BlockSpec(shape, lambda b, j: (0, 0))
    const3 = lambda shape: pl.BlockSpec(shape, lambda b, j: (0, 0, 0))
    return pl.pallas_call(
        functools.partial(_s5_kernel, tile=ts),
        grid=(batch, nj),
        in_specs=[pl.BlockSpec((ts, MIX_HALF), lambda b, j: (b * nj + j, 0)),
                  const3((S5_BLOCKS, blk_c, blk_s)), const3((S5_BLOCKS, blk_c, blk_s)),
                  const3((S5_BLOCKS, blk_s, blk_c)), const3((S5_BLOCKS, blk_s, blk_c)),
                  const3(m_re.shape), const3(m_im.shape), const2(p_re.shape), const2(p_im.shape),
                  const2((1, MIX_HALF)), const2((MIX_HALF, MIX_HALF)), const2((1, MIX_HALF))],
        out_specs=pl.BlockSpec((ts, MIX_HALF), lambda b, j: (b * nj + j, 0)),
        out_shape=jax.ShapeDtypeStruct((t, MIX_HALF), BF16),
        scratch_shapes=[pltpu.VMEM((ts, S5_LANES), F32), pltpu.VMEM((ts, S5_LANES), F32),
                        pltpu.VMEM((8, S5_LANES), F32), pltpu.VMEM((8, S5_LANES), F32)],
        compiler_params=_params("arbitrary", "arbitrary"),
        name="s5",
    )(proj, bre, bim, cre, cim, m_re, m_im, p_re, p_im, d_skip.reshape(1, MIX_HALF), glu_w.astype(BF16),
      glu_b.reshape(1, MIX_HALF))


def _gla_kernel(q_ref, k_ref, v_ref, r_ref, gl_ref, gw_ref, gb_ref, nw_ref, y_ref,
                st_ref, qd_ref, ki_ref, ke_ref, v16_ref, dec_ref, o_ref, *, tile, chunk):
    L = chunk
    nc = tile // L

    @pl.when(pl.program_id(1) == 0)
    def _():
        st_ref[...] = jnp.zeros_like(st_ref)

    z = _dot(gl_ref[...].astype(BF16), gw_ref[...]) + gb_ref[...]
    bcum = _log_sigmoid(z) * (1.0 / GLA_GATE_TEMP)
    row_in_chunk = lax.broadcasted_iota(jnp.int32, bcum.shape, 0) & (L - 1)
    s = 1
    while s < L:
        bcum = bcum + jnp.where(row_in_chunk >= s, pltpu.roll(bcum, s, 0), 0.0)
        s *= 2
    b3 = bcum.reshape(nc, L, MIX_HALF)
    b_last = b3[:, L - 1:L, :]
    k = k_ref[...]
    qd_ref[...] = (q_ref[...] * (GLA_DK ** -0.5) * jnp.exp(bcum)).astype(BF16)
    ki_ref[...] = (k * jnp.exp(-bcum)).astype(BF16)
    ke_ref[...] = (k.reshape(nc, L, MIX_HALF) * jnp.exp(b_last - b3)).reshape(tile, MIX_HALF).astype(BF16)
    v16_ref[...] = v_ref[...].astype(BF16)
    dec_ref[...] = jnp.exp(b_last)

    ri = lax.broadcasted_iota(jnp.int32, (L, L), 0)
    ci = lax.broadcasted_iota(jnp.int32, (L, L), 1)
    causal = ci <= ri

    def body(c, carry):
        r0 = pl.multiple_of(c * L, L)
        dec = dec_ref[c]
        for h in range(HEADS):
            lo = h * HEAD_DIM
            q_dec = qd_ref[pl.ds(r0, L), lo:lo + HEAD_DIM]
            v = v16_ref[pl.ds(r0, L), lo:lo + HEAD_DIM]
            st = st_ref[h]
            att = jnp.where(causal, _dot_nt(q_dec, ki_ref[pl.ds(r0, L), lo:lo + HEAD_DIM]), 0.0)
            o_ref[pl.ds(r0, L), lo:lo + HEAD_DIM] = (_dot(att.astype(BF16), v)
                                                     + _dot_nt(q_dec, st.astype(BF16)))
            st_ref[h] = dec[:, lo:lo + HEAD_DIM] * st + _dot_tn(v, ke_ref[pl.ds(r0, L), lo:lo + HEAD_DIM])
        return carry

    lax.fori_loop(0, nc, body, 0, unroll=GLA_UNROLL)

    rg = r_ref[...]
    gate = nw_ref[...] * (rg * _sigmoid(rg))
    for h in range(HEADS):
        lo = h * HEAD_DIM
        o = o_ref[:, lo:lo + HEAD_DIM]
        yn = o * lax.rsqrt(jnp.mean(o * o, axis=-1, keepdims=True) + EPS)
        y_ref[:, lo:lo + HEAD_DIM] = (yn * gate[:, lo:lo + HEAD_DIM]).astype(BF16)


def _gla(proj, glow, gate_w, gate_b, norm_w, batch, seq):
    t = batch * seq
    ts = GLA_TILE
    nj = seq // ts
    blk = lambda col: pl.BlockSpec((ts, MIX_HALF), lambda b, j, col=col: (b * nj + j, col))
    const2 = lambda shape: pl.BlockSpec(shape, lambda b, j: (0, 0))
    return pl.pallas_call(
        functools.partial(_gla_kernel, tile=ts, chunk=GLA_CHUNK),
        grid=(batch, nj),
        in_specs=[blk(1), blk(2), blk(3), blk(4),
                  pl.BlockSpec((ts, LANES), lambda b, j: (b * nj + j, 0)),
                  const2((LANES, MIX_HALF)), const2((1, MIX_HALF)), const2((1, MIX_HALF))],
        out_specs=pl.BlockSpec((ts, MIX_HALF), lambda b, j: (b * nj + j, 0)),
        out_shape=jax.ShapeDtypeStruct((t, MIX_HALF), BF16),
        scratch_shapes=[pltpu.VMEM((HEADS, HEAD_DIM, HEAD_DIM), F32),
                        pltpu.VMEM((ts, MIX_HALF), BF16), pltpu.VMEM((ts, MIX_HALF), BF16),
                        pltpu.VMEM((ts, MIX_HALF), BF16), pltpu.VMEM((ts, MIX_HALF), BF16),
                        pltpu.VMEM((ts // GLA_CHUNK, 1, MIX_HALF), F32),
                        pltpu.VMEM((ts, MIX_HALF), F32)],
        compiler_params=_params("arbitrary", "arbitrary"),
        name="gla",
    )(proj, proj, proj, proj, glow, gate_w, gate_b, norm_w.reshape(1, MIX_HALF))


def _pad_heads(w, axis):
    shape = list(w.shape)
    shape[axis:axis + 1] = [HEADS, GLA_DK]
    w = w.reshape(shape)
    pad = [(0, 0)] * w.ndim
    pad[axis + 1] = (0, HEAD_DIM - GLA_DK)
    w = jnp.pad(w, pad)
    shape[axis:axis + 2] = [HEADS * HEAD_DIM]
    return w.reshape(shape)


def _pack_bf16_pairs(z):
    hi = lax.bitcast_convert_type(z[:, :PACKED].astype(BF16).astype(F32), jnp.uint32)
    lo = lax.bitcast_convert_type(z[:, PACKED:].astype(BF16).astype(F32), jnp.uint32)
    word = (hi & jnp.uint32(0xFFFF0000)) | lax.shift_right_logical(lo, jnp.uint32(16))
    return lax.bitcast_convert_type(word, jnp.int32)


def _unpack_bf16_pairs(p):
    word = lax.bitcast_convert_type(p, jnp.uint32)
    hi = lax.bitcast_convert_type(word & jnp.uint32(0xFFFF0000), F32)
    lo = lax.bitcast_convert_type(lax.shift_left(word, jnp.uint32(16)), F32)
    return hi, lo


def _out_kernel(ya_ref, yb_ref, h_ref, w_ref, lw_ref, lb_ref, rw_ref, rb_ref,
                o_ref, opk_ref, idx_ref, rank_ref, wk_ref, cnt_ref, base_ref, *, tile):
    mixed = jnp.concatenate([ya_ref[...], yb_ref[...]], axis=1)
    z = ALPHA * h_ref[...] + _dot(mixed, w_ref[...])
    out = _layer_norm(z, lw_ref[...], lb_ref[...])
    o_ref[...] = out
    opk_ref[...] = _pack_bf16_pairs(out)
    _route_tile(out, rw_ref, rb_ref, idx_ref, rank_ref, wk_ref, cnt_ref, base_ref, tile)


def _out_proj_ln_route(ya, yb, h, w_out, ln_w, ln_b, router_w, router_bias):
    t = h.shape[0]
    tm = OUT_TILE
    const = lambda shape: pl.BlockSpec(shape, lambda i: (0, 0))
    per_tok = lambda dt: jax.ShapeDtypeStruct((TOP_K, t), dt)
    tok_blk = pl.BlockSpec((TOP_K, tm), lambda i: (0, i))
    return pl.pallas_call(
        functools.partial(_out_kernel, tile=tm),
        grid=(t // tm,),
        in_specs=[pl.BlockSpec((tm, MIX_HALF), lambda i: (i, 0)),
                  pl.BlockSpec((tm, MIX_HALF), lambda i: (i, 0)),
                  pl.BlockSpec((tm, D_MODEL), lambda i: (i, 0)),
                  const((D_MODEL, D_MODEL)), const((1, D_MODEL)), const((1, D_MODEL)),
                  const((N_EXPERTS, D_MODEL)), const((N_EXPERTS, 1))],
        out_specs=[pl.BlockSpec((tm, D_MODEL), lambda i: (i, 0)),
                   pl.BlockSpec((tm, PACKED), lambda i: (i, 0)),
                   tok_blk, tok_blk, pl.BlockSpec((tm, TOP_K), lambda i: (i, 0)), const((N_EXPERTS, LANES))],
        out_shape=[jax.ShapeDtypeStruct((t, D_MODEL), F32), jax.ShapeDtypeStruct((t, PACKED), jnp.int32),
                   per_tok(jnp.int32), per_tok(jnp.int32), jax.ShapeDtypeStruct((t, TOP_K), F32),
                   jax.ShapeDtypeStruct((N_EXPERTS, LANES), F32)],
        scratch_shapes=[pltpu.VMEM((N_EXPERTS, LANES), F32)],
        compiler_params=_params("arbitrary"),
        name="out_proj_ln_route",
    )(ya, yb, h, w_out.astype(BF16), ln_w.reshape(1, D_MODEL), ln_b.reshape(1, D_MODEL),
      router_w.T, router_bias.reshape(N_EXPERTS, 1))


def _first_index(hit, idx, big):
    return jnp.min(jnp.where(hit, idx, big), axis=0, keepdims=True)


def _route_tile(h, w_ref, b_ref, idx_ref, rank_ref, wk_ref, cnt_ref, base_ref, tile):
    @pl.when(pl.program_id(0) == 0)
    def _():
        base_ref[...] = jnp.zeros_like(base_ref)

    h_hi, h_mid, _ = _split3(h)
    w_hi, w_mid, _ = _split3(w_ref[...])
    logits = _dot_nt(w_hi, h_hi) + _dot_nt(w_hi, h_mid) + _dot_nt(w_mid, h_hi)
    scores = _sigmoid(logits)
    biased = scores + b_ref[...]

    sub = lax.broadcasted_iota(jnp.int32, (GROUP_SIZE, tile), 0)
    grp_rows = []
    for g in range(N_GROUPS):
        xg = biased[g * GROUP_SIZE:(g + 1) * GROUP_SIZE, :]
        m1 = jnp.max(xg, axis=0, keepdims=True)
        i1 = _first_index(xg == m1, sub, GROUP_SIZE)
        m2 = jnp.max(jnp.where(sub == i1, NEG_INF, xg), axis=0, keepdims=True)
        grp_rows.append(m1 + m2)
    gs = jnp.concatenate(grp_rows, axis=0)
    gsel = jnp.zeros((N_GROUPS, tile), F32)
    for _ in range(TOPK_GROUPS):
        mx = jnp.max(gs, axis=0, keepdims=True)
        hit = sub == _first_index(gs == mx, sub, N_GROUPS)
        gsel = jnp.where(hit, 1.0, gsel)
        gs = jnp.where(hit, NEG_INF, gs)
    emask = jnp.concatenate(
        [jnp.broadcast_to(gsel[g:g + 1, :], (GROUP_SIZE, tile)) for g in range(N_GROUPS)], axis=0)

    eidx = lax.broadcasted_iota(jnp.int32, (N_EXPERTS, tile), 0)
    cand = jnp.where(emask > 0.5, biased, NEG_INF)
    sel = jnp.zeros((N_EXPERTS, tile), F32)
    picks = []
    for _ in range(TOP_K):
        mx = jnp.max(cand, axis=0, keepdims=True)
        first = _first_index(cand == mx, eidx, N_EXPERTS)
        hit = eidx == first
        picks.append(first)
        sel = jnp.where(hit, 1.0, sel)
        cand = jnp.where(hit, NEG_INF, cand)
    picked = jnp.where(sel > 0.5, scores, 0.0)
    wts = picked / jnp.sum(picked, axis=0, keepdims=True) * ROUTED_SCALE

    ri = lax.broadcasted_iota(jnp.int32, (tile, tile), 0)
    ci = lax.broadcasted_iota(jnp.int32, (tile, tile), 1)
    before = (ri < ci).astype(BF16)
    prior = _dot(sel.astype(BF16), before) + base_ref[:, 0:1]
    ranks = [jnp.sum(jnp.where(eidx == p, prior, 0.0), axis=0, keepdims=True) for p in picks]
    wsel = [jnp.sum(jnp.where(eidx == p, wts, 0.0), axis=0, keepdims=True) for p in picks]
    idx_ref[...] = jnp.concatenate(picks, axis=0)
    rank_ref[...] = jnp.concatenate(ranks, axis=0).astype(jnp.int32)
    wk_ref[...] = jnp.concatenate(wsel, axis=0).T
    total = base_ref[...] + jnp.sum(sel, axis=1, keepdims=True)
    base_ref[...] = total
    cnt_ref[...] = total


def _silu(x):
    return x * _sigmoid(x)


def _sc_mesh():
    return plsc.VectorSubcoreMesh(core_axis_name="c", subcore_axis_name="s")


def _sc_worker_id():
    return lax.axis_index("s") * SC_CORES + lax.axis_index("c")


def _dispatch_rows(xpk, pos_chunks, n_rows):
    t = xpk.shape[0]
    n_ch = t // SC_WORKERS // SC_CHUNK

    @functools.partial(
        pl.kernel, mesh=_sc_mesh(),
        out_type=jax.ShapeDtypeStruct((n_rows, PACKED), jnp.int32),
        scratch_types=[pltpu.VMEM((TOP_K, SC_CHUNK), jnp.int32),
                       pltpu.VMEM((SC_CHUNK, PACKED), jnp.int32),
                       pltpu.SemaphoreType.DMA],
        name="moe_dispatch",
    )
    def scatter(x_hbm, pos_hbm, out_hbm, idx_v, rows_v, sem):
        wid = _sc_worker_id()

        @pl.loop(0, n_ch)
        def _(c):
            chunk = wid * n_ch + c
            off = pl.multiple_of(chunk * SC_CHUNK, SC_CHUNK)
            pltpu.sync_copy(pos_hbm.at[chunk], idx_v)
            pltpu.sync_copy(x_hbm.at[pl.ds(off, SC_CHUNK)], rows_v)
            copies = [pltpu.async_copy(rows_v, out_hbm.at[idx_v.at[k]], sem) for k in range(TOP_K)]
            for cp in copies:
                cp.wait()

    return scatter(xpk, pos_chunks)


def _gather_rows(table, idx):
    n = idx.shape[0]
    per_w = n // SC_WORKERS
    n_ch = per_w // SC_CHUNK
    assert n_ch % 2 == 0 and n_ch >= 2

    @functools.partial(
        pl.kernel, mesh=_sc_mesh(),
        out_type=jax.ShapeDtypeStruct((n, PACKED), jnp.int32),
        scratch_types=[pltpu.VMEM((n_ch, SC_CHUNK), jnp.int32),
                       pltpu.VMEM((SC_CHUNK, PACKED), jnp.int32), pltpu.VMEM((SC_CHUNK, PACKED), jnp.int32),
                       pltpu.SemaphoreType.DMA, pltpu.SemaphoreType.DMA,
                       pltpu.SemaphoreType.DMA, pltpu.SemaphoreType.DMA],
        name="moe_gather",
    )
    def gather(table_hbm, idx_hbm, out_hbm, idx_v, rows0, rows1, g0, g1, w0, w1):
        wid = _sc_worker_id()
        base = wid * per_w
        rows, g_sem, w_sem = (rows0, rows1), (g0, g1), (w0, w1)
        pltpu.sync_copy(idx_hbm.at[wid], idx_v)

        def fetch(c, b):
            return pltpu.make_async_copy(table_hbm.at[idx_v.at[c]], rows[b], g_sem[b])

        def flush(c, b):
            off = pl.multiple_of(base + c * SC_CHUNK, SC_CHUNK)
            return pltpu.make_async_copy(rows[b], out_hbm.at[pl.ds(off, SC_CHUNK)], w_sem[b])

        fetch(0, 0).start()

        @pl.loop(0, n_ch, step=2)
        def _(c0):
            for b in range(2):
                c = c0 + b
                fetch(c, b).wait()
                flush(c, b).start()

                @pl.when(c + 1 < n_ch)
                def _():
                    @pl.when(c >= 1)
                    def _():
                        flush(c - 1, 1 - b).wait()
                    fetch(c + 1, 1 - b).start()

        flush(n_ch - 2, 0).wait()
        flush(n_ch - 1, 1).wait()

    return gather(table, idx.reshape(SC_WORKERS, n_ch, SC_CHUNK))


def _unpacked_bf16(p):
    hi, lo = _unpack_bf16_pairs(p)
    return jnp.concatenate([hi.astype(BF16), lo.astype(BF16)], axis=1)


def _expert_kernel(be_ref, nu_ref, next_ref, slot_ref, xs_hbm, wg_hbm, wu_hbm, wd_hbm, y_ref,
                   g16_ref, u16_ref, d16_ref, gf_ref, uf_ref, df_ref, xbuf_ref, xsem, wsem, *, layer):
    i = pl.program_id(0)
    n_used = nu_ref[0]
    bm = xbuf_ref.shape[1]

    def fetch(b):
        slot = lax.rem(b, XS_SLOTS)
        rows = pl.ds(pl.multiple_of(b * bm, bm), bm)
        return pltpu.make_async_copy(xs_hbm.at[rows], xbuf_ref.at[slot], xsem.at[slot])

    @pl.when(i == 0)
    def _():
        fetch(0).start()

        @pl.when(n_used > 1)
        def _():
            fetch(1).start()

    @pl.when(i + 2 < n_used)
    def _():
        fetch(i + 2).start()

    e = be_ref[i]
    slot = slot_ref[e]
    f32_bufs = (gf_ref, uf_ref, df_ref)

    def wfetch(expert, dst_slot):
        return [pltpu.make_async_copy(w_hbm.at[layer, expert], buf.at[dst_slot], wsem.at[dst_slot, j])
                for j, (w_hbm, buf) in enumerate(zip((wg_hbm, wu_hbm, wd_hbm), f32_bufs))]

    @pl.when(i == 0)
    def _():
        for cp in wfetch(e, slot):
            cp.start()

    first_block_of_expert = jnp.logical_or(i == 0, e != be_ref[jnp.maximum(i - 1, 0)])

    @pl.when(jnp.logical_and(first_block_of_expert, i < n_used))
    def _():
        for cp in wfetch(e, slot):
            cp.wait()
        g16_ref[slot] = gf_ref[slot].astype(BF16)
        u16_ref[slot] = uf_ref[slot].astype(BF16)
        d16_ref[slot] = df_ref[slot].astype(BF16)
        nxt = next_ref[e]

        @pl.when(nxt >= 0)
        def _():
            for cp in wfetch(nxt, 1 - slot):
                cp.start()

    @pl.when(i < n_used)
    def _():
        fetch(i).wait()
        x = _unpacked_bf16(xbuf_ref[lax.rem(i, XS_SLOTS)])
        hh = _silu(_dot(x, g16_ref[slot])) * _dot(x, u16_ref[slot])
        y_ref[...] = _pack_bf16_pairs(_dot(hh.astype(BF16), d16_ref[slot]))


def _experts(block_e, n_used, next_expert, weight_slot, xs, wg, wu, wd, layer):
    nb = block_e.shape[0]
    bm = MOE_BLOCK
    hbm = pl.BlockSpec(memory_space=pl.ANY)
    two = lambda shape, dt: pltpu.VMEM((2,) + shape, dt)
    grid_spec = pltpu.PrefetchScalarGridSpec(
        num_scalar_prefetch=4,
        grid=(nb,),
        in_specs=[hbm, hbm, hbm, hbm],
        out_specs=pl.BlockSpec((bm, PACKED), lambda i, be, nu, nx, sl: (jnp.minimum(i, nu[0] - 1), 0)),
        scratch_shapes=[two((D_MODEL, D_EXPERT), BF16), two((D_MODEL, D_EXPERT), BF16), two((D_EXPERT, D_MODEL), BF16),
                        two((D_MODEL, D_EXPERT), F32), two((D_MODEL, D_EXPERT), F32), two((D_EXPERT, D_MODEL), F32),
                        pltpu.VMEM((XS_SLOTS, bm, PACKED), jnp.int32), pltpu.SemaphoreType.DMA((XS_SLOTS,)),
                        pltpu.SemaphoreType.DMA((2, 3))],
    )
    return pl.pallas_call(
        functools.partial(_expert_kernel, layer=layer),
        grid_spec=grid_spec,
        out_shape=jax.ShapeDtypeStruct((nb * bm, PACKED), jnp.int32),
        compiler_params=_params("arbitrary"),
        name="moe_experts",
    )(block_e, n_used, next_expert, weight_slot, xs, wg, wu, wd)


def _combine_kernel(g_ref, wk_ref, h_ref, xpk_ref, sg_ref, su_ref, sd_ref, lw_ref, lb_ref, *rest, with_proj):
    if with_proj:
        (wm_ref, ws_ref), (o_ref, proj_ref, small_ref) = rest[:2], rest[-3:]
        next_proj = (wm_ref, ws_ref, proj_ref, small_ref)
    else:
        o_ref, next_proj = rest[-1], None
    x = _unpacked_bf16(xpk_ref[...])
    hs = _silu(_dot(x, sg_ref[...])) * _dot(x, su_ref[...])
    shared = _dot(hs.astype(BF16), sd_ref[...])
    acc_hi = shared[:, :PACKED]
    acc_lo = shared[:, PACKED:]
    wk = wk_ref[...]
    for k in range(TOP_K):
        y_hi, y_lo = _unpack_bf16_pairs(g_ref[k])
        w = wk[:, k:k + 1]
        acc_hi = acc_hi + w * y_hi
        acc_lo = acc_lo + w * y_lo
    ffn = jnp.concatenate([acc_hi, acc_lo], axis=1)
    out = _layer_norm(ALPHA * h_ref[...] + ffn, lw_ref[...], lb_ref[...])
    o_ref[...] = out
    if next_proj is not None:
        wm_ref, ws_ref, proj_ref, small_ref = next_proj
        out16 = out.astype(BF16)
        proj_ref[...] = _dot(out16, wm_ref[...])
        small_ref[...] = _dot(out16, ws_ref[...])


def _combine_ln(g, wk, h, xpk, sg, su, sd, ln_w, ln_b, next_w, part, prev):
    t = h.shape[0]
    tm = COMBINE_TILE if next_w is not None else 2 * COMBINE_TILE
    n_blk = g.shape[1] // tm
    first = part * n_blk
    const = lambda shape: pl.BlockSpec(shape, lambda i: (0, 0))
    rows = lambda width: pl.BlockSpec((tm, width), lambda i: (i + first, 0))
    in_specs = [pl.BlockSpec((TOP_K, tm, PACKED), lambda i: (0, i, 0)), rows(TOP_K), rows(D_MODEL), rows(PACKED),
                const((D_MODEL, D_EXPERT)), const((D_MODEL, D_EXPERT)), const((D_EXPERT, D_MODEL)),
                const((1, D_MODEL)), const((1, D_MODEL))]
    args = [g, wk, h, xpk, sg.astype(BF16), su.astype(BF16), sd.astype(BF16),
            ln_w.reshape(1, D_MODEL), ln_b.reshape(1, D_MODEL)]
    out_specs = [rows(D_MODEL)]
    out_shape = [jax.ShapeDtypeStruct((t, D_MODEL), F32)]
    if next_w is not None:
        w_main, w_small = next_w
        n = w_main.shape[1]
        in_specs += [const((D_MODEL, n)), const((D_MODEL, LANES))]
        args += [w_main, w_small]
        out_specs += [rows(n), rows(LANES)]
        out_shape += [jax.ShapeDtypeStruct((t, n), F32), jax.ShapeDtypeStruct((t, LANES), F32)]
    aliases = {}
    if prev is not None:
        aliases = {len(args) + k: k for k in range(len(prev))}
        in_specs += [pl.BlockSpec(memory_space=pl.ANY)] * len(prev)
        args += list(prev)
    return pl.pallas_call(
        functools.partial(_combine_kernel, with_proj=next_w is not None),
        grid=(n_blk,),
        in_specs=in_specs,
        out_specs=out_specs,
        out_shape=out_shape,
        input_output_aliases=aliases,
        compiler_params=_params("parallel"),
        name="moe_combine_ln",
    )(*args)


def _moe_ln(h, hpk, idx, rank, wk, counts, wg, wu, wd, layer, sg, su, sd, ln_w, ln_b, next_w):
    t = h.shape[0]
    cnt = counts[:, 0].astype(jnp.int32)
    padded = (cnt + MOE_BLOCK - 1) // MOE_BLOCK * MOE_BLOCK
    pend = jnp.cumsum(padded)
    experts = jnp.arange(N_EXPERTS, dtype=jnp.int32)
    pstart_of_pick = jnp.sum(jnp.where(idx[:, :, None] == experts, pend - padded, 0), axis=-1)
    pos = pstart_of_pick + rank
    nb = -(-(t * TOP_K + N_EXPERTS * (MOE_BLOCK - 1)) // MOE_BLOCK)
    starts = jnp.arange(nb, dtype=jnp.int32) * MOE_BLOCK
    block_e = jnp.minimum(jnp.sum((pend[None, :] <= starts[:, None]).astype(jnp.int32), axis=1), N_EXPERTS - 1)
    n_used = (pend[-1] // MOE_BLOCK).astype(jnp.int32).reshape(1)
    has_rows = cnt > 0
    later = jnp.logical_and(has_rows[None, :], experts[None, :] > experts[:, None])
    next_expert = jnp.min(jnp.where(later, experts[None, :], N_EXPERTS), axis=1)
    next_expert = jnp.where(next_expert == N_EXPERTS, -1, next_expert).astype(jnp.int32)
    weight_slot = ((jnp.cumsum(has_rows) - has_rows) % 2).astype(jnp.int32)
    pos_chunks = pos.reshape(TOP_K, t // SC_CHUNK, SC_CHUNK).transpose(1, 0, 2)
    xs = _dispatch_rows(hpk, pos_chunks, nb * MOE_BLOCK)
    ys = _experts(block_e, n_used, next_expert, weight_slot, xs, wg, wu, wd, layer)
    part = t // COMBINE_PARTS
    gathered = [_gather_rows(ys, pos[:, p * part:(p + 1) * part].reshape(-1)).reshape(TOP_K, part, PACKED)
                for p in range(COMBINE_PARTS)]
    outs = None
    for p in range(COMBINE_PARTS):
        outs = _combine_ln(gathered[p], wk, h, hpk, sg, su, sd, ln_w, ln_b, next_w, p, outs)
    return outs


def _pad_cols(w, width=LANES):
    return jnp.pad(w, ((0, 0), (0, width - w.shape[1])))


def _even_proj_weights(w_in):
    a4 = 4 * MIX_HALF
    ng = 2 * HEADS
    w_main = jnp.concatenate([w_in[:, :a4], w_in[:, a4 + ng:]], axis=1).astype(BF16)
    w_gate = _pad_cols(w_in[:, a4:a4 + ng]).astype(BF16)
    return w_main, w_gate


def _even_mixer(proj, gates, batch, seq, gate_b, norm_w, conv_w, conv_b, wa, ba, wx, bx, lam):
    ya = _mlstm(proj, gates, gate_b, norm_w, batch, seq)
    yb = _rglru(proj, conv_w, conv_b, wa, ba, wx, bx, lam, batch, seq)
    return ya, yb


def _odd_proj_weights(w_in):
    c0 = MIX_HALF
    c1 = c0 + HEADS * GLA_DK
    c2 = c1 + HEADS * GLA_DK
    c3 = c2 + MIX_HALF
    c4 = c3 + MIX_HALF
    w_main = jnp.concatenate([w_in[:, :c0], _pad_heads(w_in[:, c0:c1], 1), _pad_heads(w_in[:, c1:c2], 1),
                              w_in[:, c2:c4]], axis=1).astype(BF16)
    w_low = _pad_cols(w_in[:, c4:]).astype(BF16)
    return w_main, w_low


def _odd_mixer(proj, glow, batch, seq, lam_re, lam_im, b_re, b_im, c_re, c_im, d_skip, log_dt,
               glu_w, glu_b, gate_w, gate_b, norm_w):
    tables = _s5_tables(lam_re, lam_im, b_re, b_im, c_re, c_im, log_dt)
    yc = _s5(proj, tables, d_skip, glu_w, glu_b, batch, seq)
    gw = jnp.pad(_pad_heads(gate_w, 1), ((0, LANES - GLA_GATE_RANK), (0, 0))).astype(BF16)
    gb = _pad_heads(gate_b.reshape(1, -1), 1)
    yd = _gla(proj, glow, gw, gb, norm_w, batch, seq)
    return yc, yd


def kernel(x, ln1_w, ln1_b, ln2_w, ln2_b, w_out, w_in_even, mlstm_gate_b, mlstm_norm_w, lru_conv_w, lru_conv_b, lru_wa, lru_ba, lru_wx, lru_bx, lru_lambda, w_in_odd, s5_lam_re, s5_lam_im, s5_b_re, s5_b_im, s5_c_re, s5_c_im, s5_d, s5_log_dt, s5_glu_w, s5_glu_b, gla_gate_w, gla_gate_b, gla_norm_w, router_w, router_bias, exp_w_gate, exp_w_up, exp_w_down, sh_w_gate, sh_w_up, sh_w_down):
    batch, seq, d = x.shape
    proj_w = [_even_proj_weights(w_in_even[layer // 2]) if layer % 2 == 0 else _odd_proj_weights(w_in_odd[layer // 2])
              for layer in range(DEPTH)]
    h = x.reshape(batch * seq, d)
    proj, small = _proj(h, *proj_w[0])
    for layer in range(DEPTH):
        j = layer // 2
        if layer % 2 == 0:
            y1, y2 = _even_mixer(proj, small, batch, seq, mlstm_gate_b[j], mlstm_norm_w[j],
                                 lru_conv_w[j], lru_conv_b[j], lru_wa[j], lru_ba[j], lru_wx[j],
                                 lru_bx[j], lru_lambda[j])
        else:
            y1, y2 = _odd_mixer(proj, small, batch, seq, s5_lam_re[j], s5_lam_im[j], s5_b_re[j],
                                s5_b_im[j], s5_c_re[j], s5_c_im[j], s5_d[j], s5_log_dt[j],
                                s5_glu_w[j], s5_glu_b[j], gla_gate_w[j], gla_gate_b[j], gla_norm_w[j])
        h, hpk, idx, rank, wk, counts = _out_proj_ln_route(y1, y2, h, w_out[layer], ln1_w[layer], ln1_b[layer],
                                                           router_w[layer], router_bias[layer])
        next_w = proj_w[layer + 1] if layer + 1 < DEPTH else None
        res = _moe_ln(h, hpk, idx, rank, wk, counts, exp_w_gate, exp_w_up, exp_w_down, layer,
                      sh_w_gate[layer], sh_w_up[layer], sh_w_down[layer], ln2_w[layer], ln2_b[layer], next_w)
        if next_w is None:
            (h,) = res
        else:
            h, proj, small = res
    return h.reshape(batch, seq, d)
```

```python
import functools
import math

import jax
import jax.numpy as jnp
from jax import lax
from jax.experimental import pallas as pl
from jax.experimental.pallas import tpu as pltpu
from jax.experimental.pallas import tpu_sc as plsc

F32 = jnp.float32
BF16 = jnp.bfloat16

D_MODEL = 1024
DEPTH = 2
MIX_HALF = 512
HEADS = 4
HEAD_DIM = 128
GLA_DK = 64
GLA_CHUNK = 64
GLA_GATE_RANK = 16
GLA_GATE_TEMP = 16.0
LRU_C = 8.0
LRU_CONV = 4
S5_GROUP = 16
S5_GROUPS = 32
S5_STATE = 64
S5_LANES = S5_GROUPS * S5_STATE
S5_BLOCKS = 4
N_EXPERTS = 64
N_GROUPS = 8
GROUP_SIZE = N_EXPERTS // N_GROUPS
TOP_K = 8
TOPK_GROUPS = 4
D_EXPERT = 256
ROUTED_SCALE = 2.5
ALPHA = (2.0 * DEPTH) ** 0.25
EPS = 1e-5
LANES = 128
SUBLANES = 8
NEG_INF = float("-inf")

VMEM_LIMIT = 56 * 1024 * 1024

MLSTM_CHUNK = 128
MLSTM_TILE = 1024
MLSTM_UNROLL = 2
LRU_TILE = 1024
LRU_LOG_STEPS = 3
LRU_UNROLL = 4
S5_TILE = 512
S5_LOG_STEPS = 3
S5_UNROLL = True
GLA_UNROLL = 8
GLA_TILE = 1024
PROJ_TILE = 1024
OUT_TILE = 1024
MOE_BLOCK = 1152
XS_SLOTS = 3
COMBINE_TILE = 256
COMBINE_SPLIT = (0, 1, 4)
PACKED = D_MODEL // 2
SC_CHUNK = 64
SC_CORES = 2
SC_SUBCORES = 16
SC_WORKERS = SC_CORES * SC_SUBCORES


def _params(*sem):
    return pltpu.CompilerParams(dimension_semantics=sem, vmem_limit_bytes=VMEM_LIMIT)


def _split3(x):
    hi = x.astype(BF16)
    r1 = x - hi.astype(F32)
    mid = r1.astype(BF16)
    lo = (r1 - mid.astype(F32)).astype(BF16)
    return hi, mid, lo


def _dot(a, b):
    return jnp.dot(a, b, preferred_element_type=F32)


def _dot_nt(a, b):
    return lax.dot_general(a, b, (((1,), (1,)), ((), ())), preferred_element_type=F32)


def _dot_tn(a, b):
    return lax.dot_general(a, b, (((0,), (0,)), ((), ())), preferred_element_type=F32)


def _exact_left01(mask01_bf16, x):
    hi, mid, lo = _split3(x)
    return _dot(mask01_bf16, hi) + _dot(mask01_bf16, mid) + _dot(mask01_bf16, lo)


def _exact_right01(x, mask01_bf16):
    hi, mid, lo = _split3(x)
    return _dot(hi, mask01_bf16) + _dot(mid, mask01_bf16) + _dot(lo, mask01_bf16)


def _log_sigmoid(x):
    return jnp.minimum(x, 0.0) - jnp.log(1.0 + jnp.exp(-jnp.abs(x)))


def _sigmoid(x):
    return 1.0 / (1.0 + jnp.exp(-x))


def _gelu_tanh(x):
    c = math.sqrt(2.0 / math.pi)
    return 0.5 * x * (1.0 + jnp.tanh(c * (x + 0.044715 * (x * x * x))))


def _layer_norm(z, w, b):
    mu = jnp.mean(z, axis=-1, keepdims=True)
    zc = z - mu
    return zc * lax.rsqrt(jnp.mean(zc * zc, axis=-1, keepdims=True) + EPS) * w + b


def _proj_kernel(x_ref, w_ref, wg_ref, o_ref, og_ref):
    x = x_ref[...].astype(BF16)
    o_ref[...] = _dot(x, w_ref[...])
    og_ref[...] = _dot(x, wg_ref[...])


def _proj(x, w_main, w_small):
    t, d = x.shape
    n = w_main.shape[1]
    tm = PROJ_TILE
    return pl.pallas_call(
        _proj_kernel,
        grid=(t // tm,),
        in_specs=[pl.BlockSpec((tm, d), lambda i: (i, 0)),
                  pl.BlockSpec((d, n), lambda i: (0, 0)),
                  pl.BlockSpec((d, LANES), lambda i: (0, 0))],
        out_specs=[pl.BlockSpec((tm, n), lambda i: (i, 0)),
                   pl.BlockSpec((tm, LANES), lambda i: (i, 0))],
        out_shape=[jax.ShapeDtypeStruct((t, n), F32), jax.ShapeDtypeStruct((t, LANES), F32)],
        compiler_params=_params("parallel"),
        name="in_proj",
    )(x, w_main, w_small)


def _mlstm_kernel(q_ref, k_ref, v_ref, o_ref, gc_ref, gr_ref, bc_ref, br_ref, nw_ref,
                  y_ref, c_ref, m_ref, *, chunk, n_chunks):
    L = chunk

    @pl.when(pl.program_id(1) == 0)
    def _():
        c_ref[...] = jnp.zeros_like(c_ref)
        m_ref[...] = jnp.zeros_like(m_ref)

    ri = lax.broadcasted_iota(jnp.int32, (L, L), 0)
    ci = lax.broadcasted_iota(jnp.int32, (L, L), 1)
    causal = ci <= ri
    tril = causal.astype(BF16)
    triu = (ri <= ci).astype(BF16)
    ones_v = jnp.ones((L, HEAD_DIM), BF16)
    scale = HEAD_DIM ** -0.5

    def body(c, carry):
        r0 = pl.multiple_of(c * L, L)
        g_col = gc_ref[pl.ds(r0, L), :] + bc_ref[...]
        g_row = gr_ref[c] + br_ref[...]
        b_col_all = _exact_left01(tril, _log_sigmoid(g_col))
        b_row_all = _exact_right01(_log_sigmoid(g_row), triu)
        for h in range(HEADS):
            lo = h * HEAD_DIM
            q = q_ref[pl.ds(r0, L), lo:lo + HEAD_DIM].astype(BF16)
            k = k_ref[pl.ds(r0, L), lo:lo + HEAD_DIM] * scale
            v = v_ref[pl.ds(r0, L), lo:lo + HEAD_DIM].astype(BF16)
            v_aug = jnp.concatenate([v, ones_v], axis=1)
            i_rep = jnp.broadcast_to(g_col[:, h:h + 1], (L, LANES))
            b_rep = jnp.broadcast_to(b_col_all[:, HEADS + h:HEADS + h + 1], (L, LANES))
            i_row = g_row[h:h + 1, :]
            b_row = b_row_all[HEADS + h:HEADS + h + 1, :]
            b_last = b_rep[L - 1:L, :]
            m_prev = m_ref[h:h + 1, :]
            c_prev = c_ref[h]

            d_mat = jnp.where(causal, b_rep - b_row + i_row, NEG_INF)
            m_inter = b_rep + m_prev
            m_i = jnp.maximum(m_inter, jnp.max(d_mat, axis=1, keepdims=True))
            s = _dot_nt(q, k.astype(BF16)) * jnp.exp(d_mat - m_i)
            w_inter = jnp.exp(m_inter - m_i)
            intra = _dot(s.astype(BF16), v_aug)
            inter = _dot(q, c_prev.astype(BF16))
            num = intra[:, :HEAD_DIM] + w_inter * inter[:, :HEAD_DIM]
            den = intra[:, HEAD_DIM:] + w_inter * inter[:, HEAD_DIM:]
            hh = num / jnp.maximum(jnp.abs(den), jnp.exp(-m_i))

            w_loc = b_last - b_rep + i_rep
            m_loc = jnp.max(w_loc, axis=0, keepdims=True)
            kp = (k * jnp.exp(w_loc - m_loc)).astype(BF16)
            c_loc = _dot_tn(kp, v_aug)
            m_new = jnp.maximum(b_last + m_prev, m_loc)
            keep = jnp.exp(b_last + m_prev - m_new)
            add = jnp.exp(m_loc - m_new)
            c_ref[h] = (jnp.concatenate([keep, keep], axis=1) * c_prev
                        + jnp.concatenate([add, add], axis=1) * c_loc)
            m_ref[h:h + 1, :] = m_new

            hc = hh - jnp.mean(hh, axis=-1, keepdims=True)
            yn = hc * lax.rsqrt(jnp.mean(hc * hc, axis=-1, keepdims=True) + EPS)
            og = o_ref[pl.ds(r0, L), lo:lo + HEAD_DIM]
            y_ref[pl.ds(r0, L), lo:lo + HEAD_DIM] = (yn * nw_ref[:, lo:lo + HEAD_DIM] * _sigmoid(og)).astype(BF16)
        return carry

    lax.fori_loop(0, n_chunks, body, 0, unroll=MLSTM_UNROLL)


def _mlstm(proj, gates, gate_b, norm_w, batch, seq):
    t = batch * seq
    L = MLSTM_CHUNK
    assert L == LANES, "the kernel keeps per-row gate terms replicated over one vreg of lanes"
    ts = MLSTM_TILE
    nj = seq // ts
    nc = ts // L
    g_row = gates[:, :2 * HEADS].reshape(t // L, L, 2 * HEADS).transpose(0, 2, 1)
    b_col = jnp.zeros((1, LANES), F32).at[0, :2 * HEADS].set(gate_b)
    b_row = gate_b.reshape(2 * HEADS, 1)
    blk = lambda col: pl.BlockSpec((ts, MIX_HALF), lambda b, j, col=col: (b * nj + j, col))
    kern = functools.partial(_mlstm_kernel, chunk=L, n_chunks=nc)
    return pl.pallas_call(
        kern,
        grid=(batch, nj),
        in_specs=[blk(0), blk(1), blk(2), blk(3),
                  pl.BlockSpec((ts, LANES), lambda b, j: (b * nj + j, 0)),
                  pl.BlockSpec((nc, 2 * HEADS, L), lambda b, j: (b * nj + j, 0, 0)),
                  pl.BlockSpec((1, LANES), lambda b, j: (0, 0)),
                  pl.BlockSpec((2 * HEADS, 1), lambda b, j: (0, 0)),
                  pl.BlockSpec((1, MIX_HALF), lambda b, j: (0, 0))],
        out_specs=pl.BlockSpec((ts, MIX_HALF), lambda b, j: (b * nj + j, 0)),
        out_shape=jax.ShapeDtypeStruct((t, MIX_HALF), BF16),
        scratch_shapes=[pltpu.VMEM((HEADS, HEAD_DIM, 2 * HEAD_DIM), F32),
                        pltpu.VMEM((8, LANES), F32)],
        compiler_params=_params("arbitrary", "arbitrary"),
        name="mlstm",
    )(proj, proj, proj, proj, gates, g_row, b_col, b_row, norm_w.reshape(1, MIX_HALF))


def _rglru_kernel(xb_ref, gb_ref, cw_ref, cb_ref, wa_ref, ba_ref, wx_ref, bx_ref, lam_ref,
                  y_ref, xext_ref, h_ref, a_ref, u_ref, *, tile):
    @pl.when(pl.program_id(1) == 0)
    def _():
        xext_ref[0:8, :] = jnp.zeros((8, MIX_HALF), F32)
        h_ref[...] = jnp.zeros_like(h_ref)

    x = xb_ref[...]
    xext_ref[8:8 + tile, :] = x
    xc = cb_ref[...] + cw_ref[LRU_CONV - 1:LRU_CONV, :] * x
    for tap in range(LRU_CONV - 1):
        back = LRU_CONV - 1 - tap
        xc = xc + cw_ref[tap:tap + 1, :] * xext_ref[8 - back:8 - back + tile, :]
    xext_ref[0:8, :] = x[tile - 8:tile, :]

    xc16 = xc.astype(BF16)
    r_parts, i_parts = [], []
    for h in range(HEADS):
        lo = h * HEAD_DIM
        xh = xc16[:, lo:lo + HEAD_DIM]
        r_parts.append(_dot(xh, wa_ref[h]))
        i_parts.append(_dot(xh, wx_ref[h]))
    r = _sigmoid(jnp.concatenate(r_parts, axis=1) + ba_ref[...])
    ig = _sigmoid(jnp.concatenate(i_parts, axis=1) + bx_ref[...])
    lam = lam_ref[...]
    softplus_neg = jnp.maximum(-lam, 0.0) + jnp.log(1.0 + jnp.exp(-jnp.abs(lam)))
    log_a = -LRU_C * r * softplus_neg
    a = jnp.exp(log_a)
    th = jnp.tanh(log_a)
    u = jnp.sqrt(-2.0 * th / (1.0 - th)) * ig * xc

    a_ref[...] = a
    u_ref[...] = u
    rows = lax.broadcasted_iota(jnp.int32, (SUBLANES, MIX_HALF), 0)

    def group(i, h_prev):
        r0 = pl.multiple_of(i * SUBLANES, SUBLANES)
        ag = a_ref[pl.ds(r0, SUBLANES), :]
        ug = u_ref[pl.ds(r0, SUBLANES), :]
        for k in range(LRU_LOG_STEPS):
            keep = rows >= (1 << k)
            ug = ag * jnp.where(keep, pltpu.roll(ug, 1 << k, 0), 0.0) + ug
            ag = ag * jnp.where(keep, pltpu.roll(ag, 1 << k, 0), 1.0)
        hg = ug + ag * h_prev
        u_ref[pl.ds(r0, SUBLANES), :] = hg
        return hg[SUBLANES - 1:SUBLANES, :]

    h_last = lax.fori_loop(0, tile // SUBLANES, group, h_ref[0:1, :], unroll=LRU_UNROLL)
    h_ref[...] = jnp.broadcast_to(h_last, h_ref.shape)
    y_ref[...] = (u_ref[...] * _gelu_tanh(gb_ref[...])).astype(BF16)


def _rglru(proj, conv_w, conv_b, wa, ba, wx, bx, lam, batch, seq):
    t = batch * seq
    ts = LRU_TILE
    nj = seq // ts
    row = lambda a: a.reshape(1, MIX_HALF)
    const2 = lambda shape: pl.BlockSpec(shape, lambda b, j: (0, 0))
    const3 = lambda shape: pl.BlockSpec(shape, lambda b, j: (0, 0, 0))
    blk = lambda col: pl.BlockSpec((ts, MIX_HALF), lambda b, j, col=col: (b * nj + j, col))
    return pl.pallas_call(
        functools.partial(_rglru_kernel, tile=ts),
        grid=(batch, nj),
        in_specs=[blk(4), blk(5), const2((LRU_CONV, MIX_HALF)), const2((1, MIX_HALF)),
                  const3((HEADS, HEAD_DIM, HEAD_DIM)), const2((1, MIX_HALF)),
                  const3((HEADS, HEAD_DIM, HEAD_DIM)), const2((1, MIX_HALF)), const2((1, MIX_HALF))],
        out_specs=pl.BlockSpec((ts, MIX_HALF), lambda b, j: (b * nj + j, 0)),
        out_shape=jax.ShapeDtypeStruct((t, MIX_HALF), BF16),
        scratch_shapes=[pltpu.VMEM((ts + 8, MIX_HALF), F32), pltpu.VMEM((8, MIX_HALF), F32),
                        pltpu.VMEM((ts, MIX_HALF), F32), pltpu.VMEM((ts, MIX_HALF), F32)],
        compiler_params=_params("arbitrary", "arbitrary"),
        name="rglru",
    )(proj, proj, conv_w, row(conv_b), wa.astype(BF16), row(ba), wx.astype(BF16), row(bx), row(lam))


def _s5_kernel(u_ref, bre_ref, bim_ref, cre_ref, cim_ref, mre_ref, mim_ref, pre_ref, pim_ref, d_ref, gw_ref,
               gb_ref, y_ref, xr_ref, xi_ref, cr_ref, ci_ref, *, tile):
    @pl.when(pl.program_id(1) == 0)
    def _():
        cr_ref[...] = jnp.zeros_like(cr_ref)
        ci_ref[...] = jnp.zeros_like(ci_ref)

    u = u_ref[...]
    u16 = u.astype(BF16)
    blk_c = MIX_HALF // S5_BLOCKS
    blk_s = S5_LANES // S5_BLOCKS
    parts = []
    for j in range(S5_BLOCKS):
        lanes = slice(j * blk_s, (j + 1) * blk_s)
        uj = u16[:, j * blk_c:(j + 1) * blk_c]
        xr_ref[:, lanes] = _dot(uj, bre_ref[j])
        xi_ref[:, lanes] = _dot(uj, bim_ref[j])

        def group(i, carry, lanes=lanes):
            cr, ci = carry
            r0 = pl.multiple_of(i * SUBLANES, SUBLANES)
            xr = xr_ref[pl.ds(r0, SUBLANES), lanes]
            xi = xi_ref[pl.ds(r0, SUBLANES), lanes]
            for k in range(S5_LOG_STEPS):
                sr = pltpu.roll(xr, 1 << k, 0)
                si = pltpu.roll(xi, 1 << k, 0)
                mr = mre_ref[k, :, lanes]
                mi = mim_ref[k, :, lanes]
                xr, xi = xr + mr * sr - mi * si, xi + mr * si + mi * sr
            pr = pre_ref[:, lanes]
            pi = pim_ref[:, lanes]
            xr, xi = xr + pr * cr - pi * ci, xi + pr * ci + pi * cr
            xr_ref[pl.ds(r0, SUBLANES), lanes] = xr
            xi_ref[pl.ds(r0, SUBLANES), lanes] = xi
            return xr[SUBLANES - 1:SUBLANES, :], xi[SUBLANES - 1:SUBLANES, :]

        cr, ci = lax.fori_loop(0, tile // SUBLANES, group, (cr_ref[0:1, lanes], ci_ref[0:1, lanes]),
                               unroll=S5_UNROLL)
        cr_ref[0:1, lanes] = cr
        ci_ref[0:1, lanes] = ci
        parts.append(_dot(xr_ref[:, lanes].astype(BF16), cre_ref[j])
                     - _dot(xi_ref[:, lanes].astype(BF16), cim_ref[j]))
    y = jnp.concatenate(parts, axis=1) + d_ref[...] * u
    g = _gelu_tanh(y)
    y_ref[...] = (g * _sigmoid(_dot(g.astype(BF16), gw_ref[...]) + gb_ref[...])).astype(BF16)


def _s5_tables(lam_re, lam_im, b_re, b_im, c_re, c_im, log_dt):
    lr, li = lam_re.astype(F32), lam_im.astype(F32)
    dt = jnp.exp(log_dt.astype(F32))[:, None]
    mag = jnp.exp(lr * dt)
    abar_re = mag * jnp.cos(li * dt)
    abar_im = mag * jnp.sin(li * dt)
    den = lr * lr + li * li
    nr = abar_re - 1.0
    coef_re = (nr * lr + abar_im * li) / den
    coef_im = (abar_im * lr - nr * li) / den
    bbar_re = coef_re[..., None] * b_re - coef_im[..., None] * b_im
    bbar_im = coef_re[..., None] * b_im + coef_im[..., None] * b_re
    gpb = S5_GROUPS // S5_BLOCKS
    eye = jnp.eye(gpb, dtype=F32)

    def in_map(bb):
        bb = bb.reshape(S5_BLOCKS, gpb, S5_STATE, S5_GROUP)
        return jnp.einsum("jgph,gk->jghkp", bb, eye).reshape(S5_BLOCKS, gpb * S5_GROUP, gpb * S5_STATE)

    def out_map(cc):
        cc = cc.reshape(S5_BLOCKS, gpb, S5_GROUP, S5_STATE)
        return jnp.einsum("jghp,gk->jgpkh", cc, eye).reshape(S5_BLOCKS, gpb * S5_STATE, gpb * S5_GROUP)

    def power(n):
        n = jnp.asarray(n, F32)[..., None, None]
        pmag = jnp.exp(n * (lr * dt))
        shape = n.shape[:-2] + (S5_LANES,)
        return (pmag * jnp.cos(n * (li * dt))).reshape(shape), (pmag * jnp.sin(n * (li * dt))).reshape(shape)

    row = jnp.arange(SUBLANES)
    step = 2 ** jnp.arange(S5_LOG_STEPS)
    s_re, s_im = power(step)
    keep = (row[None, :] >= step[:, None])[..., None]
    m_re = jnp.where(keep, s_re[:, None, :], 0.0)
    m_im = jnp.where(keep, s_im[:, None, :], 0.0)
    p_re, p_im = power(row + 1)
    return (in_map(bbar_re).astype(BF16), in_map(bbar_im).astype(BF16),
            out_map(c_re.astype(F32)).astype(BF16), out_map(c_im.astype(F32)).astype(BF16),
            m_re, m_im, p_re, p_im)


def _s5(proj, tables, d_skip, glu_w, glu_b, batch, seq):
    t = batch * seq
    ts = S5_TILE
    nj = seq // ts
    bre, bim, cre, cim, m_re, m_im, p_re, p_im = tables
    blk_c = MIX_HALF // S5_BLOCKS
    blk_s = S5_LANES // S5_BLOCKS
    const2 = lambda shape: pl.BlockSpec(shape, lambda b, j: (0, 0))
    const3 = lambda shape: pl.BlockSpec(shape, lambda b, j: (0, 0, 0))
    return pl.pallas_call(
        functools.partial(_s5_kernel, tile=ts),
        grid=(batch, nj),
        in_specs=[pl.BlockSpec((ts, MIX_HALF), lambda b, j: (b * nj + j, 0)),
                  const3((S5_BLOCKS, blk_c, blk_s)), const3((S5_BLOCKS, blk_c, blk_s)),
                  const3((S5_BLOCKS, blk_s, blk_c)), const3((S5_BLOCKS, blk_s, blk_c)),
                  const3(m_re.shape), const3(m_im.shape), const2(p_re.shape), const2(p_im.shape),
                  const2((1, MIX_HALF)), const2((MIX_HALF, MIX_HALF)), const2((1, MIX_HALF))],
        out_specs=pl.BlockSpec((ts, MIX_HALF), lambda b, j: (b * nj + j, 0)),
        out_shape=jax.ShapeDtypeStruct((t, MIX_HALF), BF16),
        scratch_shapes=[pltpu.VMEM((ts, S5_LANES), F32), pltpu.VMEM((ts, S5_LANES), F32),
                        pltpu.VMEM((8, S5_LANES), F32), pltpu.VMEM((8, S5_LANES), F32)],
        compiler_params=_params("arbitrary", "arbitrary"),
        name="s5",
    )(proj, bre, bim, cre, cim, m_re, m_im, p_re, p_im, d_skip.reshape(1, MIX_HALF), glu_w.astype(BF16),
      glu_b.reshape(1, MIX_HALF))


def _gla_kernel(q_ref, k_ref, v_ref, r_ref, gl_ref, gw_ref, gb_ref, nw_ref, y_ref,
                st_ref, qd_ref, ki_ref, ke_ref, v16_ref, dec_ref, o_ref, *, tile, chunk):
    L = chunk
    nc = tile // L

    @pl.when(pl.program_id(1) == 0)
    def _():
        st_ref[...] = jnp.zeros_like(st_ref)

    z = _dot(gl_ref[...].astype(BF16), gw_ref[...]) + gb_ref[...]
    bcum = _log_sigmoid(z) * (1.0 / GLA_GATE_TEMP)
    row_in_chunk = lax.broadcasted_iota(jnp.int32, bcum.shape, 0) & (L - 1)
    s = 1
    while s < L:
        bcum = bcum + jnp.where(row_in_chunk >= s, pltpu.roll(bcum, s, 0), 0.0)
        s *= 2
    b3 = bcum.reshape(nc, L, MIX_HALF)
    b_last = b3[:, L - 1:L, :]
    k = k_ref[...]
    qd_ref[...] = (q_ref[...] * (GLA_DK ** -0.5) * jnp.exp(bcum)).astype(BF16)
    ki_ref[...] = (k * jnp.exp(-bcum)).astype(BF16)
    ke_ref[...] = (k.reshape(nc, L, MIX_HALF) * jnp.exp(b_last - b3)).reshape(tile, MIX_HALF).astype(BF16)
    v16_ref[...] = v_ref[...].astype(BF16)
    dec_ref[...] = jnp.exp(b_last)

    ri = lax.broadcasted_iota(jnp.int32, (L, L), 0)
    ci = lax.broadcasted_iota(jnp.int32, (L, L), 1)
    causal = ci <= ri

    def body(c, carry):
        r0 = pl.multiple_of(c * L, L)
        dec = dec_ref[c]
        for h in range(HEADS):
            lo = h * HEAD_DIM
            q_dec = qd_ref[pl.ds(r0, L), lo:lo + HEAD_DIM]
            v = v16_ref[pl.ds(r0, L), lo:lo + HEAD_DIM]
            st = st_ref[h]
            att = jnp.where(causal, _dot_nt(q_dec, ki_ref[pl.ds(r0, L), lo:lo + HEAD_DIM]), 0.0)
            o_ref[pl.ds(r0, L), lo:lo + HEAD_DIM] = (_dot(att.astype(BF16), v)
                                                     + _dot_nt(q_dec, st.astype(BF16)))
            st_ref[h] = dec[:, lo:lo + HEAD_DIM] * st + _dot_tn(v, ke_ref[pl.ds(r0, L), lo:lo + HEAD_DIM])
        return carry

    lax.fori_loop(0, nc, body, 0, unroll=GLA_UNROLL)

    rg = r_ref[...]
    gate = nw_ref[...] * (rg * _sigmoid(rg))
    for h in range(HEADS):
        lo = h * HEAD_DIM
        o = o_ref[:, lo:lo + HEAD_DIM]
        yn = o * lax.rsqrt(jnp.mean(o * o, axis=-1, keepdims=True) + EPS)
        y_ref[:, lo:lo + HEAD_DIM] = (yn * gate[:, lo:lo + HEAD_DIM]).astype(BF16)


def _gla(proj, glow, gate_w, gate_b, norm_w, batch, seq):
    t = batch * seq
    ts = GLA_TILE
    nj = seq // ts
    blk = lambda col: pl.BlockSpec((ts, MIX_HALF), lambda b, j, col=col: (b * nj + j, col))
    const2 = lambda shape: pl.BlockSpec(shape, lambda b, j: (0, 0))
    return pl.pallas_call(
        functools.partial(_gla_kernel, tile=ts, chunk=GLA_CHUNK),
        grid=(batch, nj),
        in_specs=[blk(1), blk(2), blk(3), blk(4),
                  pl.BlockSpec((ts, LANES), lambda b, j: (b * nj + j, 0)),
                  const2((LANES, MIX_HALF)), const2((1, MIX_HALF)), const2((1, MIX_HALF))],
        out_specs=pl.BlockSpec((ts, MIX_HALF), lambda b, j: (b * nj + j, 0)),
        out_shape=jax.ShapeDtypeStruct((t, MIX_HALF), BF16),
        scratch_shapes=[pltpu.VMEM((HEADS, HEAD_DIM, HEAD_DIM), F32),
                        pltpu.VMEM((ts, MIX_HALF), BF16), pltpu.VMEM((ts, MIX_HALF), BF16),
                        pltpu.VMEM((ts, MIX_HALF), BF16), pltpu.VMEM((ts, MIX_HALF), BF16),
                        pltpu.VMEM((ts // GLA_CHUNK, 1, MIX_HALF), F32),
                        pltpu.VMEM((ts, MIX_HALF), F32)],
        compiler_params=_params("arbitrary", "arbitrary"),
        name="gla",
    )(proj, proj, proj, proj, glow, gate_w, gate_b, norm_w.reshape(1, MIX_HALF))


def _pad_heads(w, axis):
    shape = list(w.shape)
    shape[axis:axis + 1] = [HEADS, GLA_DK]
    w = w.reshape(shape)
    pad = [(0, 0)] * w.ndim
    pad[axis + 1] = (0, HEAD_DIM - GLA_DK)
    w = jnp.pad(w, pad)
    shape[axis:axis + 2] = [HEADS * HEAD_DIM]
    return w.reshape(shape)


def _pack_bf16_pairs(z):
    hi = lax.bitcast_convert_type(z[:, :PACKED].astype(BF16).astype(F32), jnp.uint32)
    lo = lax.bitcast_convert_type(z[:, PACKED:].astype(BF16).astype(F32), jnp.uint32)
    word = (hi & jnp.uint32(0xFFFF0000)) | lax.shift_right_logical(lo, jnp.uint32(16))
    return lax.bitcast_convert_type(word, jnp.int32)


def _unpack_bf16_pairs(p):
    word = lax.bitcast_convert_type(p, jnp.uint32)
    hi = lax.bitcast_convert_type(word & jnp.uint32(0xFFFF0000), F32)
    lo = lax.bitcast_convert_type(lax.shift_left(word, jnp.uint32(16)), F32)
    return hi, lo


def _out_kernel(ya_ref, yb_ref, h_ref, w_ref, lw_ref, lb_ref, rw_ref, rb_ref,
                o_ref, opk_ref, idx_ref, rank_ref, wk_ref, cnt_ref, base_ref, *, tile):
    mixed = jnp.concatenate([ya_ref[...], yb_ref[...]], axis=1)
    z = ALPHA * h_ref[...] + _dot(mixed, w_ref[...])
    out = _layer_norm(z, lw_ref[...], lb_ref[...])
    o_ref[...] = out
    opk_ref[...] = _pack_bf16_pairs(out)
    _route_tile(out, rw_ref, rb_ref, idx_ref, rank_ref, wk_ref, cnt_ref, base_ref, tile)


def _out_proj_ln_route(ya, yb, h, w_out, ln_w, ln_b, router_w, router_bias):
    t = h.shape[0]
    tm = OUT_TILE
    const = lambda shape: pl.BlockSpec(shape, lambda i: (0, 0))
    per_tok = lambda dt: jax.ShapeDtypeStruct((TOP_K, t), dt)
    tok_blk = pl.BlockSpec((TOP_K, tm), lambda i: (0, i))
    return pl.pallas_call(
        functools.partial(_out_kernel, tile=tm),
        grid=(t // tm,),
        in_specs=[pl.BlockSpec((tm, MIX_HALF), lambda i: (i, 0)),
                  pl.BlockSpec((tm, MIX_HALF), lambda i: (i, 0)),
                  pl.BlockSpec((tm, D_MODEL), lambda i: (i, 0)),
                  const((D_MODEL, D_MODEL)), const((1, D_MODEL)), const((1, D_MODEL)),
                  const((N_EXPERTS, D_MODEL)), const((N_EXPERTS, 1))],
        out_specs=[pl.BlockSpec((tm, D_MODEL), lambda i: (i, 0)),
                   pl.BlockSpec((tm, PACKED), lambda i: (i, 0)),
                   tok_blk, tok_blk, pl.BlockSpec((tm, TOP_K), lambda i: (i, 0)), const((N_EXPERTS, LANES))],
        out_shape=[jax.ShapeDtypeStruct((t, D_MODEL), F32), jax.ShapeDtypeStruct((t, PACKED), jnp.int32),
                   per_tok(jnp.int32), per_tok(jnp.int32), jax.ShapeDtypeStruct((t, TOP_K), F32),
                   jax.ShapeDtypeStruct((N_EXPERTS, LANES), F32)],
        scratch_shapes=[pltpu.VMEM((N_EXPERTS, LANES), F32)],
        compiler_params=_params("arbitrary"),
        name="out_proj_ln_route",
    )(ya, yb, h, w_out.astype(BF16), ln_w.reshape(1, D_MODEL), ln_b.reshape(1, D_MODEL),
      router_w.T, router_bias.reshape(N_EXPERTS, 1))


def _first_index(hit, idx, big):
    return jnp.min(jnp.where(hit, idx, big), axis=0, keepdims=True)


def _route_tile(h, w_ref, b_ref, idx_ref, rank_ref, wk_ref, cnt_ref, base_ref, tile):
    @pl.when(pl.program_id(0) == 0)
    def _():
        base_ref[...] = jnp.zeros_like(base_ref)

    h_hi, h_mid, _ = _split3(h)
    w_hi, w_mid, _ = _split3(w_ref[...])
    logits = _dot_nt(w_hi, h_hi) + _dot_nt(w_hi, h_mid) + _dot_nt(w_mid, h_hi)
    scores = _sigmoid(logits)
    biased = scores + b_ref[...]

    sub = lax.broadcasted_iota(jnp.int32, (GROUP_SIZE, tile), 0)
    grp_rows = []
    for g in range(N_GROUPS):
        xg = biased[g * GROUP_SIZE:(g + 1) * GROUP_SIZE, :]
        m1 = jnp.max(xg, axis=0, keepdims=True)
        i1 = _first_index(xg == m1, sub, GROUP_SIZE)
        m2 = jnp.max(jnp.where(sub == i1, NEG_INF, xg), axis=0, keepdims=True)
        grp_rows.append(m1 + m2)
    gs = jnp.concatenate(grp_rows, axis=0)
    gsel = jnp.zeros((N_GROUPS, tile), F32)
    for _ in range(TOPK_GROUPS):
        mx = jnp.max(gs, axis=0, keepdims=True)
        hit = sub == _first_index(gs == mx, sub, N_GROUPS)
        gsel = jnp.where(hit, 1.0, gsel)
        gs = jnp.where(hit, NEG_INF, gs)
    emask = jnp.concatenate(
        [jnp.broadcast_to(gsel[g:g + 1, :], (GROUP_SIZE, tile)) for g in range(N_GROUPS)], axis=0)

    eidx = lax.broadcasted_iota(jnp.int32, (N_EXPERTS, tile), 0)
    cand = jnp.where(emask > 0.5, biased, NEG_INF)
    sel = jnp.zeros((N_EXPERTS, tile), F32)
    picks = []
    for _ in range(TOP_K):
        mx = jnp.max(cand, axis=0, keepdims=True)
        first = _first_index(cand == mx, eidx, N_EXPERTS)
        hit = eidx == first
        picks.append(first)
        sel = jnp.where(hit, 1.0, sel)
        cand = jnp.where(hit, NEG_INF, cand)
    picked = jnp.where(sel > 0.5, scores, 0.0)
    wts = picked / jnp.sum(picked, axis=0, keepdims=True) * ROUTED_SCALE

    ri = lax.broadcasted_iota(jnp.int32, (tile, tile), 0)
    ci = lax.broadcasted_iota(jnp.int32, (tile, tile), 1)
    before = (ri < ci).astype(BF16)
    prior = _dot(sel.astype(BF16), before) + base_ref[:, 0:1]
    ranks = [jnp.sum(jnp.where(eidx == p, prior, 0.0), axis=0, keepdims=True) for p in picks]
    wsel = [jnp.sum(jnp.where(eidx == p, wts, 0.0), axis=0, keepdims=True) for p in picks]
    idx_ref[...] = jnp.concatenate(picks, axis=0)
    rank_ref[...] = jnp.concatenate(ranks, axis=0).astype(jnp.int32)
    wk_ref[...] = jnp.concatenate(wsel, axis=0).T
    total = base_ref[...] + jnp.sum(sel, axis=1, keepdims=True)
    base_ref[...] = total
    cnt_ref[...] = total


def _silu(x):
    return x * _sigmoid(x)


def _sc_mesh():
    return plsc.VectorSubcoreMesh(core_axis_name="c", subcore_axis_name="s")


def _sc_worker_id():
    return lax.axis_index("s") * SC_CORES + lax.axis_index("c")


def _dispatch_rows(xpk, pos_chunks, n_rows):
    t = xpk.shape[0]
    n_ch = t // SC_WORKERS // SC_CHUNK

    @functools.partial(
        pl.kernel, mesh=_sc_mesh(),
        out_type=jax.ShapeDtypeStruct((n_rows, PACKED), jnp.int32),
        scratch_types=[pltpu.VMEM((TOP_K, SC_CHUNK), jnp.int32),
                       pltpu.VMEM((SC_CHUNK, PACKED), jnp.int32),
                       pltpu.SemaphoreType.DMA],
        name="moe_dispatch",
    )
    def scatter(x_hbm, pos_hbm, out_hbm, idx_v, rows_v, sem):
        wid = _sc_worker_id()

        @pl.loop(0, n_ch)
        def _(c):
            chunk = wid * n_ch + c
            off = pl.multiple_of(chunk * SC_CHUNK, SC_CHUNK)
            pltpu.sync_copy(pos_hbm.at[chunk], idx_v)
            pltpu.sync_copy(x_hbm.at[pl.ds(off, SC_CHUNK)], rows_v)
            copies = [pltpu.async_copy(rows_v, out_hbm.at[idx_v.at[k]], sem) for k in range(TOP_K)]
            for cp in copies:
                cp.wait()

    return scatter(xpk, pos_chunks)


def _gather_rows(table, idx):
    n = idx.shape[0]
    per_w = n // SC_WORKERS
    n_ch = per_w // SC_CHUNK
    assert n_ch % 2 == 0 and n_ch >= 2

    @functools.partial(
        pl.kernel, mesh=_sc_mesh(),
        out_type=jax.ShapeDtypeStruct((n, PACKED), jnp.int32),
        scratch_types=[pltpu.VMEM((n_ch, SC_CHUNK), jnp.int32),
                       pltpu.VMEM((SC_CHUNK, PACKED), jnp.int32), pltpu.VMEM((SC_CHUNK, PACKED), jnp.int32),
                       pltpu.SemaphoreType.DMA, pltpu.SemaphoreType.DMA,
                       pltpu.SemaphoreType.DMA, pltpu.SemaphoreType.DMA],
        name="moe_gather",
    )
    def gather(table_hbm, idx_hbm, out_hbm, idx_v, rows0, rows1, g0, g1, w0, w1):
        wid = _sc_worker_id()
        base = wid * per_w
        rows, g_sem, w_sem = (rows0, rows1), (g0, g1), (w0, w1)
        pltpu.sync_copy(idx_hbm.at[wid], idx_v)

        def fetch(c, b):
            return pltpu.make_async_copy(table_hbm.at[idx_v.at[c]], rows[b], g_sem[b])

        def flush(c, b):
            off = pl.multiple_of(base + c * SC_CHUNK, SC_CHUNK)
            return pltpu.make_async_copy(rows[b], out_hbm.at[pl.ds(off, SC_CHUNK)], w_sem[b])

        fetch(0, 0).start()

        @pl.loop(0, n_ch, step=2)
        def _(c0):
            for b in range(2):
                c = c0 + b
                fetch(c, b).wait()
                flush(c, b).start()

                @pl.when(c + 1 < n_ch)
                def _():
                    @pl.when(c >= 1)
                    def _():
                        flush(c - 1, 1 - b).wait()
                    fetch(c + 1, 1 - b).start()

        flush(n_ch - 2, 0).wait()
        flush(n_ch - 1, 1).wait()

    return gather(table, idx.reshape(SC_WORKERS, n_ch, SC_CHUNK))


def _unpacked_bf16(p):
    hi, lo = _unpack_bf16_pairs(p)
    return jnp.concatenate([hi.astype(BF16), lo.astype(BF16)], axis=1)


def _expert_kernel(be_ref, nu_ref, next_ref, slot_ref, xs_hbm, wg_hbm, wu_hbm, wd_hbm, y_ref,
                   g16_ref, u16_ref, d16_ref, gf_ref, uf_ref, df_ref, xbuf_ref, xsem, wsem, *, layer):
    i = pl.program_id(0)
    n_used = nu_ref[0]
    bm = xbuf_ref.shape[1]

    def fetch(b):
        slot = lax.rem(b, XS_SLOTS)
        rows = pl.ds(pl.multiple_of(b * bm, bm), bm)
        return pltpu.make_async_copy(xs_hbm.at[rows], xbuf_ref.at[slot], xsem.at[slot])

    @pl.when(i == 0)
    def _():
        fetch(0).start()

        @pl.when(n_used > 1)
        def _():
            fetch(1).start()

    @pl.when(i + 2 < n_used)
    def _():
        fetch(i + 2).start()

    e = be_ref[i]
    slot = slot_ref[e]
    f32_bufs = (gf_ref, uf_ref, df_ref)

    def wfetch(expert, dst_slot):
        return [pltpu.make_async_copy(w_hbm.at[layer, expert], buf.at[dst_slot], wsem.at[dst_slot, j])
                for j, (w_hbm, buf) in enumerate(zip((wg_hbm, wu_hbm, wd_hbm), f32_bufs))]

    @pl.when(i == 0)
    def _():
        for cp in wfetch(e, slot):
            cp.start()

    first_block_of_expert = jnp.logical_or(i == 0, e != be_ref[jnp.maximum(i - 1, 0)])

    @pl.when(jnp.logical_and(first_block_of_expert, i < n_used))
    def _():
        for cp in wfetch(e, slot):
            cp.wait()
        g16_ref[slot] = gf_ref[slot].astype(BF16)
        u16_ref[slot] = uf_ref[slot].astype(BF16)
        d16_ref[slot] = df_ref[slot].astype(BF16)
        nxt = next_ref[e]

        @pl.when(nxt >= 0)
        def _():
            for cp in wfetch(nxt, 1 - slot):
                cp.start()

    @pl.when(i < n_used)
    def _():
        fetch(i).wait()
        x = _unpacked_bf16(xbuf_ref[lax.rem(i, XS_SLOTS)])
        hh = _silu(_dot(x, g16_ref[slot])) * _dot(x, u16_ref[slot])
        y_ref[...] = _pack_bf16_pairs(_dot(hh.astype(BF16), d16_ref[slot]))


def _experts(block_e, n_used, next_expert, weight_slot, xs, wg, wu, wd, layer):
    nb = block_e.shape[0]
    bm = MOE_BLOCK
    hbm = pl.BlockSpec(memory_space=pl.ANY)
    two = lambda shape, dt: pltpu.VMEM((2,) + shape, dt)
    grid_spec = pltpu.PrefetchScalarGridSpec(
        num_scalar_prefetch=4,
        grid=(nb,),
        in_specs=[hbm, hbm, hbm, hbm],
        out_specs=pl.BlockSpec((bm, PACKED), lambda i, be, nu, nx, sl: (jnp.minimum(i, nu[0] - 1), 0)),
        scratch_shapes=[two((D_MODEL, D_EXPERT), BF16), two((D_MODEL, D_EXPERT), BF16), two((D_EXPERT, D_MODEL), BF16),
                        two((D_MODEL, D_EXPERT), F32), two((D_MODEL, D_EXPERT), F32), two((D_EXPERT, D_MODEL), F32),
                        pltpu.VMEM((XS_SLOTS, bm, PACKED), jnp.int32), pltpu.SemaphoreType.DMA((XS_SLOTS,)),
                        pltpu.SemaphoreType.DMA((2, 3))],
    )
    return pl.pallas_call(
        functools.partial(_expert_kernel, layer=layer),
        grid_spec=grid_spec,
        out_shape=jax.ShapeDtypeStruct((nb * bm, PACKED), jnp.int32),
        compiler_params=_params("arbitrary"),
        name="moe_experts",
    )(block_e, n_used, next_expert, weight_slot, xs, wg, wu, wd)


def _combine_kernel(g_ref, wk_ref, h_ref, xpk_ref, sg_ref, su_ref, sd_ref, lw_ref, lb_ref, *rest, with_proj):
    if with_proj:
        (wm_ref, ws_ref), (o_ref, proj_ref, small_ref) = rest[:2], rest[-3:]
        next_proj = (wm_ref, ws_ref, proj_ref, small_ref)
    else:
        o_ref, next_proj = rest[-1], None
    x = _unpacked_bf16(xpk_ref[...])
    hs = _silu(_dot(x, sg_ref[...])) * _dot(x, su_ref[...])
    shared = _dot(hs.astype(BF16), sd_ref[...])
    acc_hi = shared[:, :PACKED]
    acc_lo = shared[:, PACKED:]
    wk = wk_ref[...]
    for k in range(TOP_K):
        y_hi, y_lo = _unpack_bf16_pairs(g_ref[k])
        w = wk[:, k:k + 1]
        acc_hi = acc_hi + w * y_hi
        acc_lo = acc_lo + w * y_lo
    ffn = jnp.concatenate([acc_hi, acc_lo], axis=1)
    out = _layer_norm(ALPHA * h_ref[...] + ffn, lw_ref[...], lb_ref[...])
    o_ref[...] = out
    if next_proj is not None:
        wm_ref, ws_ref, proj_ref, small_ref = next_proj
        out16 = out.astype(BF16)
        proj_ref[...] = _dot(out16, wm_ref[...])
        small_ref[...] = _dot(out16, ws_ref[...])


def _combine_ln(g, wk, h, xpk, sg, su, sd, ln_w, ln_b, next_w, part, prev):
    t = h.shape[0]
    tm = COMBINE_TILE if next_w is not None else 2 * COMBINE_TILE
    n_blk = g.shape[1] // tm
    first = part // tm
    const = lambda shape: pl.BlockSpec(shape, lambda i: (0, 0))
    rows = lambda width: pl.BlockSpec((tm, width), lambda i: (i + first, 0))
    in_specs = [pl.BlockSpec((TOP_K, tm, PACKED), lambda i: (0, i, 0)), rows(TOP_K), rows(D_MODEL), rows(PACKED),
                const((D_MODEL, D_EXPERT)), const((D_MODEL, D_EXPERT)), const((D_EXPERT, D_MODEL)),
                const((1, D_MODEL)), const((1, D_MODEL))]
    args = [g, wk, h, xpk, sg.astype(BF16), su.astype(BF16), sd.astype(BF16),
            ln_w.reshape(1, D_MODEL), ln_b.reshape(1, D_MODEL)]
    out_specs = [rows(D_MODEL)]
    out_shape = [jax.ShapeDtypeStruct((t, D_MODEL), F32)]
    if next_w is not None:
        w_main, w_small = next_w
        n = w_main.shape[1]
        in_specs += [const((D_MODEL, n)), const((D_MODEL, LANES))]
        args += [w_main, w_small]
        out_specs += [rows(n), rows(LANES)]
        out_shape += [jax.ShapeDtypeStruct((t, n), F32), jax.ShapeDtypeStruct((t, LANES), F32)]
    aliases = {}
    if prev is not None:
        aliases = {len(args) + k: k for k in range(len(prev))}
        in_specs += [pl.BlockSpec(memory_space=pl.ANY)] * len(prev)
        args += list(prev)
    return pl.pallas_call(
        functools.partial(_combine_kernel, with_proj=next_w is not None),
        grid=(n_blk,),
        in_specs=in_specs,
        out_specs=out_specs,
        out_shape=out_shape,
        input_output_aliases=aliases,
        compiler_params=_params("parallel"),
        name="moe_combine_ln",
    )(*args)


def _moe_ln(h, hpk, idx, rank, wk, counts, wg, wu, wd, layer, sg, su, sd, ln_w, ln_b, next_w):
    t = h.shape[0]
    cnt = counts[:, 0].astype(jnp.int32)
    padded = (cnt + MOE_BLOCK - 1) // MOE_BLOCK * MOE_BLOCK
    pend = jnp.cumsum(padded)
    experts = jnp.arange(N_EXPERTS, dtype=jnp.int32)
    pstart_of_pick = jnp.sum(jnp.where(idx[:, :, None] == experts, pend - padded, 0), axis=-1)
    pos = pstart_of_pick + rank
    nb = -(-(t * TOP_K + N_EXPERTS * (MOE_BLOCK - 1)) // MOE_BLOCK)
    starts = jnp.arange(nb, dtype=jnp.int32) * MOE_BLOCK
    block_e = jnp.minimum(jnp.sum((pend[None, :] <= starts[:, None]).astype(jnp.int32), axis=1), N_EXPERTS - 1)
    n_used = (pend[-1] // MOE_BLOCK).astype(jnp.int32).reshape(1)
    has_rows = cnt > 0
    later = jnp.logical_and(has_rows[None, :], experts[None, :] > experts[:, None])
    next_expert = jnp.min(jnp.where(later, experts[None, :], N_EXPERTS), axis=1)
    next_expert = jnp.where(next_expert == N_EXPERTS, -1, next_expert).astype(jnp.int32)
    weight_slot = ((jnp.cumsum(has_rows) - has_rows) % 2).astype(jnp.int32)
    pos_chunks = pos.reshape(TOP_K, t // SC_CHUNK, SC_CHUNK).transpose(1, 0, 2)
    xs = _dispatch_rows(hpk, pos_chunks, nb * MOE_BLOCK)
    ys = _experts(block_e, n_used, next_expert, weight_slot, xs, wg, wu, wd, layer)
    bounds = [t * c // COMBINE_SPLIT[-1] for c in COMBINE_SPLIT]
    spans = list(zip(bounds[:-1], bounds[1:]))
    gathered = [_gather_rows(ys, pos[:, lo:hi].reshape(-1)).reshape(TOP_K, hi - lo, PACKED) for lo, hi in spans]
    outs = None
    for (lo, _), g in zip(spans, gathered):
        outs = _combine_ln(g, wk, h, hpk, sg, su, sd, ln_w, ln_b, next_w, lo, outs)
    return outs


def _pad_cols(w, width=LANES):
    return jnp.pad(w, ((0, 0), (0, width - w.shape[1])))


def _even_proj_weights(w_in):
    a4 = 4 * MIX_HALF
    ng = 2 * HEADS
    w_main = jnp.concatenate([w_in[:, :a4], w_in[:, a4 + ng:]], axis=1).astype(BF16)
    w_gate = _pad_cols(w_in[:, a4:a4 + ng]).astype(BF16)
    return w_main, w_gate


def _even_mixer(proj, gates, batch, seq, gate_b, norm_w, conv_w, conv_b, wa, ba, wx, bx, lam):
    ya = _mlstm(proj, gates, gate_b, norm_w, batch, seq)
    yb = _rglru(proj, conv_w, conv_b, wa, ba, wx, bx, lam, batch, seq)
    return ya, yb


def _odd_proj_weights(w_in):
    c0 = MIX_HALF
    c1 = c0 + HEADS * GLA_DK
    c2 = c1 + HEADS * GLA_DK
    c3 = c2 + MIX_HALF
    c4 = c3 + MIX_HALF
    w_main = jnp.concatenate([w_in[:, :c0], _pad_heads(w_in[:, c0:c1], 1), _pad_heads(w_in[:, c1:c2], 1),
                              w_in[:, c2:c4]], axis=1).astype(BF16)
    w_low = _pad_cols(w_in[:, c4:]).astype(BF16)
    return w_main, w_low


def _odd_mixer(proj, glow, batch, seq, lam_re, lam_im, b_re, b_im, c_re, c_im, d_skip, log_dt,
               glu_w, glu_b, gate_w, gate_b, norm_w):
    tables = _s5_tables(lam_re, lam_im, b_re, b_im, c_re, c_im, log_dt)
    yc = _s5(proj, tables, d_skip, glu_w, glu_b, batch, seq)
    gw = jnp.pad(_pad_heads(gate_w, 1), ((0, LANES - GLA_GATE_RANK), (0, 0))).astype(BF16)
    gb = _pad_heads(gate_b.reshape(1, -1), 1)
    yd = _gla(proj, glow, gw, gb, norm_w, batch, seq)
    return yc, yd


def kernel(x, ln1_w, ln1_b, ln2_w, ln2_b, w_out, w_in_even, mlstm_gate_b, mlstm_norm_w, lru_conv_w, lru_conv_b, lru_wa, lru_ba, lru_wx, lru_bx, lru_lambda, w_in_odd, s5_lam_re, s5_lam_im, s5_b_re, s5_b_im, s5_c_re, s5_c_im, s5_d, s5_log_dt, s5_glu_w, s5_glu_b, gla_gate_w, gla_gate_b, gla_norm_w, router_w, router_bias, exp_w_gate, exp_w_up, exp_w_down, sh_w_gate, sh_w_up, sh_w_down):
    batch, seq, d = x.shape
    proj_w = [_even_proj_weights(w_in_even[layer // 2]) if layer % 2 == 0 else _odd_proj_weights(w_in_odd[layer // 2])
              for layer in range(DEPTH)]
    h = x.reshape(batch * seq, d)
    proj, small = _proj(h, *proj_w[0])
    for layer in range(DEPTH):
        j = layer // 2
        if layer % 2 == 0:
            y1, y2 = _even_mixer(proj, small, batch, seq, mlstm_gate_b[j], mlstm_norm_w[j],
                                 lru_conv_w[j], lru_conv_b[j], lru_wa[j], lru_ba[j], lru_wx[j],
                                 lru_bx[j], lru_lambda[j])
        else:
            y1, y2 = _odd_mixer(proj, small, batch, seq, s5_lam_re[j], s5_lam_im[j], s5_b_re[j],
                                s5_b_im[j], s5_c_re[j], s5_c_im[j], s5_d[j], s5_log_dt[j],
                                s5_glu_w[j], s5_glu_b[j], gla_gate_w[j], gla_gate_b[j], gla_norm_w[j])
        h, hpk, idx, rank, wk, counts = _out_proj_ln_route(y1, y2, h, w_out[layer], ln1_w[layer], ln1_b[layer],
                                                           router_w[layer], router_bias[layer])
        next_w = proj_w[layer + 1] if layer + 1 < DEPTH else None
        res = _moe_ln(h, hpk, idx, rank, wk, counts, exp_w_gate, exp_w_up, exp_w_down, layer,
                      sh_w_gate[layer], sh_w_up[layer], sh_w_down[layer], ln2_w[layer], ln2_b[layer], next_w)
        if next_w is None:
            (h,) = res
        else:
            h, proj, small = res
    return h.reshape(batch, seq, d)
```

```python
import functools
import math

import jax
import jax.numpy as jnp
from jax import lax
from jax.experimental import pallas as pl
from jax.experimental.pallas import tpu as pltpu
from jax.experimental.pallas import tpu_sc as plsc

F32 = jnp.float32
BF16 = jnp.bfloat16

D_MODEL = 1024
DEPTH = 2
MIX_HALF = 512
HEADS = 4
HEAD_DIM = 128
GLA_DK = 64
GLA_CHUNK = 64
GLA_GATE_RANK = 16
GLA_GATE_TEMP = 16.0
LRU_C = 8.0
LRU_CONV = 4
S5_GROUP = 16
S5_GROUPS = 32
S5_STATE = 64
S5_LANES = S5_GROUPS * S5_STATE
S5_BLOCKS = 4
N_EXPERTS = 64
N_GROUPS = 8
GROUP_SIZE = N_EXPERTS // N_GROUPS
TOP_K = 8
TOPK_GROUPS = 4
D_EXPERT = 256
ROUTED_SCALE = 2.5
ALPHA = (2.0 * DEPTH) ** 0.25
EPS = 1e-5
LANES = 128
SUBLANES = 8
NEG_INF = float("-inf")

VMEM_LIMIT = 56 * 1024 * 1024

MLSTM_CHUNK = 128
MLSTM_TILE = 1024
MLSTM_UNROLL = 2
LRU_TILE = 1024
LRU_LOG_STEPS = 3
LRU_UNROLL = 4
S5_TILE = 512
S5_LOG_STEPS = 3
S5_UNROLL = True
GLA_UNROLL = 8
GLA_TILE = 1024
PROJ_TILE = 1024
OUT_TILE = 1024
MOE_BLOCK = 1152
XS_SLOTS = 3
COMBINE_TILE = 256
COMBINE_PARTS = 2
PACKED = D_MODEL // 2
SC_CHUNK = 64
SC_CORES = 2
SC_SUBCORES = 16
SC_WORKERS = SC_CORES * SC_SUBCORES


def _params(*sem):
    return pltpu.CompilerParams(dimension_semantics=sem, vmem_limit_bytes=VMEM_LIMIT)


def _split3(x):
    hi = x.astype(BF16)
    r1 = x - hi.astype(F32)
    mid = r1.astype(BF16)
    lo = (r1 - mid.astype(F32)).astype(BF16)
    return hi, mid, lo


def _dot(a, b):
    return jnp.dot(a, b, preferred_element_type=F32)


def _dot_nt(a, b):
    return lax.dot_general(a, b, (((1,), (1,)), ((), ())), preferred_element_type=F32)


def _dot_tn(a, b):
    return lax.dot_general(a, b, (((0,), (0,)), ((), ())), preferred_element_type=F32)


def _exact_left01(mask01_bf16, x):
    hi, mid, lo = _split3(x)
    return _dot(mask01_bf16, hi) + _dot(mask01_bf16, mid) + _dot(mask01_bf16, lo)


def _exact_right01(x, mask01_bf16):
    hi, mid, lo = _split3(x)
    return _dot(hi, mask01_bf16) + _dot(mid, mask01_bf16) + _dot(lo, mask01_bf16)


def _log_sigmoid(x):
    return jnp.minimum(x, 0.0) - jnp.log(1.0 + jnp.exp(-jnp.abs(x)))


def _sigmoid(x):
    return 1.0 / (1.0 + jnp.exp(-x))


def _gelu_tanh(x):
    c = math.sqrt(2.0 / math.pi)
    return 0.5 * x * (1.0 + jnp.tanh(c * (x + 0.044715 * (x * x * x))))


def _layer_norm(z, w, b):
    mu = jnp.mean(z, axis=-1, keepdims=True)
    zc = z - mu
    return zc * lax.rsqrt(jnp.mean(zc * zc, axis=-1, keepdims=True) + EPS) * w + b


def _proj_kernel(x_ref, w_ref, wg_ref, o_ref, og_ref):
    x = x_ref[...].astype(BF16)
    o_ref[...] = _dot(x, w_ref[...])
    og_ref[...] = _dot(x, wg_ref[...])


def _proj(x, w_main, w_small):
    t, d = x.shape
    n = w_main.shape[1]
    tm = PROJ_TILE
    return pl.pallas_call(
        _proj_kernel,
        grid=(t // tm,),
        in_specs=[pl.BlockSpec((tm, d), lambda i: (i, 0)),
                  pl.BlockSpec((d, n), lambda i: (0, 0)),
                  pl.BlockSpec((d, LANES), lambda i: (0, 0))],
        out_specs=[pl.BlockSpec((tm, n), lambda i: (i, 0)),
                   pl.BlockSpec((tm, LANES), lambda i: (i, 0))],
        out_shape=[jax.ShapeDtypeStruct((t, n), F32), jax.ShapeDtypeStruct((t, LANES), F32)],
        compiler_params=_params("parallel"),
        name="in_proj",
    )(x, w_main, w_small)


def _mlstm_kernel(q_ref, k_ref, v_ref, o_ref, gc_ref, gr_ref, bc_ref, br_ref, nw_ref,
                  y_ref, c_ref, m_ref, *, chunk, n_chunks):
    L = chunk

    @pl.when(pl.program_id(1) == 0)
    def _():
        c_ref[...] = jnp.zeros_like(c_ref)
        m_ref[...] = jnp.zeros_like(m_ref)

    ri = lax.broadcasted_iota(jnp.int32, (L, L), 0)
    ci = lax.broadcasted_iota(jnp.int32, (L, L), 1)
    causal = ci <= ri
    tril = causal.astype(BF16)
    triu = (ri <= ci).astype(BF16)
    ones_v = jnp.ones((L, HEAD_DIM), BF16)
    scale = HEAD_DIM ** -0.5

    def body(c, carry):
        r0 = pl.multiple_of(c * L, L)
        g_col = gc_ref[pl.ds(r0, L), :] + bc_ref[...]
        g_row = gr_ref[c] + br_ref[...]
        b_col_all = _exact_left01(tril, _log_sigmoid(g_col))
        b_row_all = _exact_right01(_log_sigmoid(g_row), triu)
        for h in range(HEADS):
            lo = h * HEAD_DIM
            q = q_ref[pl.ds(r0, L), lo:lo + HEAD_DIM].astype(BF16)
            k = k_ref[pl.ds(r0, L), lo:lo + HEAD_DIM] * scale
            v = v_ref[pl.ds(r0, L), lo:lo + HEAD_DIM].astype(BF16)
            v_aug = jnp.concatenate([v, ones_v], axis=1)
            i_rep = jnp.broadcast_to(g_col[:, h:h + 1], (L, LANES))
            b_rep = jnp.broadcast_to(b_col_all[:, HEADS + h:HEADS + h + 1], (L, LANES))
            i_row = g_row[h:h + 1, :]
            b_row = b_row_all[HEADS + h:HEADS + h + 1, :]
            b_last = b_rep[L - 1:L, :]
            m_prev = m_ref[h:h + 1, :]
            c_prev = c_ref[h]

            d_mat = jnp.where(causal, b_rep - b_row + i_row, NEG_INF)
            m_inter = b_rep + m_prev
            m_i = jnp.maximum(m_inter, jnp.max(d_mat, axis=1, keepdims=True))
            s = _dot_nt(q, k.astype(BF16)) * jnp.exp(d_mat - m_i)
            w_inter = jnp.exp(m_inter - m_i)
            intra = _dot(s.astype(BF16), v_aug)
            inter = _dot(q, c_prev.astype(BF16))
            num = intra[:, :HEAD_DIM] + w_inter * inter[:, :HEAD_DIM]
            den = intra[:, HEAD_DIM:] + w_inter * inter[:, HEAD_DIM:]
            hh = num / jnp.maximum(jnp.abs(den), jnp.exp(-m_i))

            w_loc = b_last - b_rep + i_rep
            m_loc = jnp.max(w_loc, axis=0, keepdims=True)
            kp = (k * jnp.exp(w_loc - m_loc)).astype(BF16)
            c_loc = _dot_tn(kp, v_aug)
            m_new = jnp.maximum(b_last + m_prev, m_loc)
            keep = jnp.exp(b_last + m_prev - m_new)
            add = jnp.exp(m_loc - m_new)
            c_ref[h] = (jnp.concatenate([keep, keep], axis=1) * c_prev
                        + jnp.concatenate([add, add], axis=1) * c_loc)
            m_ref[h:h + 1, :] = m_new

            hc = hh - jnp.mean(hh, axis=-1, keepdims=True)
            yn = hc * lax.rsqrt(jnp.mean(hc * hc, axis=-1, keepdims=True) + EPS)
            og = o_ref[pl.ds(r0, L), lo:lo + HEAD_DIM]
            y_ref[pl.ds(r0, L), lo:lo + HEAD_DIM] = (yn * nw_ref[:, lo:lo + HEAD_DIM] * _sigmoid(og)).astype(BF16)
        return carry

    lax.fori_loop(0, n_chunks, body, 0, unroll=MLSTM_UNROLL)


def _mlstm(proj, gates, gate_b, norm_w, batch, seq):
    t = batch * seq
    L = MLSTM_CHUNK
    assert L == LANES, "the kernel keeps per-row gate terms replicated over one vreg of lanes"
    ts = MLSTM_TILE
    nj = seq // ts
    nc = ts // L
    g_row = gates[:, :2 * HEADS].reshape(t // L, L, 2 * HEADS).transpose(0, 2, 1)
    b_col = jnp.zeros((1, LANES), F32).at[0, :2 * HEADS].set(gate_b)
    b_row = gate_b.reshape(2 * HEADS, 1)
    blk = lambda col: pl.BlockSpec((ts, MIX_HALF), lambda b, j, col=col: (b * nj + j, col))
    kern = functools.partial(_mlstm_kernel, chunk=L, n_chunks=nc)
    return pl.pallas_call(
        kern,
        grid=(batch, nj),
        in_specs=[blk(0), blk(1), blk(2), blk(3),
                  pl.BlockSpec((ts, LANES), lambda b, j: (b * nj + j, 0)),
                  pl.BlockSpec((nc, 2 * HEADS, L), lambda b, j: (b * nj + j, 0, 0)),
                  pl.BlockSpec((1, LANES), lambda b, j: (0, 0)),
                  pl.BlockSpec((2 * HEADS, 1), lambda b, j: (0, 0)),
                  pl.BlockSpec((1, MIX_HALF), lambda b, j: (0, 0))],
        out_specs=pl.BlockSpec((ts, MIX_HALF), lambda b, j: (b * nj + j, 0)),
        out_shape=jax.ShapeDtypeStruct((t, MIX_HALF), BF16),
        scratch_shapes=[pltpu.VMEM((HEADS, HEAD_DIM, 2 * HEAD_DIM), F32),
                        pltpu.VMEM((8, LANES), F32)],
        compiler_params=_params("arbitrary", "arbitrary"),
        name="mlstm",
    )(proj, proj, proj, proj, gates, g_row, b_col, b_row, norm_w.reshape(1, MIX_HALF))


def _rglru_kernel(xb_ref, gb_ref, cw_ref, cb_ref, wa_ref, ba_ref, wx_ref, bx_ref, lam_ref,
                  y_ref, xext_ref, h_ref, a_ref, u_ref, *, tile):
    @pl.when(pl.program_id(1) == 0)
    def _():
        xext_ref[0:8, :] = jnp.zeros((8, MIX_HALF), F32)
        h_ref[...] = jnp.zeros_like(h_ref)

    x = xb_ref[...]
    xext_ref[8:8 + tile, :] = x
    xc = cb_ref[...] + cw_ref[LRU_CONV - 1:LRU_CONV, :] * x
    for tap in range(LRU_CONV - 1):
        back = LRU_CONV - 1 - tap
        xc = xc + cw_ref[tap:tap + 1, :] * xext_ref[8 - back:8 - back + tile, :]
    xext_ref[0:8, :] = x[tile - 8:tile, :]

    xc16 = xc.astype(BF16)
    r_parts, i_parts = [], []
    for h in range(HEADS):
        lo = h * HEAD_DIM
        xh = xc16[:, lo:lo + HEAD_DIM]
        r_parts.append(_dot(xh, wa_ref[h]))
        i_parts.append(_dot(xh, wx_ref[h]))
    r = _sigmoid(jnp.concatenate(r_parts, axis=1) + ba_ref[...])
    ig = _sigmoid(jnp.concatenate(i_parts, axis=1) + bx_ref[...])
    lam = lam_ref[...]
    softplus_neg = jnp.maximum(-lam, 0.0) + jnp.log(1.0 + jnp.exp(-jnp.abs(lam)))
    log_a = -LRU_C * r * softplus_neg
    a = jnp.exp(log_a)
    th = jnp.tanh(log_a)
    u = jnp.sqrt(-2.0 * th / (1.0 - th)) * ig * xc

    a_ref[...] = a
    u_ref[...] = u
    rows = lax.broadcasted_iota(jnp.int32, (SUBLANES, MIX_HALF), 0)

    def group(i, h_prev):
        r0 = pl.multiple_of(i * SUBLANES, SUBLANES)
        ag = a_ref[pl.ds(r0, SUBLANES), :]
        ug = u_ref[pl.ds(r0, SUBLANES), :]
        for k in range(LRU_LOG_STEPS):
            keep = rows >= (1 << k)
            ug = ag * jnp.where(keep, pltpu.roll(ug, 1 << k, 0), 0.0) + ug
            ag = ag * jnp.where(keep, pltpu.roll(ag, 1 << k, 0), 1.0)
        hg = ug + ag * h_prev
        u_ref[pl.ds(r0, SUBLANES), :] = hg
        return hg[SUBLANES - 1:SUBLANES, :]

    h_last = lax.fori_loop(0, tile // SUBLANES, group, h_ref[0:1, :], unroll=LRU_UNROLL)
    h_ref[...] = jnp.broadcast_to(h_last, h_ref.shape)
    y_ref[...] = (u_ref[...] * _gelu_tanh(gb_ref[...])).astype(BF16)


def _rglru(proj, conv_w, conv_b, wa, ba, wx, bx, lam, batch, seq):
    t = batch * seq
    ts = LRU_TILE
    nj = seq // ts
    row = lambda a: a.reshape(1, MIX_HALF)
    const2 = lambda shape: pl.BlockSpec(shape, lambda b, j: (0, 0))
    const3 = lambda shape: pl.BlockSpec(shape, lambda b, j: (0, 0, 0))
    blk = lambda col: pl.BlockSpec((ts, MIX_HALF), lambda b, j, col=col: (b * nj + j, col))
    return pl.pallas_call(
        functools.partial(_rglru_kernel, tile=ts),
        grid=(batch, nj),
        in_specs=[blk(4), blk(5), const2((LRU_CONV, MIX_HALF)), const2((1, MIX_HALF)),
                  const3((HEADS, HEAD_DIM, HEAD_DIM)), const2((1, MIX_HALF)),
                  const3((HEADS, HEAD_DIM, HEAD_DIM)), const2((1, MIX_HALF)), const2((1, MIX_HALF))],
        out_specs=pl.BlockSpec((ts, MIX_HALF), lambda b, j: (b * nj + j, 0)),
        out_shape=jax.ShapeDtypeStruct((t, MIX_HALF), BF16),
        scratch_shapes=[pltpu.VMEM((ts + 8, MIX_HALF), F32), pltpu.VMEM((8, MIX_HALF), F32),
                        pltpu.VMEM((ts, MIX_HALF), F32), pltpu.VMEM((ts, MIX_HALF), F32)],
        compiler_params=_params("arbitrary", "arbitrary"),
        name="rglru",
    )(proj, proj, conv_w, row(conv_b), wa.astype(BF16), row(ba), wx.astype(BF16), row(bx), row(lam))


def _s5_kernel(u_ref, bre_ref, bim_ref, cre_ref, cim_ref, mre_ref, mim_ref, pre_ref, pim_ref, d_ref, gw_ref,
               gb_ref, y_ref, xr_ref, xi_ref, cr_ref, ci_ref, *, tile):
    @pl.when(pl.program_id(1) == 0)
    def _():
        cr_ref[...] = jnp.zeros_like(cr_ref)
        ci_ref[...] = jnp.zeros_like(ci_ref)

    u = u_ref[...]
    u16 = u.astype(BF16)
    blk_c = MIX_HALF // S5_BLOCKS
    blk_s = S5_LANES // S5_BLOCKS
    parts = []
    for j in range(S5_BLOCKS):
        lanes = slice(j * blk_s, (j + 1) * blk_s)
        uj = u16[:, j * blk_c:(j + 1) * blk_c]
        xr_ref[:, lanes] = _dot(uj, bre_ref[j])
        xi_ref[:, lanes] = _dot(uj, bim_ref[j])

        def group(i, carry, lanes=lanes):
            cr, ci = carry
            r0 = pl.multiple_of(i * SUBLANES, SUBLANES)
            xr = xr_ref[pl.ds(r0, SUBLANES), lanes]
            xi = xi_ref[pl.ds(r0, SUBLANES), lanes]
            for k in range(S5_LOG_STEPS):
                sr = pltpu.roll(xr, 1 << k, 0)
                si = pltpu.roll(xi, 1 << k, 0)
                mr = mre_ref[k, :, lanes]
                mi = mim_ref[k, :, lanes]
                xr, xi = xr + mr * sr - mi * si, xi + mr * si + mi * sr
            pr = pre_ref[:, lanes]
            pi = pim_ref[:, lanes]
            xr, xi = xr + pr * cr - pi * ci, xi + pr * ci + pi * cr
            xr_ref[pl.ds(r0, SUBLANES), lanes] = xr
            xi_ref[pl.ds(r0, SUBLANES), lanes] = xi
            return xr[SUBLANES - 1:SUBLANES, :], xi[SUBLANES - 1:SUBLANES, :]

        cr, ci = lax.fori_loop(0, tile // SUBLANES, group, (cr_ref[0:1, lanes], ci_ref[0:1, lanes]),
                               unroll=S5_UNROLL)
        cr_ref[0:1, lanes] = cr
        ci_ref[0:1, lanes] = ci
        parts.append(_dot(xr_ref[:, lanes].astype(BF16), cre_ref[j])
                     - _dot(xi_ref[:, lanes].astype(BF16), cim_ref[j]))
    y = jnp.concatenate(parts, axis=1) + d_ref[...] * u
    g = _gelu_tanh(y)
    y_ref[...] = (g * _sigmoid(_dot(g.astype(BF16), gw_ref[...]) + gb_ref[...])).astype(BF16)


def _s5_tables(lam_re, lam_im, b_re, b_im, c_re, c_im, log_dt):
    lr, li = lam_re.astype(F32), lam_im.astype(F32)
    dt = jnp.exp(log_dt.astype(F32))[:, None]
    mag = jnp.exp(lr * dt)
    abar_re = mag * jnp.cos(li * dt)
    abar_im = mag * jnp.sin(li * dt)
    den = lr * lr + li * li
    nr = abar_re - 1.0
    coef_re = (nr * lr + abar_im * li) / den
    coef_im = (abar_im * lr - nr * li) / den
    bbar_re = coef_re[..., None] * b_re - coef_im[..., None] * b_im
    bbar_im = coef_re[..., None] * b_im + coef_im[..., None] * b_re
    gpb = S5_GROUPS // S5_BLOCKS
    eye = jnp.eye(gpb, dtype=F32)

    def in_map(bb):
        bb = bb.reshape(S5_BLOCKS, gpb, S5_STATE, S5_GROUP)
        return jnp.einsum("jgph,gk->jghkp", bb, eye).reshape(S5_BLOCKS, gpb * S5_GROUP, gpb * S5_STATE)

    def out_map(cc):
        cc = cc.reshape(S5_BLOCKS, gpb, S5_GROUP, S5_STATE)
        return jnp.einsum("jghp,gk->jgpkh", cc, eye).reshape(S5_BLOCKS, gpb * S5_STATE, gpb * S5_GROUP)

    def power(n):
        n = jnp.asarray(n, F32)[..., None, None]
        pmag = jnp.exp(n * (lr * dt))
        shape = n.shape[:-2] + (S5_LANES,)
        return (pmag * jnp.cos(n * (li * dt))).reshape(shape), (pmag * jnp.sin(n * (li * dt))).reshape(shape)

    row = jnp.arange(SUBLANES)
    step = 2 ** jnp.arange(S5_LOG_STEPS)
    s_re, s_im = power(step)
    keep = (row[None, :] >= step[:, None])[..., None]
    m_re = jnp.where(keep, s_re[:, None, :], 0.0)
    m_im = jnp.where(keep, s_im[:, None, :], 0.0)
    p_re, p_im = power(row + 1)
    return (in_map(bbar_re).astype(BF16), in_map(bbar_im).astype(BF16),
            out_map(c_re.astype(F32)).astype(BF16), out_map(c_im.astype(F32)).astype(BF16),
            m_re, m_im, p_re, p_im)


def _s5(proj, tables, d_skip, glu_w, glu_b, batch, seq):
    t = batch * seq
    ts = S5_TILE
    nj = seq // ts
    bre, bim, cre, cim, m_re, m_im, p_re, p_im = tables
    blk_c = MIX_HALF // S5_BLOCKS
    blk_s = S5_LANES // S5_BLOCKS
    const2 = lambda shape: pl.BlockSpec(shape, lambda b, j: (0, 0))
    const3 = lambda shape: pl.BlockSpec(shape, lambda b, j: (0, 0, 0))
    return pl.pallas_call(
        functools.partial(_s5_kernel, tile=ts),
        grid=(batch, nj),
        in_specs=[pl.BlockSpec((ts, MIX_HALF), lambda b, j: (b * nj + j, 0)),
                  const3((S5_BLOCKS, blk_c, blk_s)), const3((S5_BLOCKS, blk_c, blk_s)),
                  const3((S5_BLOCKS, blk_s, blk_c)), const3((S5_BLOCKS, blk_s, blk_c)),
                  const3(m_re.shape), const3(m_im.shape), const2(p_re.shape), const2(p_im.shape),
                  const2((1, MIX_HALF)), const2((MIX_HALF, MIX_HALF)), const2((1, MIX_HALF))],
        out_specs=pl.BlockSpec((ts, MIX_HALF), lambda b, j: (b * nj + j, 0)),
        out_shape=jax.ShapeDtypeStruct((t, MIX_HALF), BF16),
        scratch_shapes=[pltpu.VMEM((ts, S5_LANES), F32), pltpu.VMEM((ts, S5_LANES), F32),
                        pltpu.VMEM((8, S5_LANES), F32), pltpu.VMEM((8, S5_LANES), F32)],
        compiler_params=_params("arbitrary", "arbitrary"),
        name="s5",
    )(proj, bre, bim, cre, cim, m_re, m_im, p_re, p_im, d_skip.reshape(1, MIX_HALF), glu_w.astype(BF16),
      glu_b.reshape(1, MIX_HALF))


def _gla_kernel(q_ref, k_ref, v_ref, r_ref, gl_ref, gw_ref, gb_ref, nw_ref, y_ref,
                st_ref, qd_ref, ki_ref, ke_ref, v16_ref, dec_ref, o_ref, *, tile, chunk):
    L = chunk
    nc = tile // L

    @pl.when(pl.program_id(1) == 0)
    def _():
        st_ref[...] = jnp.zeros_like(st_ref)

    z = _dot(gl_ref[...].astype(BF16), gw_ref[...]) + gb_ref[...]
    bcum = _log_sigmoid(z) * (1.0 / GLA_GATE_TEMP)
    row_in_chunk = lax.broadcasted_iota(jnp.int32, bcum.shape, 0) & (L - 1)
    s = 1
    while s < L:
        bcum = bcum + jnp.where(row_in_chunk >= s, pltpu.roll(bcum, s, 0), 0.0)
        s *= 2
    b3 = bcum.reshape(nc, L, MIX_HALF)
    b_last = b3[:, L - 1:L, :]
    k = k_ref[...]
    qd_ref[...] = (q_ref[...] * (GLA_DK ** -0.5) * jnp.exp(bcum)).astype(BF16)
    ki_ref[...] = (k * jnp.exp(-bcum)).astype(BF16)
    ke_ref[...] = (k.reshape(nc, L, MIX_HALF) * jnp.exp(b_last - b3)).reshape(tile, MIX_HALF).astype(BF16)
    v16_ref[...] = v_ref[...].astype(BF16)
    dec_ref[...] = jnp.exp(b_last)

    ri = lax.broadcasted_iota(jnp.int32, (L, L), 0)
    ci = lax.broadcasted_iota(jnp.int32, (L, L), 1)
    causal = ci <= ri

    def body(c, carry):
        r0 = pl.multiple_of(c * L, L)
        dec = dec_ref[c]
        for h in range(HEADS):
            lo = h * HEAD_DIM
            q_dec = qd_ref[pl.ds(r0, L), lo:lo + HEAD_DIM]
            v = v16_ref[pl.ds(r0, L), lo:lo + HEAD_DIM]
            st = st_ref[h]
            att = jnp.where(causal, _dot_nt(q_dec, ki_ref[pl.ds(r0, L), lo:lo + HEAD_DIM]), 0.0)
            o_ref[pl.ds(r0, L), lo:lo + HEAD_DIM] = (_dot(att.astype(BF16), v)
                                                     + _dot_nt(q_dec, st.astype(BF16)))
            st_ref[h] = dec[:, lo:lo + HEAD_DIM] * st + _dot_tn(v, ke_ref[pl.ds(r0, L), lo:lo + HEAD_DIM])
        return carry

    lax.fori_loop(0, nc, body, 0, unroll=GLA_UNROLL)

    rg = r_ref[...]
    gate = nw_ref[...] * (rg * _sigmoid(rg))
    for h in range(HEADS):
        lo = h * HEAD_DIM
        o = o_ref[:, lo:lo + HEAD_DIM]
        yn = o * lax.rsqrt(jnp.mean(o * o, axis=-1, keepdims=True) + EPS)
        y_ref[:, lo:lo + HEAD_DIM] = (yn * gate[:, lo:lo + HEAD_DIM]).astype(BF16)


def _gla(proj, glow, gate_w, gate_b, norm_w, batch, seq):
    t = batch * seq
    ts = GLA_TILE
    nj = seq // ts
    blk = lambda col: pl.BlockSpec((ts, MIX_HALF), lambda b, j, col=col: (b * nj + j, col))
    const2 = lambda shape: pl.BlockSpec(shape, lambda b, j: (0, 0))
    return pl.pallas_call(
        functools.partial(_gla_kernel, tile=ts, chunk=GLA_CHUNK),
        grid=(batch, nj),
        in_specs=[blk(1), blk(2), blk(3), blk(4),
                  pl.BlockSpec((ts, LANES), lambda b, j: (b * nj + j, 0)),
                  const2((LANES, MIX_HALF)), const2((1, MIX_HALF)), const2((1, MIX_HALF))],
        out_specs=pl.BlockSpec((ts, MIX_HALF), lambda b, j: (b * nj + j, 0)),
        out_shape=jax.ShapeDtypeStruct((t, MIX_HALF), BF16),
        scratch_shapes=[pltpu.VMEM((HEADS, HEAD_DIM, HEAD_DIM), F32),
                        pltpu.VMEM((ts, MIX_HALF), BF16), pltpu.VMEM((ts, MIX_HALF), BF16),
                        pltpu.VMEM((ts, MIX_HALF), BF16), pltpu.VMEM((ts, MIX_HALF), BF16),
                        pltpu.VMEM((ts // GLA_CHUNK, 1, MIX_HALF), F32),
                        pltpu.VMEM((ts, MIX_HALF), F32)],
        compiler_params=_params("arbitrary", "arbitrary"),
        name="gla",
    )(proj, proj, proj, proj, glow, gate_w, gate_b, norm_w.reshape(1, MIX_HALF))


def _pad_heads(w, axis):
    shape = list(w.shape)
    shape[axis:axis + 1] = [HEADS, GLA_DK]
    w = w.reshape(shape)
    pad = [(0, 0)] * w.ndim
    pad[axis + 1] = (0, HEAD_DIM - GLA_DK)
    w = jnp.pad(w, pad)
    shape[axis:axis + 2] = [HEADS * HEAD_DIM]
    return w.reshape(shape)


def _pack_bf16_pairs(z):
    hi = lax.bitcast_convert_type(z[:, :PACKED].astype(BF16).astype(F32), jnp.uint32)
    lo = lax.bitcast_convert_type(z[:, PACKED:].astype(BF16).astype(F32), jnp.uint32)
    word = (hi & jnp.uint32(0xFFFF0000)) | lax.shift_right_logical(lo, jnp.uint32(16))
    return lax.bitcast_convert_type(word, jnp.int32)


def _unpack_bf16_pairs(p):
    word = lax.bitcast_convert_type(p, jnp.uint32)
    hi = lax.bitcast_convert_type(word & jnp.uint32(0xFFFF0000), F32)
    lo = lax.bitcast_convert_type(lax.shift_left(word, jnp.uint32(16)), F32)
    return hi, lo


def _out_kernel(ya_ref, yb_ref, h_ref, w_ref, lw_ref, lb_ref, rw_ref, rb_ref, sg_ref, su_ref, sd_ref,
                o_ref, opk_ref, idx_ref, rank_ref, wk_ref, cnt_ref, base_ref, *, tile):
    mixed = jnp.concatenate([ya_ref[...], yb_ref[...]], axis=1)
    z = ALPHA * h_ref[...] + _dot(mixed, w_ref[...])
    out = _layer_norm(z, lw_ref[...], lb_ref[...])
    out16 = out.astype(BF16)
    shared = _dot((_silu(_dot(out16, sg_ref[...])) * _dot(out16, su_ref[...])).astype(BF16), sd_ref[...])
    o_ref[...] = ALPHA * out + shared
    opk_ref[...] = _pack_bf16_pairs(out)
    _route_tile(out, rw_ref, rb_ref, idx_ref, rank_ref, wk_ref, cnt_ref, base_ref, tile)


def _out_proj_ln_route(ya, yb, h, w_out, ln_w, ln_b, router_w, router_bias, sg, su, sd):
    t = h.shape[0]
    tm = OUT_TILE
    const = lambda shape: pl.BlockSpec(shape, lambda i: (0, 0))
    per_tok = lambda dt: jax.ShapeDtypeStruct((TOP_K, t), dt)
    tok_blk = pl.BlockSpec((TOP_K, tm), lambda i: (0, i))
    return pl.pallas_call(
        functools.partial(_out_kernel, tile=tm),
        grid=(t // tm,),
        in_specs=[pl.BlockSpec((tm, MIX_HALF), lambda i: (i, 0)),
                  pl.BlockSpec((tm, MIX_HALF), lambda i: (i, 0)),
                  pl.BlockSpec((tm, D_MODEL), lambda i: (i, 0)),
                  const((D_MODEL, D_MODEL)), const((1, D_MODEL)), const((1, D_MODEL)),
                  const((N_EXPERTS, D_MODEL)), const((N_EXPERTS, 1)),
                  const((D_MODEL, D_EXPERT)), const((D_MODEL, D_EXPERT)), const((D_EXPERT, D_MODEL))],
        out_specs=[pl.BlockSpec((tm, D_MODEL), lambda i: (i, 0)),
                   pl.BlockSpec((tm, PACKED), lambda i: (i, 0)),
                   tok_blk, tok_blk, pl.BlockSpec((tm, TOP_K), lambda i: (i, 0)), const((N_EXPERTS, LANES))],
        out_shape=[jax.ShapeDtypeStruct((t, D_MODEL), F32), jax.ShapeDtypeStruct((t, PACKED), jnp.int32),
                   per_tok(jnp.int32), per_tok(jnp.int32), jax.ShapeDtypeStruct((t, TOP_K), F32),
                   jax.ShapeDtypeStruct((N_EXPERTS, LANES), F32)],
        scratch_shapes=[pltpu.VMEM((N_EXPERTS, LANES), F32)],
        compiler_params=_params("arbitrary"),
        name="out_proj_ln_route",
    )(ya, yb, h, w_out.astype(BF16), ln_w.reshape(1, D_MODEL), ln_b.reshape(1, D_MODEL),
      router_w.T, router_bias.reshape(N_EXPERTS, 1), sg.astype(BF16), su.astype(BF16), sd.astype(BF16))


def _first_index(hit, idx, big):
    return jnp.min(jnp.where(hit, idx, big), axis=0, keepdims=True)


def _route_tile(h, w_ref, b_ref, idx_ref, rank_ref, wk_ref, cnt_ref, base_ref, tile):
    @pl.when(pl.program_id(0) == 0)
    def _():
        base_ref[...] = jnp.zeros_like(base_ref)

    h_hi, h_mid, _ = _split3(h)
    w_hi, w_mid, _ = _split3(w_ref[...])
    logits = _dot_nt(w_hi, h_hi) + _dot_nt(w_hi, h_mid) + _dot_nt(w_mid, h_hi)
    scores = _sigmoid(logits)
    biased = scores + b_ref[...]

    sub = lax.broadcasted_iota(jnp.int32, (GROUP_SIZE, tile), 0)
    grp_rows = []
    for g in range(N_GROUPS):
        xg = biased[g * GROUP_SIZE:(g + 1) * GROUP_SIZE, :]
        m1 = jnp.max(xg, axis=0, keepdims=True)
        i1 = _first_index(xg == m1, sub, GROUP_SIZE)
        m2 = jnp.max(jnp.where(sub == i1, NEG_INF, xg), axis=0, keepdims=True)
        grp_rows.append(m1 + m2)
    gs = jnp.concatenate(grp_rows, axis=0)
    gsel = jnp.zeros((N_GROUPS, tile), F32)
    for _ in range(TOPK_GROUPS):
        mx = jnp.max(gs, axis=0, keepdims=True)
        hit = sub == _first_index(gs == mx, sub, N_GROUPS)
        gsel = jnp.where(hit, 1.0, gsel)
        gs = jnp.where(hit, NEG_INF, gs)
    emask = jnp.concatenate(
        [jnp.broadcast_to(gsel[g:g + 1, :], (GROUP_SIZE, tile)) for g in range(N_GROUPS)], axis=0)

    eidx = lax.broadcasted_iota(jnp.int32, (N_EXPERTS, tile), 0)
    cand = jnp.where(emask > 0.5, biased, NEG_INF)
    sel = jnp.zeros((N_EXPERTS, tile), F32)
    picks = []
    for _ in range(TOP_K):
        mx = jnp.max(cand, axis=0, keepdims=True)
        first = _first_index(cand == mx, eidx, N_EXPERTS)
        hit = eidx == first
        picks.append(first)
        sel = jnp.where(hit, 1.0, sel)
        cand = jnp.where(hit, NEG_INF, cand)
    picked = jnp.where(sel > 0.5, scores, 0.0)
    wts = picked / jnp.sum(picked, axis=0, keepdims=True) * ROUTED_SCALE

    ri = lax.broadcasted_iota(jnp.int32, (tile, tile), 0)
    ci = lax.broadcasted_iota(jnp.int32, (tile, tile), 1)
    before = (ri < ci).astype(BF16)
    prior = _dot(sel.astype(BF16), before) + base_ref[:, 0:1]
    ranks = [jnp.sum(jnp.where(eidx == p, prior, 0.0), axis=0, keepdims=True) for p in picks]
    wsel = [jnp.sum(jnp.where(eidx == p, wts, 0.0), axis=0, keepdims=True) for p in picks]
    idx_ref[...] = jnp.concatenate(picks, axis=0)
    rank_ref[...] = jnp.concatenate(ranks, axis=0).astype(jnp.int32)
    wk_ref[...] = jnp.concatenate(wsel, axis=0).T
    total = base_ref[...] + jnp.sum(sel, axis=1, keepdims=True)
    base_ref[...] = total
    cnt_ref[...] = total


def _silu(x):
    return x * _sigmoid(x)


def _sc_mesh():
    return plsc.VectorSubcoreMesh(core_axis_name="c", subcore_axis_name="s")


def _sc_worker_id():
    return lax.axis_index("s") * SC_CORES + lax.axis_index("c")


def _dispatch_rows(xpk, pos_chunks, n_rows):
    t = xpk.shape[0]
    n_ch = t // SC_WORKERS // SC_CHUNK

    @functools.partial(
        pl.kernel, mesh=_sc_mesh(),
        out_type=jax.ShapeDtypeStruct((n_rows, PACKED), jnp.int32),
        scratch_types=[pltpu.VMEM((TOP_K, SC_CHUNK), jnp.int32),
                       pltpu.VMEM((SC_CHUNK, PACKED), jnp.int32),
                       pltpu.SemaphoreType.DMA],
        name="moe_dispatch",
    )
    def scatter(x_hbm, pos_hbm, out_hbm, idx_v, rows_v, sem):
        wid = _sc_worker_id()

        @pl.loop(0, n_ch)
        def _(c):
            chunk = wid * n_ch + c
            off = pl.multiple_of(chunk * SC_CHUNK, SC_CHUNK)
            pltpu.sync_copy(pos_hbm.at[chunk], idx_v)
            pltpu.sync_copy(x_hbm.at[pl.ds(off, SC_CHUNK)], rows_v)
            copies = [pltpu.async_copy(rows_v, out_hbm.at[idx_v.at[k]], sem) for k in range(TOP_K)]
            for cp in copies:
                cp.wait()

    return scatter(xpk, pos_chunks)


def _gather_rows(table, idx):
    n = idx.shape[0]
    per_w = n // SC_WORKERS
    n_ch = per_w // SC_CHUNK
    assert n_ch % 2 == 0 and n_ch >= 2

    @functools.partial(
        pl.kernel, mesh=_sc_mesh(),
        out_type=jax.ShapeDtypeStruct((n, PACKED), jnp.int32),
        scratch_types=[pltpu.VMEM((n_ch, SC_CHUNK), jnp.int32),
                       pltpu.VMEM((SC_CHUNK, PACKED), jnp.int32), pltpu.VMEM((SC_CHUNK, PACKED), jnp.int32),
                       pltpu.SemaphoreType.DMA, pltpu.SemaphoreType.DMA,
                       pltpu.SemaphoreType.DMA, pltpu.SemaphoreType.DMA],
        name="moe_gather",
    )
    def gather(table_hbm, idx_hbm, out_hbm, idx_v, rows0, rows1, g0, g1, w0, w1):
        wid = _sc_worker_id()
        base = wid * per_w
        rows, g_sem, w_sem = (rows0, rows1), (g0, g1), (w0, w1)
        pltpu.sync_copy(idx_hbm.at[wid], idx_v)

        def fetch(c, b):
            return pltpu.make_async_copy(table_hbm.at[idx_v.at[c]], rows[b], g_sem[b])

        def flush(c, b):
            off = pl.multiple_of(base + c * SC_CHUNK, SC_CHUNK)
            return pltpu.make_async_copy(rows[b], out_hbm.at[pl.ds(off, SC_CHUNK)], w_sem[b])

        fetch(0, 0).start()

        @pl.loop(0, n_ch, step=2)
        def _(c0):
            for b in range(2):
                c = c0 + b
                fetch(c, b).wait()
                flush(c, b).start()

                @pl.when(c + 1 < n_ch)
                def _():
                    @pl.when(c >= 1)
                    def _():
                        flush(c - 1, 1 - b).wait()
                    fetch(c + 1, 1 - b).start()

        flush(n_ch - 2, 0).wait()
        flush(n_ch - 1, 1).wait()

    return gather(table, idx.reshape(SC_WORKERS, n_ch, SC_CHUNK))


def _unpacked_bf16(p):
    hi, lo = _unpack_bf16_pairs(p)
    return jnp.concatenate([hi.astype(BF16), lo.astype(BF16)], axis=1)


def _expert_kernel(be_ref, nu_ref, next_ref, slot_ref, xs_hbm, wg_hbm, wu_hbm, wd_hbm, y_ref,
                   g16_ref, u16_ref, d16_ref, gf_ref, uf_ref, df_ref, xbuf_ref, xsem, wsem, *, layer):
    i = pl.program_id(0)
    n_used = nu_ref[0]
    bm = xbuf_ref.shape[1]

    def fetch(b):
        slot = lax.rem(b, XS_SLOTS)
        rows = pl.ds(pl.multiple_of(b * bm, bm), bm)
        return pltpu.make_async_copy(xs_hbm.at[rows], xbuf_ref.at[slot], xsem.at[slot])

    @pl.when(i == 0)
    def _():
        fetch(0).start()

        @pl.when(n_used > 1)
        def _():
            fetch(1).start()

    @pl.when(i + 2 < n_used)
    def _():
        fetch(i + 2).start()

    e = be_ref[i]
    slot = slot_ref[e]
    f32_bufs = (gf_ref, uf_ref, df_ref)

    def wfetch(expert, dst_slot):
        return [pltpu.make_async_copy(w_hbm.at[layer, expert], buf.at[dst_slot], wsem.at[dst_slot, j])
                for j, (w_hbm, buf) in enumerate(zip((wg_hbm, wu_hbm, wd_hbm), f32_bufs))]

    @pl.when(i == 0)
    def _():
        for cp in wfetch(e, slot):
            cp.start()

    first_block_of_expert = jnp.logical_or(i == 0, e != be_ref[jnp.maximum(i - 1, 0)])

    @pl.when(jnp.logical_and(first_block_of_expert, i < n_used))
    def _():
        for cp in wfetch(e, slot):
            cp.wait()
        g16_ref[slot] = gf_ref[slot].astype(BF16)
        u16_ref[slot] = uf_ref[slot].astype(BF16)
        d16_ref[slot] = df_ref[slot].astype(BF16)
        nxt = next_ref[e]

        @pl.when(nxt >= 0)
        def _():
            for cp in wfetch(nxt, 1 - slot):
                cp.start()

    @pl.when(i < n_used)
    def _():
        fetch(i).wait()
        x = _unpacked_bf16(xbuf_ref[lax.rem(i, XS_SLOTS)])
        hh = _silu(_dot(x, g16_ref[slot])) * _dot(x, u16_ref[slot])
        y_ref[...] = _pack_bf16_pairs(_dot(hh.astype(BF16), d16_ref[slot]))


def _experts(block_e, n_used, next_expert, weight_slot, xs, wg, wu, wd, layer):
    nb = block_e.shape[0]
    bm = MOE_BLOCK
    hbm = pl.BlockSpec(memory_space=pl.ANY)
    two = lambda shape, dt: pltpu.VMEM((2,) + shape, dt)
    grid_spec = pltpu.PrefetchScalarGridSpec(
        num_scalar_prefetch=4,
        grid=(nb,),
        in_specs=[hbm, hbm, hbm, hbm],
        out_specs=pl.BlockSpec((bm, PACKED), lambda i, be, nu, nx, sl: (jnp.minimum(i, nu[0] - 1), 0)),
        scratch_shapes=[two((D_MODEL, D_EXPERT), BF16), two((D_MODEL, D_EXPERT), BF16), two((D_EXPERT, D_MODEL), BF16),
                        two((D_MODEL, D_EXPERT), F32), two((D_MODEL, D_EXPERT), F32), two((D_EXPERT, D_MODEL), F32),
                        pltpu.VMEM((XS_SLOTS, bm, PACKED), jnp.int32), pltpu.SemaphoreType.DMA((XS_SLOTS,)),
                        pltpu.SemaphoreType.DMA((2, 3))],
    )
    return pl.pallas_call(
        functools.partial(_expert_kernel, layer=layer),
        grid_spec=grid_spec,
        out_shape=jax.ShapeDtypeStruct((nb * bm, PACKED), jnp.int32),
        compiler_params=_params("arbitrary"),
        name="moe_experts",
    )(block_e, n_used, next_expert, weight_slot, xs, wg, wu, wd)


def _combine_kernel(g_ref, wk_ref, base_ref, lw_ref, lb_ref, *rest, with_proj):
    if with_proj:
        (wm_ref, ws_ref), (o_ref, proj_ref, small_ref) = rest[:2], rest[-3:]
        next_proj = (wm_ref, ws_ref, proj_ref, small_ref)
    else:
        o_ref, next_proj = rest[-1], None
    acc_hi = base_ref[:, :PACKED]
    acc_lo = base_ref[:, PACKED:]
    wk = wk_ref[...]
    for k in range(TOP_K):
        y_hi, y_lo = _unpack_bf16_pairs(g_ref[k])
        w = wk[:, k:k + 1]
        acc_hi = acc_hi + w * y_hi
        acc_lo = acc_lo + w * y_lo
    out = _layer_norm(jnp.concatenate([acc_hi, acc_lo], axis=1), lw_ref[...], lb_ref[...])
    o_ref[...] = out
    if next_proj is not None:
        wm_ref, ws_ref, proj_ref, small_ref = next_proj
        out16 = out.astype(BF16)
        proj_ref[...] = _dot(out16, wm_ref[...])
        small_ref[...] = _dot(out16, ws_ref[...])


def _combine_ln(g, wk, base, ln_w, ln_b, next_w, part, prev):
    t = base.shape[0]
    tm = COMBINE_TILE if next_w is not None else 2 * COMBINE_TILE
    n_blk = g.shape[1] // tm
    first = part * n_blk
    const = lambda shape: pl.BlockSpec(shape, lambda i: (0, 0))
    rows = lambda width: pl.BlockSpec((tm, width), lambda i: (i + first, 0))
    in_specs = [pl.BlockSpec((TOP_K, tm, PACKED), lambda i: (0, i, 0)), rows(TOP_K), rows(D_MODEL),
                const((1, D_MODEL)), const((1, D_MODEL))]
    args = [g, wk, base, ln_w.reshape(1, D_MODEL), ln_b.reshape(1, D_MODEL)]
    out_specs = [rows(D_MODEL)]
    out_shape = [jax.ShapeDtypeStruct((t, D_MODEL), F32)]
    if next_w is not None:
        w_main, w_small = next_w
        n = w_main.shape[1]
        in_specs += [const((D_MODEL, n)), const((D_MODEL, LANES))]
        args += [w_main, w_small]
        out_specs += [rows(n), rows(LANES)]
        out_shape += [jax.ShapeDtypeStruct((t, n), F32), jax.ShapeDtypeStruct((t, LANES), F32)]
    aliases = {}
    if prev is not None:
        aliases = {len(args) + k: k for k in range(len(prev))}
        in_specs += [pl.BlockSpec(memory_space=pl.ANY)] * len(prev)
        args += list(prev)
    return pl.pallas_call(
        functools.partial(_combine_kernel, with_proj=next_w is not None),
        grid=(n_blk,),
        in_specs=in_specs,
        out_specs=out_specs,
        out_shape=out_shape,
        input_output_aliases=aliases,
        compiler_params=_params("parallel"),
        name="moe_combine_ln",
    )(*args)


def _moe_ln(base, hpk, idx, rank, wk, counts, wg, wu, wd, layer, ln_w, ln_b, next_w):
    t = base.shape[0]
    cnt = counts[:, 0].astype(jnp.int32)
    padded = (cnt + MOE_BLOCK - 1) // MOE_BLOCK * MOE_BLOCK
    pend = jnp.cumsum(padded)
    experts = jnp.arange(N_EXPERTS, dtype=jnp.int32)
    pstart_of_pick = jnp.sum(jnp.where(idx[:, :, None] == experts, pend - padded, 0), axis=-1)
    pos = pstart_of_pick + rank
    nb = -(-(t * TOP_K + N_EXPERTS * (MOE_BLOCK - 1)) // MOE_BLOCK)
    starts = jnp.arange(nb, dtype=jnp.int32) * MOE_BLOCK
    block_e = jnp.minimum(jnp.sum((pend[None, :] <= starts[:, None]).astype(jnp.int32), axis=1), N_EXPERTS - 1)
    n_used = (pend[-1] // MOE_BLOCK).astype(jnp.int32).reshape(1)
    has_rows = cnt > 0
    later = jnp.logical_and(has_rows[None, :], experts[None, :] > experts[:, None])
    next_expert = jnp.min(jnp.where(later, experts[None, :], N_EXPERTS), axis=1)
    next_expert = jnp.where(next_expert == N_EXPERTS, -1, next_expert).astype(jnp.int32)
    weight_slot = ((jnp.cumsum(has_rows) - has_rows) % 2).astype(jnp.int32)
    pos_chunks = pos.reshape(TOP_K, t // SC_CHUNK, SC_CHUNK).transpose(1, 0, 2)
    xs = _dispatch_rows(hpk, pos_chunks, nb * MOE_BLOCK)
    ys = _experts(block_e, n_used, next_expert, weight_slot, xs, wg, wu, wd, layer)
    part = t // COMBINE_PARTS
    gathered = [_gather_rows(ys, pos[:, p * part:(p + 1) * part].reshape(-1)).reshape(TOP_K, part, PACKED)
                for p in range(COMBINE_PARTS)]
    outs = None
    for p in range(COMBINE_PARTS):
        outs = _combine_ln(gathered[p], wk, base, ln_w, ln_b, next_w, p, outs)
    return outs


def _pad_cols(w, width=LANES):
    return jnp.pad(w, ((0, 0), (0, width - w.shape[1])))


def _even_proj_weights(w_in):
    a4 = 4 * MIX_HALF
    ng = 2 * HEADS
    w_main = jnp.concatenate([w_in[:, :a4], w_in[:, a4 + ng:]], axis=1).astype(BF16)
    w_gate = _pad_cols(w_in[:, a4:a4 + ng]).astype(BF16)
    return w_main, w_gate


def _even_mixer(proj, gates, batch, seq, gate_b, norm_w, conv_w, conv_b, wa, ba, wx, bx, lam):
    ya = _mlstm(proj, gates, gate_b, norm_w, batch, seq)
    yb = _rglru(proj, conv_w, conv_b, wa, ba, wx, bx, lam, batch, seq)
    return ya, yb


def _odd_proj_weights(w_in):
    c0 = MIX_HALF
    c1 = c0 + HEADS * GLA_DK
    c2 = c1 + HEADS * GLA_DK
    c3 = c2 + MIX_HALF
    c4 = c3 + MIX_HALF
    w_main = jnp.concatenate([w_in[:, :c0], _pad_heads(w_in[:, c0:c1], 1), _pad_heads(w_in[:, c1:c2], 1),
                              w_in[:, c2:c4]], axis=1).astype(BF16)
    w_low = _pad_cols(w_in[:, c4:]).astype(BF16)
    return w_main, w_low


def _odd_mixer(proj, glow, batch, seq, lam_re, lam_im, b_re, b_im, c_re, c_im, d_skip, log_dt,
               glu_w, glu_b, gate_w, gate_b, norm_w):
    tables = _s5_tables(lam_re, lam_im, b_re, b_im, c_re, c_im, log_dt)
    yc = _s5(proj, tables, d_skip, glu_w, glu_b, batch, seq)
    gw = jnp.pad(_pad_heads(gate_w, 1), ((0, LANES - GLA_GATE_RANK), (0, 0))).astype(BF16)
    gb = _pad_heads(gate_b.reshape(1, -1), 1)
    yd = _gla(proj, glow, gw, gb, norm_w, batch, seq)
    return yc, yd


def kernel(x, ln1_w, ln1_b, ln2_w, ln2_b, w_out, w_in_even, mlstm_gate_b, mlstm_norm_w, lru_conv_w, lru_conv_b, lru_wa, lru_ba, lru_wx, lru_bx, lru_lambda, w_in_odd, s5_lam_re, s5_lam_im, s5_b_re, s5_b_im, s5_c_re, s5_c_im, s5_d, s5_log_dt, s5_glu_w, s5_glu_b, gla_gate_w, gla_gate_b, gla_norm_w, router_w, router_bias, exp_w_gate, exp_w_up, exp_w_down, sh_w_gate, sh_w_up, sh_w_down):
    batch, seq, d = x.shape
    proj_w = [_even_proj_weights(w_in_even[layer // 2]) if layer % 2 == 0 else _odd_proj_weights(w_in_odd[layer // 2])
              for layer in range(DEPTH)]
    h = x.reshape(batch * seq, d)
    proj, small = _proj(h, *proj_w[0])
    for layer in range(DEPTH):
        j = layer // 2
        if layer % 2 == 0:
            y1, y2 = _even_mixer(proj, small, batch, seq, mlstm_gate_b[j], mlstm_norm_w[j],
                                 lru_conv_w[j], lru_conv_b[j], lru_wa[j], lru_ba[j], lru_wx[j],
                                 lru_bx[j], lru_lambda[j])
        else:
            y1, y2 = _odd_mixer(proj, small, batch, seq, s5_lam_re[j], s5_lam_im[j], s5_b_re[j],
                                s5_b_im[j], s5_c_re[j], s5_c_im[j], s5_d[j], s5_log_dt[j],
                                s5_glu_w[j], s5_glu_b[j], gla_gate_w[j], gla_gate_b[j], gla_norm_w[j])
        base, hpk, idx, rank, wk, counts = _out_proj_ln_route(
            y1, y2, h, w_out[layer], ln1_w[layer], ln1_b[layer], router_w[layer], router_bias[layer],
            sh_w_gate[layer], sh_w_up[layer], sh_w_down[layer])
        next_w = proj_w[layer + 1] if layer + 1 < DEPTH else None
        res = _moe_ln(base, hpk, idx, rank, wk, counts, exp_w_gate, exp_w_up, exp_w_down, layer,
                      ln2_w[layer], ln2_b[layer], next_w)
        if next_w is None:
            (h,) = res
        else:
            h, proj, small = res
    return h.reshape(batch, seq, d)
```

```python
import functools
import math

import jax
import jax.numpy as jnp
from jax import lax
from jax.experimental import pallas as pl
from jax.experimental.pallas import tpu as pltpu
from jax.experimental.pallas import tpu_sc as plsc

F32 = jnp.float32
BF16 = jnp.bfloat16

D_MODEL = 1024
DEPTH = 2
MIX_HALF = 512
HEADS = 4
HEAD_DIM = 128
GLA_DK = 64
GLA_CHUNK = 64
GLA_GATE_RANK = 16
GLA_GATE_TEMP = 16.0
LRU_C = 8.0
LRU_CONV = 4
S5_GROUP = 16
S5_GROUPS = 32
S5_STATE = 64
S5_LANES = S5_GROUPS * S5_STATE
S5_BLOCKS = 4
N_EXPERTS = 64
N_GROUPS = 8
GROUP_SIZE = N_EXPERTS // N_GROUPS
TOP_K = 8
TOPK_GROUPS = 4
D_EXPERT = 256
ROUTED_SCALE = 2.5
ALPHA = (2.0 * DEPTH) ** 0.25
EPS = 1e-5
LANES = 128
SUBLANES = 8
NEG_INF = float("-inf")

VMEM_LIMIT = 56 * 1024 * 1024

MLSTM_CHUNK = 128
MLSTM_TILE = 1024
MLSTM_UNROLL = 2
LRU_TILE = 1024
LRU_LOG_STEPS = 3
LRU_UNROLL = 4
S5_TILE = 512
S5_LOG_STEPS = 3
S5_UNROLL = True
GLA_UNROLL = 16
GLA_TILE = 1024
PROJ_TILE = 1024
OUT_TILE = 1024
MOE_BLOCK = 1152
XS_SLOTS = 3
COMBINE_TILE = 256
COMBINE_PARTS = 2
PACKED = D_MODEL // 2
SC_CHUNK = 64
SC_DISPATCH_CHUNK = 128
SC_CORES = 2
SC_SUBCORES = 16
SC_WORKERS = SC_CORES * SC_SUBCORES


def _params(*sem):
    return pltpu.CompilerParams(dimension_semantics=sem, vmem_limit_bytes=VMEM_LIMIT)


def _split3(x):
    hi = x.astype(BF16)
    r1 = x - hi.astype(F32)
    mid = r1.astype(BF16)
    lo = (r1 - mid.astype(F32)).astype(BF16)
    return hi, mid, lo


def _dot(a, b):
    return jnp.dot(a, b, preferred_element_type=F32)


def _dot_nt(a, b):
    return lax.dot_general(a, b, (((1,), (1,)), ((), ())), preferred_element_type=F32)


def _dot_tn(a, b):
    return lax.dot_general(a, b, (((0,), (0,)), ((), ())), preferred_element_type=F32)


def _exact_left01(mask01_bf16, x):
    hi, mid, lo = _split3(x)
    return _dot(mask01_bf16, hi) + _dot(mask01_bf16, mid) + _dot(mask01_bf16, lo)


def _exact_right01(x, mask01_bf16):
    hi, mid, lo = _split3(x)
    return _dot(hi, mask01_bf16) + _dot(mid, mask01_bf16) + _dot(lo, mask01_bf16)


def _log_sigmoid(x):
    return jnp.minimum(x, 0.0) - jnp.log(1.0 + jnp.exp(-jnp.abs(x)))


def _sigmoid(x):
    return 1.0 / (1.0 + jnp.exp(-x))


def _gelu_tanh(x):
    c = math.sqrt(2.0 / math.pi)
    return 0.5 * x * (1.0 + jnp.tanh(c * (x + 0.044715 * (x * x * x))))


def _layer_norm(z, w, b):
    mu = jnp.mean(z, axis=-1, keepdims=True)
    zc = z - mu
    return zc * lax.rsqrt(jnp.mean(zc * zc, axis=-1, keepdims=True) + EPS) * w + b


def _proj_kernel(x_ref, w_ref, wg_ref, o_ref, og_ref):
    x = x_ref[...].astype(BF16)
    o_ref[...] = _dot(x, w_ref[...])
    og_ref[...] = _dot(x, wg_ref[...])


def _proj(x, w_main, w_small):
    t, d = x.shape
    n = w_main.shape[1]
    tm = PROJ_TILE
    return pl.pallas_call(
        _proj_kernel,
        grid=(t // tm,),
        in_specs=[pl.BlockSpec((tm, d), lambda i: (i, 0)),
                  pl.BlockSpec((d, n), lambda i: (0, 0)),
                  pl.BlockSpec((d, LANES), lambda i: (0, 0))],
        out_specs=[pl.BlockSpec((tm, n), lambda i: (i, 0)),
                   pl.BlockSpec((tm, LANES), lambda i: (i, 0))],
        out_shape=[jax.ShapeDtypeStruct((t, n), F32), jax.ShapeDtypeStruct((t, LANES), F32)],
        compiler_params=_params("parallel"),
        name="in_proj",
    )(x, w_main, w_small)


def _mlstm_kernel(q_ref, k_ref, v_ref, o_ref, gc_ref, gr_ref, bc_ref, br_ref, nw_ref,
                  y_ref, c_ref, m_ref, *, chunk, n_chunks):
    L = chunk

    @pl.when(pl.program_id(1) == 0)
    def _():
        c_ref[...] = jnp.zeros_like(c_ref)
        m_ref[...] = jnp.zeros_like(m_ref)

    ri = lax.broadcasted_iota(jnp.int32, (L, L), 0)
    ci = lax.broadcasted_iota(jnp.int32, (L, L), 1)
    causal = ci <= ri
    tril = causal.astype(BF16)
    triu = (ri <= ci).astype(BF16)
    ones_v = jnp.ones((L, HEAD_DIM), BF16)
    scale = HEAD_DIM ** -0.5

    def body(c, carry):
        r0 = pl.multiple_of(c * L, L)
        g_col = gc_ref[pl.ds(r0, L), :] + bc_ref[...]
        g_row = gr_ref[c] + br_ref[...]
        b_col_all = _exact_left01(tril, _log_sigmoid(g_col))
        b_row_all = _exact_right01(_log_sigmoid(g_row), triu)
        for h in range(HEADS):
            lo = h * HEAD_DIM
            q = q_ref[pl.ds(r0, L), lo:lo + HEAD_DIM].astype(BF16)
            k = k_ref[pl.ds(r0, L), lo:lo + HEAD_DIM] * scale
            v = v_ref[pl.ds(r0, L), lo:lo + HEAD_DIM].astype(BF16)
            v_aug = jnp.concatenate([v, ones_v], axis=1)
            i_rep = jnp.broadcast_to(g_col[:, h:h + 1], (L, LANES))
            b_rep = jnp.broadcast_to(b_col_all[:, HEADS + h:HEADS + h + 1], (L, LANES))
            i_row = g_row[h:h + 1, :]
            b_row = b_row_all[HEADS + h:HEADS + h + 1, :]
            b_last = b_rep[L - 1:L, :]
            m_prev = m_ref[h:h + 1, :]
            c_prev = c_ref[h]

            d_mat = jnp.where(causal, b_rep - b_row + i_row, NEG_INF)
            m_inter = b_rep + m_prev
            m_i = jnp.maximum(m_inter, jnp.max(d_mat, axis=1, keepdims=True))
            s = _dot_nt(q, k.astype(BF16)) * jnp.exp(d_mat - m_i)
            w_inter = jnp.exp(m_inter - m_i)
            intra = _dot(s.astype(BF16), v_aug)
            inter = _dot(q, c_prev.astype(BF16))
            num = intra[:, :HEAD_DIM] + w_inter * inter[:, :HEAD_DIM]
            den = intra[:, HEAD_DIM:] + w_inter * inter[:, HEAD_DIM:]
            hh = num / jnp.maximum(jnp.abs(den), jnp.exp(-m_i))

            w_loc = b_last - b_rep + i_rep
            m_loc = jnp.max(w_loc, axis=0, keepdims=True)
            kp = (k * jnp.exp(w_loc - m_loc)).astype(BF16)
            c_loc = _dot_tn(kp, v_aug)
            m_new = jnp.maximum(b_last + m_prev, m_loc)
            keep = jnp.exp(b_last + m_prev - m_new)
            add = jnp.exp(m_loc - m_new)
            c_ref[h] = (jnp.concatenate([keep, keep], axis=1) * c_prev
                        + jnp.concatenate([add, add], axis=1) * c_loc)
            m_ref[h:h + 1, :] = m_new

            hc = hh - jnp.mean(hh, axis=-1, keepdims=True)
            yn = hc * lax.rsqrt(jnp.mean(hc * hc, axis=-1, keepdims=True) + EPS)
            og = o_ref[pl.ds(r0, L), lo:lo + HEAD_DIM]
            y_ref[pl.ds(r0, L), lo:lo + HEAD_DIM] = (yn * nw_ref[:, lo:lo + HEAD_DIM] * _sigmoid(og)).astype(BF16)
        return carry

    lax.fori_loop(0, n_chunks, body, 0, unroll=MLSTM_UNROLL)


def _mlstm(proj, gates, gate_b, norm_w, batch, seq):
    t = batch * seq
    L = MLSTM_CHUNK
    assert L == LANES, "the kernel keeps per-row gate terms replicated over one vreg of lanes"
    ts = MLSTM_TILE
    nj = seq // ts
    nc = ts // L
    g_row = gates[:, :2 * HEADS].reshape(t // L, L, 2 * HEADS).transpose(0, 2, 1)
    b_col = jnp.zeros((1, LANES), F32).at[0, :2 * HEADS].set(gate_b)
    b_row = gate_b.reshape(2 * HEADS, 1)
    blk = lambda col: pl.BlockSpec((ts, MIX_HALF), lambda b, j, col=col: (b * nj + j, col))
    kern = functools.partial(_mlstm_kernel, chunk=L, n_chunks=nc)
    return pl.pallas_call(
        kern,
        grid=(batch, nj),
        in_specs=[blk(0), blk(1), blk(2), blk(3),
                  pl.BlockSpec((ts, LANES), lambda b, j: (b * nj + j, 0)),
                  pl.BlockSpec((nc, 2 * HEADS, L), lambda b, j: (b * nj + j, 0, 0)),
                  pl.BlockSpec((1, LANES), lambda b, j: (0, 0)),
                  pl.BlockSpec((2 * HEADS, 1), lambda b, j: (0, 0)),
                  pl.BlockSpec((1, MIX_HALF), lambda b, j: (0, 0))],
        out_specs=pl.BlockSpec((ts, MIX_HALF), lambda b, j: (b * nj + j, 0)),
        out_shape=jax.ShapeDtypeStruct((t, MIX_HALF), BF16),
        scratch_shapes=[pltpu.VMEM((HEADS, HEAD_DIM, 2 * HEAD_DIM), F32),
                        pltpu.VMEM((8, LANES), F32)],
        compiler_params=_params("arbitrary", "arbitrary"),
        name="mlstm",
    )(proj, proj, proj, proj, gates, g_row, b_col, b_row, norm_w.reshape(1, MIX_HALF))


def _rglru_kernel(xb_ref, gb_ref, cw_ref, cb_ref, wa_ref, ba_ref, wx_ref, bx_ref, lam_ref,
                  y_ref, xext_ref, h_ref, a_ref, u_ref, *, tile):
    @pl.when(pl.program_id(1) == 0)
    def _():
        xext_ref[0:8, :] = jnp.zeros((8, MIX_HALF), F32)
        h_ref[...] = jnp.zeros_like(h_ref)

    x = xb_ref[...]
    xext_ref[8:8 + tile, :] = x
    xc = cb_ref[...] + cw_ref[LRU_CONV - 1:LRU_CONV, :] * x
    for tap in range(LRU_CONV - 1):
        back = LRU_CONV - 1 - tap
        xc = xc + cw_ref[tap:tap + 1, :] * xext_ref[8 - back:8 - back + tile, :]
    xext_ref[0:8, :] = x[tile - 8:tile, :]

    xc16 = xc.astype(BF16)
    r_parts, i_parts = [], []
    for h in range(HEADS):
        lo = h * HEAD_DIM
        xh = xc16[:, lo:lo + HEAD_DIM]
        r_parts.append(_dot(xh, wa_ref[h]))
        i_parts.append(_dot(xh, wx_ref[h]))
    r = _sigmoid(jnp.concatenate(r_parts, axis=1) + ba_ref[...])
    ig = _sigmoid(jnp.concatenate(i_parts, axis=1) + bx_ref[...])
    lam = lam_ref[...]
    softplus_neg = jnp.maximum(-lam, 0.0) + jnp.log(1.0 + jnp.exp(-jnp.abs(lam)))
    log_a = -LRU_C * r * softplus_neg
    a = jnp.exp(log_a)
    th = jnp.tanh(log_a)
    u = jnp.sqrt(-2.0 * th / (1.0 - th)) * ig * xc

    a_ref[...] = a
    u_ref[...] = u
    rows = lax.broadcasted_iota(jnp.int32, (SUBLANES, MIX_HALF), 0)

    def group(i, h_prev):
        r0 = pl.multiple_of(i * SUBLANES, SUBLANES)
        ag = a_ref[pl.ds(r0, SUBLANES), :]
        ug = u_ref[pl.ds(r0, SUBLANES), :]
        for k in range(LRU_LOG_STEPS):
            keep = rows >= (1 << k)
            ug = ag * jnp.where(keep, pltpu.roll(ug, 1 << k, 0), 0.0) + ug
            ag = ag * jnp.where(keep, pltpu.roll(ag, 1 << k, 0), 1.0)
        hg = ug + ag * h_prev
        u_ref[pl.ds(r0, SUBLANES), :] = hg
        return hg[SUBLANES - 1:SUBLANES, :]

    h_last = lax.fori_loop(0, tile // SUBLANES, group, h_ref[0:1, :], unroll=LRU_UNROLL)
    h_ref[...] = jnp.broadcast_to(h_last, h_ref.shape)
    y_ref[...] = (u_ref[...] * _gelu_tanh(gb_ref[...])).astype(BF16)


def _rglru(proj, conv_w, conv_b, wa, ba, wx, bx, lam, batch, seq):
    t = batch * seq
    ts = LRU_TILE
    nj = seq // ts
    row = lambda a: a.reshape(1, MIX_HALF)
    const2 = lambda shape: pl.BlockSpec(shape, lambda b, j: (0, 0))
    const3 = lambda shape: pl.BlockSpec(shape, lambda b, j: (0, 0, 0))
    blk = lambda col: pl.BlockSpec((ts, MIX_HALF), lambda b, j, col=col: (b * nj + j, col))
    return pl.pallas_call(
        functools.partial(_rglru_kernel, tile=ts),
        grid=(batch, nj),
        in_specs=[blk(4), blk(5), const2((LRU_CONV, MIX_HALF)), const2((1, MIX_HALF)),
                  const3((HEADS, HEAD_DIM, HEAD_DIM)), const2((1, MIX_HALF)),
                  const3((HEADS, HEAD_DIM, HEAD_DIM)), const2((1, MIX_HALF)), const2((1, MIX_HALF))],
        out_specs=pl.BlockSpec((ts, MIX_HALF), lambda b, j: (b * nj + j, 0)),
        out_shape=jax.ShapeDtypeStruct((t, MIX_HALF), BF16),
        scratch_shapes=[pltpu.VMEM((ts + 8, MIX_HALF), F32), pltpu.VMEM((8, MIX_HALF), F32),
                        pltpu.VMEM((ts, MIX_HALF), F32), pltpu.VMEM((ts, MIX_HALF), F32)],
        compiler_params=_params("arbitrary", "arbitrary"),
        name="rglru",
    )(proj, proj, conv_w, row(conv_b), wa.astype(BF16), row(ba), wx.astype(BF16), row(bx), row(lam))


def _s5_kernel(u_ref, bre_ref, bim_ref, cre_ref, cim_ref, mre_ref, mim_ref, pre_ref, pim_ref, d_ref, gw_ref,
               gb_ref, y_ref, xr_ref, xi_ref, cr_ref, ci_ref, *, tile):
    @pl.when(pl.program_id(1) == 0)
    def _():
        cr_ref[...] = jnp.zeros_like(cr_ref)
        ci_ref[...] = jnp.zeros_like(ci_ref)

    u = u_ref[...]
    u16 = u.astype(BF16)
    blk_c = MIX_HALF // S5_BLOCKS
    blk_s = S5_LANES // S5_BLOCKS
    parts = []
    for j in range(S5_BLOCKS):
        lanes = slice(j * blk_s, (j + 1) * blk_s)
        uj = u16[:, j * blk_c:(j + 1) * blk_c]
        xr_ref[:, lanes] = _dot(uj, bre_ref[j])
        xi_ref[:, lanes] = _dot(uj, bim_ref[j])

        def group(i, carry, lanes=lanes):
            cr, ci = carry
            r0 = pl.multiple_of(i * SUBLANES, SUBLANES)
            xr = xr_ref[pl.ds(r0, SUBLANES), lanes]
            xi = xi_ref[pl.ds(r0, SUBLANES), lanes]
            for k in range(S5_LOG_STEPS):
                sr = pltpu.roll(xr, 1 << k, 0)
                si = pltpu.roll(xi, 1 << k, 0)
                mr = mre_ref[k, :, lanes]
                mi = mim_ref[k, :, lanes]
                xr, xi = xr + mr * sr - mi * si, xi + mr * si + mi * sr
            pr = pre_ref[:, lanes]
            pi = pim_ref[:, lanes]
            xr, xi = xr + pr * cr - pi * ci, xi + pr * ci + pi * cr
            xr_ref[pl.ds(r0, SUBLANES), lanes] = xr
            xi_ref[pl.ds(r0, SUBLANES), lanes] = xi
            return xr[SUBLANES - 1:SUBLANES, :], xi[SUBLANES - 1:SUBLANES, :]

        cr, ci = lax.fori_loop(0, tile // SUBLANES, group, (cr_ref[0:1, lanes], ci_ref[0:1, lanes]),
                               unroll=S5_UNROLL)
        cr_ref[0:1, lanes] = cr
        ci_ref[0:1, lanes] = ci
        parts.append(_dot(xr_ref[:, lanes].astype(BF16), cre_ref[j])
                     - _dot(xi_ref[:, lanes].astype(BF16), cim_ref[j]))
    y = jnp.concatenate(parts, axis=1) + d_ref[...] * u
    g = _gelu_tanh(y)
    y_ref[...] = (g * _sigmoid(_dot(g.astype(BF16), gw_ref[...]) + gb_ref[...])).astype(BF16)


def _s5_tables(lam_re, lam_im, b_re, b_im, c_re, c_im, log_dt):
    lr, li = lam_re.astype(F32), lam_im.astype(F32)
    dt = jnp.exp(log_dt.astype(F32))[:, None]
    mag = jnp.exp(lr * dt)
    abar_re = mag * jnp.cos(li * dt)
    abar_im = mag * jnp.sin(li * dt)
    den = lr * lr + li * li
    nr = abar_re - 1.0
    coef_re = (nr * lr + abar_im * li) / den
    coef_im = (abar_im * lr - nr * li) / den
    bbar_re = coef_re[..., None] * b_re - coef_im[..., None] * b_im
    bbar_im = coef_re[..., None] * b_im + coef_im[..., None] * b_re
    gpb = S5_GROUPS // S5_BLOCKS
    eye = jnp.eye(gpb, dtype=F32)

    def in_map(bb):
        bb = bb.reshape(S5_BLOCKS, gpb, S5_STATE, S5_GROUP)
        return jnp.einsum("jgph,gk->jghkp", bb, eye).reshape(S5_BLOCKS, gpb * S5_GROUP, gpb * S5_STATE)

    def out_map(cc):
        cc = cc.reshape(S5_BLOCKS, gpb, S5_GROUP, S5_STATE)
        return jnp.einsum("jghp,gk->jgpkh", cc, eye).reshape(S5_BLOCKS, gpb * S5_STATE, gpb * S5_GROUP)

    def power(n):
        n = jnp.asarray(n, F32)[..., None, None]
        pmag = jnp.exp(n * (lr * dt))
        shape = n.shape[:-2] + (S5_LANES,)
        return (pmag * jnp.cos(n * (li * dt))).reshape(shape), (pmag * jnp.sin(n * (li * dt))).reshape(shape)

    row = jnp.arange(SUBLANES)
    step = 2 ** jnp.arange(S5_LOG_STEPS)
    s_re, s_im = power(step)
    keep = (row[None, :] >= step[:, None])[..., None]
    m_re = jnp.where(keep, s_re[:, None, :], 0.0)
    m_im = jnp.where(keep, s_im[:, None, :], 0.0)
    p_re, p_im = power(row + 1)
    return (in_map(bbar_re).astype(BF16), in_map(bbar_im).astype(BF16),
            out_map(c_re.astype(F32)).astype(BF16), out_map(c_im.astype(F32)).astype(BF16),
            m_re, m_im, p_re, p_im)


def _s5(proj, tables, d_skip, glu_w, glu_b, batch, seq):
    t = batch * seq
    ts = S5_TILE
    nj = seq // ts
    bre, bim, cre, cim, m_re, m_im, p_re, p_im = tables
    blk_c = MIX_HALF // S5_BLOCKS
    blk_s = S5_LANES // S5_BLOCKS
    const2 = lambda shape: pl.BlockSpec(shape, lambda b, j: (0, 0))
    const3 = lambda shape: pl.BlockSpec(shape, lambda b, j: (0, 0, 0))
    return pl.pallas_call(
        functools.partial(_s5_kernel, tile=ts),
        grid=(batch, nj),
        in_specs=[pl.BlockSpec((ts, MIX_HALF), lambda b, j: (b * nj + j, 0)),
                  const3((S5_BLOCKS, blk_c, blk_s)), const3((S5_BLOCKS, blk_c, blk_s)),
                  const3((S5_BLOCKS, blk_s, blk_c)), const3((S5_BLOCKS, blk_s, blk_c)),
                  const3(m_re.shape), const3(m_im.shape), const2(p_re.shape), const2(p_im.shape),
                  const2((1, MIX_HALF)), const2((MIX_HALF, MIX_HALF)), const2((1, MIX_HALF))],
        out_specs=pl.BlockSpec((ts, MIX_HALF), lambda b, j: (b * nj + j, 0)),
        out_shape=jax.ShapeDtypeStruct((t, MIX_HALF), BF16),
        scratch_shapes=[pltpu.VMEM((ts, S5_LANES), F32), pltpu.VMEM((ts, S5_LANES), F32),
                        pltpu.VMEM((8, S5_LANES), F32), pltpu.VMEM((8, S5_LANES), F32)],
        compiler_params=_params("arbitrary", "arbitrary"),
        name="s5",
    )(proj, bre, bim, cre, cim, m_re, m_im, p_re, p_im, d_skip.reshape(1, MIX_HALF), glu_w.astype(BF16),
      glu_b.reshape(1, MIX_HALF))


def _gla_kernel(q_ref, k_ref, v_ref, r_ref, gl_ref, gw_ref, gb_ref, nw_ref, y_ref,
                st_ref, qd_ref, ki_ref, ke_ref, v16_ref, dec_ref, o_ref, *, tile, chunk):
    L = chunk
    nc = tile // L

    @pl.when(pl.program_id(1) == 0)
    def _():
        st_ref[...] = jnp.zeros_like(st_ref)

    z = _dot(gl_ref[...].astype(BF16), gw_ref[...]) + gb_ref[...]
    bcum = _log_sigmoid(z) * (1.0 / GLA_GATE_TEMP)
    row_in_chunk = lax.broadcasted_iota(jnp.int32, bcum.shape, 0) & (L - 1)
    s = 1
    while s < L:
        bcum = bcum + jnp.where(row_in_chunk >= s, pltpu.roll(bcum, s, 0), 0.0)
        s *= 2
    b3 = bcum.reshape(nc, L, MIX_HALF)
    b_last = b3[:, L - 1:L, :]
    k = k_ref[...]
    qd_ref[...] = (q_ref[...] * (GLA_DK ** -0.5) * jnp.exp(bcum)).astype(BF16)
    ki_ref[...] = (k * jnp.exp(-bcum)).astype(BF16)
    ke_ref[...] = (k.reshape(nc, L, MIX_HALF) * jnp.exp(b_last - b3)).reshape(tile, MIX_HALF).astype(BF16)
    v16_ref[...] = v_ref[...].astype(BF16)
    dec_ref[...] = jnp.exp(b_last)

    ri = lax.broadcasted_iota(jnp.int32, (L, L), 0)
    ci = lax.broadcasted_iota(jnp.int32, (L, L), 1)
    causal = ci <= ri

    def body(c, carry):
        r0 = pl.multiple_of(c * L, L)
        dec = dec_ref[c]
        for h in range(HEADS):
            lo = h * HEAD_DIM
            q_dec = qd_ref[pl.ds(r0, L), lo:lo + HEAD_DIM]
            v = v16_ref[pl.ds(r0, L), lo:lo + HEAD_DIM]
            st = st_ref[h]
            att = jnp.where(causal, _dot_nt(q_dec, ki_ref[pl.ds(r0, L), lo:lo + HEAD_DIM]), 0.0)
            o_ref[pl.ds(r0, L), lo:lo + HEAD_DIM] = (_dot(att.astype(BF16), v)
                                                     + _dot_nt(q_dec, st.astype(BF16)))
            st_ref[h] = dec[:, lo:lo + HEAD_DIM] * st + _dot_tn(v, ke_ref[pl.ds(r0, L), lo:lo + HEAD_DIM])
        return carry

    lax.fori_loop(0, nc, body, 0, unroll=GLA_UNROLL)

    rg = r_ref[...]
    gate = nw_ref[...] * (rg * _sigmoid(rg))
    for h in range(HEADS):
        lo = h * HEAD_DIM
        o = o_ref[:, lo:lo + HEAD_DIM]
        yn = o * lax.rsqrt(jnp.mean(o * o, axis=-1, keepdims=True) + EPS)
        y_ref[:, lo:lo + HEAD_DIM] = (yn * gate[:, lo:lo + HEAD_DIM]).astype(BF16)


def _gla(proj, glow, gate_w, gate_b, norm_w, batch, seq):
    t = batch * seq
    ts = GLA_TILE
    nj = seq // ts
    blk = lambda col: pl.BlockSpec((ts, MIX_HALF), lambda b, j, col=col: (b * nj + j, col))
    const2 = lambda shape: pl.BlockSpec(shape, lambda b, j: (0, 0))
    return pl.pallas_call(
        functools.partial(_gla_kernel, tile=ts, chunk=GLA_CHUNK),
        grid=(batch, nj),
        in_specs=[blk(1), blk(2), blk(3), blk(4),
                  pl.BlockSpec((ts, LANES), lambda b, j: (b * nj + j, 0)),
                  const2((LANES, MIX_HALF)), const2((1, MIX_HALF)), const2((1, MIX_HALF))],
        out_specs=pl.BlockSpec((ts, MIX_HALF), lambda b, j: (b * nj + j, 0)),
        out_shape=jax.ShapeDtypeStruct((t, MIX_HALF), BF16),
        scratch_shapes=[pltpu.VMEM((HEADS, HEAD_DIM, HEAD_DIM), F32),
                        pltpu.VMEM((ts, MIX_HALF), BF16), pltpu.VMEM((ts, MIX_HALF), BF16),
                        pltpu.VMEM((ts, MIX_HALF), BF16), pltpu.VMEM((ts, MIX_HALF), BF16),
                        pltpu.VMEM((ts // GLA_CHUNK, 1, MIX_HALF), F32),
                        pltpu.VMEM((ts, MIX_HALF), F32)],
        compiler_params=_params("arbitrary", "arbitrary"),
        name="gla",
    )(proj, proj, proj, proj, glow, gate_w, gate_b, norm_w.reshape(1, MIX_HALF))


def _pad_heads(w, axis):
    shape = list(w.shape)
    shape[axis:axis + 1] = [HEADS, GLA_DK]
    w = w.reshape(shape)
    pad = [(0, 0)] * w.ndim
    pad[axis + 1] = (0, HEAD_DIM - GLA_DK)
    w = jnp.pad(w, pad)
    shape[axis:axis + 2] = [HEADS * HEAD_DIM]
    return w.reshape(shape)


def _pack_bf16_pairs(z):
    hi = lax.bitcast_convert_type(z[:, :PACKED].astype(BF16).astype(F32), jnp.uint32)
    lo = lax.bitcast_convert_type(z[:, PACKED:].astype(BF16).astype(F32), jnp.uint32)
    word = (hi & jnp.uint32(0xFFFF0000)) | lax.shift_right_logical(lo, jnp.uint32(16))
    return lax.bitcast_convert_type(word, jnp.int32)


def _unpack_bf16_pairs(p):
    word = lax.bitcast_convert_type(p, jnp.uint32)
    hi = lax.bitcast_convert_type(word & jnp.uint32(0xFFFF0000), F32)
    lo = lax.bitcast_convert_type(lax.shift_left(word, jnp.uint32(16)), F32)
    return hi, lo


def _out_kernel(ya_ref, yb_ref, h_ref, w_ref, lw_ref, lb_ref, rw_ref, rb_ref, sg_ref, su_ref, sd_ref,
                o_ref, opk_ref, idx_ref, rank_ref, wk_ref, cnt_ref, base_ref, *, tile):
    mixed = jnp.concatenate([ya_ref[...], yb_ref[...]], axis=1)
    z = ALPHA * h_ref[...] + _dot(mixed, w_ref[...])
    out = _layer_norm(z, lw_ref[...], lb_ref[...])
    out16 = out.astype(BF16)
    shared = _dot((_silu(_dot(out16, sg_ref[...])) * _dot(out16, su_ref[...])).astype(BF16), sd_ref[...])
    o_ref[...] = ALPHA * out + shared
    opk_ref[...] = _pack_bf16_pairs(out)
    _route_tile(out, rw_ref, rb_ref, idx_ref, rank_ref, wk_ref, cnt_ref, base_ref, tile)


def _out_proj_ln_route(ya, yb, h, w_out, ln_w, ln_b, router_w, router_bias, sg, su, sd):
    t = h.shape[0]
    tm = OUT_TILE
    const = lambda shape: pl.BlockSpec(shape, lambda i: (0, 0))
    per_tok = lambda dt: jax.ShapeDtypeStruct((TOP_K, t), dt)
    tok_blk = pl.BlockSpec((TOP_K, tm), lambda i: (0, i))
    return pl.pallas_call(
        functools.partial(_out_kernel, tile=tm),
        grid=(t // tm,),
        in_specs=[pl.BlockSpec((tm, MIX_HALF), lambda i: (i, 0)),
                  pl.BlockSpec((tm, MIX_HALF), lambda i: (i, 0)),
                  pl.BlockSpec((tm, D_MODEL), lambda i: (i, 0)),
                  const((D_MODEL, D_MODEL)), const((1, D_MODEL)), const((1, D_MODEL)),
                  const((N_EXPERTS, D_MODEL)), const((N_EXPERTS, 1)),
                  const((D_MODEL, D_EXPERT)), const((D_MODEL, D_EXPERT)), const((D_EXPERT, D_MODEL))],
        out_specs=[pl.BlockSpec((tm, D_MODEL), lambda i: (i, 0)),
                   pl.BlockSpec((tm, PACKED), lambda i: (i, 0)),
                   tok_blk, tok_blk, pl.BlockSpec((tm, TOP_K), lambda i: (i, 0)), const((N_EXPERTS, LANES))],
        out_shape=[jax.ShapeDtypeStruct((t, D_MODEL), F32), jax.ShapeDtypeStruct((t, PACKED), jnp.int32),
                   per_tok(jnp.int32), per_tok(jnp.int32), jax.ShapeDtypeStruct((t, TOP_K), F32),
                   jax.ShapeDtypeStruct((N_EXPERTS, LANES), F32)],
        scratch_shapes=[pltpu.VMEM((N_EXPERTS, LANES), F32)],
        compiler_params=_params("arbitrary"),
        name="out_proj_ln_route",
    )(ya, yb, h, w_out.astype(BF16), ln_w.reshape(1, D_MODEL), ln_b.reshape(1, D_MODEL),
      router_w.T, router_bias.reshape(N_EXPERTS, 1), sg.astype(BF16), su.astype(BF16), sd.astype(BF16))


def _first_index(hit, idx, big):
    return jnp.min(jnp.where(hit, idx, big), axis=0, keepdims=True)


def _route_tile(h, w_ref, b_ref, idx_ref, rank_ref, wk_ref, cnt_ref, base_ref, tile):
    @pl.when(pl.program_id(0) == 0)
    def _():
        base_ref[...] = jnp.zeros_like(base_ref)

    h_hi, h_mid, _ = _split3(h)
    w_hi, w_mid, _ = _split3(w_ref[...])
    logits = _dot_nt(w_hi, h_hi) + _dot_nt(w_hi, h_mid) + _dot_nt(w_mid, h_hi)
    scores = _sigmoid(logits)
    biased = scores + b_ref[...]

    sub = lax.broadcasted_iota(jnp.int32, (GROUP_SIZE, tile), 0)
    grp_rows = []
    for g in range(N_GROUPS):
        xg = biased[g * GROUP_SIZE:(g + 1) * GROUP_SIZE, :]
        m1 = jnp.max(xg, axis=0, keepdims=True)
        i1 = _first_index(xg == m1, sub, GROUP_SIZE)
        m2 = jnp.max(jnp.where(sub == i1, NEG_INF, xg), axis=0, keepdims=True)
        grp_rows.append(m1 + m2)
    gs = jnp.concatenate(grp_rows, axis=0)
    gsel = jnp.zeros((N_GROUPS, tile), F32)
    for _ in range(TOPK_GROUPS):
        mx = jnp.max(gs, axis=0, keepdims=True)
        hit = sub == _first_index(gs == mx, sub, N_GROUPS)
        gsel = jnp.where(hit, 1.0, gsel)
        gs = jnp.where(hit, NEG_INF, gs)
    emask = jnp.concatenate(
        [jnp.broadcast_to(gsel[g:g + 1, :], (GROUP_SIZE, tile)) for g in range(N_GROUPS)], axis=0)

    eidx = lax.broadcasted_iota(jnp.int32, (N_EXPERTS, tile), 0)
    cand = jnp.where(emask > 0.5, biased, NEG_INF)
    sel = jnp.zeros((N_EXPERTS, tile), F32)
    picks = []
    for _ in range(TOP_K):
        mx = jnp.max(cand, axis=0, keepdims=True)
        first = _first_index(cand == mx, eidx, N_EXPERTS)
        hit = eidx == first
        picks.append(first)
        sel = jnp.where(hit, 1.0, sel)
        cand = jnp.where(hit, NEG_INF, cand)
    picked = jnp.where(sel > 0.5, scores, 0.0)
    wts = picked / jnp.sum(picked, axis=0, keepdims=True) * ROUTED_SCALE

    ri = lax.broadcasted_iota(jnp.int32, (tile, tile), 0)
    ci = lax.broadcasted_iota(jnp.int32, (tile, tile), 1)
    before = (ri < ci).astype(BF16)
    prior = _dot(sel.astype(BF16), before) + base_ref[:, 0:1]
    ranks = [jnp.sum(jnp.where(eidx == p, prior, 0.0), axis=0, keepdims=True) for p in picks]
    wsel = [jnp.sum(jnp.where(eidx == p, wts, 0.0), axis=0, keepdims=True) for p in picks]
    idx_ref[...] = jnp.concatenate(picks, axis=0)
    rank_ref[...] = jnp.concatenate(ranks, axis=0).astype(jnp.int32)
    wk_ref[...] = jnp.concatenate(wsel, axis=0).T
    total = base_ref[...] + jnp.sum(sel, axis=1, keepdims=True)
    base_ref[...] = total
    cnt_ref[...] = total


def _silu(x):
    return x * _sigmoid(x)


def _sc_mesh():
    return plsc.VectorSubcoreMesh(core_axis_name="c", subcore_axis_name="s")


def _sc_worker_id():
    return lax.axis_index("s") * SC_CORES + lax.axis_index("c")


def _dispatch_rows(xpk, pos_chunks, n_rows):
    t = xpk.shape[0]
    ch = pos_chunks.shape[2]
    n_ch = t // SC_WORKERS // ch

    @functools.partial(
        pl.kernel, mesh=_sc_mesh(),
        out_type=jax.ShapeDtypeStruct((n_rows, PACKED), jnp.int32),
        scratch_types=[pltpu.VMEM((TOP_K, ch), jnp.int32),
                       pltpu.VMEM((ch, PACKED), jnp.int32),
                       pltpu.SemaphoreType.DMA],
        name="moe_dispatch",
    )
    def scatter(x_hbm, pos_hbm, out_hbm, idx_v, rows_v, sem):
        wid = _sc_worker_id()

        @pl.loop(0, n_ch)
        def _(c):
            chunk = wid * n_ch + c
            off = pl.multiple_of(chunk * ch, ch)
            pltpu.sync_copy(pos_hbm.at[chunk], idx_v)
            pltpu.sync_copy(x_hbm.at[pl.ds(off, ch)], rows_v)
            copies = [pltpu.async_copy(rows_v, out_hbm.at[idx_v.at[k]], sem) for k in range(TOP_K)]
            for cp in copies:
                cp.wait()

    return scatter(xpk, pos_chunks)


def _gather_rows(table, idx):
    n = idx.shape[0]
    per_w = n // SC_WORKERS
    n_ch = per_w // SC_CHUNK
    assert n_ch % 2 == 0 and n_ch >= 2

    @functools.partial(
        pl.kernel, mesh=_sc_mesh(),
        out_type=jax.ShapeDtypeStruct((n, PACKED), jnp.int32),
        scratch_types=[pltpu.VMEM((n_ch, SC_CHUNK), jnp.int32),
                       pltpu.VMEM((SC_CHUNK, PACKED), jnp.int32), pltpu.VMEM((SC_CHUNK, PACKED), jnp.int32),
                       pltpu.SemaphoreType.DMA, pltpu.SemaphoreType.DMA,
                       pltpu.SemaphoreType.DMA, pltpu.SemaphoreType.DMA],
        name="moe_gather",
    )
    def gather(table_hbm, idx_hbm, out_hbm, idx_v, rows0, rows1, g0, g1, w0, w1):
        wid = _sc_worker_id()
        base = wid * per_w
        rows, g_sem, w_sem = (rows0, rows1), (g0, g1), (w0, w1)
        pltpu.sync_copy(idx_hbm.at[wid], idx_v)

        def fetch(c, b):
            return pltpu.make_async_copy(table_hbm.at[idx_v.at[c]], rows[b], g_sem[b])

        def flush(c, b):
            off = pl.multiple_of(base + c * SC_CHUNK, SC_CHUNK)
            return pltpu.make_async_copy(rows[b], out_hbm.at[pl.ds(off, SC_CHUNK)], w_sem[b])

        fetch(0, 0).start()

        @pl.loop(0, n_ch, step=2)
        def _(c0):
            for b in range(2):
                c = c0 + b
                fetch(c, b).wait()
                flush(c, b).start()

                @pl.when(c + 1 < n_ch)
                def _():
                    @pl.when(c >= 1)
                    def _():
                        flush(c - 1, 1 - b).wait()
                    fetch(c + 1, 1 - b).start()

        flush(n_ch - 2, 0).wait()
        flush(n_ch - 1, 1).wait()

    return gather(table, idx.reshape(SC_WORKERS, n_ch, SC_CHUNK))


def _unpacked_bf16(p):
    hi, lo = _unpack_bf16_pairs(p)
    return jnp.concatenate([hi.astype(BF16), lo.astype(BF16)], axis=1)


def _expert_kernel(be_ref, nu_ref, next_ref, slot_ref, xs_hbm, wg_hbm, wu_hbm, wd_hbm, y_ref,
                   g16_ref, u16_ref, d16_ref, gf_ref, uf_ref, df_ref, xbuf_ref, xsem, wsem, *, layer):
    i = pl.program_id(0)
    n_used = nu_ref[0]
    bm = xbuf_ref.shape[1]

    def fetch(b):
        slot = lax.rem(b, XS_SLOTS)
        rows = pl.ds(pl.multiple_of(b * bm, bm), bm)
        return pltpu.make_async_copy(xs_hbm.at[rows], xbuf_ref.at[slot], xsem.at[slot])

    @pl.when(i == 0)
    def _():
        fetch(0).start()

        @pl.when(n_used > 1)
        def _():
            fetch(1).start()

    @pl.when(i + 2 < n_used)
    def _():
        fetch(i + 2).start()

    e = be_ref[i]
    slot = slot_ref[e]
    f32_bufs = (gf_ref, uf_ref, df_ref)

    def wfetch(expert, dst_slot):
        return [pltpu.make_async_copy(w_hbm.at[layer, expert], buf.at[dst_slot], wsem.at[dst_slot, j])
                for j, (w_hbm, buf) in enumerate(zip((wg_hbm, wu_hbm, wd_hbm), f32_bufs))]

    @pl.when(i == 0)
    def _():
        for cp in wfetch(e, slot):
            cp.start()

    first_block_of_expert = jnp.logical_or(i == 0, e != be_ref[jnp.maximum(i - 1, 0)])

    @pl.when(jnp.logical_and(first_block_of_expert, i < n_used))
    def _():
        for cp in wfetch(e, slot):
            cp.wait()
        g16_ref[slot] = gf_ref[slot].astype(BF16)
        u16_ref[slot] = uf_ref[slot].astype(BF16)
        d16_ref[slot] = df_ref[slot].astype(BF16)
        nxt = next_ref[e]

        @pl.when(nxt >= 0)
        def _():
            for cp in wfetch(nxt, 1 - slot):
                cp.start()

    @pl.when(i < n_used)
    def _():
        fetch(i).wait()
        x = _unpacked_bf16(xbuf_ref[lax.rem(i, XS_SLOTS)])
        hh = _silu(_dot(x, g16_ref[slot])) * _dot(x, u16_ref[slot])
        y_ref[...] = _pack_bf16_pairs(_dot(hh.astype(BF16), d16_ref[slot]))


def _experts(block_e, n_used, next_expert, weight_slot, xs, wg, wu, wd, layer):
    nb = block_e.shape[0]
    bm = MOE_BLOCK
    hbm = pl.BlockSpec(memory_space=pl.ANY)
    two = lambda shape, dt: pltpu.VMEM((2,) + shape, dt)
    grid_spec = pltpu.PrefetchScalarGridSpec(
        num_scalar_prefetch=4,
        grid=(nb,),
        in_specs=[hbm, hbm, hbm, hbm],
        out_specs=pl.BlockSpec((bm, PACKED), lambda i, be, nu, nx, sl: (jnp.minimum(i, nu[0] - 1), 0)),
        scratch_shapes=[two((D_MODEL, D_EXPERT), BF16), two((D_MODEL, D_EXPERT), BF16), two((D_EXPERT, D_MODEL), BF16),
                        two((D_MODEL, D_EXPERT), F32), two((D_MODEL, D_EXPERT), F32), two((D_EXPERT, D_MODEL), F32),
                        pltpu.VMEM((XS_SLOTS, bm, PACKED), jnp.int32), pltpu.SemaphoreType.DMA((XS_SLOTS,)),
                        pltpu.SemaphoreType.DMA((2, 3))],
    )
    return pl.pallas_call(
        functools.partial(_expert_kernel, layer=layer),
        grid_spec=grid_spec,
        out_shape=jax.ShapeDtypeStruct((nb * bm, PACKED), jnp.int32),
        compiler_params=_params("arbitrary"),
        name="moe_experts",
    )(block_e, n_used, next_expert, weight_slot, xs, wg, wu, wd)


def _combine_kernel(g_ref, wk_ref, base_ref, lw_ref, lb_ref, *rest, with_proj):
    if with_proj:
        (wm_ref, ws_ref), (o_ref, proj_ref, small_ref) = rest[:2], rest[-3:]
        next_proj = (wm_ref, ws_ref, proj_ref, small_ref)
    else:
        o_ref, next_proj = rest[-1], None
    acc_hi = base_ref[:, :PACKED]
    acc_lo = base_ref[:, PACKED:]
    wk = wk_ref[...]
    for k in range(TOP_K):
        y_hi, y_lo = _unpack_bf16_pairs(g_ref[k])
        w = wk[:, k:k + 1]
        acc_hi = acc_hi + w * y_hi
        acc_lo = acc_lo + w * y_lo
    out = _layer_norm(jnp.concatenate([acc_hi, acc_lo], axis=1), lw_ref[...], lb_ref[...])
    o_ref[...] = out
    if next_proj is not None:
        wm_ref, ws_ref, proj_ref, small_ref = next_proj
        out16 = out.astype(BF16)
        proj_ref[...] = _dot(out16, wm_ref[...])
        small_ref[...] = _dot(out16, ws_ref[...])


def _combine_ln(g, wk, base, ln_w, ln_b, next_w, part, prev):
    t = base.shape[0]
    tm = COMBINE_TILE if next_w is not None else 2 * COMBINE_TILE
    n_blk = g.shape[1] // tm
    first = part * n_blk
    const = lambda shape: pl.BlockSpec(shape, lambda i: (0, 0))
    rows = lambda width: pl.BlockSpec((tm, width), lambda i: (i + first, 0))
    in_specs = [pl.BlockSpec((TOP_K, tm, PACKED), lambda i: (0, i, 0)), rows(TOP_K), rows(D_MODEL),
                const((1, D_MODEL)), const((1, D_MODEL))]
    args = [g, wk, base, ln_w.reshape(1, D_MODEL), ln_b.reshape(1, D_MODEL)]
    out_specs = [rows(D_MODEL)]
    out_shape = [jax.ShapeDtypeStruct((t, D_MODEL), F32)]
    if next_w is not None:
        w_main, w_small = next_w
        n = w_main.shape[1]
        in_specs += [const((D_MODEL, n)), const((D_MODEL, LANES))]
        args += [w_main, w_small]
        out_specs += [rows(n), rows(LANES)]
        out_shape += [jax.ShapeDtypeStruct((t, n), F32), jax.ShapeDtypeStruct((t, LANES), F32)]
    aliases = {}
    if prev is not None:
        aliases = {len(args) + k: k for k in range(len(prev))}
        in_specs += [pl.BlockSpec(memory_space=pl.ANY)] * len(prev)
        args += list(prev)
    return pl.pallas_call(
        functools.partial(_combine_kernel, with_proj=next_w is not None),
        grid=(n_blk,),
        in_specs=in_specs,
        out_specs=out_specs,
        out_shape=out_shape,
        input_output_aliases=aliases,
        compiler_params=_params("parallel"),
        name="moe_combine_ln",
    )(*args)


def _moe_ln(base, hpk, idx, rank, wk, counts, wg, wu, wd, layer, ln_w, ln_b, next_w):
    t = base.shape[0]
    cnt = counts[:, 0].astype(jnp.int32)
    padded = (cnt + MOE_BLOCK - 1) // MOE_BLOCK * MOE_BLOCK
    pend = jnp.cumsum(padded)
    experts = jnp.arange(N_EXPERTS, dtype=jnp.int32)
    pstart_of_pick = jnp.sum(jnp.where(idx[:, :, None] == experts, pend - padded, 0), axis=-1)
    pos = pstart_of_pick + rank
    nb = -(-(t * TOP_K + N_EXPERTS * (MOE_BLOCK - 1)) // MOE_BLOCK)
    starts = jnp.arange(nb, dtype=jnp.int32) * MOE_BLOCK
    block_e = jnp.minimum(jnp.sum((pend[None, :] <= starts[:, None]).astype(jnp.int32), axis=1), N_EXPERTS - 1)
    n_used = (pend[-1] // MOE_BLOCK).astype(jnp.int32).reshape(1)
    has_rows = cnt > 0
    later = jnp.logical_and(has_rows[None, :], experts[None, :] > experts[:, None])
    next_expert = jnp.min(jnp.where(later, experts[None, :], N_EXPERTS), axis=1)
    next_expert = jnp.where(next_expert == N_EXPERTS, -1, next_expert).astype(jnp.int32)
    weight_slot = ((jnp.cumsum(has_rows) - has_rows) % 2).astype(jnp.int32)
    pos_chunks = pos.reshape(TOP_K, t // SC_DISPATCH_CHUNK, SC_DISPATCH_CHUNK).transpose(1, 0, 2)
    xs = _dispatch_rows(hpk, pos_chunks, nb * MOE_BLOCK)
    ys = _experts(block_e, n_used, next_expert, weight_slot, xs, wg, wu, wd, layer)
    part = t // COMBINE_PARTS
    gathered = [_gather_rows(ys, pos[:, p * part:(p + 1) * part].reshape(-1)).reshape(TOP_K, part, PACKED)
                for p in range(COMBINE_PARTS)]
    outs = None
    for p in range(COMBINE_PARTS):
        outs = _combine_ln(gathered[p], wk, base, ln_w, ln_b, next_w, p, outs)
    return outs


def _pad_cols(w, width=LANES):
    return jnp.pad(w, ((0, 0), (0, width - w.shape[1])))


def _even_proj_weights(w_in):
    a4 = 4 * MIX_HALF
    ng = 2 * HEADS
    w_main = jnp.concatenate([w_in[:, :a4], w_in[:, a4 + ng:]], axis=1).astype(BF16)
    w_gate = _pad_cols(w_in[:, a4:a4 + ng]).astype(BF16)
    return w_main, w_gate


def _even_mixer(proj, gates, batch, seq, gate_b, norm_w, conv_w, conv_b, wa, ba, wx, bx, lam):
    ya = _mlstm(proj, gates, gate_b, norm_w, batch, seq)
    yb = _rglru(proj, conv_w, conv_b, wa, ba, wx, bx, lam, batch, seq)
    return ya, yb


def _odd_proj_weights(w_in):
    c0 = MIX_HALF
    c1 = c0 + HEADS * GLA_DK
    c2 = c1 + HEADS * GLA_DK
    c3 = c2 + MIX_HALF
    c4 = c3 + MIX_HALF
    w_main = jnp.concatenate([w_in[:, :c0], _pad_heads(w_in[:, c0:c1], 1), _pad_heads(w_in[:, c1:c2], 1),
                              w_in[:, c2:c4]], axis=1).astype(BF16)
    w_low = _pad_cols(w_in[:, c4:]).astype(BF16)
    return w_main, w_low


def _odd_mixer(proj, glow, batch, seq, lam_re, lam_im, b_re, b_im, c_re, c_im, d_skip, log_dt,
               glu_w, glu_b, gate_w, gate_b, norm_w):
    tables = _s5_tables(lam_re, lam_im, b_re, b_im, c_re, c_im, log_dt)
    yc = _s5(proj, tables, d_skip, glu_w, glu_b, batch, seq)
    gw = jnp.pad(_pad_heads(gate_w, 1), ((0, LANES - GLA_GATE_RANK), (0, 0))).astype(BF16)
    gb = _pad_heads(gate_b.reshape(1, -1), 1)
    yd = _gla(proj, glow, gw, gb, norm_w, batch, seq)
    return yc, yd


def kernel(x, ln1_w, ln1_b, ln2_w, ln2_b, w_out, w_in_even, mlstm_gate_b, mlstm_norm_w, lru_conv_w, lru_conv_b, lru_wa, lru_ba, lru_wx, lru_bx, lru_lambda, w_in_odd, s5_lam_re, s5_lam_im, s5_b_re, s5_b_im, s5_c_re, s5_c_im, s5_d, s5_log_dt, s5_glu_w, s5_glu_b, gla_gate_w, gla_gate_b, gla_norm_w, router_w, router_bias, exp_w_gate, exp_w_up, exp_w_down, sh_w_gate, sh_w_up, sh_w_down):
    batch, seq, d = x.shape
    proj_w = [_even_proj_weights(w_in_even[layer // 2]) if layer % 2 == 0 else _odd_proj_weights(w_in_odd[layer // 2])
              for layer in range(DEPTH)]
    h = x.reshape(batch * seq, d)
    proj, small = _proj(h, *proj_w[0])
    for layer in range(DEPTH):
        j = layer // 2
        if layer % 2 == 0:
            y1, y2 = _even_mixer(proj, small, batch, seq, mlstm_gate_b[j], mlstm_norm_w[j],
                                 lru_conv_w[j], lru_conv_b[j], lru_wa[j], lru_ba[j], lru_wx[j],
                                 lru_bx[j], lru_lambda[j])
        else:
            y1, y2 = _odd_mixer(proj, small, batch, seq, s5_lam_re[j], s5_lam_im[j], s5_b_re[j],
                                s5_b_im[j], s5_c_re[j], s5_c_im[j], s5_d[j], s5_log_dt[j],
                                s5_glu_w[j], s5_glu_b[j], gla_gate_w[j], gla_gate_b[j], gla_norm_w[j])
        base, hpk, idx, rank, wk, counts = _out_proj_ln_route(
            y1, y2, h, w_out[layer], ln1_w[layer], ln1_b[layer], router_w[layer], router_bias[layer],
            sh_w_gate[layer], sh_w_up[layer], sh_w_down[layer])
        next_w = proj_w[layer + 1] if layer + 1 < DEPTH else None
        res = _moe_ln(base, hpk, idx, rank, wk, counts, exp_w_gate, exp_w_up, exp_w_down, layer,
                      ln2_w[layer], ln2_b[layer], next_w)
        if next_w is None:
            (h,) = res
        else:
            h, proj, small = res
    return h.reshape(batch, seq, d)
```

```python
import functools
import math

import jax
import jax.numpy as jnp
from jax import lax
from jax.experimental import pallas as pl
from jax.experimental.pallas import tpu as pltpu
from jax.experimental.pallas import tpu_sc as plsc

F32 = jnp.float32
BF16 = jnp.bfloat16

D_MODEL = 1024
DEPTH = 2
MIX_HALF = 512
HEADS = 4
HEAD_DIM = 128
GLA_DK = 64
GLA_CHUNK = 64
GLA_GATE_RANK = 16
GLA_GATE_TEMP = 16.0
LRU_C = 8.0
LRU_CONV = 4
S5_GROUP = 16
S5_GROUPS = 32
S5_STATE = 64
S5_LANES = S5_GROUPS * S5_STATE
S5_BLOCKS = 4
N_EXPERTS = 64
N_GROUPS = 8
GROUP_SIZE = N_EXPERTS // N_GROUPS
TOP_K = 8
TOPK_GROUPS = 4
D_EXPERT = 256
ROUTED_SCALE = 2.5
ALPHA = (2.0 * DEPTH) ** 0.25
EPS = 1e-5
LANES = 128
SUBLANES = 8
NEG_INF = float("-inf")

VMEM_LIMIT = 56 * 1024 * 1024

MLSTM_CHUNK = 128
MLSTM_TILE = 1024
MLSTM_UNROLL = 2
LRU_TILE = 1024
LRU_LOG_STEPS = 3
LRU_UNROLL = True
S5_TILE = 512
S5_LOG_STEPS = 3
S5_UNROLL = True
GLA_UNROLL = 16
GLA_TILE = 1024
PROJ_TILE = 1024
OUT_TILE = 1024
MOE_BLOCK = 1152
XS_SLOTS = 3
COMBINE_TILE = 256
COMBINE_PARTS = 2
PACKED = D_MODEL // 2
SC_CHUNK = 64
SC_DISPATCH_CHUNK = 128
SC_CORES = 2
SC_SUBCORES = 16
SC_WORKERS = SC_CORES * SC_SUBCORES


def _params(*sem):
    return pltpu.CompilerParams(dimension_semantics=sem, vmem_limit_bytes=VMEM_LIMIT)


def _split3(x):
    hi = x.astype(BF16)
    r1 = x - hi.astype(F32)
    mid = r1.astype(BF16)
    lo = (r1 - mid.astype(F32)).astype(BF16)
    return hi, mid, lo


def _dot(a, b):
    return jnp.dot(a, b, preferred_element_type=F32)


def _dot_nt(a, b):
    return lax.dot_general(a, b, (((1,), (1,)), ((), ())), preferred_element_type=F32)


def _dot_tn(a, b):
    return lax.dot_general(a, b, (((0,), (0,)), ((), ())), preferred_element_type=F32)


def _exact_left01(mask01_bf16, x):
    hi, mid, lo = _split3(x)
    return _dot(mask01_bf16, hi) + _dot(mask01_bf16, mid) + _dot(mask01_bf16, lo)


def _exact_right01(x, mask01_bf16):
    hi, mid, lo = _split3(x)
    return _dot(hi, mask01_bf16) + _dot(mid, mask01_bf16) + _dot(lo, mask01_bf16)


def _log_sigmoid(x):
    return jnp.minimum(x, 0.0) - jnp.log(1.0 + jnp.exp(-jnp.abs(x)))


def _sigmoid(x):
    return 1.0 / (1.0 + jnp.exp(-x))


def _gelu_tanh(x):
    c = math.sqrt(2.0 / math.pi)
    return 0.5 * x * (1.0 + jnp.tanh(c * (x + 0.044715 * (x * x * x))))


def _layer_norm(z, w, b):
    mu = jnp.mean(z, axis=-1, keepdims=True)
    zc = z - mu
    return zc * lax.rsqrt(jnp.mean(zc * zc, axis=-1, keepdims=True) + EPS) * w + b


def _proj_kernel(x_ref, w_ref, wg_ref, o_ref, og_ref):
    x = x_ref[...].astype(BF16)
    o_ref[...] = _dot(x, w_ref[...])
    og_ref[...] = _dot(x, wg_ref[...])


def _proj(x, w_main, w_small):
    t, d = x.shape
    n = w_main.shape[1]
    tm = PROJ_TILE
    return pl.pallas_call(
        _proj_kernel,
        grid=(t // tm,),
        in_specs=[pl.BlockSpec((tm, d), lambda i: (i, 0)),
                  pl.BlockSpec((d, n), lambda i: (0, 0)),
                  pl.BlockSpec((d, LANES), lambda i: (0, 0))],
        out_specs=[pl.BlockSpec((tm, n), lambda i: (i, 0)),
                   pl.BlockSpec((tm, LANES), lambda i: (i, 0))],
        out_shape=[jax.ShapeDtypeStruct((t, n), F32), jax.ShapeDtypeStruct((t, LANES), F32)],
        compiler_params=_params("parallel"),
        name="in_proj",
    )(x, w_main, w_small)


def _mlstm_kernel(q_ref, k_ref, v_ref, o_ref, gc_ref, gr_ref, bc_ref, br_ref, nw_ref,
                  y_ref, c_ref, m_ref, *, chunk, n_chunks):
    L = chunk

    @pl.when(pl.program_id(1) == 0)
    def _():
        c_ref[...] = jnp.zeros_like(c_ref)
        m_ref[...] = jnp.zeros_like(m_ref)

    ri = lax.broadcasted_iota(jnp.int32, (L, L), 0)
    ci = lax.broadcasted_iota(jnp.int32, (L, L), 1)
    causal = ci <= ri
    tril = causal.astype(BF16)
    triu = (ri <= ci).astype(BF16)
    ones_v = jnp.ones((L, HEAD_DIM), BF16)
    scale = HEAD_DIM ** -0.5

    def body(c, carry):
        r0 = pl.multiple_of(c * L, L)
        g_col = gc_ref[pl.ds(r0, L), :] + bc_ref[...]
        g_row = gr_ref[c] + br_ref[...]
        b_col_all = _exact_left01(tril, _log_sigmoid(g_col))
        b_row_all = _exact_right01(_log_sigmoid(g_row), triu)
        for h in range(HEADS):
            lo = h * HEAD_DIM
            q = q_ref[pl.ds(r0, L), lo:lo + HEAD_DIM].astype(BF16)
            k = k_ref[pl.ds(r0, L), lo:lo + HEAD_DIM] * scale
            v = v_ref[pl.ds(r0, L), lo:lo + HEAD_DIM].astype(BF16)
            v_aug = jnp.concatenate([v, ones_v], axis=1)
            i_rep = jnp.broadcast_to(g_col[:, h:h + 1], (L, LANES))
            b_rep = jnp.broadcast_to(b_col_all[:, HEADS + h:HEADS + h + 1], (L, LANES))
            i_row = g_row[h:h + 1, :]
            b_row = b_row_all[HEADS + h:HEADS + h + 1, :]
            b_last = b_rep[L - 1:L, :]
            m_prev = m_ref[h:h + 1, :]
            c_prev = c_ref[h]

            d_mat = jnp.where(causal, b_rep - b_row + i_row, NEG_INF)
            m_inter = b_rep + m_prev
            m_i = jnp.maximum(m_inter, jnp.max(d_mat, axis=1, keepdims=True))
            s = _dot_nt(q, k.astype(BF16)) * jnp.exp(d_mat - m_i)
            w_inter = jnp.exp(m_inter - m_i)
            intra = _dot(s.astype(BF16), v_aug)
            inter = _dot(q, c_prev.astype(BF16))
            num = intra[:, :HEAD_DIM] + w_inter * inter[:, :HEAD_DIM]
            den = intra[:, HEAD_DIM:] + w_inter * inter[:, HEAD_DIM:]
            hh = num / jnp.maximum(jnp.abs(den), jnp.exp(-m_i))

            w_loc = b_last - b_rep + i_rep
            m_loc = jnp.max(w_loc, axis=0, keepdims=True)
            kp = (k * jnp.exp(w_loc - m_loc)).astype(BF16)
            c_loc = _dot_tn(kp, v_aug)
            m_new = jnp.maximum(b_last + m_prev, m_loc)
            keep = jnp.exp(b_last + m_prev - m_new)
            add = jnp.exp(m_loc - m_new)
            c_ref[h] = (jnp.concatenate([keep, keep], axis=1) * c_prev
                        + jnp.concatenate([add, add], axis=1) * c_loc)
            m_ref[h:h + 1, :] = m_new

            hc = hh - jnp.mean(hh, axis=-1, keepdims=True)
            yn = hc * lax.rsqrt(jnp.mean(hc * hc, axis=-1, keepdims=True) + EPS)
            og = o_ref[pl.ds(r0, L), lo:lo + HEAD_DIM]
            y_ref[pl.ds(r0, L), lo:lo + HEAD_DIM] = (yn * nw_ref[:, lo:lo + HEAD_DIM] * _sigmoid(og)).astype(BF16)
        return carry

    lax.fori_loop(0, n_chunks, body, 0, unroll=MLSTM_UNROLL)


def _mlstm(proj, gates, gate_b, norm_w, batch, seq):
    t = batch * seq
    L = MLSTM_CHUNK
    assert L == LANES, "the kernel keeps per-row gate terms replicated over one vreg of lanes"
    ts = MLSTM_TILE
    nj = seq // ts
    nc = ts // L
    g_row = gates[:, :2 * HEADS].reshape(t // L, L, 2 * HEADS).transpose(0, 2, 1)
    b_col = jnp.zeros((1, LANES), F32).at[0, :2 * HEADS].set(gate_b)
    b_row = gate_b.reshape(2 * HEADS, 1)
    blk = lambda col: pl.BlockSpec((ts, MIX_HALF), lambda b, j, col=col: (b * nj + j, col))
    kern = functools.partial(_mlstm_kernel, chunk=L, n_chunks=nc)
    return pl.pallas_call(
        kern,
        grid=(batch, nj),
        in_specs=[blk(0), blk(1), blk(2), blk(3),
                  pl.BlockSpec((ts, LANES), lambda b, j: (b * nj + j, 0)),
                  pl.BlockSpec((nc, 2 * HEADS, L), lambda b, j: (b * nj + j, 0, 0)),
                  pl.BlockSpec((1, LANES), lambda b, j: (0, 0)),
                  pl.BlockSpec((2 * HEADS, 1), lambda b, j: (0, 0)),
                  pl.BlockSpec((1, MIX_HALF), lambda b, j: (0, 0))],
        out_specs=pl.BlockSpec((ts, MIX_HALF), lambda b, j: (b * nj + j, 0)),
        out_shape=jax.ShapeDtypeStruct((t, MIX_HALF), BF16),
        scratch_shapes=[pltpu.VMEM((HEADS, HEAD_DIM, 2 * HEAD_DIM), F32),
                        pltpu.VMEM((8, LANES), F32)],
        compiler_params=_params("arbitrary", "arbitrary"),
        name="mlstm",
    )(proj, proj, proj, proj, gates, g_row, b_col, b_row, norm_w.reshape(1, MIX_HALF))


def _rglru_kernel(xb_ref, gb_ref, cw_ref, cb_ref, wa_ref, ba_ref, wx_ref, bx_ref, lam_ref,
                  y_ref, xext_ref, h_ref, a_ref, u_ref, *, tile):
    @pl.when(pl.program_id(1) == 0)
    def _():
        xext_ref[0:8, :] = jnp.zeros((8, MIX_HALF), F32)
        h_ref[...] = jnp.zeros_like(h_ref)

    x = xb_ref[...]
    xext_ref[8:8 + tile, :] = x
    xc = cb_ref[...] + cw_ref[LRU_CONV - 1:LRU_CONV, :] * x
    for tap in range(LRU_CONV - 1):
        back = LRU_CONV - 1 - tap
        xc = xc + cw_ref[tap:tap + 1, :] * xext_ref[8 - back:8 - back + tile, :]
    xext_ref[0:8, :] = x[tile - 8:tile, :]

    xc16 = xc.astype(BF16)
    r_parts, i_parts = [], []
    for h in range(HEADS):
        lo = h * HEAD_DIM
        xh = xc16[:, lo:lo + HEAD_DIM]
        r_parts.append(_dot(xh, wa_ref[h]))
        i_parts.append(_dot(xh, wx_ref[h]))
    r = _sigmoid(jnp.concatenate(r_parts, axis=1) + ba_ref[...])
    ig = _sigmoid(jnp.concatenate(i_parts, axis=1) + bx_ref[...])
    lam = lam_ref[...]
    softplus_neg = jnp.maximum(-lam, 0.0) + jnp.log(1.0 + jnp.exp(-jnp.abs(lam)))
    log_a = -LRU_C * r * softplus_neg
    a = jnp.exp(log_a)
    th = jnp.tanh(log_a)
    u = jnp.sqrt(-2.0 * th / (1.0 - th)) * ig * xc

    a_ref[...] = a
    u_ref[...] = u
    rows = lax.broadcasted_iota(jnp.int32, (SUBLANES, MIX_HALF), 0)

    def group(i, h_prev):
        r0 = pl.multiple_of(i * SUBLANES, SUBLANES)
        ag = a_ref[pl.ds(r0, SUBLANES), :]
        ug = u_ref[pl.ds(r0, SUBLANES), :]
        for k in range(LRU_LOG_STEPS):
            keep = rows >= (1 << k)
            ug = ag * jnp.where(keep, pltpu.roll(ug, 1 << k, 0), 0.0) + ug
            ag = ag * jnp.where(keep, pltpu.roll(ag, 1 << k, 0), 1.0)
        hg = ug + ag * h_prev
        u_ref[pl.ds(r0, SUBLANES), :] = hg
        return hg[SUBLANES - 1:SUBLANES, :]

    h_last = lax.fori_loop(0, tile // SUBLANES, group, h_ref[0:1, :], unroll=LRU_UNROLL)
    h_ref[...] = jnp.broadcast_to(h_last, h_ref.shape)
    y_ref[...] = (u_ref[...] * _gelu_tanh(gb_ref[...])).astype(BF16)


def _rglru(proj, conv_w, conv_b, wa, ba, wx, bx, lam, batch, seq):
    t = batch * seq
    ts = LRU_TILE
    nj = seq // ts
    row = lambda a: a.reshape(1, MIX_HALF)
    const2 = lambda shape: pl.BlockSpec(shape, lambda b, j: (0, 0))
    const3 = lambda shape: pl.BlockSpec(shape, lambda b, j: (0, 0, 0))
    blk = lambda col: pl.BlockSpec((ts, MIX_HALF), lambda b, j, col=col: (b * nj + j, col))
    return pl.pallas_call(
        functools.partial(_rglru_kernel, tile=ts),
        grid=(batch, nj),
        in_specs=[blk(4), blk(5), const2((LRU_CONV, MIX_HALF)), const2((1, MIX_HALF)),
                  const3((HEADS, HEAD_DIM, HEAD_DIM)), const2((1, MIX_HALF)),
                  const3((HEADS, HEAD_DIM, HEAD_DIM)), const2((1, MIX_HALF)), const2((1, MIX_HALF))],
        out_specs=pl.BlockSpec((ts, MIX_HALF), lambda b, j: (b * nj + j, 0)),
        out_shape=jax.ShapeDtypeStruct((t, MIX_HALF), BF16),
        scratch_shapes=[pltpu.VMEM((ts + 8, MIX_HALF), F32), pltpu.VMEM((8, MIX_HALF), F32),
                        pltpu.VMEM((ts, MIX_HALF), F32), pltpu.VMEM((ts, MIX_HALF), F32)],
        compiler_params=_params("arbitrary", "arbitrary"),
        name="rglru",
    )(proj, proj, conv_w, row(conv_b), wa.astype(BF16), row(ba), wx.astype(BF16), row(bx), row(lam))


def _s5_kernel(u_ref, bre_ref, bim_ref, cre_ref, cim_ref, mre_ref, mim_ref, pre_ref, pim_ref, d_ref, gw_ref,
               gb_ref, y_ref, xr_ref, xi_ref, cr_ref, ci_ref, *, tile):
    @pl.when(pl.program_id(1) == 0)
    def _():
        cr_ref[...] = jnp.zeros_like(cr_ref)
        ci_ref[...] = jnp.zeros_like(ci_ref)

    u = u_ref[...]
    u16 = u.astype(BF16)
    blk_c = MIX_HALF // S5_BLOCKS
    blk_s = S5_LANES // S5_BLOCKS
    parts = []
    for j in range(S5_BLOCKS):
        lanes = slice(j * blk_s, (j + 1) * blk_s)
        uj = u16[:, j * blk_c:(j + 1) * blk_c]
        xr_ref[:, lanes] = _dot(uj, bre_ref[j])
        xi_ref[:, lanes] = _dot(uj, bim_ref[j])

        def group(i, carry, lanes=lanes):
            cr, ci = carry
            r0 = pl.multiple_of(i * SUBLANES, SUBLANES)
            xr = xr_ref[pl.ds(r0, SUBLANES), lanes]
            xi = xi_ref[pl.ds(r0, SUBLANES), lanes]
            for k in range(S5_LOG_STEPS):
                sr = pltpu.roll(xr, 1 << k, 0)
                si = pltpu.roll(xi, 1 << k, 0)
                mr = mre_ref[k, :, lanes]
                mi = mim_ref[k, :, lanes]
                xr, xi = xr + mr * sr - mi * si, xi + mr * si + mi * sr
            pr = pre_ref[:, lanes]
            pi = pim_ref[:, lanes]
            xr, xi = xr + pr * cr - pi * ci, xi + pr * ci + pi * cr
            xr_ref[pl.ds(r0, SUBLANES), lanes] = xr
            xi_ref[pl.ds(r0, SUBLANES), lanes] = xi
            return xr[SUBLANES - 1:SUBLANES, :], xi[SUBLANES - 1:SUBLANES, :]

        cr, ci = lax.fori_loop(0, tile // SUBLANES, group, (cr_ref[0:1, lanes], ci_ref[0:1, lanes]),
                               unroll=S5_UNROLL)
        cr_ref[0:1, lanes] = cr
        ci_ref[0:1, lanes] = ci
        parts.append(_dot(xr_ref[:, lanes].astype(BF16), cre_ref[j])
                     - _dot(xi_ref[:, lanes].astype(BF16), cim_ref[j]))
    y = jnp.concatenate(parts, axis=1) + d_ref[...] * u
    g = _gelu_tanh(y)
    y_ref[...] = (g * _sigmoid(_dot(g.astype(BF16), gw_ref[...]) + gb_ref[...])).astype(BF16)


def _s5_tables(lam_re, lam_im, b_re, b_im, c_re, c_im, log_dt):
    lr, li = lam_re.astype(F32), lam_im.astype(F32)
    dt = jnp.exp(log_dt.astype(F32))[:, None]
    mag = jnp.exp(lr * dt)
    abar_re = mag * jnp.cos(li * dt)
    abar_im = mag * jnp.sin(li * dt)
    den = lr * lr + li * li
    nr = abar_re - 1.0
    coef_re = (nr * lr + abar_im * li) / den
    coef_im = (abar_im * lr - nr * li) / den
    bbar_re = coef_re[..., None] * b_re - coef_im[..., None] * b_im
    bbar_im = coef_re[..., None] * b_im + coef_im[..., None] * b_re
    gpb = S5_GROUPS // S5_BLOCKS
    eye = jnp.eye(gpb, dtype=F32)

    def in_map(bb):
        bb = bb.reshape(S5_BLOCKS, gpb, S5_STATE, S5_GROUP)
        return jnp.einsum("jgph,gk->jghkp", bb, eye).reshape(S5_BLOCKS, gpb * S5_GROUP, gpb * S5_STATE)

    def out_map(cc):
        cc = cc.reshape(S5_BLOCKS, gpb, S5_GROUP, S5_STATE)
        return jnp.einsum("jghp,gk->jgpkh", cc, eye).reshape(S5_BLOCKS, gpb * S5_STATE, gpb * S5_GROUP)

    def power(n):
        n = jnp.asarray(n, F32)[..., None, None]
        pmag = jnp.exp(n * (lr * dt))
        shape = n.shape[:-2] + (S5_LANES,)
        return (pmag * jnp.cos(n * (li * dt))).reshape(shape), (pmag * jnp.sin(n * (li * dt))).reshape(shape)

    row = jnp.arange(SUBLANES)
    step = 2 ** jnp.arange(S5_LOG_STEPS)
    s_re, s_im = power(step)
    keep = (row[None, :] >= step[:, None])[..., None]
    m_re = jnp.where(keep, s_re[:, None, :], 0.0)
    m_im = jnp.where(keep, s_im[:, None, :], 0.0)
    p_re, p_im = power(row + 1)
    return (in_map(bbar_re).astype(BF16), in_map(bbar_im).astype(BF16),
            out_map(c_re.astype(F32)).astype(BF16), out_map(c_im.astype(F32)).astype(BF16),
            m_re, m_im, p_re, p_im)


def _s5(proj, tables, d_skip, glu_w, glu_b, batch, seq):
    t = batch * seq
    ts = S5_TILE
    nj = seq // ts
    bre, bim, cre, cim, m_re, m_im, p_re, p_im = tables
    blk_c = MIX_HALF // S5_BLOCKS
    blk_s = S5_LANES // S5_BLOCKS
    const2 = lambda shape: pl.BlockSpec(shape, lambda b, j: (0, 0))
    const3 = lambda shape: pl.BlockSpec(shape, lambda b, j: (0, 0, 0))
    return pl.pallas_call(
        functools.partial(_s5_kernel, tile=ts),
        grid=(batch, nj),
        in_specs=[pl.BlockSpec((ts, MIX_HALF), lambda b, j: (b * nj + j, 0)),
                  const3((S5_BLOCKS, blk_c, blk_s)), const3((S5_BLOCKS, blk_c, blk_s)),
                  const3((S5_BLOCKS, blk_s, blk_c)), const3((S5_BLOCKS, blk_s, blk_c)),
                  const3(m_re.shape), const3(m_im.shape), const2(p_re.shape), const2(p_im.shape),
                  const2((1, MIX_HALF)), const2((MIX_HALF, MIX_HALF)), const2((1, MIX_HALF))],
        out_specs=pl.BlockSpec((ts, MIX_HALF), lambda b, j: (b * nj + j, 0)),
        out_shape=jax.ShapeDtypeStruct((t, MIX_HALF), BF16),
        scratch_shapes=[pltpu.VMEM((ts, S5_LANES), F32), pltpu.VMEM((ts, S5_LANES), F32),
                        pltpu.VMEM((8, S5_LANES), F32), pltpu.VMEM((8, S5_LANES), F32)],
        compiler_params=_params("arbitrary", "arbitrary"),
        name="s5",
    )(proj, bre, bim, cre, cim, m_re, m_im, p_re, p_im, d_skip.reshape(1, MIX_HALF), glu_w.astype(BF16),
      glu_b.reshape(1, MIX_HALF))


def _gla_kernel(q_ref, k_ref, v_ref, r_ref, gl_ref, gw_ref, gb_ref, nw_ref, y_ref,
                st_ref, qd_ref, ki_ref, ke_ref, v16_ref, dec_ref, o_ref, *, tile, chunk):
    L = chunk
    nc = tile // L

    @pl.when(pl.program_id(1) == 0)
    def _():
        st_ref[...] = jnp.zeros_like(st_ref)

    z = _dot(gl_ref[...].astype(BF16), gw_ref[...]) + gb_ref[...]
    bcum = _log_sigmoid(z) * (1.0 / GLA_GATE_TEMP)
    row_in_chunk = lax.broadcasted_iota(jnp.int32, bcum.shape, 0) & (L - 1)
    s = 1
    while s < L:
        bcum = bcum + jnp.where(row_in_chunk >= s, pltpu.roll(bcum, s, 0), 0.0)
        s *= 2
    b3 = bcum.reshape(nc, L, MIX_HALF)
    b_last = b3[:, L - 1:L, :]
    k = k_ref[...]
    qd_ref[...] = (q_ref[...] * (GLA_DK ** -0.5) * jnp.exp(bcum)).astype(BF16)
    ki_ref[...] = (k * jnp.exp(-bcum)).astype(BF16)
    ke_ref[...] = (k.reshape(nc, L, MIX_HALF) * jnp.exp(b_last - b3)).reshape(tile, MIX_HALF).astype(BF16)
    v16_ref[...] = v_ref[...].astype(BF16)
    dec_ref[...] = jnp.exp(b_last)

    ri = lax.broadcasted_iota(jnp.int32, (L, L), 0)
    ci = lax.broadcasted_iota(jnp.int32, (L, L), 1)
    causal = ci <= ri

    def body(c, carry):
        r0 = pl.multiple_of(c * L, L)
        dec = dec_ref[c]
        for h in range(HEADS):
            lo = h * HEAD_DIM
            q_dec = qd_ref[pl.ds(r0, L), lo:lo + HEAD_DIM]
            v = v16_ref[pl.ds(r0, L), lo:lo + HEAD_DIM]
            st = st_ref[h]
            att = jnp.where(causal, _dot_nt(q_dec, ki_ref[pl.ds(r0, L), lo:lo + HEAD_DIM]), 0.0)
            o_ref[pl.ds(r0, L), lo:lo + HEAD_DIM] = (_dot(att.astype(BF16), v)
                                                     + _dot_nt(q_dec, st.astype(BF16)))
            st_ref[h] = dec[:, lo:lo + HEAD_DIM] * st + _dot_tn(v, ke_ref[pl.ds(r0, L), lo:lo + HEAD_DIM])
        return carry

    lax.fori_loop(0, nc, body, 0, unroll=GLA_UNROLL)

    rg = r_ref[...]
    gate = nw_ref[...] * (rg * _sigmoid(rg))
    for h in range(HEADS):
        lo = h * HEAD_DIM
        o = o_ref[:, lo:lo + HEAD_DIM]
        yn = o * lax.rsqrt(jnp.mean(o * o, axis=-1, keepdims=True) + EPS)
        y_ref[:, lo:lo + HEAD_DIM] = (yn * gate[:, lo:lo + HEAD_DIM]).astype(BF16)


def _gla(proj, glow, gate_w, gate_b, norm_w, batch, seq):
    t = batch * seq
    ts = GLA_TILE
    nj = seq // ts
    blk = lambda col: pl.BlockSpec((ts, MIX_HALF), lambda b, j, col=col: (b * nj + j, col))
    const2 = lambda shape: pl.BlockSpec(shape, lambda b, j: (0, 0))
    return pl.pallas_call(
        functools.partial(_gla_kernel, tile=ts, chunk=GLA_CHUNK),
        grid=(batch, nj),
        in_specs=[blk(1), blk(2), blk(3), blk(4),
                  pl.BlockSpec((ts, LANES), lambda b, j: (b * nj + j, 0)),
                  const2((LANES, MIX_HALF)), const2((1, MIX_HALF)), const2((1, MIX_HALF))],
        out_specs=pl.BlockSpec((ts, MIX_HALF), lambda b, j: (b * nj + j, 0)),
        out_shape=jax.ShapeDtypeStruct((t, MIX_HALF), BF16),
        scratch_shapes=[pltpu.VMEM((HEADS, HEAD_DIM, HEAD_DIM), F32),
                        pltpu.VMEM((ts, MIX_HALF), BF16), pltpu.VMEM((ts, MIX_HALF), BF16),
                        pltpu.VMEM((ts, MIX_HALF), BF16), pltpu.VMEM((ts, MIX_HALF), BF16),
                        pltpu.VMEM((ts // GLA_CHUNK, 1, MIX_HALF), F32),
                        pltpu.VMEM((ts, MIX_HALF), F32)],
        compiler_params=_params("arbitrary", "arbitrary"),
        name="gla",
    )(proj, proj, proj, proj, glow, gate_w, gate_b, norm_w.reshape(1, MIX_HALF))


def _pad_heads(w, axis):
    shape = list(w.shape)
    shape[axis:axis + 1] = [HEADS, GLA_DK]
    w = w.reshape(shape)
    pad = [(0, 0)] * w.ndim
    pad[axis + 1] = (0, HEAD_DIM - GLA_DK)
    w = jnp.pad(w, pad)
    shape[axis:axis + 2] = [HEADS * HEAD_DIM]
    return w.reshape(shape)


def _pack_bf16_pairs(z):
    hi = lax.bitcast_convert_type(z[:, :PACKED].astype(BF16).astype(F32), jnp.uint32)
    lo = lax.bitcast_convert_type(z[:, PACKED:].astype(BF16).astype(F32), jnp.uint32)
    word = (hi & jnp.uint32(0xFFFF0000)) | lax.shift_right_logical(lo, jnp.uint32(16))
    return lax.bitcast_convert_type(word, jnp.int32)


def _unpack_bf16_pairs(p):
    word = lax.bitcast_convert_type(p, jnp.uint32)
    hi = lax.bitcast_convert_type(word & jnp.uint32(0xFFFF0000), F32)
    lo = lax.bitcast_convert_type(lax.shift_left(word, jnp.uint32(16)), F32)
    return hi, lo


def _out_kernel(ya_ref, yb_ref, h_ref, w_ref, lw_ref, lb_ref, rw_ref, rb_ref, sg_ref, su_ref, sd_ref,
                o_ref, opk_ref, idx_ref, rank_ref, wk_ref, cnt_ref, base_ref, *, tile):
    mixed = jnp.concatenate([ya_ref[...], yb_ref[...]], axis=1)
    z = ALPHA * h_ref[...] + _dot(mixed, w_ref[...])
    out = _layer_norm(z, lw_ref[...], lb_ref[...])
    out16 = out.astype(BF16)
    shared = _dot((_silu(_dot(out16, sg_ref[...])) * _dot(out16, su_ref[...])).astype(BF16), sd_ref[...])
    o_ref[...] = ALPHA * out + shared
    opk_ref[...] = _pack_bf16_pairs(out)
    _route_tile(out, rw_ref, rb_ref, idx_ref, rank_ref, wk_ref, cnt_ref, base_ref, tile)


def _out_proj_ln_route(ya, yb, h, w_out, ln_w, ln_b, router_w, router_bias, sg, su, sd):
    t = h.shape[0]
    tm = OUT_TILE
    const = lambda shape: pl.BlockSpec(shape, lambda i: (0, 0))
    per_tok = lambda dt: jax.ShapeDtypeStruct((TOP_K, t), dt)
    tok_blk = pl.BlockSpec((TOP_K, tm), lambda i: (0, i))
    return pl.pallas_call(
        functools.partial(_out_kernel, tile=tm),
        grid=(t // tm,),
        in_specs=[pl.BlockSpec((tm, MIX_HALF), lambda i: (i, 0)),
                  pl.BlockSpec((tm, MIX_HALF), lambda i: (i, 0)),
                  pl.BlockSpec((tm, D_MODEL), lambda i: (i, 0)),
                  const((D_MODEL, D_MODEL)), const((1, D_MODEL)), const((1, D_MODEL)),
                  const((N_EXPERTS, D_MODEL)), const((N_EXPERTS, 1)),
                  const((D_MODEL, D_EXPERT)), const((D_MODEL, D_EXPERT)), const((D_EXPERT, D_MODEL))],
        out_specs=[pl.BlockSpec((tm, D_MODEL), lambda i: (i, 0)),
                   pl.BlockSpec((tm, PACKED), lambda i: (i, 0)),
                   tok_blk, tok_blk, pl.BlockSpec((tm, TOP_K), lambda i: (i, 0)), const((N_EXPERTS, LANES))],
        out_shape=[jax.ShapeDtypeStruct((t, D_MODEL), F32), jax.ShapeDtypeStruct((t, PACKED), jnp.int32),
                   per_tok(jnp.int32), per_tok(jnp.int32), jax.ShapeDtypeStruct((t, TOP_K), F32),
                   jax.ShapeDtypeStruct((N_EXPERTS, LANES), F32)],
        scratch_shapes=[pltpu.VMEM((N_EXPERTS, LANES), F32)],
        compiler_params=_params("arbitrary"),
        name="out_proj_ln_route",
    )(ya, yb, h, w_out.astype(BF16), ln_w.reshape(1, D_MODEL), ln_b.reshape(1, D_MODEL),
      router_w.T, router_bias.reshape(N_EXPERTS, 1), sg.astype(BF16), su.astype(BF16), sd.astype(BF16))


def _first_index(hit, idx, big):
    return jnp.min(jnp.where(hit, idx, big), axis=0, keepdims=True)


def _route_tile(h, w_ref, b_ref, idx_ref, rank_ref, wk_ref, cnt_ref, base_ref, tile):
    @pl.when(pl.program_id(0) == 0)
    def _():
        base_ref[...] = jnp.zeros_like(base_ref)

    h_hi, h_mid, _ = _split3(h)
    w_hi, w_mid, _ = _split3(w_ref[...])
    logits = _dot_nt(w_hi, h_hi) + _dot_nt(w_hi, h_mid) + _dot_nt(w_mid, h_hi)
    scores = _sigmoid(logits)
    biased = scores + b_ref[...]

    sub = lax.broadcasted_iota(jnp.int32, (GROUP_SIZE, tile), 0)
    grp_rows = []
    for g in range(N_GROUPS):
        xg = biased[g * GROUP_SIZE:(g + 1) * GROUP_SIZE, :]
        m1 = jnp.max(xg, axis=0, keepdims=True)
        i1 = _first_index(xg == m1, sub, GROUP_SIZE)
        m2 = jnp.max(jnp.where(sub == i1, NEG_INF, xg), axis=0, keepdims=True)
        grp_rows.append(m1 + m2)
    gs = jnp.concatenate(grp_rows, axis=0)
    gsel = jnp.zeros((N_GROUPS, tile), F32)
    for _ in range(TOPK_GROUPS):
        mx = jnp.max(gs, axis=0, keepdims=True)
        hit = sub == _first_index(gs == mx, sub, N_GROUPS)
        gsel = jnp.where(hit, 1.0, gsel)
        gs = jnp.where(hit, NEG_INF, gs)
    emask = jnp.concatenate(
        [jnp.broadcast_to(gsel[g:g + 1, :], (GROUP_SIZE, tile)) for g in range(N_GROUPS)], axis=0)

    eidx = lax.broadcasted_iota(jnp.int32, (N_EXPERTS, tile), 0)
    cand = jnp.where(emask > 0.5, biased, NEG_INF)
    sel = jnp.zeros((N_EXPERTS, tile), F32)
    picks = []
    for _ in range(TOP_K):
        mx = jnp.max(cand, axis=0, keepdims=True)
        first = _first_index(cand == mx, eidx, N_EXPERTS)
        hit = eidx == first
        picks.append(first)
        sel = jnp.where(hit, 1.0, sel)
        cand = jnp.where(hit, NEG_INF, cand)
    picked = jnp.where(sel > 0.5, scores, 0.0)
    wts = picked / jnp.sum(picked, axis=0, keepdims=True) * ROUTED_SCALE

    ri = lax.broadcasted_iota(jnp.int32, (tile, tile), 0)
    ci = lax.broadcasted_iota(jnp.int32, (tile, tile), 1)
    before = (ri < ci).astype(BF16)
    prior = _dot(sel.astype(BF16), before) + base_ref[:, 0:1]
    ranks = [jnp.sum(jnp.where(eidx == p, prior, 0.0), axis=0, keepdims=True) for p in picks]
    wsel = [jnp.sum(jnp.where(eidx == p, wts, 0.0), axis=0, keepdims=True) for p in picks]
    idx_ref[...] = jnp.concatenate(picks, axis=0)
    rank_ref[...] = jnp.concatenate(ranks, axis=0).astype(jnp.int32)
    wk_ref[...] = jnp.concatenate(wsel, axis=0).T
    total = base_ref[...] + jnp.sum(sel, axis=1, keepdims=True)
    base_ref[...] = total
    cnt_ref[...] = total


def _silu(x):
    return x * _sigmoid(x)


def _sc_mesh():
    return plsc.VectorSubcoreMesh(core_axis_name="c", subcore_axis_name="s")


def _sc_worker_id():
    return lax.axis_index("s") * SC_CORES + lax.axis_index("c")


def _dispatch_rows(xpk, pos_chunks, n_rows):
    t = xpk.shape[0]
    ch = pos_chunks.shape[2]
    n_ch = t // SC_WORKERS // ch

    @functools.partial(
        pl.kernel, mesh=_sc_mesh(),
        out_type=jax.ShapeDtypeStruct((n_rows, PACKED), jnp.int32),
        scratch_types=[pltpu.VMEM((TOP_K, ch), jnp.int32),
                       pltpu.VMEM((ch, PACKED), jnp.int32),
                       pltpu.SemaphoreType.DMA],
        name="moe_dispatch",
    )
    def scatter(x_hbm, pos_hbm, out_hbm, idx_v, rows_v, sem):
        wid = _sc_worker_id()

        @pl.loop(0, n_ch)
        def _(c):
            chunk = wid * n_ch + c
            off = pl.multiple_of(chunk * ch, ch)
            pltpu.sync_copy(pos_hbm.at[chunk], idx_v)
            pltpu.sync_copy(x_hbm.at[pl.ds(off, ch)], rows_v)
            copies = [pltpu.async_copy(rows_v, out_hbm.at[idx_v.at[k]], sem) for k in range(TOP_K)]
            for cp in copies:
                cp.wait()

    return scatter(xpk, pos_chunks)


def _gather_rows(table, idx):
    n = idx.shape[0]
    per_w = n // SC_WORKERS
    n_ch = per_w // SC_CHUNK
    assert n_ch % 2 == 0 and n_ch >= 2

    @functools.partial(
        pl.kernel, mesh=_sc_mesh(),
        out_type=jax.ShapeDtypeStruct((n, PACKED), jnp.int32),
        scratch_types=[pltpu.VMEM((n_ch, SC_CHUNK), jnp.int32),
                       pltpu.VMEM((SC_CHUNK, PACKED), jnp.int32), pltpu.VMEM((SC_CHUNK, PACKED), jnp.int32),
                       pltpu.SemaphoreType.DMA, pltpu.SemaphoreType.DMA,
                       pltpu.SemaphoreType.DMA, pltpu.SemaphoreType.DMA],
        name="moe_gather",
    )
    def gather(table_hbm, idx_hbm, out_hbm, idx_v, rows0, rows1, g0, g1, w0, w1):
        wid = _sc_worker_id()
        base = wid * per_w
        rows, g_sem, w_sem = (rows0, rows1), (g0, g1), (w0, w1)
        pltpu.sync_copy(idx_hbm.at[wid], idx_v)

        def fetch(c, b):
            return pltpu.make_async_copy(table_hbm.at[idx_v.at[c]], rows[b], g_sem[b])

        def flush(c, b):
            off = pl.multiple_of(base + c * SC_CHUNK, SC_CHUNK)
            return pltpu.make_async_copy(rows[b], out_hbm.at[pl.ds(off, SC_CHUNK)], w_sem[b])

        fetch(0, 0).start()

        @pl.loop(0, n_ch, step=2)
        def _(c0):
            for b in range(2):
                c = c0 + b
                fetch(c, b).wait()
                flush(c, b).start()

                @pl.when(c + 1 < n_ch)
                def _():
                    @pl.when(c >= 1)
                    def _():
                        flush(c - 1, 1 - b).wait()
                    fetch(c + 1, 1 - b).start()

        flush(n_ch - 2, 0).wait()
        flush(n_ch - 1, 1).wait()

    return gather(table, idx.reshape(SC_WORKERS, n_ch, SC_CHUNK))


def _unpacked_bf16(p):
    hi, lo = _unpack_bf16_pairs(p)
    return jnp.concatenate([hi.astype(BF16), lo.astype(BF16)], axis=1)


def _expert_kernel(be_ref, nu_ref, next_ref, slot_ref, xs_hbm, wg_hbm, wu_hbm, wd_hbm, y_ref,
                   g16_ref, u16_ref, d16_ref, gf_ref, uf_ref, df_ref, xbuf_ref, xsem, wsem, *, layer):
    i = pl.program_id(0)
    n_used = nu_ref[0]
    bm = xbuf_ref.shape[1]

    def fetch(b):
        slot = lax.rem(b, XS_SLOTS)
        rows = pl.ds(pl.multiple_of(b * bm, bm), bm)
        return pltpu.make_async_copy(xs_hbm.at[rows], xbuf_ref.at[slot], xsem.at[slot])

    @pl.when(i == 0)
    def _():
        fetch(0).start()

        @pl.when(n_used > 1)
        def _():
            fetch(1).start()

    @pl.when(i + 2 < n_used)
    def _():
        fetch(i + 2).start()

    e = be_ref[i]
    slot = slot_ref[e]
    f32_bufs = (gf_ref, uf_ref, df_ref)

    def wfetch(expert, dst_slot):
        return [pltpu.make_async_copy(w_hbm.at[layer, expert], buf.at[dst_slot], wsem.at[dst_slot, j])
                for j, (w_hbm, buf) in enumerate(zip((wg_hbm, wu_hbm, wd_hbm), f32_bufs))]

    @pl.when(i == 0)
    def _():
        for cp in wfetch(e, slot):
            cp.start()

    first_block_of_expert = jnp.logical_or(i == 0, e != be_ref[jnp.maximum(i - 1, 0)])

    @pl.when(jnp.logical_and(first_block_of_expert, i < n_used))
    def _():
        for cp in wfetch(e, slot):
            cp.wait()
        g16_ref[slot] = gf_ref[slot].astype(BF16)
        u16_ref[slot] = uf_ref[slot].astype(BF16)
        d16_ref[slot] = df_ref[slot].astype(BF16)
        nxt = next_ref[e]

        @pl.when(nxt >= 0)
        def _():
            for cp in wfetch(nxt, 1 - slot):
                cp.start()

    @pl.when(i < n_used)
    def _():
        fetch(i).wait()
        x = _unpacked_bf16(xbuf_ref[lax.rem(i, XS_SLOTS)])
        hh = _silu(_dot(x, g16_ref[slot])) * _dot(x, u16_ref[slot])
        y_ref[...] = _pack_bf16_pairs(_dot(hh.astype(BF16), d16_ref[slot]))


def _experts(block_e, n_used, next_expert, weight_slot, xs, wg, wu, wd, layer):
    nb = block_e.shape[0]
    bm = MOE_BLOCK
    hbm = pl.BlockSpec(memory_space=pl.ANY)
    two = lambda shape, dt: pltpu.VMEM((2,) + shape, dt)
    grid_spec = pltpu.PrefetchScalarGridSpec(
        num_scalar_prefetch=4,
        grid=(nb,),
        in_specs=[hbm, hbm, hbm, hbm],
        out_specs=pl.BlockSpec((bm, PACKED), lambda i, be, nu, nx, sl: (jnp.minimum(i, nu[0] - 1), 0)),
        scratch_shapes=[two((D_MODEL, D_EXPERT), BF16), two((D_MODEL, D_EXPERT), BF16), two((D_EXPERT, D_MODEL), BF16),
                        two((D_MODEL, D_EXPERT), F32), two((D_MODEL, D_EXPERT), F32), two((D_EXPERT, D_MODEL), F32),
                        pltpu.VMEM((XS_SLOTS, bm, PACKED), jnp.int32), pltpu.SemaphoreType.DMA((XS_SLOTS,)),
                        pltpu.SemaphoreType.DMA((2, 3))],
    )
    return pl.pallas_call(
        functools.partial(_expert_kernel, layer=layer),
        grid_spec=grid_spec,
        out_shape=jax.ShapeDtypeStruct((nb * bm, PACKED), jnp.int32),
        compiler_params=_params("arbitrary"),
        name="moe_experts",
    )(block_e, n_used, next_expert, weight_slot, xs, wg, wu, wd)


def _combine_kernel(g_ref, wk_ref, base_ref, lw_ref, lb_ref, *rest, with_proj):
    if with_proj:
        (wm_ref, ws_ref), (o_ref, proj_ref, small_ref) = rest[:2], rest[-3:]
        next_proj = (wm_ref, ws_ref, proj_ref, small_ref)
    else:
        o_ref, next_proj = rest[-1], None
    acc_hi = base_ref[:, :PACKED]
    acc_lo = base_ref[:, PACKED:]
    wk = wk_ref[...]
    for k in range(TOP_K):
        y_hi, y_lo = _unpack_bf16_pairs(g_ref[k])
        w = wk[:, k:k + 1]
        acc_hi = acc_hi + w * y_hi
        acc_lo = acc_lo + w * y_lo
    out = _layer_norm(jnp.concatenate([acc_hi, acc_lo], axis=1), lw_ref[...], lb_ref[...])
    o_ref[...] = out
    if next_proj is not None:
        wm_ref, ws_ref, proj_ref, small_ref = next_proj
        out16 = out.astype(BF16)
        proj_ref[...] = _dot(out16, wm_ref[...])
        small_ref[...] = _dot(out16, ws_ref[...])


def _combine_ln(g, wk, base, ln_w, ln_b, next_w, part, prev):
    t = base.shape[0]
    tm = COMBINE_TILE if next_w is not None else 2 * COMBINE_TILE
    n_blk = g.shape[1] // tm
    first = part * n_blk
    const = lambda shape: pl.BlockSpec(shape, lambda i: (0, 0))
    rows = lambda width: pl.BlockSpec((tm, width), lambda i: (i + first, 0))
    in_specs = [pl.BlockSpec((TOP_K, tm, PACKED), lambda i: (0, i, 0)), rows(TOP_K), rows(D_MODEL),
                const((1, D_MODEL)), const((1, D_MODEL))]
    args = [g, wk, base, ln_w.reshape(1, D_MODEL), ln_b.reshape(1, D_MODEL)]
    out_specs = [rows(D_MODEL)]
    out_shape = [jax.ShapeDtypeStruct((t, D_MODEL), F32)]
    if next_w is not None:
        w_main, w_small = next_w
        n = w_main.shape[1]
        in_specs += [const((D_MODEL, n)), const((D_MODEL, LANES))]
        args += [w_main, w_small]
        out_specs += [rows(n), rows(LANES)]
        out_shape += [jax.ShapeDtypeStruct((t, n), F32), jax.ShapeDtypeStruct((t, LANES), F32)]
    aliases = {}
    if prev is not None:
        aliases = {len(args) + k: k for k in range(len(prev))}
        in_specs += [pl.BlockSpec(memory_space=pl.ANY)] * len(prev)
        args += list(prev)
    return pl.pallas_call(
        functools.partial(_combine_kernel, with_proj=next_w is not None),
        grid=(n_blk,),
        in_specs=in_specs,
        out_specs=out_specs,
        out_shape=out_shape,
        input_output_aliases=aliases,
        compiler_params=_params("parallel"),
        name="moe_combine_ln",
    )(*args)


def _moe_ln(base, hpk, idx, rank, wk, counts, wg, wu, wd, layer, ln_w, ln_b, next_w):
    t = base.shape[0]
    cnt = counts[:, 0].astype(jnp.int32)
    padded = (cnt + MOE_BLOCK - 1) // MOE_BLOCK * MOE_BLOCK
    pend = jnp.cumsum(padded)
    experts = jnp.arange(N_EXPERTS, dtype=jnp.int32)
    pstart_of_pick = jnp.sum(jnp.where(idx[:, :, None] == experts, pend - padded, 0), axis=-1)
    pos = pstart_of_pick + rank
    nb = -(-(t * TOP_K + N_EXPERTS * (MOE_BLOCK - 1)) // MOE_BLOCK)
    starts = jnp.arange(nb, dtype=jnp.int32) * MOE_BLOCK
    block_e = jnp.minimum(jnp.sum((pend[None, :] <= starts[:, None]).astype(jnp.int32), axis=1), N_EXPERTS - 1)
    n_used = (pend[-1] // MOE_BLOCK).astype(jnp.int32).reshape(1)
    has_rows = cnt > 0
    later = jnp.logical_and(has_rows[None, :], experts[None, :] > experts[:, None])
    next_expert = jnp.min(jnp.where(later, experts[None, :], N_EXPERTS), axis=1)
    next_expert = jnp.where(next_expert == N_EXPERTS, -1, next_expert).astype(jnp.int32)
    weight_slot = ((jnp.cumsum(has_rows) - has_rows) % 2).astype(jnp.int32)
    pos_chunks = pos.reshape(TOP_K, t // SC_DISPATCH_CHUNK, SC_DISPATCH_CHUNK).transpose(1, 0, 2)
    xs = _dispatch_rows(hpk, pos_chunks, nb * MOE_BLOCK)
    ys = _experts(block_e, n_used, next_expert, weight_slot, xs, wg, wu, wd, layer)
    part = t // COMBINE_PARTS
    gathered = [_gather_rows(ys, pos[:, p * part:(p + 1) * part].reshape(-1)).reshape(TOP_K, part, PACKED)
                for p in range(COMBINE_PARTS)]
    outs = None
    for p in range(COMBINE_PARTS):
        outs = _combine_ln(gathered[p], wk, base, ln_w, ln_b, next_w, p, outs)
    return outs


def _pad_cols(w, width=LANES):
    return jnp.pad(w, ((0, 0), (0, width - w.shape[1])))


def _even_proj_weights(w_in):
    a4 = 4 * MIX_HALF
    ng = 2 * HEADS
    w_main = jnp.concatenate([w_in[:, :a4], w_in[:, a4 + ng:]], axis=1).astype(BF16)
    w_gate = _pad_cols(w_in[:, a4:a4 + ng]).astype(BF16)
    return w_main, w_gate


def _even_mixer(proj, gates, batch, seq, gate_b, norm_w, conv_w, conv_b, wa, ba, wx, bx, lam):
    ya = _mlstm(proj, gates, gate_b, norm_w, batch, seq)
    yb = _rglru(proj, conv_w, conv_b, wa, ba, wx, bx, lam, batch, seq)
    return ya, yb


def _odd_proj_weights(w_in):
    c0 = MIX_HALF
    c1 = c0 + HEADS * GLA_DK
    c2 = c1 + HEADS * GLA_DK
    c3 = c2 + MIX_HALF
    c4 = c3 + MIX_HALF
    w_main = jnp.concatenate([w_in[:, :c0], _pad_heads(w_in[:, c0:c1], 1), _pad_heads(w_in[:, c1:c2], 1),
                              w_in[:, c2:c4]], axis=1).astype(BF16)
    w_low = _pad_cols(w_in[:, c4:]).astype(BF16)
    return w_main, w_low


def _odd_mixer(proj, glow, batch, seq, lam_re, lam_im, b_re, b_im, c_re, c_im, d_skip, log_dt,
               glu_w, glu_b, gate_w, gate_b, norm_w):
    tables = _s5_tables(lam_re, lam_im, b_re, b_im, c_re, c_im, log_dt)
    yc = _s5(proj, tables, d_skip, glu_w, glu_b, batch, seq)
    gw = jnp.pad(_pad_heads(gate_w, 1), ((0, LANES - GLA_GATE_RANK), (0, 0))).astype(BF16)
    gb = _pad_heads(gate_b.reshape(1, -1), 1)
    yd = _gla(proj, glow, gw, gb, norm_w, batch, seq)
    return yc, yd


def kernel(x, ln1_w, ln1_b, ln2_w, ln2_b, w_out, w_in_even, mlstm_gate_b, mlstm_norm_w, lru_conv_w, lru_conv_b, lru_wa, lru_ba, lru_wx, lru_bx, lru_lambda, w_in_odd, s5_lam_re, s5_lam_im, s5_b_re, s5_b_im, s5_c_re, s5_c_im, s5_d, s5_log_dt, s5_glu_w, s5_glu_b, gla_gate_w, gla_gate_b, gla_norm_w, router_w, router_bias, exp_w_gate, exp_w_up, exp_w_down, sh_w_gate, sh_w_up, sh_w_down):
    batch, seq, d = x.shape
    proj_w = [_even_proj_weights(w_in_even[layer // 2]) if layer % 2 == 0 else _odd_proj_weights(w_in_odd[layer // 2])
              for layer in range(DEPTH)]
    h = x.reshape(batch * seq, d)
    proj, small = _proj(h, *proj_w[0])
    for layer in range(DEPTH):
        j = layer // 2
        if layer % 2 == 0:
            y1, y2 = _even_mixer(proj, small, batch, seq, mlstm_gate_b[j], mlstm_norm_w[j],
                                 lru_conv_w[j], lru_conv_b[j], lru_wa[j], lru_ba[j], lru_wx[j],
                                 lru_bx[j], lru_lambda[j])
        else:
            y1, y2 = _odd_mixer(proj, small, batch, seq, s5_lam_re[j], s5_lam_im[j], s5_b_re[j],
                                s5_b_im[j], s5_c_re[j], s5_c_im[j], s5_d[j], s5_log_dt[j],
                                s5_glu_w[j], s5_glu_b[j], gla_gate_w[j], gla_gate_b[j], gla_norm_w[j])
        base, hpk, idx, rank, wk, counts = _out_proj_ln_route(
            y1, y2, h, w_out[layer], ln1_w[layer], ln1_b[layer], router_w[layer], router_bias[layer],
            sh_w_gate[layer], sh_w_up[layer], sh_w_down[layer])
        next_w = proj_w[layer + 1] if layer + 1 < DEPTH else None
        res = _moe_ln(base, hpk, idx, rank, wk, counts, exp_w_gate, exp_w_up, exp_w_down, layer,
                      ln2_w[layer], ln2_b[layer], next_w)
        if next_w is None:
            (h,) = res
        else:
            h, proj, small = res
    return h.reshape(batch, seq, d)
```

```python
import functools
import math

import jax
import jax.numpy as jnp
from jax import lax
from jax.experimental import pallas as pl
from jax.experimental.pallas import tpu as pltpu
from jax.experimental.pallas import tpu_sc as plsc

F32 = jnp.float32
BF16 = jnp.bfloat16

D_MODEL = 1024
DEPTH = 2
MIX_HALF = 512
HEADS = 4
HEAD_DIM = 128
GLA_DK = 64
GLA_CHUNK = 64
GLA_GATE_RANK = 16
GLA_GATE_TEMP = 16.0
LRU_C = 8.0
LRU_CONV = 4
S5_GROUP = 16
S5_GROUPS = 32
S5_STATE = 64
S5_LANES = S5_GROUPS * S5_STATE
S5_BLOCKS = 4
N_EXPERTS = 64
N_GROUPS = 8
GROUP_SIZE = N_EXPERTS // N_GROUPS
TOP_K = 8
TOPK_GROUPS = 4
D_EXPERT = 256
ROUTED_SCALE = 2.5
ALPHA = (2.0 * DEPTH) ** 0.25
EPS = 1e-5
LANES = 128
SUBLANES = 8
NEG_INF = float("-inf")

VMEM_LIMIT = 56 * 1024 * 1024

MLSTM_CHUNK = 128
MLSTM_TILE = 1024
MLSTM_UNROLL = 2
LRU_TILE = 1024
LRU_LOG_STEPS = 3
LRU_UNROLL = True
S5_TILE = 512
S5_LOG_STEPS = 3
S5_UNROLL = True
GLA_UNROLL = 16
GLA_TILE = 1024
PROJ_TILE = 1024
OUT_TILE = 1024
MOE_BLOCK = 1152
XS_SLOTS = 3
COMBINE_TILE = 256
COMBINE_PARTS = 2
PACKED = D_MODEL // 2
SC_CHUNK = 64
SC_DISPATCH_CHUNK = 128
SC_CORES = 2
SC_SUBCORES = 16
SC_WORKERS = SC_CORES * SC_SUBCORES


def _params(*sem):
    return pltpu.CompilerParams(dimension_semantics=sem, vmem_limit_bytes=VMEM_LIMIT)


def _split3(x):
    hi = x.astype(BF16)
    r1 = x - hi.astype(F32)
    mid = r1.astype(BF16)
    lo = (r1 - mid.astype(F32)).astype(BF16)
    return hi, mid, lo


def _dot(a, b):
    return jnp.dot(a, b, preferred_element_type=F32)


def _dot_nt(a, b):
    return lax.dot_general(a, b, (((1,), (1,)), ((), ())), preferred_element_type=F32)


def _dot_tn(a, b):
    return lax.dot_general(a, b, (((0,), (0,)), ((), ())), preferred_element_type=F32)


def _exact_left01(mask01_bf16, x):
    hi, mid, lo = _split3(x)
    return _dot(mask01_bf16, hi) + _dot(mask01_bf16, mid) + _dot(mask01_bf16, lo)


def _exact_right01(x, mask01_bf16):
    hi, mid, lo = _split3(x)
    return _dot(hi, mask01_bf16) + _dot(mid, mask01_bf16) + _dot(lo, mask01_bf16)


def _log_sigmoid(x):
    return jnp.minimum(x, 0.0) - jnp.log(1.0 + jnp.exp(-jnp.abs(x)))


def _sigmoid(x):
    return 1.0 / (1.0 + jnp.exp(-x))


def _gelu_tanh(x):
    c = math.sqrt(2.0 / math.pi)
    return 0.5 * x * (1.0 + jnp.tanh(c * (x + 0.044715 * (x * x * x))))


def _layer_norm(z, w, b):
    mu = jnp.mean(z, axis=-1, keepdims=True)
    zc = z - mu
    return zc * lax.rsqrt(jnp.mean(zc * zc, axis=-1, keepdims=True) + EPS) * w + b


def _proj_kernel(x_ref, w_ref, wg_ref, o_ref, og_ref):
    x = x_ref[...].astype(BF16)
    o_ref[...] = _dot(x, w_ref[...])
    og_ref[...] = _dot(x, wg_ref[...])


def _proj(x, w_main, w_small):
    t, d = x.shape
    n = w_main.shape[1]
    tm = PROJ_TILE
    return pl.pallas_call(
        _proj_kernel,
        grid=(t // tm,),
        in_specs=[pl.BlockSpec((tm, d), lambda i: (i, 0)),
                  pl.BlockSpec((d, n), lambda i: (0, 0)),
                  pl.BlockSpec((d, LANES), lambda i: (0, 0))],
        out_specs=[pl.BlockSpec((tm, n), lambda i: (i, 0)),
                   pl.BlockSpec((tm, LANES), lambda i: (i, 0))],
        out_shape=[jax.ShapeDtypeStruct((t, n), F32), jax.ShapeDtypeStruct((t, LANES), F32)],
        compiler_params=_params("parallel"),
        name="in_proj",
    )(x, w_main, w_small)


def _mlstm_kernel(q_ref, k_ref, v_ref, o_ref, gc_ref, gr_ref, bc_ref, br_ref, nw_ref,
                  y_ref, c_ref, m_ref, *, chunk, n_chunks):
    L = chunk

    @pl.when(pl.program_id(1) == 0)
    def _():
        c_ref[...] = jnp.zeros_like(c_ref)
        m_ref[...] = jnp.zeros_like(m_ref)

    ri = lax.broadcasted_iota(jnp.int32, (L, L), 0)
    ci = lax.broadcasted_iota(jnp.int32, (L, L), 1)
    causal = ci <= ri
    tril = causal.astype(BF16)
    triu = (ri <= ci).astype(BF16)
    ones_v = jnp.ones((L, HEAD_DIM), BF16)
    scale = HEAD_DIM ** -0.5

    def body(c, carry):
        r0 = pl.multiple_of(c * L, L)
        g_col = gc_ref[pl.ds(r0, L), :] + bc_ref[...]
        g_row = gr_ref[c] + br_ref[...]
        b_col_all = _exact_left01(tril, _log_sigmoid(g_col))
        b_row_all = _exact_right01(_log_sigmoid(g_row), triu)
        for h in range(HEADS):
            lo = h * HEAD_DIM
            q = q_ref[pl.ds(r0, L), lo:lo + HEAD_DIM].astype(BF16)
            k = k_ref[pl.ds(r0, L), lo:lo + HEAD_DIM] * scale
            v = v_ref[pl.ds(r0, L), lo:lo + HEAD_DIM].astype(BF16)
            v_aug = jnp.concatenate([v, ones_v], axis=1)
            i_rep = jnp.broadcast_to(g_col[:, h:h + 1], (L, LANES))
            b_rep = jnp.broadcast_to(b_col_all[:, HEADS + h:HEADS + h + 1], (L, LANES))
            i_row = g_row[h:h + 1, :]
            b_row = b_row_all[HEADS + h:HEADS + h + 1, :]
            b_last = b_rep[L - 1:L, :]
            m_prev = m_ref[h:h + 1, :]
            c_prev = c_ref[h]

            d_mat = jnp.where(causal, b_rep - b_row + i_row, NEG_INF)
            m_inter = b_rep + m_prev
            m_i = jnp.maximum(m_inter, jnp.max(d_mat, axis=1, keepdims=True))
            s = _dot_nt(q, k.astype(BF16)) * jnp.exp(d_mat - m_i)
            w_inter = jnp.exp(m_inter - m_i)
            intra = _dot(s.astype(BF16), v_aug)
            inter = _dot(q, c_prev.astype(BF16))
            num = intra[:, :HEAD_DIM] + w_inter * inter[:, :HEAD_DIM]
            den = intra[:, HEAD_DIM:] + w_inter * inter[:, HEAD_DIM:]
            hh = num / jnp.maximum(jnp.abs(den), jnp.exp(-m_i))

            w_loc = b_last - b_rep + i_rep
            m_loc = jnp.max(w_loc, axis=0, keepdims=True)
            kp = (k * jnp.exp(w_loc - m_loc)).astype(BF16)
            c_loc = _dot_tn(kp, v_aug)
            m_new = jnp.maximum(b_last + m_prev, m_loc)
            keep = jnp.exp(b_last + m_prev - m_new)
            add = jnp.exp(m_loc - m_new)
            c_ref[h] = (jnp.concatenate([keep, keep], axis=1) * c_prev
                        + jnp.concatenate([add, add], axis=1) * c_loc)
            m_ref[h:h + 1, :] = m_new

            hc = hh - jnp.mean(hh, axis=-1, keepdims=True)
            yn = hc * lax.rsqrt(jnp.mean(hc * hc, axis=-1, keepdims=True) + EPS)
            og = o_ref[pl.ds(r0, L), lo:lo + HEAD_DIM]
            y_ref[pl.ds(r0, L), lo:lo + HEAD_DIM] = (yn * nw_ref[:, lo:lo + HEAD_DIM] * _sigmoid(og)).astype(BF16)
        return carry

    lax.fori_loop(0, n_chunks, body, 0, unroll=MLSTM_UNROLL)


def _mlstm(proj, gates, gate_b, norm_w, batch, seq):
    t = batch * seq
    L = MLSTM_CHUNK
    assert L == LANES, "the kernel keeps per-row gate terms replicated over one vreg of lanes"
    ts = MLSTM_TILE
    nj = seq // ts
    nc = ts // L
    g_row = gates[:, :2 * HEADS].reshape(t // L, L, 2 * HEADS).transpose(0, 2, 1)
    b_col = jnp.zeros((1, LANES), F32).at[0, :2 * HEADS].set(gate_b)
    b_row = gate_b.reshape(2 * HEADS, 1)
    blk = lambda col: pl.BlockSpec((ts, MIX_HALF), lambda b, j, col=col: (b * nj + j, col))
    kern = functools.partial(_mlstm_kernel, chunk=L, n_chunks=nc)
    return pl.pallas_call(
        kern,
        grid=(batch, nj),
        in_specs=[blk(0), blk(1), blk(2), blk(3),
                  pl.BlockSpec((ts, LANES), lambda b, j: (b * nj + j, 0)),
                  pl.BlockSpec((nc, 2 * HEADS, L), lambda b, j: (b * nj + j, 0, 0)),
                  pl.BlockSpec((1, LANES), lambda b, j: (0, 0)),
                  pl.BlockSpec((2 * HEADS, 1), lambda b, j: (0, 0)),
                  pl.BlockSpec((1, MIX_HALF), lambda b, j: (0, 0))],
        out_specs=pl.BlockSpec((ts, MIX_HALF), lambda b, j: (b * nj + j, 0)),
        out_shape=jax.ShapeDtypeStruct((t, MIX_HALF), BF16),
        scratch_shapes=[pltpu.VMEM((HEADS, HEAD_DIM, 2 * HEAD_DIM), F32),
                        pltpu.VMEM((8, LANES), F32)],
        compiler_params=_params("arbitrary", "arbitrary"),
        name="mlstm",
    )(proj, proj, proj, proj, gates, g_row, b_col, b_row, norm_w.reshape(1, MIX_HALF))


def _rglru_kernel(xb_ref, gb_ref, cw_ref, cb_ref, wa_ref, ba_ref, wx_ref, bx_ref, lam_ref,
                  y_ref, xext_ref, h_ref, a_ref, u_ref, *, tile):
    @pl.when(pl.program_id(1) == 0)
    def _():
        xext_ref[0:8, :] = jnp.zeros((8, MIX_HALF), F32)
        h_ref[...] = jnp.zeros_like(h_ref)

    x = xb_ref[...]
    xext_ref[8:8 + tile, :] = x
    xc = cb_ref[...] + cw_ref[LRU_CONV - 1:LRU_CONV, :] * x
    for tap in range(LRU_CONV - 1):
        back = LRU_CONV - 1 - tap
        xc = xc + cw_ref[tap:tap + 1, :] * xext_ref[8 - back:8 - back + tile, :]
    xext_ref[0:8, :] = x[tile - 8:tile, :]

    xc16 = xc.astype(BF16)
    r_parts, i_parts = [], []
    for h in range(HEADS):
        lo = h * HEAD_DIM
        xh = xc16[:, lo:lo + HEAD_DIM]
        r_parts.append(_dot(xh, wa_ref[h]))
        i_parts.append(_dot(xh, wx_ref[h]))
    r = _sigmoid(jnp.concatenate(r_parts, axis=1) + ba_ref[...])
    ig = _sigmoid(jnp.concatenate(i_parts, axis=1) + bx_ref[...])
    lam = lam_ref[...]
    softplus_neg = jnp.maximum(-lam, 0.0) + jnp.log(1.0 + jnp.exp(-jnp.abs(lam)))
    log_a = -LRU_C * r * softplus_neg
    a = jnp.exp(log_a)
    th = jnp.tanh(log_a)
    u = jnp.sqrt(-2.0 * th / (1.0 - th)) * ig * xc

    a_ref[...] = a
    u_ref[...] = u
    rows = lax.broadcasted_iota(jnp.int32, (SUBLANES, MIX_HALF), 0)

    def group(i, h_prev):
        r0 = pl.multiple_of(i * SUBLANES, SUBLANES)
        ag = a_ref[pl.ds(r0, SUBLANES), :]
        ug = u_ref[pl.ds(r0, SUBLANES), :]
        for k in range(LRU_LOG_STEPS):
            keep = rows >= (1 << k)
            ug = ag * jnp.where(keep, pltpu.roll(ug, 1 << k, 0), 0.0) + ug
            ag = ag * jnp.where(keep, pltpu.roll(ag, 1 << k, 0), 1.0)
        hg = ug + ag * h_prev
        u_ref[pl.ds(r0, SUBLANES), :] = hg
        return hg[SUBLANES - 1:SUBLANES, :]

    h_last = lax.fori_loop(0, tile // SUBLANES, group, h_ref[0:1, :], unroll=LRU_UNROLL)
    h_ref[...] = jnp.broadcast_to(h_last, h_ref.shape)
    y_ref[...] = (u_ref[...] * _gelu_tanh(gb_ref[...])).astype(BF16)


def _rglru(proj, conv_w, conv_b, wa, ba, wx, bx, lam, batch, seq):
    t = batch * seq
    ts = LRU_TILE
    nj = seq // ts
    row = lambda a: a.reshape(1, MIX_HALF)
    const2 = lambda shape: pl.BlockSpec(shape, lambda b, j: (0, 0))
    const3 = lambda shape: pl.BlockSpec(shape, lambda b, j: (0, 0, 0))
    blk = lambda col: pl.BlockSpec((ts, MIX_HALF), lambda b, j, col=col: (b * nj + j, col))
    return pl.pallas_call(
        functools.partial(_rglru_kernel, tile=ts),
        grid=(batch, nj),
        in_specs=[blk(4), blk(5), const2((LRU_CONV, MIX_HALF)), const2((1, MIX_HALF)),
                  const3((HEADS, HEAD_DIM, HEAD_DIM)), const2((1, MIX_HALF)),
                  const3((HEADS, HEAD_DIM, HEAD_DIM)), const2((1, MIX_HALF)), const2((1, MIX_HALF))],
        out_specs=pl.BlockSpec((ts, MIX_HALF), lambda b, j: (b * nj + j, 0)),
        out_shape=jax.ShapeDtypeStruct((t, MIX_HALF), BF16),
        scratch_shapes=[pltpu.VMEM((ts + 8, MIX_HALF), F32), pltpu.VMEM((8, MIX_HALF), F32),
                        pltpu.VMEM((ts, MIX_HALF), F32), pltpu.VMEM((ts, MIX_HALF), F32)],
        compiler_params=_params("arbitrary", "arbitrary"),
        name="rglru",
    )(proj, proj, conv_w, row(conv_b), wa.astype(BF16), row(ba), wx.astype(BF16), row(bx), row(lam))


def _s5_kernel(u_ref, bre_ref, bim_ref, cre_ref, cim_ref, mre_ref, mim_ref, pre_ref, pim_ref, d_ref, gw_ref,
               gb_ref, y_ref, xr_ref, xi_ref, cr_ref, ci_ref, *, tile):
    @pl.when(pl.program_id(1) == 0)
    def _():
        cr_ref[...] = jnp.zeros_like(cr_ref)
        ci_ref[...] = jnp.zeros_like(ci_ref)

    u = u_ref[...]
    u16 = u.astype(BF16)
    blk_c = MIX_HALF // S5_BLOCKS
    blk_s = S5_LANES // S5_BLOCKS
    parts = []
    for j in range(S5_BLOCKS):
        lanes = slice(j * blk_s, (j + 1) * blk_s)
        uj = u16[:, j * blk_c:(j + 1) * blk_c]
        xr_ref[:, lanes] = _dot(uj, bre_ref[j])
        xi_ref[:, lanes] = _dot(uj, bim_ref[j])

        def group(i, carry, lanes=lanes):
            cr, ci = carry
            r0 = pl.multiple_of(i * SUBLANES, SUBLANES)
            xr = xr_ref[pl.ds(r0, SUBLANES), lanes]
            xi = xi_ref[pl.ds(r0, SUBLANES), lanes]
            for k in range(S5_LOG_STEPS):
                sr = pltpu.roll(xr, 1 << k, 0)
                si = pltpu.roll(xi, 1 << k, 0)
                mr = mre_ref[k, :, lanes]
                mi = mim_ref[k, :, lanes]
                xr, xi = xr + mr * sr - mi * si, xi + mr * si + mi * sr
            pr = pre_ref[:, lanes]
            pi = pim_ref[:, lanes]
            xr, xi = xr + pr * cr - pi * ci, xi + pr * ci + pi * cr
            xr_ref[pl.ds(r0, SUBLANES), lanes] = xr
            xi_ref[pl.ds(r0, SUBLANES), lanes] = xi
            return xr[SUBLANES - 1:SUBLANES, :], xi[SUBLANES - 1:SUBLANES, :]

        cr, ci = lax.fori_loop(0, tile // SUBLANES, group, (cr_ref[0:1, lanes], ci_ref[0:1, lanes]),
                               unroll=S5_UNROLL)
        cr_ref[0:1, lanes] = cr
        ci_ref[0:1, lanes] = ci
        parts.append(_dot(xr_ref[:, lanes].astype(BF16), cre_ref[j])
                     - _dot(xi_ref[:, lanes].astype(BF16), cim_ref[j]))
    y = jnp.concatenate(parts, axis=1) + d_ref[...] * u
    g = _gelu_tanh(y)
    y_ref[...] = (g * _sigmoid(_dot(g.astype(BF16), gw_ref[...]) + gb_ref[...])).astype(BF16)


def _s5_tables(lam_re, lam_im, b_re, b_im, c_re, c_im, log_dt):
    lr, li = lam_re.astype(F32), lam_im.astype(F32)
    dt = jnp.exp(log_dt.astype(F32))[:, None]
    mag = jnp.exp(lr * dt)
    abar_re = mag * jnp.cos(li * dt)
    abar_im = mag * jnp.sin(li * dt)
    den = lr * lr + li * li
    nr = abar_re - 1.0
    coef_re = (nr * lr + abar_im * li) / den
    coef_im = (abar_im * lr - nr * li) / den
    bbar_re = coef_re[..., None] * b_re - coef_im[..., None] * b_im
    bbar_im = coef_re[..., None] * b_im + coef_im[..., None] * b_re
    gpb = S5_GROUPS // S5_BLOCKS
    eye = jnp.eye(gpb, dtype=F32)

    def in_map(bb):
        bb = bb.reshape(S5_BLOCKS, gpb, S5_STATE, S5_GROUP)
        return jnp.einsum("jgph,gk->jghkp", bb, eye).reshape(S5_BLOCKS, gpb * S5_GROUP, gpb * S5_STATE)

    def out_map(cc):
        cc = cc.reshape(S5_BLOCKS, gpb, S5_GROUP, S5_STATE)
        return jnp.einsum("jghp,gk->jgpkh", cc, eye).reshape(S5_BLOCKS, gpb * S5_STATE, gpb * S5_GROUP)

    def power(n):
        n = jnp.asarray(n, F32)[..., None, None]
        pmag = jnp.exp(n * (lr * dt))
        shape = n.shape[:-2] + (S5_LANES,)
        return (pmag * jnp.cos(n * (li * dt))).reshape(shape), (pmag * jnp.sin(n * (li * dt))).reshape(shape)

    row = jnp.arange(SUBLANES)
    step = 2 ** jnp.arange(S5_LOG_STEPS)
    s_re, s_im = power(step)
    keep = (row[None, :] >= step[:, None])[..., None]
    m_re = jnp.where(keep, s_re[:, None, :], 0.0)
    m_im = jnp.where(keep, s_im[:, None, :], 0.0)
    p_re, p_im = power(row + 1)
    return (in_map(bbar_re).astype(BF16), in_map(bbar_im).astype(BF16),
            out_map(c_re.astype(F32)).astype(BF16), out_map(c_im.astype(F32)).astype(BF16),
            m_re, m_im, p_re, p_im)


def _s5(proj, tables, d_skip, glu_w, glu_b, batch, seq):
    t = batch * seq
    ts = S5_TILE
    nj = seq // ts
    bre, bim, cre, cim, m_re, m_im, p_re, p_im = tables
    blk_c = MIX_HALF // S5_BLOCKS
    blk_s = S5_LANES // S5_BLOCKS
    const2 = lambda shape: pl.BlockSpec(shape, lambda b, j: (0, 0))
    const3 = lambda shape: pl.BlockSpec(shape, lambda b, j: (0, 0, 0))
    return pl.pallas_call(
        functools.partial(_s5_kernel, tile=ts),
        grid=(batch, nj),
        in_specs=[pl.BlockSpec((ts, MIX_HALF), lambda b, j: (b * nj + j, 0)),
                  const3((S5_BLOCKS, blk_c, blk_s)), const3((S5_BLOCKS, blk_c, blk_s)),
                  const3((S5_BLOCKS, blk_s, blk_c)), const3((S5_BLOCKS, blk_s, blk_c)),
                  const3(m_re.shape), const3(m_im.shape), const2(p_re.shape), const2(p_im.shape),
                  const2((1, MIX_HALF)), const2((MIX_HALF, MIX_HALF)), const2((1, MIX_HALF))],
        out_specs=pl.BlockSpec((ts, MIX_HALF), lambda b, j: (b * nj + j, 0)),
        out_shape=jax.ShapeDtypeStruct((t, MIX_HALF), BF16),
        scratch_shapes=[pltpu.VMEM((ts, S5_LANES), F32), pltpu.VMEM((ts, S5_LANES), F32),
                        pltpu.VMEM((8, S5_LANES), F32), pltpu.VMEM((8, S5_LANES), F32)],
        compiler_params=_params("arbitrary", "arbitrary"),
        name="s5",
    )(proj, bre, bim, cre, cim, m_re, m_im, p_re, p_im, d_skip.reshape(1, MIX_HALF), glu_w.astype(BF16),
      glu_b.reshape(1, MIX_HALF))


def _gla_kernel(q_ref, k_ref, v_ref, r_ref, gl_ref, gw_ref, gb_ref, nw_ref, y_ref,
                st_ref, qd_ref, ki_ref, ke_ref, v16_ref, dec_ref, o_ref, *, tile, chunk):
    L = chunk
    nc = tile // L

    @pl.when(pl.program_id(1) == 0)
    def _():
        st_ref[...] = jnp.zeros_like(st_ref)

    z = _dot(gl_ref[...].astype(BF16), gw_ref[...]) + gb_ref[...]
    bcum = _log_sigmoid(z) * (1.0 / GLA_GATE_TEMP)
    row_in_chunk = lax.broadcasted_iota(jnp.int32, bcum.shape, 0) & (L - 1)
    s = 1
    while s < L:
        bcum = bcum + jnp.where(row_in_chunk >= s, pltpu.roll(bcum, s, 0), 0.0)
        s *= 2
    b3 = bcum.reshape(nc, L, MIX_HALF)
    b_last = b3[:, L - 1:L, :]
    k = k_ref[...]
    qd_ref[...] = (q_ref[...] * (GLA_DK ** -0.5) * jnp.exp(bcum)).astype(BF16)
    ki_ref[...] = (k * jnp.exp(-bcum)).astype(BF16)
    ke_ref[...] = (k.reshape(nc, L, MIX_HALF) * jnp.exp(b_last - b3)).reshape(tile, MIX_HALF).astype(BF16)
    v16_ref[...] = v_ref[...].astype(BF16)
    dec_ref[...] = jnp.exp(b_last)

    ri = lax.broadcasted_iota(jnp.int32, (L, L), 0)
    ci = lax.broadcasted_iota(jnp.int32, (L, L), 1)
    causal = ci <= ri

    def body(c, carry):
        r0 = pl.multiple_of(c * L, L)
        dec = dec_ref[c]
        for h in range(HEADS):
            lo = h * HEAD_DIM
            q_dec = qd_ref[pl.ds(r0, L), lo:lo + HEAD_DIM]
            v = v16_ref[pl.ds(r0, L), lo:lo + HEAD_DIM]
            st = st_ref[h]
            att = jnp.where(causal, _dot_nt(q_dec, ki_ref[pl.ds(r0, L), lo:lo + HEAD_DIM]), 0.0)
            o_ref[pl.ds(r0, L), lo:lo + HEAD_DIM] = (_dot(att.astype(BF16), v)
                                                     + _dot_nt(q_dec, st.astype(BF16)))
            st_ref[h] = dec[:, lo:lo + HEAD_DIM] * st + _dot_tn(v, ke_ref[pl.ds(r0, L), lo:lo + HEAD_DIM])
        return carry

    lax.fori_loop(0, nc, body, 0, unroll=GLA_UNROLL)

    rg = r_ref[...]
    gate = nw_ref[...] * (rg * _sigmoid(rg))
    for h in range(HEADS):
        lo = h * HEAD_DIM
        o = o_ref[:, lo:lo + HEAD_DIM]
        yn = o * lax.rsqrt(jnp.mean(o * o, axis=-1, keepdims=True) + EPS)
        y_ref[:, lo:lo + HEAD_DIM] = (yn * gate[:, lo:lo + HEAD_DIM]).astype(BF16)


def _gla(proj, glow, gate_w, gate_b, norm_w, batch, seq):
    t = batch * seq
    ts = GLA_TILE
    nj = seq // ts
    blk = lambda col: pl.BlockSpec((ts, MIX_HALF), lambda b, j, col=col: (b * nj + j, col))
    const2 = lambda shape: pl.BlockSpec(shape, lambda b, j: (0, 0))
    return pl.pallas_call(
        functools.partial(_gla_kernel, tile=ts, chunk=GLA_CHUNK),
        grid=(batch, nj),
        in_specs=[blk(1), blk(2), blk(3), blk(4),
                  pl.BlockSpec((ts, LANES), lambda b, j: (b * nj + j, 0)),
                  const2((LANES, MIX_HALF)), const2((1, MIX_HALF)), const2((1, MIX_HALF))],
        out_specs=pl.BlockSpec((ts, MIX_HALF), lambda b, j: (b * nj + j, 0)),
        out_shape=jax.ShapeDtypeStruct((t, MIX_HALF), BF16),
        scratch_shapes=[pltpu.VMEM((HEADS, HEAD_DIM, HEAD_DIM), F32),
                        pltpu.VMEM((ts, MIX_HALF), BF16), pltpu.VMEM((ts, MIX_HALF), BF16),
                        pltpu.VMEM((ts, MIX_HALF), BF16), pltpu.VMEM((ts, MIX_HALF), BF16),
                        pltpu.VMEM((ts // GLA_CHUNK, 1, MIX_HALF), F32),
                        pltpu.VMEM((ts, MIX_HALF), F32)],
        compiler_params=_params("arbitrary", "arbitrary"),
        name="gla",
    )(proj, proj, proj, proj, glow, gate_w, gate_b, norm_w.reshape(1, MIX_HALF))


def _pad_heads(w, axis):
    shape = list(w.shape)
    shape[axis:axis + 1] = [HEADS, GLA_DK]
    w = w.reshape(shape)
    pad = [(0, 0)] * w.ndim
    pad[axis + 1] = (0, HEAD_DIM - GLA_DK)
    w = jnp.pad(w, pad)
    shape[axis:axis + 2] = [HEADS * HEAD_DIM]
    return w.reshape(shape)


def _pack_bf16_pairs(z):
    hi = lax.bitcast_convert_type(z[:, :PACKED].astype(BF16).astype(F32), jnp.uint32)
    lo = lax.bitcast_convert_type(z[:, PACKED:].astype(BF16).astype(F32), jnp.uint32)
    word = (hi & jnp.uint32(0xFFFF0000)) | lax.shift_right_logical(lo, jnp.uint32(16))
    return lax.bitcast_convert_type(word, jnp.int32)


def _unpack_bf16_pairs(p):
    word = lax.bitcast_convert_type(p, jnp.uint32)
    hi = lax.bitcast_convert_type(word & jnp.uint32(0xFFFF0000), F32)
    lo = lax.bitcast_convert_type(lax.shift_left(word, jnp.uint32(16)), F32)
    return hi, lo


def _out_kernel(ya_ref, yb_ref, h_ref, w_ref, lw_ref, lb_ref, rw_ref, rb_ref, sg_ref, su_ref, sd_ref,
                o_ref, opk_ref, idx_ref, rank_ref, wk_ref, cnt_ref, base_ref, *, tile):
    mixed = jnp.concatenate([ya_ref[...], yb_ref[...]], axis=1)
    z = ALPHA * h_ref[...] + _dot(mixed, w_ref[...])
    out = _layer_norm(z, lw_ref[...], lb_ref[...])
    out16 = out.astype(BF16)
    shared = _dot((_silu(_dot(out16, sg_ref[...])) * _dot(out16, su_ref[...])).astype(BF16), sd_ref[...])
    o_ref[...] = ALPHA * out + shared
    opk_ref[...] = _pack_bf16_pairs(out)
    _route_tile(out, rw_ref, rb_ref, idx_ref, rank_ref, wk_ref, cnt_ref, base_ref, tile)


def _out_proj_ln_route(ya, yb, h, w_out, ln_w, ln_b, router_w, router_bias, sg, su, sd):
    t = h.shape[0]
    tm = OUT_TILE
    const = lambda shape: pl.BlockSpec(shape, lambda i: (0, 0))
    per_tok = lambda dt: jax.ShapeDtypeStruct((TOP_K, t), dt)
    tok_blk = pl.BlockSpec((TOP_K, tm), lambda i: (0, i))
    return pl.pallas_call(
        functools.partial(_out_kernel, tile=tm),
        grid=(t // tm,),
        in_specs=[pl.BlockSpec((tm, MIX_HALF), lambda i: (i, 0)),
                  pl.BlockSpec((tm, MIX_HALF), lambda i: (i, 0)),
                  pl.BlockSpec((tm, D_MODEL), lambda i: (i, 0)),
                  const((D_MODEL, D_MODEL)), const((1, D_MODEL)), const((1, D_MODEL)),
                  const((N_EXPERTS, D_MODEL)), const((N_EXPERTS, 1)),
                  const((D_MODEL, D_EXPERT)), const((D_MODEL, D_EXPERT)), const((D_EXPERT, D_MODEL))],
        out_specs=[pl.BlockSpec((tm, D_MODEL), lambda i: (i, 0)),
                   pl.BlockSpec((tm, PACKED), lambda i: (i, 0)),
                   tok_blk, tok_blk, pl.BlockSpec((tm, TOP_K), lambda i: (i, 0)), const((N_EXPERTS, LANES))],
        out_shape=[jax.ShapeDtypeStruct((t, D_MODEL), F32), jax.ShapeDtypeStruct((t, PACKED), jnp.int32),
                   per_tok(jnp.int32), per_tok(jnp.int32), jax.ShapeDtypeStruct((t, TOP_K), F32),
                   jax.ShapeDtypeStruct((N_EXPERTS, LANES), F32)],
        scratch_shapes=[pltpu.VMEM((N_EXPERTS, LANES), F32)],
        compiler_params=_params("arbitrary"),
        name="out_proj_ln_route",
    )(ya, yb, h, w_out.astype(BF16), ln_w.reshape(1, D_MODEL), ln_b.reshape(1, D_MODEL),
      router_w.T, router_bias.reshape(N_EXPERTS, 1), sg.astype(BF16), su.astype(BF16), sd.astype(BF16))


def _first_index(hit, idx, big):
    return jnp.min(jnp.where(hit, idx, big), axis=0, keepdims=True)


def _route_tile(h, w_ref, b_ref, idx_ref, rank_ref, wk_ref, cnt_ref, base_ref, tile):
    @pl.when(pl.program_id(0) == 0)
    def _():
        base_ref[...] = jnp.zeros_like(base_ref)

    h_hi, h_mid, _ = _split3(h)
    w_hi, w_mid, _ = _split3(w_ref[...])
    logits = _dot_nt(w_hi, h_hi) + _dot_nt(w_hi, h_mid) + _dot_nt(w_mid, h_hi)
    scores = _sigmoid(logits)
    biased = scores + b_ref[...]

    sub = lax.broadcasted_iota(jnp.int32, (GROUP_SIZE, tile), 0)
    grp_rows = []
    for g in range(N_GROUPS):
        xg = biased[g * GROUP_SIZE:(g + 1) * GROUP_SIZE, :]
        m1 = jnp.max(xg, axis=0, keepdims=True)
        i1 = _first_index(xg == m1, sub, GROUP_SIZE)
        m2 = jnp.max(jnp.where(sub == i1, NEG_INF, xg), axis=0, keepdims=True)
        grp_rows.append(m1 + m2)
    gs = jnp.concatenate(grp_rows, axis=0)
    gsel = jnp.zeros((N_GROUPS, tile), F32)
    for _ in range(TOPK_GROUPS):
        mx = jnp.max(gs, axis=0, keepdims=True)
        hit = sub == _first_index(gs == mx, sub, N_GROUPS)
        gsel = jnp.where(hit, 1.0, gsel)
        gs = jnp.where(hit, NEG_INF, gs)
    emask = jnp.concatenate(
        [jnp.broadcast_to(gsel[g:g + 1, :], (GROUP_SIZE, tile)) for g in range(N_GROUPS)], axis=0)

    eidx = lax.broadcasted_iota(jnp.int32, (N_EXPERTS, tile), 0)
    cand = jnp.where(emask > 0.5, biased, NEG_INF)
    sel = jnp.zeros((N_EXPERTS, tile), F32)
    picks = []
    for _ in range(TOP_K):
        mx = jnp.max(cand, axis=0, keepdims=True)
        first = _first_index(cand == mx, eidx, N_EXPERTS)
        hit = eidx == first
        picks.append(first)
        sel = jnp.where(hit, 1.0, sel)
        cand = jnp.where(hit, NEG_INF, cand)
    picked = jnp.where(sel > 0.5, scores, 0.0)
    wts = picked / jnp.sum(picked, axis=0, keepdims=True) * ROUTED_SCALE

    ri = lax.broadcasted_iota(jnp.int32, (tile, tile), 0)
    ci = lax.broadcasted_iota(jnp.int32, (tile, tile), 1)
    before = (ri < ci).astype(BF16)
    prior = _dot(sel.astype(BF16), before) + base_ref[:, 0:1]
    ranks = [jnp.sum(jnp.where(eidx == p, prior, 0.0), axis=0, keepdims=True) for p in picks]
    wsel = [jnp.sum(jnp.where(eidx == p, wts, 0.0), axis=0, keepdims=True) for p in picks]
    idx_ref[...] = jnp.concatenate(picks, axis=0)
    rank_ref[...] = jnp.concatenate(ranks, axis=0).astype(jnp.int32)
    wk_ref[...] = jnp.concatenate(wsel, axis=0).T
    total = base_ref[...] + jnp.sum(sel, axis=1, keepdims=True)
    base_ref[...] = total
    cnt_ref[...] = total


def _silu(x):
    return x * _sigmoid(x)


def _sc_mesh():
    return plsc.VectorSubcoreMesh(core_axis_name="c", subcore_axis_name="s")


def _sc_worker_id():
    return lax.axis_index("s") * SC_CORES + lax.axis_index("c")


def _dispatch_rows(xpk, pos_chunks, n_rows):
    t = xpk.shape[0]
    ch = pos_chunks.shape[2]
    n_ch = t // SC_WORKERS // ch

    @functools.partial(
        pl.kernel, mesh=_sc_mesh(),
        out_type=jax.ShapeDtypeStruct((n_rows, PACKED), jnp.int32),
        scratch_types=[pltpu.VMEM((TOP_K, ch), jnp.int32),
                       pltpu.VMEM((ch, PACKED), jnp.int32),
                       pltpu.SemaphoreType.DMA],
        name="moe_dispatch",
    )
    def scatter(x_hbm, pos_hbm, out_hbm, idx_v, rows_v, sem):
        wid = _sc_worker_id()

        @pl.loop(0, n_ch)
        def _(c):
            chunk = wid * n_ch + c
            off = pl.multiple_of(chunk * ch, ch)
            pltpu.sync_copy(pos_hbm.at[chunk], idx_v)
            pltpu.sync_copy(x_hbm.at[pl.ds(off, ch)], rows_v)
            copies = [pltpu.async_copy(rows_v, out_hbm.at[idx_v.at[k]], sem) for k in range(TOP_K)]
            for cp in copies:
                cp.wait()

    return scatter(xpk, pos_chunks)


def _gather_rows(table, idx):
    n = idx.shape[0]
    per_w = n // SC_WORKERS
    n_ch = per_w // SC_CHUNK
    assert n_ch % 2 == 0 and n_ch >= 2

    @functools.partial(
        pl.kernel, mesh=_sc_mesh(),
        out_type=jax.ShapeDtypeStruct((n, PACKED), jnp.int32),
        scratch_types=[pltpu.VMEM((n_ch, SC_CHUNK), jnp.int32),
                       pltpu.VMEM((SC_CHUNK, PACKED), jnp.int32), pltpu.VMEM((SC_CHUNK, PACKED), jnp.int32),
                       pltpu.SemaphoreType.DMA, pltpu.SemaphoreType.DMA,
                       pltpu.SemaphoreType.DMA, pltpu.SemaphoreType.DMA],
        name="moe_gather",
    )
    def gather(table_hbm, idx_hbm, out_hbm, idx_v, rows0, rows1, g0, g1, w0, w1):
        wid = _sc_worker_id()
        base = wid * per_w
        rows, g_sem, w_sem = (rows0, rows1), (g0, g1), (w0, w1)
        pltpu.sync_copy(idx_hbm.at[wid], idx_v)

        def fetch(c, b):
            return pltpu.make_async_copy(table_hbm.at[idx_v.at[c]], rows[b], g_sem[b])

        def flush(c, b):
            off = pl.multiple_of(base + c * SC_CHUNK, SC_CHUNK)
            return pltpu.make_async_copy(rows[b], out_hbm.at[pl.ds(off, SC_CHUNK)], w_sem[b])

        fetch(0, 0).start()

        @pl.loop(0, n_ch, step=2)
        def _(c0):
            for b in range(2):
                c = c0 + b
                fetch(c, b).wait()
                flush(c, b).start()

                @pl.when(c + 1 < n_ch)
                def _():
                    @pl.when(c >= 1)
                    def _():
                        flush(c - 1, 1 - b).wait()
                    fetch(c + 1, 1 - b).start()

        flush(n_ch - 2, 0).wait()
        flush(n_ch - 1, 1).wait()

    return gather(table, idx.reshape(SC_WORKERS, n_ch, SC_CHUNK))


def _unpacked_bf16(p):
    hi, lo = _unpack_bf16_pairs(p)
    return jnp.concatenate([hi.astype(BF16), lo.astype(BF16)], axis=1)


def _expert_kernel(be_ref, nu_ref, next_ref, slot_ref, xs_hbm, wg_hbm, wu_hbm, wd_hbm, y_ref,
                   g16_ref, u16_ref, d16_ref, gf_ref, uf_ref, df_ref, xbuf_ref, xsem, wsem, *, layer):
    i = pl.program_id(0)
    n_used = nu_ref[0]
    bm = xbuf_ref.shape[1]

    def fetch(b):
        slot = lax.rem(b, XS_SLOTS)
        rows = pl.ds(pl.multiple_of(b * bm, bm), bm)
        return pltpu.make_async_copy(xs_hbm.at[rows], xbuf_ref.at[slot], xsem.at[slot])

    @pl.when(i == 0)
    def _():
        fetch(0).start()

        @pl.when(n_used > 1)
        def _():
            fetch(1).start()

    @pl.when(i + 2 < n_used)
    def _():
        fetch(i + 2).start()

    e = be_ref[i]
    slot = slot_ref[e]
    f32_bufs = (gf_ref, uf_ref, df_ref)

    def wfetch(expert, dst_slot):
        return [pltpu.make_async_copy(w_hbm.at[layer, expert], buf.at[dst_slot], wsem.at[dst_slot, j])
                for j, (w_hbm, buf) in enumerate(zip((wg_hbm, wu_hbm, wd_hbm), f32_bufs))]

    @pl.when(i == 0)
    def _():
        for cp in wfetch(e, slot):
            cp.start()

    first_block_of_expert = jnp.logical_or(i == 0, e != be_ref[jnp.maximum(i - 1, 0)])

    @pl.when(jnp.logical_and(first_block_of_expert, i < n_used))
    def _():
        for cp in wfetch(e, slot):
            cp.wait()
        g16_ref[slot] = gf_ref[slot].astype(BF16)
        u16_ref[slot] = uf_ref[slot].astype(BF16)
        d16_ref[slot] = df_ref[slot].astype(BF16)
        nxt = next_ref[e]

        @pl.when(nxt >= 0)
        def _():
            for cp in wfetch(nxt, 1 - slot):
                cp.start()

    @pl.when(i < n_used)
    def _():
        fetch(i).wait()
        x = _unpacked_bf16(xbuf_ref[lax.rem(i, XS_SLOTS)])
        hh = _silu(_dot(x, g16_ref[slot])) * _dot(x, u16_ref[slot])
        y_ref[...] = _pack_bf16_pairs(_dot(hh.astype(BF16), d16_ref[slot]))


def _experts(block_e, n_used, next_expert, weight_slot, xs, wg, wu, wd, layer):
    nb = block_e.shape[0]
    bm = MOE_BLOCK
    hbm = pl.BlockSpec(memory_space=pl.ANY)
    two = lambda shape, dt: pltpu.VMEM((2,) + shape, dt)
    grid_spec = pltpu.PrefetchScalarGridSpec(
        num_scalar_prefetch=4,
        grid=(n_used[0],),
        in_specs=[hbm, hbm, hbm, hbm],
        out_specs=pl.BlockSpec((bm, PACKED), lambda i, be, nu, nx, sl: (jnp.minimum(i, nu[0] - 1), 0)),
        scratch_shapes=[two((D_MODEL, D_EXPERT), BF16), two((D_MODEL, D_EXPERT), BF16), two((D_EXPERT, D_MODEL), BF16),
                        two((D_MODEL, D_EXPERT), F32), two((D_MODEL, D_EXPERT), F32), two((D_EXPERT, D_MODEL), F32),
                        pltpu.VMEM((XS_SLOTS, bm, PACKED), jnp.int32), pltpu.SemaphoreType.DMA((XS_SLOTS,)),
                        pltpu.SemaphoreType.DMA((2, 3))],
    )
    return pl.pallas_call(
        functools.partial(_expert_kernel, layer=layer),
        grid_spec=grid_spec,
        out_shape=jax.ShapeDtypeStruct((nb * bm, PACKED), jnp.int32),
        compiler_params=_params("arbitrary"),
        name="moe_experts",
    )(block_e, n_used, next_expert, weight_slot, xs, wg, wu, wd)


def _combine_kernel(g_ref, wk_ref, base_ref, lw_ref, lb_ref, *rest, with_proj):
    if with_proj:
        (wm_ref, ws_ref), (o_ref, proj_ref, small_ref) = rest[:2], rest[-3:]
        next_proj = (wm_ref, ws_ref, proj_ref, small_ref)
    else:
        o_ref, next_proj = rest[-1], None
    acc_hi = base_ref[:, :PACKED]
    acc_lo = base_ref[:, PACKED:]
    wk = wk_ref[...]
    for k in range(TOP_K):
        y_hi, y_lo = _unpack_bf16_pairs(g_ref[k])
        w = wk[:, k:k + 1]
        acc_hi = acc_hi + w * y_hi
        acc_lo = acc_lo + w * y_lo
    out = _layer_norm(jnp.concatenate([acc_hi, acc_lo], axis=1), lw_ref[...], lb_ref[...])
    o_ref[...] = out
    if next_proj is not None:
        wm_ref, ws_ref, proj_ref, small_ref = next_proj
        out16 = out.astype(BF16)
        proj_ref[...] = _dot(out16, wm_ref[...])
        small_ref[...] = _dot(out16, ws_ref[...])


def _combine_ln(g, wk, base, ln_w, ln_b, next_w, part, prev):
    t = base.shape[0]
    tm = COMBINE_TILE if next_w is not None else 2 * COMBINE_TILE
    n_blk = g.shape[1] // tm
    first = part * n_blk
    const = lambda shape: pl.BlockSpec(shape, lambda i: (0, 0))
    rows = lambda width: pl.BlockSpec((tm, width), lambda i: (i + first, 0))
    in_specs = [pl.BlockSpec((TOP_K, tm, PACKED), lambda i: (0, i, 0)), rows(TOP_K), rows(D_MODEL),
                const((1, D_MODEL)), const((1, D_MODEL))]
    args = [g, wk, base, ln_w.reshape(1, D_MODEL), ln_b.reshape(1, D_MODEL)]
    out_specs = [rows(D_MODEL)]
    out_shape = [jax.ShapeDtypeStruct((t, D_MODEL), F32)]
    if next_w is not None:
        w_main, w_small = next_w
        n = w_main.shape[1]
        in_specs += [const((D_MODEL, n)), const((D_MODEL, LANES))]
        args += [w_main, w_small]
        out_specs += [rows(n), rows(LANES)]
        out_shape += [jax.ShapeDtypeStruct((t, n), F32), jax.ShapeDtypeStruct((t, LANES), F32)]
    aliases = {}
    if prev is not None:
        aliases = {len(args) + k: k for k in range(len(prev))}
        in_specs += [pl.BlockSpec(memory_space=pl.ANY)] * len(prev)
        args += list(prev)
    return pl.pallas_call(
        functools.partial(_combine_kernel, with_proj=next_w is not None),
        grid=(n_blk,),
        in_specs=in_specs,
        out_specs=out_specs,
        out_shape=out_shape,
        input_output_aliases=aliases,
        compiler_params=_params("parallel"),
        name="moe_combine_ln",
    )(*args)


def _moe_ln(base, hpk, idx, rank, wk, counts, wg, wu, wd, layer, ln_w, ln_b, next_w):
    t = base.shape[0]
    cnt = counts[:, 0].astype(jnp.int32)
    padded = (cnt + MOE_BLOCK - 1) // MOE_BLOCK * MOE_BLOCK
    pend = jnp.cumsum(padded)
    experts = jnp.arange(N_EXPERTS, dtype=jnp.int32)
    pstart_of_pick = jnp.sum(jnp.where(idx[:, :, None] == experts, pend - padded, 0), axis=-1)
    pos = pstart_of_pick + rank
    nb = -(-(t * TOP_K + N_EXPERTS * (MOE_BLOCK - 1)) // MOE_BLOCK)
    starts = jnp.arange(nb, dtype=jnp.int32) * MOE_BLOCK
    block_e = jnp.minimum(jnp.sum((pend[None, :] <= starts[:, None]).astype(jnp.int32), axis=1), N_EXPERTS - 1)
    n_used = (pend[-1] // MOE_BLOCK).astype(jnp.int32).reshape(1)
    has_rows = cnt > 0
    later = jnp.logical_and(has_rows[None, :], experts[None, :] > experts[:, None])
    next_expert = jnp.min(jnp.where(later, experts[None, :], N_EXPERTS), axis=1)
    next_expert = jnp.where(next_expert == N_EXPERTS, -1, next_expert).astype(jnp.int32)
    weight_slot = ((jnp.cumsum(has_rows) - has_rows) % 2).astype(jnp.int32)
    pos_chunks = pos.reshape(TOP_K, t // SC_DISPATCH_CHUNK, SC_DISPATCH_CHUNK).transpose(1, 0, 2)
    xs = _dispatch_rows(hpk, pos_chunks, nb * MOE_BLOCK)
    ys = _experts(block_e, n_used, next_expert, weight_slot, xs, wg, wu, wd, layer)
    part = t // COMBINE_PARTS
    gathered = [_gather_rows(ys, pos[:, p * part:(p + 1) * part].reshape(-1)).reshape(TOP_K, part, PACKED)
                for p in range(COMBINE_PARTS)]
    outs = None
    for p in range(COMBINE_PARTS):
        outs = _combine_ln(gathered[p], wk, base, ln_w, ln_b, next_w, p, outs)
    return outs


def _pad_cols(w, width=LANES):
    return jnp.pad(w, ((0, 0), (0, width - w.shape[1])))


def _even_proj_weights(w_in):
    a4 = 4 * MIX_HALF
    ng = 2 * HEADS
    w_main = jnp.concatenate([w_in[:, :a4], w_in[:, a4 + ng:]], axis=1).astype(BF16)
    w_gate = _pad_cols(w_in[:, a4:a4 + ng]).astype(BF16)
    return w_main, w_gate


def _even_mixer(proj, gates, batch, seq, gate_b, norm_w, conv_w, conv_b, wa, ba, wx, bx, lam):
    ya = _mlstm(proj, gates, gate_b, norm_w, batch, seq)
    yb = _rglru(proj, conv_w, conv_b, wa, ba, wx, bx, lam, batch, seq)
    return ya, yb


def _odd_proj_weights(w_in):
    c0 = MIX_HALF
    c1 = c0 + HEADS * GLA_DK
    c2 = c1 + HEADS * GLA_DK
    c3 = c2 + MIX_HALF
    c4 = c3 + MIX_HALF
    w_main = jnp.concatenate([w_in[:, :c0], _pad_heads(w_in[:, c0:c1], 1), _pad_heads(w_in[:, c1:c2], 1),
                              w_in[:, c2:c4]], axis=1).astype(BF16)
    w_low = _pad_cols(w_in[:, c4:]).astype(BF16)
    return w_main, w_low


def _odd_mixer(proj, glow, batch, seq, lam_re, lam_im, b_re, b_im, c_re, c_im, d_skip, log_dt,
               glu_w, glu_b, gate_w, gate_b, norm_w):
    tables = _s5_tables(lam_re, lam_im, b_re, b_im, c_re, c_im, log_dt)
    yc = _s5(proj, tables, d_skip, glu_w, glu_b, batch, seq)
    gw = jnp.pad(_pad_heads(gate_w, 1), ((0, LANES - GLA_GATE_RANK), (0, 0))).astype(BF16)
    gb = _pad_heads(gate_b.reshape(1, -1), 1)
    yd = _gla(proj, glow, gw, gb, norm_w, batch, seq)
    return yc, yd


def kernel(x, ln1_w, ln1_b, ln2_w, ln2_b, w_out, w_in_even, mlstm_gate_b, mlstm_norm_w, lru_conv_w, lru_conv_b, lru_wa, lru_ba, lru_wx, lru_bx, lru_lambda, w_in_odd, s5_lam_re, s5_lam_im, s5_b_re, s5_b_im, s5_c_re, s5_c_im, s5_d, s5_log_dt, s5_glu_w, s5_glu_b, gla_gate_w, gla_gate_b, gla_norm_w, router_w, router_bias, exp_w_gate, exp_w_up, exp_w_down, sh_w_gate, sh_w_up, sh_w_down):
    batch, seq, d = x.shape
    proj_w = [_even_proj_weights(w_in_even[layer // 2]) if layer % 2 == 0 else _odd_proj_weights(w_in_odd[layer // 2])
              for layer in range(DEPTH)]
    h = x.reshape(batch * seq, d)
    proj, small = _proj(h, *proj_w[0])
    for layer in range(DEPTH):
        j = layer // 2
        if layer % 2 == 0:
            y1, y2 = _even_mixer(proj, small, batch, seq, mlstm_gate_b[j], mlstm_norm_w[j],
                                 lru_conv_w[j], lru_conv_b[j], lru_wa[j], lru_ba[j], lru_wx[j],
                                 lru_bx[j], lru_lambda[j])
        else:
            y1, y2 = _odd_mixer(proj, small, batch, seq, s5_lam_re[j], s5_lam_im[j], s5_b_re[j],
                                s5_b_im[j], s5_c_re[j], s5_c_im[j], s5_d[j], s5_log_dt[j],
                                s5_glu_w[j], s5_glu_b[j], gla_gate_w[j], gla_gate_b[j], gla_norm_w[j])
        base, hpk, idx, rank, wk, counts = _out_proj_ln_route(
            y1, y2, h, w_out[layer], ln1_w[layer], ln1_b[layer], router_w[layer], router_bias[layer],
            sh_w_gate[layer], sh_w_up[layer], sh_w_down[layer])
        next_w = proj_w[layer + 1] if layer + 1 < DEPTH else None
        res = _moe_ln(base, hpk, idx, rank, wk, counts, exp_w_gate, exp_w_up, exp_w_down, layer,
                      ln2_w[layer], ln2_b[layer], next_w)
        if next_w is None:
            (h,) = res
        else:
            h, proj, small = res
    return h.reshape(batch, seq, d)
```
